```python
import math
import jax, jax.numpy as jnp
from jax import lax
import numpy as np

D_MODEL = 1024
BATCH = 8
SEQ = 8192
DEPTH = 4

N_MIXERS = 3
N_POOL_LAYERS = (DEPTH + 2) // 3
N_SSD_LAYERS = (DEPTH + 1) // 3
N_SB_LAYERS = DEPTH // 3
NORM_EPS = 1e-6

POOL_GROUPS = 4
POOL_WINDOWS = (2, 4, 8, 16)
POOL_GROUP_DIM = D_MODEL // POOL_GROUPS

SSD_D_INNER = 2 * D_MODEL
SSD_HEAD_DIM = 64
SSD_HEADS = SSD_D_INNER // SSD_HEAD_DIM
SSD_GROUPS = 8
SSD_HEADS_PER_GROUP = SSD_HEADS // SSD_GROUPS
SSD_STATE = 128
SSD_CONV = 4
SSD_CHUNK = 256
SSD_GN = SSD_GROUPS * SSD_STATE
SSD_CONV_CH = SSD_D_INNER + 2 * SSD_GN
SSD_IN_DIM = SSD_D_INNER + SSD_CONV_CH + SSD_HEADS
SSD_NORM_GROUP = SSD_D_INNER // SSD_GROUPS

SB_HEADS = 16
SB_HEAD_DIM = D_MODEL // SB_HEADS
SB_BLOCK = 128

FFN_HIDDEN = ((8 * D_MODEL + 3 * 256 - 1) // (3 * 256)) * 256

kernel_name = "hybrid_pool_ssd_stickbreak_block"


def rms_norm(x, gain):
    xf = x.astype(jnp.float32)
    y = xf * lax.rsqrt(jnp.mean(xf * xf, axis=-1, keepdims=True) + NORM_EPS)
    return (y * gain.astype(jnp.float32)).astype(x.dtype)


def pool_mixer(h, w_in, w_group, scale):
    b, s, _ = h.shape
    u = (h @ w_in).reshape(b, s, POOL_GROUPS, POOL_GROUP_DIM).astype(jnp.float32)
    cs = jnp.cumsum(u, axis=1)
    pos = jnp.arange(s)
    outs = []
    for g, w in enumerate(POOL_WINDOWS):
        csg = cs[:, :, g]
        lagged = jnp.pad(csg[:, : s - w], ((0, 0), (w, 0), (0, 0)))
        count = jnp.minimum(pos + 1, w).astype(jnp.float32)[None, :, None]
        outs.append((csg - lagged) / count - u[:, :, g])
    p = jnp.stack(outs, axis=2)
    y = jnp.einsum('bsgc,gcd->bsgd', p, w_group.astype(jnp.float32))
    y = y.reshape(b, s, D_MODEL) * scale.astype(jnp.float32)
    return y.astype(h.dtype)


def ssd_chunked_scan(xdt, da, bmat, cmat):
    b, s = da.shape[:2]
    pad = (-s) % SSD_CHUNK

    def chunks(t):
        t = jnp.pad(t, [(0, 0), (0, pad)] + [(0, 0)] * (t.ndim - 2))
        return jnp.swapaxes(t.reshape(b, -1, SSD_CHUNK, *t.shape[2:]), 0, 1)

    causal = jnp.tril(jnp.ones((SSD_CHUNK, SSD_CHUNK), bool))[None, :, :, None, None]

    def step(state, inp):
        xc, ac, bc, cc = inp
        acum = jnp.cumsum(ac, axis=1)
        diff = acum[:, :, None] - acum[:, None, :]
        decay = jnp.exp(jnp.where(causal, diff, -jnp.inf))
        cb = jnp.einsum('btgn,bsgn->btsg', cc, bc)
        y = jnp.einsum('btsg,btsgh,bsghp->btghp', cb, decay, xc)
        y = y + jnp.einsum('btgn,bghpn,btgh->btghp', cc, state, jnp.exp(acum))
        a_last = acum[:, -1]
        w = jnp.exp(a_last[:, None] - acum)
        state = state * jnp.exp(a_last)[..., None, None] + jnp.einsum('bsgn,bsgh,bsghp->bghpn', bc, w, xc)
        return state, y

    state0 = jnp.zeros((b, SSD_GROUPS, SSD_HEADS_PER_GROUP, SSD_HEAD_DIM, SSD_STATE), jnp.float32)
    _, ys = lax.scan(step, state0, (chunks(xdt), chunks(da), chunks(bmat), chunks(cmat)))
    ys = jnp.swapaxes(ys, 0, 1).reshape(b, -1, *ys.shape[3:])
    return ys[:, :s]


def ssd_mixer(h, w_in, conv_w, conv_b, dt_bias, a_log, d_skip, out_norm, w_out):
    b, s, _ = h.shape
    f32 = jnp.float32
    proj = h @ w_in
    z = proj[..., :SSD_D_INNER]
    xbc = proj[..., SSD_D_INNER:SSD_D_INNER + SSD_CONV_CH]
    dt = proj[..., SSD_D_INNER + SSD_CONV_CH:]
    xbc = lax.conv_general_dilated(
        xbc, conv_w[:, None, :].astype(xbc.dtype), window_strides=(1,),
        padding=[(SSD_CONV - 1, 0)], dimension_numbers=('NWC', 'WIO', 'NWC'),
        feature_group_count=SSD_CONV_CH)
    xbc = jax.nn.silu(xbc.astype(f32) + conv_b.astype(f32))
    xs = xbc[..., :SSD_D_INNER].reshape(b, s, SSD_GROUPS, SSD_HEADS_PER_GROUP, SSD_HEAD_DIM)
    bm = xbc[..., SSD_D_INNER:SSD_D_INNER + SSD_GN].reshape(b, s, SSD_GROUPS, SSD_STATE)
    cm = xbc[..., SSD_D_INNER + SSD_GN:].reshape(b, s, SSD_GROUPS, SSD_STATE)
    dt = jax.nn.softplus(dt.astype(f32) + dt_bias.astype(f32)).reshape(b, s, SSD_GROUPS, SSD_HEADS_PER_GROUP)
    a = -jnp.exp(a_log.astype(f32)).reshape(SSD_GROUPS, SSD_HEADS_PER_GROUP)
    y = ssd_chunked_scan(xs * dt[..., None], dt * a, bm, cm)
    y = y + d_skip.astype(f32).reshape(SSD_GROUPS, SSD_HEADS_PER_GROUP, 1) * xs
    g = (y.reshape(b, s, SSD_D_INNER) * jax.nn.silu(z.astype(f32))).reshape(b, s, SSD_GROUPS, SSD_NORM_GROUP)
    g = g * lax.rsqrt(jnp.mean(g * g, axis=-1, keepdims=True) + NORM_EPS)
    g = g.reshape(b, s, SSD_D_INNER) * out_norm.astype(f32)
    return g.astype(h.dtype) @ w_out


def stick_breaking_mixer(h, w_qkv, q_norm, k_norm, w_out):
    b, s, _ = h.shape
    f32 = jnp.float32
    qkv = (h @ w_qkv).reshape(b, s, 3, SB_HEADS, SB_HEAD_DIM)
    q = rms_norm(qkv[:, :, 0], q_norm).astype(f32).transpose(0, 2, 1, 3)
    k = rms_norm(qkv[:, :, 1], k_norm).astype(f32).transpose(0, 2, 1, 3)
    v = qkv[:, :, 2].astype(f32).transpose(0, 2, 1, 3)
    n_blocks = s // SB_BLOCK
    qb = q.reshape(b, SB_HEADS, n_blocks, SB_BLOCK, SB_HEAD_DIM).transpose(2, 0, 1, 3, 4)
    inv_sqrt_d = 1.0 / math.sqrt(SB_HEAD_DIM)
    key_pos = jnp.arange(s)

    def block(args):
        q_blk, blk = args
        z = jnp.einsum('bhqd,bhkd->bhqk', q_blk, k) * inv_sqrt_d
        t = blk * SB_BLOCK + jnp.arange(SB_BLOCK)
        mask = key_pos[None, :] < t[:, None]
        log_1m = jnp.where(mask, jax.nn.log_sigmoid(-z), 0.0)
        after = lax.cumsum(log_1m, axis=3, reverse=True) - log_1m
        a = jnp.where(mask, jnp.exp(jax.nn.log_sigmoid(z) + after), 0.0)
        return jnp.einsum('bhqk,bhkd->bhqd', a, v)

    o = lax.map(block, (qb, jnp.arange(n_blocks)))
    o = o.transpose(1, 0, 3, 2, 4).reshape(b, s, D_MODEL)
    return o.astype(h.dtype) @ w_out


def swiglu(h, w_gate, w_up, w_down):
    return (jax.nn.silu(h @ w_gate) * (h @ w_up)) @ w_down


def _fwd_setup_inputs(seed: int = 0) -> dict:
    key = jax.random.key(seed)
    ks = jax.random.split(key, 24)
    f32 = jnp.float32

    def nrm(k, shape, scale):
        return jax.random.normal(k, shape, f32) * scale

    def gain(k, shape):
        return 1.0 + 0.02 * jax.random.normal(k, shape, f32)

    dt0 = jnp.exp(jax.random.uniform(ks[10], (N_SSD_LAYERS, SSD_HEADS), f32, math.log(1e-3), math.log(1e-1)))
    return {
        "x": nrm(ks[0], (BATCH, SEQ, D_MODEL), 1.0),
        "mix_norm": gain(ks[1], (DEPTH, D_MODEL)),
        "pool_in": nrm(ks[2], (N_POOL_LAYERS, D_MODEL, D_MODEL), D_MODEL ** -0.5),
        "pool_group": nrm(ks[3], (N_POOL_LAYERS, POOL_GROUPS, POOL_GROUP_DIM, POOL_GROUP_DIM), POOL_GROUP_DIM ** -0.5),
        "pool_scale": gain(ks[4], (N_POOL_LAYERS, D_MODEL)),
        "ssd_in": nrm(ks[5], (N_SSD_LAYERS, D_MODEL, SSD_IN_DIM), D_MODEL ** -0.5),
        "ssd_conv_w": nrm(ks[6], (N_SSD_LAYERS, SSD_CONV, SSD_CONV_CH), SSD_CONV ** -0.5),
        "ssd_conv_b": nrm(ks[7], (N_SSD_LAYERS, SSD_CONV_CH), 0.01),
        "ssd_dt_bias": dt0 + jnp.log(-jnp.expm1(-dt0)),
        "ssd_a_log": jnp.log(jax.random.uniform(ks[8], (N_SSD_LAYERS, SSD_HEADS), f32, 1.0, 16.0)),
        "ssd_d": gain(ks[9], (N_SSD_LAYERS, SSD_HEADS)),
        "ssd_out_norm": gain(ks[11], (N_SSD_LAYERS, SSD_D_INNER)),
        "ssd_out": nrm(ks[12], (N_SSD_LAYERS, SSD_D_INNER, D_MODEL), SSD_D_INNER ** -0.5),
        "sb_qkv": nrm(ks[13], (N_SB_LAYERS, D_MODEL, 3 * D_MODEL), D_MODEL ** -0.5),
        "sb_q_norm": gain(ks[14], (N_SB_LAYERS, SB_HEAD_DIM)),
        "sb_k_norm": gain(ks[15], (N_SB_LAYERS, SB_HEAD_DIM)),
        "sb_out": nrm(ks[16], (N_SB_LAYERS, D_MODEL, D_MODEL), D_MODEL ** -0.5),
        "ffn_norm": gain(ks[17], (DEPTH, D_MODEL)),
        "ffn_gate": nrm(ks[18], (DEPTH, D_MODEL, FFN_HIDDEN), D_MODEL ** -0.5),
        "ffn_up": nrm(ks[19], (DEPTH, D_MODEL, FFN_HIDDEN), D_MODEL ** -0.5),
        "ffn_down": nrm(ks[20], (DEPTH, FFN_HIDDEN, D_MODEL), FFN_HIDDEN ** -0.5),
    }


def _fwd_reference(x, mix_norm, pool_in, pool_group, pool_scale, ssd_in, ssd_conv_w, ssd_conv_b,
              ssd_dt_bias, ssd_a_log, ssd_d, ssd_out_norm, ssd_out, sb_qkv, sb_q_norm, sb_k_norm,
              sb_out, ffn_norm, ffn_gate, ffn_up, ffn_down):
    for i in range(DEPTH):
        kind, j = i % N_MIXERS, i // N_MIXERS
        h = rms_norm(x, mix_norm[i])
        if kind == 0:
            m = pool_mixer(h, pool_in[j], pool_group[j], pool_scale[j])
        elif kind == 1:
            m = ssd_mixer(h, ssd_in[j], ssd_conv_w[j], ssd_conv_b[j], ssd_dt_bias[j], ssd_a_log[j],
                          ssd_d[j], ssd_out_norm[j], ssd_out[j])
        else:
            m = stick_breaking_mixer(h, sb_qkv[j], sb_q_norm[j], sb_k_norm[j], sb_out[j])
        x = x + m
        h = rms_norm(x, ffn_norm[i])
        x = x + swiglu(h, ffn_gate[i], ffn_up[i], ffn_down[i])
    return x


import jax as _jax
import jax.numpy as _jnp

TWIN_FORMAT = 'train_step'
FWD_PARAMS = ['x', 'mix_norm', 'pool_in', 'pool_group', 'pool_scale', 'ssd_in', 'ssd_conv_w', 'ssd_conv_b', 'ssd_dt_bias', 'ssd_a_log', 'ssd_d', 'ssd_out_norm', 'ssd_out', 'sb_qkv', 'sb_q_norm', 'sb_k_norm', 'sb_out', 'ffn_norm', 'ffn_gate', 'ffn_up', 'ffn_down']
TWIN_WEIGHTS = ['mix_norm', 'pool_in', 'pool_group', 'pool_scale', 'ssd_in', 'ssd_conv_w', 'ssd_conv_b', 'ssd_dt_bias', 'ssd_a_log', 'ssd_d', 'ssd_out_norm', 'ssd_out', 'sb_qkv', 'sb_q_norm', 'sb_k_norm', 'sb_out', 'ffn_norm', 'ffn_gate', 'ffn_up', 'ffn_down']
TWIN_DIFF_INPUT = 'x'
TWIN_INPUTS = ['x', 'mix_norm', 'pool_in', 'pool_group', 'pool_scale', 'ssd_in', 'ssd_conv_w', 'ssd_conv_b', 'ssd_dt_bias', 'ssd_a_log', 'ssd_d', 'ssd_out_norm', 'ssd_out', 'sb_qkv', 'sb_q_norm', 'sb_k_norm', 'sb_out', 'ffn_norm', 'ffn_gate', 'ffn_up', 'ffn_down', 'loss_target', 'm_mix_norm', 'm_pool_in', 'm_pool_group', 'm_pool_scale', 'm_ssd_in', 'm_ssd_conv_w', 'm_ssd_conv_b', 'm_ssd_dt_bias', 'm_ssd_a_log', 'm_ssd_d', 'm_ssd_out_norm', 'm_ssd_out', 'm_sb_qkv', 'm_sb_q_norm', 'm_sb_k_norm', 'm_sb_out', 'm_ffn_norm', 'm_ffn_gate', 'm_ffn_up', 'm_ffn_down', 'v_mix_norm', 'v_pool_in', 'v_pool_group', 'v_pool_scale', 'v_ssd_in', 'v_ssd_conv_w', 'v_ssd_conv_b', 'v_ssd_dt_bias', 'v_ssd_a_log', 'v_ssd_d', 'v_ssd_out_norm', 'v_ssd_out', 'v_sb_qkv', 'v_sb_q_norm', 'v_sb_k_norm', 'v_sb_out', 'v_ffn_norm', 'v_ffn_gate', 'v_ffn_up', 'v_ffn_down']
TWIN_OUTPUTS = ['loss', 'grad_x', 'grad_mix_norm', 'grad_pool_in', 'grad_pool_group', 'grad_pool_scale', 'grad_ssd_in', 'grad_ssd_conv_w', 'grad_ssd_conv_b', 'grad_ssd_dt_bias', 'grad_ssd_a_log', 'grad_ssd_d', 'grad_ssd_out_norm', 'grad_ssd_out', 'grad_sb_qkv', 'grad_sb_q_norm', 'grad_sb_k_norm', 'grad_sb_out', 'grad_ffn_norm', 'grad_ffn_gate', 'grad_ffn_up', 'grad_ffn_down', 'delta_mix_norm', 'delta_pool_in', 'delta_pool_group', 'delta_pool_scale', 'delta_ssd_in', 'delta_ssd_conv_w', 'delta_ssd_conv_b', 'delta_ssd_dt_bias', 'delta_ssd_a_log', 'delta_ssd_d', 'delta_ssd_out_norm', 'delta_ssd_out', 'delta_sb_qkv', 'delta_sb_q_norm', 'delta_sb_k_norm', 'delta_sb_out', 'delta_ffn_norm', 'delta_ffn_gate', 'delta_ffn_up', 'delta_ffn_down', 'new_m_mix_norm', 'new_m_pool_in', 'new_m_pool_group', 'new_m_pool_scale', 'new_m_ssd_in', 'new_m_ssd_conv_w', 'new_m_ssd_conv_b', 'new_m_ssd_dt_bias', 'new_m_ssd_a_log', 'new_m_ssd_d', 'new_m_ssd_out_norm', 'new_m_ssd_out', 'new_m_sb_qkv', 'new_m_sb_q_norm', 'new_m_sb_k_norm', 'new_m_sb_out', 'new_m_ffn_norm', 'new_m_ffn_gate', 'new_m_ffn_up', 'new_m_ffn_down', 'new_v_mix_norm', 'new_v_pool_in', 'new_v_pool_group', 'new_v_pool_scale', 'new_v_ssd_in', 'new_v_ssd_conv_w', 'new_v_ssd_conv_b', 'new_v_ssd_dt_bias', 'new_v_ssd_a_log', 'new_v_ssd_d', 'new_v_ssd_out_norm', 'new_v_ssd_out', 'new_v_sb_qkv', 'new_v_sb_q_norm', 'new_v_sb_k_norm', 'new_v_sb_out', 'new_v_ffn_norm', 'new_v_ffn_gate', 'new_v_ffn_up', 'new_v_ffn_down']
TWIN_LEAF_KINDS = {'loss': 'loss', 'grad_x': 'grad_x', 'grad_mix_norm': 'grad_w', 'grad_pool_in': 'grad_w', 'grad_pool_group': 'grad_w', 'grad_pool_scale': 'grad_w', 'grad_ssd_in': 'grad_w', 'grad_ssd_conv_w': 'grad_w', 'grad_ssd_conv_b': 'grad_w', 'grad_ssd_dt_bias': 'grad_w', 'grad_ssd_a_log': 'grad_w', 'grad_ssd_d': 'grad_w', 'grad_ssd_out_norm': 'grad_w', 'grad_ssd_out': 'grad_w', 'grad_sb_qkv': 'grad_w', 'grad_sb_q_norm': 'grad_w', 'grad_sb_k_norm': 'grad_w', 'grad_sb_out': 'grad_w', 'grad_ffn_norm': 'grad_w', 'grad_ffn_gate': 'grad_w', 'grad_ffn_up': 'grad_w', 'grad_ffn_down': 'grad_w', 'delta_mix_norm': 'delta_w', 'delta_pool_in': 'delta_w', 'delta_pool_group': 'delta_w', 'delta_pool_scale': 'delta_w', 'delta_ssd_in': 'delta_w', 'delta_ssd_conv_w': 'delta_w', 'delta_ssd_conv_b': 'delta_w', 'delta_ssd_dt_bias': 'delta_w', 'delta_ssd_a_log': 'delta_w', 'delta_ssd_d': 'delta_w', 'delta_ssd_out_norm': 'delta_w', 'delta_ssd_out': 'delta_w', 'delta_sb_qkv': 'delta_w', 'delta_sb_q_norm': 'delta_w', 'delta_sb_k_norm': 'delta_w', 'delta_sb_out': 'delta_w', 'delta_ffn_norm': 'delta_w', 'delta_ffn_gate': 'delta_w', 'delta_ffn_up': 'delta_w', 'delta_ffn_down': 'delta_w', 'new_m_mix_norm': 'new_m', 'new_m_pool_in': 'new_m', 'new_m_pool_group': 'new_m', 'new_m_pool_scale': 'new_m', 'new_m_ssd_in': 'new_m', 'new_m_ssd_conv_w': 'new_m', 'new_m_ssd_conv_b': 'new_m', 'new_m_ssd_dt_bias': 'new_m', 'new_m_ssd_a_log': 'new_m', 'new_m_ssd_d': 'new_m', 'new_m_ssd_out_norm': 'new_m', 'new_m_ssd_out': 'new_m', 'new_m_sb_qkv': 'new_m', 'new_m_sb_q_norm': 'new_m', 'new_m_sb_k_norm': 'new_m', 'new_m_sb_out': 'new_m', 'new_m_ffn_norm': 'new_m', 'new_m_ffn_gate': 'new_m', 'new_m_ffn_up': 'new_m', 'new_m_ffn_down': 'new_m', 'new_v_mix_norm': 'new_v', 'new_v_pool_in': 'new_v', 'new_v_pool_group': 'new_v', 'new_v_pool_scale': 'new_v', 'new_v_ssd_in': 'new_v', 'new_v_ssd_conv_w': 'new_v', 'new_v_ssd_conv_b': 'new_v', 'new_v_ssd_dt_bias': 'new_v', 'new_v_ssd_a_log': 'new_v', 'new_v_ssd_d': 'new_v', 'new_v_ssd_out_norm': 'new_v', 'new_v_ssd_out': 'new_v', 'new_v_sb_qkv': 'new_v', 'new_v_sb_q_norm': 'new_v', 'new_v_sb_k_norm': 'new_v', 'new_v_sb_out': 'new_v', 'new_v_ffn_norm': 'new_v', 'new_v_ffn_gate': 'new_v', 'new_v_ffn_up': 'new_v', 'new_v_ffn_down': 'new_v'}


def _forward(args):
    return _fwd_reference(*[args[k] for k in FWD_PARAMS])


def _output_shape():
    def fwd():
        inp = _fwd_setup_inputs(0)
        return _fwd_reference(*[inp[k] for k in FWD_PARAMS])
    out = _jax.eval_shape(fwd)
    return out.shape, out.dtype

N_MICROBATCH = 1
ADAM_LR = 0.001
ADAM_B1 = 0.9
ADAM_B2 = 0.999
ADAM_EPS = 1e-08
ADAM_WD = 0.01
ADAM_STEP = 10
PER_EXAMPLE_BATCH_AXIS = {'x': 0, 'loss_target': 0}
SHARED_INPUTS = []
_WEIGHT_DTYPES = {'mix_norm': _jnp.float32, 'pool_in': _jnp.float32, 'pool_group': _jnp.float32, 'pool_scale': _jnp.float32, 'ssd_in': _jnp.float32, 'ssd_conv_w': _jnp.float32, 'ssd_conv_b': _jnp.float32, 'ssd_dt_bias': _jnp.float32, 'ssd_a_log': _jnp.float32, 'ssd_d': _jnp.float32, 'ssd_out_norm': _jnp.float32, 'ssd_out': _jnp.float32, 'sb_qkv': _jnp.float32, 'sb_q_norm': _jnp.float32, 'sb_k_norm': _jnp.float32, 'sb_out': _jnp.float32, 'ffn_norm': _jnp.float32, 'ffn_gate': _jnp.float32, 'ffn_up': _jnp.float32, 'ffn_down': _jnp.float32}
MOMENT_SCALE = {'mix_norm': 3.662380e+01, 'pool_in': 3.029657e+00, 'pool_group': 3.797522e+00, 'pool_scale': 5.011240e+01, 'ssd_in': 8.261412e-01, 'ssd_conv_w': 1.300574e+00, 'ssd_conv_b': 4.870769e+00, 'ssd_dt_bias': 1.760733e+00, 'ssd_a_log': 9.744072e+00, 'ssd_d': 1.135943e+01, 'ssd_out_norm': 4.185266e+01, 'ssd_out': 4.121146e+00, 'sb_qkv': 1.023777e+00, 'sb_q_norm': 6.333961e+01, 'sb_k_norm': 6.347713e+01, 'sb_out': 1.609441e+00, 'ffn_norm': 4.969401e+01, 'ffn_gate': 6.127533e-01, 'ffn_up': 6.226376e-01, 'ffn_down': 1.013272e+00}


def _to_microbatches(a, axis):
    t = _jnp.moveaxis(a, axis, 0)
    t = t.reshape((N_MICROBATCH, t.shape[0] // N_MICROBATCH) + t.shape[1:])
    return _jnp.moveaxis(t, 1, axis + 1)


def setup_inputs(seed: int = 0) -> dict:
    inp = _fwd_setup_inputs(seed)
    key = _jax.random.fold_in(_jax.random.key(seed), 7919)
    shape, _ = _output_shape()
    out = dict(inp)
    out["loss_target"] = _jax.random.normal(_jax.random.fold_in(key, 0), shape, _jnp.float32)
    for i, name in enumerate(TWIN_WEIGHTS):
        w = inp[name].astype(_jnp.float32)
        if MOMENT_SCALE is None:
            s = _jnp.sqrt(_jnp.mean(_jnp.square(w)) + 1e-30)
        else:
            s = MOMENT_SCALE[name]
        km, kv = _jax.random.split(_jax.random.fold_in(key, i + 1))
        out[name] = w
        out["m_" + name] = s * _jax.random.normal(km, w.shape, _jnp.float32)
        out["v_" + name] = (s * s) * _jax.random.uniform(kv, w.shape, _jnp.float32, 0.5, 1.5)
    if N_MICROBATCH > 1:
        for name, axis in PER_EXAMPLE_BATCH_AXIS.items():
            out[name] = _to_microbatches(out[name], axis)
    return {'x': out['x'], 'mix_norm': out['mix_norm'], 'pool_in': out['pool_in'], 'pool_group': out['pool_group'], 'pool_scale': out['pool_scale'], 'ssd_in': out['ssd_in'], 'ssd_conv_w': out['ssd_conv_w'], 'ssd_conv_b': out['ssd_conv_b'], 'ssd_dt_bias': out['ssd_dt_bias'], 'ssd_a_log': out['ssd_a_log'], 'ssd_d': out['ssd_d'], 'ssd_out_norm': out['ssd_out_norm'], 'ssd_out': out['ssd_out'], 'sb_qkv': out['sb_qkv'], 'sb_q_norm': out['sb_q_norm'], 'sb_k_norm': out['sb_k_norm'], 'sb_out': out['sb_out'], 'ffn_norm': out['ffn_norm'], 'ffn_gate': out['ffn_gate'], 'ffn_up': out['ffn_up'], 'ffn_down': out['ffn_down'], 'loss_target': out['loss_target'], 'm_mix_norm': out['m_mix_norm'], 'm_pool_in': out['m_pool_in'], 'm_pool_group': out['m_pool_group'], 'm_pool_scale': out['m_pool_scale'], 'm_ssd_in': out['m_ssd_in'], 'm_ssd_conv_w': out['m_ssd_conv_w'], 'm_ssd_conv_b': out['m_ssd_conv_b'], 'm_ssd_dt_bias': out['m_ssd_dt_bias'], 'm_ssd_a_log': out['m_ssd_a_log'], 'm_ssd_d': out['m_ssd_d'], 'm_ssd_out_norm': out['m_ssd_out_norm'], 'm_ssd_out': out['m_ssd_out'], 'm_sb_qkv': out['m_sb_qkv'], 'm_sb_q_norm': out['m_sb_q_norm'], 'm_sb_k_norm': out['m_sb_k_norm'], 'm_sb_out': out['m_sb_out'], 'm_ffn_norm': out['m_ffn_norm'], 'm_ffn_gate': out['m_ffn_gate'], 'm_ffn_up': out['m_ffn_up'], 'm_ffn_down': out['m_ffn_down'], 'v_mix_norm': out['v_mix_norm'], 'v_pool_in': out['v_pool_in'], 'v_pool_group': out['v_pool_group'], 'v_pool_scale': out['v_pool_scale'], 'v_ssd_in': out['v_ssd_in'], 'v_ssd_conv_w': out['v_ssd_conv_w'], 'v_ssd_conv_b': out['v_ssd_conv_b'], 'v_ssd_dt_bias': out['v_ssd_dt_bias'], 'v_ssd_a_log': out['v_ssd_a_log'], 'v_ssd_d': out['v_ssd_d'], 'v_ssd_out_norm': out['v_ssd_out_norm'], 'v_ssd_out': out['v_ssd_out'], 'v_sb_qkv': out['v_sb_qkv'], 'v_sb_q_norm': out['v_sb_q_norm'], 'v_sb_k_norm': out['v_sb_k_norm'], 'v_sb_out': out['v_sb_out'], 'v_ffn_norm': out['v_ffn_norm'], 'v_ffn_gate': out['v_ffn_gate'], 'v_ffn_up': out['v_ffn_up'], 'v_ffn_down': out['v_ffn_down']}


def _loss(weights, diff, rest, loss_target):
    with _jax.named_scope("forward"):
        args = {**rest, TWIN_DIFF_INPUT: diff, **{k: w.astype(_WEIGHT_DTYPES[k]) for k, w in weights.items()}}
        y = _forward(args)
    with _jax.named_scope("loss_head"):
        err = _jnp.square(y.astype(_jnp.float32) - loss_target)
        return 0.5 * _jnp.sum(_jnp.mean(err, axis=-1)) if err.ndim else 0.5 * err


def _adamw(w, g, m, v):
    m = ADAM_B1 * m + (1.0 - ADAM_B1) * g
    v = ADAM_B2 * v + (1.0 - ADAM_B2) * _jnp.square(g)
    m_hat = m / (1.0 - ADAM_B1 ** ADAM_STEP)
    v_hat = v / (1.0 - ADAM_B2 ** ADAM_STEP)
    delta = -ADAM_LR * (m_hat / (_jnp.sqrt(v_hat) + ADAM_EPS) + ADAM_WD * w)
    return delta, m, v


def reference(x, mix_norm, pool_in, pool_group, pool_scale, ssd_in, ssd_conv_w, ssd_conv_b, ssd_dt_bias, ssd_a_log, ssd_d, ssd_out_norm, ssd_out, sb_qkv, sb_q_norm, sb_k_norm, sb_out, ffn_norm, ffn_gate, ffn_up, ffn_down, loss_target, m_mix_norm, m_pool_in, m_pool_group, m_pool_scale, m_ssd_in, m_ssd_conv_w, m_ssd_conv_b, m_ssd_dt_bias, m_ssd_a_log, m_ssd_d, m_ssd_out_norm, m_ssd_out, m_sb_qkv, m_sb_q_norm, m_sb_k_norm, m_sb_out, m_ffn_norm, m_ffn_gate, m_ffn_up, m_ffn_down, v_mix_norm, v_pool_in, v_pool_group, v_pool_scale, v_ssd_in, v_ssd_conv_w, v_ssd_conv_b, v_ssd_dt_bias, v_ssd_a_log, v_ssd_d, v_ssd_out_norm, v_ssd_out, v_sb_qkv, v_sb_q_norm, v_sb_k_norm, v_sb_out, v_ffn_norm, v_ffn_gate, v_ffn_up, v_ffn_down):
    given = dict(x=x, mix_norm=mix_norm, pool_in=pool_in, pool_group=pool_group, pool_scale=pool_scale, ssd_in=ssd_in, ssd_conv_w=ssd_conv_w, ssd_conv_b=ssd_conv_b, ssd_dt_bias=ssd_dt_bias, ssd_a_log=ssd_a_log, ssd_d=ssd_d, ssd_out_norm=ssd_out_norm, ssd_out=ssd_out, sb_qkv=sb_qkv, sb_q_norm=sb_q_norm, sb_k_norm=sb_k_norm, sb_out=sb_out, ffn_norm=ffn_norm, ffn_gate=ffn_gate, ffn_up=ffn_up, ffn_down=ffn_down, loss_target=loss_target, m_mix_norm=m_mix_norm, m_pool_in=m_pool_in, m_pool_group=m_pool_group, m_pool_scale=m_pool_scale, m_ssd_in=m_ssd_in, m_ssd_conv_w=m_ssd_conv_w, m_ssd_conv_b=m_ssd_conv_b, m_ssd_dt_bias=m_ssd_dt_bias, m_ssd_a_log=m_ssd_a_log, m_ssd_d=m_ssd_d, m_ssd_out_norm=m_ssd_out_norm, m_ssd_out=m_ssd_out, m_sb_qkv=m_sb_qkv, m_sb_q_norm=m_sb_q_norm, m_sb_k_norm=m_sb_k_norm, m_sb_out=m_sb_out, m_ffn_norm=m_ffn_norm, m_ffn_gate=m_ffn_gate, m_ffn_up=m_ffn_up, m_ffn_down=m_ffn_down, v_mix_norm=v_mix_norm, v_pool_in=v_pool_in, v_pool_group=v_pool_group, v_pool_scale=v_pool_scale, v_ssd_in=v_ssd_in, v_ssd_conv_w=v_ssd_conv_w, v_ssd_conv_b=v_ssd_conv_b, v_ssd_dt_bias=v_ssd_dt_bias, v_ssd_a_log=v_ssd_a_log, v_ssd_d=v_ssd_d, v_ssd_out_norm=v_ssd_out_norm, v_ssd_out=v_ssd_out, v_sb_qkv=v_sb_qkv, v_sb_q_norm=v_sb_q_norm, v_sb_k_norm=v_sb_k_norm, v_sb_out=v_sb_out, v_ffn_norm=v_ffn_norm, v_ffn_gate=v_ffn_gate, v_ffn_up=v_ffn_up, v_ffn_down=v_ffn_down)
    weights = {n: given[n] for n in TWIN_WEIGHTS}
    shared = {n: given[n] for n in SHARED_INPUTS}
    per_example = {n: given[n] for n in ['x']}
    grad_fn = _jax.value_and_grad(_loss, argnums=(0, 1))

    def one_microbatch(ex, loss_target):
        ex = dict(ex)
        diff = ex.pop(TWIN_DIFF_INPUT)
        return grad_fn(weights, diff, {**shared, **ex}, loss_target)

    if N_MICROBATCH == 1:
        loss, (grad_w, grad_x) = one_microbatch(per_example, given["loss_target"])
    else:
        def body(carry, xs):
            loss_sum, grad_sum = carry
            l_k, (gw_k, gx_k) = one_microbatch(xs[0], xs[1])
            with _jax.named_scope("update"):
                return (loss_sum + l_k, _jax.tree.map(_jnp.add, grad_sum, gw_k)), gx_k

        init = (_jnp.zeros((), _jnp.float32), _jax.tree.map(_jnp.zeros_like, weights))
        (loss, grad_w), grad_x = _jax.lax.scan(body, init, (per_example, given["loss_target"]))
    with _jax.named_scope("update"):
        delta_w, new_m, new_v = {}, {}, {}
        for n in TWIN_WEIGHTS:
            delta_w[n], new_m[n], new_v[n] = _adamw(weights[n], grad_w[n], given["m_" + n], given["v_" + n])
    return (loss, grad_x, *[grad_w[n] for n in TWIN_WEIGHTS], *[delta_w[n] for n in TWIN_WEIGHTS],
            *[new_m[n] for n in TWIN_WEIGHTS], *[new_v[n] for n in TWIN_WEIGHTS])
```

```python
import math

import jax
import jax.numpy as jnp
from jax import lax
from jax.experimental import pallas as pl
from jax.experimental.pallas import tpu as pltpu

F32 = jnp.float32
BF16 = jnp.bfloat16

N_DEV = 8
NORM_EPS = 1e-6
V7X_VMEM_LIMIT_BYTES = 48 * 1024 * 1024
LANES = 128
SUBLANES = 8

POOL_WINDOWS = (2, 4, 8, 16)
SSD_CHUNK = 256
SSD_HEAD_DIM = 64
SSD_STATE = 128
SSD_HEADS_PER_GROUP = 4
SSD_CONV = 4
SB_HEAD_DIM = 64
SB_BLOCK = 128

ADAM_LR = 0.001
ADAM_B1 = 0.9
ADAM_B2 = 0.999
ADAM_EPS = 1e-08
ADAM_WD = 0.01
ADAM_STEP = 10


def _params(*sem):
    return pltpu.CompilerParams(dimension_semantics=sem, vmem_limit_bytes=V7X_VMEM_LIMIT_BYTES)


def _tile(n, cap, mult):
    best = None
    for t in range(mult, min(n, cap) + 1, mult):
        if n % t == 0:
            best = t
    return best or n


def _load_slabs(ref, slabs):
    if not slabs:
        return ref[...]
    return jnp.concatenate([ref[p] for p in range(ref.shape[0])], axis=1)


def _matmul(a, b, mode, *, name, out_dtype=F32, resid=None, a_slabs=False, out_slabs=False,
            tm_cap=512, tn_cap=512, tk_cap=2048):
    if a_slabs:
        m, k = a.shape[1], a.shape[0] * LANES
    else:
        m, k = a.shape
    n = b.shape[1] if mode == "nn" else b.shape[0]
    assert (b.shape[0] if mode == "nn" else b.shape[1]) == k
    tm, tn, tk = _tile(m, tm_cap, SUBLANES), _tile(n, tn_cap, LANES), _tile(k, tk_cap, LANES)
    nk = k // tk
    dn = (((1,), (0,)), ((), ())) if mode == "nn" else (((1,), (1,)), ((), ()))
    has_resid = resid is not None

    def body(*refs):
        if has_resid:
            a_ref, b_ref, r_ref, o_ref, acc = refs
        else:
            a_ref, b_ref, o_ref, acc = refs
        kk = pl.program_id(2)

        @pl.when(kk == 0)
        def _():
            acc[...] = jnp.zeros_like(acc)

        acc[...] += lax.dot_general(_load_slabs(a_ref, a_slabs).astype(BF16), b_ref[...].astype(BF16), dn,
                                    preferred_element_type=F32)

        @pl.when(kk == nk - 1)
        def _():
            r = acc[...]
            if has_resid:
                r = r + r_ref[...]
            if out_slabs:
                for p in range(tn // LANES):
                    o_ref[p] = r[:, p * LANES:(p + 1) * LANES].astype(out_dtype)
            else:
                o_ref[...] = r.astype(out_dtype)

    b_spec = (pl.BlockSpec((tk, tn), lambda i, j, kk: (kk, j)) if mode == "nn"
              else pl.BlockSpec((tn, tk), lambda i, j, kk: (j, kk)))
    a_spec = (pl.BlockSpec((tk // LANES, tm, LANES), lambda i, j, kk: (kk, i, 0)) if a_slabs
              else pl.BlockSpec((tm, tk), lambda i, j, kk: (i, kk)))
    in_specs = [a_spec, b_spec]
    args = [a, b]
    if has_resid:
        in_specs.append(pl.BlockSpec((tm, tn), lambda i, j, kk: (i, j)))
        args.append(resid)
    if out_slabs:
        out_spec = pl.BlockSpec((tn // LANES, tm, LANES), lambda i, j, kk: (j, i, 0))
        out_shape = jax.ShapeDtypeStruct((n // LANES, m, LANES), out_dtype)
    else:
        out_spec = pl.BlockSpec((tm, tn), lambda i, j, kk: (i, j))
        out_shape = jax.ShapeDtypeStruct((m, n), out_dtype)
    return pl.pallas_call(
        body, name=name, grid=(m // tm, n // tn, nk),
        in_specs=in_specs, out_specs=out_spec, out_shape=out_shape,
        scratch_shapes=[pltpu.VMEM((tm, tn), F32)],
        compiler_params=_params("parallel", "parallel", "arbitrary"),
    )(*args)


def _matmul_tn(a, b, *, name, a_slabs=False, ta_cap=512, tb_cap=512, tr_cap=1024):
    if a_slabs:
        r, ka = a.shape[1], a.shape[0] * LANES
    else:
        r, ka = a.shape
    nb = b.shape[1]
    assert b.shape[0] == r
    ta, tb, tr = _tile(ka, ta_cap, LANES), _tile(nb, tb_cap, LANES), _tile(r, tr_cap, SUBLANES)

    def body(a_ref, b_ref, o_ref):
        @pl.when(pl.program_id(2) == 0)
        def _():
            o_ref[...] = jnp.zeros_like(o_ref)

        o_ref[...] += lax.dot_general(_load_slabs(a_ref, a_slabs).astype(BF16), b_ref[...].astype(BF16),
                                      (((0,), (0,)), ((), ())), preferred_element_type=F32)

    a_spec = (pl.BlockSpec((ta // LANES, tr, LANES), lambda i, j, kk: (i, kk, 0)) if a_slabs
              else pl.BlockSpec((tr, ta), lambda i, j, kk: (kk, i)))
    return pl.pallas_call(
        body, name=name, grid=(ka // ta, nb // tb, r // tr),
        in_specs=[a_spec, pl.BlockSpec((tr, tb), lambda i, j, kk: (kk, j))],
        out_specs=pl.BlockSpec((ta, tb), lambda i, j, kk: (i, j)),
        out_shape=jax.ShapeDtypeStruct((ka, nb), F32),
        compiler_params=_params("parallel", "parallel", "arbitrary"),
    )(a, b)


def _rms_fwd(x, gain, *, name):
    t, d = x.shape
    tm = _tile(t, 512, SUBLANES)

    def body(x_ref, g_ref, o_ref):
        xv = x_ref[...]
        r = lax.rsqrt(jnp.mean(xv * xv, axis=-1, keepdims=True) + NORM_EPS)
        o_ref[...] = (xv * r * g_ref[...]).astype(BF16)

    return pl.pallas_call(
        body, name=name, grid=(t // tm,),
        in_specs=[pl.BlockSpec((tm, d), lambda i: (i, 0)), pl.BlockSpec((1, d), lambda i: (0, 0))],
        out_specs=pl.BlockSpec((tm, d), lambda i: (i, 0)),
        out_shape=jax.ShapeDtypeStruct((t, d), BF16),
        compiler_params=_params("parallel"),
    )(x, gain)


def _rms_bwd(x, gain, dh, dres, *, name):
    t, d = x.shape
    tm = _tile(t, 512, SUBLANES)

    def body(x_ref, g_ref, dh_ref, dres_ref, dx_ref, dg_ref):
        @pl.when(pl.program_id(0) == 0)
        def _():
            dg_ref[...] = jnp.zeros_like(dg_ref)

        xv = x_ref[...]
        r = lax.rsqrt(jnp.mean(xv * xv, axis=-1, keepdims=True) + NORM_EPS)
        xhat = xv * r
        dhv = dh_ref[...]
        u = dhv * g_ref[...]
        dx_ref[...] = dres_ref[...] + r * (u - xhat * jnp.mean(u * xhat, axis=-1, keepdims=True))
        dg_ref[...] += jnp.sum(dhv * xhat, axis=0, keepdims=True)

    return pl.pallas_call(
        body, name=name, grid=(t // tm,),
        in_specs=[pl.BlockSpec((tm, d), lambda i: (i, 0)), pl.BlockSpec((1, d), lambda i: (0, 0)),
                  pl.BlockSpec((tm, d), lambda i: (i, 0)), pl.BlockSpec((tm, d), lambda i: (i, 0))],
        out_specs=[pl.BlockSpec((tm, d), lambda i: (i, 0)), pl.BlockSpec((1, d), lambda i: (0, 0))],
        out_shape=[jax.ShapeDtypeStruct((t, d), F32), jax.ShapeDtypeStruct((1, d), F32)],
        compiler_params=_params("arbitrary"),
    )(x, gain, dh, dres)


def _loss_head(y, target, *, name):
    t, d = y.shape
    tm = _tile(t, 512, SUBLANES)

    def body(y_ref, t_ref, dy_ref, l_ref):
        @pl.when(pl.program_id(0) == 0)
        def _():
            l_ref[...] = jnp.zeros_like(l_ref)

        e = y_ref[...] - t_ref[...]
        dy_ref[...] = e * (1.0 / d)
        l_ref[...] += jnp.sum(e * e, axis=0, keepdims=True) * (0.5 / d)

    return pl.pallas_call(
        body, name=name, grid=(t // tm,),
        in_specs=[pl.BlockSpec((tm, d), lambda i: (i, 0)), pl.BlockSpec((tm, d), lambda i: (i, 0))],
        out_specs=[pl.BlockSpec((tm, d), lambda i: (i, 0)), pl.BlockSpec((1, d), lambda i: (0, 0))],
        out_shape=[jax.ShapeDtypeStruct((t, d), F32), jax.ShapeDtypeStruct((1, d), F32)],
        compiler_params=_params("arbitrary"),
    )(y, target)


def _sigmoid(v):
    return 1.0 / (1.0 + jnp.exp(-v))


def _swiglu_fwd(ab, *, name):
    t, f2 = ab.shape
    f = f2 // 2
    tm, tn = _tile(t, 256, SUBLANES), _tile(f, 1536, LANES)
    nb = f // tn

    def body(a_ref, b_ref, o_ref):
        av = a_ref[...]
        o_ref[...] = (av * _sigmoid(av) * b_ref[...]).astype(BF16)

    return pl.pallas_call(
        body, name=name, grid=(t // tm, nb),
        in_specs=[pl.BlockSpec((tm, tn), lambda i, j: (i, j)), pl.BlockSpec((tm, tn), lambda i, j: (i, j + nb))],
        out_specs=pl.BlockSpec((tm, tn), lambda i, j: (i, j)),
        out_shape=jax.ShapeDtypeStruct((t, f), BF16),
        compiler_params=_params("parallel", "parallel"),
    )(ab, ab)


def _swiglu_bwd(ds, ab, *, name):
    t, f2 = ab.shape
    f = f2 // 2
    tm, tn = _tile(t, 256, SUBLANES), _tile(f, 1536, LANES)
    nb = f // tn

    def body(ds_ref, a_ref, b_ref, o_ref):
        av = a_ref[...]
        sg = _sigmoid(av)
        dsv = ds_ref[...]

        @pl.when(pl.program_id(2) == 0)
        def _():
            o_ref[...] = (dsv * b_ref[...] * (sg * (1.0 + av * (1.0 - sg)))).astype(BF16)

        @pl.when(pl.program_id(2) == 1)
        def _():
            o_ref[...] = (dsv * av * sg).astype(BF16)

    return pl.pallas_call(
        body, name=name, grid=(t // tm, nb, 2),
        in_specs=[pl.BlockSpec((tm, tn), lambda i, j, p: (i, j)), pl.BlockSpec((tm, tn), lambda i, j, p: (i, j)),
                  pl.BlockSpec((tm, tn), lambda i, j, p: (i, j + nb))],
        out_specs=pl.BlockSpec((tm, tn), lambda i, j, p: (i, j + p * nb)),
        out_shape=jax.ShapeDtypeStruct((t, f2), BF16),
        compiler_params=_params("parallel", "parallel", "arbitrary"),
    )(ds, ab, ab)


def _ffn_fwd(x, gain, w_gu_t, w_down, tag):
    h = _rms_fwd(x, gain, name=f"ffn_norm_{tag}")
    ab = _matmul(h, w_gu_t, "nt", name=f"ffn_gu_{tag}")
    s = _swiglu_fwd(ab, name=f"ffn_act_{tag}")
    x_new = _matmul(s, w_down, "nn", resid=x, name=f"ffn_down_{tag}")
    return x_new, (x, h, ab, s)


def _ffn_bwd(dx, saved, gain, w_gu_t, w_down, tag):
    x, h, ab, s = saved
    ds = _matmul(dx, w_down, "nt", name=f"ffn_dact_{tag}")
    dw_down = _matmul_tn(s, dx, name=f"ffn_dwdown_{tag}")
    dab = _swiglu_bwd(ds, ab, name=f"ffn_dgu_{tag}")
    dw_gu_t = _matmul_tn(dab, h, name=f"ffn_dwgu_{tag}")
    dh = _matmul(dab, w_gu_t, "nn", name=f"ffn_dh_{tag}")
    dx_in, dgain = _rms_bwd(x, gain, dh, dx, name=f"ffn_dnorm_{tag}")
    return dx_in, dw_gu_t, dw_down, dgain


POOL_HALO = 16


def _shift_rows(v, k):
    n = v.shape[0]
    return pltpu.roll(v, k % n, 0)


def _window_sum(v, w, direction):
    k = 1
    while k < w:
        v = v + _shift_rows(v, direction * k)
        k *= 2
    return v


def _pool_fwd(u, x, w_group, scale, *, name):
    t, d = u.shape
    ng, dg = w_group.shape[0], w_group.shape[1]
    tm = _tile(t, 512, POOL_HALO)
    hb = tm // POOL_HALO

    def body(u_ref, halo_ref, x_ref, w_ref, s_ref, xo_ref, p_ref, y_ref):
        i, g = pl.program_id(0), pl.program_id(1)
        halo = jnp.where(i > 0, halo_ref[...], 0.0)
        ext = jnp.concatenate([halo, u_ref[...]], axis=0)
        pos = i * tm + lax.broadcasted_iota(jnp.int32, (tm, 1), 0)
        for gi, win in enumerate(POOL_WINDOWS):
            @pl.when(g == gi)
            def _(win=win):
                tot = _window_sum(ext, win, 1)[POOL_HALO:]
                cnt = jnp.minimum(pos + 1, win).astype(F32)
                p = (tot / cnt - u_ref[...]).astype(BF16)
                p_ref[...] = p
                y = jnp.dot(p, w_ref[...].astype(BF16), preferred_element_type=F32)
                y_ref[...] = y
                xo_ref[...] = x_ref[...] + y * s_ref[...]

    blk = pl.BlockSpec((tm, dg), lambda i, g: (i, g))
    return pl.pallas_call(
        body, name=name, grid=(t // tm, ng),
        in_specs=[blk, pl.BlockSpec((POOL_HALO, dg), lambda i, g: (jnp.maximum(i * hb - 1, 0), g)), blk,
                  pl.BlockSpec((None, dg, dg), lambda i, g: (g, 0, 0)), pl.BlockSpec((1, dg), lambda i, g: (0, g))],
        out_specs=[blk, blk, blk],
        out_shape=[jax.ShapeDtypeStruct((t, d), F32), jax.ShapeDtypeStruct((t, d), BF16),
                   jax.ShapeDtypeStruct((t, d), F32)],
        compiler_params=_params("parallel", "parallel"),
    )(u, u, x, w_group, scale)


def _pool_bwd(dx, p, y_pre, w_group, scale, *, name):
    t, d = dx.shape
    ng, dg = w_group.shape[0], w_group.shape[1]
    tm = _tile(t, 512, POOL_HALO)
    hb = tm // POOL_HALO
    nt = t // tm

    def body(dx_ref, nxt_ref, p_ref, y_ref, w_ref, s_ref, du_ref, dw_ref, ds_ref):
        g, i = pl.program_id(0), pl.program_id(1)

        @pl.when(i == 0)
        def _():
            dw_ref[...] = jnp.zeros_like(dw_ref)
            ds_ref[...] = jnp.zeros_like(ds_ref)

        dxv = dx_ref[...]
        ds_ref[...] += jnp.sum(dxv * y_ref[...], axis=0, keepdims=True)
        nxt = jnp.where(i < nt - 1, nxt_ref[...], 0.0)
        dyp = (jnp.concatenate([dxv, nxt], axis=0) * s_ref[...]).astype(BF16)
        dw_ref[...] += lax.dot_general(p_ref[...], dyp[:tm], (((0,), (0,)), ((), ())), preferred_element_type=F32)
        dp = lax.dot_general(dyp, w_ref[...].astype(BF16), (((1,), (1,)), ((), ())), preferred_element_type=F32)
        pos = i * tm + lax.broadcasted_iota(jnp.int32, (tm + POOL_HALO, 1), 0)
        for gi, win in enumerate(POOL_WINDOWS):
            @pl.when(g == gi)
            def _(win=win):
                q = dp / jnp.minimum(pos + 1, win).astype(F32)
                du_ref[...] = (_window_sum(q, win, -1)[:tm] - dp[:tm]).astype(BF16)

    blk = pl.BlockSpec((tm, dg), lambda g, i: (i, g))
    return pl.pallas_call(
        body, name=name, grid=(ng, nt),
        in_specs=[blk, pl.BlockSpec((POOL_HALO, dg), lambda g, i: (jnp.minimum((i + 1) * hb, t // POOL_HALO - 1), g)),
                  blk, blk, pl.BlockSpec((None, dg, dg), lambda g, i: (g, 0, 0)),
                  pl.BlockSpec((1, dg), lambda g, i: (0, g))],
        out_specs=[blk, pl.BlockSpec((None, dg, dg), lambda g, i: (g, 0, 0)), pl.BlockSpec((1, dg), lambda g, i: (0, g))],
        out_shape=[jax.ShapeDtypeStruct((t, d), BF16), jax.ShapeDtypeStruct((ng, dg, dg), F32),
                   jax.ShapeDtypeStruct((1, d), F32)],
        compiler_params=_params("parallel", "arbitrary"),
    )(dx, dx, p, y_pre, w_group, scale)


def _pool_mixer_fwd(x, gain, w_in, w_group, scale, tag):
    h = _rms_fwd(x, gain, name=f"pool_norm_{tag}")
    u = _matmul(h, w_in, "nn", name=f"pool_in_{tag}")
    x_new, p, y_pre = _pool_fwd(u, x, w_group, scale, name=f"pool_mix_{tag}")
    return x_new, (x, h, p, y_pre)


def _pool_mixer_bwd(dx, saved, gain, w_in, w_group, scale, tag):
    x, h, p, y_pre = saved
    du, dw_group, dscale = _pool_bwd(dx, p, y_pre, w_group, scale, name=f"pool_dmix_{tag}")
    dw_in = _matmul_tn(h, du, name=f"pool_dwin_{tag}")
    dh = _matmul(du, w_in, "nt", name=f"pool_dh_{tag}")
    dx_in, dgain = _rms_bwd(x, gain, dh, dx, name=f"pool_dnorm_{tag}")
    return dx_in, dw_in, dw_group, dscale, dgain


CONV_HALO = 8
HIGHEST = lax.Precision.HIGHEST
NEG_BIG = -1e30


def _softplus(v):
    return jnp.maximum(v, 0.0) + jnp.log(1.0 + jnp.exp(-jnp.abs(v)))


def _dot_exact(a, b):
    return jnp.dot(a, b, precision=HIGHEST, preferred_element_type=F32)


def _conv_taps(ext, w_ref, off, rows):
    acc = None
    for k in range(SSD_CONV):
        shift = SSD_CONV - 1 - k
        v = (_shift_rows(ext, shift) if shift else ext)[off:off + rows] * w_ref[k:k + 1, :]
        acc = v if acc is None else acc + v
    return acc


def _ssd_conv_fwd(zx, conv_w, conv_b, col0, *, name):
    t = zx.shape[0]
    c = conv_w.shape[1]
    tm, tc = _tile(t, 512, CONV_HALO), _tile(c, 512, LANES)
    hb, cb0 = tm // CONV_HALO, col0 // tc
    assert col0 % tc == 0

    def body(x_ref, halo_ref, w_ref, b_ref, o_ref):
        halo = jnp.where(pl.program_id(0) > 0, halo_ref[...], 0.0)
        ext = jnp.concatenate([halo, x_ref[...]], axis=0)
        pre = _conv_taps(ext, w_ref, CONV_HALO, tm) + b_ref[...]
        o_ref[...] = pre * _sigmoid(pre)

    return pl.pallas_call(
        body, name=name, grid=(t // tm, c // tc),
        in_specs=[pl.BlockSpec((tm, tc), lambda i, j: (i, j + cb0)),
                  pl.BlockSpec((CONV_HALO, tc), lambda i, j: (jnp.maximum(i * hb - 1, 0), j + cb0)),
                  pl.BlockSpec((SSD_CONV, tc), lambda i, j: (0, j)), pl.BlockSpec((1, tc), lambda i, j: (0, j))],
        out_specs=pl.BlockSpec((tm, tc), lambda i, j: (i, j)),
        out_shape=jax.ShapeDtypeStruct((t, c), F32),
        compiler_params=_params("parallel", "parallel"),
    )(zx, zx, conv_w, conv_b)


def _ssd_conv_bwd(dxa, zx, conv_w, conv_b, col0, *, name):
    t = zx.shape[0]
    c = conv_w.shape[1]
    tm, tc = _tile(t, 512, CONV_HALO), _tile(c, 512, LANES)
    hb, cb0, nt = tm // CONV_HALO, col0 // tc, t // tm
    last_halo = t // CONV_HALO - 1

    def body(x_ref, prev_ref, nxt_ref, d_ref, dnxt_ref, w_ref, b_ref, dx_ref, dw_ref, db_ref):
        i = pl.program_id(1)

        @pl.when(i == 0)
        def _():
            dw_ref[...] = jnp.zeros_like(dw_ref)
            db_ref[...] = jnp.zeros_like(db_ref)

        prev = jnp.where(i > 0, prev_ref[...], 0.0)
        has_next = i < nt - 1
        ext = jnp.concatenate([prev, x_ref[...], jnp.where(has_next, nxt_ref[...], 0.0)], axis=0)
        pre = _conv_taps(ext, w_ref, CONV_HALO, tm + CONV_HALO) + b_ref[...]
        sg = _sigmoid(pre)
        dact = jnp.concatenate([d_ref[...], jnp.where(has_next, dnxt_ref[...], 0.0)], axis=0)
        dpre = dact * (sg * (1.0 + pre * (1.0 - sg)))
        db_ref[...] += jnp.sum(dpre[:tm], axis=0, keepdims=True)
        acc = None
        for k in range(SSD_CONV):
            shift = SSD_CONV - 1 - k
            src = (_shift_rows(ext, shift) if shift else ext)[CONV_HALO:CONV_HALO + tm]
            dw_ref[k:k + 1, :] += jnp.sum(dpre[:tm] * src, axis=0, keepdims=True)
            v = (_shift_rows(dpre, -shift) if shift else dpre)[:tm] * w_ref[k:k + 1, :]
            acc = v if acc is None else acc + v
        dx_ref[...] = acc.astype(BF16)

    main = lambda j, i: (i, j + cb0)
    return pl.pallas_call(
        body, name=name, grid=(c // tc, nt),
        in_specs=[pl.BlockSpec((tm, tc), main),
                  pl.BlockSpec((CONV_HALO, tc), lambda j, i: (jnp.maximum(i * hb - 1, 0), j + cb0)),
                  pl.BlockSpec((CONV_HALO, tc), lambda j, i: (jnp.minimum((i + 1) * hb, last_halo), j + cb0)),
                  pl.BlockSpec((tm, tc), lambda j, i: (i, j)),
                  pl.BlockSpec((CONV_HALO, tc), lambda j, i: (jnp.minimum((i + 1) * hb, last_halo), j)),
                  pl.BlockSpec((SSD_CONV, tc), lambda j, i: (0, j)), pl.BlockSpec((1, tc), lambda j, i: (0, j))],
        out_specs=[pl.BlockSpec((tm, tc), lambda j, i: (i, j)), pl.BlockSpec((SSD_CONV, tc), lambda j, i: (0, j)),
                   pl.BlockSpec((1, tc), lambda j, i: (0, j))],
        out_shape=[jax.ShapeDtypeStruct((t, c), BF16), jax.ShapeDtypeStruct((SSD_CONV, c), F32),
                   jax.ShapeDtypeStruct((1, c), F32)],
        compiler_params=_params("parallel", "arbitrary"),
    )(zx, zx, zx, dxa, dxa, conv_w, conv_b)


def _head_select(g, transposed):
    row = lax.broadcasted_iota(jnp.int32, (LANES, LANES), 0)
    col = lax.broadcasted_iota(jnp.int32, (LANES, LANES), 1)
    if transposed:
        row, col = col, row
    return ((row == g * SSD_HEADS_PER_GROUP + col) & (col < SSD_HEADS_PER_GROUP)).astype(F32)


def _ssd_chunk_common(dtp_ref, par_ref, g):
    ell = SSD_CHUNK
    dt_all = _softplus(dtp_ref[...] + par_ref[0:1, :])
    a_all = -jnp.exp(par_ref[1:2, :])
    sel = _head_select(g, False)
    dtg = _dot_exact(dt_all, sel)
    dag = _dot_exact(dt_all * a_all, sel)
    row = lax.broadcasted_iota(jnp.int32, (ell, ell), 0)
    col = lax.broadcasted_iota(jnp.int32, (ell, ell), 1)
    causal = row >= col
    acum = _dot_exact(causal.astype(F32), dag)
    return dt_all, a_all, dtg, acum, acum.T, causal


def _ssd_scan_fwd(xa, dtp, par, n_groups, *, name):
    t = xa.shape[0]
    ell, hd, hpg, ns = SSD_CHUNK, SSD_HEAD_DIM, SSD_HEADS_PER_GROUP, SSD_STATE
    gw = hpg * hd
    nc = t // ell
    b_blk0, c_blk0 = n_groups * gw // ns, n_groups * gw // ns + n_groups

    def body(xs_ref, b_ref, c_ref, dtp_ref, par_ref, y_ref, sin_ref, st):
        c, g = pl.program_id(0), pl.program_id(1)

        @pl.when(c == 0)
        def _():
            for hh in range(hpg):
                st[g * hpg + hh] = jnp.zeros((hd, ns), F32)

        _, _, dtg, acum, acum_t, causal = _ssd_chunk_common(dtp_ref, par_ref, g)
        bb, cc = b_ref[...].astype(BF16), c_ref[...].astype(BF16)
        cb = lax.dot_general(cc, bb, (((1,), (1,)), ((), ())), preferred_element_type=F32)
        for hh in range(hpg):
            lanes = slice(hh * hd, (hh + 1) * hd)
            col_a, row_a = acum[:, hh:hh + 1], acum_t[hh:hh + 1, :]
            decay = jnp.exp(jnp.where(causal, col_a - row_a, NEG_BIG))
            xdt = xs_ref[:, lanes] * dtg[:, hh:hh + 1]
            s_h = st[g * hpg + hh]
            sin_ref[lanes, :] = s_h
            y = jnp.dot((cb * decay).astype(BF16), xdt.astype(BF16), preferred_element_type=F32)
            y += jnp.exp(col_a) * lax.dot_general(cc, s_h.astype(BF16), (((1,), (1,)), ((), ())),
                                                  preferred_element_type=F32)
            y_ref[:, lanes] = y
            a_last = acum[ell - 1:ell, hh:hh + 1]
            w = jnp.exp(a_last - col_a)
            st[g * hpg + hh] = jnp.exp(a_last) * s_h + lax.dot_general(
                (xdt * w).astype(BF16), bb, (((0,), (0,)), ((), ())), preferred_element_type=F32)

    return pl.pallas_call(
        body, name=name, grid=(nc, n_groups),
        in_specs=[pl.BlockSpec((ell, gw), lambda c, g: (c, g)),
                  pl.BlockSpec((ell, ns), lambda c, g: (c, b_blk0 + g)),
                  pl.BlockSpec((ell, ns), lambda c, g: (c, c_blk0 + g)),
                  pl.BlockSpec((ell, LANES), lambda c, g: (c, 0)),
                  pl.BlockSpec((SUBLANES, LANES), lambda c, g: (0, 0))],
        out_specs=[pl.BlockSpec((ell, gw), lambda c, g: (c, g)),
                   pl.BlockSpec((None, None, gw, ns), lambda c, g: (c, g, 0, 0))],
        out_shape=[jax.ShapeDtypeStruct((t, n_groups * gw), F32),
                   jax.ShapeDtypeStruct((nc, n_groups, gw, ns), F32)],
        scratch_shapes=[pltpu.VMEM((n_groups * hpg, hd, ns), F32)],
        compiler_params=_params("arbitrary", "arbitrary"),
    )(xa, xa, xa, dtp, par)


def _ssd_scan_bwd(dy, xa, dtp, par, s_in, n_groups, *, name):
    t = xa.shape[0]
    ell, hd, hpg, ns = SSD_CHUNK, SSD_HEAD_DIM, SSD_HEADS_PER_GROUP, SSD_STATE
    gw = hpg * hd
    nc = t // ell
    b_blk0, c_blk0 = n_groups * gw // ns, n_groups * gw // ns + n_groups
    nt_dims = (((1,), (1,)), ((), ()))
    tn_dims = (((0,), (0,)), ((), ()))

    def body(dy_ref, xs_ref, b_ref, c_ref, dtp_ref, par_ref, sin_ref,
             dxs_ref, db_ref, dc_ref, ddtp_ref, dpar_ref, dst):
        i, g = pl.program_id(0), pl.program_id(1)

        @pl.when(i == 0)
        def _():
            for hh in range(hpg):
                dst[g * hpg + hh] = jnp.zeros((hd, ns), F32)

        @pl.when((i == 0) & (g == 0))
        def _():
            dpar_ref[...] = jnp.zeros_like(dpar_ref)

        @pl.when(g == 0)
        def _():
            ddtp_ref[...] = jnp.zeros_like(ddtp_ref)

        dt_all, a_all, dtg, acum, acum_t, causal = _ssd_chunk_common(dtp_ref, par_ref, g)
        sel_t = _head_select(g, True)
        d_g = _dot_exact(jnp.broadcast_to(par_ref[2:3, :], (SUBLANES, LANES)), _head_select(g, False))
        bf, cf = b_ref[...], c_ref[...]
        bb, cc = bf.astype(BF16), cf.astype(BF16)
        cb = lax.dot_general(cc, bb, nt_dims, preferred_element_type=F32)
        lane = lax.broadcasted_iota(jnp.int32, (1, LANES), 1)
        ones = jnp.ones((ell, LANES), F32)
        dcb = jnp.zeros((ell, ell), F32)
        dc = jnp.zeros((ell, ns), F32)
        db = jnp.zeros((ell, ns), F32)
        dacum = jnp.zeros((ell, LANES), F32)
        xsum = jnp.zeros((ell, LANES), F32)
        dsum = jnp.zeros((1, LANES), F32)
        for hh in range(hpg):
            lanes = slice(hh * hd, (hh + 1) * hd)
            onehot = (lane == hh).astype(F32)
            col_a, row_a = acum[:, hh:hh + 1], acum_t[hh:hh + 1, :]
            decay = jnp.exp(jnp.where(causal, col_a - row_a, NEG_BIG))
            e_col = jnp.exp(col_a)
            a_last = acum[ell - 1:ell, hh:hh + 1]
            w = jnp.exp(a_last - col_a)
            e_last = jnp.exp(a_last)
            xs_h, dy_h = xs_ref[:, lanes], dy_ref[:, lanes]
            dt_h = dtg[:, hh:hh + 1]
            xdt = xs_h * dt_h
            xdt_b, dy_b = xdt.astype(BF16), dy_h.astype(BF16)
            s_h, ds_h = sin_ref[lanes, :], dst[g * hpg + hh]
            m = cb * decay
            dm = lax.dot_general(dy_b, xdt_b, nt_dims, preferred_element_type=F32)
            dcb += dm * decay
            gd = dm * m
            dac = jnp.sum(gd, axis=1, keepdims=True) - lax.dot_general(gd, ones, tn_dims, precision=HIGHEST,
                                                                       preferred_element_type=F32)[:, 0:1]
            b_ds = lax.dot_general(bb, ds_h.astype(BF16), nt_dims, preferred_element_type=F32)
            dxdt = lax.dot_general(m.astype(BF16), dy_b, tn_dims, preferred_element_type=F32) + w * b_ds
            c_s = lax.dot_general(cc, s_h.astype(BF16), nt_dims, preferred_element_type=F32)
            dc += e_col * jnp.dot(dy_b, s_h.astype(BF16), preferred_element_type=F32)
            db += w * jnp.dot(xdt_b, ds_h.astype(BF16), preferred_element_type=F32)
            dac += jnp.sum(dy_h * c_s, axis=1, keepdims=True) * e_col
            q = jnp.sum(xdt * b_ds, axis=1, keepdims=True) * w
            dac -= q
            d_last = jnp.sum(q, axis=0, keepdims=True) + e_last * jnp.sum(
                jnp.sum(s_h * ds_h, axis=1, keepdims=True), axis=0, keepdims=True)
            is_last = lax.broadcasted_iota(jnp.int32, (ell, 1), 0) == ell - 1
            dac += jnp.where(is_last, d_last, 0.0)
            dacum += dac * onehot
            dst[g * hpg + hh] = e_last * ds_h + lax.dot_general((dy_h * e_col).astype(BF16), cc, tn_dims,
                                                                preferred_element_type=F32)
            d_h = d_g[0:1, hh:hh + 1]
            dxs_ref[:, lanes] = dxdt * dt_h + dy_h * d_h
            xsum += jnp.sum(dxdt * xs_h, axis=1, keepdims=True) * onehot
            dsum += jnp.sum(jnp.sum(dy_h * xs_h, axis=1, keepdims=True), axis=0, keepdims=True) * onehot
        dc_ref[...] = dc + jnp.dot(dcb.astype(BF16), bb, preferred_element_type=F32)
        db_ref[...] = db + lax.dot_general(dcb.astype(BF16), cc, tn_dims, preferred_element_type=F32)
        row = lax.broadcasted_iota(jnp.int32, (ell, ell), 0)
        col = lax.broadcasted_iota(jnp.int32, (ell, ell), 1)
        dda = _dot_exact((col >= row).astype(F32), dacum)
        a_g = _dot_exact(jnp.broadcast_to(a_all, (SUBLANES, LANES)), _head_select(g, False))[0:1, :]
        ddt_g = xsum + dda * a_g
        ddt_all = _dot_exact(ddt_g, sel_t)
        z = dtp_ref[...] + par_ref[0:1, :]
        ddtp = ddt_all * _sigmoid(z)
        ddtp_ref[...] += ddtp
        da_all = jnp.sum(_dot_exact(dda * dtg, sel_t), axis=0, keepdims=True) * a_all
        dd_all = _dot_exact(jnp.broadcast_to(dsum, (SUBLANES, LANES)), sel_t)[0:1, :]
        dpar_ref[0:1, :] += jnp.sum(ddtp, axis=0, keepdims=True)
        dpar_ref[1:2, :] += da_all
        dpar_ref[2:3, :] += dd_all

    rev = lambda i: nc - 1 - i
    return pl.pallas_call(
        body, name=name, grid=(nc, n_groups),
        in_specs=[pl.BlockSpec((ell, gw), lambda i, g: (rev(i), g)),
                  pl.BlockSpec((ell, gw), lambda i, g: (rev(i), g)),
                  pl.BlockSpec((ell, ns), lambda i, g: (rev(i), b_blk0 + g)),
                  pl.BlockSpec((ell, ns), lambda i, g: (rev(i), c_blk0 + g)),
                  pl.BlockSpec((ell, LANES), lambda i, g: (rev(i), 0)),
                  pl.BlockSpec((SUBLANES, LANES), lambda i, g: (0, 0)),
                  pl.BlockSpec((None, None, gw, ns), lambda i, g: (rev(i), g, 0, 0))],
        out_specs=[pl.BlockSpec((ell, gw), lambda i, g: (rev(i), g)),
                   pl.BlockSpec((ell, ns), lambda i, g: (rev(i), g)),
                   pl.BlockSpec((ell, ns), lambda i, g: (rev(i), g)),
                   pl.BlockSpec((ell, LANES), lambda i, g: (rev(i), 0)),
                   pl.BlockSpec((SUBLANES, LANES), lambda i, g: (0, 0))],
        out_shape=[jax.ShapeDtypeStruct((t, n_groups * gw), F32), jax.ShapeDtypeStruct((t, n_groups * ns), F32),
                   jax.ShapeDtypeStruct((t, n_groups * ns), F32), jax.ShapeDtypeStruct((t, LANES), F32),
                   jax.ShapeDtypeStruct((SUBLANES, LANES), F32)],
        scratch_shapes=[pltpu.VMEM((n_groups * hpg, hd, ns), F32)],
        compiler_params=_params("arbitrary", "arbitrary"),
    )(dy, xa, xa, xa, dtp, par, s_in)


def _ssd_gate_fwd(y, xa, zx, d_rep, out_norm, *, name):
    t, di = y.shape
    gw = SSD_HEADS_PER_GROUP * SSD_HEAD_DIM
    tm = _tile(t, 512, SUBLANES)

    def body(y_ref, xs_ref, z_ref, d_ref, n_ref, o_ref):
        zv = z_ref[...]
        gt = (y_ref[...] + d_ref[...] * xs_ref[...]) * (zv * _sigmoid(zv))
        r = lax.rsqrt(jnp.mean(gt * gt, axis=-1, keepdims=True) + NORM_EPS)
        o_ref[...] = (gt * r * n_ref[...]).astype(BF16)

    blk = pl.BlockSpec((tm, gw), lambda i, g: (i, g))
    vec = pl.BlockSpec((1, gw), lambda i, g: (0, g))
    return pl.pallas_call(
        body, name=name, grid=(t // tm, di // gw),
        in_specs=[blk, blk, blk, vec, vec], out_specs=blk,
        out_shape=jax.ShapeDtypeStruct((t, di), BF16),
        compiler_params=_params("parallel", "parallel"),
    )(y, xa, zx, d_rep, out_norm)


def _ssd_gate_bwd(dgn, y, xa, zx, d_rep, out_norm, *, name):
    t, di = y.shape
    gw = SSD_HEADS_PER_GROUP * SSD_HEAD_DIM
    tm = _tile(t, 512, SUBLANES)

    def body(dg_ref, y_ref, xs_ref, z_ref, d_ref, n_ref, dy_ref, dz_ref, dn_ref):
        @pl.when(pl.program_id(1) == 0)
        def _():
            dn_ref[...] = jnp.zeros_like(dn_ref)

        zv = z_ref[...]
        sg = _sigmoid(zv)
        sz = zv * sg
        y2 = y_ref[...] + d_ref[...] * xs_ref[...]
        gt = y2 * sz
        r = lax.rsqrt(jnp.mean(gt * gt, axis=-1, keepdims=True) + NORM_EPS)
        ghat = gt * r
        dgv = dg_ref[...]
        dn_ref[...] += jnp.sum(dgv * ghat, axis=0, keepdims=True)
        u = dgv * n_ref[...]
        dgt = r * (u - ghat * jnp.mean(u * ghat, axis=-1, keepdims=True))
        dy_ref[...] = dgt * sz
        dz_ref[...] = (dgt * y2 * (sg * (1.0 + zv * (1.0 - sg)))).astype(BF16)

    blk = pl.BlockSpec((tm, gw), lambda g, i: (i, g))
    vec = pl.BlockSpec((1, gw), lambda g, i: (0, g))
    return pl.pallas_call(
        body, name=name, grid=(di // gw, t // tm),
        in_specs=[blk, blk, blk, blk, vec, vec], out_specs=[blk, blk, vec],
        out_shape=[jax.ShapeDtypeStruct((t, di), F32), jax.ShapeDtypeStruct((t, di), BF16),
                   jax.ShapeDtypeStruct((1, di), F32)],
        compiler_params=_params("parallel", "arbitrary"),
    )(dgn, y, xa, zx, d_rep, out_norm)


def _ssd_mixer_fwd(x, gain, w_zx_t, w_dt_t, conv_w, conv_b, par, d_rep, out_norm, w_out, tag):
    di = w_out.shape[0]
    n_groups = di // (SSD_HEADS_PER_GROUP * SSD_HEAD_DIM)
    h = _rms_fwd(x, gain, name=f"ssd_norm_{tag}")
    zx = _matmul(h, w_zx_t, "nt", name=f"ssd_in_{tag}")
    dtp = _matmul(h, w_dt_t, "nt", name=f"ssd_dt_{tag}")
    xa = _ssd_conv_fwd(zx, conv_w, conv_b, di, name=f"ssd_conv_{tag}")
    y, s_in = _ssd_scan_fwd(xa, dtp, par, n_groups, name=f"ssd_scan_{tag}")
    gn = _ssd_gate_fwd(y, xa, zx, d_rep, out_norm, name=f"ssd_gate_{tag}")
    x_new = _matmul(gn, w_out, "nn", resid=x, name=f"ssd_out_{tag}")
    return x_new, (x, h, zx, dtp, xa, y, s_in, gn)


def _ssd_mixer_bwd(dx, saved, gain, w_zx_t, w_dt_t, conv_w, conv_b, par, d_rep, out_norm, w_out, tag):
    x, h, zx, dtp, xa, y, s_in, gn = saved
    di = w_out.shape[0]
    n_groups = di // (SSD_HEADS_PER_GROUP * SSD_HEAD_DIM)
    dgn = _matmul(dx, w_out, "nt", name=f"ssd_dgn_{tag}")
    dw_out = _matmul_tn(gn, dx, name=f"ssd_dwout_{tag}")
    dy2, dz, dnorm = _ssd_gate_bwd(dgn, y, xa, zx, d_rep, out_norm, name=f"ssd_dgate_{tag}")
    dxs, db, dc, ddtp, dpar = _ssd_scan_bwd(dy2, xa, dtp, par, s_in, n_groups, name=f"ssd_dscan_{tag}")
    dxa = jnp.concatenate([dxs, db, dc], axis=1)
    dxbc, dconv_w, dconv_b = _ssd_conv_bwd(dxa, zx, conv_w, conv_b, di, name=f"ssd_dconv_{tag}")
    dzx = jnp.concatenate([dz, dxbc], axis=1)
    dw_zx_t = _matmul_tn(dzx, h, name=f"ssd_dwin_{tag}")
    dw_dt_t = _matmul_tn(ddtp, h, name=f"ssd_dwdt_{tag}")
    dh = _matmul(dzx, w_zx_t, "nn", name=f"ssd_dh_{tag}")
    dh = _matmul(ddtp, w_dt_t, "nn", resid=dh, name=f"ssd_dhdt_{tag}")
    dx_in, dgain = _rms_bwd(x, gain, dh, dx, name=f"ssd_dnorm_{tag}")
    return dx_in, dw_zx_t, dw_dt_t, dconv_w, dconv_b, dpar, dnorm, dw_out, dgain


def _sb_qk_norm_fwd(qkv, gains, *, name):
    ns, t, _ = qkv.shape
    per = ns // 3
    tm = _tile(t, 1024, SUBLANES)
    inv_sqrt_d = 1.0 / math.sqrt(SB_HEAD_DIM)

    def body(x_ref, g_ref, o_ref):
        kind = pl.program_id(0) // per
        xv = x_ref[...]

        @pl.when(kind == 2)
        def _():
            o_ref[...] = xv.astype(BF16)

        @pl.when(kind < 2)
        def _():
            left = lax.broadcasted_iota(jnp.int32, (1, LANES), 1) < SB_HEAD_DIM
            sq = xv * xv
            ms = jnp.where(left, jnp.sum(jnp.where(left, sq, 0.0), axis=1, keepdims=True),
                           jnp.sum(jnp.where(left, 0.0, sq), axis=1, keepdims=True)) * (1.0 / SB_HEAD_DIM)
            y = xv * lax.rsqrt(ms + NORM_EPS) * g_ref[pl.ds(kind, 1), :]
            o_ref[...] = (y * jnp.where(kind == 0, inv_sqrt_d, 1.0)).astype(BF16)

    blk = pl.BlockSpec((None, tm, LANES), lambda s, i: (s, i, 0))
    return pl.pallas_call(
        body, name=name, grid=(ns, t // tm),
        in_specs=[blk, pl.BlockSpec((SUBLANES, LANES), lambda s, i: (0, 0))], out_specs=blk,
        out_shape=jax.ShapeDtypeStruct((ns, t, LANES), BF16),
        compiler_params=_params("parallel", "parallel"),
    )(qkv, gains)


def _sb_qk_norm_bwd(dq, dk, dv, qkv, gains, *, name):
    ns, t, _ = qkv.shape
    per = ns // 3
    tm = _tile(t, 1024, SUBLANES)
    inv_sqrt_d = 1.0 / math.sqrt(SB_HEAD_DIM)

    def body(dq_ref, dk_ref, dv_ref, x_ref, g_ref, o_ref, dg_ref):
        s = pl.program_id(0)
        kind = s // per

        @pl.when((s == 0) & (pl.program_id(1) == 0))
        def _():
            dg_ref[...] = jnp.zeros_like(dg_ref)

        @pl.when(kind == 2)
        def _():
            o_ref[...] = dv_ref[...].astype(BF16)

        @pl.when(kind < 2)
        def _():
            xv = x_ref[...]
            dy = jnp.where(kind == 0, dq_ref[...] * inv_sqrt_d, dk_ref[...])
            left = lax.broadcasted_iota(jnp.int32, (1, LANES), 1) < SB_HEAD_DIM

            def halves(v):
                return jnp.where(left, jnp.sum(jnp.where(left, v, 0.0), axis=1, keepdims=True),
                                 jnp.sum(jnp.where(left, 0.0, v), axis=1, keepdims=True))

            r = lax.rsqrt(halves(xv * xv) * (1.0 / SB_HEAD_DIM) + NORM_EPS)
            xhat = xv * r
            u = dy * g_ref[pl.ds(kind, 1), :]
            o_ref[...] = (r * (u - xhat * halves(u * xhat) * (1.0 / SB_HEAD_DIM))).astype(BF16)
            dg_ref[pl.ds(kind, 1), :] += jnp.sum(dy * xhat, axis=0, keepdims=True)

    def grad_blk(kind):
        return pl.BlockSpec((None, tm, LANES), lambda s, i: (jnp.clip(s - kind * per, 0, per - 1), i, 0))

    blk = pl.BlockSpec((None, tm, LANES), lambda s, i: (s, i, 0))
    vec = pl.BlockSpec((SUBLANES, LANES), lambda s, i: (0, 0))
    return pl.pallas_call(
        body, name=name, grid=(ns, t // tm),
        in_specs=[grad_blk(0), grad_blk(1), grad_blk(2), blk, vec], out_specs=[blk, vec],
        out_shape=[jax.ShapeDtypeStruct((ns, t, LANES), BF16), jax.ShapeDtypeStruct((SUBLANES, LANES), F32)],
        compiler_params=_params("arbitrary", "arbitrary"),
    )(dq, dk, dv, qkv, gains)


def _split_dot(v, ones_mat, pieces, left=False):
    total, rest = None, v
    for p in range(pieces):
        part = rest.astype(BF16)
        if p + 1 < pieces:
            rest = rest - part.astype(F32)
        d = (jnp.dot(ones_mat, part, preferred_element_type=F32) if left
             else jnp.dot(part, ones_mat, preferred_element_type=F32))
        total = d if total is None else total + d
    return total


LOGIT_SUM_PIECES = 3
GRAD_SUM_PIECES = 2


def _sb_attn_fwd(qkv_n, n_heads, *, name):
    ns, t, _ = qkv_n.shape
    per = ns // 3
    blk, hd = SB_BLOCK, SB_HEAD_DIM
    nq = t // blk

    def body(q_ref, k_ref, v_ref, o_ref, tot_ref):
        i = pl.program_id(1)
        row = lax.broadcasted_iota(jnp.int32, (blk, blk), 0)
        col = lax.broadcasted_iota(jnp.int32, (blk, blk), 1)
        later_keys = (row > col).astype(BF16)

        def tile(kb, carry, diag):
            out = []
            start = pl.multiple_of(kb * blk, blk)
            for hf in range(2):
                lanes = slice(hf * hd, (hf + 1) * hd)
                run, acc = carry[hf]
                z = lax.dot_general(q_ref[:, lanes], k_ref[pl.ds(start, blk), lanes], (((1,), (1,)), ((), ())),
                                    preferred_element_type=F32)
                sp = _softplus(z)
                lm = jnp.where(col < row, -sp, 0.0) if diag else -sp
                after = _split_dot(lm, later_keys, LOGIT_SUM_PIECES) + run
                a = jnp.exp(z - sp + after)
                if diag:
                    a = jnp.where(col < row, a, 0.0)
                acc = acc + jnp.dot(a.astype(BF16), v_ref[pl.ds(start, blk), lanes], preferred_element_type=F32)
                out.append((run + jnp.sum(lm, axis=1, keepdims=True), acc))
            return tuple(out)

        init = tuple((jnp.zeros((blk, 1), F32), jnp.zeros((blk, hd), F32)) for _ in range(2))
        carry = tile(i, init, True)
        carry = lax.fori_loop(0, i, lambda s, c: tile(i - 1 - s, c, False), carry)
        o_ref[...] = jnp.concatenate([carry[0][1], carry[1][1]], axis=1)
        first_half = lax.broadcasted_iota(jnp.int32, (1, LANES), 1) < hd
        tot_ref[...] = jnp.where(first_half, carry[0][0], carry[1][0])

    return pl.pallas_call(
        body, name=name, grid=(per, nq),
        in_specs=[pl.BlockSpec((None, blk, LANES), lambda p, i: (p, i, 0)),
                  pl.BlockSpec((None, t, LANES), lambda p, i: (per + p, 0, 0)),
                  pl.BlockSpec((None, t, LANES), lambda p, i: (2 * per + p, 0, 0))],
        out_specs=[pl.BlockSpec((blk, LANES), lambda p, i: (i, p)),
                   pl.BlockSpec((None, blk, LANES), lambda p, i: (p, i, 0))],
        out_shape=[jax.ShapeDtypeStruct((t, n_heads * hd), F32), jax.ShapeDtypeStruct((per, t, LANES), F32)],
        compiler_params=_params("parallel", "arbitrary"),
    )(qkv_n, qkv_n, qkv_n)


def _sb_attn_bwd(do, lm_tot, qkv_n, *, name):
    ns, t, _ = qkv_n.shape
    per = ns // 3
    blk, hd = SB_BLOCK, SB_HEAD_DIM
    nq = t // blk
    nt_dims = (((1,), (1,)), ((), ()))
    tn_dims = (((0,), (0,)), ((), ()))

    def body(q_ref, k_ref, v_ref, do_ref, tot_ref, dq_ref, dk_ref, dv_ref):
        i = pl.program_id(1)

        @pl.when(i == 0)
        def _():
            dk_ref[...] = jnp.zeros_like(dk_ref)
            dv_ref[...] = jnp.zeros_like(dv_ref)

        key = lax.broadcasted_iota(jnp.int32, (blk, blk), 0)
        qry = lax.broadcasted_iota(jnp.int32, (blk, blk), 1)
        later_keys = (qry > key).astype(BF16)
        earlier_keys = (qry < key).astype(BF16)
        tot_t = tot_ref[...].T
        halves = [slice(hf * hd, (hf + 1) * hd) for hf in range(2)]
        q_hs = [q_ref[:, lanes] for lanes in halves]
        do_bs = [do_ref[:, lanes].astype(BF16) for lanes in halves]
        tots = [tot_t[hf * hd:hf * hd + 1, :] for hf in range(2)]

        def tile(kb, carry, diag):
            out = []
            start = pl.multiple_of(kb * blk, blk)
            for hf, lanes in enumerate(halves):
                seen, gsum, dq = carry[hf]
                q_h, do_b = q_hs[hf], do_bs[hf]
                k_blk = k_ref[pl.ds(start, blk), lanes]
                z = lax.dot_general(k_blk, q_h, nt_dims, preferred_element_type=F32)
                sp = _softplus(z)
                lm = jnp.where(key < qry, -sp, 0.0) if diag else -sp
                blk_tot = jnp.sum(lm, axis=0, keepdims=True)
                after = _split_dot(lm, later_keys, LOGIT_SUM_PIECES, left=True) + (tots[hf] - seen - blk_tot)
                a = jnp.exp(z - sp + after)
                if diag:
                    a = jnp.where(key < qry, a, 0.0)
                da = lax.dot_general(v_ref[pl.ds(start, blk), lanes], do_b, nt_dims, preferred_element_type=F32)
                g = da * a
                before = _split_dot(g, earlier_keys, GRAD_SUM_PIECES, left=True) + gsum
                omb = jnp.exp(-sp)
                dz = g * omb - (1.0 - omb) * before
                if diag:
                    dz = jnp.where(key < qry, dz, 0.0)
                dz_b = dz.astype(BF16)
                dk_ref[pl.ds(start, blk), lanes] += jnp.dot(dz_b, q_h, preferred_element_type=F32)
                dv_ref[pl.ds(start, blk), lanes] += jnp.dot(a.astype(BF16), do_b, preferred_element_type=F32)
                dq = dq + lax.dot_general(dz_b, k_blk, tn_dims, preferred_element_type=F32)
                out.append((seen + blk_tot, gsum + jnp.sum(g, axis=0, keepdims=True), dq))
            return tuple(out)

        init = tuple((jnp.zeros((1, blk), F32), jnp.zeros((1, blk), F32), jnp.zeros((blk, hd), F32))
                     for _ in range(2))
        carry = lax.fori_loop(0, i, lambda kb, c: tile(kb, c, False), init)
        carry = tile(i, carry, True)
        dq_ref[...] = jnp.concatenate([carry[0][2], carry[1][2]], axis=1)

    full = lambda off: pl.BlockSpec((None, t, LANES), lambda p, i: (off + p, 0, 0))
    q_blk = pl.BlockSpec((None, blk, LANES), lambda p, i: (p, i, 0))
    slab = jax.ShapeDtypeStruct((per, t, LANES), F32)
    return pl.pallas_call(
        body, name=name, grid=(per, nq),
        in_specs=[q_blk, full(per), full(2 * per), pl.BlockSpec((blk, LANES), lambda p, i: (i, p)), q_blk],
        out_specs=[q_blk, full(0), full(0)],
        out_shape=[slab, slab, slab],
        compiler_params=_params("parallel", "arbitrary"),
    )(qkv_n, qkv_n, qkv_n, do, lm_tot)


def _sb_mixer_fwd(x, gain, w_qkv_t, qk_gains, w_out, tag):
    n_heads = w_out.shape[0] // SB_HEAD_DIM
    h = _rms_fwd(x, gain, name=f"sb_norm_{tag}")
    qkv = _matmul(h, w_qkv_t, "nt", out_slabs=True, tn_cap=256, name=f"sb_qkv_{tag}")
    qkv_n = _sb_qk_norm_fwd(qkv, qk_gains, name=f"sb_qknorm_{tag}")
    o, lm_tot = _sb_attn_fwd(qkv_n, n_heads, name=f"sb_attn_{tag}")
    x_new = _matmul(o, w_out, "nn", resid=x, name=f"sb_out_{tag}")
    return x_new, (x, h, qkv, qkv_n, o, lm_tot)


def _sb_mixer_bwd(dx, saved, gain, w_qkv_t, qk_gains, w_out, tag):
    x, h, qkv, qkv_n, o, lm_tot = saved
    do = _matmul(dx, w_out, "nt", name=f"sb_do_{tag}")
    dw_out = _matmul_tn(o, dx, name=f"sb_dwout_{tag}")
    dq, dk, dv = _sb_attn_bwd(do, lm_tot, qkv_n, name=f"sb_dattn_{tag}")
    dqkv, dqk_gains = _sb_qk_norm_bwd(dq, dk, dv, qkv, qk_gains, name=f"sb_dqknorm_{tag}")
    dw_qkv_t = _matmul_tn(dqkv, h, a_slabs=True, name=f"sb_dwqkv_{tag}")
    dh = _matmul(dqkv, w_qkv_t, "nn", a_slabs=True, name=f"sb_dh_{tag}")
    dx_in, dgain = _rms_bwd(x, gain, dh, dx, name=f"sb_dnorm_{tag}")
    return dx_in, dw_qkv_t, dqk_gains, dw_out, dgain


MESH = pl.DeviceIdType.MESH


def _position():
    return lax.axis_index("x"), lax.axis_index("y"), lax.axis_index("c")


def _all_gather(shard, *, name, in_vmem):
    rows, n = shard.shape
    space = pltpu.VMEM if in_vmem else pltpu.HBM

    def body(x_ref, out_ref, send_sems, recv_sems, local_sem):
        x, y, c = _position()
        me, sibling = (x, y, c), (x, y, 1 - c)
        chips = [(1 - x, y), (x, 1 - y), (1 - x, 1 - y)]

        def block(px, py, pc):
            return out_ref.at[4 * px + 2 * py + pc]

        def copy(k, blk, to, src=None):
            return pltpu.make_async_remote_copy(
                src_ref=block(*blk) if src is None else src, dst_ref=block(*blk),
                send_sem=send_sems.at[k], recv_sem=recv_sems.at[k], device_id=to, device_id_type=MESH)

        mine = pltpu.make_async_copy(x_ref, block(*me), local_sem)
        mine.start()
        first = [copy(0, me, sibling, src=x_ref)]
        first += [copy(1 + j, me, (*chip, c), src=x_ref) for j, chip in enumerate(chips)]
        for cp in first:
            cp.start()
        passed = [copy(4 + j, (*chip, c), sibling) for j, chip in enumerate(chips)]
        for j, chip in enumerate(chips):
            copy(1 + j, (*chip, c), me).wait_recv()
            passed[j].start()
        copy(0, sibling, me).wait_recv()
        for j, chip in enumerate(chips):
            copy(4 + j, (*chip, 1 - c), me).wait_recv()
        for cp in first + passed:
            cp.wait_send()
        mine.wait()

    return pl.pallas_call(
        body, name=name,
        out_shape=jax.ShapeDtypeStruct((N_DEV, rows, n), shard.dtype),
        in_specs=[pl.BlockSpec(memory_space=space)], out_specs=pl.BlockSpec(memory_space=space),
        scratch_shapes=[pltpu.SemaphoreType.DMA((7,)), pltpu.SemaphoreType.DMA((7,)), pltpu.SemaphoreType.DMA],
        compiler_params=pltpu.CompilerParams(vmem_limit_bytes=V7X_VMEM_LIMIT_BYTES),
    )(shard)


def _exchange_sibling(parts, *, name):
    _, nchip, rows, n = parts.shape

    def body(p_ref, recv_ref, send_sem, recv_sem):
        x, y, c = _position()
        cp = pltpu.make_async_remote_copy(src_ref=p_ref.at[1 - c], dst_ref=recv_ref, send_sem=send_sem,
                                          recv_sem=recv_sem, device_id=(x, y, 1 - c), device_id_type=MESH)
        cp.start()
        cp.wait()

    return pl.pallas_call(
        body, name=name,
        out_shape=jax.ShapeDtypeStruct((nchip, rows, n), parts.dtype),
        in_specs=[pl.BlockSpec(memory_space=pltpu.HBM)], out_specs=pl.BlockSpec(memory_space=pltpu.HBM),
        scratch_shapes=[pltpu.SemaphoreType.DMA, pltpu.SemaphoreType.DMA],
    )(parts)


def _exchange_chips(chip_sums, *, name):
    _, rows, n = chip_sums.shape

    def body(s_ref, recv_ref, send_sems, recv_sems):
        x, y, c = _position()
        chips = [(1 - x, y), (x, 1 - y), (1 - x, 1 - y)]
        copies = [pltpu.make_async_remote_copy(
            src_ref=s_ref.at[2 * cx + cy], dst_ref=recv_ref.at[j], send_sem=send_sems.at[j],
            recv_sem=recv_sems.at[j], device_id=(cx, cy, c), device_id_type=MESH)
            for j, (cx, cy) in enumerate(chips)]
        for cp in copies:
            cp.start()
        for cp in copies:
            cp.wait()

    return pl.pallas_call(
        body, name=name,
        out_shape=jax.ShapeDtypeStruct((3, rows, n), chip_sums.dtype),
        in_specs=[pl.BlockSpec(memory_space=pltpu.HBM)], out_specs=pl.BlockSpec(memory_space=pltpu.HBM),
        scratch_shapes=[pltpu.SemaphoreType.DMA((3,)), pltpu.SemaphoreType.DMA((3,))],
    )(chip_sums)


def _add_pairs(parts, recv, c_mine, *, name):
    _, nchip, rows, n = parts.shape
    tr = _tile(rows, 512, SUBLANES)

    def body(c_ref, a_ref, b_ref, o_ref):
        o_ref[...] = a_ref[...] + b_ref[...]

    return pl.pallas_call(
        body, name=name,
        grid_spec=pltpu.PrefetchScalarGridSpec(
            num_scalar_prefetch=1, grid=(nchip, rows // tr),
            in_specs=[pl.BlockSpec((None, None, tr, n), lambda k, i, c: (c[0], k, i, 0)),
                      pl.BlockSpec((None, tr, n), lambda k, i, c: (k, i, 0))],
            out_specs=pl.BlockSpec((None, tr, n), lambda k, i, c: (k, i, 0))),
        out_shape=jax.ShapeDtypeStruct((nchip, rows, n), parts.dtype),
        compiler_params=_params("parallel", "parallel"),
    )(c_mine, parts, recv)


def _adamw_math(w, g, m, v):
    m = ADAM_B1 * m + (1.0 - ADAM_B1) * g
    v = ADAM_B2 * v + (1.0 - ADAM_B2) * (g * g)
    m_hat = m / (1.0 - ADAM_B1 ** ADAM_STEP)
    v_hat = v / (1.0 - ADAM_B2 ** ADAM_STEP)
    delta = -ADAM_LR * (m_hat / (jnp.sqrt(v_hat) + ADAM_EPS) + ADAM_WD * w)
    return delta, m, v


def _adamw_sharded(chip_sums, recv, k_mine, w, m, v, *, name):
    rows, n = w.shape
    tr = _tile(rows, 256, SUBLANES)

    def body(k_ref, s_ref, r_ref, w_ref, m_ref, v_ref, g_out, d_out, m_out, v_out):
        g = ((s_ref[...] + r_ref[0]) + r_ref[1]) + r_ref[2]
        delta, m_new, v_new = _adamw_math(w_ref[...], g, m_ref[...], v_ref[...])
        g_out[...] = g
        d_out[...] = delta
        m_out[...] = m_new
        v_out[...] = v_new

    blk = pl.BlockSpec((tr, n), lambda i, k: (i, 0))
    out = jax.ShapeDtypeStruct((rows, n), F32)
    return pl.pallas_call(
        body, name=name,
        grid_spec=pltpu.PrefetchScalarGridSpec(
            num_scalar_prefetch=1, grid=(rows // tr,),
            in_specs=[pl.BlockSpec((None, tr, n), lambda i, k: (k[0], i, 0)),
                      pl.BlockSpec((3, tr, n), lambda i, k: (0, i, 0)), blk, blk, blk],
            out_specs=[blk, blk, blk, blk]),
        out_shape=[out, out, out, out],
        compiler_params=_params("parallel"),
    )(k_mine, chip_sums, recv, w, m, v)


SMALL_ROWS = 40
ROW_MIX_NORM, ROW_FFN_NORM, ROW_CONV_B, ROW_OUT_NORM, ROW_POOL_SCALE, ROW_CONV_W = 0, 4, 8, 12, 14, 16
ROW_SSD_VEC, ROW_QK_GAIN, ROW_LOSS = 32, 33, 34


def _adamw_small(gathered, w, m, v, *, name):
    _, rows, n = gathered.shape

    def body(a_ref, w_ref, m_ref, v_ref, g_out, d_out, m_out, v_out):
        g = a_ref[0]
        for d in range(1, N_DEV):
            g = g + a_ref[d]
        row = lax.broadcasted_iota(jnp.int32, (rows, 1), 0)
        g = jnp.where(row == ROW_QK_GAIN, g + pltpu.roll(g, SB_HEAD_DIM, 1), g)
        g = jnp.where(row == ROW_LOSS, jnp.sum(g, axis=1, keepdims=True), g)
        g_out[...] = g
        delta, m_new, v_new = _adamw_math(w_ref[...], g, m_ref[...], v_ref[...])
        d_out[...] = delta
        m_out[...] = m_new
        v_out[...] = v_new

    out = jax.ShapeDtypeStruct((rows, n), F32)
    return pl.pallas_call(body, name=name, out_shape=[out, out, out, out])(gathered, w, m, v)


BIG_WEIGHTS = ("pool_in", "pool_group", "ssd_in", "ssd_out", "sb_qkv", "sb_out", "ffn_gate", "ffn_up", "ffn_down")
COLUMN_SHARDED = ("ssd_in", "sb_qkv", "ffn_gate", "ffn_up")
ROW_PAD = 512
WIRE_DTYPE = jnp.bfloat16


def _to_rows(name, shard, d):
    if name in COLUMN_SHARDED:
        shard = jnp.swapaxes(shard, -1, -2)
    return shard.reshape(-1, d)


def _from_rows(name, rows, shard_shape):
    if name in COLUMN_SHARDED:
        lead, k, n = shard_shape
        return jnp.swapaxes(rows.reshape(lead, n, k), -1, -2)
    return rows.reshape(shard_shape)


def _pad_rows(a, total):
    return jnp.pad(a, ((0, total - a.shape[0]),) + ((0, 0),) * (a.ndim - 1))


def _exact_bf16_rows(v, d):
    words = lax.bitcast_convert_type(v.reshape(-1), WIRE_DTYPE).reshape(-1)
    return _pad_rows(words, -(-words.shape[0] // d) * d).reshape(-1, d)


def _exact_f32(rows, count):
    words = rows.reshape(rows.shape[0], -1)[:, :2 * count].reshape(rows.shape[0], count, 2)
    return lax.bitcast_convert_type(words, F32)


def _device_blocks(name, full, lead, d):
    if name == "pool_group":
        j, g, dg, _ = full.shape
        return full.reshape(j, g, N_DEV, dg // N_DEV, dg).transpose(2, 0, 1, 3, 4).reshape(N_DEV, -1, d)
    per = full.shape[1] // N_DEV
    return full.reshape(lead, N_DEV, per, d).transpose(1, 0, 2, 3).reshape(N_DEV, lead * per, d)


def kernel(x, mix_norm, pool_in, pool_group, pool_scale, ssd_in, ssd_conv_w, ssd_conv_b, ssd_dt_bias, ssd_a_log, ssd_d, ssd_out_norm, ssd_out, sb_qkv, sb_q_norm, sb_k_norm, sb_out, ffn_norm, ffn_gate, ffn_up, ffn_down, loss_target, m_mix_norm, m_pool_in, m_pool_group, m_pool_scale, m_ssd_in, m_ssd_conv_w, m_ssd_conv_b, m_ssd_dt_bias, m_ssd_a_log, m_ssd_d, m_ssd_out_norm, m_ssd_out, m_sb_qkv, m_sb_q_norm, m_sb_k_norm, m_sb_out, m_ffn_norm, m_ffn_gate, m_ffn_up, m_ffn_down, v_mix_norm, v_pool_in, v_pool_group, v_pool_scale, v_ssd_in, v_ssd_conv_w, v_ssd_conv_b, v_ssd_dt_bias, v_ssd_a_log, v_ssd_d, v_ssd_out_norm, v_ssd_out, v_sb_qkv, v_sb_q_norm, v_sb_k_norm, v_sb_out, v_ffn_norm, v_ffn_gate, v_ffn_up, v_ffn_down):
    weights = dict(mix_norm=mix_norm, pool_in=pool_in, pool_group=pool_group, pool_scale=pool_scale, ssd_in=ssd_in,
                   ssd_conv_w=ssd_conv_w, ssd_conv_b=ssd_conv_b, ssd_dt_bias=ssd_dt_bias, ssd_a_log=ssd_a_log,
                   ssd_d=ssd_d, ssd_out_norm=ssd_out_norm, ssd_out=ssd_out, sb_qkv=sb_qkv, sb_q_norm=sb_q_norm,
                   sb_k_norm=sb_k_norm, sb_out=sb_out, ffn_norm=ffn_norm, ffn_gate=ffn_gate, ffn_up=ffn_up,
                   ffn_down=ffn_down)
    mom1 = dict(mix_norm=m_mix_norm, pool_in=m_pool_in, pool_group=m_pool_group, pool_scale=m_pool_scale,
                ssd_in=m_ssd_in, ssd_conv_w=m_ssd_conv_w, ssd_conv_b=m_ssd_conv_b, ssd_dt_bias=m_ssd_dt_bias,
                ssd_a_log=m_ssd_a_log, ssd_d=m_ssd_d, ssd_out_norm=m_ssd_out_norm, ssd_out=m_ssd_out,
                sb_qkv=m_sb_qkv, sb_q_norm=m_sb_q_norm, sb_k_norm=m_sb_k_norm, sb_out=m_sb_out,
                ffn_norm=m_ffn_norm, ffn_gate=m_ffn_gate, ffn_up=m_ffn_up, ffn_down=m_ffn_down)
    mom2 = dict(mix_norm=v_mix_norm, pool_in=v_pool_in, pool_group=v_pool_group, pool_scale=v_pool_scale,
                ssd_in=v_ssd_in, ssd_conv_w=v_ssd_conv_w, ssd_conv_b=v_ssd_conv_b, ssd_dt_bias=v_ssd_dt_bias,
                ssd_a_log=v_ssd_a_log, ssd_d=v_ssd_d, ssd_out_norm=v_ssd_out_norm, ssd_out=v_ssd_out,
                sb_qkv=v_sb_qkv, sb_q_norm=v_sb_q_norm, sb_k_norm=v_sb_k_norm, sb_out=v_sb_out,
                ffn_norm=v_ffn_norm, ffn_gate=v_ffn_gate, ffn_up=v_ffn_up, ffn_down=v_ffn_down)
    names = list(weights)
    depth, d = mix_norm.shape
    xs, ys, cs = _position()
    dev = 4 * xs + 2 * ys + cs
    chip = 2 * xs + ys

    seg = {}
    row = 0
    for name in BIG_WEIGHTS:
        n_rows = weights[name].size // d
        seg[name] = (row, n_rows)
        row += n_rows
    big_rows = row
    n_scale, n_convw = pool_scale.size, ssd_conv_w.size
    exact = jnp.concatenate([_exact_bf16_rows(pool_scale, d), _exact_bf16_rows(ssd_conv_w, d)], axis=0)
    scale_rows = _exact_bf16_rows(pool_scale, d).shape[0]
    packed_rows = -(-(big_rows + exact.shape[0]) // ROW_PAD) * ROW_PAD

    def pack(tree, dtype):
        return jnp.concatenate([_to_rows(n, tree[n], d).astype(dtype) for n in BIG_WEIGHTS], axis=0)

    w_wire = _pad_rows(jnp.concatenate([pack(weights, WIRE_DTYPE), exact], axis=0), packed_rows)
    gathered = _all_gather(w_wire, name="gather_weights", in_vmem=False)

    def seg_of(name):
        a, n = seg[name]
        return gathered[:, a:a + n]

    n_pool, n_ssd, n_sb = pool_in.shape[0], ssd_in.shape[0], sb_qkv.shape[0]
    assert n_ssd == 1 and n_sb == 1
    w_pool_in = seg_of("pool_in").reshape(N_DEV, n_pool, -1, d).transpose(1, 0, 2, 3).reshape(n_pool, d, d)
    grp = pool_group.shape
    w_pool_group = seg_of("pool_group").reshape(N_DEV, grp[0], grp[1], grp[2], grp[3]).transpose(1, 2, 0, 3, 4)
    w_pool_group = w_pool_group.reshape(grp[0], grp[1], grp[3], grp[3])
    w_ssd_in_t = seg_of("ssd_in").reshape(-1, d)
    w_ssd_out = seg_of("ssd_out").reshape(-1, d)
    w_sb_qkv_t = seg_of("sb_qkv").reshape(-1, d)
    w_sb_out = seg_of("sb_out").reshape(-1, d)
    hidden = ffn_down.shape[1] * N_DEV
    w_gate_t = seg_of("ffn_gate").reshape(N_DEV, depth, -1, d).transpose(1, 0, 2, 3).reshape(depth, hidden, d)
    w_up_t = seg_of("ffn_up").reshape(N_DEV, depth, -1, d).transpose(1, 0, 2, 3).reshape(depth, hidden, d)
    w_gu_t = jnp.concatenate([w_gate_t, w_up_t], axis=1)
    w_down = seg_of("ffn_down").reshape(N_DEV, depth, -1, d).transpose(1, 0, 2, 3).reshape(depth, hidden, d)
    exact_all = gathered[:, big_rows:big_rows + exact.shape[0]]
    scale_full = _exact_f32(exact_all[:, :scale_rows], n_scale).reshape(N_DEV, n_pool, -1)
    scale_full = scale_full.transpose(1, 0, 2).reshape(n_pool, d)
    convw_full = _exact_f32(exact_all[:, scale_rows:], n_convw).reshape(N_DEV, SSD_CONV, -1)
    convw_full = convw_full.transpose(1, 0, 2).reshape(SSD_CONV, -1)

    d_inner = w_ssd_out.shape[0]
    n_zx = w_ssd_in_t.shape[0] - ssd_dt_bias.shape[1]
    w_zx_t = w_ssd_in_t[:n_zx]
    w_dt_t = _pad_rows(w_ssd_in_t[n_zx:], LANES)
    n_ssd_heads = ssd_dt_bias.shape[1]
    par = jnp.zeros((SUBLANES, LANES), F32)
    par = par.at[0, :n_ssd_heads].set(ssd_dt_bias[0]).at[1, :n_ssd_heads].set(ssd_a_log[0])
    par = par.at[2, :n_ssd_heads].set(ssd_d[0])
    d_rep = jnp.repeat(ssd_d[0], SSD_HEAD_DIM)[None]
    qk_gains = jnp.zeros((SUBLANES, LANES), F32).at[0].set(jnp.tile(sb_q_norm[0], 2)).at[1].set(jnp.tile(sb_k_norm[0], 2))

    act = x[0]
    saved = []
    for i in range(depth):
        kind, j = i % 3, i // 3
        gain = mix_norm[i:i + 1]
        if kind == 0:
            act, s = _pool_mixer_fwd(act, gain, w_pool_in[j], w_pool_group[j], scale_full[j:j + 1], f"l{i}")
        elif kind == 1:
            act, s = _ssd_mixer_fwd(act, gain, w_zx_t, w_dt_t, convw_full, ssd_conv_b, par, d_rep, ssd_out_norm,
                                    w_ssd_out, f"l{i}")
        else:
            act, s = _sb_mixer_fwd(act, gain, w_sb_qkv_t, qk_gains, w_sb_out, f"l{i}")
        act, f = _ffn_fwd(act, ffn_norm[i:i + 1], w_gu_t[i], w_down[i], f"l{i}")
        saved.append((s, f))
    dact, loss_cols = _loss_head(act, loss_target[0], name="loss_head")

    g_mix_norm, g_ffn_norm = [None] * depth, [None] * depth
    g_pool_in, g_pool_group, g_pool_scale = [None] * n_pool, [None] * n_pool, [None] * n_pool
    g_gu_t, g_down = [None] * depth, [None] * depth
    for i in reversed(range(depth)):
        kind, j = i % 3, i // 3
        gain = mix_norm[i:i + 1]
        s, f = saved[i]
        dact, g_gu_t[i], g_down[i], g_ffn_norm[i] = _ffn_bwd(dact, f, ffn_norm[i:i + 1], w_gu_t[i], w_down[i], f"l{i}")
        if kind == 0:
            dact, g_pool_in[j], g_pool_group[j], g_pool_scale[j], g_mix_norm[i] = _pool_mixer_bwd(
                dact, s, gain, w_pool_in[j], w_pool_group[j], scale_full[j:j + 1], f"l{i}")
        elif kind == 1:
            (dact, g_zx_t, g_dt_t, g_conv_w, g_conv_b, g_par, g_out_norm, g_ssd_out,
             g_mix_norm[i]) = _ssd_mixer_bwd(dact, s, gain, w_zx_t, w_dt_t, convw_full, ssd_conv_b, par, d_rep,
                                             ssd_out_norm, w_ssd_out, f"l{i}")
        else:
            dact, g_qkv_t, g_qk_gains, g_sb_out, g_mix_norm[i] = _sb_mixer_bwd(
                dact, s, gain, w_sb_qkv_t, qk_gains, w_sb_out, f"l{i}")
    grad_x = dact[None]

    g_gu_t = jnp.stack(g_gu_t)
    full_grads = dict(
        pool_in=jnp.stack(g_pool_in), pool_group=jnp.stack(g_pool_group),
        ssd_in=jnp.concatenate([g_zx_t, g_dt_t[:n_ssd_heads]], axis=0)[None], ssd_out=g_ssd_out[None],
        sb_qkv=g_qkv_t[None], sb_out=g_sb_out[None],
        ffn_gate=g_gu_t[:, :hidden], ffn_up=g_gu_t[:, hidden:], ffn_down=jnp.stack(g_down))
    blocks = [_device_blocks(n, full_grads[n], full_grads[n].shape[0], d) for n in BIG_WEIGHTS]
    parts = jnp.concatenate(blocks, axis=1)
    parts = jnp.pad(parts, ((0, 0), (0, packed_rows - big_rows), (0, 0)))
    parts = parts.reshape(N_DEV // 2, 2, packed_rows, d).transpose(1, 0, 2, 3)
    from_sibling = _exchange_sibling(parts, name="reduce_sibling")
    chip_sums = _add_pairs(parts, from_sibling, cs.reshape(1).astype(jnp.int32), name="reduce_sibling_add")
    from_chips = _exchange_chips(chip_sums, name="reduce_chips")

    def pack_f32(tree):
        return _pad_rows(pack(tree, F32), packed_rows)

    big_out = _adamw_sharded(chip_sums, from_chips, chip.reshape(1).astype(jnp.int32), pack_f32(weights),
                             pack_f32(mom1), pack_f32(mom2), name="adamw_sharded")

    def small_pack(mix, ffn, conv_b, out_norm, scale, conv_w, vec, qk, loss=None):
        buf = jnp.zeros((SMALL_ROWS, d), F32)
        buf = buf.at[ROW_MIX_NORM:ROW_MIX_NORM + depth].set(mix).at[ROW_FFN_NORM:ROW_FFN_NORM + depth].set(ffn)
        buf = buf.at[ROW_CONV_B:ROW_CONV_B + conv_b.size // d].set(conv_b.reshape(-1, d))
        buf = buf.at[ROW_OUT_NORM:ROW_OUT_NORM + out_norm.size // d].set(out_norm.reshape(-1, d))
        buf = buf.at[ROW_POOL_SCALE:ROW_POOL_SCALE + n_pool].set(scale)
        buf = buf.at[ROW_CONV_W:ROW_CONV_W + conv_w.size // d].set(conv_w.reshape(-1, d))
        buf = buf.at[ROW_SSD_VEC].set(vec.reshape(-1)).at[ROW_QK_GAIN].set(qk.reshape(-1))
        if loss is not None:
            buf = buf.at[ROW_LOSS].set(loss.reshape(-1))
        return buf

    def small_params(tree):
        scale = lax.dynamic_update_slice(jnp.zeros((n_pool, d), F32), tree["pool_scale"],
                                         (0, dev * tree["pool_scale"].shape[1]))
        conv_w = lax.dynamic_update_slice(jnp.zeros(convw_full.shape, F32), tree["ssd_conv_w"][0],
                                          (0, dev * tree["ssd_conv_w"].shape[2]))
        vec = jnp.zeros((SUBLANES, LANES), F32)
        vec = vec.at[0, :n_ssd_heads].set(tree["ssd_dt_bias"][0]).at[1, :n_ssd_heads].set(tree["ssd_a_log"][0])
        vec = vec.at[2, :n_ssd_heads].set(tree["ssd_d"][0])
        qk = jnp.zeros((SUBLANES, LANES), F32)
        qk = qk.at[0, SB_HEAD_DIM:].set(tree["sb_q_norm"][0]).at[1, SB_HEAD_DIM:].set(tree["sb_k_norm"][0])
        return small_pack(tree["mix_norm"], tree["ffn_norm"], tree["ssd_conv_b"], tree["ssd_out_norm"], scale,
                          conv_w, vec, qk)

    small_partial = small_pack(jnp.concatenate(g_mix_norm, axis=0), jnp.concatenate(g_ffn_norm, axis=0), g_conv_b,
                               g_out_norm, jnp.concatenate(g_pool_scale, axis=0), g_conv_w, g_par, g_qk_gains,
                               loss_cols)
    small_all = _all_gather(small_partial, name="gather_small", in_vmem=True)
    small_out = _adamw_small(small_all, small_params(weights), small_params(mom1), small_params(mom2),
                             name="adamw_small")
    loss = small_out[0][ROW_LOSS, 0]

    def unpack(big, small):
        out = {}
        for name in BIG_WEIGHTS:
            a, n = seg[name]
            out[name] = _from_rows(name, big[a:a + n], weights[name].shape)
        out["mix_norm"] = small[ROW_MIX_NORM:ROW_MIX_NORM + depth]
        out["ffn_norm"] = small[ROW_FFN_NORM:ROW_FFN_NORM + depth]
        out["ssd_conv_b"] = small[ROW_CONV_B:ROW_CONV_B + ssd_conv_b.size // d].reshape(ssd_conv_b.shape)
        out["ssd_out_norm"] = small[ROW_OUT_NORM:ROW_OUT_NORM + ssd_out_norm.size // d].reshape(ssd_out_norm.shape)
        out["pool_scale"] = lax.dynamic_slice(small[ROW_POOL_SCALE:ROW_POOL_SCALE + n_pool],
                                              (0, dev * pool_scale.shape[1]), pool_scale.shape)
        conv_w = small[ROW_CONV_W:ROW_CONV_W + convw_full.size // d].reshape(convw_full.shape)
        out["ssd_conv_w"] = lax.dynamic_slice(conv_w, (0, dev * ssd_conv_w.shape[2]), ssd_conv_w.shape[1:])[None]
        vec = small[ROW_SSD_VEC].reshape(SUBLANES, LANES)
        out["ssd_dt_bias"], out["ssd_a_log"], out["ssd_d"] = (vec[r:r + 1, :n_ssd_heads] for r in range(3))
        qk = small[ROW_QK_GAIN].reshape(SUBLANES, LANES)
        out["sb_q_norm"], out["sb_k_norm"] = qk[0:1, SB_HEAD_DIM:], qk[1:2, SB_HEAD_DIM:]
        return [out[n] for n in names]

    results = [unpack(b, s) for b, s in zip(big_out, small_out)]
    return (loss, grad_x, *results[0], *results[1], *results[2], *results[3])
```

```python
import math

import jax
import jax.numpy as jnp
from jax import lax
from jax.experimental import pallas as pl
from jax.experimental.pallas import tpu as pltpu

F32 = jnp.float32
BF16 = jnp.bfloat16

N_DEV = 8
NORM_EPS = 1e-6
V7X_VMEM_LIMIT_BYTES = 48 * 1024 * 1024
LANES = 128
SUBLANES = 8

POOL_WINDOWS = (2, 4, 8, 16)
SSD_CHUNK = 256
SSD_HEAD_DIM = 64
SSD_STATE = 128
SSD_HEADS_PER_GROUP = 4
SSD_CONV = 4
SB_HEAD_DIM = 64
SB_BLOCK = 128

ADAM_LR = 0.001
ADAM_B1 = 0.9
ADAM_B2 = 0.999
ADAM_EPS = 1e-08
ADAM_WD = 0.01
ADAM_STEP = 10


def _params(*sem):
    return pltpu.CompilerParams(dimension_semantics=sem, vmem_limit_bytes=V7X_VMEM_LIMIT_BYTES)


def _tile(n, cap, mult):
    best = None
    for t in range(mult, min(n, cap) + 1, mult):
        if n % t == 0:
            best = t
    return best or n


def _load_slabs(ref, slabs):
    if not slabs:
        return ref[...]
    return jnp.concatenate([ref[p] for p in range(ref.shape[0])], axis=1)


def _matmul(a, b, mode, *, name, out_dtype=F32, resid=None, a_slabs=False, out_slabs=False,
            tm_cap=512, tn_cap=512, tk_cap=2048):
    if a_slabs:
        m, k = a.shape[1], a.shape[0] * LANES
    else:
        m, k = a.shape
    n = b.shape[1] if mode == "nn" else b.shape[0]
    assert (b.shape[0] if mode == "nn" else b.shape[1]) == k
    tm, tn, tk = _tile(m, tm_cap, SUBLANES), _tile(n, tn_cap, LANES), _tile(k, tk_cap, LANES)
    nk = k // tk
    dn = (((1,), (0,)), ((), ())) if mode == "nn" else (((1,), (1,)), ((), ()))
    has_resid = resid is not None

    def body(*refs):
        if has_resid:
            a_ref, b_ref, r_ref, o_ref, acc = refs
        else:
            a_ref, b_ref, o_ref, acc = refs
        kk = pl.program_id(2)

        @pl.when(kk == 0)
        def _():
            acc[...] = jnp.zeros_like(acc)

        acc[...] += lax.dot_general(_load_slabs(a_ref, a_slabs).astype(BF16), b_ref[...].astype(BF16), dn,
                                    preferred_element_type=F32)

        @pl.when(kk == nk - 1)
        def _():
            r = acc[...]
            if has_resid:
                r = r + r_ref[...]
            if out_slabs:
                for p in range(tn // LANES):
                    o_ref[p] = r[:, p * LANES:(p + 1) * LANES].astype(out_dtype)
            else:
                o_ref[...] = r.astype(out_dtype)

    b_spec = (pl.BlockSpec((tk, tn), lambda i, j, kk: (kk, j)) if mode == "nn"
              else pl.BlockSpec((tn, tk), lambda i, j, kk: (j, kk)))
    a_spec = (pl.BlockSpec((tk // LANES, tm, LANES), lambda i, j, kk: (kk, i, 0)) if a_slabs
              else pl.BlockSpec((tm, tk), lambda i, j, kk: (i, kk)))
    in_specs = [a_spec, b_spec]
    args = [a, b]
    if has_resid:
        in_specs.append(pl.BlockSpec((tm, tn), lambda i, j, kk: (i, j)))
        args.append(resid)
    if out_slabs:
        out_spec = pl.BlockSpec((tn // LANES, tm, LANES), lambda i, j, kk: (j, i, 0))
        out_shape = jax.ShapeDtypeStruct((n // LANES, m, LANES), out_dtype)
    else:
        out_spec = pl.BlockSpec((tm, tn), lambda i, j, kk: (i, j))
        out_shape = jax.ShapeDtypeStruct((m, n), out_dtype)
    return pl.pallas_call(
        body, name=name, grid=(m // tm, n // tn, nk),
        in_specs=in_specs, out_specs=out_spec, out_shape=out_shape,
        scratch_shapes=[pltpu.VMEM((tm, tn), F32)],
        compiler_params=_params("parallel", "parallel", "arbitrary"),
    )(*args)


def _matmul_tn(a, b, *, name, a_slabs=False, ta_cap=512, tb_cap=512, tr_cap=1024):
    if a_slabs:
        r, ka = a.shape[1], a.shape[0] * LANES
    else:
        r, ka = a.shape
    nb = b.shape[1]
    assert b.shape[0] == r
    ta, tb, tr = _tile(ka, ta_cap, LANES), _tile(nb, tb_cap, LANES), _tile(r, tr_cap, SUBLANES)

    def body(a_ref, b_ref, o_ref):
        @pl.when(pl.program_id(2) == 0)
        def _():
            o_ref[...] = jnp.zeros_like(o_ref)

        o_ref[...] += lax.dot_general(_load_slabs(a_ref, a_slabs).astype(BF16), b_ref[...].astype(BF16),
                                      (((0,), (0,)), ((), ())), preferred_element_type=F32)

    a_spec = (pl.BlockSpec((ta // LANES, tr, LANES), lambda i, j, kk: (i, kk, 0)) if a_slabs
              else pl.BlockSpec((tr, ta), lambda i, j, kk: (kk, i)))
    return pl.pallas_call(
        body, name=name, grid=(ka // ta, nb // tb, r // tr),
        in_specs=[a_spec, pl.BlockSpec((tr, tb), lambda i, j, kk: (kk, j))],
        out_specs=pl.BlockSpec((ta, tb), lambda i, j, kk: (i, j)),
        out_shape=jax.ShapeDtypeStruct((ka, nb), F32),
        compiler_params=_params("parallel", "parallel", "arbitrary"),
    )(a, b)


def _rms_fwd(x, gain, *, name):
    t, d = x.shape
    tm = _tile(t, 512, SUBLANES)

    def body(x_ref, g_ref, o_ref):
        xv = x_ref[...]
        r = lax.rsqrt(jnp.mean(xv * xv, axis=-1, keepdims=True) + NORM_EPS)
        o_ref[...] = (xv * r * g_ref[...]).astype(BF16)

    return pl.pallas_call(
        body, name=name, grid=(t // tm,),
        in_specs=[pl.BlockSpec((tm, d), lambda i: (i, 0)), pl.BlockSpec((1, d), lambda i: (0, 0))],
        out_specs=pl.BlockSpec((tm, d), lambda i: (i, 0)),
        out_shape=jax.ShapeDtypeStruct((t, d), BF16),
        compiler_params=_params("parallel"),
    )(x, gain)


def _rms_bwd(x, gain, dh, dres, *, name):
    t, d = x.shape
    tm = _tile(t, 512, SUBLANES)

    def body(x_ref, g_ref, dh_ref, dres_ref, dx_ref, dg_ref):
        @pl.when(pl.program_id(0) == 0)
        def _():
            dg_ref[...] = jnp.zeros_like(dg_ref)

        xv = x_ref[...]
        r = lax.rsqrt(jnp.mean(xv * xv, axis=-1, keepdims=True) + NORM_EPS)
        xhat = xv * r
        dhv = dh_ref[...]
        u = dhv * g_ref[...]
        dx_ref[...] = dres_ref[...] + r * (u - xhat * jnp.mean(u * xhat, axis=-1, keepdims=True))
        dg_ref[...] += jnp.sum(dhv * xhat, axis=0, keepdims=True)

    return pl.pallas_call(
        body, name=name, grid=(t // tm,),
        in_specs=[pl.BlockSpec((tm, d), lambda i: (i, 0)), pl.BlockSpec((1, d), lambda i: (0, 0)),
                  pl.BlockSpec((tm, d), lambda i: (i, 0)), pl.BlockSpec((tm, d), lambda i: (i, 0))],
        out_specs=[pl.BlockSpec((tm, d), lambda i: (i, 0)), pl.BlockSpec((1, d), lambda i: (0, 0))],
        out_shape=[jax.ShapeDtypeStruct((t, d), F32), jax.ShapeDtypeStruct((1, d), F32)],
        compiler_params=_params("arbitrary"),
    )(x, gain, dh, dres)


def _loss_head(y, target, *, name):
    t, d = y.shape
    tm = _tile(t, 512, SUBLANES)

    def body(y_ref, t_ref, dy_ref, l_ref):
        @pl.when(pl.program_id(0) == 0)
        def _():
            l_ref[...] = jnp.zeros_like(l_ref)

        e = y_ref[...] - t_ref[...]
        dy_ref[...] = e * (1.0 / d)
        l_ref[...] += jnp.sum(e * e, axis=0, keepdims=True) * (0.5 / d)

    return pl.pallas_call(
        body, name=name, grid=(t // tm,),
        in_specs=[pl.BlockSpec((tm, d), lambda i: (i, 0)), pl.BlockSpec((tm, d), lambda i: (i, 0))],
        out_specs=[pl.BlockSpec((tm, d), lambda i: (i, 0)), pl.BlockSpec((1, d), lambda i: (0, 0))],
        out_shape=[jax.ShapeDtypeStruct((t, d), F32), jax.ShapeDtypeStruct((1, d), F32)],
        compiler_params=_params("arbitrary"),
    )(y, target)


def _sigmoid(v):
    return 0.5 * jnp.tanh(0.5 * v) + 0.5


def _swiglu_fwd(ab, *, name):
    t, f2 = ab.shape
    f = f2 // 2
    tm, tn = _tile(t, 256, SUBLANES), _tile(f, 1536, LANES)
    nb = f // tn

    def body(a_ref, b_ref, o_ref):
        av = a_ref[...]
        o_ref[...] = (av * _sigmoid(av) * b_ref[...]).astype(BF16)

    return pl.pallas_call(
        body, name=name, grid=(t // tm, nb),
        in_specs=[pl.BlockSpec((tm, tn), lambda i, j: (i, j)), pl.BlockSpec((tm, tn), lambda i, j: (i, j + nb))],
        out_specs=pl.BlockSpec((tm, tn), lambda i, j: (i, j)),
        out_shape=jax.ShapeDtypeStruct((t, f), BF16),
        compiler_params=_params("parallel", "parallel"),
    )(ab, ab)


def _swiglu_bwd(ds, ab, *, name):
    t, f2 = ab.shape
    f = f2 // 2
    tm, tn = _tile(t, 256, SUBLANES), _tile(f, 1536, LANES)
    nb = f // tn

    def body(ds_ref, a_ref, b_ref, o_ref):
        av = a_ref[...]
        sg = _sigmoid(av)
        dsv = ds_ref[...]

        @pl.when(pl.program_id(2) == 0)
        def _():
            o_ref[...] = (dsv * b_ref[...] * (sg * (1.0 + av * (1.0 - sg)))).astype(BF16)

        @pl.when(pl.program_id(2) == 1)
        def _():
            o_ref[...] = (dsv * av * sg).astype(BF16)

    return pl.pallas_call(
        body, name=name, grid=(t // tm, nb, 2),
        in_specs=[pl.BlockSpec((tm, tn), lambda i, j, p: (i, j)), pl.BlockSpec((tm, tn), lambda i, j, p: (i, j)),
                  pl.BlockSpec((tm, tn), lambda i, j, p: (i, j + nb))],
        out_specs=pl.BlockSpec((tm, tn), lambda i, j, p: (i, j + p * nb)),
        out_shape=jax.ShapeDtypeStruct((t, f2), BF16),
        compiler_params=_params("parallel", "parallel", "arbitrary"),
    )(ds, ab, ab)


def _ffn_fwd(x, gain, w_gu_t, w_down, tag):
    h = _rms_fwd(x, gain, name=f"ffn_norm_{tag}")
    ab = _matmul(h, w_gu_t, "nt", name=f"ffn_gu_{tag}")
    s = _swiglu_fwd(ab, name=f"ffn_act_{tag}")
    x_new = _matmul(s, w_down, "nn", resid=x, name=f"ffn_down_{tag}")
    return x_new, (x, h, ab, s)


def _ffn_bwd(dx, saved, gain, w_gu_t, w_down, tag):
    x, h, ab, s = saved
    ds = _matmul(dx, w_down, "nt", name=f"ffn_dact_{tag}")
    dw_down = _matmul_tn(s, dx, name=f"ffn_dwdown_{tag}")
    dab = _swiglu_bwd(ds, ab, name=f"ffn_dgu_{tag}")
    dw_gu_t = _matmul_tn(dab, h, name=f"ffn_dwgu_{tag}")
    dh = _matmul(dab, w_gu_t, "nn", name=f"ffn_dh_{tag}")
    dx_in, dgain = _rms_bwd(x, gain, dh, dx, name=f"ffn_dnorm_{tag}")
    return dx_in, dw_gu_t, dw_down, dgain


POOL_HALO = 16


def _shift_rows(v, k):
    n = v.shape[0]
    return pltpu.roll(v, k % n, 0)


def _window_sum(v, w, direction):
    k = 1
    while k < w:
        v = v + _shift_rows(v, direction * k)
        k *= 2
    return v


def _pool_fwd(u, x, w_group, scale, *, name):
    t, d = u.shape
    ng, dg = w_group.shape[0], w_group.shape[1]
    tm = _tile(t, 512, POOL_HALO)
    hb = tm // POOL_HALO

    def body(u_ref, halo_ref, x_ref, w_ref, s_ref, xo_ref, p_ref, y_ref):
        i, g = pl.program_id(0), pl.program_id(1)
        halo = jnp.where(i > 0, halo_ref[...], 0.0)
        ext = jnp.concatenate([halo, u_ref[...]], axis=0)
        pos = i * tm + lax.broadcasted_iota(jnp.int32, (tm, 1), 0)
        for gi, win in enumerate(POOL_WINDOWS):
            @pl.when(g == gi)
            def _(win=win):
                tot = _window_sum(ext, win, 1)[POOL_HALO:]
                cnt = jnp.minimum(pos + 1, win).astype(F32)
                p = (tot / cnt - u_ref[...]).astype(BF16)
                p_ref[...] = p
                y = jnp.dot(p, w_ref[...].astype(BF16), preferred_element_type=F32)
                y_ref[...] = y
                xo_ref[...] = x_ref[...] + y * s_ref[...]

    blk = pl.BlockSpec((tm, dg), lambda i, g: (i, g))
    return pl.pallas_call(
        body, name=name, grid=(t // tm, ng),
        in_specs=[blk, pl.BlockSpec((POOL_HALO, dg), lambda i, g: (jnp.maximum(i * hb - 1, 0), g)), blk,
                  pl.BlockSpec((None, dg, dg), lambda i, g: (g, 0, 0)), pl.BlockSpec((1, dg), lambda i, g: (0, g))],
        out_specs=[blk, blk, blk],
        out_shape=[jax.ShapeDtypeStruct((t, d), F32), jax.ShapeDtypeStruct((t, d), BF16),
                   jax.ShapeDtypeStruct((t, d), F32)],
        compiler_params=_params("parallel", "parallel"),
    )(u, u, x, w_group, scale)


def _pool_bwd(dx, p, y_pre, w_group, scale, *, name):
    t, d = dx.shape
    ng, dg = w_group.shape[0], w_group.shape[1]
    tm = _tile(t, 512, POOL_HALO)
    hb = tm // POOL_HALO
    nt = t // tm

    def body(dx_ref, nxt_ref, p_ref, y_ref, w_ref, s_ref, du_ref, dw_ref, ds_ref):
        g, i = pl.program_id(0), pl.program_id(1)

        @pl.when(i == 0)
        def _():
            dw_ref[...] = jnp.zeros_like(dw_ref)
            ds_ref[...] = jnp.zeros_like(ds_ref)

        dxv = dx_ref[...]
        ds_ref[...] += jnp.sum(dxv * y_ref[...], axis=0, keepdims=True)
        nxt = jnp.where(i < nt - 1, nxt_ref[...], 0.0)
        dyp = (jnp.concatenate([dxv, nxt], axis=0) * s_ref[...]).astype(BF16)
        dw_ref[...] += lax.dot_general(p_ref[...], dyp[:tm], (((0,), (0,)), ((), ())), preferred_element_type=F32)
        dp = lax.dot_general(dyp, w_ref[...].astype(BF16), (((1,), (1,)), ((), ())), preferred_element_type=F32)
        pos = i * tm + lax.broadcasted_iota(jnp.int32, (tm + POOL_HALO, 1), 0)
        for gi, win in enumerate(POOL_WINDOWS):
            @pl.when(g == gi)
            def _(win=win):
                q = dp / jnp.minimum(pos + 1, win).astype(F32)
                du_ref[...] = (_window_sum(q, win, -1)[:tm] - dp[:tm]).astype(BF16)

    blk = pl.BlockSpec((tm, dg), lambda g, i: (i, g))
    return pl.pallas_call(
        body, name=name, grid=(ng, nt),
        in_specs=[blk, pl.BlockSpec((POOL_HALO, dg), lambda g, i: (jnp.minimum((i + 1) * hb, t // POOL_HALO - 1), g)),
                  blk, blk, pl.BlockSpec((None, dg, dg), lambda g, i: (g, 0, 0)),
                  pl.BlockSpec((1, dg), lambda g, i: (0, g))],
        out_specs=[blk, pl.BlockSpec((None, dg, dg), lambda g, i: (g, 0, 0)), pl.BlockSpec((1, dg), lambda g, i: (0, g))],
        out_shape=[jax.ShapeDtypeStruct((t, d), BF16), jax.ShapeDtypeStruct((ng, dg, dg), F32),
                   jax.ShapeDtypeStruct((1, d), F32)],
        compiler_params=_params("parallel", "arbitrary"),
    )(dx, dx, p, y_pre, w_group, scale)


def _pool_mixer_fwd(x, gain, w_in, w_group, scale, tag):
    h = _rms_fwd(x, gain, name=f"pool_norm_{tag}")
    u = _matmul(h, w_in, "nn", name=f"pool_in_{tag}")
    x_new, p, y_pre = _pool_fwd(u, x, w_group, scale, name=f"pool_mix_{tag}")
    return x_new, (x, h, p, y_pre)


def _pool_mixer_bwd(dx, saved, gain, w_in, w_group, scale, tag):
    x, h, p, y_pre = saved
    du, dw_group, dscale = _pool_bwd(dx, p, y_pre, w_group, scale, name=f"pool_dmix_{tag}")
    dw_in = _matmul_tn(h, du, name=f"pool_dwin_{tag}")
    dh = _matmul(du, w_in, "nt", name=f"pool_dh_{tag}")
    dx_in, dgain = _rms_bwd(x, gain, dh, dx, name=f"pool_dnorm_{tag}")
    return dx_in, dw_in, dw_group, dscale, dgain


CONV_HALO = 8
HIGHEST = lax.Precision.HIGHEST
NEG_BIG = -1e30


def _softplus(v):
    return jnp.maximum(v, 0.0) + jnp.log(1.0 + jnp.exp(-jnp.abs(v)))


def _dot_exact(a, b):
    return jnp.dot(a, b, precision=HIGHEST, preferred_element_type=F32)


def _conv_taps(ext, w_ref, off, rows):
    acc = None
    for k in range(SSD_CONV):
        shift = SSD_CONV - 1 - k
        v = (_shift_rows(ext, shift) if shift else ext)[off:off + rows] * w_ref[k:k + 1, :]
        acc = v if acc is None else acc + v
    return acc


def _ssd_conv_fwd(zx, conv_w, conv_b, col0, *, name):
    t = zx.shape[0]
    c = conv_w.shape[1]
    tm, tc = _tile(t, 512, CONV_HALO), _tile(c, 512, LANES)
    hb, cb0 = tm // CONV_HALO, col0 // tc
    assert col0 % tc == 0

    def body(x_ref, halo_ref, w_ref, b_ref, o_ref):
        halo = jnp.where(pl.program_id(0) > 0, halo_ref[...], 0.0)
        ext = jnp.concatenate([halo, x_ref[...]], axis=0)
        pre = _conv_taps(ext, w_ref, CONV_HALO, tm) + b_ref[...]
        o_ref[...] = pre * _sigmoid(pre)

    return pl.pallas_call(
        body, name=name, grid=(t // tm, c // tc),
        in_specs=[pl.BlockSpec((tm, tc), lambda i, j: (i, j + cb0)),
                  pl.BlockSpec((CONV_HALO, tc), lambda i, j: (jnp.maximum(i * hb - 1, 0), j + cb0)),
                  pl.BlockSpec((SSD_CONV, tc), lambda i, j: (0, j)), pl.BlockSpec((1, tc), lambda i, j: (0, j))],
        out_specs=pl.BlockSpec((tm, tc), lambda i, j: (i, j)),
        out_shape=jax.ShapeDtypeStruct((t, c), F32),
        compiler_params=_params("parallel", "parallel"),
    )(zx, zx, conv_w, conv_b)


def _ssd_conv_bwd(dxa, zx, conv_w, conv_b, col0, *, name):
    t = zx.shape[0]
    c = conv_w.shape[1]
    tm, tc = _tile(t, 512, CONV_HALO), _tile(c, 512, LANES)
    hb, cb0, nt = tm // CONV_HALO, col0 // tc, t // tm
    last_halo = t // CONV_HALO - 1

    def body(x_ref, prev_ref, nxt_ref, d_ref, dnxt_ref, w_ref, b_ref, dx_ref, dw_ref, db_ref):
        i = pl.program_id(1)

        @pl.when(i == 0)
        def _():
            dw_ref[...] = jnp.zeros_like(dw_ref)
            db_ref[...] = jnp.zeros_like(db_ref)

        prev = jnp.where(i > 0, prev_ref[...], 0.0)
        has_next = i < nt - 1
        ext = jnp.concatenate([prev, x_ref[...], jnp.where(has_next, nxt_ref[...], 0.0)], axis=0)
        pre = _conv_taps(ext, w_ref, CONV_HALO, tm + CONV_HALO) + b_ref[...]
        sg = _sigmoid(pre)
        dact = jnp.concatenate([d_ref[...], jnp.where(has_next, dnxt_ref[...], 0.0)], axis=0)
        dpre = dact * (sg * (1.0 + pre * (1.0 - sg)))
        db_ref[...] += jnp.sum(dpre[:tm], axis=0, keepdims=True)
        acc = None
        for k in range(SSD_CONV):
            shift = SSD_CONV - 1 - k
            src = (_shift_rows(ext, shift) if shift else ext)[CONV_HALO:CONV_HALO + tm]
            dw_ref[k:k + 1, :] += jnp.sum(dpre[:tm] * src, axis=0, keepdims=True)
            v = (_shift_rows(dpre, -shift) if shift else dpre)[:tm] * w_ref[k:k + 1, :]
            acc = v if acc is None else acc + v
        dx_ref[...] = acc.astype(BF16)

    main = lambda j, i: (i, j + cb0)
    return pl.pallas_call(
        body, name=name, grid=(c // tc, nt),
        in_specs=[pl.BlockSpec((tm, tc), main),
                  pl.BlockSpec((CONV_HALO, tc), lambda j, i: (jnp.maximum(i * hb - 1, 0), j + cb0)),
                  pl.BlockSpec((CONV_HALO, tc), lambda j, i: (jnp.minimum((i + 1) * hb, last_halo), j + cb0)),
                  pl.BlockSpec((tm, tc), lambda j, i: (i, j)),
                  pl.BlockSpec((CONV_HALO, tc), lambda j, i: (jnp.minimum((i + 1) * hb, last_halo), j)),
                  pl.BlockSpec((SSD_CONV, tc), lambda j, i: (0, j)), pl.BlockSpec((1, tc), lambda j, i: (0, j))],
        out_specs=[pl.BlockSpec((tm, tc), lambda j, i: (i, j)), pl.BlockSpec((SSD_CONV, tc), lambda j, i: (0, j)),
                   pl.BlockSpec((1, tc), lambda j, i: (0, j))],
        out_shape=[jax.ShapeDtypeStruct((t, c), BF16), jax.ShapeDtypeStruct((SSD_CONV, c), F32),
                   jax.ShapeDtypeStruct((1, c), F32)],
        compiler_params=_params("parallel", "arbitrary"),
    )(zx, zx, zx, dxa, dxa, conv_w, conv_b)


def _head_select(g, transposed):
    row = lax.broadcasted_iota(jnp.int32, (LANES, LANES), 0)
    col = lax.broadcasted_iota(jnp.int32, (LANES, LANES), 1)
    if transposed:
        row, col = col, row
    return ((row == g * SSD_HEADS_PER_GROUP + col) & (col < SSD_HEADS_PER_GROUP)).astype(F32)


def _ssd_chunk_common(dtp_ref, par_ref, g):
    ell = SSD_CHUNK
    dt_all = _softplus(dtp_ref[...] + par_ref[0:1, :])
    a_all = -jnp.exp(par_ref[1:2, :])
    sel = _head_select(g, False)
    dtg = _dot_exact(dt_all, sel)
    dag = _dot_exact(dt_all * a_all, sel)
    row = lax.broadcasted_iota(jnp.int32, (ell, ell), 0)
    col = lax.broadcasted_iota(jnp.int32, (ell, ell), 1)
    causal = row >= col
    acum = _dot_exact(causal.astype(F32), dag)
    return dt_all, a_all, dtg, acum, acum.T, causal


def _ssd_scan_fwd(xa, dtp, par, n_groups, *, name):
    t = xa.shape[0]
    ell, hd, hpg, ns = SSD_CHUNK, SSD_HEAD_DIM, SSD_HEADS_PER_GROUP, SSD_STATE
    gw = hpg * hd
    nc = t // ell
    b_blk0, c_blk0 = n_groups * gw // ns, n_groups * gw // ns + n_groups

    def body(xs_ref, b_ref, c_ref, dtp_ref, par_ref, y_ref, sin_ref, st):
        c, g = pl.program_id(0), pl.program_id(1)

        @pl.when(c == 0)
        def _():
            for hh in range(hpg):
                st[g * hpg + hh] = jnp.zeros((hd, ns), F32)

        _, _, dtg, acum, acum_t, causal = _ssd_chunk_common(dtp_ref, par_ref, g)
        bb, cc = b_ref[...].astype(BF16), c_ref[...].astype(BF16)
        cb = lax.dot_general(cc, bb, (((1,), (1,)), ((), ())), preferred_element_type=F32)
        for hh in range(hpg):
            lanes = slice(hh * hd, (hh + 1) * hd)
            col_a, row_a = acum[:, hh:hh + 1], acum_t[hh:hh + 1, :]
            decay = jnp.exp(jnp.where(causal, col_a - row_a, NEG_BIG))
            xdt = xs_ref[:, lanes] * dtg[:, hh:hh + 1]
            s_h = st[g * hpg + hh]
            sin_ref[lanes, :] = s_h
            y = jnp.dot((cb * decay).astype(BF16), xdt.astype(BF16), preferred_element_type=F32)
            y += jnp.exp(col_a) * lax.dot_general(cc, s_h.astype(BF16), (((1,), (1,)), ((), ())),
                                                  preferred_element_type=F32)
            y_ref[:, lanes] = y
            a_last = acum[ell - 1:ell, hh:hh + 1]
            w = jnp.exp(a_last - col_a)
            st[g * hpg + hh] = jnp.exp(a_last) * s_h + lax.dot_general(
                (xdt * w).astype(BF16), bb, (((0,), (0,)), ((), ())), preferred_element_type=F32)

    return pl.pallas_call(
        body, name=name, grid=(nc, n_groups),
        in_specs=[pl.BlockSpec((ell, gw), lambda c, g: (c, g)),
                  pl.BlockSpec((ell, ns), lambda c, g: (c, b_blk0 + g)),
                  pl.BlockSpec((ell, ns), lambda c, g: (c, c_blk0 + g)),
                  pl.BlockSpec((ell, LANES), lambda c, g: (c, 0)),
                  pl.BlockSpec((SUBLANES, LANES), lambda c, g: (0, 0))],
        out_specs=[pl.BlockSpec((ell, gw), lambda c, g: (c, g)),
                   pl.BlockSpec((None, None, gw, ns), lambda c, g: (c, g, 0, 0))],
        out_shape=[jax.ShapeDtypeStruct((t, n_groups * gw), F32),
                   jax.ShapeDtypeStruct((nc, n_groups, gw, ns), F32)],
        scratch_shapes=[pltpu.VMEM((n_groups * hpg, hd, ns), F32)],
        compiler_params=_params("arbitrary", "arbitrary"),
    )(xa, xa, xa, dtp, par)


def _ssd_scan_bwd(dy, xa, dtp, par, s_in, n_groups, *, name):
    t = xa.shape[0]
    ell, hd, hpg, ns = SSD_CHUNK, SSD_HEAD_DIM, SSD_HEADS_PER_GROUP, SSD_STATE
    gw = hpg * hd
    nc = t // ell
    b_blk0, c_blk0 = n_groups * gw // ns, n_groups * gw // ns + n_groups
    nt_dims = (((1,), (1,)), ((), ()))
    tn_dims = (((0,), (0,)), ((), ()))

    def body(dy_ref, xs_ref, b_ref, c_ref, dtp_ref, par_ref, sin_ref,
             dxs_ref, db_ref, dc_ref, ddtp_ref, dpar_ref, dst):
        i, g = pl.program_id(0), pl.program_id(1)

        @pl.when(i == 0)
        def _():
            for hh in range(hpg):
                dst[g * hpg + hh] = jnp.zeros((hd, ns), F32)

        @pl.when((i == 0) & (g == 0))
        def _():
            dpar_ref[...] = jnp.zeros_like(dpar_ref)

        @pl.when(g == 0)
        def _():
            ddtp_ref[...] = jnp.zeros_like(ddtp_ref)

        dt_all, a_all, dtg, acum, acum_t, causal = _ssd_chunk_common(dtp_ref, par_ref, g)
        sel_t = _head_select(g, True)
        d_g = _dot_exact(jnp.broadcast_to(par_ref[2:3, :], (SUBLANES, LANES)), _head_select(g, False))
        bf, cf = b_ref[...], c_ref[...]
        bb, cc = bf.astype(BF16), cf.astype(BF16)
        cb = lax.dot_general(cc, bb, nt_dims, preferred_element_type=F32)
        lane = lax.broadcasted_iota(jnp.int32, (1, LANES), 1)
        ones = jnp.ones((ell, LANES), F32)
        dcb = jnp.zeros((ell, ell), F32)
        dc = jnp.zeros((ell, ns), F32)
        db = jnp.zeros((ell, ns), F32)
        dacum = jnp.zeros((ell, LANES), F32)
        xsum = jnp.zeros((ell, LANES), F32)
        dsum = jnp.zeros((1, LANES), F32)
        for hh in range(hpg):
            lanes = slice(hh * hd, (hh + 1) * hd)
            onehot = (lane == hh).astype(F32)
            col_a, row_a = acum[:, hh:hh + 1], acum_t[hh:hh + 1, :]
            decay = jnp.exp(jnp.where(causal, col_a - row_a, NEG_BIG))
            e_col = jnp.exp(col_a)
            a_last = acum[ell - 1:ell, hh:hh + 1]
            w = jnp.exp(a_last - col_a)
            e_last = jnp.exp(a_last)
            xs_h, dy_h = xs_ref[:, lanes], dy_ref[:, lanes]
            dt_h = dtg[:, hh:hh + 1]
            xdt = xs_h * dt_h
            xdt_b, dy_b = xdt.astype(BF16), dy_h.astype(BF16)
            s_h, ds_h = sin_ref[lanes, :], dst[g * hpg + hh]
            m = cb * decay
            dm = lax.dot_general(dy_b, xdt_b, nt_dims, preferred_element_type=F32)
            dcb += dm * decay
            gd = dm * m
            dac = jnp.sum(gd, axis=1, keepdims=True) - lax.dot_general(gd, ones, tn_dims, precision=HIGHEST,
                                                                       preferred_element_type=F32)[:, 0:1]
            b_ds = lax.dot_general(bb, ds_h.astype(BF16), nt_dims, preferred_element_type=F32)
            dxdt = lax.dot_general(m.astype(BF16), dy_b, tn_dims, preferred_element_type=F32) + w * b_ds
            c_s = lax.dot_general(cc, s_h.astype(BF16), nt_dims, preferred_element_type=F32)
            dc += e_col * jnp.dot(dy_b, s_h.astype(BF16), preferred_element_type=F32)
            db += w * jnp.dot(xdt_b, ds_h.astype(BF16), preferred_element_type=F32)
            dac += jnp.sum(dy_h * c_s, axis=1, keepdims=True) * e_col
            q = jnp.sum(xdt * b_ds, axis=1, keepdims=True) * w
            dac -= q
            d_last = jnp.sum(q, axis=0, keepdims=True) + e_last * jnp.sum(
                jnp.sum(s_h * ds_h, axis=1, keepdims=True), axis=0, keepdims=True)
            is_last = lax.broadcasted_iota(jnp.int32, (ell, 1), 0) == ell - 1
            dac += jnp.where(is_last, d_last, 0.0)
            dacum += dac * onehot
            dst[g * hpg + hh] = e_last * ds_h + lax.dot_general((dy_h * e_col).astype(BF16), cc, tn_dims,
                                                                preferred_element_type=F32)
            d_h = d_g[0:1, hh:hh + 1]
            dxs_ref[:, lanes] = dxdt * dt_h + dy_h * d_h
            xsum += jnp.sum(dxdt * xs_h, axis=1, keepdims=True) * onehot
            dsum += jnp.sum(jnp.sum(dy_h * xs_h, axis=1, keepdims=True), axis=0, keepdims=True) * onehot
        dc_ref[...] = dc + jnp.dot(dcb.astype(BF16), bb, preferred_element_type=F32)
        db_ref[...] = db + lax.dot_general(dcb.astype(BF16), cc, tn_dims, preferred_element_type=F32)
        row = lax.broadcasted_iota(jnp.int32, (ell, ell), 0)
        col = lax.broadcasted_iota(jnp.int32, (ell, ell), 1)
        dda = _dot_exact((col >= row).astype(F32), dacum)
        a_g = _dot_exact(jnp.broadcast_to(a_all, (SUBLANES, LANES)), _head_select(g, False))[0:1, :]
        ddt_g = xsum + dda * a_g
        ddt_all = _dot_exact(ddt_g, sel_t)
        z = dtp_ref[...] + par_ref[0:1, :]
        ddtp = ddt_all * _sigmoid(z)
        ddtp_ref[...] += ddtp
        da_all = jnp.sum(_dot_exact(dda * dtg, sel_t), axis=0, keepdims=True) * a_all
        dd_all = _dot_exact(jnp.broadcast_to(dsum, (SUBLANES, LANES)), sel_t)[0:1, :]
        dpar_ref[0:1, :] += jnp.sum(ddtp, axis=0, keepdims=True)
        dpar_ref[1:2, :] += da_all
        dpar_ref[2:3, :] += dd_all

    rev = lambda i: nc - 1 - i
    return pl.pallas_call(
        body, name=name, grid=(nc, n_groups),
        in_specs=[pl.BlockSpec((ell, gw), lambda i, g: (rev(i), g)),
                  pl.BlockSpec((ell, gw), lambda i, g: (rev(i), g)),
                  pl.BlockSpec((ell, ns), lambda i, g: (rev(i), b_blk0 + g)),
                  pl.BlockSpec((ell, ns), lambda i, g: (rev(i), c_blk0 + g)),
                  pl.BlockSpec((ell, LANES), lambda i, g: (rev(i), 0)),
                  pl.BlockSpec((SUBLANES, LANES), lambda i, g: (0, 0)),
                  pl.BlockSpec((None, None, gw, ns), lambda i, g: (rev(i), g, 0, 0))],
        out_specs=[pl.BlockSpec((ell, gw), lambda i, g: (rev(i), g)),
                   pl.BlockSpec((ell, ns), lambda i, g: (rev(i), g)),
                   pl.BlockSpec((ell, ns), lambda i, g: (rev(i), g)),
                   pl.BlockSpec((ell, LANES), lambda i, g: (rev(i), 0)),
                   pl.BlockSpec((SUBLANES, LANES), lambda i, g: (0, 0))],
        out_shape=[jax.ShapeDtypeStruct((t, n_groups * gw), F32), jax.ShapeDtypeStruct((t, n_groups * ns), F32),
                   jax.ShapeDtypeStruct((t, n_groups * ns), F32), jax.ShapeDtypeStruct((t, LANES), F32),
                   jax.ShapeDtypeStruct((SUBLANES, LANES), F32)],
        scratch_shapes=[pltpu.VMEM((n_groups * hpg, hd, ns), F32)],
        compiler_params=_params("arbitrary", "arbitrary"),
    )(dy, xa, xa, xa, dtp, par, s_in)


def _ssd_gate_fwd(y, xa, zx, d_rep, out_norm, *, name):
    t, di = y.shape
    gw = SSD_HEADS_PER_GROUP * SSD_HEAD_DIM
    tm = _tile(t, 512, SUBLANES)

    def body(y_ref, xs_ref, z_ref, d_ref, n_ref, o_ref):
        zv = z_ref[...]
        gt = (y_ref[...] + d_ref[...] * xs_ref[...]) * (zv * _sigmoid(zv))
        r = lax.rsqrt(jnp.mean(gt * gt, axis=-1, keepdims=True) + NORM_EPS)
        o_ref[...] = (gt * r * n_ref[...]).astype(BF16)

    blk = pl.BlockSpec((tm, gw), lambda i, g: (i, g))
    vec = pl.BlockSpec((1, gw), lambda i, g: (0, g))
    return pl.pallas_call(
        body, name=name, grid=(t // tm, di // gw),
        in_specs=[blk, blk, blk, vec, vec], out_specs=blk,
        out_shape=jax.ShapeDtypeStruct((t, di), BF16),
        compiler_params=_params("parallel", "parallel"),
    )(y, xa, zx, d_rep, out_norm)


def _ssd_gate_bwd(dgn, y, xa, zx, d_rep, out_norm, *, name):
    t, di = y.shape
    gw = SSD_HEADS_PER_GROUP * SSD_HEAD_DIM
    tm = _tile(t, 512, SUBLANES)

    def body(dg_ref, y_ref, xs_ref, z_ref, d_ref, n_ref, dy_ref, dz_ref, dn_ref):
        @pl.when(pl.program_id(1) == 0)
        def _():
            dn_ref[...] = jnp.zeros_like(dn_ref)

        zv = z_ref[...]
        sg = _sigmoid(zv)
        sz = zv * sg
        y2 = y_ref[...] + d_ref[...] * xs_ref[...]
        gt = y2 * sz
        r = lax.rsqrt(jnp.mean(gt * gt, axis=-1, keepdims=True) + NORM_EPS)
        ghat = gt * r
        dgv = dg_ref[...]
        dn_ref[...] += jnp.sum(dgv * ghat, axis=0, keepdims=True)
        u = dgv * n_ref[...]
        dgt = r * (u - ghat * jnp.mean(u * ghat, axis=-1, keepdims=True))
        dy_ref[...] = dgt * sz
        dz_ref[...] = (dgt * y2 * (sg * (1.0 + zv * (1.0 - sg)))).astype(BF16)

    blk = pl.BlockSpec((tm, gw), lambda g, i: (i, g))
    vec = pl.BlockSpec((1, gw), lambda g, i: (0, g))
    return pl.pallas_call(
        body, name=name, grid=(di // gw, t // tm),
        in_specs=[blk, blk, blk, blk, vec, vec], out_specs=[blk, blk, vec],
        out_shape=[jax.ShapeDtypeStruct((t, di), F32), jax.ShapeDtypeStruct((t, di), BF16),
                   jax.ShapeDtypeStruct((1, di), F32)],
        compiler_params=_params("parallel", "arbitrary"),
    )(dgn, y, xa, zx, d_rep, out_norm)


def _ssd_mixer_fwd(x, gain, w_zx_t, w_dt_t, conv_w, conv_b, par, d_rep, out_norm, w_out, tag):
    di = w_out.shape[0]
    n_groups = di // (SSD_HEADS_PER_GROUP * SSD_HEAD_DIM)
    h = _rms_fwd(x, gain, name=f"ssd_norm_{tag}")
    zx = _matmul(h, w_zx_t, "nt", name=f"ssd_in_{tag}")
    dtp = _matmul(h, w_dt_t, "nt", name=f"ssd_dt_{tag}")
    xa = _ssd_conv_fwd(zx, conv_w, conv_b, di, name=f"ssd_conv_{tag}")
    y, s_in = _ssd_scan_fwd(xa, dtp, par, n_groups, name=f"ssd_scan_{tag}")
    gn = _ssd_gate_fwd(y, xa, zx, d_rep, out_norm, name=f"ssd_gate_{tag}")
    x_new = _matmul(gn, w_out, "nn", resid=x, name=f"ssd_out_{tag}")
    return x_new, (x, h, zx, dtp, xa, y, s_in, gn)


def _ssd_mixer_bwd(dx, saved, gain, w_zx_t, w_dt_t, conv_w, conv_b, par, d_rep, out_norm, w_out, tag):
    x, h, zx, dtp, xa, y, s_in, gn = saved
    di = w_out.shape[0]
    n_groups = di // (SSD_HEADS_PER_GROUP * SSD_HEAD_DIM)
    dgn = _matmul(dx, w_out, "nt", name=f"ssd_dgn_{tag}")
    dw_out = _matmul_tn(gn, dx, name=f"ssd_dwout_{tag}")
    dy2, dz, dnorm = _ssd_gate_bwd(dgn, y, xa, zx, d_rep, out_norm, name=f"ssd_dgate_{tag}")
    dxs, db, dc, ddtp, dpar = _ssd_scan_bwd(dy2, xa, dtp, par, s_in, n_groups, name=f"ssd_dscan_{tag}")
    dxa = jnp.concatenate([dxs, db, dc], axis=1)
    dxbc, dconv_w, dconv_b = _ssd_conv_bwd(dxa, zx, conv_w, conv_b, di, name=f"ssd_dconv_{tag}")
    dzx = jnp.concatenate([dz, dxbc], axis=1)
    dw_zx_t = _matmul_tn(dzx, h, name=f"ssd_dwin_{tag}")
    dw_dt_t = _matmul_tn(ddtp, h, name=f"ssd_dwdt_{tag}")
    dh = _matmul(dzx, w_zx_t, "nn", name=f"ssd_dh_{tag}")
    dh = _matmul(ddtp, w_dt_t, "nn", resid=dh, name=f"ssd_dhdt_{tag}")
    dx_in, dgain = _rms_bwd(x, gain, dh, dx, name=f"ssd_dnorm_{tag}")
    return dx_in, dw_zx_t, dw_dt_t, dconv_w, dconv_b, dpar, dnorm, dw_out, dgain


def _sb_qk_norm_fwd(qkv, gains, *, name):
    ns, t, _ = qkv.shape
    per = ns // 3
    tm = _tile(t, 1024, SUBLANES)
    inv_sqrt_d = 1.0 / math.sqrt(SB_HEAD_DIM)

    def body(x_ref, g_ref, o_ref):
        kind = pl.program_id(0) // per
        xv = x_ref[...]

        @pl.when(kind == 2)
        def _():
            o_ref[...] = xv.astype(BF16)

        @pl.when(kind < 2)
        def _():
            left = lax.broadcasted_iota(jnp.int32, (1, LANES), 1) < SB_HEAD_DIM
            sq = xv * xv
            ms = jnp.where(left, jnp.sum(jnp.where(left, sq, 0.0), axis=1, keepdims=True),
                           jnp.sum(jnp.where(left, 0.0, sq), axis=1, keepdims=True)) * (1.0 / SB_HEAD_DIM)
            y = xv * lax.rsqrt(ms + NORM_EPS) * g_ref[pl.ds(kind, 1), :]
            o_ref[...] = (y * jnp.where(kind == 0, inv_sqrt_d, 1.0)).astype(BF16)

    blk = pl.BlockSpec((None, tm, LANES), lambda s, i: (s, i, 0))
    return pl.pallas_call(
        body, name=name, grid=(ns, t // tm),
        in_specs=[blk, pl.BlockSpec((SUBLANES, LANES), lambda s, i: (0, 0))], out_specs=blk,
        out_shape=jax.ShapeDtypeStruct((ns, t, LANES), BF16),
        compiler_params=_params("parallel", "parallel"),
    )(qkv, gains)


def _sb_qk_norm_bwd(dq, dk, dv, qkv, gains, *, name):
    ns, t, _ = qkv.shape
    per = ns // 3
    tm = _tile(t, 1024, SUBLANES)
    inv_sqrt_d = 1.0 / math.sqrt(SB_HEAD_DIM)

    def body(dq_ref, dk_ref, dv_ref, x_ref, g_ref, o_ref, dg_ref):
        s = pl.program_id(0)
        kind = s // per

        @pl.when((s == 0) & (pl.program_id(1) == 0))
        def _():
            dg_ref[...] = jnp.zeros_like(dg_ref)

        @pl.when(kind == 2)
        def _():
            o_ref[...] = dv_ref[...].astype(BF16)

        @pl.when(kind < 2)
        def _():
            xv = x_ref[...]
            dy = jnp.where(kind == 0, dq_ref[...] * inv_sqrt_d, dk_ref[...])
            left = lax.broadcasted_iota(jnp.int32, (1, LANES), 1) < SB_HEAD_DIM

            def halves(v):
                return jnp.where(left, jnp.sum(jnp.where(left, v, 0.0), axis=1, keepdims=True),
                                 jnp.sum(jnp.where(left, 0.0, v), axis=1, keepdims=True))

            r = lax.rsqrt(halves(xv * xv) * (1.0 / SB_HEAD_DIM) + NORM_EPS)
            xhat = xv * r
            u = dy * g_ref[pl.ds(kind, 1), :]
            o_ref[...] = (r * (u - xhat * halves(u * xhat) * (1.0 / SB_HEAD_DIM))).astype(BF16)
            dg_ref[pl.ds(kind, 1), :] += jnp.sum(dy * xhat, axis=0, keepdims=True)

    def grad_blk(kind):
        return pl.BlockSpec((None, tm, LANES), lambda s, i: (jnp.clip(s - kind * per, 0, per - 1), i, 0))

    blk = pl.BlockSpec((None, tm, LANES), lambda s, i: (s, i, 0))
    vec = pl.BlockSpec((SUBLANES, LANES), lambda s, i: (0, 0))
    return pl.pallas_call(
        body, name=name, grid=(ns, t // tm),
        in_specs=[grad_blk(0), grad_blk(1), grad_blk(2), blk, vec], out_specs=[blk, vec],
        out_shape=[jax.ShapeDtypeStruct((ns, t, LANES), BF16), jax.ShapeDtypeStruct((SUBLANES, LANES), F32)],
        compiler_params=_params("arbitrary", "arbitrary"),
    )(dq, dk, dv, qkv, gains)


def _split_dot(v, ones_mat, pieces, left=False):
    total, rest = None, v
    for p in range(pieces):
        part = rest.astype(BF16)
        if p + 1 < pieces:
            rest = rest - part.astype(F32)
        d = (jnp.dot(ones_mat, part, preferred_element_type=F32) if left
             else jnp.dot(part, ones_mat, preferred_element_type=F32))
        total = d if total is None else total + d
    return total


LOGIT_SUM_PIECES = 3
GRAD_SUM_PIECES = 2
LOG_WEIGHT_UNDERFLOW = -105.0


def _sb_attn_fwd(qkv_n, n_heads, *, name):
    ns, t, _ = qkv_n.shape
    per = ns // 3
    blk, hd = SB_BLOCK, SB_HEAD_DIM
    nq = t // blk

    def body(q_ref, k_ref, v_ref, o_ref):
        i = pl.program_id(1)
        row = lax.broadcasted_iota(jnp.int32, (blk, blk), 0)
        col = lax.broadcasted_iota(jnp.int32, (blk, blk), 1)
        later_keys = (row > col).astype(BF16)

        def tile(kb, carry, diag):
            out = []
            start = pl.multiple_of(kb * blk, blk)
            for hf in range(2):
                lanes = slice(hf * hd, (hf + 1) * hd)
                run, acc = carry[hf]
                z = lax.dot_general(q_ref[:, lanes], k_ref[pl.ds(start, blk), lanes], (((1,), (1,)), ((), ())),
                                    preferred_element_type=F32)
                sp = _softplus(z)
                lm = jnp.where(col < row, -sp, 0.0) if diag else -sp
                after = _split_dot(lm, later_keys, LOGIT_SUM_PIECES) + run
                a = jnp.exp(z - sp + after)
                if diag:
                    a = jnp.where(col < row, a, 0.0)
                acc = acc + jnp.dot(a.astype(BF16), v_ref[pl.ds(start, blk), lanes], preferred_element_type=F32)
                out.append((run + jnp.sum(lm, axis=1, keepdims=True), acc))
            return tuple(out)

        def live(carry):
            return jnp.max(jnp.maximum(carry[0][0], carry[1][0])) > LOG_WEIGHT_UNDERFLOW

        def step(state):
            s, _, carry = state
            carry = tile(i - 1 - s, carry, False)
            return s + 1, live(carry), carry

        init = tuple((jnp.zeros((blk, 1), F32), jnp.zeros((blk, hd), F32)) for _ in range(2))
        carry = tile(i, init, True)
        _, _, carry = lax.while_loop(lambda st: (st[0] < i) & st[1], step, (jnp.int32(0), live(carry), carry))
        o_ref[...] = jnp.concatenate([carry[0][1], carry[1][1]], axis=1)

    return pl.pallas_call(
        body, name=name, grid=(per, nq),
        in_specs=[pl.BlockSpec((None, blk, LANES), lambda p, i: (p, i, 0)),
                  pl.BlockSpec((None, t, LANES), lambda p, i: (per + p, 0, 0)),
                  pl.BlockSpec((None, t, LANES), lambda p, i: (2 * per + p, 0, 0))],
        out_specs=pl.BlockSpec((blk, LANES), lambda p, i: (i, p)),
        out_shape=jax.ShapeDtypeStruct((t, n_heads * hd), F32),
        compiler_params=_params("parallel", "arbitrary"),
    )(qkv_n, qkv_n, qkv_n)


def _sb_attn_bwd(do, qkv_n, *, name):
    ns, t, _ = qkv_n.shape
    per = ns // 3
    blk, hd = SB_BLOCK, SB_HEAD_DIM
    nq = t // blk
    nt_dims = (((1,), (1,)), ((), ()))
    tn_dims = (((0,), (0,)), ((), ()))

    def body(q_ref, k_ref, v_ref, do_ref, dq_ref, dk_ref, dv_ref):
        i = pl.program_id(1)

        @pl.when(i == 0)
        def _():
            dk_ref[...] = jnp.zeros_like(dk_ref)
            dv_ref[...] = jnp.zeros_like(dv_ref)

        key = lax.broadcasted_iota(jnp.int32, (blk, blk), 0)
        qry = lax.broadcasted_iota(jnp.int32, (blk, blk), 1)
        later_keys = (qry > key).astype(BF16)
        earlier_keys = (qry < key).astype(BF16)
        halves = [slice(hf * hd, (hf + 1) * hd) for hf in range(2)]
        q_hs = [q_ref[:, lanes] for lanes in halves]
        do_bs = [do_ref[:, lanes].astype(BF16) for lanes in halves]

        def scores(kb, hf, diag):
            k_blk = k_ref[pl.ds(pl.multiple_of(kb * blk, blk), blk), halves[hf]]
            z = lax.dot_general(k_blk, q_hs[hf], nt_dims, preferred_element_type=F32)
            sp = _softplus(z)
            return k_blk, z, sp, (jnp.where(key < qry, -sp, 0.0) if diag else -sp)

        def column_sums(kb, diag):
            return [jnp.sum(scores(kb, hf, diag)[3], axis=0, keepdims=True) for hf in range(2)]

        def live(tots):
            return jnp.max(jnp.maximum(tots[0], tots[1])) > LOG_WEIGHT_UNDERFLOW

        def reach(state):
            s, _, tots = state
            more = column_sums(i - 1 - s, False)
            tots = [tots[0] + more[0], tots[1] + more[1]]
            return s + 1, live(tots), tots

        tots = column_sums(i, True)
        reached, _, tots = lax.while_loop(lambda st: (st[0] < i) & st[1], reach, (jnp.int32(0), live(tots), tots))

        def tile(kb, carry, diag):
            out = []
            start = pl.multiple_of(kb * blk, blk)
            for hf, lanes in enumerate(halves):
                seen, gsum, dq = carry[hf]
                q_h, do_b = q_hs[hf], do_bs[hf]
                k_blk, z, sp, lm = scores(kb, hf, diag)
                blk_tot = jnp.sum(lm, axis=0, keepdims=True)
                after = _split_dot(lm, later_keys, LOGIT_SUM_PIECES, left=True) + (tots[hf] - seen - blk_tot)
                a = jnp.exp(z - sp + after)
                if diag:
                    a = jnp.where(key < qry, a, 0.0)
                da = lax.dot_general(v_ref[pl.ds(start, blk), lanes], do_b, nt_dims, preferred_element_type=F32)
                g = da * a
                before = _split_dot(g, earlier_keys, GRAD_SUM_PIECES, left=True) + gsum
                omb = jnp.exp(-sp)
                dz = g * omb - (1.0 - omb) * before
                if diag:
                    dz = jnp.where(key < qry, dz, 0.0)
                dz_b = dz.astype(BF16)
                dk_ref[pl.ds(start, blk), lanes] += jnp.dot(dz_b, q_h, preferred_element_type=F32)
                dv_ref[pl.ds(start, blk), lanes] += jnp.dot(a.astype(BF16), do_b, preferred_element_type=F32)
                dq = dq + lax.dot_general(dz_b, k_blk, tn_dims, preferred_element_type=F32)
                out.append((seen + blk_tot, gsum + jnp.sum(g, axis=0, keepdims=True), dq))
            return tuple(out)

        init = tuple((jnp.zeros((1, blk), F32), jnp.zeros((1, blk), F32), jnp.zeros((blk, hd), F32))
                     for _ in range(2))
        carry = lax.fori_loop(i - reached, i, lambda kb, c: tile(kb, c, False), init)
        carry = tile(i, carry, True)
        dq_ref[...] = jnp.concatenate([carry[0][2], carry[1][2]], axis=1)

    full = lambda off: pl.BlockSpec((None, t, LANES), lambda p, i: (off + p, 0, 0))
    q_blk = pl.BlockSpec((None, blk, LANES), lambda p, i: (p, i, 0))
    slab = jax.ShapeDtypeStruct((per, t, LANES), F32)
    return pl.pallas_call(
        body, name=name, grid=(per, nq),
        in_specs=[q_blk, full(per), full(2 * per), pl.BlockSpec((blk, LANES), lambda p, i: (i, p))],
        out_specs=[q_blk, full(0), full(0)],
        out_shape=[slab, slab, slab],
        compiler_params=_params("parallel", "arbitrary"),
    )(qkv_n, qkv_n, qkv_n, do)


def _sb_mixer_fwd(x, gain, w_qkv_t, qk_gains, w_out, tag):
    n_heads = w_out.shape[0] // SB_HEAD_DIM
    h = _rms_fwd(x, gain, name=f"sb_norm_{tag}")
    qkv = _matmul(h, w_qkv_t, "nt", out_slabs=True, tn_cap=256, name=f"sb_qkv_{tag}")
    qkv_n = _sb_qk_norm_fwd(qkv, qk_gains, name=f"sb_qknorm_{tag}")
    o = _sb_attn_fwd(qkv_n, n_heads, name=f"sb_attn_{tag}")
    x_new = _matmul(o, w_out, "nn", resid=x, name=f"sb_out_{tag}")
    return x_new, (x, h, qkv, qkv_n, o)


def _sb_mixer_bwd(dx, saved, gain, w_qkv_t, qk_gains, w_out, tag):
    x, h, qkv, qkv_n, o = saved
    do = _matmul(dx, w_out, "nt", name=f"sb_do_{tag}")
    dw_out = _matmul_tn(o, dx, name=f"sb_dwout_{tag}")
    dq, dk, dv = _sb_attn_bwd(do, qkv_n, name=f"sb_dattn_{tag}")
    dqkv, dqk_gains = _sb_qk_norm_bwd(dq, dk, dv, qkv, qk_gains, name=f"sb_dqknorm_{tag}")
    dw_qkv_t = _matmul_tn(dqkv, h, a_slabs=True, name=f"sb_dwqkv_{tag}")
    dh = _matmul(dqkv, w_qkv_t, "nn", a_slabs=True, name=f"sb_dh_{tag}")
    dx_in, dgain = _rms_bwd(x, gain, dh, dx, name=f"sb_dnorm_{tag}")
    return dx_in, dw_qkv_t, dqk_gains, dw_out, dgain


MESH = pl.DeviceIdType.MESH


def _position():
    return lax.axis_index("x"), lax.axis_index("y"), lax.axis_index("c")


def _all_gather(shard, *, name, in_vmem):
    rows, n = shard.shape
    space = pltpu.VMEM if in_vmem else pltpu.HBM

    def body(x_ref, out_ref, send_sems, recv_sems, local_sem):
        x, y, c = _position()
        me, sibling = (x, y, c), (x, y, 1 - c)
        chips = [(1 - x, y), (x, 1 - y), (1 - x, 1 - y)]

        def block(px, py, pc):
            return out_ref.at[4 * px + 2 * py + pc]

        def copy(k, blk, to, src=None):
            return pltpu.make_async_remote_copy(
                src_ref=block(*blk) if src is None else src, dst_ref=block(*blk),
                send_sem=send_sems.at[k], recv_sem=recv_sems.at[k], device_id=to, device_id_type=MESH)

        mine = pltpu.make_async_copy(x_ref, block(*me), local_sem)
        mine.start()
        first = [copy(0, me, sibling, src=x_ref)]
        first += [copy(1 + j, me, (*chip, c), src=x_ref) for j, chip in enumerate(chips)]
        for cp in first:
            cp.start()
        passed = [copy(4 + j, (*chip, c), sibling) for j, chip in enumerate(chips)]
        for j, chip in enumerate(chips):
            copy(1 + j, (*chip, c), me).wait_recv()
            passed[j].start()
        copy(0, sibling, me).wait_recv()
        for j, chip in enumerate(chips):
            copy(4 + j, (*chip, 1 - c), me).wait_recv()
        for cp in first + passed:
            cp.wait_send()
        mine.wait()

    return pl.pallas_call(
        body, name=name,
        out_shape=jax.ShapeDtypeStruct((N_DEV, rows, n), shard.dtype),
        in_specs=[pl.BlockSpec(memory_space=space)], out_specs=pl.BlockSpec(memory_space=space),
        scratch_shapes=[pltpu.SemaphoreType.DMA((7,)), pltpu.SemaphoreType.DMA((7,)), pltpu.SemaphoreType.DMA],
        compiler_params=pltpu.CompilerParams(vmem_limit_bytes=V7X_VMEM_LIMIT_BYTES),
    )(shard)


def _exchange_sibling(parts, *, name):
    _, nchip, rows, n = parts.shape

    def body(p_ref, recv_ref, send_sem, recv_sem):
        x, y, c = _position()
        cp = pltpu.make_async_remote_copy(src_ref=p_ref.at[1 - c], dst_ref=recv_ref, send_sem=send_sem,
                                          recv_sem=recv_sem, device_id=(x, y, 1 - c), device_id_type=MESH)
        cp.start()
        cp.wait()

    return pl.pallas_call(
        body, name=name,
        out_shape=jax.ShapeDtypeStruct((nchip, rows, n), parts.dtype),
        in_specs=[pl.BlockSpec(memory_space=pltpu.HBM)], out_specs=pl.BlockSpec(memory_space=pltpu.HBM),
        scratch_shapes=[pltpu.SemaphoreType.DMA, pltpu.SemaphoreType.DMA],
    )(parts)


def _exchange_chips(chip_sums, *, name):
    _, rows, n = chip_sums.shape

    def body(s_ref, recv_ref, send_sems, recv_sems):
        x, y, c = _position()
        chips = [(1 - x, y), (x, 1 - y), (1 - x, 1 - y)]
        copies = [pltpu.make_async_remote_copy(
            src_ref=s_ref.at[2 * cx + cy], dst_ref=recv_ref.at[j], send_sem=send_sems.at[j],
            recv_sem=recv_sems.at[j], device_id=(cx, cy, c), device_id_type=MESH)
            for j, (cx, cy) in enumerate(chips)]
        for cp in copies:
            cp.start()
        for cp in copies:
            cp.wait()

    return pl.pallas_call(
        body, name=name,
        out_shape=jax.ShapeDtypeStruct((3, rows, n), chip_sums.dtype),
        in_specs=[pl.BlockSpec(memory_space=pltpu.HBM)], out_specs=pl.BlockSpec(memory_space=pltpu.HBM),
        scratch_shapes=[pltpu.SemaphoreType.DMA((3,)), pltpu.SemaphoreType.DMA((3,))],
    )(chip_sums)


def _add_pairs(parts, recv, c_mine, *, name):
    _, nchip, rows, n = parts.shape
    tr = _tile(rows, 512, SUBLANES)

    def body(c_ref, a_ref, b_ref, o_ref, wire_ref):
        s = a_ref[...] + b_ref[...]
        o_ref[...] = s
        wire_ref[...] = s.astype(WIRE_DTYPE)

    out_blk = pl.BlockSpec((None, tr, n), lambda k, i, c: (k, i, 0))
    return pl.pallas_call(
        body, name=name,
        grid_spec=pltpu.PrefetchScalarGridSpec(
            num_scalar_prefetch=1, grid=(nchip, rows // tr),
            in_specs=[pl.BlockSpec((None, None, tr, n), lambda k, i, c: (c[0], k, i, 0)),
                      pl.BlockSpec((None, tr, n), lambda k, i, c: (k, i, 0))],
            out_specs=[out_blk, out_blk]),
        out_shape=[jax.ShapeDtypeStruct((nchip, rows, n), parts.dtype),
                   jax.ShapeDtypeStruct((nchip, rows, n), WIRE_DTYPE)],
        compiler_params=_params("parallel", "parallel"),
    )(c_mine, parts, recv)


def _adamw_math(w, g, m, v):
    m = ADAM_B1 * m + (1.0 - ADAM_B1) * g
    v = ADAM_B2 * v + (1.0 - ADAM_B2) * (g * g)
    m_hat = m / (1.0 - ADAM_B1 ** ADAM_STEP)
    v_hat = v / (1.0 - ADAM_B2 ** ADAM_STEP)
    delta = -ADAM_LR * (m_hat / (jnp.sqrt(v_hat) + ADAM_EPS) + ADAM_WD * w)
    return delta, m, v


def _adamw_sharded(chip_sums, recv, k_mine, w, m, v, *, name):
    rows, n = w.shape
    tr = _tile(rows, 256, SUBLANES)

    def body(k_ref, s_ref, r_ref, w_ref, m_ref, v_ref, g_out, d_out, m_out, v_out):
        g = ((s_ref[...] + r_ref[0].astype(F32)) + r_ref[1].astype(F32)) + r_ref[2].astype(F32)
        delta, m_new, v_new = _adamw_math(w_ref[...], g, m_ref[...], v_ref[...])
        g_out[...] = g
        d_out[...] = delta
        m_out[...] = m_new
        v_out[...] = v_new

    blk = pl.BlockSpec((tr, n), lambda i, k: (i, 0))
    out = jax.ShapeDtypeStruct((rows, n), F32)
    return pl.pallas_call(
        body, name=name,
        grid_spec=pltpu.PrefetchScalarGridSpec(
            num_scalar_prefetch=1, grid=(rows // tr,),
            in_specs=[pl.BlockSpec((None, tr, n), lambda i, k: (k[0], i, 0)),
                      pl.BlockSpec((3, tr, n), lambda i, k: (0, i, 0)), blk, blk, blk],
            out_specs=[blk, blk, blk, blk]),
        out_shape=[out, out, out, out],
        compiler_params=_params("parallel"),
    )(k_mine, chip_sums, recv, w, m, v)


SMALL_ROWS = 40
ROW_MIX_NORM, ROW_FFN_NORM, ROW_CONV_B, ROW_OUT_NORM, ROW_POOL_SCALE, ROW_CONV_W = 0, 4, 8, 12, 14, 16
ROW_SSD_VEC, ROW_QK_GAIN, ROW_LOSS = 32, 33, 34


def _adamw_small(gathered, w, m, v, *, name):
    _, rows, n = gathered.shape

    def body(a_ref, w_ref, m_ref, v_ref, g_out, d_out, m_out, v_out):
        g = a_ref[0]
        for d in range(1, N_DEV):
            g = g + a_ref[d]
        row = lax.broadcasted_iota(jnp.int32, (rows, 1), 0)
        g = jnp.where(row == ROW_QK_GAIN, g + pltpu.roll(g, SB_HEAD_DIM, 1), g)
        g = jnp.where(row == ROW_LOSS, jnp.sum(g, axis=1, keepdims=True), g)
        g_out[...] = g
        delta, m_new, v_new = _adamw_math(w_ref[...], g, m_ref[...], v_ref[...])
        d_out[...] = delta
        m_out[...] = m_new
        v_out[...] = v_new

    out = jax.ShapeDtypeStruct((rows, n), F32)
    return pl.pallas_call(body, name=name, out_shape=[out, out, out, out])(gathered, w, m, v)


BIG_WEIGHTS = ("pool_in", "pool_group", "ssd_in", "ssd_out", "sb_qkv", "sb_out", "ffn_gate", "ffn_up", "ffn_down")
COLUMN_SHARDED = ("ssd_in", "sb_qkv", "ffn_gate", "ffn_up")
ROW_PAD = 512
WIRE_DTYPE = jnp.bfloat16


def _to_rows(name, shard, d):
    if name in COLUMN_SHARDED:
        shard = jnp.swapaxes(shard, -1, -2)
    return shard.reshape(-1, d)


def _from_rows(name, rows, shard_shape):
    if name in COLUMN_SHARDED:
        lead, k, n = shard_shape
        return jnp.swapaxes(rows.reshape(lead, n, k), -1, -2)
    return rows.reshape(shard_shape)


def _pad_rows(a, total):
    return jnp.pad(a, ((0, total - a.shape[0]),) + ((0, 0),) * (a.ndim - 1))


def _exact_bf16_rows(v, d):
    words = lax.bitcast_convert_type(v.reshape(-1), WIRE_DTYPE).reshape(-1)
    return _pad_rows(words, -(-words.shape[0] // d) * d).reshape(-1, d)


def _exact_f32(rows, count):
    words = rows.reshape(rows.shape[0], -1)[:, :2 * count].reshape(rows.shape[0], count, 2)
    return lax.bitcast_convert_type(words, F32)


def _device_blocks(name, full, lead, d):
    if name == "pool_group":
        j, g, dg, _ = full.shape
        return full.reshape(j, g, N_DEV, dg // N_DEV, dg).transpose(2, 0, 1, 3, 4).reshape(N_DEV, -1, d)
    per = full.shape[1] // N_DEV
    return full.reshape(lead, N_DEV, per, d).transpose(1, 0, 2, 3).reshape(N_DEV, lead * per, d)


def kernel(x, mix_norm, pool_in, pool_group, pool_scale, ssd_in, ssd_conv_w, ssd_conv_b, ssd_dt_bias, ssd_a_log, ssd_d, ssd_out_norm, ssd_out, sb_qkv, sb_q_norm, sb_k_norm, sb_out, ffn_norm, ffn_gate, ffn_up, ffn_down, loss_target, m_mix_norm, m_pool_in, m_pool_group, m_pool_scale, m_ssd_in, m_ssd_conv_w, m_ssd_conv_b, m_ssd_dt_bias, m_ssd_a_log, m_ssd_d, m_ssd_out_norm, m_ssd_out, m_sb_qkv, m_sb_q_norm, m_sb_k_norm, m_sb_out, m_ffn_norm, m_ffn_gate, m_ffn_up, m_ffn_down, v_mix_norm, v_pool_in, v_pool_group, v_pool_scale, v_ssd_in, v_ssd_conv_w, v_ssd_conv_b, v_ssd_dt_bias, v_ssd_a_log, v_ssd_d, v_ssd_out_norm, v_ssd_out, v_sb_qkv, v_sb_q_norm, v_sb_k_norm, v_sb_out, v_ffn_norm, v_ffn_gate, v_ffn_up, v_ffn_down):
    weights = dict(mix_norm=mix_norm, pool_in=pool_in, pool_group=pool_group, pool_scale=pool_scale, ssd_in=ssd_in,
                   ssd_conv_w=ssd_conv_w, ssd_conv_b=ssd_conv_b, ssd_dt_bias=ssd_dt_bias, ssd_a_log=ssd_a_log,
                   ssd_d=ssd_d, ssd_out_norm=ssd_out_norm, ssd_out=ssd_out, sb_qkv=sb_qkv, sb_q_norm=sb_q_norm,
                   sb_k_norm=sb_k_norm, sb_out=sb_out, ffn_norm=ffn_norm, ffn_gate=ffn_gate, ffn_up=ffn_up,
                   ffn_down=ffn_down)
    mom1 = dict(mix_norm=m_mix_norm, pool_in=m_pool_in, pool_group=m_pool_group, pool_scale=m_pool_scale,
                ssd_in=m_ssd_in, ssd_conv_w=m_ssd_conv_w, ssd_conv_b=m_ssd_conv_b, ssd_dt_bias=m_ssd_dt_bias,
                ssd_a_log=m_ssd_a_log, ssd_d=m_ssd_d, ssd_out_norm=m_ssd_out_norm, ssd_out=m_ssd_out,
                sb_qkv=m_sb_qkv, sb_q_norm=m_sb_q_norm, sb_k_norm=m_sb_k_norm, sb_out=m_sb_out,
                ffn_norm=m_ffn_norm, ffn_gate=m_ffn_gate, ffn_up=m_ffn_up, ffn_down=m_ffn_down)
    mom2 = dict(mix_norm=v_mix_norm, pool_in=v_pool_in, pool_group=v_pool_group, pool_scale=v_pool_scale,
                ssd_in=v_ssd_in, ssd_conv_w=v_ssd_conv_w, ssd_conv_b=v_ssd_conv_b, ssd_dt_bias=v_ssd_dt_bias,
                ssd_a_log=v_ssd_a_log, ssd_d=v_ssd_d, ssd_out_norm=v_ssd_out_norm, ssd_out=v_ssd_out,
                sb_qkv=v_sb_qkv, sb_q_norm=v_sb_q_norm, sb_k_norm=v_sb_k_norm, sb_out=v_sb_out,
                ffn_norm=v_ffn_norm, ffn_gate=v_ffn_gate, ffn_up=v_ffn_up, ffn_down=v_ffn_down)
    names = list(weights)
    depth, d = mix_norm.shape
    xs, ys, cs = _position()
    dev = 4 * xs + 2 * ys + cs
    chip = 2 * xs + ys

    seg = {}
    row = 0
    for name in BIG_WEIGHTS:
        n_rows = weights[name].size // d
        seg[name] = (row, n_rows)
        row += n_rows
    big_rows = row
    n_scale, n_convw = pool_scale.size, ssd_conv_w.size
    exact = jnp.concatenate([_exact_bf16_rows(pool_scale, d), _exact_bf16_rows(ssd_conv_w, d)], axis=0)
    scale_rows = _exact_bf16_rows(pool_scale, d).shape[0]
    packed_rows = -(-(big_rows + exact.shape[0]) // ROW_PAD) * ROW_PAD

    def pack(tree, dtype):
        return jnp.concatenate([_to_rows(n, tree[n], d).astype(dtype) for n in BIG_WEIGHTS], axis=0)

    w_wire = _pad_rows(jnp.concatenate([pack(weights, WIRE_DTYPE), exact], axis=0), packed_rows)
    gathered = _all_gather(w_wire, name="gather_weights", in_vmem=False)

    def seg_of(name):
        a, n = seg[name]
        return gathered[:, a:a + n]

    n_pool, n_ssd, n_sb = pool_in.shape[0], ssd_in.shape[0], sb_qkv.shape[0]
    assert n_ssd == 1 and n_sb == 1
    w_pool_in = seg_of("pool_in").reshape(N_DEV, n_pool, -1, d).transpose(1, 0, 2, 3).reshape(n_pool, d, d)
    grp = pool_group.shape
    w_pool_group = seg_of("pool_group").reshape(N_DEV, grp[0], grp[1], grp[2], grp[3]).transpose(1, 2, 0, 3, 4)
    w_pool_group = w_pool_group.reshape(grp[0], grp[1], grp[3], grp[3])
    w_ssd_in_t = seg_of("ssd_in").reshape(-1, d)
    w_ssd_out = seg_of("ssd_out").reshape(-1, d)
    w_sb_qkv_t = seg_of("sb_qkv").reshape(-1, d)
    w_sb_out = seg_of("sb_out").reshape(-1, d)
    hidden = ffn_down.shape[1] * N_DEV
    w_gate_t = seg_of("ffn_gate").reshape(N_DEV, depth, -1, d).transpose(1, 0, 2, 3).reshape(depth, hidden, d)
    w_up_t = seg_of("ffn_up").reshape(N_DEV, depth, -1, d).transpose(1, 0, 2, 3).reshape(depth, hidden, d)
    w_gu_t = jnp.concatenate([w_gate_t, w_up_t], axis=1)
    w_down = seg_of("ffn_down").reshape(N_DEV, depth, -1, d).transpose(1, 0, 2, 3).reshape(depth, hidden, d)
    exact_all = gathered[:, big_rows:big_rows + exact.shape[0]]
    scale_full = _exact_f32(exact_all[:, :scale_rows], n_scale).reshape(N_DEV, n_pool, -1)
    scale_full = scale_full.transpose(1, 0, 2).reshape(n_pool, d)
    convw_full = _exact_f32(exact_all[:, scale_rows:], n_convw).reshape(N_DEV, SSD_CONV, -1)
    convw_full = convw_full.transpose(1, 0, 2).reshape(SSD_CONV, -1)

    d_inner = w_ssd_out.shape[0]
    n_zx = w_ssd_in_t.shape[0] - ssd_dt_bias.shape[1]
    w_zx_t = w_ssd_in_t[:n_zx]
    w_dt_t = _pad_rows(w_ssd_in_t[n_zx:], LANES)
    n_ssd_heads = ssd_dt_bias.shape[1]
    par = jnp.zeros((SUBLANES, LANES), F32)
    par = par.at[0, :n_ssd_heads].set(ssd_dt_bias[0]).at[1, :n_ssd_heads].set(ssd_a_log[0])
    par = par.at[2, :n_ssd_heads].set(ssd_d[0])
    d_rep = jnp.repeat(ssd_d[0], SSD_HEAD_DIM)[None]
    qk_gains = jnp.zeros((SUBLANES, LANES), F32).at[0].set(jnp.tile(sb_q_norm[0], 2)).at[1].set(jnp.tile(sb_k_norm[0], 2))

    act = x[0]
    saved = []
    for i in range(depth):
        kind, j = i % 3, i // 3
        gain = mix_norm[i:i + 1]
        if kind == 0:
            act, s = _pool_mixer_fwd(act, gain, w_pool_in[j], w_pool_group[j], scale_full[j:j + 1], f"l{i}")
        elif kind == 1:
            act, s = _ssd_mixer_fwd(act, gain, w_zx_t, w_dt_t, convw_full, ssd_conv_b, par, d_rep, ssd_out_norm,
                                    w_ssd_out, f"l{i}")
        else:
            act, s = _sb_mixer_fwd(act, gain, w_sb_qkv_t, qk_gains, w_sb_out, f"l{i}")
        act, f = _ffn_fwd(act, ffn_norm[i:i + 1], w_gu_t[i], w_down[i], f"l{i}")
        saved.append((s, f))
    dact, loss_cols = _loss_head(act, loss_target[0], name="loss_head")

    g_mix_norm, g_ffn_norm = [None] * depth, [None] * depth
    g_pool_in, g_pool_group, g_pool_scale = [None] * n_pool, [None] * n_pool, [None] * n_pool
    g_gu_t, g_down = [None] * depth, [None] * depth
    for i in reversed(range(depth)):
        kind, j = i % 3, i // 3
        gain = mix_norm[i:i + 1]
        s, f = saved[i]
        dact, g_gu_t[i], g_down[i], g_ffn_norm[i] = _ffn_bwd(dact, f, ffn_norm[i:i + 1], w_gu_t[i], w_down[i], f"l{i}")
        if kind == 0:
            dact, g_pool_in[j], g_pool_group[j], g_pool_scale[j], g_mix_norm[i] = _pool_mixer_bwd(
                dact, s, gain, w_pool_in[j], w_pool_group[j], scale_full[j:j + 1], f"l{i}")
        elif kind == 1:
            (dact, g_zx_t, g_dt_t, g_conv_w, g_conv_b, g_par, g_out_norm, g_ssd_out,
             g_mix_norm[i]) = _ssd_mixer_bwd(dact, s, gain, w_zx_t, w_dt_t, convw_full, ssd_conv_b, par, d_rep,
                                             ssd_out_norm, w_ssd_out, f"l{i}")
        else:
            dact, g_qkv_t, g_qk_gains, g_sb_out, g_mix_norm[i] = _sb_mixer_bwd(
                dact, s, gain, w_sb_qkv_t, qk_gains, w_sb_out, f"l{i}")
    grad_x = dact[None]

    g_gu_t = jnp.stack(g_gu_t)
    full_grads = dict(
        pool_in=jnp.stack(g_pool_in), pool_group=jnp.stack(g_pool_group),
        ssd_in=jnp.concatenate([g_zx_t, g_dt_t[:n_ssd_heads]], axis=0)[None], ssd_out=g_ssd_out[None],
        sb_qkv=g_qkv_t[None], sb_out=g_sb_out[None],
        ffn_gate=g_gu_t[:, :hidden], ffn_up=g_gu_t[:, hidden:], ffn_down=jnp.stack(g_down))
    blocks = [_device_blocks(n, full_grads[n], full_grads[n].shape[0], d) for n in BIG_WEIGHTS]
    parts = jnp.concatenate(blocks, axis=1)
    parts = jnp.pad(parts, ((0, 0), (0, packed_rows - big_rows), (0, 0)))
    parts = parts.reshape(N_DEV // 2, 2, packed_rows, d).transpose(1, 0, 2, 3)
    from_sibling = _exchange_sibling(parts, name="reduce_sibling")
    chip_sums, chip_sums_wire = _add_pairs(parts, from_sibling, cs.reshape(1).astype(jnp.int32),
                                           name="reduce_sibling_add")
    from_chips = _exchange_chips(chip_sums_wire, name="reduce_chips")

    def pack_f32(tree):
        return _pad_rows(pack(tree, F32), packed_rows)

    big_out = _adamw_sharded(chip_sums, from_chips, chip.reshape(1).astype(jnp.int32), pack_f32(weights),
                             pack_f32(mom1), pack_f32(mom2), name="adamw_sharded")

    def small_pack(mix, ffn, conv_b, out_norm, scale, conv_w, vec, qk, loss=None):
        buf = jnp.zeros((SMALL_ROWS, d), F32)
        buf = buf.at[ROW_MIX_NORM:ROW_MIX_NORM + depth].set(mix).at[ROW_FFN_NORM:ROW_FFN_NORM + depth].set(ffn)
        buf = buf.at[ROW_CONV_B:ROW_CONV_B + conv_b.size // d].set(conv_b.reshape(-1, d))
        buf = buf.at[ROW_OUT_NORM:ROW_OUT_NORM + out_norm.size // d].set(out_norm.reshape(-1, d))
        buf = buf.at[ROW_POOL_SCALE:ROW_POOL_SCALE + n_pool].set(scale)
        buf = buf.at[ROW_CONV_W:ROW_CONV_W + conv_w.size // d].set(conv_w.reshape(-1, d))
        buf = buf.at[ROW_SSD_VEC].set(vec.reshape(-1)).at[ROW_QK_GAIN].set(qk.reshape(-1))
        if loss is not None:
            buf = buf.at[ROW_LOSS].set(loss.reshape(-1))
        return buf

    def small_params(tree):
        scale = lax.dynamic_update_slice(jnp.zeros((n_pool, d), F32), tree["pool_scale"],
                                         (0, dev * tree["pool_scale"].shape[1]))
        conv_w = lax.dynamic_update_slice(jnp.zeros(convw_full.shape, F32), tree["ssd_conv_w"][0],
                                          (0, dev * tree["ssd_conv_w"].shape[2]))
        vec = jnp.zeros((SUBLANES, LANES), F32)
        vec = vec.at[0, :n_ssd_heads].set(tree["ssd_dt_bias"][0]).at[1, :n_ssd_heads].set(tree["ssd_a_log"][0])
        vec = vec.at[2, :n_ssd_heads].set(tree["ssd_d"][0])
        qk = jnp.zeros((SUBLANES, LANES), F32)
        qk = qk.at[0, SB_HEAD_DIM:].set(tree["sb_q_norm"][0]).at[1, SB_HEAD_DIM:].set(tree["sb_k_norm"][0])
        return small_pack(tree["mix_norm"], tree["ffn_norm"], tree["ssd_conv_b"], tree["ssd_out_norm"], scale,
                          conv_w, vec, qk)

    small_partial = small_pack(jnp.concatenate(g_mix_norm, axis=0), jnp.concatenate(g_ffn_norm, axis=0), g_conv_b,
                               g_out_norm, jnp.concatenate(g_pool_scale, axis=0), g_conv_w, g_par, g_qk_gains,
                               loss_cols)
    small_all = _all_gather(small_partial, name="gather_small", in_vmem=True)
    small_out = _adamw_small(small_all, small_params(weights), small_params(mom1), small_params(mom2),
                             name="adamw_small")
    loss = small_out[0][ROW_LOSS, 0]

    def unpack(big, small):
        out = {}
        for name in BIG_WEIGHTS:
            a, n = seg[name]
            out[name] = _from_rows(name, big[a:a + n], weights[name].shape)
        out["mix_norm"] = small[ROW_MIX_NORM:ROW_MIX_NORM + depth]
        out["ffn_norm"] = small[ROW_FFN_NORM:ROW_FFN_NORM + depth]
        out["ssd_conv_b"] = small[ROW_CONV_B:ROW_CONV_B + ssd_conv_b.size // d].reshape(ssd_conv_b.shape)
        out["ssd_out_norm"] = small[ROW_OUT_NORM:ROW_OUT_NORM + ssd_out_norm.size // d].reshape(ssd_out_norm.shape)
        out["pool_scale"] = lax.dynamic_slice(small[ROW_POOL_SCALE:ROW_POOL_SCALE + n_pool],
                                              (0, dev * pool_scale.shape[1]), pool_scale.shape)
        conv_w = small[ROW_CONV_W:ROW_CONV_W + convw_full.size // d].reshape(convw_full.shape)
        out["ssd_conv_w"] = lax.dynamic_slice(conv_w, (0, dev * ssd_conv_w.shape[2]), ssd_conv_w.shape[1:])[None]
        vec = small[ROW_SSD_VEC].reshape(SUBLANES, LANES)
        out["ssd_dt_bias"], out["ssd_a_log"], out["ssd_d"] = (vec[r:r + 1, :n_ssd_heads] for r in range(3))
        qk = small[ROW_QK_GAIN].reshape(SUBLANES, LANES)
        out["sb_q_norm"], out["sb_k_norm"] = qk[0:1, SB_HEAD_DIM:], qk[1:2, SB_HEAD_DIM:]
        return [out[n] for n in names]

    results = [unpack(b, s) for b, s in zip(big_out, small_out)]
    return (loss, grad_x, *results[0], *results[1], *results[2], *results[3])
```

```python
import math

import jax
import jax.numpy as jnp
from jax import lax
from jax.experimental import pallas as pl
from jax.experimental.pallas import tpu as pltpu

F32 = jnp.float32
BF16 = jnp.bfloat16

N_DEV = 8
NORM_EPS = 1e-6
V7X_VMEM_LIMIT_BYTES = 48 * 1024 * 1024
LANES = 128
SUBLANES = 8

POOL_WINDOWS = (2, 4, 8, 16)
SSD_CHUNK = 256
SSD_HEAD_DIM = 64
SSD_STATE = 128
SSD_HEADS_PER_GROUP = 4
SSD_CONV = 4
SB_HEAD_DIM = 64
SB_BLOCK = 128

ADAM_LR = 0.001
ADAM_B1 = 0.9
ADAM_B2 = 0.999
ADAM_EPS = 1e-08
ADAM_WD = 0.01
ADAM_STEP = 10


def _params(*sem):
    return pltpu.CompilerParams(dimension_semantics=sem, vmem_limit_bytes=V7X_VMEM_LIMIT_BYTES)


def _tile(n, cap, mult):
    best = None
    for t in range(mult, min(n, cap) + 1, mult):
        if n % t == 0:
            best = t
    return best or n


def _load_slabs(ref, slabs):
    if not slabs:
        return ref[...]
    return jnp.concatenate([ref[p] for p in range(ref.shape[0])], axis=1)


def _matmul(a, b, mode, *, name, out_dtype=F32, resid=None, a_slabs=False, out_slabs=False,
            tm_cap=1024, tn_cap=1024, tk_cap=2048):
    pairs = list(zip(a, b)) if isinstance(a, (list, tuple)) else [(a, b)]
    a, b = pairs[0]
    if a_slabs:
        m, k = a.shape[1], a.shape[0] * LANES
    else:
        m, k = a.shape
    n = b.shape[1] if mode == "nn" else b.shape[0]
    assert (b.shape[0] if mode == "nn" else b.shape[1]) == k
    assert all(pa.shape == a.shape and pb.shape == b.shape for pa, pb in pairs)
    tm, tn, tk = _tile(m, tm_cap, SUBLANES), _tile(n, tn_cap, LANES), _tile(k, tk_cap, LANES)
    nk = k // tk
    dn = (((1,), (0,)), ((), ())) if mode == "nn" else (((1,), (1,)), ((), ()))
    has_resid = resid is not None
    n_pairs = len(pairs)

    def body(*refs):
        ab_refs, rest = refs[:2 * n_pairs], refs[2 * n_pairs:]
        r_ref = rest[0] if has_resid else None
        o_ref = rest[1] if has_resid else rest[0]
        kk = pl.program_id(2)

        def partial():
            total = None
            for p in range(n_pairs):
                d = lax.dot_general(_load_slabs(ab_refs[2 * p], a_slabs).astype(BF16),
                                    ab_refs[2 * p + 1][...].astype(BF16), dn, preferred_element_type=F32)
                total = d if total is None else total + d
            return total

        def finish(r):
            if has_resid:
                r = r + r_ref[...]
            if out_slabs:
                for p in range(tn // LANES):
                    o_ref[p] = r[:, p * LANES:(p + 1) * LANES].astype(out_dtype)
            else:
                o_ref[...] = r.astype(out_dtype)

        if nk == 1:
            finish(partial())
        else:
            acc = rest[-1]

            @pl.when(kk == 0)
            def _():
                acc[...] = jnp.zeros_like(acc)

            acc[...] += partial()

            @pl.when(kk == nk - 1)
            def _():
                finish(acc[...])

    b_spec = (pl.BlockSpec((tk, tn), lambda i, j, kk: (kk, j)) if mode == "nn"
              else pl.BlockSpec((tn, tk), lambda i, j, kk: (j, kk)))
    a_spec = (pl.BlockSpec((tk // LANES, tm, LANES), lambda i, j, kk: (kk, i, 0)) if a_slabs
              else pl.BlockSpec((tm, tk), lambda i, j, kk: (i, kk)))
    in_specs = [a_spec, b_spec] * n_pairs
    args = [t for pair in pairs for t in pair]
    if has_resid:
        in_specs.append(pl.BlockSpec((tm, tn), lambda i, j, kk: (i, j)))
        args.append(resid)
    if out_slabs:
        out_spec = pl.BlockSpec((tn // LANES, tm, LANES), lambda i, j, kk: (j, i, 0))
        out_shape = jax.ShapeDtypeStruct((n // LANES, m, LANES), out_dtype)
    else:
        out_spec = pl.BlockSpec((tm, tn), lambda i, j, kk: (i, j))
        out_shape = jax.ShapeDtypeStruct((m, n), out_dtype)
    return pl.pallas_call(
        body, name=name, grid=(m // tm, n // tn, nk),
        in_specs=in_specs, out_specs=out_spec, out_shape=out_shape,
        scratch_shapes=[pltpu.VMEM((tm, tn), F32)] if nk > 1 else [],
        compiler_params=_params("parallel", "parallel", "arbitrary"),
    )(*args)


def _matmul_tn(a, b, *, name, a_slabs=False, ta_cap=1024, tb_cap=1024, tr_cap=512):
    if a_slabs:
        r, ka = a.shape[1], a.shape[0] * LANES
    else:
        r, ka = a.shape
    nb = b.shape[1]
    assert b.shape[0] == r
    ta, tb, tr = _tile(ka, ta_cap, LANES), _tile(nb, tb_cap, LANES), _tile(r, tr_cap, SUBLANES)

    def body(a_ref, b_ref, o_ref):
        @pl.when(pl.program_id(2) == 0)
        def _():
            o_ref[...] = jnp.zeros_like(o_ref)

        o_ref[...] += lax.dot_general(_load_slabs(a_ref, a_slabs).astype(BF16), b_ref[...].astype(BF16),
                                      (((0,), (0,)), ((), ())), preferred_element_type=F32)

    a_spec = (pl.BlockSpec((ta // LANES, tr, LANES), lambda i, j, kk: (i, kk, 0)) if a_slabs
              else pl.BlockSpec((tr, ta), lambda i, j, kk: (kk, i)))
    return pl.pallas_call(
        body, name=name, grid=(ka // ta, nb // tb, r // tr),
        in_specs=[a_spec, pl.BlockSpec((tr, tb), lambda i, j, kk: (kk, j))],
        out_specs=pl.BlockSpec((ta, tb), lambda i, j, kk: (i, j)),
        out_shape=jax.ShapeDtypeStruct((ka, nb), F32),
        compiler_params=_params("parallel", "parallel", "arbitrary"),
    )(a, b)


def _rms_fwd(x, gain, *, name):
    t, d = x.shape
    tm = _tile(t, 512, SUBLANES)

    def body(x_ref, g_ref, o_ref):
        xv = x_ref[...]
        r = lax.rsqrt(jnp.mean(xv * xv, axis=-1, keepdims=True) + NORM_EPS)
        o_ref[...] = (xv * r * g_ref[...]).astype(BF16)

    return pl.pallas_call(
        body, name=name, grid=(t // tm,),
        in_specs=[pl.BlockSpec((tm, d), lambda i: (i, 0)), pl.BlockSpec((1, d), lambda i: (0, 0))],
        out_specs=pl.BlockSpec((tm, d), lambda i: (i, 0)),
        out_shape=jax.ShapeDtypeStruct((t, d), BF16),
        compiler_params=_params("parallel"),
    )(x, gain)


def _rms_bwd(x, gain, dh, dres, *, name):
    t, d = x.shape
    tm = _tile(t, 512, SUBLANES)

    def body(x_ref, g_ref, dh_ref, dres_ref, dx_ref, dg_ref):
        @pl.when(pl.program_id(0) == 0)
        def _():
            dg_ref[...] = jnp.zeros_like(dg_ref)

        xv = x_ref[...]
        r = lax.rsqrt(jnp.mean(xv * xv, axis=-1, keepdims=True) + NORM_EPS)
        xhat = xv * r
        dhv = dh_ref[...]
        u = dhv * g_ref[...]
        dx_ref[...] = dres_ref[...] + r * (u - xhat * jnp.mean(u * xhat, axis=-1, keepdims=True))
        dg_ref[...] += jnp.sum(dhv * xhat, axis=0, keepdims=True)

    return pl.pallas_call(
        body, name=name, grid=(t // tm,),
        in_specs=[pl.BlockSpec((tm, d), lambda i: (i, 0)), pl.BlockSpec((1, d), lambda i: (0, 0)),
                  pl.BlockSpec((tm, d), lambda i: (i, 0)), pl.BlockSpec((tm, d), lambda i: (i, 0))],
        out_specs=[pl.BlockSpec((tm, d), lambda i: (i, 0)), pl.BlockSpec((1, d), lambda i: (0, 0))],
        out_shape=[jax.ShapeDtypeStruct((t, d), F32), jax.ShapeDtypeStruct((1, d), F32)],
        compiler_params=_params("arbitrary"),
    )(x, gain, dh, dres)


def _loss_head(y, target, *, name):
    t, d = y.shape
    tm = _tile(t, 512, SUBLANES)

    def body(y_ref, t_ref, dy_ref, l_ref):
        @pl.when(pl.program_id(0) == 0)
        def _():
            l_ref[...] = jnp.zeros_like(l_ref)

        e = y_ref[...] - t_ref[...]
        dy_ref[...] = e * (1.0 / d)
        l_ref[...] += jnp.sum(e * e, axis=0, keepdims=True) * (0.5 / d)

    return pl.pallas_call(
        body, name=name, grid=(t // tm,),
        in_specs=[pl.BlockSpec((tm, d), lambda i: (i, 0)), pl.BlockSpec((tm, d), lambda i: (i, 0))],
        out_specs=[pl.BlockSpec((tm, d), lambda i: (i, 0)), pl.BlockSpec((1, d), lambda i: (0, 0))],
        out_shape=[jax.ShapeDtypeStruct((t, d), F32), jax.ShapeDtypeStruct((1, d), F32)],
        compiler_params=_params("arbitrary"),
    )(y, target)


def _sigmoid(v):
    return 0.5 * jnp.tanh(0.5 * v) + 0.5


FFN_TOKEN_TILE = 512
FFN_HIDDEN_TILE = 1408
NT_DIMS = (((1,), (1,)), ((), ()))


def _ffn_up(h, w_gate_t, w_up_t, *, name):
    t, d = h.shape
    f = w_gate_t.shape[0]
    tm, tn = _tile(t, FFN_TOKEN_TILE, SUBLANES), _tile(f, FFN_HIDDEN_TILE, LANES)

    def body(h_ref, g_ref, u_ref, s_ref, a_ref, b_ref):
        hv = h_ref[...].astype(BF16)
        av = lax.dot_general(hv, g_ref[...].astype(BF16), NT_DIMS, preferred_element_type=F32)
        bv = lax.dot_general(hv, u_ref[...].astype(BF16), NT_DIMS, preferred_element_type=F32)
        s_ref[...] = (av * _sigmoid(av) * bv).astype(BF16)
        a_ref[...] = av.astype(BF16)
        b_ref[...] = bv.astype(BF16)

    w_spec = pl.BlockSpec((tn, d), lambda j, i: (j, 0))
    out_spec = pl.BlockSpec((tm, tn), lambda j, i: (i, j))
    out = jax.ShapeDtypeStruct((t, f), BF16)
    return pl.pallas_call(
        body, name=name, grid=(f // tn, t // tm),
        in_specs=[pl.BlockSpec((tm, d), lambda j, i: (i, 0)), w_spec, w_spec],
        out_specs=[out_spec, out_spec, out_spec], out_shape=[out, out, out],
        compiler_params=_params("parallel", "parallel"),
    )(h, w_gate_t, w_up_t)


def _ffn_dact(dx, w_down, a, b, *, name):
    t, d = dx.shape
    f = w_down.shape[0]
    tm, tn = _tile(t, FFN_TOKEN_TILE, SUBLANES), _tile(f, FFN_HIDDEN_TILE, LANES)

    def body(dx_ref, w_ref, a_ref, b_ref, da_ref, db_ref):
        ds = lax.dot_general(dx_ref[...].astype(BF16), w_ref[...].astype(BF16), NT_DIMS, preferred_element_type=F32)
        av = a_ref[...].astype(F32)
        sg = _sigmoid(av)
        da_ref[...] = (ds * b_ref[...].astype(F32) * (sg * (1.0 + av * (1.0 - sg)))).astype(BF16)
        db_ref[...] = (ds * av * sg).astype(BF16)

    blk = pl.BlockSpec((tm, tn), lambda j, i: (i, j))
    out = jax.ShapeDtypeStruct((t, f), BF16)
    return pl.pallas_call(
        body, name=name, grid=(f // tn, t // tm),
        in_specs=[pl.BlockSpec((tm, d), lambda j, i: (i, 0)), pl.BlockSpec((tn, d), lambda j, i: (j, 0)), blk, blk],
        out_specs=[blk, blk], out_shape=[out, out],
        compiler_params=_params("parallel", "parallel"),
    )(dx, w_down, a, b)


def _ffn_fwd(x, gain, w_gate_t, w_up_t, w_down, tag):
    h = _rms_fwd(x, gain, name=f"ffn_norm_{tag}")
    s, a, b = _ffn_up(h, w_gate_t, w_up_t, name=f"ffn_up_{tag}")
    x_new = _matmul(s, w_down, "nn", resid=x, tn_cap=1024, tk_cap=2816, name=f"ffn_down_{tag}")
    return x_new, (x, h, a, b, s)


def _ffn_bwd(dx, saved, gain, w_gate_t, w_up_t, w_down, tag):
    x, h, a, b, s = saved
    da, db = _ffn_dact(dx, w_down, a, b, name=f"ffn_dact_{tag}")
    wide = dict(ta_cap=FFN_HIDDEN_TILE, tb_cap=1024, tr_cap=512)
    dw_down = _matmul_tn(s, dx, name=f"ffn_dwdown_{tag}", **wide)
    dw_gate_t = _matmul_tn(da, h, name=f"ffn_dwgate_{tag}", **wide)
    dw_up_t = _matmul_tn(db, h, name=f"ffn_dwup_{tag}", **wide)
    dh = _matmul([da, db], [w_gate_t, w_up_t], "nn", tm_cap=512, tn_cap=1024, tk_cap=2816, name=f"ffn_dh_{tag}")
    dx_in, dgain = _rms_bwd(x, gain, dh, dx, name=f"ffn_dnorm_{tag}")
    return dx_in, dw_gate_t, dw_up_t, dw_down, dgain


POOL_HALO = 16


def _shift_rows(v, k):
    n = v.shape[0]
    return pltpu.roll(v, k % n, 0)


def _window_sum(v, w, direction):
    k = 1
    while k < w:
        v = v + _shift_rows(v, direction * k)
        k *= 2
    return v


def _pool_fwd(u, x, w_group, scale, *, name):
    t, d = u.shape
    ng, dg = w_group.shape[0], w_group.shape[1]
    tm = _tile(t, 512, POOL_HALO)
    hb = tm // POOL_HALO

    def body(u_ref, halo_ref, x_ref, w_ref, s_ref, xo_ref, p_ref, y_ref):
        i, g = pl.program_id(0), pl.program_id(1)
        halo = jnp.where(i > 0, halo_ref[...], 0.0)
        ext = jnp.concatenate([halo, u_ref[...]], axis=0)
        pos = i * tm + lax.broadcasted_iota(jnp.int32, (tm, 1), 0)
        for gi, win in enumerate(POOL_WINDOWS):
            @pl.when(g == gi)
            def _(win=win):
                tot = _window_sum(ext, win, 1)[POOL_HALO:]
                cnt = jnp.minimum(pos + 1, win).astype(F32)
                p = (tot / cnt - u_ref[...]).astype(BF16)
                p_ref[...] = p
                y = jnp.dot(p, w_ref[...].astype(BF16), preferred_element_type=F32)
                y_ref[...] = y
                xo_ref[...] = x_ref[...] + y * s_ref[...]

    blk = pl.BlockSpec((tm, dg), lambda i, g: (i, g))
    return pl.pallas_call(
        body, name=name, grid=(t // tm, ng),
        in_specs=[blk, pl.BlockSpec((POOL_HALO, dg), lambda i, g: (jnp.maximum(i * hb - 1, 0), g)), blk,
                  pl.BlockSpec((None, dg, dg), lambda i, g: (g, 0, 0)), pl.BlockSpec((1, dg), lambda i, g: (0, g))],
        out_specs=[blk, blk, blk],
        out_shape=[jax.ShapeDtypeStruct((t, d), F32), jax.ShapeDtypeStruct((t, d), BF16),
                   jax.ShapeDtypeStruct((t, d), F32)],
        compiler_params=_params("parallel", "parallel"),
    )(u, u, x, w_group, scale)


def _pool_bwd(dx, p, y_pre, w_group, scale, *, name):
    t, d = dx.shape
    ng, dg = w_group.shape[0], w_group.shape[1]
    tm = _tile(t, 512, POOL_HALO)
    hb = tm // POOL_HALO
    nt = t // tm

    def body(dx_ref, nxt_ref, p_ref, y_ref, w_ref, s_ref, du_ref, dw_ref, ds_ref):
        g, i = pl.program_id(0), pl.program_id(1)

        @pl.when(i == 0)
        def _():
            dw_ref[...] = jnp.zeros_like(dw_ref)
            ds_ref[...] = jnp.zeros_like(ds_ref)

        dxv = dx_ref[...]
        ds_ref[...] += jnp.sum(dxv * y_ref[...], axis=0, keepdims=True)
        nxt = jnp.where(i < nt - 1, nxt_ref[...], 0.0)
        dyp = (jnp.concatenate([dxv, nxt], axis=0) * s_ref[...]).astype(BF16)
        dw_ref[...] += lax.dot_general(p_ref[...], dyp[:tm], (((0,), (0,)), ((), ())), preferred_element_type=F32)
        dp = lax.dot_general(dyp, w_ref[...].astype(BF16), (((1,), (1,)), ((), ())), preferred_element_type=F32)
        pos = i * tm + lax.broadcasted_iota(jnp.int32, (tm + POOL_HALO, 1), 0)
        for gi, win in enumerate(POOL_WINDOWS):
            @pl.when(g == gi)
            def _(win=win):
                q = dp / jnp.minimum(pos + 1, win).astype(F32)
                du_ref[...] = (_window_sum(q, win, -1)[:tm] - dp[:tm]).astype(BF16)

    blk = pl.BlockSpec((tm, dg), lambda g, i: (i, g))
    return pl.pallas_call(
        body, name=name, grid=(ng, nt),
        in_specs=[blk, pl.BlockSpec((POOL_HALO, dg), lambda g, i: (jnp.minimum((i + 1) * hb, t // POOL_HALO - 1), g)),
                  blk, blk, pl.BlockSpec((None, dg, dg), lambda g, i: (g, 0, 0)),
                  pl.BlockSpec((1, dg), lambda g, i: (0, g))],
        out_specs=[blk, pl.BlockSpec((None, dg, dg), lambda g, i: (g, 0, 0)), pl.BlockSpec((1, dg), lambda g, i: (0, g))],
        out_shape=[jax.ShapeDtypeStruct((t, d), BF16), jax.ShapeDtypeStruct((ng, dg, dg), F32),
                   jax.ShapeDtypeStruct((1, d), F32)],
        compiler_params=_params("parallel", "arbitrary"),
    )(dx, dx, p, y_pre, w_group, scale)


def _pool_mixer_fwd(x, gain, w_in, w_group, scale, tag):
    h = _rms_fwd(x, gain, name=f"pool_norm_{tag}")
    u = _matmul(h, w_in, "nn", name=f"pool_in_{tag}")
    x_new, p, y_pre = _pool_fwd(u, x, w_group, scale, name=f"pool_mix_{tag}")
    return x_new, (x, h, p, y_pre)


def _pool_mixer_bwd(dx, saved, gain, w_in, w_group, scale, tag):
    x, h, p, y_pre = saved
    du, dw_group, dscale = _pool_bwd(dx, p, y_pre, w_group, scale, name=f"pool_dmix_{tag}")
    dw_in = _matmul_tn(h, du, name=f"pool_dwin_{tag}")
    dh = _matmul(du, w_in, "nt", name=f"pool_dh_{tag}")
    dx_in, dgain = _rms_bwd(x, gain, dh, dx, name=f"pool_dnorm_{tag}")
    return dx_in, dw_in, dw_group, dscale, dgain


CONV_HALO = 8
HIGHEST = lax.Precision.HIGHEST
NEG_BIG = -1e30


def _softplus(v):
    return jnp.maximum(v, 0.0) + jnp.log(1.0 + jnp.exp(-jnp.abs(v)))


def _dot_exact(a, b):
    return jnp.dot(a, b, precision=HIGHEST, preferred_element_type=F32)


def _conv_taps(ext, w_ref, off, rows):
    acc = None
    for k in range(SSD_CONV):
        shift = SSD_CONV - 1 - k
        v = (_shift_rows(ext, shift) if shift else ext)[off:off + rows] * w_ref[k:k + 1, :]
        acc = v if acc is None else acc + v
    return acc


def _ssd_conv_fwd(zx, conv_w, conv_b, col0, *, name):
    t = zx.shape[0]
    c = conv_w.shape[1]
    tm, tc = _tile(t, 512, CONV_HALO), _tile(c, 512, LANES)
    hb, cb0 = tm // CONV_HALO, col0 // tc
    assert col0 % tc == 0

    def body(x_ref, halo_ref, w_ref, b_ref, o_ref):
        halo = jnp.where(pl.program_id(0) > 0, halo_ref[...], 0.0)
        ext = jnp.concatenate([halo, x_ref[...]], axis=0)
        pre = _conv_taps(ext, w_ref, CONV_HALO, tm) + b_ref[...]
        o_ref[...] = pre * _sigmoid(pre)

    return pl.pallas_call(
        body, name=name, grid=(t // tm, c // tc),
        in_specs=[pl.BlockSpec((tm, tc), lambda i, j: (i, j + cb0)),
                  pl.BlockSpec((CONV_HALO, tc), lambda i, j: (jnp.maximum(i * hb - 1, 0), j + cb0)),
                  pl.BlockSpec((SSD_CONV, tc), lambda i, j: (0, j)), pl.BlockSpec((1, tc), lambda i, j: (0, j))],
        out_specs=pl.BlockSpec((tm, tc), lambda i, j: (i, j)),
        out_shape=jax.ShapeDtypeStruct((t, c), F32),
        compiler_params=_params("parallel", "parallel"),
    )(zx, zx, conv_w, conv_b)


def _ssd_conv_bwd(dxa, zx, conv_w, conv_b, col0, *, name):
    t = zx.shape[0]
    c = conv_w.shape[1]
    tm, tc = _tile(t, 512, CONV_HALO), _tile(c, 512, LANES)
    hb, cb0, nt = tm // CONV_HALO, col0 // tc, t // tm
    last_halo = t // CONV_HALO - 1

    def body(x_ref, prev_ref, nxt_ref, d_ref, dnxt_ref, w_ref, b_ref, dx_ref, dw_ref, db_ref):
        i = pl.program_id(1)

        @pl.when(i == 0)
        def _():
            dw_ref[...] = jnp.zeros_like(dw_ref)
            db_ref[...] = jnp.zeros_like(db_ref)

        prev = jnp.where(i > 0, prev_ref[...], 0.0)
        has_next = i < nt - 1
        ext = jnp.concatenate([prev, x_ref[...], jnp.where(has_next, nxt_ref[...], 0.0)], axis=0)
        pre = _conv_taps(ext, w_ref, CONV_HALO, tm + CONV_HALO) + b_ref[...]
        sg = _sigmoid(pre)
        dact = jnp.concatenate([d_ref[...], jnp.where(has_next, dnxt_ref[...], 0.0)], axis=0)
        dpre = dact * (sg * (1.0 + pre * (1.0 - sg)))
        db_ref[...] += jnp.sum(dpre[:tm], axis=0, keepdims=True)
        acc = None
        for k in range(SSD_CONV):
            shift = SSD_CONV - 1 - k
            src = (_shift_rows(ext, shift) if shift else ext)[CONV_HALO:CONV_HALO + tm]
            dw_ref[k:k + 1, :] += jnp.sum(dpre[:tm] * src, axis=0, keepdims=True)
            v = (_shift_rows(dpre, -shift) if shift else dpre)[:tm] * w_ref[k:k + 1, :]
            acc = v if acc is None else acc + v
        dx_ref[...] = acc.astype(BF16)

    main = lambda j, i: (i, j + cb0)
    return pl.pallas_call(
        body, name=name, grid=(c // tc, nt),
        in_specs=[pl.BlockSpec((tm, tc), main),
                  pl.BlockSpec((CONV_HALO, tc), lambda j, i: (jnp.maximum(i * hb - 1, 0), j + cb0)),
                  pl.BlockSpec((CONV_HALO, tc), lambda j, i: (jnp.minimum((i + 1) * hb, last_halo), j + cb0)),
                  pl.BlockSpec((tm, tc), lambda j, i: (i, j)),
                  pl.BlockSpec((CONV_HALO, tc), lambda j, i: (jnp.minimum((i + 1) * hb, last_halo), j)),
                  pl.BlockSpec((SSD_CONV, tc), lambda j, i: (0, j)), pl.BlockSpec((1, tc), lambda j, i: (0, j))],
        out_specs=[pl.BlockSpec((tm, tc), lambda j, i: (i, j)), pl.BlockSpec((SSD_CONV, tc), lambda j, i: (0, j)),
                   pl.BlockSpec((1, tc), lambda j, i: (0, j))],
        out_shape=[jax.ShapeDtypeStruct((t, c), BF16), jax.ShapeDtypeStruct((SSD_CONV, c), F32),
                   jax.ShapeDtypeStruct((1, c), F32)],
        compiler_params=_params("parallel", "arbitrary"),
    )(zx, zx, zx, dxa, dxa, conv_w, conv_b)


SSD_CUMSUM_PIECES = 3


def _ssd_group_pad(v, n_groups):
    lead = v.shape[:-1]
    v = v.reshape(*lead, n_groups, SSD_HEADS_PER_GROUP)
    v = jnp.pad(v, [(0, 0)] * (len(lead) + 1) + [(0, LANES - SSD_HEADS_PER_GROUP)])
    return v.reshape(*lead, n_groups * LANES)


def _ssd_group_unpad(v, n_groups):
    lead = v.shape[:-1]
    return v.reshape(*lead, n_groups, LANES)[..., :SSD_HEADS_PER_GROUP].reshape(*lead, -1)


def _ssd_chunk_common(dtp_ref, par_ref):
    ell = SSD_CHUNK
    dt = _softplus(dtp_ref[...] + par_ref[0:1, :])
    a = -jnp.exp(par_ref[1:2, :])
    row = lax.broadcasted_iota(jnp.int32, (ell, ell), 0)
    col = lax.broadcasted_iota(jnp.int32, (ell, ell), 1)
    acum = _split_dot(dt * a, (row >= col).astype(BF16), SSD_CUMSUM_PIECES, left=True)
    return dt, a, acum, acum.T, row, col


def _ssd_scan_fwd(xa, dtp, par, n_groups, *, name):
    t = xa.shape[0]
    ell, hd, hpg, ns = SSD_CHUNK, SSD_HEAD_DIM, SSD_HEADS_PER_GROUP, SSD_STATE
    gw = hpg * hd
    nc = t // ell
    b_blk0, c_blk0 = n_groups * gw // ns, n_groups * gw // ns + n_groups

    def body(xs_ref, b_ref, c_ref, dtp_ref, par_ref, y_ref, sin_ref, st):
        @pl.when(pl.program_id(1) == 0)
        def _():
            st[...] = jnp.zeros_like(st)

        dt, _, acum, acum_t, row, col = _ssd_chunk_common(dtp_ref, par_ref)
        bb, cc = b_ref[...].astype(BF16), c_ref[...].astype(BF16)
        cb = lax.dot_general(cc, bb, NT_DIMS, preferred_element_type=F32)
        for hh in range(hpg):
            lanes = slice(hh * hd, (hh + 1) * hd)
            col_a, row_a = acum[:, hh:hh + 1], acum_t[hh:hh + 1, :]
            decay = jnp.exp(jnp.where(row >= col, col_a - row_a, NEG_BIG))
            xdt = xs_ref[:, lanes] * dt[:, hh:hh + 1]
            s_h = st[hh]
            sin_ref[lanes, :] = s_h
            y = jnp.dot((cb * decay).astype(BF16), xdt.astype(BF16), preferred_element_type=F32)
            y += jnp.exp(col_a) * lax.dot_general(cc, s_h.astype(BF16), NT_DIMS, preferred_element_type=F32)
            y_ref[:, lanes] = y
            a_last = acum[ell - 1:ell, hh:hh + 1]
            w = jnp.exp(a_last - col_a)
            st[hh] = jnp.exp(a_last) * s_h + lax.dot_general(
                (xdt * w).astype(BF16), bb, (((0,), (0,)), ((), ())), preferred_element_type=F32)

    return pl.pallas_call(
        body, name=name, grid=(n_groups, nc),
        in_specs=[pl.BlockSpec((ell, gw), lambda g, c: (c, g)),
                  pl.BlockSpec((ell, ns), lambda g, c: (c, b_blk0 + g)),
                  pl.BlockSpec((ell, ns), lambda g, c: (c, c_blk0 + g)),
                  pl.BlockSpec((ell, LANES), lambda g, c: (c, g)),
                  pl.BlockSpec((SUBLANES, LANES), lambda g, c: (0, g))],
        out_specs=[pl.BlockSpec((ell, gw), lambda g, c: (c, g)),
                   pl.BlockSpec((None, None, gw, ns), lambda g, c: (c, g, 0, 0))],
        out_shape=[jax.ShapeDtypeStruct((t, n_groups * gw), F32),
                   jax.ShapeDtypeStruct((nc, n_groups, gw, ns), F32)],
        scratch_shapes=[pltpu.VMEM((hpg, hd, ns), F32)],
        compiler_params=_params("parallel", "arbitrary"),
    )(xa, xa, xa, dtp, par)


def _ssd_scan_bwd(dy, xa, dtp, par, s_in, n_groups, *, name):
    t = xa.shape[0]
    ell, hd, hpg, ns = SSD_CHUNK, SSD_HEAD_DIM, SSD_HEADS_PER_GROUP, SSD_STATE
    gw = hpg * hd
    nc = t // ell
    b_blk0, c_blk0 = n_groups * gw // ns, n_groups * gw // ns + n_groups
    nt_dims = (((1,), (1,)), ((), ()))
    tn_dims = (((0,), (0,)), ((), ()))

    def body(dy_ref, xs_ref, b_ref, c_ref, dtp_ref, par_ref, sin_ref,
             dxs_ref, db_ref, dc_ref, ddtp_ref, dpar_ref, dst):
        @pl.when(pl.program_id(1) == 0)
        def _():
            dst[...] = jnp.zeros_like(dst)
            dpar_ref[...] = jnp.zeros_like(dpar_ref)

        dtg, a_g, acum, acum_t, row, col = _ssd_chunk_common(dtp_ref, par_ref)
        bb, cc = b_ref[...].astype(BF16), c_ref[...].astype(BF16)
        cb = lax.dot_general(cc, bb, nt_dims, preferred_element_type=F32)
        cb_t = lax.dot_general(bb, cc, nt_dims, preferred_element_type=F32)
        lane = lax.broadcasted_iota(jnp.int32, (1, LANES), 1)
        dcb = jnp.zeros((ell, ell), F32)
        dcb_t = jnp.zeros((ell, ell), F32)
        dc = jnp.zeros((ell, ns), F32)
        db = jnp.zeros((ell, ns), F32)
        dacum = jnp.zeros((ell, LANES), F32)
        xsum = jnp.zeros((ell, LANES), F32)
        dsum = jnp.zeros((1, LANES), F32)
        for hh in range(hpg):
            lanes = slice(hh * hd, (hh + 1) * hd)
            onehot = (lane == hh).astype(F32)
            col_a, row_a = acum[:, hh:hh + 1], acum_t[hh:hh + 1, :]
            decay = jnp.exp(jnp.where(row >= col, col_a - row_a, NEG_BIG))
            decay_t = jnp.exp(jnp.where(col >= row, row_a - col_a, NEG_BIG))
            e_col = jnp.exp(col_a)
            a_last = acum[ell - 1:ell, hh:hh + 1]
            w = jnp.exp(a_last - col_a)
            e_last = jnp.exp(a_last)
            xs_h, dy_h = xs_ref[:, lanes], dy_ref[:, lanes]
            dt_h = dtg[:, hh:hh + 1]
            xdt = xs_h * dt_h
            xdt_b, dy_b = xdt.astype(BF16), dy_h.astype(BF16)
            s_h, ds_h = sin_ref[lanes, :], dst[hh]
            dm_decay = lax.dot_general(dy_b, xdt_b, nt_dims, preferred_element_type=F32) * decay
            dm_decay_t = lax.dot_general(xdt_b, dy_b, nt_dims, preferred_element_type=F32) * decay_t
            dcb += dm_decay
            dcb_t += dm_decay_t
            m_t = cb_t * decay_t
            dac = jnp.sum(dm_decay * cb, axis=1, keepdims=True) - jnp.sum(dm_decay_t * cb_t, axis=1, keepdims=True)
            b_ds = lax.dot_general(bb, ds_h.astype(BF16), nt_dims, preferred_element_type=F32)
            dxdt = jnp.dot(m_t.astype(BF16), dy_b, preferred_element_type=F32) + w * b_ds
            c_s = lax.dot_general(cc, s_h.astype(BF16), nt_dims, preferred_element_type=F32)
            dc += e_col * jnp.dot(dy_b, s_h.astype(BF16), preferred_element_type=F32)
            db += w * jnp.dot(xdt_b, ds_h.astype(BF16), preferred_element_type=F32)
            dac += jnp.sum(dy_h * c_s, axis=1, keepdims=True) * e_col
            q = jnp.sum(xdt * b_ds, axis=1, keepdims=True) * w
            dac -= q
            d_last = jnp.sum(q, axis=0, keepdims=True) + e_last * jnp.sum(
                jnp.sum(s_h * ds_h, axis=1, keepdims=True), axis=0, keepdims=True)
            is_last = lax.broadcasted_iota(jnp.int32, (ell, 1), 0) == ell - 1
            dac += jnp.where(is_last, d_last, 0.0)
            dacum += dac * onehot
            dst[hh] = e_last * ds_h + lax.dot_general((dy_h * e_col).astype(BF16), cc, tn_dims,
                                                      preferred_element_type=F32)
            dxs_ref[:, lanes] = dxdt * dt_h + dy_h * par_ref[2:3, hh:hh + 1]
            xsum += jnp.sum(dxdt * xs_h, axis=1, keepdims=True) * onehot
            dsum += jnp.sum(jnp.sum(dy_h * xs_h, axis=1, keepdims=True), axis=0, keepdims=True) * onehot
        dc_ref[...] = dc + jnp.dot(dcb.astype(BF16), bb, preferred_element_type=F32)
        db_ref[...] = db + jnp.dot(dcb_t.astype(BF16), cc, preferred_element_type=F32)
        dda = _split_dot(dacum, (col >= row).astype(BF16), SSD_CUMSUM_PIECES, left=True)
        ddtp = (xsum + dda * a_g) * _sigmoid(dtp_ref[...] + par_ref[0:1, :])
        ddtp_ref[...] = ddtp
        dpar_ref[0:1, :] += jnp.sum(ddtp, axis=0, keepdims=True)
        dpar_ref[1:2, :] += jnp.sum(dda * dtg, axis=0, keepdims=True) * a_g
        dpar_ref[2:3, :] += dsum

    rev = lambda i: nc - 1 - i
    return pl.pallas_call(
        body, name=name, grid=(n_groups, nc),
        in_specs=[pl.BlockSpec((ell, gw), lambda g, i: (rev(i), g)),
                  pl.BlockSpec((ell, gw), lambda g, i: (rev(i), g)),
                  pl.BlockSpec((ell, ns), lambda g, i: (rev(i), b_blk0 + g)),
                  pl.BlockSpec((ell, ns), lambda g, i: (rev(i), c_blk0 + g)),
                  pl.BlockSpec((ell, LANES), lambda g, i: (rev(i), g)),
                  pl.BlockSpec((SUBLANES, LANES), lambda g, i: (0, g)),
                  pl.BlockSpec((None, None, gw, ns), lambda g, i: (rev(i), g, 0, 0))],
        out_specs=[pl.BlockSpec((ell, gw), lambda g, i: (rev(i), g)),
                   pl.BlockSpec((ell, ns), lambda g, i: (rev(i), g)),
                   pl.BlockSpec((ell, ns), lambda g, i: (rev(i), g)),
                   pl.BlockSpec((ell, LANES), lambda g, i: (rev(i), g)),
                   pl.BlockSpec((SUBLANES, LANES), lambda g, i: (0, g))],
        out_shape=[jax.ShapeDtypeStruct((t, n_groups * gw), F32), jax.ShapeDtypeStruct((t, n_groups * ns), F32),
                   jax.ShapeDtypeStruct((t, n_groups * ns), F32), jax.ShapeDtypeStruct((t, n_groups * LANES), F32),
                   jax.ShapeDtypeStruct((SUBLANES, n_groups * LANES), F32)],
        scratch_shapes=[pltpu.VMEM((hpg, hd, ns), F32)],
        compiler_params=_params("parallel", "arbitrary"),
    )(dy, xa, xa, xa, dtp, par, s_in)


def _ssd_gate_fwd(y, xa, zx, d_rep, out_norm, *, name):
    t, di = y.shape
    gw = SSD_HEADS_PER_GROUP * SSD_HEAD_DIM
    tm = _tile(t, 512, SUBLANES)

    def body(y_ref, xs_ref, z_ref, d_ref, n_ref, o_ref):
        zv = z_ref[...]
        gt = (y_ref[...] + d_ref[...] * xs_ref[...]) * (zv * _sigmoid(zv))
        r = lax.rsqrt(jnp.mean(gt * gt, axis=-1, keepdims=True) + NORM_EPS)
        o_ref[...] = (gt * r * n_ref[...]).astype(BF16)

    blk = pl.BlockSpec((tm, gw), lambda i, g: (i, g))
    vec = pl.BlockSpec((1, gw), lambda i, g: (0, g))
    return pl.pallas_call(
        body, name=name, grid=(t // tm, di // gw),
        in_specs=[blk, blk, blk, vec, vec], out_specs=blk,
        out_shape=jax.ShapeDtypeStruct((t, di), BF16),
        compiler_params=_params("parallel", "parallel"),
    )(y, xa, zx, d_rep, out_norm)


def _ssd_gate_bwd(dgn, y, xa, zx, d_rep, out_norm, *, name):
    t, di = y.shape
    gw = SSD_HEADS_PER_GROUP * SSD_HEAD_DIM
    tm = _tile(t, 512, SUBLANES)

    def body(dg_ref, y_ref, xs_ref, z_ref, d_ref, n_ref, dy_ref, dz_ref, dn_ref):
        @pl.when(pl.program_id(1) == 0)
        def _():
            dn_ref[...] = jnp.zeros_like(dn_ref)

        zv = z_ref[...]
        sg = _sigmoid(zv)
        sz = zv * sg
        y2 = y_ref[...] + d_ref[...] * xs_ref[...]
        gt = y2 * sz
        r = lax.rsqrt(jnp.mean(gt * gt, axis=-1, keepdims=True) + NORM_EPS)
        ghat = gt * r
        dgv = dg_ref[...]
        dn_ref[...] += jnp.sum(dgv * ghat, axis=0, keepdims=True)
        u = dgv * n_ref[...]
        dgt = r * (u - ghat * jnp.mean(u * ghat, axis=-1, keepdims=True))
        dy_ref[...] = dgt * sz
        dz_ref[...] = (dgt * y2 * (sg * (1.0 + zv * (1.0 - sg)))).astype(BF16)

    blk = pl.BlockSpec((tm, gw), lambda g, i: (i, g))
    vec = pl.BlockSpec((1, gw), lambda g, i: (0, g))
    return pl.pallas_call(
        body, name=name, grid=(di // gw, t // tm),
        in_specs=[blk, blk, blk, blk, vec, vec], out_specs=[blk, blk, vec],
        out_shape=[jax.ShapeDtypeStruct((t, di), F32), jax.ShapeDtypeStruct((t, di), BF16),
                   jax.ShapeDtypeStruct((1, di), F32)],
        compiler_params=_params("parallel", "arbitrary"),
    )(dgn, y, xa, zx, d_rep, out_norm)


def _ssd_mixer_fwd(x, gain, w_zx_t, w_dt_t, conv_w, conv_b, par, d_rep, out_norm, w_out, tag):
    di = w_out.shape[0]
    n_groups = di // (SSD_HEADS_PER_GROUP * SSD_HEAD_DIM)
    h = _rms_fwd(x, gain, name=f"ssd_norm_{tag}")
    zx = _matmul(h, w_zx_t, "nt", name=f"ssd_in_{tag}")
    dtp = _matmul(h, w_dt_t, "nt", name=f"ssd_dt_{tag}")
    xa = _ssd_conv_fwd(zx, conv_w, conv_b, di, name=f"ssd_conv_{tag}")
    y, s_in = _ssd_scan_fwd(xa, dtp, par, n_groups, name=f"ssd_scan_{tag}")
    gn = _ssd_gate_fwd(y, xa, zx, d_rep, out_norm, name=f"ssd_gate_{tag}")
    x_new = _matmul(gn, w_out, "nn", resid=x, name=f"ssd_out_{tag}")
    return x_new, (x, h, zx, dtp, xa, y, s_in, gn)


def _ssd_mixer_bwd(dx, saved, gain, w_zx_t, w_dt_t, conv_w, conv_b, par, d_rep, out_norm, w_out, tag):
    x, h, zx, dtp, xa, y, s_in, gn = saved
    di = w_out.shape[0]
    n_groups = di // (SSD_HEADS_PER_GROUP * SSD_HEAD_DIM)
    dgn = _matmul(dx, w_out, "nt", name=f"ssd_dgn_{tag}")
    dw_out = _matmul_tn(gn, dx, name=f"ssd_dwout_{tag}")
    dy2, dz, dnorm = _ssd_gate_bwd(dgn, y, xa, zx, d_rep, out_norm, name=f"ssd_dgate_{tag}")
    dxs, db, dc, ddtp, dpar = _ssd_scan_bwd(dy2, xa, dtp, par, s_in, n_groups, name=f"ssd_dscan_{tag}")
    dxa = jnp.concatenate([dxs, db, dc], axis=1)
    dxbc, dconv_w, dconv_b = _ssd_conv_bwd(dxa, zx, conv_w, conv_b, di, name=f"ssd_dconv_{tag}")
    dzx = jnp.concatenate([dz, dxbc], axis=1)
    dw_zx_t = _matmul_tn(dzx, h, name=f"ssd_dwin_{tag}")
    dw_dt_t = _matmul_tn(ddtp, h, name=f"ssd_dwdt_{tag}")
    dh = _matmul(dzx, w_zx_t, "nn", name=f"ssd_dh_{tag}")
    dh = _matmul(ddtp, w_dt_t, "nn", resid=dh, name=f"ssd_dhdt_{tag}")
    dx_in, dgain = _rms_bwd(x, gain, dh, dx, name=f"ssd_dnorm_{tag}")
    return dx_in, dw_zx_t, dw_dt_t, dconv_w, dconv_b, dpar, dnorm, dw_out, dgain


def _sb_qk_norm_fwd(qkv, gains, *, name):
    ns, t, _ = qkv.shape
    per = ns // 3
    tm = _tile(t, 1024, SUBLANES)
    inv_sqrt_d = 1.0 / math.sqrt(SB_HEAD_DIM)

    def body(x_ref, g_ref, o_ref):
        kind = pl.program_id(0) // per
        xv = x_ref[...]

        @pl.when(kind == 2)
        def _():
            o_ref[...] = xv.astype(BF16)

        @pl.when(kind < 2)
        def _():
            left = lax.broadcasted_iota(jnp.int32, (1, LANES), 1) < SB_HEAD_DIM
            sq = xv * xv
            ms = jnp.where(left, jnp.sum(jnp.where(left, sq, 0.0), axis=1, keepdims=True),
                           jnp.sum(jnp.where(left, 0.0, sq), axis=1, keepdims=True)) * (1.0 / SB_HEAD_DIM)
            y = xv * lax.rsqrt(ms + NORM_EPS) * g_ref[pl.ds(kind, 1), :]
            o_ref[...] = (y * jnp.where(kind == 0, inv_sqrt_d, 1.0)).astype(BF16)

    blk = pl.BlockSpec((None, tm, LANES), lambda s, i: (s, i, 0))
    return pl.pallas_call(
        body, name=name, grid=(ns, t // tm),
        in_specs=[blk, pl.BlockSpec((SUBLANES, LANES), lambda s, i: (0, 0))], out_specs=blk,
        out_shape=jax.ShapeDtypeStruct((ns, t, LANES), BF16),
        compiler_params=_params("parallel", "parallel"),
    )(qkv, gains)


def _sb_qk_norm_bwd(dq, dk, dv, qkv, gains, *, name):
    ns, t, _ = qkv.shape
    per = ns // 3
    tm = _tile(t, 1024, SUBLANES)
    inv_sqrt_d = 1.0 / math.sqrt(SB_HEAD_DIM)

    def body(dq_ref, dk_ref, dv_ref, x_ref, g_ref, o_ref, dg_ref):
        s = pl.program_id(0)
        kind = s // per

        @pl.when((s == 0) & (pl.program_id(1) == 0))
        def _():
            dg_ref[...] = jnp.zeros_like(dg_ref)

        @pl.when(kind == 2)
        def _():
            o_ref[...] = dv_ref[...].astype(BF16)

        @pl.when(kind < 2)
        def _():
            xv = x_ref[...]
            dy = jnp.where(kind == 0, dq_ref[...] * inv_sqrt_d, dk_ref[...])
            left = lax.broadcasted_iota(jnp.int32, (1, LANES), 1) < SB_HEAD_DIM

            def halves(v):
                return jnp.where(left, jnp.sum(jnp.where(left, v, 0.0), axis=1, keepdims=True),
                                 jnp.sum(jnp.where(left, 0.0, v), axis=1, keepdims=True))

            r = lax.rsqrt(halves(xv * xv) * (1.0 / SB_HEAD_DIM) + NORM_EPS)
            xhat = xv * r
            u = dy * g_ref[pl.ds(kind, 1), :]
            o_ref[...] = (r * (u - xhat * halves(u * xhat) * (1.0 / SB_HEAD_DIM))).astype(BF16)
            dg_ref[pl.ds(kind, 1), :] += jnp.sum(dy * xhat, axis=0, keepdims=True)

    def grad_blk(kind):
        return pl.BlockSpec((None, tm, LANES), lambda s, i: (jnp.clip(s - kind * per, 0, per - 1), i, 0))

    blk = pl.BlockSpec((None, tm, LANES), lambda s, i: (s, i, 0))
    vec = pl.BlockSpec((SUBLANES, LANES), lambda s, i: (0, 0))
    return pl.pallas_call(
        body, name=name, grid=(ns, t // tm),
        in_specs=[grad_blk(0), grad_blk(1), grad_blk(2), blk, vec], out_specs=[blk, vec],
        out_shape=[jax.ShapeDtypeStruct((ns, t, LANES), BF16), jax.ShapeDtypeStruct((SUBLANES, LANES), F32)],
        compiler_params=_params("arbitrary", "arbitrary"),
    )(dq, dk, dv, qkv, gains)


def _split_dot(v, ones_mat, pieces, left=False):
    total, rest = None, v
    for p in range(pieces):
        part = rest.astype(BF16)
        if p + 1 < pieces:
            rest = rest - part.astype(F32)
        d = (jnp.dot(ones_mat, part, preferred_element_type=F32) if left
             else jnp.dot(part, ones_mat, preferred_element_type=F32))
        total = d if total is None else total + d
    return total


LOGIT_SUM_PIECES = 3
GRAD_SUM_PIECES = 2
LOG_WEIGHT_UNDERFLOW = -105.0


def _sb_attn_fwd(qkv_n, n_heads, *, name):
    ns, t, _ = qkv_n.shape
    per = ns // 3
    blk, hd = SB_BLOCK, SB_HEAD_DIM
    nq = t // blk

    def body(q_ref, k_ref, v_ref, o_ref):
        i = pl.program_id(1)
        row = lax.broadcasted_iota(jnp.int32, (blk, blk), 0)
        col = lax.broadcasted_iota(jnp.int32, (blk, blk), 1)
        later_keys = (row > col).astype(BF16)

        def tile(kb, carry, diag):
            out = []
            start = pl.multiple_of(kb * blk, blk)
            for hf in range(2):
                lanes = slice(hf * hd, (hf + 1) * hd)
                run, acc = carry[hf]
                z = lax.dot_general(q_ref[:, lanes], k_ref[pl.ds(start, blk), lanes], (((1,), (1,)), ((), ())),
                                    preferred_element_type=F32)
                sp = _softplus(z)
                lm = jnp.where(col < row, -sp, 0.0) if diag else -sp
                after = _split_dot(lm, later_keys, LOGIT_SUM_PIECES) + run
                a = jnp.exp(z - sp + after)
                if diag:
                    a = jnp.where(col < row, a, 0.0)
                acc = acc + jnp.dot(a.astype(BF16), v_ref[pl.ds(start, blk), lanes], preferred_element_type=F32)
                out.append((run + jnp.sum(lm, axis=1, keepdims=True), acc))
            return tuple(out)

        def live(carry):
            return jnp.max(jnp.maximum(carry[0][0], carry[1][0])) > LOG_WEIGHT_UNDERFLOW

        def step(state):
            s, _, carry = state
            carry = tile(i - 1 - s, carry, False)
            return s + 1, live(carry), carry

        init = tuple((jnp.zeros((blk, 1), F32), jnp.zeros((blk, hd), F32)) for _ in range(2))
        carry = tile(i, init, True)
        _, _, carry = lax.while_loop(lambda st: (st[0] < i) & st[1], step, (jnp.int32(0), live(carry), carry))
        o_ref[...] = jnp.concatenate([carry[0][1], carry[1][1]], axis=1)

    return pl.pallas_call(
        body, name=name, grid=(per, nq),
        in_specs=[pl.BlockSpec((None, blk, LANES), lambda p, i: (p, i, 0)),
                  pl.BlockSpec((None, t, LANES), lambda p, i: (per + p, 0, 0)),
                  pl.BlockSpec((None, t, LANES), lambda p, i: (2 * per + p, 0, 0))],
        out_specs=pl.BlockSpec((blk, LANES), lambda p, i: (i, p)),
        out_shape=jax.ShapeDtypeStruct((t, n_heads * hd), F32),
        compiler_params=_params("parallel", "arbitrary"),
    )(qkv_n, qkv_n, qkv_n)


def _sb_attn_bwd(do, qkv_n, *, name):
    ns, t, _ = qkv_n.shape
    per = ns // 3
    blk, hd = SB_BLOCK, SB_HEAD_DIM
    nq = t // blk
    nt_dims = (((1,), (1,)), ((), ()))
    tn_dims = (((0,), (0,)), ((), ()))

    def body(q_ref, k_ref, v_ref, do_ref, dq_ref, dk_ref, dv_ref):
        i = pl.program_id(1)

        @pl.when(i == 0)
        def _():
            dk_ref[...] = jnp.zeros_like(dk_ref)
            dv_ref[...] = jnp.zeros_like(dv_ref)

        key = lax.broadcasted_iota(jnp.int32, (blk, blk), 0)
        qry = lax.broadcasted_iota(jnp.int32, (blk, blk), 1)
        later_keys = (qry > key).astype(BF16)
        earlier_keys = (qry < key).astype(BF16)
        halves = [slice(hf * hd, (hf + 1) * hd) for hf in range(2)]
        q_hs = [q_ref[:, lanes] for lanes in halves]
        do_bs = [do_ref[:, lanes].astype(BF16) for lanes in halves]

        def scores(kb, hf, diag):
            k_blk = k_ref[pl.ds(pl.multiple_of(kb * blk, blk), blk), halves[hf]]
            z = lax.dot_general(k_blk, q_hs[hf], nt_dims, preferred_element_type=F32)
            sp = _softplus(z)
            return k_blk, z, sp, (jnp.where(key < qry, -sp, 0.0) if diag else -sp)

        def column_sums(kb, diag):
            return [jnp.sum(scores(kb, hf, diag)[3], axis=0, keepdims=True) for hf in range(2)]

        def live(tots):
            return jnp.max(jnp.maximum(tots[0], tots[1])) > LOG_WEIGHT_UNDERFLOW

        def reach(state):
            s, _, tots = state
            more = column_sums(i - 1 - s, False)
            tots = [tots[0] + more[0], tots[1] + more[1]]
            return s + 1, live(tots), tots

        tots = column_sums(i, True)
        reached, _, tots = lax.while_loop(lambda st: (st[0] < i) & st[1], reach, (jnp.int32(0), live(tots), tots))

        def tile(kb, carry, diag):
            out = []
            start = pl.multiple_of(kb * blk, blk)
            for hf, lanes in enumerate(halves):
                seen, gsum, dq = carry[hf]
                q_h, do_b = q_hs[hf], do_bs[hf]
                k_blk, z, sp, lm = scores(kb, hf, diag)
                blk_tot = jnp.sum(lm, axis=0, keepdims=True)
                after = _split_dot(lm, later_keys, LOGIT_SUM_PIECES, left=True) + (tots[hf] - seen - blk_tot)
                a = jnp.exp(z - sp + after)
                if diag:
                    a = jnp.where(key < qry, a, 0.0)
                da = lax.dot_general(v_ref[pl.ds(start, blk), lanes], do_b, nt_dims, preferred_element_type=F32)
                g = da * a
                before = _split_dot(g, earlier_keys, GRAD_SUM_PIECES, left=True) + gsum
                omb = jnp.exp(-sp)
                dz = g * omb - (1.0 - omb) * before
                if diag:
                    dz = jnp.where(key < qry, dz, 0.0)
                dz_b = dz.astype(BF16)
                dk_ref[pl.ds(start, blk), lanes] += jnp.dot(dz_b, q_h, preferred_element_type=F32)
                dv_ref[pl.ds(start, blk), lanes] += jnp.dot(a.astype(BF16), do_b, preferred_element_type=F32)
                dq = dq + lax.dot_general(dz_b, k_blk, tn_dims, preferred_element_type=F32)
                out.append((seen + blk_tot, gsum + jnp.sum(g, axis=0, keepdims=True), dq))
            return tuple(out)

        init = tuple((jnp.zeros((1, blk), F32), jnp.zeros((1, blk), F32), jnp.zeros((blk, hd), F32))
                     for _ in range(2))
        carry = lax.fori_loop(i - reached, i, lambda kb, c: tile(kb, c, False), init)
        carry = tile(i, carry, True)
        dq_ref[...] = jnp.concatenate([carry[0][2], carry[1][2]], axis=1)

    full = lambda off: pl.BlockSpec((None, t, LANES), lambda p, i: (off + p, 0, 0))
    q_blk = pl.BlockSpec((None, blk, LANES), lambda p, i: (p, i, 0))
    slab = jax.ShapeDtypeStruct((per, t, LANES), F32)
    return pl.pallas_call(
        body, name=name, grid=(per, nq),
        in_specs=[q_blk, full(per), full(2 * per), pl.BlockSpec((blk, LANES), lambda p, i: (i, p))],
        out_specs=[q_blk, full(0), full(0)],
        out_shape=[slab, slab, slab],
        compiler_params=_params("parallel", "arbitrary"),
    )(qkv_n, qkv_n, qkv_n, do)


def _sb_mixer_fwd(x, gain, w_qkv_t, qk_gains, w_out, tag):
    n_heads = w_out.shape[0] // SB_HEAD_DIM
    h = _rms_fwd(x, gain, name=f"sb_norm_{tag}")
    qkv = _matmul(h, w_qkv_t, "nt", out_slabs=True, tn_cap=256, name=f"sb_qkv_{tag}")
    qkv_n = _sb_qk_norm_fwd(qkv, qk_gains, name=f"sb_qknorm_{tag}")
    o = _sb_attn_fwd(qkv_n, n_heads, name=f"sb_attn_{tag}")
    x_new = _matmul(o, w_out, "nn", resid=x, name=f"sb_out_{tag}")
    return x_new, (x, h, qkv, qkv_n, o)


def _sb_mixer_bwd(dx, saved, gain, w_qkv_t, qk_gains, w_out, tag):
    x, h, qkv, qkv_n, o = saved
    do = _matmul(dx, w_out, "nt", name=f"sb_do_{tag}")
    dw_out = _matmul_tn(o, dx, name=f"sb_dwout_{tag}")
    dq, dk, dv = _sb_attn_bwd(do, qkv_n, name=f"sb_dattn_{tag}")
    dqkv, dqk_gains = _sb_qk_norm_bwd(dq, dk, dv, qkv, qk_gains, name=f"sb_dqknorm_{tag}")
    dw_qkv_t = _matmul_tn(dqkv, h, a_slabs=True, name=f"sb_dwqkv_{tag}")
    dh = _matmul(dqkv, w_qkv_t, "nn", a_slabs=True, name=f"sb_dh_{tag}")
    dx_in, dgain = _rms_bwd(x, gain, dh, dx, name=f"sb_dnorm_{tag}")
    return dx_in, dw_qkv_t, dqk_gains, dw_out, dgain


MESH = pl.DeviceIdType.MESH


def _position():
    return lax.axis_index("x"), lax.axis_index("y"), lax.axis_index("c")


def _all_gather(shard, *, name, in_vmem):
    rows, n = shard.shape
    space = pltpu.VMEM if in_vmem else pltpu.HBM

    def body(x_ref, out_ref, send_sems, recv_sems, local_sem):
        x, y, c = _position()
        me, sibling = (x, y, c), (x, y, 1 - c)
        chips = [(1 - x, y), (x, 1 - y), (1 - x, 1 - y)]

        def block(px, py, pc):
            return out_ref.at[4 * px + 2 * py + pc]

        def copy(k, blk, to, src=None):
            return pltpu.make_async_remote_copy(
                src_ref=block(*blk) if src is None else src, dst_ref=block(*blk),
                send_sem=send_sems.at[k], recv_sem=recv_sems.at[k], device_id=to, device_id_type=MESH)

        mine = pltpu.make_async_copy(x_ref, block(*me), local_sem)
        mine.start()
        first = [copy(0, me, sibling, src=x_ref)]
        first += [copy(1 + j, me, (*chip, c), src=x_ref) for j, chip in enumerate(chips)]
        for cp in first:
            cp.start()
        passed = [copy(4 + j, (*chip, c), sibling) for j, chip in enumerate(chips)]
        for j, chip in enumerate(chips):
            copy(1 + j, (*chip, c), me).wait_recv()
            passed[j].start()
        copy(0, sibling, me).wait_recv()
        for j, chip in enumerate(chips):
            copy(4 + j, (*chip, 1 - c), me).wait_recv()
        for cp in first + passed:
            cp.wait_send()
        mine.wait()

    return pl.pallas_call(
        body, name=name,
        out_shape=jax.ShapeDtypeStruct((N_DEV, rows, n), shard.dtype),
        in_specs=[pl.BlockSpec(memory_space=space)], out_specs=pl.BlockSpec(memory_space=space),
        scratch_shapes=[pltpu.SemaphoreType.DMA((7,)), pltpu.SemaphoreType.DMA((7,)), pltpu.SemaphoreType.DMA],
        compiler_params=pltpu.CompilerParams(vmem_limit_bytes=V7X_VMEM_LIMIT_BYTES),
    )(shard)


def _exchange_sibling(parts, *, name):
    _, nchip, rows, n = parts.shape

    def body(p_ref, recv_ref, send_sem, recv_sem):
        x, y, c = _position()
        cp = pltpu.make_async_remote_copy(src_ref=p_ref.at[1 - c], dst_ref=recv_ref, send_sem=send_sem,
                                          recv_sem=recv_sem, device_id=(x, y, 1 - c), device_id_type=MESH)
        cp.start()
        cp.wait()

    return pl.pallas_call(
        body, name=name,
        out_shape=jax.ShapeDtypeStruct((nchip, rows, n), parts.dtype),
        in_specs=[pl.BlockSpec(memory_space=pltpu.HBM)], out_specs=pl.BlockSpec(memory_space=pltpu.HBM),
        scratch_shapes=[pltpu.SemaphoreType.DMA, pltpu.SemaphoreType.DMA],
    )(parts)


def _exchange_chips(chip_sums, *, name):
    _, rows, n = chip_sums.shape

    def body(s_ref, recv_ref, send_sems, recv_sems):
        x, y, c = _position()
        chips = [(1 - x, y), (x, 1 - y), (1 - x, 1 - y)]
        copies = [pltpu.make_async_remote_copy(
            src_ref=s_ref.at[2 * cx + cy], dst_ref=recv_ref.at[j], send_sem=send_sems.at[j],
            recv_sem=recv_sems.at[j], device_id=(cx, cy, c), device_id_type=MESH)
            for j, (cx, cy) in enumerate(chips)]
        for cp in copies:
            cp.start()
        for cp in copies:
            cp.wait()

    return pl.pallas_call(
        body, name=name,
        out_shape=jax.ShapeDtypeStruct((3, rows, n), chip_sums.dtype),
        in_specs=[pl.BlockSpec(memory_space=pltpu.HBM)], out_specs=pl.BlockSpec(memory_space=pltpu.HBM),
        scratch_shapes=[pltpu.SemaphoreType.DMA((3,)), pltpu.SemaphoreType.DMA((3,))],
    )(chip_sums)


def _add_pairs(parts, recv, c_mine, *, name):
    _, nchip, rows, n = parts.shape
    tr = _tile(rows, 512, SUBLANES)

    def body(c_ref, a_ref, b_ref, o_ref, wire_ref):
        s = a_ref[...] + b_ref[...]
        o_ref[...] = s
        wire_ref[...] = s.astype(WIRE_DTYPE)

    out_blk = pl.BlockSpec((None, tr, n), lambda k, i, c: (k, i, 0))
    return pl.pallas_call(
        body, name=name,
        grid_spec=pltpu.PrefetchScalarGridSpec(
            num_scalar_prefetch=1, grid=(nchip, rows // tr),
            in_specs=[pl.BlockSpec((None, None, tr, n), lambda k, i, c: (c[0], k, i, 0)),
                      pl.BlockSpec((None, tr, n), lambda k, i, c: (k, i, 0))],
            out_specs=[out_blk, out_blk]),
        out_shape=[jax.ShapeDtypeStruct((nchip, rows, n), parts.dtype),
                   jax.ShapeDtypeStruct((nchip, rows, n), WIRE_DTYPE)],
        compiler_params=_params("parallel", "parallel"),
    )(c_mine, parts, recv)


def _adamw_math(w, g, m, v):
    m = ADAM_B1 * m + (1.0 - ADAM_B1) * g
    v = ADAM_B2 * v + (1.0 - ADAM_B2) * (g * g)
    m_hat = m / (1.0 - ADAM_B1 ** ADAM_STEP)
    v_hat = v / (1.0 - ADAM_B2 ** ADAM_STEP)
    delta = -ADAM_LR * (m_hat / (jnp.sqrt(v_hat) + ADAM_EPS) + ADAM_WD * w)
    return delta, m, v


def _adamw_sharded(chip_sums, recv, k_mine, w, m, v, *, name):
    rows, n = w.shape
    tr = _tile(rows, 256, SUBLANES)

    def body(k_ref, s_ref, r_ref, w_ref, m_ref, v_ref, g_out, d_out, m_out, v_out):
        g = ((s_ref[...] + r_ref[0].astype(F32)) + r_ref[1].astype(F32)) + r_ref[2].astype(F32)
        delta, m_new, v_new = _adamw_math(w_ref[...], g, m_ref[...], v_ref[...])
        g_out[...] = g
        d_out[...] = delta
        m_out[...] = m_new
        v_out[...] = v_new

    blk = pl.BlockSpec((tr, n), lambda i, k: (i, 0))
    out = jax.ShapeDtypeStruct((rows, n), F32)
    return pl.pallas_call(
        body, name=name,
        grid_spec=pltpu.PrefetchScalarGridSpec(
            num_scalar_prefetch=1, grid=(rows // tr,),
            in_specs=[pl.BlockSpec((None, tr, n), lambda i, k: (k[0], i, 0)),
                      pl.BlockSpec((3, tr, n), lambda i, k: (0, i, 0)), blk, blk, blk],
            out_specs=[blk, blk, blk, blk]),
        out_shape=[out, out, out, out],
        compiler_params=_params("parallel"),
    )(k_mine, chip_sums, recv, w, m, v)


SMALL_ROWS = 40
ROW_MIX_NORM, ROW_FFN_NORM, ROW_CONV_B, ROW_OUT_NORM, ROW_POOL_SCALE, ROW_CONV_W = 0, 4, 8, 12, 14, 16
ROW_SSD_VEC, ROW_QK_GAIN, ROW_LOSS = 32, 33, 34


def _adamw_small(gathered, w, m, v, *, name):
    _, rows, n = gathered.shape

    def body(a_ref, w_ref, m_ref, v_ref, g_out, d_out, m_out, v_out):
        g = a_ref[0]
        for d in range(1, N_DEV):
            g = g + a_ref[d]
        row = lax.broadcasted_iota(jnp.int32, (rows, 1), 0)
        g = jnp.where(row == ROW_QK_GAIN, g + pltpu.roll(g, SB_HEAD_DIM, 1), g)
        g = jnp.where(row == ROW_LOSS, jnp.sum(g, axis=1, keepdims=True), g)
        g_out[...] = g
        delta, m_new, v_new = _adamw_math(w_ref[...], g, m_ref[...], v_ref[...])
        d_out[...] = delta
        m_out[...] = m_new
        v_out[...] = v_new

    out = jax.ShapeDtypeStruct((rows, n), F32)
    return pl.pallas_call(body, name=name, out_shape=[out, out, out, out])(gathered, w, m, v)


BIG_WEIGHTS = ("pool_in", "pool_group", "ssd_in", "ssd_out", "sb_qkv", "sb_out", "ffn_gate", "ffn_up", "ffn_down")
COLUMN_SHARDED = ("ssd_in", "sb_qkv", "ffn_gate", "ffn_up")
ROW_PAD = 512
WIRE_DTYPE = jnp.bfloat16


def _to_rows(name, shard, d):
    if name in COLUMN_SHARDED:
        shard = jnp.swapaxes(shard, -1, -2)
    return shard.reshape(-1, d)


def _from_rows(name, rows, shard_shape):
    if name in COLUMN_SHARDED:
        lead, k, n = shard_shape
        return jnp.swapaxes(rows.reshape(lead, n, k), -1, -2)
    return rows.reshape(shard_shape)


def _pad_rows(a, total):
    return jnp.pad(a, ((0, total - a.shape[0]),) + ((0, 0),) * (a.ndim - 1))


def _exact_bf16_rows(v, d):
    words = lax.bitcast_convert_type(v.reshape(-1), WIRE_DTYPE).reshape(-1)
    return _pad_rows(words, -(-words.shape[0] // d) * d).reshape(-1, d)


def _exact_f32(rows, count):
    words = rows.reshape(rows.shape[0], -1)[:, :2 * count].reshape(rows.shape[0], count, 2)
    return lax.bitcast_convert_type(words, F32)


def _device_blocks(name, full, lead, d):
    if name == "pool_group":
        j, g, dg, _ = full.shape
        return full.reshape(j, g, N_DEV, dg // N_DEV, dg).transpose(2, 0, 1, 3, 4).reshape(N_DEV, -1, d)
    per = full.shape[1] // N_DEV
    return full.reshape(lead, N_DEV, per, d).transpose(1, 0, 2, 3).reshape(N_DEV, lead * per, d)


def kernel(x, mix_norm, pool_in, pool_group, pool_scale, ssd_in, ssd_conv_w, ssd_conv_b, ssd_dt_bias, ssd_a_log, ssd_d, ssd_out_norm, ssd_out, sb_qkv, sb_q_norm, sb_k_norm, sb_out, ffn_norm, ffn_gate, ffn_up, ffn_down, loss_target, m_mix_norm, m_pool_in, m_pool_group, m_pool_scale, m_ssd_in, m_ssd_conv_w, m_ssd_conv_b, m_ssd_dt_bias, m_ssd_a_log, m_ssd_d, m_ssd_out_norm, m_ssd_out, m_sb_qkv, m_sb_q_norm, m_sb_k_norm, m_sb_out, m_ffn_norm, m_ffn_gate, m_ffn_up, m_ffn_down, v_mix_norm, v_pool_in, v_pool_group, v_pool_scale, v_ssd_in, v_ssd_conv_w, v_ssd_conv_b, v_ssd_dt_bias, v_ssd_a_log, v_ssd_d, v_ssd_out_norm, v_ssd_out, v_sb_qkv, v_sb_q_norm, v_sb_k_norm, v_sb_out, v_ffn_norm, v_ffn_gate, v_ffn_up, v_ffn_down):
    weights = dict(mix_norm=mix_norm, pool_in=pool_in, pool_group=pool_group, pool_scale=pool_scale, ssd_in=ssd_in,
                   ssd_conv_w=ssd_conv_w, ssd_conv_b=ssd_conv_b, ssd_dt_bias=ssd_dt_bias, ssd_a_log=ssd_a_log,
                   ssd_d=ssd_d, ssd_out_norm=ssd_out_norm, ssd_out=ssd_out, sb_qkv=sb_qkv, sb_q_norm=sb_q_norm,
                   sb_k_norm=sb_k_norm, sb_out=sb_out, ffn_norm=ffn_norm, ffn_gate=ffn_gate, ffn_up=ffn_up,
                   ffn_down=ffn_down)
    mom1 = dict(mix_norm=m_mix_norm, pool_in=m_pool_in, pool_group=m_pool_group, pool_scale=m_pool_scale,
                ssd_in=m_ssd_in, ssd_conv_w=m_ssd_conv_w, ssd_conv_b=m_ssd_conv_b, ssd_dt_bias=m_ssd_dt_bias,
                ssd_a_log=m_ssd_a_log, ssd_d=m_ssd_d, ssd_out_norm=m_ssd_out_norm, ssd_out=m_ssd_out,
                sb_qkv=m_sb_qkv, sb_q_norm=m_sb_q_norm, sb_k_norm=m_sb_k_norm, sb_out=m_sb_out,
                ffn_norm=m_ffn_norm, ffn_gate=m_ffn_gate, ffn_up=m_ffn_up, ffn_down=m_ffn_down)
    mom2 = dict(mix_norm=v_mix_norm, pool_in=v_pool_in, pool_group=v_pool_group, pool_scale=v_pool_scale,
                ssd_in=v_ssd_in, ssd_conv_w=v_ssd_conv_w, ssd_conv_b=v_ssd_conv_b, ssd_dt_bias=v_ssd_dt_bias,
                ssd_a_log=v_ssd_a_log, ssd_d=v_ssd_d, ssd_out_norm=v_ssd_out_norm, ssd_out=v_ssd_out,
                sb_qkv=v_sb_qkv, sb_q_norm=v_sb_q_norm, sb_k_norm=v_sb_k_norm, sb_out=v_sb_out,
                ffn_norm=v_ffn_norm, ffn_gate=v_ffn_gate, ffn_up=v_ffn_up, ffn_down=v_ffn_down)
    names = list(weights)
    depth, d = mix_norm.shape
    xs, ys, cs = _position()
    dev = 4 * xs + 2 * ys + cs
    chip = 2 * xs + ys

    seg = {}
    row = 0
    for name in BIG_WEIGHTS:
        n_rows = weights[name].size // d
        seg[name] = (row, n_rows)
        row += n_rows
    big_rows = row
    n_scale, n_convw = pool_scale.size, ssd_conv_w.size
    exact = jnp.concatenate([_exact_bf16_rows(pool_scale, d), _exact_bf16_rows(ssd_conv_w, d)], axis=0)
    scale_rows = _exact_bf16_rows(pool_scale, d).shape[0]
    packed_rows = -(-(big_rows + exact.shape[0]) // ROW_PAD) * ROW_PAD

    def pack(tree, dtype):
        return jnp.concatenate([_to_rows(n, tree[n], d).astype(dtype) for n in BIG_WEIGHTS], axis=0)

    w_wire = _pad_rows(jnp.concatenate([pack(weights, WIRE_DTYPE), exact], axis=0), packed_rows)
    gathered = _all_gather(w_wire, name="gather_weights", in_vmem=False)

    def seg_of(name):
        a, n = seg[name]
        return gathered[:, a:a + n]

    n_pool, n_ssd, n_sb = pool_in.shape[0], ssd_in.shape[0], sb_qkv.shape[0]
    assert n_ssd == 1 and n_sb == 1
    w_pool_in = seg_of("pool_in").reshape(N_DEV, n_pool, -1, d).transpose(1, 0, 2, 3).reshape(n_pool, d, d)
    grp = pool_group.shape
    w_pool_group = seg_of("pool_group").reshape(N_DEV, grp[0], grp[1], grp[2], grp[3]).transpose(1, 2, 0, 3, 4)
    w_pool_group = w_pool_group.reshape(grp[0], grp[1], grp[3], grp[3])
    w_ssd_in_t = seg_of("ssd_in").reshape(-1, d)
    w_ssd_out = seg_of("ssd_out").reshape(-1, d)
    w_sb_qkv_t = seg_of("sb_qkv").reshape(-1, d)
    w_sb_out = seg_of("sb_out").reshape(-1, d)
    hidden = ffn_down.shape[1] * N_DEV
    w_gate_t = seg_of("ffn_gate").reshape(N_DEV, depth, -1, d).transpose(1, 0, 2, 3).reshape(depth, hidden, d)
    w_up_t = seg_of("ffn_up").reshape(N_DEV, depth, -1, d).transpose(1, 0, 2, 3).reshape(depth, hidden, d)
    w_down = seg_of("ffn_down").reshape(N_DEV, depth, -1, d).transpose(1, 0, 2, 3).reshape(depth, hidden, d)
    exact_all = gathered[:, big_rows:big_rows + exact.shape[0]]
    scale_full = _exact_f32(exact_all[:, :scale_rows], n_scale).reshape(N_DEV, n_pool, -1)
    scale_full = scale_full.transpose(1, 0, 2).reshape(n_pool, d)
    convw_full = _exact_f32(exact_all[:, scale_rows:], n_convw).reshape(N_DEV, SSD_CONV, -1)
    convw_full = convw_full.transpose(1, 0, 2).reshape(SSD_CONV, -1)

    d_inner = w_ssd_out.shape[0]
    n_zx = w_ssd_in_t.shape[0] - ssd_dt_bias.shape[1]
    w_zx_t = w_ssd_in_t[:n_zx]
    n_ssd_heads = ssd_dt_bias.shape[1]
    n_ssd_groups = n_ssd_heads // SSD_HEADS_PER_GROUP
    w_dt_t = _ssd_group_pad(w_ssd_in_t[n_zx:].T, n_ssd_groups).T
    par = _pad_rows(_ssd_group_pad(jnp.concatenate([ssd_dt_bias, ssd_a_log, ssd_d], axis=0), n_ssd_groups), SUBLANES)
    d_rep = jnp.repeat(ssd_d[0], SSD_HEAD_DIM)[None]
    qk_gains = jnp.zeros((SUBLANES, LANES), F32).at[0].set(jnp.tile(sb_q_norm[0], 2)).at[1].set(jnp.tile(sb_k_norm[0], 2))

    act = x[0]
    saved = []
    for i in range(depth):
        kind, j = i % 3, i // 3
        gain = mix_norm[i:i + 1]
        if kind == 0:
            act, s = _pool_mixer_fwd(act, gain, w_pool_in[j], w_pool_group[j], scale_full[j:j + 1], f"l{i}")
        elif kind == 1:
            act, s = _ssd_mixer_fwd(act, gain, w_zx_t, w_dt_t, convw_full, ssd_conv_b, par, d_rep, ssd_out_norm,
                                    w_ssd_out, f"l{i}")
        else:
            act, s = _sb_mixer_fwd(act, gain, w_sb_qkv_t, qk_gains, w_sb_out, f"l{i}")
        act, f = _ffn_fwd(act, ffn_norm[i:i + 1], w_gate_t[i], w_up_t[i], w_down[i], f"l{i}")
        saved.append((s, f))
    dact, loss_cols = _loss_head(act, loss_target[0], name="loss_head")

    g_mix_norm, g_ffn_norm = [None] * depth, [None] * depth
    g_pool_in, g_pool_group, g_pool_scale = [None] * n_pool, [None] * n_pool, [None] * n_pool
    g_gate_t, g_up_t, g_down = [None] * depth, [None] * depth, [None] * depth
    for i in reversed(range(depth)):
        kind, j = i % 3, i // 3
        gain = mix_norm[i:i + 1]
        s, f = saved[i]
        dact, g_gate_t[i], g_up_t[i], g_down[i], g_ffn_norm[i] = _ffn_bwd(
            dact, f, ffn_norm[i:i + 1], w_gate_t[i], w_up_t[i], w_down[i], f"l{i}")
        if kind == 0:
            dact, g_pool_in[j], g_pool_group[j], g_pool_scale[j], g_mix_norm[i] = _pool_mixer_bwd(
                dact, s, gain, w_pool_in[j], w_pool_group[j], scale_full[j:j + 1], f"l{i}")
        elif kind == 1:
            (dact, g_zx_t, g_dt_t, g_conv_w, g_conv_b, g_par, g_out_norm, g_ssd_out,
             g_mix_norm[i]) = _ssd_mixer_bwd(dact, s, gain, w_zx_t, w_dt_t, convw_full, ssd_conv_b, par, d_rep,
                                             ssd_out_norm, w_ssd_out, f"l{i}")
        else:
            dact, g_qkv_t, g_qk_gains, g_sb_out, g_mix_norm[i] = _sb_mixer_bwd(
                dact, s, gain, w_sb_qkv_t, qk_gains, w_sb_out, f"l{i}")
    grad_x = dact[None]

    full_grads = dict(
        pool_in=jnp.stack(g_pool_in), pool_group=jnp.stack(g_pool_group),
        ssd_in=jnp.concatenate([g_zx_t, _ssd_group_unpad(g_dt_t.T, n_ssd_groups).T], axis=0)[None],
        ssd_out=g_ssd_out[None],
        sb_qkv=g_qkv_t[None], sb_out=g_sb_out[None],
        ffn_gate=jnp.stack(g_gate_t), ffn_up=jnp.stack(g_up_t), ffn_down=jnp.stack(g_down))
    blocks = [_device_blocks(n, full_grads[n], full_grads[n].shape[0], d) for n in BIG_WEIGHTS]
    parts = jnp.concatenate(blocks, axis=1)
    parts = jnp.pad(parts, ((0, 0), (0, packed_rows - big_rows), (0, 0)))
    parts = parts.reshape(N_DEV // 2, 2, packed_rows, d).transpose(1, 0, 2, 3)
    from_sibling = _exchange_sibling(parts, name="reduce_sibling")
    chip_sums, chip_sums_wire = _add_pairs(parts, from_sibling, cs.reshape(1).astype(jnp.int32),
                                           name="reduce_sibling_add")
    from_chips = _exchange_chips(chip_sums_wire, name="reduce_chips")

    def pack_f32(tree):
        return _pad_rows(pack(tree, F32), packed_rows)

    big_out = _adamw_sharded(chip_sums, from_chips, chip.reshape(1).astype(jnp.int32), pack_f32(weights),
                             pack_f32(mom1), pack_f32(mom2), name="adamw_sharded")

    def small_pack(mix, ffn, conv_b, out_norm, scale, conv_w, vec, qk, loss=None):
        buf = jnp.zeros((SMALL_ROWS, d), F32)
        buf = buf.at[ROW_MIX_NORM:ROW_MIX_NORM + depth].set(mix).at[ROW_FFN_NORM:ROW_FFN_NORM + depth].set(ffn)
        buf = buf.at[ROW_CONV_B:ROW_CONV_B + conv_b.size // d].set(conv_b.reshape(-1, d))
        buf = buf.at[ROW_OUT_NORM:ROW_OUT_NORM + out_norm.size // d].set(out_norm.reshape(-1, d))
        buf = buf.at[ROW_POOL_SCALE:ROW_POOL_SCALE + n_pool].set(scale)
        buf = buf.at[ROW_CONV_W:ROW_CONV_W + conv_w.size // d].set(conv_w.reshape(-1, d))
        buf = buf.at[ROW_SSD_VEC].set(vec.reshape(-1)).at[ROW_QK_GAIN].set(qk.reshape(-1))
        if loss is not None:
            buf = buf.at[ROW_LOSS].set(loss.reshape(-1))
        return buf

    def small_params(tree):
        scale = lax.dynamic_update_slice(jnp.zeros((n_pool, d), F32), tree["pool_scale"],
                                         (0, dev * tree["pool_scale"].shape[1]))
        conv_w = lax.dynamic_update_slice(jnp.zeros(convw_full.shape, F32), tree["ssd_conv_w"][0],
                                          (0, dev * tree["ssd_conv_w"].shape[2]))
        vec = jnp.zeros((SUBLANES, LANES), F32)
        vec = vec.at[0, :n_ssd_heads].set(tree["ssd_dt_bias"][0]).at[1, :n_ssd_heads].set(tree["ssd_a_log"][0])
        vec = vec.at[2, :n_ssd_heads].set(tree["ssd_d"][0])
        qk = jnp.zeros((SUBLANES, LANES), F32)
        qk = qk.at[0, SB_HEAD_DIM:].set(tree["sb_q_norm"][0]).at[1, SB_HEAD_DIM:].set(tree["sb_k_norm"][0])
        return small_pack(tree["mix_norm"], tree["ffn_norm"], tree["ssd_conv_b"], tree["ssd_out_norm"], scale,
                          conv_w, vec, qk)

    small_partial = small_pack(jnp.concatenate(g_mix_norm, axis=0), jnp.concatenate(g_ffn_norm, axis=0), g_conv_b,
                               g_out_norm, jnp.concatenate(g_pool_scale, axis=0), g_conv_w,
                               jnp.zeros((SUBLANES, LANES), F32).at[:3, :n_ssd_heads].set(
                                   _ssd_group_unpad(g_par[:3], n_ssd_groups)), g_qk_gains,
                               loss_cols)
    small_all = _all_gather(small_partial, name="gather_small", in_vmem=True)
    small_out = _adamw_small(small_all, small_params(weights), small_params(mom1), small_params(mom2),
                             name="adamw_small")
    loss = small_out[0][ROW_LOSS, 0]

    def unpack(big, small):
        out = {}
        for name in BIG_WEIGHTS:
            a, n = seg[name]
            out[name] = _from_rows(name, big[a:a + n], weights[name].shape)
        out["mix_norm"] = small[ROW_MIX_NORM:ROW_MIX_NORM + depth]
        out["ffn_norm"] = small[ROW_FFN_NORM:ROW_FFN_NORM + depth]
        out["ssd_conv_b"] = small[ROW_CONV_B:ROW_CONV_B + ssd_conv_b.size // d].reshape(ssd_conv_b.shape)
        out["ssd_out_norm"] = small[ROW_OUT_NORM:ROW_OUT_NORM + ssd_out_norm.size // d].reshape(ssd_out_norm.shape)
        out["pool_scale"] = lax.dynamic_slice(small[ROW_POOL_SCALE:ROW_POOL_SCALE + n_pool],
                                              (0, dev * pool_scale.shape[1]), pool_scale.shape)
        conv_w = small[ROW_CONV_W:ROW_CONV_W + convw_full.size // d].reshape(convw_full.shape)
        out["ssd_conv_w"] = lax.dynamic_slice(conv_w, (0, dev * ssd_conv_w.shape[2]), ssd_conv_w.shape[1:])[None]
        vec = small[ROW_SSD_VEC].reshape(SUBLANES, LANES)
        out["ssd_dt_bias"], out["ssd_a_log"], out["ssd_d"] = (vec[r:r + 1, :n_ssd_heads] for r in range(3))
        qk = small[ROW_QK_GAIN].reshape(SUBLANES, LANES)
        out["sb_q_norm"], out["sb_k_norm"] = qk[0:1, SB_HEAD_DIM:], qk[1:2, SB_HEAD_DIM:]
        return [out[n] for n in names]

    results = [unpack(b, s) for b, s in zip(big_out, small_out)]
    return (loss, grad_x, *results[0], *results[1], *results[2], *results[3])
```

```python
import math

import jax
import jax.numpy as jnp
from jax import lax
from jax.experimental import pallas as pl
from jax.experimental.pallas import tpu as pltpu

F32 = jnp.float32
BF16 = jnp.bfloat16

N_DEV = 8
NORM_EPS = 1e-6
V7X_VMEM_LIMIT_BYTES = 48 * 1024 * 1024
LANES = 128
SUBLANES = 8

POOL_WINDOWS = (2, 4, 8, 16)
SSD_CHUNK = 256
SSD_HEAD_DIM = 64
SSD_STATE = 128
SSD_HEADS_PER_GROUP = 4
SSD_CONV = 4
SB_HEAD_DIM = 64
SB_BLOCK = 128
SB_QUERY_BLOCK = 256

ADAM_LR = 0.001
ADAM_B1 = 0.9
ADAM_B2 = 0.999
ADAM_EPS = 1e-08
ADAM_WD = 0.01
ADAM_STEP = 10


def _params(*sem):
    return pltpu.CompilerParams(dimension_semantics=sem, vmem_limit_bytes=V7X_VMEM_LIMIT_BYTES)


def _tile(n, cap, mult):
    best = None
    for t in range(mult, min(n, cap) + 1, mult):
        if n % t == 0:
            best = t
    return best or n


def _load_slabs(ref, slabs):
    if not slabs:
        return ref[...]
    return jnp.concatenate([ref[p] for p in range(ref.shape[0])], axis=1)


def _matmul(a, b, mode, *, name, out_dtype=F32, resid=None, a_slabs=False, out_slabs=False,
            tm_cap=1024, tn_cap=1024, tk_cap=2048):
    pairs = list(zip(a, b)) if isinstance(a, (list, tuple)) else [(a, b)]
    a, b = pairs[0]
    if a_slabs:
        m, k = a.shape[1], a.shape[0] * LANES
    else:
        m, k = a.shape
    n = b.shape[1] if mode == "nn" else b.shape[0]
    assert (b.shape[0] if mode == "nn" else b.shape[1]) == k
    assert all(pa.shape == a.shape and pb.shape == b.shape for pa, pb in pairs)
    tm, tn, tk = _tile(m, tm_cap, SUBLANES), _tile(n, tn_cap, LANES), _tile(k, tk_cap, LANES)
    nk = k // tk
    dn = (((1,), (0,)), ((), ())) if mode == "nn" else (((1,), (1,)), ((), ()))
    has_resid = resid is not None
    n_pairs = len(pairs)

    def body(*refs):
        ab_refs, rest = refs[:2 * n_pairs], refs[2 * n_pairs:]
        r_ref = rest[0] if has_resid else None
        o_ref = rest[1] if has_resid else rest[0]
        kk = pl.program_id(2)

        def partial():
            total = None
            for p in range(n_pairs):
                d = lax.dot_general(_load_slabs(ab_refs[2 * p], a_slabs).astype(BF16),
                                    ab_refs[2 * p + 1][...].astype(BF16), dn, preferred_element_type=F32)
                total = d if total is None else total + d
            return total

        def finish(r):
            if has_resid:
                r = r + r_ref[...]
            if out_slabs:
                for p in range(tn // LANES):
                    o_ref[p] = r[:, p * LANES:(p + 1) * LANES].astype(out_dtype)
            else:
                o_ref[...] = r.astype(out_dtype)

        if nk == 1:
            finish(partial())
        else:
            acc = rest[-1]

            @pl.when(kk == 0)
            def _():
                acc[...] = jnp.zeros_like(acc)

            acc[...] += partial()

            @pl.when(kk == nk - 1)
            def _():
                finish(acc[...])

    b_spec = (pl.BlockSpec((tk, tn), lambda i, j, kk: (kk, j)) if mode == "nn"
              else pl.BlockSpec((tn, tk), lambda i, j, kk: (j, kk)))
    a_spec = (pl.BlockSpec((tk // LANES, tm, LANES), lambda i, j, kk: (kk, i, 0)) if a_slabs
              else pl.BlockSpec((tm, tk), lambda i, j, kk: (i, kk)))
    in_specs = [a_spec, b_spec] * n_pairs
    args = [t for pair in pairs for t in pair]
    if has_resid:
        in_specs.append(pl.BlockSpec((tm, tn), lambda i, j, kk: (i, j)))
        args.append(resid)
    if out_slabs:
        out_spec = pl.BlockSpec((tn // LANES, tm, LANES), lambda i, j, kk: (j, i, 0))
        out_shape = jax.ShapeDtypeStruct((n // LANES, m, LANES), out_dtype)
    else:
        out_spec = pl.BlockSpec((tm, tn), lambda i, j, kk: (i, j))
        out_shape = jax.ShapeDtypeStruct((m, n), out_dtype)
    return pl.pallas_call(
        body, name=name, grid=(m // tm, n // tn, nk),
        in_specs=in_specs, out_specs=out_spec, out_shape=out_shape,
        scratch_shapes=[pltpu.VMEM((tm, tn), F32)] if nk > 1 else [],
        compiler_params=_params("parallel", "parallel", "arbitrary"),
    )(*args)


def _matmul_tn(a, b, *, name, a_slabs=False, ta_cap=1024, tb_cap=1024, tr_cap=512):
    if a_slabs:
        r, ka = a.shape[1], a.shape[0] * LANES
    else:
        r, ka = a.shape
    nb = b.shape[1]
    assert b.shape[0] == r
    ta, tb, tr = _tile(ka, ta_cap, LANES), _tile(nb, tb_cap, LANES), _tile(r, tr_cap, SUBLANES)

    def body(a_ref, b_ref, o_ref):
        @pl.when(pl.program_id(2) == 0)
        def _():
            o_ref[...] = jnp.zeros_like(o_ref)

        o_ref[...] += lax.dot_general(_load_slabs(a_ref, a_slabs).astype(BF16), b_ref[...].astype(BF16),
                                      (((0,), (0,)), ((), ())), preferred_element_type=F32)

    a_spec = (pl.BlockSpec((ta // LANES, tr, LANES), lambda i, j, kk: (i, kk, 0)) if a_slabs
              else pl.BlockSpec((tr, ta), lambda i, j, kk: (kk, i)))
    return pl.pallas_call(
        body, name=name, grid=(ka // ta, nb // tb, r // tr),
        in_specs=[a_spec, pl.BlockSpec((tr, tb), lambda i, j, kk: (kk, j))],
        out_specs=pl.BlockSpec((ta, tb), lambda i, j, kk: (i, j)),
        out_shape=jax.ShapeDtypeStruct((ka, nb), F32),
        compiler_params=_params("parallel", "parallel", "arbitrary"),
    )(a, b)


def _rms_fwd(x, gain, *, name):
    t, d = x.shape
    tm = _tile(t, 512, SUBLANES)

    def body(x_ref, g_ref, o_ref):
        xv = x_ref[...]
        r = lax.rsqrt(jnp.mean(xv * xv, axis=-1, keepdims=True) + NORM_EPS)
        o_ref[...] = (xv * r * g_ref[...]).astype(BF16)

    return pl.pallas_call(
        body, name=name, grid=(t // tm,),
        in_specs=[pl.BlockSpec((tm, d), lambda i: (i, 0)), pl.BlockSpec((1, d), lambda i: (0, 0))],
        out_specs=pl.BlockSpec((tm, d), lambda i: (i, 0)),
        out_shape=jax.ShapeDtypeStruct((t, d), BF16),
        compiler_params=_params("parallel"),
    )(x, gain)


def _rms_bwd(x, gain, dh, dres, *, name):
    t, d = x.shape
    tm = _tile(t, 512, SUBLANES)

    def body(x_ref, g_ref, dh_ref, dres_ref, dx_ref, dg_ref):
        @pl.when(pl.program_id(0) == 0)
        def _():
            dg_ref[...] = jnp.zeros_like(dg_ref)

        xv = x_ref[...]
        r = lax.rsqrt(jnp.mean(xv * xv, axis=-1, keepdims=True) + NORM_EPS)
        xhat = xv * r
        dhv = dh_ref[...]
        u = dhv * g_ref[...]
        dx_ref[...] = dres_ref[...] + r * (u - xhat * jnp.mean(u * xhat, axis=-1, keepdims=True))
        dg_ref[...] += jnp.sum(dhv * xhat, axis=0, keepdims=True)

    return pl.pallas_call(
        body, name=name, grid=(t // tm,),
        in_specs=[pl.BlockSpec((tm, d), lambda i: (i, 0)), pl.BlockSpec((1, d), lambda i: (0, 0)),
                  pl.BlockSpec((tm, d), lambda i: (i, 0)), pl.BlockSpec((tm, d), lambda i: (i, 0))],
        out_specs=[pl.BlockSpec((tm, d), lambda i: (i, 0)), pl.BlockSpec((1, d), lambda i: (0, 0))],
        out_shape=[jax.ShapeDtypeStruct((t, d), F32), jax.ShapeDtypeStruct((1, d), F32)],
        compiler_params=_params("arbitrary"),
    )(x, gain, dh, dres)


def _loss_head(y, target, *, name):
    t, d = y.shape
    tm = _tile(t, 512, SUBLANES)

    def body(y_ref, t_ref, dy_ref, l_ref):
        @pl.when(pl.program_id(0) == 0)
        def _():
            l_ref[...] = jnp.zeros_like(l_ref)

        e = y_ref[...] - t_ref[...]
        dy_ref[...] = e * (1.0 / d)
        l_ref[...] += jnp.sum(e * e, axis=0, keepdims=True) * (0.5 / d)

    return pl.pallas_call(
        body, name=name, grid=(t // tm,),
        in_specs=[pl.BlockSpec((tm, d), lambda i: (i, 0)), pl.BlockSpec((tm, d), lambda i: (i, 0))],
        out_specs=[pl.BlockSpec((tm, d), lambda i: (i, 0)), pl.BlockSpec((1, d), lambda i: (0, 0))],
        out_shape=[jax.ShapeDtypeStruct((t, d), F32), jax.ShapeDtypeStruct((1, d), F32)],
        compiler_params=_params("arbitrary"),
    )(y, target)


def _sigmoid(v):
    return 0.5 * jnp.tanh(0.5 * v) + 0.5


FFN_TOKEN_TILE = 512
FFN_HIDDEN_TILE = 1408
NT_DIMS = (((1,), (1,)), ((), ()))


def _ffn_up(h, w_gate_t, w_up_t, *, name):
    t, d = h.shape
    f = w_gate_t.shape[0]
    tm, tn = _tile(t, FFN_TOKEN_TILE, SUBLANES), _tile(f, FFN_HIDDEN_TILE, LANES)

    def body(h_ref, g_ref, u_ref, s_ref, a_ref, b_ref):
        hv = h_ref[...].astype(BF16)
        av = lax.dot_general(hv, g_ref[...].astype(BF16), NT_DIMS, preferred_element_type=F32)
        bv = lax.dot_general(hv, u_ref[...].astype(BF16), NT_DIMS, preferred_element_type=F32)
        s_ref[...] = (av * _sigmoid(av) * bv).astype(BF16)
        a_ref[...] = av.astype(BF16)
        b_ref[...] = bv.astype(BF16)

    w_spec = pl.BlockSpec((tn, d), lambda j, i: (j, 0))
    out_spec = pl.BlockSpec((tm, tn), lambda j, i: (i, j))
    out = jax.ShapeDtypeStruct((t, f), BF16)
    return pl.pallas_call(
        body, name=name, grid=(f // tn, t // tm),
        in_specs=[pl.BlockSpec((tm, d), lambda j, i: (i, 0)), w_spec, w_spec],
        out_specs=[out_spec, out_spec, out_spec], out_shape=[out, out, out],
        compiler_params=_params("parallel", "parallel"),
    )(h, w_gate_t, w_up_t)


def _ffn_dact(dx, w_down, a, b, *, name):
    t, d = dx.shape
    f = w_down.shape[0]
    tm, tn = _tile(t, FFN_TOKEN_TILE, SUBLANES), _tile(f, FFN_HIDDEN_TILE, LANES)

    def body(dx_ref, w_ref, a_ref, b_ref, da_ref, db_ref):
        ds = lax.dot_general(dx_ref[...].astype(BF16), w_ref[...].astype(BF16), NT_DIMS, preferred_element_type=F32)
        av = a_ref[...].astype(F32)
        sg = _sigmoid(av)
        da_ref[...] = (ds * b_ref[...].astype(F32) * (sg * (1.0 + av * (1.0 - sg)))).astype(BF16)
        db_ref[...] = (ds * av * sg).astype(BF16)

    blk = pl.BlockSpec((tm, tn), lambda j, i: (i, j))
    out = jax.ShapeDtypeStruct((t, f), BF16)
    return pl.pallas_call(
        body, name=name, grid=(f // tn, t // tm),
        in_specs=[pl.BlockSpec((tm, d), lambda j, i: (i, 0)), pl.BlockSpec((tn, d), lambda j, i: (j, 0)), blk, blk],
        out_specs=[blk, blk], out_shape=[out, out],
        compiler_params=_params("parallel", "parallel"),
    )(dx, w_down, a, b)


def _ffn_fwd(x, gain, w_gate_t, w_up_t, w_down, tag):
    h = _rms_fwd(x, gain, name=f"ffn_norm_{tag}")
    s, a, b = _ffn_up(h, w_gate_t, w_up_t, name=f"ffn_up_{tag}")
    x_new = _matmul(s, w_down, "nn", resid=x, tn_cap=1024, tk_cap=2816, name=f"ffn_down_{tag}")
    return x_new, (x, h, a, b, s)


def _ffn_bwd(dx, saved, gain, w_gate_t, w_up_t, w_down, tag):
    x, h, a, b, s = saved
    da, db = _ffn_dact(dx, w_down, a, b, name=f"ffn_dact_{tag}")
    wide = dict(ta_cap=FFN_HIDDEN_TILE, tb_cap=1024, tr_cap=512)
    dw_down = _matmul_tn(s, dx, name=f"ffn_dwdown_{tag}", **wide)
    dw_gate_t = _matmul_tn(da, h, name=f"ffn_dwgate_{tag}", **wide)
    dw_up_t = _matmul_tn(db, h, name=f"ffn_dwup_{tag}", **wide)
    dh = _matmul([da, db], [w_gate_t, w_up_t], "nn", tm_cap=512, tn_cap=1024, tk_cap=2816, name=f"ffn_dh_{tag}")
    dx_in, dgain = _rms_bwd(x, gain, dh, dx, name=f"ffn_dnorm_{tag}")
    return dx_in, dw_gate_t, dw_up_t, dw_down, dgain


POOL_HALO = 16


def _shift_rows(v, k):
    n = v.shape[0]
    return pltpu.roll(v, k % n, 0)


def _window_sum(v, w, direction):
    k = 1
    while k < w:
        v = v + _shift_rows(v, direction * k)
        k *= 2
    return v


def _pool_fwd(u, x, w_group, scale, *, name):
    t, d = u.shape
    ng, dg = w_group.shape[0], w_group.shape[1]
    tm = _tile(t, 512, POOL_HALO)
    hb = tm // POOL_HALO

    def body(u_ref, halo_ref, x_ref, w_ref, s_ref, xo_ref, p_ref, y_ref):
        i, g = pl.program_id(0), pl.program_id(1)
        halo = jnp.where(i > 0, halo_ref[...], 0.0)
        ext = jnp.concatenate([halo, u_ref[...]], axis=0)
        pos = i * tm + lax.broadcasted_iota(jnp.int32, (tm, 1), 0)
        for gi, win in enumerate(POOL_WINDOWS):
            @pl.when(g == gi)
            def _(win=win):
                tot = _window_sum(ext, win, 1)[POOL_HALO:]
                cnt = jnp.minimum(pos + 1, win).astype(F32)
                p = (tot / cnt - u_ref[...]).astype(BF16)
                p_ref[...] = p
                y = jnp.dot(p, w_ref[...].astype(BF16), preferred_element_type=F32)
                y_ref[...] = y
                xo_ref[...] = x_ref[...] + y * s_ref[...]

    blk = pl.BlockSpec((tm, dg), lambda i, g: (i, g))
    return pl.pallas_call(
        body, name=name, grid=(t // tm, ng),
        in_specs=[blk, pl.BlockSpec((POOL_HALO, dg), lambda i, g: (jnp.maximum(i * hb - 1, 0), g)), blk,
                  pl.BlockSpec((None, dg, dg), lambda i, g: (g, 0, 0)), pl.BlockSpec((1, dg), lambda i, g: (0, g))],
        out_specs=[blk, blk, blk],
        out_shape=[jax.ShapeDtypeStruct((t, d), F32), jax.ShapeDtypeStruct((t, d), BF16),
                   jax.ShapeDtypeStruct((t, d), F32)],
        compiler_params=_params("parallel", "parallel"),
    )(u, u, x, w_group, scale)


def _pool_bwd(dx, p, y_pre, w_group, scale, *, name):
    t, d = dx.shape
    ng, dg = w_group.shape[0], w_group.shape[1]
    tm = _tile(t, 512, POOL_HALO)
    hb = tm // POOL_HALO
    nt = t // tm

    def body(dx_ref, nxt_ref, p_ref, y_ref, w_ref, s_ref, du_ref, dw_ref, ds_ref):
        g, i = pl.program_id(0), pl.program_id(1)

        @pl.when(i == 0)
        def _():
            dw_ref[...] = jnp.zeros_like(dw_ref)
            ds_ref[...] = jnp.zeros_like(ds_ref)

        dxv = dx_ref[...]
        ds_ref[...] += jnp.sum(dxv * y_ref[...], axis=0, keepdims=True)
        nxt = jnp.where(i < nt - 1, nxt_ref[...], 0.0)
        dyp = (jnp.concatenate([dxv, nxt], axis=0) * s_ref[...]).astype(BF16)
        dw_ref[...] += lax.dot_general(p_ref[...], dyp[:tm], (((0,), (0,)), ((), ())), preferred_element_type=F32)
        dp = lax.dot_general(dyp, w_ref[...].astype(BF16), (((1,), (1,)), ((), ())), preferred_element_type=F32)
        pos = i * tm + lax.broadcasted_iota(jnp.int32, (tm + POOL_HALO, 1), 0)
        for gi, win in enumerate(POOL_WINDOWS):
            @pl.when(g == gi)
            def _(win=win):
                q = dp / jnp.minimum(pos + 1, win).astype(F32)
                du_ref[...] = (_window_sum(q, win, -1)[:tm] - dp[:tm]).astype(BF16)

    blk = pl.BlockSpec((tm, dg), lambda g, i: (i, g))
    return pl.pallas_call(
        body, name=name, grid=(ng, nt),
        in_specs=[blk, pl.BlockSpec((POOL_HALO, dg), lambda g, i: (jnp.minimum((i + 1) * hb, t // POOL_HALO - 1), g)),
                  blk, blk, pl.BlockSpec((None, dg, dg), lambda g, i: (g, 0, 0)),
                  pl.BlockSpec((1, dg), lambda g, i: (0, g))],
        out_specs=[blk, pl.BlockSpec((None, dg, dg), lambda g, i: (g, 0, 0)), pl.BlockSpec((1, dg), lambda g, i: (0, g))],
        out_shape=[jax.ShapeDtypeStruct((t, d), BF16), jax.ShapeDtypeStruct((ng, dg, dg), F32),
                   jax.ShapeDtypeStruct((1, d), F32)],
        compiler_params=_params("parallel", "arbitrary"),
    )(dx, dx, p, y_pre, w_group, scale)


def _pool_mixer_fwd(x, gain, w_in, w_group, scale, tag):
    h = _rms_fwd(x, gain, name=f"pool_norm_{tag}")
    u = _matmul(h, w_in, "nn", name=f"pool_in_{tag}")
    x_new, p, y_pre = _pool_fwd(u, x, w_group, scale, name=f"pool_mix_{tag}")
    return x_new, (x, h, p, y_pre)


def _pool_mixer_bwd(dx, saved, gain, w_in, w_group, scale, tag):
    x, h, p, y_pre = saved
    du, dw_group, dscale = _pool_bwd(dx, p, y_pre, w_group, scale, name=f"pool_dmix_{tag}")
    dw_in = _matmul_tn(h, du, name=f"pool_dwin_{tag}")
    dh = _matmul(du, w_in, "nt", name=f"pool_dh_{tag}")
    dx_in, dgain = _rms_bwd(x, gain, dh, dx, name=f"pool_dnorm_{tag}")
    return dx_in, dw_in, dw_group, dscale, dgain


CONV_HALO = 8
HIGHEST = lax.Precision.HIGHEST
NEG_BIG = -1e30


def _softplus(v):
    return jnp.maximum(v, 0.0) + jnp.log(1.0 + jnp.exp(-jnp.abs(v)))


def _dot_exact(a, b):
    return jnp.dot(a, b, precision=HIGHEST, preferred_element_type=F32)


def _conv_taps(ext, w_ref, off, rows):
    acc = None
    for k in range(SSD_CONV):
        shift = SSD_CONV - 1 - k
        v = (_shift_rows(ext, shift) if shift else ext)[off:off + rows] * w_ref[k:k + 1, :]
        acc = v if acc is None else acc + v
    return acc


def _ssd_conv_fwd(zx, conv_w, conv_b, col0, *, name):
    t = zx.shape[0]
    c = conv_w.shape[1]
    tm, tc = _tile(t, 512, CONV_HALO), _tile(c, 512, LANES)
    hb, cb0 = tm // CONV_HALO, col0 // tc
    assert col0 % tc == 0

    def body(x_ref, halo_ref, w_ref, b_ref, o_ref):
        halo = jnp.where(pl.program_id(0) > 0, halo_ref[...], 0.0)
        ext = jnp.concatenate([halo, x_ref[...]], axis=0)
        pre = _conv_taps(ext, w_ref, CONV_HALO, tm) + b_ref[...]
        o_ref[...] = pre * _sigmoid(pre)

    return pl.pallas_call(
        body, name=name, grid=(t // tm, c // tc),
        in_specs=[pl.BlockSpec((tm, tc), lambda i, j: (i, j + cb0)),
                  pl.BlockSpec((CONV_HALO, tc), lambda i, j: (jnp.maximum(i * hb - 1, 0), j + cb0)),
                  pl.BlockSpec((SSD_CONV, tc), lambda i, j: (0, j)), pl.BlockSpec((1, tc), lambda i, j: (0, j))],
        out_specs=pl.BlockSpec((tm, tc), lambda i, j: (i, j)),
        out_shape=jax.ShapeDtypeStruct((t, c), F32),
        compiler_params=_params("parallel", "parallel"),
    )(zx, zx, conv_w, conv_b)


def _ssd_conv_bwd(dxa, zx, conv_w, conv_b, col0, *, name):
    t = zx.shape[0]
    c = conv_w.shape[1]
    tm, tc = _tile(t, 512, CONV_HALO), _tile(c, 512, LANES)
    hb, cb0, nt = tm // CONV_HALO, col0 // tc, t // tm
    last_halo = t // CONV_HALO - 1

    def body(x_ref, prev_ref, nxt_ref, d_ref, dnxt_ref, w_ref, b_ref, dx_ref, dw_ref, db_ref):
        i = pl.program_id(1)

        @pl.when(i == 0)
        def _():
            dw_ref[...] = jnp.zeros_like(dw_ref)
            db_ref[...] = jnp.zeros_like(db_ref)

        prev = jnp.where(i > 0, prev_ref[...], 0.0)
        has_next = i < nt - 1
        ext = jnp.concatenate([prev, x_ref[...], jnp.where(has_next, nxt_ref[...], 0.0)], axis=0)
        pre = _conv_taps(ext, w_ref, CONV_HALO, tm + CONV_HALO) + b_ref[...]
        sg = _sigmoid(pre)
        dact = jnp.concatenate([d_ref[...], jnp.where(has_next, dnxt_ref[...], 0.0)], axis=0)
        dpre = dact * (sg * (1.0 + pre * (1.0 - sg)))
        db_ref[...] += jnp.sum(dpre[:tm], axis=0, keepdims=True)
        acc = None
        for k in range(SSD_CONV):
            shift = SSD_CONV - 1 - k
            src = (_shift_rows(ext, shift) if shift else ext)[CONV_HALO:CONV_HALO + tm]
            dw_ref[k:k + 1, :] += jnp.sum(dpre[:tm] * src, axis=0, keepdims=True)
            v = (_shift_rows(dpre, -shift) if shift else dpre)[:tm] * w_ref[k:k + 1, :]
            acc = v if acc is None else acc + v
        dx_ref[...] = acc.astype(BF16)

    main = lambda j, i: (i, j + cb0)
    return pl.pallas_call(
        body, name=name, grid=(c // tc, nt),
        in_specs=[pl.BlockSpec((tm, tc), main),
                  pl.BlockSpec((CONV_HALO, tc), lambda j, i: (jnp.maximum(i * hb - 1, 0), j + cb0)),
                  pl.BlockSpec((CONV_HALO, tc), lambda j, i: (jnp.minimum((i + 1) * hb, last_halo), j + cb0)),
                  pl.BlockSpec((tm, tc), lambda j, i: (i, j)),
                  pl.BlockSpec((CONV_HALO, tc), lambda j, i: (jnp.minimum((i + 1) * hb, last_halo), j)),
                  pl.BlockSpec((SSD_CONV, tc), lambda j, i: (0, j)), pl.BlockSpec((1, tc), lambda j, i: (0, j))],
        out_specs=[pl.BlockSpec((tm, tc), lambda j, i: (i, j)), pl.BlockSpec((SSD_CONV, tc), lambda j, i: (0, j)),
                   pl.BlockSpec((1, tc), lambda j, i: (0, j))],
        out_shape=[jax.ShapeDtypeStruct((t, c), BF16), jax.ShapeDtypeStruct((SSD_CONV, c), F32),
                   jax.ShapeDtypeStruct((1, c), F32)],
        compiler_params=_params("parallel", "arbitrary"),
    )(zx, zx, zx, dxa, dxa, conv_w, conv_b)


SSD_CUMSUM_PIECES = 3


def _ssd_group_pad(v, n_groups):
    lead = v.shape[:-1]
    v = v.reshape(*lead, n_groups, SSD_HEADS_PER_GROUP)
    v = jnp.pad(v, [(0, 0)] * (len(lead) + 1) + [(0, LANES - SSD_HEADS_PER_GROUP)])
    return v.reshape(*lead, n_groups * LANES)


def _ssd_group_unpad(v, n_groups):
    lead = v.shape[:-1]
    return v.reshape(*lead, n_groups, LANES)[..., :SSD_HEADS_PER_GROUP].reshape(*lead, -1)


def _ssd_chunk_common(dtp_ref, par_ref):
    ell = SSD_CHUNK
    dt = _softplus(dtp_ref[...] + par_ref[0:1, :])
    a = -jnp.exp(par_ref[1:2, :])
    row = lax.broadcasted_iota(jnp.int32, (ell, ell), 0)
    col = lax.broadcasted_iota(jnp.int32, (ell, ell), 1)
    acum = _split_dot(dt * a, (row >= col).astype(BF16), SSD_CUMSUM_PIECES, left=True)
    return dt, a, acum, acum.T, row, col


def _ssd_scan_fwd(xa, dtp, par, n_groups, *, name):
    t = xa.shape[0]
    ell, hd, hpg, ns = SSD_CHUNK, SSD_HEAD_DIM, SSD_HEADS_PER_GROUP, SSD_STATE
    gw = hpg * hd
    nc = t // ell
    b_blk0, c_blk0 = n_groups * gw // ns, n_groups * gw // ns + n_groups

    def body(xs_ref, b_ref, c_ref, dtp_ref, par_ref, y_ref, sin_ref, st):
        @pl.when(pl.program_id(1) == 0)
        def _():
            st[...] = jnp.zeros_like(st)

        dt, _, acum, acum_t, row, col = _ssd_chunk_common(dtp_ref, par_ref)
        bb, cc = b_ref[...].astype(BF16), c_ref[...].astype(BF16)
        cb = lax.dot_general(cc, bb, NT_DIMS, preferred_element_type=F32)
        for hh in range(hpg):
            lanes = slice(hh * hd, (hh + 1) * hd)
            col_a, row_a = acum[:, hh:hh + 1], acum_t[hh:hh + 1, :]
            decay = jnp.exp(jnp.where(row >= col, col_a - row_a, NEG_BIG))
            xdt = xs_ref[:, lanes] * dt[:, hh:hh + 1]
            s_h = st[hh]
            sin_ref[lanes, :] = s_h
            y = jnp.dot((cb * decay).astype(BF16), xdt.astype(BF16), preferred_element_type=F32)
            y += jnp.exp(col_a) * lax.dot_general(cc, s_h.astype(BF16), NT_DIMS, preferred_element_type=F32)
            y_ref[:, lanes] = y
            a_last = acum[ell - 1:ell, hh:hh + 1]
            w = jnp.exp(a_last - col_a)
            st[hh] = jnp.exp(a_last) * s_h + lax.dot_general(
                (xdt * w).astype(BF16), bb, (((0,), (0,)), ((), ())), preferred_element_type=F32)

    return pl.pallas_call(
        body, name=name, grid=(n_groups, nc),
        in_specs=[pl.BlockSpec((ell, gw), lambda g, c: (c, g)),
                  pl.BlockSpec((ell, ns), lambda g, c: (c, b_blk0 + g)),
                  pl.BlockSpec((ell, ns), lambda g, c: (c, c_blk0 + g)),
                  pl.BlockSpec((ell, LANES), lambda g, c: (c, g)),
                  pl.BlockSpec((SUBLANES, LANES), lambda g, c: (0, g))],
        out_specs=[pl.BlockSpec((ell, gw), lambda g, c: (c, g)),
                   pl.BlockSpec((None, None, gw, ns), lambda g, c: (c, g, 0, 0))],
        out_shape=[jax.ShapeDtypeStruct((t, n_groups * gw), F32),
                   jax.ShapeDtypeStruct((nc, n_groups, gw, ns), F32)],
        scratch_shapes=[pltpu.VMEM((hpg, hd, ns), F32)],
        compiler_params=_params("parallel", "arbitrary"),
    )(xa, xa, xa, dtp, par)


def _ssd_scan_bwd(dy, xa, dtp, par, s_in, n_groups, *, name):
    t = xa.shape[0]
    ell, hd, hpg, ns = SSD_CHUNK, SSD_HEAD_DIM, SSD_HEADS_PER_GROUP, SSD_STATE
    gw = hpg * hd
    nc = t // ell
    b_blk0, c_blk0 = n_groups * gw // ns, n_groups * gw // ns + n_groups
    nt_dims = (((1,), (1,)), ((), ()))
    tn_dims = (((0,), (0,)), ((), ()))

    def body(dy_ref, xs_ref, b_ref, c_ref, dtp_ref, par_ref, sin_ref,
             dxs_ref, db_ref, dc_ref, ddtp_ref, dpar_ref, dst):
        @pl.when(pl.program_id(1) == 0)
        def _():
            dst[...] = jnp.zeros_like(dst)
            dpar_ref[...] = jnp.zeros_like(dpar_ref)

        dtg, a_g, acum, acum_t, row, col = _ssd_chunk_common(dtp_ref, par_ref)
        bb, cc = b_ref[...].astype(BF16), c_ref[...].astype(BF16)
        cb = lax.dot_general(cc, bb, nt_dims, preferred_element_type=F32)
        cb_t = lax.dot_general(bb, cc, nt_dims, preferred_element_type=F32)
        lane = lax.broadcasted_iota(jnp.int32, (1, LANES), 1)
        dcb = jnp.zeros((ell, ell), F32)
        dcb_t = jnp.zeros((ell, ell), F32)
        dc = jnp.zeros((ell, ns), F32)
        db = jnp.zeros((ell, ns), F32)
        dacum = jnp.zeros((ell, LANES), F32)
        xsum = jnp.zeros((ell, LANES), F32)
        dsum = jnp.zeros((1, LANES), F32)
        for hh in range(hpg):
            lanes = slice(hh * hd, (hh + 1) * hd)
            onehot = (lane == hh).astype(F32)
            col_a, row_a = acum[:, hh:hh + 1], acum_t[hh:hh + 1, :]
            decay = jnp.exp(jnp.where(row >= col, col_a - row_a, NEG_BIG))
            decay_t = jnp.exp(jnp.where(col >= row, row_a - col_a, NEG_BIG))
            e_col = jnp.exp(col_a)
            a_last = acum[ell - 1:ell, hh:hh + 1]
            w = jnp.exp(a_last - col_a)
            e_last = jnp.exp(a_last)
            xs_h, dy_h = xs_ref[:, lanes], dy_ref[:, lanes]
            dt_h = dtg[:, hh:hh + 1]
            xdt = xs_h * dt_h
            xdt_b, dy_b = xdt.astype(BF16), dy_h.astype(BF16)
            s_h, ds_h = sin_ref[lanes, :], dst[hh]
            dm_decay = lax.dot_general(dy_b, xdt_b, nt_dims, preferred_element_type=F32) * decay
            dm_decay_t = lax.dot_general(xdt_b, dy_b, nt_dims, preferred_element_type=F32) * decay_t
            dcb += dm_decay
            dcb_t += dm_decay_t
            m_t = cb_t * decay_t
            dac = jnp.sum(dm_decay * cb, axis=1, keepdims=True) - jnp.sum(dm_decay_t * cb_t, axis=1, keepdims=True)
            b_ds = lax.dot_general(bb, ds_h.astype(BF16), nt_dims, preferred_element_type=F32)
            dxdt = jnp.dot(m_t.astype(BF16), dy_b, preferred_element_type=F32) + w * b_ds
            c_s = lax.dot_general(cc, s_h.astype(BF16), nt_dims, preferred_element_type=F32)
            dc += e_col * jnp.dot(dy_b, s_h.astype(BF16), preferred_element_type=F32)
            db += w * jnp.dot(xdt_b, ds_h.astype(BF16), preferred_element_type=F32)
            dac += jnp.sum(dy_h * c_s, axis=1, keepdims=True) * e_col
            q = jnp.sum(xdt * b_ds, axis=1, keepdims=True) * w
            dac -= q
            d_last = jnp.sum(q, axis=0, keepdims=True) + e_last * jnp.sum(
                jnp.sum(s_h * ds_h, axis=1, keepdims=True), axis=0, keepdims=True)
            is_last = lax.broadcasted_iota(jnp.int32, (ell, 1), 0) == ell - 1
            dac += jnp.where(is_last, d_last, 0.0)
            dacum += dac * onehot
            dst[hh] = e_last * ds_h + lax.dot_general((dy_h * e_col).astype(BF16), cc, tn_dims,
                                                      preferred_element_type=F32)
            dxs_ref[:, lanes] = dxdt * dt_h + dy_h * par_ref[2:3, hh:hh + 1]
            xsum += jnp.sum(dxdt * xs_h, axis=1, keepdims=True) * onehot
            dsum += jnp.sum(jnp.sum(dy_h * xs_h, axis=1, keepdims=True), axis=0, keepdims=True) * onehot
        dc_ref[...] = dc + jnp.dot(dcb.astype(BF16), bb, preferred_element_type=F32)
        db_ref[...] = db + jnp.dot(dcb_t.astype(BF16), cc, preferred_element_type=F32)
        dda = _split_dot(dacum, (col >= row).astype(BF16), SSD_CUMSUM_PIECES, left=True)
        ddtp = (xsum + dda * a_g) * _sigmoid(dtp_ref[...] + par_ref[0:1, :])
        ddtp_ref[...] = ddtp
        dpar_ref[0:1, :] += jnp.sum(ddtp, axis=0, keepdims=True)
        dpar_ref[1:2, :] += jnp.sum(dda * dtg, axis=0, keepdims=True) * a_g
        dpar_ref[2:3, :] += dsum

    rev = lambda i: nc - 1 - i
    return pl.pallas_call(
        body, name=name, grid=(n_groups, nc),
        in_specs=[pl.BlockSpec((ell, gw), lambda g, i: (rev(i), g)),
                  pl.BlockSpec((ell, gw), lambda g, i: (rev(i), g)),
                  pl.BlockSpec((ell, ns), lambda g, i: (rev(i), b_blk0 + g)),
                  pl.BlockSpec((ell, ns), lambda g, i: (rev(i), c_blk0 + g)),
                  pl.BlockSpec((ell, LANES), lambda g, i: (rev(i), g)),
                  pl.BlockSpec((SUBLANES, LANES), lambda g, i: (0, g)),
                  pl.BlockSpec((None, None, gw, ns), lambda g, i: (rev(i), g, 0, 0))],
        out_specs=[pl.BlockSpec((ell, gw), lambda g, i: (rev(i), g)),
                   pl.BlockSpec((ell, ns), lambda g, i: (rev(i), g)),
                   pl.BlockSpec((ell, ns), lambda g, i: (rev(i), g)),
                   pl.BlockSpec((ell, LANES), lambda g, i: (rev(i), g)),
                   pl.BlockSpec((SUBLANES, LANES), lambda g, i: (0, g))],
        out_shape=[jax.ShapeDtypeStruct((t, n_groups * gw), F32), jax.ShapeDtypeStruct((t, n_groups * ns), F32),
                   jax.ShapeDtypeStruct((t, n_groups * ns), F32), jax.ShapeDtypeStruct((t, n_groups * LANES), F32),
                   jax.ShapeDtypeStruct((SUBLANES, n_groups * LANES), F32)],
        scratch_shapes=[pltpu.VMEM((hpg, hd, ns), F32)],
        compiler_params=_params("parallel", "arbitrary"),
    )(dy, xa, xa, xa, dtp, par, s_in)


def _ssd_gate_fwd(y, xa, zx, d_rep, out_norm, *, name):
    t, di = y.shape
    gw = SSD_HEADS_PER_GROUP * SSD_HEAD_DIM
    tm = _tile(t, 512, SUBLANES)

    def body(y_ref, xs_ref, z_ref, d_ref, n_ref, o_ref):
        zv = z_ref[...]
        gt = (y_ref[...] + d_ref[...] * xs_ref[...]) * (zv * _sigmoid(zv))
        r = lax.rsqrt(jnp.mean(gt * gt, axis=-1, keepdims=True) + NORM_EPS)
        o_ref[...] = (gt * r * n_ref[...]).astype(BF16)

    blk = pl.BlockSpec((tm, gw), lambda i, g: (i, g))
    vec = pl.BlockSpec((1, gw), lambda i, g: (0, g))
    return pl.pallas_call(
        body, name=name, grid=(t // tm, di // gw),
        in_specs=[blk, blk, blk, vec, vec], out_specs=blk,
        out_shape=jax.ShapeDtypeStruct((t, di), BF16),
        compiler_params=_params("parallel", "parallel"),
    )(y, xa, zx, d_rep, out_norm)


def _ssd_gate_bwd(dgn, y, xa, zx, d_rep, out_norm, *, name):
    t, di = y.shape
    gw = SSD_HEADS_PER_GROUP * SSD_HEAD_DIM
    tm = _tile(t, 512, SUBLANES)

    def body(dg_ref, y_ref, xs_ref, z_ref, d_ref, n_ref, dy_ref, dz_ref, dn_ref):
        @pl.when(pl.program_id(1) == 0)
        def _():
            dn_ref[...] = jnp.zeros_like(dn_ref)

        zv = z_ref[...]
        sg = _sigmoid(zv)
        sz = zv * sg
        y2 = y_ref[...] + d_ref[...] * xs_ref[...]
        gt = y2 * sz
        r = lax.rsqrt(jnp.mean(gt * gt, axis=-1, keepdims=True) + NORM_EPS)
        ghat = gt * r
        dgv = dg_ref[...]
        dn_ref[...] += jnp.sum(dgv * ghat, axis=0, keepdims=True)
        u = dgv * n_ref[...]
        dgt = r * (u - ghat * jnp.mean(u * ghat, axis=-1, keepdims=True))
        dy_ref[...] = dgt * sz
        dz_ref[...] = (dgt * y2 * (sg * (1.0 + zv * (1.0 - sg)))).astype(BF16)

    blk = pl.BlockSpec((tm, gw), lambda g, i: (i, g))
    vec = pl.BlockSpec((1, gw), lambda g, i: (0, g))
    return pl.pallas_call(
        body, name=name, grid=(di // gw, t // tm),
        in_specs=[blk, blk, blk, blk, vec, vec], out_specs=[blk, blk, vec],
        out_shape=[jax.ShapeDtypeStruct((t, di), F32), jax.ShapeDtypeStruct((t, di), BF16),
                   jax.ShapeDtypeStruct((1, di), F32)],
        compiler_params=_params("parallel", "arbitrary"),
    )(dgn, y, xa, zx, d_rep, out_norm)


def _ssd_mixer_fwd(x, gain, w_zx_t, w_dt_t, conv_w, conv_b, par, d_rep, out_norm, w_out, tag):
    di = w_out.shape[0]
    n_groups = di // (SSD_HEADS_PER_GROUP * SSD_HEAD_DIM)
    h = _rms_fwd(x, gain, name=f"ssd_norm_{tag}")
    zx = _matmul(h, w_zx_t, "nt", name=f"ssd_in_{tag}")
    dtp = _matmul(h, w_dt_t, "nt", name=f"ssd_dt_{tag}")
    xa = _ssd_conv_fwd(zx, conv_w, conv_b, di, name=f"ssd_conv_{tag}")
    y, s_in = _ssd_scan_fwd(xa, dtp, par, n_groups, name=f"ssd_scan_{tag}")
    gn = _ssd_gate_fwd(y, xa, zx, d_rep, out_norm, name=f"ssd_gate_{tag}")
    x_new = _matmul(gn, w_out, "nn", resid=x, name=f"ssd_out_{tag}")
    return x_new, (x, h, zx, dtp, xa, y, s_in, gn)


def _ssd_mixer_bwd(dx, saved, gain, w_zx_t, w_dt_t, conv_w, conv_b, par, d_rep, out_norm, w_out, tag):
    x, h, zx, dtp, xa, y, s_in, gn = saved
    di = w_out.shape[0]
    n_groups = di // (SSD_HEADS_PER_GROUP * SSD_HEAD_DIM)
    dgn = _matmul(dx, w_out, "nt", name=f"ssd_dgn_{tag}")
    dw_out = _matmul_tn(gn, dx, name=f"ssd_dwout_{tag}")
    dy2, dz, dnorm = _ssd_gate_bwd(dgn, y, xa, zx, d_rep, out_norm, name=f"ssd_dgate_{tag}")
    dxs, db, dc, ddtp, dpar = _ssd_scan_bwd(dy2, xa, dtp, par, s_in, n_groups, name=f"ssd_dscan_{tag}")
    dxa = jnp.concatenate([dxs, db, dc], axis=1)
    dxbc, dconv_w, dconv_b = _ssd_conv_bwd(dxa, zx, conv_w, conv_b, di, name=f"ssd_dconv_{tag}")
    dzx = jnp.concatenate([dz, dxbc], axis=1)
    dw_zx_t = _matmul_tn(dzx, h, name=f"ssd_dwin_{tag}")
    dw_dt_t = _matmul_tn(ddtp, h, name=f"ssd_dwdt_{tag}")
    dh = _matmul(dzx, w_zx_t, "nn", name=f"ssd_dh_{tag}")
    dh = _matmul(ddtp, w_dt_t, "nn", resid=dh, name=f"ssd_dhdt_{tag}")
    dx_in, dgain = _rms_bwd(x, gain, dh, dx, name=f"ssd_dnorm_{tag}")
    return dx_in, dw_zx_t, dw_dt_t, dconv_w, dconv_b, dpar, dnorm, dw_out, dgain


def _sb_qk_norm_fwd(qkv, gains, *, name):
    ns, t, _ = qkv.shape
    per = ns // 3
    tm = _tile(t, 1024, SUBLANES)
    inv_sqrt_d = 1.0 / math.sqrt(SB_HEAD_DIM)

    def body(x_ref, g_ref, o_ref):
        kind = pl.program_id(0) // per
        xv = x_ref[...]

        @pl.when(kind == 2)
        def _():
            o_ref[...] = xv.astype(BF16)

        @pl.when(kind < 2)
        def _():
            left = lax.broadcasted_iota(jnp.int32, (1, LANES), 1) < SB_HEAD_DIM
            sq = xv * xv
            ms = jnp.where(left, jnp.sum(jnp.where(left, sq, 0.0), axis=1, keepdims=True),
                           jnp.sum(jnp.where(left, 0.0, sq), axis=1, keepdims=True)) * (1.0 / SB_HEAD_DIM)
            y = xv * lax.rsqrt(ms + NORM_EPS) * g_ref[pl.ds(kind, 1), :]
            o_ref[...] = (y * jnp.where(kind == 0, inv_sqrt_d, 1.0)).astype(BF16)

    blk = pl.BlockSpec((None, tm, LANES), lambda s, i: (s, i, 0))
    return pl.pallas_call(
        body, name=name, grid=(ns, t // tm),
        in_specs=[blk, pl.BlockSpec((SUBLANES, LANES), lambda s, i: (0, 0))], out_specs=blk,
        out_shape=jax.ShapeDtypeStruct((ns, t, LANES), BF16),
        compiler_params=_params("parallel", "parallel"),
    )(qkv, gains)


def _sb_qk_norm_bwd(dq, dk, dv, qkv, gains, *, name):
    ns, t, _ = qkv.shape
    per = ns // 3
    tm = _tile(t, 1024, SUBLANES)
    inv_sqrt_d = 1.0 / math.sqrt(SB_HEAD_DIM)

    def body(dq_ref, dk_ref, dv_ref, x_ref, g_ref, o_ref, dg_ref):
        s = pl.program_id(0)
        kind = s // per

        @pl.when((s == 0) & (pl.program_id(1) == 0))
        def _():
            dg_ref[...] = jnp.zeros_like(dg_ref)

        @pl.when(kind == 2)
        def _():
            o_ref[...] = dv_ref[...].astype(BF16)

        @pl.when(kind < 2)
        def _():
            xv = x_ref[...]
            dy = jnp.where(kind == 0, dq_ref[...] * inv_sqrt_d, dk_ref[...])
            left = lax.broadcasted_iota(jnp.int32, (1, LANES), 1) < SB_HEAD_DIM

            def halves(v):
                return jnp.where(left, jnp.sum(jnp.where(left, v, 0.0), axis=1, keepdims=True),
                                 jnp.sum(jnp.where(left, 0.0, v), axis=1, keepdims=True))

            r = lax.rsqrt(halves(xv * xv) * (1.0 / SB_HEAD_DIM) + NORM_EPS)
            xhat = xv * r
            u = dy * g_ref[pl.ds(kind, 1), :]
            o_ref[...] = (r * (u - xhat * halves(u * xhat) * (1.0 / SB_HEAD_DIM))).astype(BF16)
            dg_ref[pl.ds(kind, 1), :] += jnp.sum(dy * xhat, axis=0, keepdims=True)

    def grad_blk(kind):
        def index(s, i):
            mine = (s >= kind * per) & (s < (kind + 1) * per)
            return jnp.where(mine, s - kind * per, 0), jnp.where(mine, i, 0), 0
        return pl.BlockSpec((None, tm, LANES), index)

    blk = pl.BlockSpec((None, tm, LANES), lambda s, i: (s, i, 0))
    vec = pl.BlockSpec((SUBLANES, LANES), lambda s, i: (0, 0))
    return pl.pallas_call(
        body, name=name, grid=(ns, t // tm),
        in_specs=[grad_blk(0), grad_blk(1), grad_blk(2), blk, vec], out_specs=[blk, vec],
        out_shape=[jax.ShapeDtypeStruct((ns, t, LANES), BF16), jax.ShapeDtypeStruct((SUBLANES, LANES), F32)],
        compiler_params=_params("arbitrary", "arbitrary"),
    )(dq, dk, dv, qkv, gains)


def _split_dot(v, ones_mat, pieces, left=False):
    total, rest = None, v
    for p in range(pieces):
        part = rest.astype(BF16)
        if p + 1 < pieces:
            rest = rest - part.astype(F32)
        d = (jnp.dot(ones_mat, part, preferred_element_type=F32) if left
             else jnp.dot(part, ones_mat, preferred_element_type=F32))
        total = d if total is None else total + d
    return total


LOGIT_SUM_PIECES = 3
GRAD_SUM_PIECES = 2
LOG_WEIGHT_UNDERFLOW = -105.0


def _sb_attn_fwd(qkv_n, n_heads, *, name):
    ns, t, _ = qkv_n.shape
    per = ns // 3
    bq, blk, hd = SB_QUERY_BLOCK, SB_BLOCK, SB_HEAD_DIM
    nq, n_diag = t // bq, bq // blk

    def body(q_ref, k_ref, v_ref, o_ref):
        i = pl.program_id(1)
        row = lax.broadcasted_iota(jnp.int32, (blk, blk), 0)
        col = lax.broadcasted_iota(jnp.int32, (blk, blk), 1)
        later_keys = (row > col).astype(BF16)
        qry = lax.broadcasted_iota(jnp.int32, (bq, blk), 0)
        key = lax.broadcasted_iota(jnp.int32, (bq, blk), 1)

        def tile(kb, carry, key_offset):
            out = []
            start = pl.multiple_of(kb * blk, blk)
            for hf in range(2):
                lanes = slice(hf * hd, (hf + 1) * hd)
                run, acc = carry[hf]
                z = lax.dot_general(q_ref[:, lanes], k_ref[pl.ds(start, blk), lanes], NT_DIMS,
                                    preferred_element_type=F32)
                sp = _softplus(z)
                lm = -sp if key_offset is None else jnp.where(key + key_offset < qry, -sp, 0.0)
                after = _split_dot(lm, later_keys, LOGIT_SUM_PIECES) + run
                a = jnp.exp(z - sp + after)
                if key_offset is not None:
                    a = jnp.where(key + key_offset < qry, a, 0.0)
                acc = acc + jnp.dot(a.astype(BF16), v_ref[pl.ds(start, blk), lanes], preferred_element_type=F32)
                out.append((run + jnp.sum(lm, axis=1, keepdims=True), acc))
            return tuple(out)

        def live(carry):
            return jnp.max(jnp.maximum(carry[0][0], carry[1][0])) > LOG_WEIGHT_UNDERFLOW

        def step(state):
            s, _, carry = state
            carry = tile(n_diag * i - 1 - s, carry, None)
            return s + 1, live(carry), carry

        carry = tuple((jnp.zeros((bq, 1), F32), jnp.zeros((bq, hd), F32)) for _ in range(2))
        for j in reversed(range(n_diag)):
            carry = tile(n_diag * i + j, carry, j * blk)
        _, _, carry = lax.while_loop(lambda st: (st[0] < n_diag * i) & st[1], step,
                                     (jnp.int32(0), live(carry), carry))
        o_ref[...] = jnp.concatenate([carry[0][1], carry[1][1]], axis=1)

    return pl.pallas_call(
        body, name=name, grid=(per, nq),
        in_specs=[pl.BlockSpec((None, bq, LANES), lambda p, i: (p, i, 0)),
                  pl.BlockSpec((None, t, LANES), lambda p, i: (per + p, 0, 0)),
                  pl.BlockSpec((None, t, LANES), lambda p, i: (2 * per + p, 0, 0))],
        out_specs=pl.BlockSpec((bq, LANES), lambda p, i: (i, p)),
        out_shape=jax.ShapeDtypeStruct((t, n_heads * hd), F32),
        compiler_params=_params("parallel", "arbitrary"),
    )(qkv_n, qkv_n, qkv_n)


def _sb_attn_bwd(do, qkv_n, *, name):
    ns, t, _ = qkv_n.shape
    per = ns // 3
    bq, blk, hd = SB_QUERY_BLOCK, SB_BLOCK, SB_HEAD_DIM
    nq, n_diag = t // bq, bq // blk
    nt_dims = (((1,), (1,)), ((), ()))
    tn_dims = (((0,), (0,)), ((), ()))

    def body(q_ref, k_ref, v_ref, do_ref, dq_ref, dk_ref, dv_ref):
        i = pl.program_id(1)

        @pl.when(i == 0)
        def _():
            dk_ref[...] = jnp.zeros_like(dk_ref)
            dv_ref[...] = jnp.zeros_like(dv_ref)

        row = lax.broadcasted_iota(jnp.int32, (blk, blk), 0)
        col = lax.broadcasted_iota(jnp.int32, (blk, blk), 1)
        later_keys = (col > row).astype(BF16)
        earlier_keys = (col < row).astype(BF16)
        key = lax.broadcasted_iota(jnp.int32, (blk, bq), 0)
        qry = lax.broadcasted_iota(jnp.int32, (blk, bq), 1)
        halves = [slice(hf * hd, (hf + 1) * hd) for hf in range(2)]
        q_hs = [q_ref[:, lanes] for lanes in halves]
        do_bs = [do_ref[:, lanes].astype(BF16) for lanes in halves]

        def scores(kb, hf, key_offset):
            k_blk = k_ref[pl.ds(pl.multiple_of(kb * blk, blk), blk), halves[hf]]
            z = lax.dot_general(k_blk, q_hs[hf], nt_dims, preferred_element_type=F32)
            sp = _softplus(z)
            return k_blk, z, sp, (-sp if key_offset is None else jnp.where(key + key_offset < qry, -sp, 0.0))

        def add_column_sums(tots, kb, key_offset):
            return [tots[hf] + jnp.sum(scores(kb, hf, key_offset)[3], axis=0, keepdims=True) for hf in range(2)]

        def live(tots):
            return jnp.max(jnp.maximum(tots[0], tots[1])) > LOG_WEIGHT_UNDERFLOW

        def reach(state):
            s, _, tots = state
            tots = add_column_sums(tots, n_diag * i - 1 - s, None)
            return s + 1, live(tots), tots

        tots = [jnp.zeros((1, bq), F32)] * 2
        for j in reversed(range(n_diag)):
            tots = add_column_sums(tots, n_diag * i + j, j * blk)
        reached, _, tots = lax.while_loop(lambda st: (st[0] < n_diag * i) & st[1], reach,
                                          (jnp.int32(0), live(tots), tots))

        def tile(kb, carry, key_offset):
            out = []
            start = pl.multiple_of(kb * blk, blk)
            for hf, lanes in enumerate(halves):
                seen, gsum, dq = carry[hf]
                q_h, do_b = q_hs[hf], do_bs[hf]
                k_blk, z, sp, lm = scores(kb, hf, key_offset)
                blk_tot = jnp.sum(lm, axis=0, keepdims=True)
                after = _split_dot(lm, later_keys, LOGIT_SUM_PIECES, left=True) + (tots[hf] - seen - blk_tot)
                a = jnp.exp(z - sp + after)
                if key_offset is not None:
                    a = jnp.where(key + key_offset < qry, a, 0.0)
                da = lax.dot_general(v_ref[pl.ds(start, blk), lanes], do_b, nt_dims, preferred_element_type=F32)
                g = da * a
                before = _split_dot(g, earlier_keys, GRAD_SUM_PIECES, left=True) + gsum
                omb = jnp.exp(-sp)
                dz = g * omb - (1.0 - omb) * before
                if key_offset is not None:
                    dz = jnp.where(key + key_offset < qry, dz, 0.0)
                dz_b = dz.astype(BF16)
                dk_ref[pl.ds(start, blk), lanes] += jnp.dot(dz_b, q_h, preferred_element_type=F32)
                dv_ref[pl.ds(start, blk), lanes] += jnp.dot(a.astype(BF16), do_b, preferred_element_type=F32)
                dq = dq + lax.dot_general(dz_b, k_blk, tn_dims, preferred_element_type=F32)
                out.append((seen + blk_tot, gsum + jnp.sum(g, axis=0, keepdims=True), dq))
            return tuple(out)

        init = tuple((jnp.zeros((1, bq), F32), jnp.zeros((1, bq), F32), jnp.zeros((bq, hd), F32))
                     for _ in range(2))
        carry = lax.fori_loop(n_diag * i - reached, n_diag * i, lambda kb, c: tile(kb, c, None), init)
        for j in range(n_diag):
            carry = tile(n_diag * i + j, carry, j * blk)
        dq_ref[...] = jnp.concatenate([carry[0][2], carry[1][2]], axis=1)

    full = lambda off: pl.BlockSpec((None, t, LANES), lambda p, i: (off + p, 0, 0))
    q_blk = pl.BlockSpec((None, bq, LANES), lambda p, i: (p, i, 0))
    slab = jax.ShapeDtypeStruct((per, t, LANES), F32)
    return pl.pallas_call(
        body, name=name, grid=(per, nq),
        in_specs=[q_blk, full(per), full(2 * per), pl.BlockSpec((bq, LANES), lambda p, i: (i, p))],
        out_specs=[q_blk, full(0), full(0)],
        out_shape=[slab, slab, slab],
        compiler_params=_params("parallel", "arbitrary"),
    )(qkv_n, qkv_n, qkv_n, do)


def _sb_mixer_fwd(x, gain, w_qkv_t, qk_gains, w_out, tag):
    n_heads = w_out.shape[0] // SB_HEAD_DIM
    h = _rms_fwd(x, gain, name=f"sb_norm_{tag}")
    qkv = _matmul(h, w_qkv_t, "nt", out_slabs=True, tn_cap=256, name=f"sb_qkv_{tag}")
    qkv_n = _sb_qk_norm_fwd(qkv, qk_gains, name=f"sb_qknorm_{tag}")
    o = _sb_attn_fwd(qkv_n, n_heads, name=f"sb_attn_{tag}")
    x_new = _matmul(o, w_out, "nn", resid=x, name=f"sb_out_{tag}")
    return x_new, (x, h, qkv, qkv_n, o)


def _sb_mixer_bwd(dx, saved, gain, w_qkv_t, qk_gains, w_out, tag):
    x, h, qkv, qkv_n, o = saved
    do = _matmul(dx, w_out, "nt", name=f"sb_do_{tag}")
    dw_out = _matmul_tn(o, dx, name=f"sb_dwout_{tag}")
    dq, dk, dv = _sb_attn_bwd(do, qkv_n, name=f"sb_dattn_{tag}")
    dqkv, dqk_gains = _sb_qk_norm_bwd(dq, dk, dv, qkv, qk_gains, name=f"sb_dqknorm_{tag}")
    dw_qkv_t = _matmul_tn(dqkv, h, a_slabs=True, name=f"sb_dwqkv_{tag}")
    dh = _matmul(dqkv, w_qkv_t, "nn", a_slabs=True, name=f"sb_dh_{tag}")
    dx_in, dgain = _rms_bwd(x, gain, dh, dx, name=f"sb_dnorm_{tag}")
    return dx_in, dw_qkv_t, dqk_gains, dw_out, dgain


MESH = pl.DeviceIdType.MESH


def _position():
    return lax.axis_index("x"), lax.axis_index("y"), lax.axis_index("c")


def _all_gather(shard, *, name, in_vmem):
    rows, n = shard.shape
    space = pltpu.VMEM if in_vmem else pltpu.HBM

    def body(x_ref, out_ref, send_sems, recv_sems, local_sem):
        x, y, c = _position()
        me, sibling = (x, y, c), (x, y, 1 - c)
        chips = [(1 - x, y), (x, 1 - y), (1 - x, 1 - y)]

        def block(px, py, pc):
            return out_ref.at[4 * px + 2 * py + pc]

        def copy(k, blk, to, src=None):
            return pltpu.make_async_remote_copy(
                src_ref=block(*blk) if src is None else src, dst_ref=block(*blk),
                send_sem=send_sems.at[k], recv_sem=recv_sems.at[k], device_id=to, device_id_type=MESH)

        mine = pltpu.make_async_copy(x_ref, block(*me), local_sem)
        mine.start()
        first = [copy(0, me, sibling, src=x_ref)]
        first += [copy(1 + j, me, (*chip, c), src=x_ref) for j, chip in enumerate(chips)]
        for cp in first:
            cp.start()
        passed = [copy(4 + j, (*chip, c), sibling) for j, chip in enumerate(chips)]
        for j, chip in enumerate(chips):
            copy(1 + j, (*chip, c), me).wait_recv()
            passed[j].start()
        copy(0, sibling, me).wait_recv()
        for j, chip in enumerate(chips):
            copy(4 + j, (*chip, 1 - c), me).wait_recv()
        for cp in first + passed:
            cp.wait_send()
        mine.wait()

    return pl.pallas_call(
        body, name=name,
        out_shape=jax.ShapeDtypeStruct((N_DEV, rows, n), shard.dtype),
        in_specs=[pl.BlockSpec(memory_space=space)], out_specs=pl.BlockSpec(memory_space=space),
        scratch_shapes=[pltpu.SemaphoreType.DMA((7,)), pltpu.SemaphoreType.DMA((7,)), pltpu.SemaphoreType.DMA],
        compiler_params=pltpu.CompilerParams(vmem_limit_bytes=V7X_VMEM_LIMIT_BYTES),
    )(shard)


def _exchange_sibling(parts, *, name):
    nchip, _, rows, n = parts.shape

    def body(p_ref, recv_ref, send_sem, recv_sem):
        x, y, c = _position()
        cp = pltpu.make_async_remote_copy(src_ref=p_ref.at[:, 1 - c], dst_ref=recv_ref, send_sem=send_sem,
                                          recv_sem=recv_sem, device_id=(x, y, 1 - c), device_id_type=MESH)
        cp.start()
        cp.wait()

    return pl.pallas_call(
        body, name=name,
        out_shape=jax.ShapeDtypeStruct((nchip, rows, n), parts.dtype),
        in_specs=[pl.BlockSpec(memory_space=pltpu.HBM)], out_specs=pl.BlockSpec(memory_space=pltpu.HBM),
        scratch_shapes=[pltpu.SemaphoreType.DMA, pltpu.SemaphoreType.DMA],
    )(parts)


def _exchange_chips(chip_sums, *, name):
    _, rows, n = chip_sums.shape

    def body(s_ref, recv_ref, send_sems, recv_sems):
        x, y, c = _position()
        chips = [(1 - x, y), (x, 1 - y), (1 - x, 1 - y)]
        copies = [pltpu.make_async_remote_copy(
            src_ref=s_ref.at[2 * cx + cy], dst_ref=recv_ref.at[j], send_sem=send_sems.at[j],
            recv_sem=recv_sems.at[j], device_id=(cx, cy, c), device_id_type=MESH)
            for j, (cx, cy) in enumerate(chips)]
        for cp in copies:
            cp.start()
        for cp in copies:
            cp.wait()

    return pl.pallas_call(
        body, name=name,
        out_shape=jax.ShapeDtypeStruct((3, rows, n), chip_sums.dtype),
        in_specs=[pl.BlockSpec(memory_space=pltpu.HBM)], out_specs=pl.BlockSpec(memory_space=pltpu.HBM),
        scratch_shapes=[pltpu.SemaphoreType.DMA((3,)), pltpu.SemaphoreType.DMA((3,))],
    )(chip_sums)


def _add_pairs(parts, recv, c_mine, *, name):
    nchip, _, rows, n = parts.shape
    tr = _tile(rows, 512, SUBLANES)

    def body(c_ref, a_ref, b_ref, o_ref, wire_ref):
        s = a_ref[...] + b_ref[...]
        o_ref[...] = s
        wire_ref[...] = s.astype(WIRE_DTYPE)

    out_blk = pl.BlockSpec((None, tr, n), lambda k, i, c: (k, i, 0))
    return pl.pallas_call(
        body, name=name,
        grid_spec=pltpu.PrefetchScalarGridSpec(
            num_scalar_prefetch=1, grid=(nchip, rows // tr),
            in_specs=[pl.BlockSpec((None, None, tr, n), lambda k, i, c: (k, c[0], i, 0)),
                      pl.BlockSpec((None, tr, n), lambda k, i, c: (k, i, 0))],
            out_specs=[out_blk, out_blk]),
        out_shape=[jax.ShapeDtypeStruct((nchip, rows, n), parts.dtype),
                   jax.ShapeDtypeStruct((nchip, rows, n), WIRE_DTYPE)],
        compiler_params=_params("parallel", "parallel"),
    )(c_mine, parts, recv)


def _adamw_math(w, g, m, v):
    m = ADAM_B1 * m + (1.0 - ADAM_B1) * g
    v = ADAM_B2 * v + (1.0 - ADAM_B2) * (g * g)
    m_hat = m / (1.0 - ADAM_B1 ** ADAM_STEP)
    v_hat = v / (1.0 - ADAM_B2 ** ADAM_STEP)
    delta = -ADAM_LR * (m_hat / (jnp.sqrt(v_hat) + ADAM_EPS) + ADAM_WD * w)
    return delta, m, v


def _adamw_sharded(chip_sums, recv, k_mine, w, m, v, *, name):
    rows, n = w.shape
    tr = _tile(rows, 256, SUBLANES)

    def body(k_ref, s_ref, r_ref, w_ref, m_ref, v_ref, g_out, d_out, m_out, v_out):
        g = ((s_ref[...] + r_ref[0].astype(F32)) + r_ref[1].astype(F32)) + r_ref[2].astype(F32)
        delta, m_new, v_new = _adamw_math(w_ref[...], g, m_ref[...], v_ref[...])
        g_out[...] = g
        d_out[...] = delta
        m_out[...] = m_new
        v_out[...] = v_new

    blk = pl.BlockSpec((tr, n), lambda i, k: (i, 0))
    out = jax.ShapeDtypeStruct((rows, n), F32)
    return pl.pallas_call(
        body, name=name,
        grid_spec=pltpu.PrefetchScalarGridSpec(
            num_scalar_prefetch=1, grid=(rows // tr,),
            in_specs=[pl.BlockSpec((None, tr, n), lambda i, k: (k[0], i, 0)),
                      pl.BlockSpec((3, tr, n), lambda i, k: (0, i, 0)), blk, blk, blk],
            out_specs=[blk, blk, blk, blk]),
        out_shape=[out, out, out, out],
        compiler_params=_params("parallel"),
    )(k_mine, chip_sums, recv, w, m, v)


SMALL_ROWS = 40
ROW_MIX_NORM, ROW_FFN_NORM, ROW_CONV_B, ROW_OUT_NORM, ROW_POOL_SCALE, ROW_CONV_W = 0, 4, 8, 12, 14, 16
ROW_SSD_VEC, ROW_QK_GAIN, ROW_LOSS = 32, 33, 34


def _adamw_small(gathered, w, m, v, *, name):
    _, rows, n = gathered.shape

    def body(a_ref, w_ref, m_ref, v_ref, g_out, d_out, m_out, v_out):
        g = a_ref[0]
        for d in range(1, N_DEV):
            g = g + a_ref[d]
        row = lax.broadcasted_iota(jnp.int32, (rows, 1), 0)
        g = jnp.where(row == ROW_QK_GAIN, g + pltpu.roll(g, SB_HEAD_DIM, 1), g)
        g = jnp.where(row == ROW_LOSS, jnp.sum(g, axis=1, keepdims=True), g)
        g_out[...] = g
        delta, m_new, v_new = _adamw_math(w_ref[...], g, m_ref[...], v_ref[...])
        d_out[...] = delta
        m_out[...] = m_new
        v_out[...] = v_new

    out = jax.ShapeDtypeStruct((rows, n), F32)
    return pl.pallas_call(body, name=name, out_shape=[out, out, out, out])(gathered, w, m, v)


BIG_WEIGHTS = ("pool_in", "pool_group", "ssd_in", "ssd_out", "sb_qkv", "sb_out", "ffn_gate", "ffn_up", "ffn_down")
COLUMN_SHARDED = ("ssd_in", "sb_qkv", "ffn_gate", "ffn_up")
ROW_PAD = 512
WIRE_DTYPE = jnp.bfloat16


def _to_rows(name, shard, d):
    if name in COLUMN_SHARDED:
        shard = jnp.swapaxes(shard, -1, -2)
    return shard.reshape(-1, d)


def _from_rows(name, rows, shard_shape):
    if name in COLUMN_SHARDED:
        lead, k, n = shard_shape
        return jnp.swapaxes(rows.reshape(lead, n, k), -1, -2)
    return rows.reshape(shard_shape)


def _pad_rows(a, total):
    return jnp.pad(a, ((0, total - a.shape[0]),) + ((0, 0),) * (a.ndim - 1))


def _exact_bf16_rows(v, d):
    words = lax.bitcast_convert_type(v.reshape(-1), WIRE_DTYPE).reshape(-1)
    return _pad_rows(words, -(-words.shape[0] // d) * d).reshape(-1, d)


def _exact_f32(rows, count):
    words = rows.reshape(rows.shape[0], -1)[:, :2 * count].reshape(rows.shape[0], count, 2)
    return lax.bitcast_convert_type(words, F32)


def _device_blocks(full, d):
    return full.reshape(N_DEV, -1, d)


def kernel(x, mix_norm, pool_in, pool_group, pool_scale, ssd_in, ssd_conv_w, ssd_conv_b, ssd_dt_bias, ssd_a_log, ssd_d, ssd_out_norm, ssd_out, sb_qkv, sb_q_norm, sb_k_norm, sb_out, ffn_norm, ffn_gate, ffn_up, ffn_down, loss_target, m_mix_norm, m_pool_in, m_pool_group, m_pool_scale, m_ssd_in, m_ssd_conv_w, m_ssd_conv_b, m_ssd_dt_bias, m_ssd_a_log, m_ssd_d, m_ssd_out_norm, m_ssd_out, m_sb_qkv, m_sb_q_norm, m_sb_k_norm, m_sb_out, m_ffn_norm, m_ffn_gate, m_ffn_up, m_ffn_down, v_mix_norm, v_pool_in, v_pool_group, v_pool_scale, v_ssd_in, v_ssd_conv_w, v_ssd_conv_b, v_ssd_dt_bias, v_ssd_a_log, v_ssd_d, v_ssd_out_norm, v_ssd_out, v_sb_qkv, v_sb_q_norm, v_sb_k_norm, v_sb_out, v_ffn_norm, v_ffn_gate, v_ffn_up, v_ffn_down):
    weights = dict(mix_norm=mix_norm, pool_in=pool_in, pool_group=pool_group, pool_scale=pool_scale, ssd_in=ssd_in,
                   ssd_conv_w=ssd_conv_w, ssd_conv_b=ssd_conv_b, ssd_dt_bias=ssd_dt_bias, ssd_a_log=ssd_a_log,
                   ssd_d=ssd_d, ssd_out_norm=ssd_out_norm, ssd_out=ssd_out, sb_qkv=sb_qkv, sb_q_norm=sb_q_norm,
                   sb_k_norm=sb_k_norm, sb_out=sb_out, ffn_norm=ffn_norm, ffn_gate=ffn_gate, ffn_up=ffn_up,
                   ffn_down=ffn_down)
    mom1 = dict(mix_norm=m_mix_norm, pool_in=m_pool_in, pool_group=m_pool_group, pool_scale=m_pool_scale,
                ssd_in=m_ssd_in, ssd_conv_w=m_ssd_conv_w, ssd_conv_b=m_ssd_conv_b, ssd_dt_bias=m_ssd_dt_bias,
                ssd_a_log=m_ssd_a_log, ssd_d=m_ssd_d, ssd_out_norm=m_ssd_out_norm, ssd_out=m_ssd_out,
                sb_qkv=m_sb_qkv, sb_q_norm=m_sb_q_norm, sb_k_norm=m_sb_k_norm, sb_out=m_sb_out,
                ffn_norm=m_ffn_norm, ffn_gate=m_ffn_gate, ffn_up=m_ffn_up, ffn_down=m_ffn_down)
    mom2 = dict(mix_norm=v_mix_norm, pool_in=v_pool_in, pool_group=v_pool_group, pool_scale=v_pool_scale,
                ssd_in=v_ssd_in, ssd_conv_w=v_ssd_conv_w, ssd_conv_b=v_ssd_conv_b, ssd_dt_bias=v_ssd_dt_bias,
                ssd_a_log=v_ssd_a_log, ssd_d=v_ssd_d, ssd_out_norm=v_ssd_out_norm, ssd_out=v_ssd_out,
                sb_qkv=v_sb_qkv, sb_q_norm=v_sb_q_norm, sb_k_norm=v_sb_k_norm, sb_out=v_sb_out,
                ffn_norm=v_ffn_norm, ffn_gate=v_ffn_gate, ffn_up=v_ffn_up, ffn_down=v_ffn_down)
    names = list(weights)
    depth, d = mix_norm.shape
    xs, ys, cs = _position()
    dev = 4 * xs + 2 * ys + cs
    chip = 2 * xs + ys

    seg = {}
    row = 0
    for name in BIG_WEIGHTS:
        n_rows = weights[name].size // d
        seg[name] = (row, n_rows)
        row += n_rows
    big_rows = row
    n_scale, n_convw = pool_scale.size, ssd_conv_w.size
    exact = jnp.concatenate([_exact_bf16_rows(pool_scale, d), _exact_bf16_rows(ssd_conv_w, d)], axis=0)
    scale_rows = _exact_bf16_rows(pool_scale, d).shape[0]
    packed_rows = -(-(big_rows + exact.shape[0]) // ROW_PAD) * ROW_PAD

    def pack(tree, dtype):
        return jnp.concatenate([_to_rows(n, tree[n], d).astype(dtype) for n in BIG_WEIGHTS], axis=0)

    w_wire = _pad_rows(jnp.concatenate([pack(weights, WIRE_DTYPE), exact], axis=0), packed_rows)
    gathered = _all_gather(w_wire, name="gather_weights", in_vmem=False)

    def seg_of(name):
        a, n = seg[name]
        return gathered[:, a:a + n]

    n_pool, n_ssd, n_sb = pool_in.shape[0], ssd_in.shape[0], sb_qkv.shape[0]
    assert n_ssd == 1 and n_sb == 1
    w_pool_in = seg_of("pool_in").reshape(N_DEV, n_pool, -1, d).transpose(1, 0, 2, 3).reshape(n_pool, d, d)
    grp = pool_group.shape
    w_pool_group = seg_of("pool_group").reshape(N_DEV, grp[0], grp[1], grp[2], grp[3]).transpose(1, 2, 0, 3, 4)
    w_pool_group = w_pool_group.reshape(grp[0], grp[1], grp[3], grp[3])
    w_ssd_in_t = seg_of("ssd_in").reshape(-1, d)
    w_ssd_out = seg_of("ssd_out").reshape(-1, d)
    w_sb_qkv_t = seg_of("sb_qkv").reshape(-1, d)
    w_sb_out = seg_of("sb_out").reshape(-1, d)
    hidden = ffn_down.shape[1] * N_DEV
    w_gate_t = seg_of("ffn_gate").reshape(N_DEV, depth, -1, d).transpose(1, 0, 2, 3).reshape(depth, hidden, d)
    w_up_t = seg_of("ffn_up").reshape(N_DEV, depth, -1, d).transpose(1, 0, 2, 3).reshape(depth, hidden, d)
    w_down = seg_of("ffn_down").reshape(N_DEV, depth, -1, d).transpose(1, 0, 2, 3).reshape(depth, hidden, d)
    exact_all = gathered[:, big_rows:big_rows + exact.shape[0]]
    scale_full = _exact_f32(exact_all[:, :scale_rows], n_scale).reshape(N_DEV, n_pool, -1)
    scale_full = scale_full.transpose(1, 0, 2).reshape(n_pool, d)
    convw_full = _exact_f32(exact_all[:, scale_rows:], n_convw).reshape(N_DEV, SSD_CONV, -1)
    convw_full = convw_full.transpose(1, 0, 2).reshape(SSD_CONV, -1)

    d_inner = w_ssd_out.shape[0]
    n_zx = w_ssd_in_t.shape[0] - ssd_dt_bias.shape[1]
    w_zx_t = w_ssd_in_t[:n_zx]
    n_ssd_heads = ssd_dt_bias.shape[1]
    n_ssd_groups = n_ssd_heads // SSD_HEADS_PER_GROUP
    w_dt_t = _ssd_group_pad(w_ssd_in_t[n_zx:].T, n_ssd_groups).T
    par = _pad_rows(_ssd_group_pad(jnp.concatenate([ssd_dt_bias, ssd_a_log, ssd_d], axis=0), n_ssd_groups), SUBLANES)
    d_rep = jnp.repeat(ssd_d[0], SSD_HEAD_DIM)[None]
    qk_gains = jnp.zeros((SUBLANES, LANES), F32).at[0].set(jnp.tile(sb_q_norm[0], 2)).at[1].set(jnp.tile(sb_k_norm[0], 2))

    act = x[0]
    saved = []
    for i in range(depth):
        kind, j = i % 3, i // 3
        gain = mix_norm[i:i + 1]
        if kind == 0:
            act, s = _pool_mixer_fwd(act, gain, w_pool_in[j], w_pool_group[j], scale_full[j:j + 1], f"l{i}")
        elif kind == 1:
            act, s = _ssd_mixer_fwd(act, gain, w_zx_t, w_dt_t, convw_full, ssd_conv_b, par, d_rep, ssd_out_norm,
                                    w_ssd_out, f"l{i}")
        else:
            act, s = _sb_mixer_fwd(act, gain, w_sb_qkv_t, qk_gains, w_sb_out, f"l{i}")
        act, f = _ffn_fwd(act, ffn_norm[i:i + 1], w_gate_t[i], w_up_t[i], w_down[i], f"l{i}")
        saved.append((s, f))
    dact, loss_cols = _loss_head(act, loss_target[0], name="loss_head")

    g_mix_norm, g_ffn_norm = [None] * depth, [None] * depth
    g_pool_in, g_pool_group, g_pool_scale = [None] * n_pool, [None] * n_pool, [None] * n_pool
    g_gate_t, g_up_t, g_down = [None] * depth, [None] * depth, [None] * depth
    for i in reversed(range(depth)):
        kind, j = i % 3, i // 3
        gain = mix_norm[i:i + 1]
        s, f = saved[i]
        dact, g_gate_t[i], g_up_t[i], g_down[i], g_ffn_norm[i] = _ffn_bwd(
            dact, f, ffn_norm[i:i + 1], w_gate_t[i], w_up_t[i], w_down[i], f"l{i}")
        if kind == 0:
            dact, g_pool_in[j], g_pool_group[j], g_pool_scale[j], g_mix_norm[i] = _pool_mixer_bwd(
                dact, s, gain, w_pool_in[j], w_pool_group[j], scale_full[j:j + 1], f"l{i}")
        elif kind == 1:
            (dact, g_zx_t, g_dt_t, g_conv_w, g_conv_b, g_par, g_out_norm, g_ssd_out,
             g_mix_norm[i]) = _ssd_mixer_bwd(dact, s, gain, w_zx_t, w_dt_t, convw_full, ssd_conv_b, par, d_rep,
                                             ssd_out_norm, w_ssd_out, f"l{i}")
        else:
            dact, g_qkv_t, g_qk_gains, g_sb_out, g_mix_norm[i] = _sb_mixer_bwd(
                dact, s, gain, w_sb_qkv_t, qk_gains, w_sb_out, f"l{i}")
    grad_x = dact[None]

    full_grads = dict(
        pool_in=g_pool_in, pool_group=[gg[k] for gg in g_pool_group for k in range(gg.shape[0])],
        ssd_in=[g_zx_t, _ssd_group_unpad(g_dt_t.T, n_ssd_groups).T], ssd_out=[g_ssd_out],
        sb_qkv=[g_qkv_t], sb_out=[g_sb_out], ffn_gate=g_gate_t, ffn_up=g_up_t, ffn_down=g_down)
    full_grads["ssd_in"] = [jnp.concatenate(full_grads["ssd_in"], axis=0)]
    blocks = [_device_blocks(g, d) for n in BIG_WEIGHTS for g in full_grads[n]]
    blocks.append(jnp.zeros((N_DEV, packed_rows - big_rows, d), F32))
    parts = jnp.concatenate(blocks, axis=1).reshape(N_DEV // 2, 2, packed_rows, d)
    from_sibling = _exchange_sibling(parts, name="reduce_sibling")
    chip_sums, chip_sums_wire = _add_pairs(parts, from_sibling, cs.reshape(1).astype(jnp.int32),
                                           name="reduce_sibling_add")
    from_chips = _exchange_chips(chip_sums_wire, name="reduce_chips")

    def pack_f32(tree):
        return _pad_rows(pack(tree, F32), packed_rows)

    big_out = _adamw_sharded(chip_sums, from_chips, chip.reshape(1).astype(jnp.int32), pack_f32(weights),
                             pack_f32(mom1), pack_f32(mom2), name="adamw_sharded")

    def small_pack(mix, ffn, conv_b, out_norm, scale, conv_w, vec, qk, loss=None):
        buf = jnp.zeros((SMALL_ROWS, d), F32)
        buf = buf.at[ROW_MIX_NORM:ROW_MIX_NORM + depth].set(mix).at[ROW_FFN_NORM:ROW_FFN_NORM + depth].set(ffn)
        buf = buf.at[ROW_CONV_B:ROW_CONV_B + conv_b.size // d].set(conv_b.reshape(-1, d))
        buf = buf.at[ROW_OUT_NORM:ROW_OUT_NORM + out_norm.size // d].set(out_norm.reshape(-1, d))
        buf = buf.at[ROW_POOL_SCALE:ROW_POOL_SCALE + n_pool].set(scale)
        buf = buf.at[ROW_CONV_W:ROW_CONV_W + conv_w.size // d].set(conv_w.reshape(-1, d))
        buf = buf.at[ROW_SSD_VEC].set(vec.reshape(-1)).at[ROW_QK_GAIN].set(qk.reshape(-1))
        if loss is not None:
            buf = buf.at[ROW_LOSS].set(loss.reshape(-1))
        return buf

    def small_params(tree):
        scale = lax.dynamic_update_slice(jnp.zeros((n_pool, d), F32), tree["pool_scale"],
                                         (0, dev * tree["pool_scale"].shape[1]))
        conv_w = lax.dynamic_update_slice(jnp.zeros(convw_full.shape, F32), tree["ssd_conv_w"][0],
                                          (0, dev * tree["ssd_conv_w"].shape[2]))
        vec = jnp.zeros((SUBLANES, LANES), F32)
        vec = vec.at[0, :n_ssd_heads].set(tree["ssd_dt_bias"][0]).at[1, :n_ssd_heads].set(tree["ssd_a_log"][0])
        vec = vec.at[2, :n_ssd_heads].set(tree["ssd_d"][0])
        qk = jnp.zeros((SUBLANES, LANES), F32)
        qk = qk.at[0, SB_HEAD_DIM:].set(tree["sb_q_norm"][0]).at[1, SB_HEAD_DIM:].set(tree["sb_k_norm"][0])
        return small_pack(tree["mix_norm"], tree["ffn_norm"], tree["ssd_conv_b"], tree["ssd_out_norm"], scale,
                          conv_w, vec, qk)

    small_partial = small_pack(jnp.concatenate(g_mix_norm, axis=0), jnp.concatenate(g_ffn_norm, axis=0), g_conv_b,
                               g_out_norm, jnp.concatenate(g_pool_scale, axis=0), g_conv_w,
                               jnp.zeros((SUBLANES, LANES), F32).at[:3, :n_ssd_heads].set(
                                   _ssd_group_unpad(g_par[:3], n_ssd_groups)), g_qk_gains,
                               loss_cols)
    small_all = _all_gather(small_partial, name="gather_small", in_vmem=True)
    small_out = _adamw_small(small_all, small_params(weights), small_params(mom1), small_params(mom2),
                             name="adamw_small")
    loss = small_out[0][ROW_LOSS, 0]

    def unpack(big, small):
        out = {}
        for name in BIG_WEIGHTS:
            a, n = seg[name]
            out[name] = _from_rows(name, big[a:a + n], weights[name].shape)
        out["mix_norm"] = small[ROW_MIX_NORM:ROW_MIX_NORM + depth]
        out["ffn_norm"] = small[ROW_FFN_NORM:ROW_FFN_NORM + depth]
        out["ssd_conv_b"] = small[ROW_CONV_B:ROW_CONV_B + ssd_conv_b.size // d].reshape(ssd_conv_b.shape)
        out["ssd_out_norm"] = small[ROW_OUT_NORM:ROW_OUT_NORM + ssd_out_norm.size // d].reshape(ssd_out_norm.shape)
        out["pool_scale"] = lax.dynamic_slice(small[ROW_POOL_SCALE:ROW_POOL_SCALE + n_pool],
                                              (0, dev * pool_scale.shape[1]), pool_scale.shape)
        conv_w = small[ROW_CONV_W:ROW_CONV_W + convw_full.size // d].reshape(convw_full.shape)
        out["ssd_conv_w"] = lax.dynamic_slice(conv_w, (0, dev * ssd_conv_w.shape[2]), ssd_conv_w.shape[1:])[None]
        vec = small[ROW_SSD_VEC].reshape(SUBLANES, LANES)
        out["ssd_dt_bias"], out["ssd_a_log"], out["ssd_d"] = (vec[r:r + 1, :n_ssd_heads] for r in range(3))
        qk = small[ROW_QK_GAIN].reshape(SUBLANES, LANES)
        out["sb_q_norm"], out["sb_k_norm"] = qk[0:1, SB_HEAD_DIM:], qk[1:2, SB_HEAD_DIM:]
        return [out[n] for n in names]

    results = [unpack(b, s) for b, s in zip(big_out, small_out)]
    return (loss, grad_x, *results[0], *results[1], *results[2], *results[3])
```

```python
import math

import jax
import jax.numpy as jnp
from jax import lax
from jax.experimental import pallas as pl
from jax.experimental.pallas import tpu as pltpu

F32 = jnp.float32
BF16 = jnp.bfloat16

N_DEV = 8
NORM_EPS = 1e-6
V7X_VMEM_LIMIT_BYTES = 48 * 1024 * 1024
LANES = 128
SUBLANES = 8

POOL_WINDOWS = (2, 4, 8, 16)
SSD_CHUNK = 256
SSD_HEAD_DIM = 64
SSD_STATE = 128
SSD_HEADS_PER_GROUP = 4
SSD_CONV = 4
SB_HEAD_DIM = 64
SB_BLOCK = 128
SB_QUERY_BLOCK = 256

ADAM_LR = 0.001
ADAM_B1 = 0.9
ADAM_B2 = 0.999
ADAM_EPS = 1e-08
ADAM_WD = 0.01
ADAM_STEP = 10


def _params(*sem):
    return pltpu.CompilerParams(dimension_semantics=sem, vmem_limit_bytes=V7X_VMEM_LIMIT_BYTES)


def _tile(n, cap, mult):
    best = None
    for t in range(mult, min(n, cap) + 1, mult):
        if n % t == 0:
            best = t
    return best or n


def _load_slabs(ref, slabs):
    if not slabs:
        return ref[...]
    return jnp.concatenate([ref[p] for p in range(ref.shape[0])], axis=1)


def _matmul(a, b, mode, *, name, out_dtype=F32, resid=None, a_slabs=False, out_slabs=False,
            tm_cap=1024, tn_cap=1024, tk_cap=2048):
    pairs = list(zip(a, b)) if isinstance(a, (list, tuple)) else [(a, b)]
    a, b = pairs[0]
    if a_slabs:
        m, k = a.shape[1], a.shape[0] * LANES
    else:
        m, k = a.shape
    n = b.shape[1] if mode == "nn" else b.shape[0]
    assert (b.shape[0] if mode == "nn" else b.shape[1]) == k
    assert all(pa.shape == a.shape and pb.shape == b.shape for pa, pb in pairs)
    tm, tn, tk = _tile(m, tm_cap, SUBLANES), _tile(n, tn_cap, LANES), _tile(k, tk_cap, LANES)
    nk = k // tk
    dn = (((1,), (0,)), ((), ())) if mode == "nn" else (((1,), (1,)), ((), ()))
    has_resid = resid is not None
    n_pairs = len(pairs)

    def body(*refs):
        ab_refs, rest = refs[:2 * n_pairs], refs[2 * n_pairs:]
        r_ref = rest[0] if has_resid else None
        o_ref = rest[1] if has_resid else rest[0]
        kk = pl.program_id(2)

        def partial():
            total = None
            for p in range(n_pairs):
                d = lax.dot_general(_load_slabs(ab_refs[2 * p], a_slabs).astype(BF16),
                                    ab_refs[2 * p + 1][...].astype(BF16), dn, preferred_element_type=F32)
                total = d if total is None else total + d
            return total

        def finish(r):
            if has_resid:
                r = r + r_ref[...]
            if out_slabs:
                for p in range(tn // LANES):
                    o_ref[p] = r[:, p * LANES:(p + 1) * LANES].astype(out_dtype)
            else:
                o_ref[...] = r.astype(out_dtype)

        if nk == 1:
            finish(partial())
        else:
            acc = rest[-1]

            @pl.when(kk == 0)
            def _():
                acc[...] = jnp.zeros_like(acc)

            acc[...] += partial()

            @pl.when(kk == nk - 1)
            def _():
                finish(acc[...])

    b_spec = (pl.BlockSpec((tk, tn), lambda i, j, kk: (kk, j)) if mode == "nn"
              else pl.BlockSpec((tn, tk), lambda i, j, kk: (j, kk)))
    a_spec = (pl.BlockSpec((tk // LANES, tm, LANES), lambda i, j, kk: (kk, i, 0)) if a_slabs
              else pl.BlockSpec((tm, tk), lambda i, j, kk: (i, kk)))
    in_specs = [a_spec, b_spec] * n_pairs
    args = [t for pair in pairs for t in pair]
    if has_resid:
        in_specs.append(pl.BlockSpec((tm, tn), lambda i, j, kk: (i, j)))
        args.append(resid)
    if out_slabs:
        out_spec = pl.BlockSpec((tn // LANES, tm, LANES), lambda i, j, kk: (j, i, 0))
        out_shape = jax.ShapeDtypeStruct((n // LANES, m, LANES), out_dtype)
    else:
        out_spec = pl.BlockSpec((tm, tn), lambda i, j, kk: (i, j))
        out_shape = jax.ShapeDtypeStruct((m, n), out_dtype)
    return pl.pallas_call(
        body, name=name, grid=(m // tm, n // tn, nk),
        in_specs=in_specs, out_specs=out_spec, out_shape=out_shape,
        scratch_shapes=[pltpu.VMEM((tm, tn), F32)] if nk > 1 else [],
        compiler_params=_params("parallel", "parallel", "arbitrary"),
    )(*args)


def _matmul_tn(a, b, *, name, a_slabs=False, ta_cap=1024, tb_cap=1024, tr_cap=512):
    if a_slabs:
        r, ka = a.shape[1], a.shape[0] * LANES
    else:
        r, ka = a.shape
    nb = b.shape[1]
    assert b.shape[0] == r
    ta, tb, tr = _tile(ka, ta_cap, LANES), _tile(nb, tb_cap, LANES), _tile(r, tr_cap, SUBLANES)

    def body(a_ref, b_ref, o_ref):
        @pl.when(pl.program_id(2) == 0)
        def _():
            o_ref[...] = jnp.zeros_like(o_ref)

        o_ref[...] += lax.dot_general(_load_slabs(a_ref, a_slabs).astype(BF16), b_ref[...].astype(BF16),
                                      (((0,), (0,)), ((), ())), preferred_element_type=F32)

    a_spec = (pl.BlockSpec((ta // LANES, tr, LANES), lambda i, j, kk: (i, kk, 0)) if a_slabs
              else pl.BlockSpec((tr, ta), lambda i, j, kk: (kk, i)))
    return pl.pallas_call(
        body, name=name, grid=(ka // ta, nb // tb, r // tr),
        in_specs=[a_spec, pl.BlockSpec((tr, tb), lambda i, j, kk: (kk, j))],
        out_specs=pl.BlockSpec((ta, tb), lambda i, j, kk: (i, j)),
        out_shape=jax.ShapeDtypeStruct((ka, nb), F32),
        compiler_params=_params("parallel", "parallel", "arbitrary"),
    )(a, b)


def _rms_fwd(x, gain, *, name):
    t, d = x.shape
    tm = _tile(t, 512, SUBLANES)

    def body(x_ref, g_ref, o_ref):
        xv = x_ref[...]
        r = lax.rsqrt(jnp.mean(xv * xv, axis=-1, keepdims=True) + NORM_EPS)
        o_ref[...] = (xv * r * g_ref[...]).astype(BF16)

    return pl.pallas_call(
        body, name=name, grid=(t // tm,),
        in_specs=[pl.BlockSpec((tm, d), lambda i: (i, 0)), pl.BlockSpec((1, d), lambda i: (0, 0))],
        out_specs=pl.BlockSpec((tm, d), lambda i: (i, 0)),
        out_shape=jax.ShapeDtypeStruct((t, d), BF16),
        compiler_params=_params("parallel"),
    )(x, gain)


def _rms_bwd(x, gain, dh, dres, *, name):
    t, d = x.shape
    tm = _tile(t, 512, SUBLANES)

    def body(x_ref, g_ref, dh_ref, dres_ref, dx_ref, dg_ref):
        @pl.when(pl.program_id(0) == 0)
        def _():
            dg_ref[...] = jnp.zeros_like(dg_ref)

        xv = x_ref[...]
        r = lax.rsqrt(jnp.mean(xv * xv, axis=-1, keepdims=True) + NORM_EPS)
        xhat = xv * r
        dhv = dh_ref[...]
        u = dhv * g_ref[...]
        dx_ref[...] = dres_ref[...] + r * (u - xhat * jnp.mean(u * xhat, axis=-1, keepdims=True))
        dg_ref[...] += jnp.sum(dhv * xhat, axis=0, keepdims=True)

    return pl.pallas_call(
        body, name=name, grid=(t // tm,),
        in_specs=[pl.BlockSpec((tm, d), lambda i: (i, 0)), pl.BlockSpec((1, d), lambda i: (0, 0)),
                  pl.BlockSpec((tm, d), lambda i: (i, 0)), pl.BlockSpec((tm, d), lambda i: (i, 0))],
        out_specs=[pl.BlockSpec((tm, d), lambda i: (i, 0)), pl.BlockSpec((1, d), lambda i: (0, 0))],
        out_shape=[jax.ShapeDtypeStruct((t, d), F32), jax.ShapeDtypeStruct((1, d), F32)],
        compiler_params=_params("arbitrary"),
    )(x, gain, dh, dres)


def _loss_head(y, target, *, name):
    t, d = y.shape
    tm = _tile(t, 512, SUBLANES)

    def body(y_ref, t_ref, dy_ref, l_ref):
        @pl.when(pl.program_id(0) == 0)
        def _():
            l_ref[...] = jnp.zeros_like(l_ref)

        e = y_ref[...] - t_ref[...]
        dy_ref[...] = e * (1.0 / d)
        l_ref[...] += jnp.sum(e * e, axis=0, keepdims=True) * (0.5 / d)

    return pl.pallas_call(
        body, name=name, grid=(t // tm,),
        in_specs=[pl.BlockSpec((tm, d), lambda i: (i, 0)), pl.BlockSpec((tm, d), lambda i: (i, 0))],
        out_specs=[pl.BlockSpec((tm, d), lambda i: (i, 0)), pl.BlockSpec((1, d), lambda i: (0, 0))],
        out_shape=[jax.ShapeDtypeStruct((t, d), F32), jax.ShapeDtypeStruct((1, d), F32)],
        compiler_params=_params("arbitrary"),
    )(y, target)


def _sigmoid(v):
    return 0.5 * jnp.tanh(0.5 * v) + 0.5


FFN_TOKEN_TILE = 512
FFN_HIDDEN_TILE = 1408
NT_DIMS = (((1,), (1,)), ((), ()))


def _ffn_up(h, w_gate_t, w_up_t, *, name):
    t, d = h.shape
    f = w_gate_t.shape[0]
    tm, tn = _tile(t, FFN_TOKEN_TILE, SUBLANES), _tile(f, FFN_HIDDEN_TILE, LANES)

    def body(h_ref, g_ref, u_ref, s_ref, a_ref, b_ref):
        hv = h_ref[...].astype(BF16)
        av = lax.dot_general(hv, g_ref[...].astype(BF16), NT_DIMS, preferred_element_type=F32)
        bv = lax.dot_general(hv, u_ref[...].astype(BF16), NT_DIMS, preferred_element_type=F32)
        s_ref[...] = (av * _sigmoid(av) * bv).astype(BF16)
        a_ref[...] = av.astype(BF16)
        b_ref[...] = bv.astype(BF16)

    w_spec = pl.BlockSpec((tn, d), lambda j, i: (j, 0))
    out_spec = pl.BlockSpec((tm, tn), lambda j, i: (i, j))
    out = jax.ShapeDtypeStruct((t, f), BF16)
    return pl.pallas_call(
        body, name=name, grid=(f // tn, t // tm),
        in_specs=[pl.BlockSpec((tm, d), lambda j, i: (i, 0)), w_spec, w_spec],
        out_specs=[out_spec, out_spec, out_spec], out_shape=[out, out, out],
        compiler_params=_params("parallel", "parallel"),
    )(h, w_gate_t, w_up_t)


def _ffn_dact(dx, w_down, a, b, *, name):
    t, d = dx.shape
    f = w_down.shape[0]
    tm, tn = _tile(t, FFN_TOKEN_TILE, SUBLANES), _tile(f, FFN_HIDDEN_TILE, LANES)

    def body(dx_ref, w_ref, a_ref, b_ref, da_ref, db_ref):
        ds = lax.dot_general(dx_ref[...].astype(BF16), w_ref[...].astype(BF16), NT_DIMS, preferred_element_type=F32)
        av = a_ref[...].astype(F32)
        sg = _sigmoid(av)
        da_ref[...] = (ds * b_ref[...].astype(F32) * (sg * (1.0 + av * (1.0 - sg)))).astype(BF16)
        db_ref[...] = (ds * av * sg).astype(BF16)

    blk = pl.BlockSpec((tm, tn), lambda j, i: (i, j))
    out = jax.ShapeDtypeStruct((t, f), BF16)
    return pl.pallas_call(
        body, name=name, grid=(f // tn, t // tm),
        in_specs=[pl.BlockSpec((tm, d), lambda j, i: (i, 0)), pl.BlockSpec((tn, d), lambda j, i: (j, 0)), blk, blk],
        out_specs=[blk, blk], out_shape=[out, out],
        compiler_params=_params("parallel", "parallel"),
    )(dx, w_down, a, b)


def _ffn_fwd(x, gain, w_gate_t, w_up_t, w_down, tag):
    h = _rms_fwd(x, gain, name=f"ffn_norm_{tag}")
    s, a, b = _ffn_up(h, w_gate_t, w_up_t, name=f"ffn_up_{tag}")
    x_new = _matmul(s, w_down, "nn", resid=x, tn_cap=1024, tk_cap=2816, name=f"ffn_down_{tag}")
    return x_new, (x, h, a, b, s)


def _ffn_bwd(dx, saved, gain, w_gate_t, w_up_t, w_down, tag):
    x, h, a, b, s = saved
    da, db = _ffn_dact(dx, w_down, a, b, name=f"ffn_dact_{tag}")
    wide = dict(ta_cap=FFN_HIDDEN_TILE, tb_cap=1024, tr_cap=512)
    dw_down = _matmul_tn(s, dx, name=f"ffn_dwdown_{tag}", **wide)
    dw_gate_t = _matmul_tn(da, h, name=f"ffn_dwgate_{tag}", **wide)
    dw_up_t = _matmul_tn(db, h, name=f"ffn_dwup_{tag}", **wide)
    dh = _matmul([da, db], [w_gate_t, w_up_t], "nn", tm_cap=512, tn_cap=1024, tk_cap=2816, name=f"ffn_dh_{tag}")
    dx_in, dgain = _rms_bwd(x, gain, dh, dx, name=f"ffn_dnorm_{tag}")
    return dx_in, dw_gate_t, dw_up_t, dw_down, dgain


POOL_HALO = 16


def _shift_rows(v, k):
    n = v.shape[0]
    return pltpu.roll(v, k % n, 0)


def _window_sum(v, w, direction):
    k = 1
    while k < w:
        v = v + _shift_rows(v, direction * k)
        k *= 2
    return v


def _pool_fwd(u, x, w_group, scale, *, name):
    t, d = u.shape
    ng, dg = w_group.shape[0], w_group.shape[1]
    tm = _tile(t, 512, POOL_HALO)
    hb = tm // POOL_HALO

    def body(u_ref, halo_ref, x_ref, w_ref, s_ref, xo_ref, p_ref, y_ref):
        i, g = pl.program_id(0), pl.program_id(1)
        halo = jnp.where(i > 0, halo_ref[...], 0.0)
        ext = jnp.concatenate([halo, u_ref[...]], axis=0)
        pos = i * tm + lax.broadcasted_iota(jnp.int32, (tm, 1), 0)
        for gi, win in enumerate(POOL_WINDOWS):
            @pl.when(g == gi)
            def _(win=win):
                tot = _window_sum(ext, win, 1)[POOL_HALO:]
                cnt = jnp.minimum(pos + 1, win).astype(F32)
                p = (tot / cnt - u_ref[...]).astype(BF16)
                p_ref[...] = p
                y = jnp.dot(p, w_ref[...].astype(BF16), preferred_element_type=F32)
                y_ref[...] = y
                xo_ref[...] = x_ref[...] + y * s_ref[...]

    blk = pl.BlockSpec((tm, dg), lambda i, g: (i, g))
    return pl.pallas_call(
        body, name=name, grid=(t // tm, ng),
        in_specs=[blk, pl.BlockSpec((POOL_HALO, dg), lambda i, g: (jnp.maximum(i * hb - 1, 0), g)), blk,
                  pl.BlockSpec((None, dg, dg), lambda i, g: (g, 0, 0)), pl.BlockSpec((1, dg), lambda i, g: (0, g))],
        out_specs=[blk, blk, blk],
        out_shape=[jax.ShapeDtypeStruct((t, d), F32), jax.ShapeDtypeStruct((t, d), BF16),
                   jax.ShapeDtypeStruct((t, d), F32)],
        compiler_params=_params("parallel", "parallel"),
    )(u, u, x, w_group, scale)


def _pool_bwd(dx, p, y_pre, w_group, scale, *, name):
    t, d = dx.shape
    ng, dg = w_group.shape[0], w_group.shape[1]
    tm = _tile(t, 512, POOL_HALO)
    hb = tm // POOL_HALO
    nt = t // tm

    def body(dx_ref, nxt_ref, p_ref, y_ref, w_ref, s_ref, du_ref, dw_ref, ds_ref):
        g, i = pl.program_id(0), pl.program_id(1)

        @pl.when(i == 0)
        def _():
            dw_ref[...] = jnp.zeros_like(dw_ref)
            ds_ref[...] = jnp.zeros_like(ds_ref)

        dxv = dx_ref[...]
        ds_ref[...] += jnp.sum(dxv * y_ref[...], axis=0, keepdims=True)
        nxt = jnp.where(i < nt - 1, nxt_ref[...], 0.0)
        dyp = (jnp.concatenate([dxv, nxt], axis=0) * s_ref[...]).astype(BF16)
        dw_ref[...] += lax.dot_general(p_ref[...], dyp[:tm], (((0,), (0,)), ((), ())), preferred_element_type=F32)
        dp = lax.dot_general(dyp, w_ref[...].astype(BF16), (((1,), (1,)), ((), ())), preferred_element_type=F32)
        pos = i * tm + lax.broadcasted_iota(jnp.int32, (tm + POOL_HALO, 1), 0)
        for gi, win in enumerate(POOL_WINDOWS):
            @pl.when(g == gi)
            def _(win=win):
                q = dp / jnp.minimum(pos + 1, win).astype(F32)
                du_ref[...] = (_window_sum(q, win, -1)[:tm] - dp[:tm]).astype(BF16)

    blk = pl.BlockSpec((tm, dg), lambda g, i: (i, g))
    return pl.pallas_call(
        body, name=name, grid=(ng, nt),
        in_specs=[blk, pl.BlockSpec((POOL_HALO, dg), lambda g, i: (jnp.minimum((i + 1) * hb, t // POOL_HALO - 1), g)),
                  blk, blk, pl.BlockSpec((None, dg, dg), lambda g, i: (g, 0, 0)),
                  pl.BlockSpec((1, dg), lambda g, i: (0, g))],
        out_specs=[blk, pl.BlockSpec((None, dg, dg), lambda g, i: (g, 0, 0)), pl.BlockSpec((1, dg), lambda g, i: (0, g))],
        out_shape=[jax.ShapeDtypeStruct((t, d), BF16), jax.ShapeDtypeStruct((ng, dg, dg), F32),
                   jax.ShapeDtypeStruct((1, d), F32)],
        compiler_params=_params("parallel", "arbitrary"),
    )(dx, dx, p, y_pre, w_group, scale)


def _pool_mixer_fwd(x, gain, w_in, w_group, scale, tag):
    h = _rms_fwd(x, gain, name=f"pool_norm_{tag}")
    u = _matmul(h, w_in, "nn", name=f"pool_in_{tag}")
    x_new, p, y_pre = _pool_fwd(u, x, w_group, scale, name=f"pool_mix_{tag}")
    return x_new, (x, h, p, y_pre)


def _pool_mixer_bwd(dx, saved, gain, w_in, w_group, scale, tag):
    x, h, p, y_pre = saved
    du, dw_group, dscale = _pool_bwd(dx, p, y_pre, w_group, scale, name=f"pool_dmix_{tag}")
    dw_in = _matmul_tn(h, du, name=f"pool_dwin_{tag}")
    dh = _matmul(du, w_in, "nt", name=f"pool_dh_{tag}")
    dx_in, dgain = _rms_bwd(x, gain, dh, dx, name=f"pool_dnorm_{tag}")
    return dx_in, dw_in, dw_group, dscale, dgain


CONV_HALO = 8
HIGHEST = lax.Precision.HIGHEST
NEG_BIG = -1e30


def _softplus(v):
    return jnp.maximum(v, 0.0) + jnp.log(1.0 + jnp.exp(-jnp.abs(v)))


def _dot_exact(a, b):
    return jnp.dot(a, b, precision=HIGHEST, preferred_element_type=F32)


def _conv_taps(ext, w_ref, off, rows):
    acc = None
    for k in range(SSD_CONV):
        shift = SSD_CONV - 1 - k
        v = (_shift_rows(ext, shift) if shift else ext)[off:off + rows] * w_ref[k:k + 1, :]
        acc = v if acc is None else acc + v
    return acc


def _ssd_conv_fwd(zx, conv_w, conv_b, col0, *, name):
    t = zx.shape[0]
    c = conv_w.shape[1]
    tm, tc = _tile(t, 512, CONV_HALO), _tile(c, 512, LANES)
    hb, cb0 = tm // CONV_HALO, col0 // tc
    assert col0 % tc == 0

    def body(x_ref, halo_ref, w_ref, b_ref, o_ref):
        halo = jnp.where(pl.program_id(0) > 0, halo_ref[...], 0.0)
        ext = jnp.concatenate([halo, x_ref[...]], axis=0)
        pre = _conv_taps(ext, w_ref, CONV_HALO, tm) + b_ref[...]
        o_ref[...] = pre * _sigmoid(pre)

    return pl.pallas_call(
        body, name=name, grid=(t // tm, c // tc),
        in_specs=[pl.BlockSpec((tm, tc), lambda i, j: (i, j + cb0)),
                  pl.BlockSpec((CONV_HALO, tc), lambda i, j: (jnp.maximum(i * hb - 1, 0), j + cb0)),
                  pl.BlockSpec((SSD_CONV, tc), lambda i, j: (0, j)), pl.BlockSpec((1, tc), lambda i, j: (0, j))],
        out_specs=pl.BlockSpec((tm, tc), lambda i, j: (i, j)),
        out_shape=jax.ShapeDtypeStruct((t, c), F32),
        compiler_params=_params("parallel", "parallel"),
    )(zx, zx, conv_w, conv_b)


def _ssd_conv_bwd(dxa, zx, conv_w, conv_b, col0, *, name):
    t = zx.shape[0]
    c = conv_w.shape[1]
    tm, tc = _tile(t, 512, CONV_HALO), _tile(c, 512, LANES)
    hb, cb0, nt = tm // CONV_HALO, col0 // tc, t // tm
    last_halo = t // CONV_HALO - 1

    def body(x_ref, prev_ref, nxt_ref, d_ref, dnxt_ref, w_ref, b_ref, dx_ref, dw_ref, db_ref):
        i = pl.program_id(1)

        @pl.when(i == 0)
        def _():
            dw_ref[...] = jnp.zeros_like(dw_ref)
            db_ref[...] = jnp.zeros_like(db_ref)

        prev = jnp.where(i > 0, prev_ref[...], 0.0)
        has_next = i < nt - 1
        ext = jnp.concatenate([prev, x_ref[...], jnp.where(has_next, nxt_ref[...], 0.0)], axis=0)
        pre = _conv_taps(ext, w_ref, CONV_HALO, tm + CONV_HALO) + b_ref[...]
        sg = _sigmoid(pre)
        dact = jnp.concatenate([d_ref[...], jnp.where(has_next, dnxt_ref[...], 0.0)], axis=0)
        dpre = dact * (sg * (1.0 + pre * (1.0 - sg)))
        db_ref[...] += jnp.sum(dpre[:tm], axis=0, keepdims=True)
        acc = None
        for k in range(SSD_CONV):
            shift = SSD_CONV - 1 - k
            src = (_shift_rows(ext, shift) if shift else ext)[CONV_HALO:CONV_HALO + tm]
            dw_ref[k:k + 1, :] += jnp.sum(dpre[:tm] * src, axis=0, keepdims=True)
            v = (_shift_rows(dpre, -shift) if shift else dpre)[:tm] * w_ref[k:k + 1, :]
            acc = v if acc is None else acc + v
        dx_ref[...] = acc.astype(BF16)

    main = lambda j, i: (i, j + cb0)
    return pl.pallas_call(
        body, name=name, grid=(c // tc, nt),
        in_specs=[pl.BlockSpec((tm, tc), main),
                  pl.BlockSpec((CONV_HALO, tc), lambda j, i: (jnp.maximum(i * hb - 1, 0), j + cb0)),
                  pl.BlockSpec((CONV_HALO, tc), lambda j, i: (jnp.minimum((i + 1) * hb, last_halo), j + cb0)),
                  pl.BlockSpec((tm, tc), lambda j, i: (i, j)),
                  pl.BlockSpec((CONV_HALO, tc), lambda j, i: (jnp.minimum((i + 1) * hb, last_halo), j)),
                  pl.BlockSpec((SSD_CONV, tc), lambda j, i: (0, j)), pl.BlockSpec((1, tc), lambda j, i: (0, j))],
        out_specs=[pl.BlockSpec((tm, tc), lambda j, i: (i, j)), pl.BlockSpec((SSD_CONV, tc), lambda j, i: (0, j)),
                   pl.BlockSpec((1, tc), lambda j, i: (0, j))],
        out_shape=[jax.ShapeDtypeStruct((t, c), BF16), jax.ShapeDtypeStruct((SSD_CONV, c), F32),
                   jax.ShapeDtypeStruct((1, c), F32)],
        compiler_params=_params("parallel", "arbitrary"),
    )(zx, zx, zx, dxa, dxa, conv_w, conv_b)


SSD_CUMSUM_PIECES = 3


def _ssd_group_pad(v, n_groups):
    lead = v.shape[:-1]
    v = v.reshape(*lead, n_groups, SSD_HEADS_PER_GROUP)
    v = jnp.pad(v, [(0, 0)] * (len(lead) + 1) + [(0, LANES - SSD_HEADS_PER_GROUP)])
    return v.reshape(*lead, n_groups * LANES)


def _ssd_group_unpad(v, n_groups):
    lead = v.shape[:-1]
    return v.reshape(*lead, n_groups, LANES)[..., :SSD_HEADS_PER_GROUP].reshape(*lead, -1)


def _ssd_chunk_common(dtp_ref, par_ref):
    ell = SSD_CHUNK
    dt = _softplus(dtp_ref[...] + par_ref[0:1, :])
    a = -jnp.exp(par_ref[1:2, :])
    row = lax.broadcasted_iota(jnp.int32, (ell, ell), 0)
    col = lax.broadcasted_iota(jnp.int32, (ell, ell), 1)
    acum = _split_dot(dt * a, (row >= col).astype(BF16), SSD_CUMSUM_PIECES, left=True)
    return dt, a, acum, acum.T, row, col


def _ssd_scan_fwd(xa, dtp, par, n_groups, *, name):
    t = xa.shape[0]
    ell, hd, hpg, ns = SSD_CHUNK, SSD_HEAD_DIM, SSD_HEADS_PER_GROUP, SSD_STATE
    gw = hpg * hd
    nc = t // ell
    b_blk0, c_blk0 = n_groups * gw // ns, n_groups * gw // ns + n_groups

    def body(xs_ref, b_ref, c_ref, dtp_ref, par_ref, y_ref, sin_ref, st):
        @pl.when(pl.program_id(1) == 0)
        def _():
            st[...] = jnp.zeros_like(st)

        dt, _, acum, acum_t, row, col = _ssd_chunk_common(dtp_ref, par_ref)
        bb, cc = b_ref[...].astype(BF16), c_ref[...].astype(BF16)
        cb = lax.dot_general(cc, bb, NT_DIMS, preferred_element_type=F32)
        for hh in range(hpg):
            lanes = slice(hh * hd, (hh + 1) * hd)
            col_a, row_a = acum[:, hh:hh + 1], acum_t[hh:hh + 1, :]
            decay = jnp.exp(jnp.where(row >= col, col_a - row_a, NEG_BIG))
            xdt = xs_ref[:, lanes] * dt[:, hh:hh + 1]
            s_h = st[hh]
            sin_ref[lanes, :] = s_h
            y = jnp.dot((cb * decay).astype(BF16), xdt.astype(BF16), preferred_element_type=F32)
            y += jnp.exp(col_a) * lax.dot_general(cc, s_h.astype(BF16), NT_DIMS, preferred_element_type=F32)
            y_ref[:, lanes] = y
            a_last = acum[ell - 1:ell, hh:hh + 1]
            w = jnp.exp(a_last - col_a)
            st[hh] = jnp.exp(a_last) * s_h + lax.dot_general(
                (xdt * w).astype(BF16), bb, (((0,), (0,)), ((), ())), preferred_element_type=F32)

    return pl.pallas_call(
        body, name=name, grid=(n_groups, nc),
        in_specs=[pl.BlockSpec((ell, gw), lambda g, c: (c, g)),
                  pl.BlockSpec((ell, ns), lambda g, c: (c, b_blk0 + g)),
                  pl.BlockSpec((ell, ns), lambda g, c: (c, c_blk0 + g)),
                  pl.BlockSpec((ell, LANES), lambda g, c: (c, g)),
                  pl.BlockSpec((SUBLANES, LANES), lambda g, c: (0, g))],
        out_specs=[pl.BlockSpec((ell, gw), lambda g, c: (c, g)),
                   pl.BlockSpec((None, None, gw, ns), lambda g, c: (c, g, 0, 0))],
        out_shape=[jax.ShapeDtypeStruct((t, n_groups * gw), F32),
                   jax.ShapeDtypeStruct((nc, n_groups, gw, ns), F32)],
        scratch_shapes=[pltpu.VMEM((hpg, hd, ns), F32)],
        compiler_params=_params("parallel", "arbitrary"),
    )(xa, xa, xa, dtp, par)


def _ssd_scan_bwd(dy, xa, dtp, par, s_in, n_groups, *, name):
    t = xa.shape[0]
    ell, hd, hpg, ns = SSD_CHUNK, SSD_HEAD_DIM, SSD_HEADS_PER_GROUP, SSD_STATE
    gw = hpg * hd
    nc = t // ell
    b_blk0, c_blk0 = n_groups * gw // ns, n_groups * gw // ns + n_groups
    nt_dims = (((1,), (1,)), ((), ()))
    tn_dims = (((0,), (0,)), ((), ()))

    def body(dy_ref, xs_ref, b_ref, c_ref, dtp_ref, par_ref, sin_ref,
             dxs_ref, db_ref, dc_ref, ddtp_ref, dpar_ref, dst):
        @pl.when(pl.program_id(1) == 0)
        def _():
            dst[...] = jnp.zeros_like(dst)
            dpar_ref[...] = jnp.zeros_like(dpar_ref)

        dtg, a_g, acum, acum_t, row, col = _ssd_chunk_common(dtp_ref, par_ref)
        bb, cc = b_ref[...].astype(BF16), c_ref[...].astype(BF16)
        cb = lax.dot_general(cc, bb, nt_dims, preferred_element_type=F32)
        cb_t = lax.dot_general(bb, cc, nt_dims, preferred_element_type=F32)
        lane = lax.broadcasted_iota(jnp.int32, (1, LANES), 1)
        dcb = jnp.zeros((ell, ell), F32)
        dcb_t = jnp.zeros((ell, ell), F32)
        dc = jnp.zeros((ell, ns), F32)
        db = jnp.zeros((ell, ns), F32)
        dacum = jnp.zeros((ell, LANES), F32)
        xsum = jnp.zeros((ell, LANES), F32)
        dsum = jnp.zeros((1, LANES), F32)
        for hh in range(hpg):
            lanes = slice(hh * hd, (hh + 1) * hd)
            onehot = (lane == hh).astype(F32)
            col_a, row_a = acum[:, hh:hh + 1], acum_t[hh:hh + 1, :]
            decay = jnp.exp(jnp.where(row >= col, col_a - row_a, NEG_BIG))
            decay_t = jnp.exp(jnp.where(col >= row, row_a - col_a, NEG_BIG))
            e_col = jnp.exp(col_a)
            a_last = acum[ell - 1:ell, hh:hh + 1]
            w = jnp.exp(a_last - col_a)
            e_last = jnp.exp(a_last)
            xs_h, dy_h = xs_ref[:, lanes], dy_ref[:, lanes]
            dt_h = dtg[:, hh:hh + 1]
            xdt = xs_h * dt_h
            xdt_b, dy_b = xdt.astype(BF16), dy_h.astype(BF16)
            s_h, ds_h = sin_ref[lanes, :], dst[hh]
            dm_decay = lax.dot_general(dy_b, xdt_b, nt_dims, preferred_element_type=F32) * decay
            dm_decay_t = lax.dot_general(xdt_b, dy_b, nt_dims, preferred_element_type=F32) * decay_t
            dcb += dm_decay
            dcb_t += dm_decay_t
            m_t = cb_t * decay_t
            dac = jnp.sum(dm_decay * cb, axis=1, keepdims=True) - jnp.sum(dm_decay_t * cb_t, axis=1, keepdims=True)
            b_ds = lax.dot_general(bb, ds_h.astype(BF16), nt_dims, preferred_element_type=F32)
            dxdt = jnp.dot(m_t.astype(BF16), dy_b, preferred_element_type=F32) + w * b_ds
            c_s = lax.dot_general(cc, s_h.astype(BF16), nt_dims, preferred_element_type=F32)
            dc += e_col * jnp.dot(dy_b, s_h.astype(BF16), preferred_element_type=F32)
            db += w * jnp.dot(xdt_b, ds_h.astype(BF16), preferred_element_type=F32)
            dac += jnp.sum(dy_h * c_s, axis=1, keepdims=True) * e_col
            q = jnp.sum(xdt * b_ds, axis=1, keepdims=True) * w
            dac -= q
            d_last = jnp.sum(q, axis=0, keepdims=True) + e_last * jnp.sum(
                jnp.sum(s_h * ds_h, axis=1, keepdims=True), axis=0, keepdims=True)
            is_last = lax.broadcasted_iota(jnp.int32, (ell, 1), 0) == ell - 1
            dac += jnp.where(is_last, d_last, 0.0)
            dacum += dac * onehot
            dst[hh] = e_last * ds_h + lax.dot_general((dy_h * e_col).astype(BF16), cc, tn_dims,
                                                      preferred_element_type=F32)
            dxs_ref[:, lanes] = dxdt * dt_h + dy_h * par_ref[2:3, hh:hh + 1]
            xsum += jnp.sum(dxdt * xs_h, axis=1, keepdims=True) * onehot
            dsum += jnp.sum(jnp.sum(dy_h * xs_h, axis=1, keepdims=True), axis=0, keepdims=True) * onehot
        dc_ref[...] = dc + jnp.dot(dcb.astype(BF16), bb, preferred_element_type=F32)
        db_ref[...] = db + jnp.dot(dcb_t.astype(BF16), cc, preferred_element_type=F32)
        dda = _split_dot(dacum, (col >= row).astype(BF16), SSD_CUMSUM_PIECES, left=True)
        ddtp = (xsum + dda * a_g) * _sigmoid(dtp_ref[...] + par_ref[0:1, :])
        ddtp_ref[...] = ddtp
        dpar_ref[0:1, :] += jnp.sum(ddtp, axis=0, keepdims=True)
        dpar_ref[1:2, :] += jnp.sum(dda * dtg, axis=0, keepdims=True) * a_g
        dpar_ref[2:3, :] += dsum

    rev = lambda i: nc - 1 - i
    return pl.pallas_call(
        body, name=name, grid=(n_groups, nc),
        in_specs=[pl.BlockSpec((ell, gw), lambda g, i: (rev(i), g)),
                  pl.BlockSpec((ell, gw), lambda g, i: (rev(i), g)),
                  pl.BlockSpec((ell, ns), lambda g, i: (rev(i), b_blk0 + g)),
                  pl.BlockSpec((ell, ns), lambda g, i: (rev(i), c_blk0 + g)),
                  pl.BlockSpec((ell, LANES), lambda g, i: (rev(i), g)),
                  pl.BlockSpec((SUBLANES, LANES), lambda g, i: (0, g)),
                  pl.BlockSpec((None, None, gw, ns), lambda g, i: (rev(i), g, 0, 0))],
        out_specs=[pl.BlockSpec((ell, gw), lambda g, i: (rev(i), g)),
                   pl.BlockSpec((ell, ns), lambda g, i: (rev(i), g)),
                   pl.BlockSpec((ell, ns), lambda g, i: (rev(i), g)),
                   pl.BlockSpec((ell, LANES), lambda g, i: (rev(i), g)),
                   pl.BlockSpec((SUBLANES, LANES), lambda g, i: (0, g))],
        out_shape=[jax.ShapeDtypeStruct((t, n_groups * gw), F32), jax.ShapeDtypeStruct((t, n_groups * ns), F32),
                   jax.ShapeDtypeStruct((t, n_groups * ns), F32), jax.ShapeDtypeStruct((t, n_groups * LANES), F32),
                   jax.ShapeDtypeStruct((SUBLANES, n_groups * LANES), F32)],
        scratch_shapes=[pltpu.VMEM((hpg, hd, ns), F32)],
        compiler_params=_params("parallel", "arbitrary"),
    )(dy, xa, xa, xa, dtp, par, s_in)


def _ssd_gate_fwd(y, xa, zx, d_rep, out_norm, *, name):
    t, di = y.shape
    gw = SSD_HEADS_PER_GROUP * SSD_HEAD_DIM
    tm = _tile(t, 512, SUBLANES)

    def body(y_ref, xs_ref, z_ref, d_ref, n_ref, o_ref):
        zv = z_ref[...]
        gt = (y_ref[...] + d_ref[...] * xs_ref[...]) * (zv * _sigmoid(zv))
        r = lax.rsqrt(jnp.mean(gt * gt, axis=-1, keepdims=True) + NORM_EPS)
        o_ref[...] = (gt * r * n_ref[...]).astype(BF16)

    blk = pl.BlockSpec((tm, gw), lambda i, g: (i, g))
    vec = pl.BlockSpec((1, gw), lambda i, g: (0, g))
    return pl.pallas_call(
        body, name=name, grid=(t // tm, di // gw),
        in_specs=[blk, blk, blk, vec, vec], out_specs=blk,
        out_shape=jax.ShapeDtypeStruct((t, di), BF16),
        compiler_params=_params("parallel", "parallel"),
    )(y, xa, zx, d_rep, out_norm)


def _ssd_gate_bwd(dgn, y, xa, zx, d_rep, out_norm, *, name):
    t, di = y.shape
    gw = SSD_HEADS_PER_GROUP * SSD_HEAD_DIM
    tm = _tile(t, 512, SUBLANES)

    def body(dg_ref, y_ref, xs_ref, z_ref, d_ref, n_ref, dy_ref, dz_ref, dn_ref):
        @pl.when(pl.program_id(1) == 0)
        def _():
            dn_ref[...] = jnp.zeros_like(dn_ref)

        zv = z_ref[...]
        sg = _sigmoid(zv)
        sz = zv * sg
        y2 = y_ref[...] + d_ref[...] * xs_ref[...]
        gt = y2 * sz
        r = lax.rsqrt(jnp.mean(gt * gt, axis=-1, keepdims=True) + NORM_EPS)
        ghat = gt * r
        dgv = dg_ref[...]
        dn_ref[...] += jnp.sum(dgv * ghat, axis=0, keepdims=True)
        u = dgv * n_ref[...]
        dgt = r * (u - ghat * jnp.mean(u * ghat, axis=-1, keepdims=True))
        dy_ref[...] = dgt * sz
        dz_ref[...] = (dgt * y2 * (sg * (1.0 + zv * (1.0 - sg)))).astype(BF16)

    blk = pl.BlockSpec((tm, gw), lambda g, i: (i, g))
    vec = pl.BlockSpec((1, gw), lambda g, i: (0, g))
    return pl.pallas_call(
        body, name=name, grid=(di // gw, t // tm),
        in_specs=[blk, blk, blk, blk, vec, vec], out_specs=[blk, blk, vec],
        out_shape=[jax.ShapeDtypeStruct((t, di), F32), jax.ShapeDtypeStruct((t, di), BF16),
                   jax.ShapeDtypeStruct((1, di), F32)],
        compiler_params=_params("parallel", "arbitrary"),
    )(dgn, y, xa, zx, d_rep, out_norm)


def _ssd_mixer_fwd(x, gain, w_zx_t, w_dt_t, conv_w, conv_b, par, d_rep, out_norm, w_out, tag):
    di = w_out.shape[0]
    n_groups = di // (SSD_HEADS_PER_GROUP * SSD_HEAD_DIM)
    h = _rms_fwd(x, gain, name=f"ssd_norm_{tag}")
    zx = _matmul(h, w_zx_t, "nt", name=f"ssd_in_{tag}")
    dtp = _matmul(h, w_dt_t, "nt", name=f"ssd_dt_{tag}")
    xa = _ssd_conv_fwd(zx, conv_w, conv_b, di, name=f"ssd_conv_{tag}")
    y, s_in = _ssd_scan_fwd(xa, dtp, par, n_groups, name=f"ssd_scan_{tag}")
    gn = _ssd_gate_fwd(y, xa, zx, d_rep, out_norm, name=f"ssd_gate_{tag}")
    x_new = _matmul(gn, w_out, "nn", resid=x, name=f"ssd_out_{tag}")
    return x_new, (x, h, zx, dtp, xa, y, s_in, gn)


def _ssd_mixer_bwd(dx, saved, gain, w_zx_t, w_dt_t, conv_w, conv_b, par, d_rep, out_norm, w_out, tag):
    x, h, zx, dtp, xa, y, s_in, gn = saved
    di = w_out.shape[0]
    n_groups = di // (SSD_HEADS_PER_GROUP * SSD_HEAD_DIM)
    dgn = _matmul(dx, w_out, "nt", name=f"ssd_dgn_{tag}")
    dw_out = _matmul_tn(gn, dx, name=f"ssd_dwout_{tag}")
    dy2, dz, dnorm = _ssd_gate_bwd(dgn, y, xa, zx, d_rep, out_norm, name=f"ssd_dgate_{tag}")
    dxs, db, dc, ddtp, dpar = _ssd_scan_bwd(dy2, xa, dtp, par, s_in, n_groups, name=f"ssd_dscan_{tag}")
    dxa = jnp.concatenate([dxs, db, dc], axis=1)
    dxbc, dconv_w, dconv_b = _ssd_conv_bwd(dxa, zx, conv_w, conv_b, di, name=f"ssd_dconv_{tag}")
    dzx = jnp.concatenate([dz, dxbc], axis=1)
    dw_zx_t = _matmul_tn(dzx, h, name=f"ssd_dwin_{tag}")
    dw_dt_t = _matmul_tn(ddtp, h, name=f"ssd_dwdt_{tag}")
    dh = _matmul(dzx, w_zx_t, "nn", name=f"ssd_dh_{tag}")
    dh = _matmul(ddtp, w_dt_t, "nn", resid=dh, name=f"ssd_dhdt_{tag}")
    dx_in, dgain = _rms_bwd(x, gain, dh, dx, name=f"ssd_dnorm_{tag}")
    return dx_in, dw_zx_t, dw_dt_t, dconv_w, dconv_b, dpar, dnorm, dw_out, dgain


def _sb_qk_norm_fwd(qkv, gains, *, name):
    ns, t, _ = qkv.shape
    per = ns // 3
    tm = _tile(t, 1024, SUBLANES)
    inv_sqrt_d = 1.0 / math.sqrt(SB_HEAD_DIM)

    def body(x_ref, g_ref, o_ref):
        kind = pl.program_id(0) // per
        xv = x_ref[...]

        @pl.when(kind == 2)
        def _():
            o_ref[...] = xv.astype(BF16)

        @pl.when(kind < 2)
        def _():
            left = lax.broadcasted_iota(jnp.int32, (1, LANES), 1) < SB_HEAD_DIM
            sq = xv * xv
            ms = jnp.where(left, jnp.sum(jnp.where(left, sq, 0.0), axis=1, keepdims=True),
                           jnp.sum(jnp.where(left, 0.0, sq), axis=1, keepdims=True)) * (1.0 / SB_HEAD_DIM)
            y = xv * lax.rsqrt(ms + NORM_EPS) * g_ref[pl.ds(kind, 1), :]
            o_ref[...] = (y * jnp.where(kind == 0, inv_sqrt_d, 1.0)).astype(BF16)

    blk = pl.BlockSpec((None, tm, LANES), lambda s, i: (s, i, 0))
    return pl.pallas_call(
        body, name=name, grid=(ns, t // tm),
        in_specs=[blk, pl.BlockSpec((SUBLANES, LANES), lambda s, i: (0, 0))], out_specs=blk,
        out_shape=jax.ShapeDtypeStruct((ns, t, LANES), BF16),
        compiler_params=_params("parallel", "parallel"),
    )(qkv, gains)


def _sb_qk_norm_bwd(dq, dk, dv, qkv, gains, *, name):
    ns, t, _ = qkv.shape
    per = ns // 3
    tm = _tile(t, 1024, SUBLANES)
    inv_sqrt_d = 1.0 / math.sqrt(SB_HEAD_DIM)

    def body(dq_ref, dk_ref, dv_ref, x_ref, g_ref, o_ref, dg_ref):
        s = pl.program_id(0)
        kind = s // per

        @pl.when((s == 0) & (pl.program_id(1) == 0))
        def _():
            dg_ref[...] = jnp.zeros_like(dg_ref)

        @pl.when(kind == 2)
        def _():
            o_ref[...] = dv_ref[...].astype(BF16)

        @pl.when(kind < 2)
        def _():
            xv = x_ref[...]
            dy = jnp.where(kind == 0, dq_ref[...] * inv_sqrt_d, dk_ref[...])
            left = lax.broadcasted_iota(jnp.int32, (1, LANES), 1) < SB_HEAD_DIM

            def halves(v):
                return jnp.where(left, jnp.sum(jnp.where(left, v, 0.0), axis=1, keepdims=True),
                                 jnp.sum(jnp.where(left, 0.0, v), axis=1, keepdims=True))

            r = lax.rsqrt(halves(xv * xv) * (1.0 / SB_HEAD_DIM) + NORM_EPS)
            xhat = xv * r
            u = dy * g_ref[pl.ds(kind, 1), :]
            o_ref[...] = (r * (u - xhat * halves(u * xhat) * (1.0 / SB_HEAD_DIM))).astype(BF16)
            dg_ref[pl.ds(kind, 1), :] += jnp.sum(dy * xhat, axis=0, keepdims=True)

    def grad_blk(kind):
        def index(s, i):
            mine = (s >= kind * per) & (s < (kind + 1) * per)
            return jnp.where(mine, s - kind * per, 0), jnp.where(mine, i, 0), 0
        return pl.BlockSpec((None, tm, LANES), index)

    blk = pl.BlockSpec((None, tm, LANES), lambda s, i: (s, i, 0))
    vec = pl.BlockSpec((SUBLANES, LANES), lambda s, i: (0, 0))
    return pl.pallas_call(
        body, name=name, grid=(ns, t // tm),
        in_specs=[grad_blk(0), grad_blk(1), grad_blk(2), blk, vec], out_specs=[blk, vec],
        out_shape=[jax.ShapeDtypeStruct((ns, t, LANES), BF16), jax.ShapeDtypeStruct((SUBLANES, LANES), F32)],
        compiler_params=_params("arbitrary", "arbitrary"),
    )(dq, dk, dv, qkv, gains)


def _split_dot(v, ones_mat, pieces, left=False):
    total, rest = None, v
    for p in range(pieces):
        part = rest.astype(BF16)
        if p + 1 < pieces:
            rest = rest - part.astype(F32)
        d = (jnp.dot(ones_mat, part, preferred_element_type=F32) if left
             else jnp.dot(part, ones_mat, preferred_element_type=F32))
        total = d if total is None else total + d
    return total


LOGIT_SUM_PIECES = 2
GRAD_SUM_PIECES = 2
LOG_WEIGHT_UNDERFLOW = -105.0


def _sb_attn_fwd(qkv_n, n_heads, *, name):
    ns, t, _ = qkv_n.shape
    per = ns // 3
    bq, blk, hd = SB_QUERY_BLOCK, SB_BLOCK, SB_HEAD_DIM
    nq, n_diag = t // bq, bq // blk

    def body(q_ref, k_ref, v_ref, o_ref):
        i = pl.program_id(1)
        row = lax.broadcasted_iota(jnp.int32, (blk, blk), 0)
        col = lax.broadcasted_iota(jnp.int32, (blk, blk), 1)
        later_keys = (row > col).astype(BF16)
        qry = lax.broadcasted_iota(jnp.int32, (bq, blk), 0)
        key = lax.broadcasted_iota(jnp.int32, (bq, blk), 1)

        def tile(kb, carry, key_offset):
            out = []
            start = pl.multiple_of(kb * blk, blk)
            for hf in range(2):
                lanes = slice(hf * hd, (hf + 1) * hd)
                run, acc = carry[hf]
                z = lax.dot_general(q_ref[:, lanes], k_ref[pl.ds(start, blk), lanes], NT_DIMS,
                                    preferred_element_type=F32)
                sp = _softplus(z)
                lm = -sp if key_offset is None else jnp.where(key + key_offset < qry, -sp, 0.0)
                after = _split_dot(lm, later_keys, LOGIT_SUM_PIECES) + run
                a = jnp.exp(z - sp + after)
                if key_offset is not None:
                    a = jnp.where(key + key_offset < qry, a, 0.0)
                acc = acc + jnp.dot(a.astype(BF16), v_ref[pl.ds(start, blk), lanes], preferred_element_type=F32)
                out.append((run + jnp.sum(lm, axis=1, keepdims=True), acc))
            return tuple(out)

        def live(carry):
            return jnp.max(jnp.maximum(carry[0][0], carry[1][0])) > LOG_WEIGHT_UNDERFLOW

        def step(state):
            s, _, carry = state
            carry = tile(n_diag * i - 1 - s, carry, None)
            return s + 1, live(carry), carry

        carry = tuple((jnp.zeros((bq, 1), F32), jnp.zeros((bq, hd), F32)) for _ in range(2))
        for j in reversed(range(n_diag)):
            carry = tile(n_diag * i + j, carry, j * blk)
        _, _, carry = lax.while_loop(lambda st: (st[0] < n_diag * i) & st[1], step,
                                     (jnp.int32(0), live(carry), carry))
        o_ref[...] = jnp.concatenate([carry[0][1], carry[1][1]], axis=1)

    return pl.pallas_call(
        body, name=name, grid=(per, nq),
        in_specs=[pl.BlockSpec((None, bq, LANES), lambda p, i: (p, i, 0)),
                  pl.BlockSpec((None, t, LANES), lambda p, i: (per + p, 0, 0)),
                  pl.BlockSpec((None, t, LANES), lambda p, i: (2 * per + p, 0, 0))],
        out_specs=pl.BlockSpec((bq, LANES), lambda p, i: (i, p)),
        out_shape=jax.ShapeDtypeStruct((t, n_heads * hd), F32),
        compiler_params=_params("parallel", "arbitrary"),
    )(qkv_n, qkv_n, qkv_n)


def _sb_attn_bwd(do, qkv_n, *, name):
    ns, t, _ = qkv_n.shape
    per = ns // 3
    bq, blk, hd = SB_QUERY_BLOCK, SB_BLOCK, SB_HEAD_DIM
    nq, n_diag = t // bq, bq // blk
    nt_dims = (((1,), (1,)), ((), ()))
    tn_dims = (((0,), (0,)), ((), ()))

    def body(q_ref, k_ref, v_ref, do_ref, dq_ref, dk_ref, dv_ref):
        i = pl.program_id(1)

        @pl.when(i == 0)
        def _():
            dk_ref[...] = jnp.zeros_like(dk_ref)
            dv_ref[...] = jnp.zeros_like(dv_ref)

        row = lax.broadcasted_iota(jnp.int32, (blk, blk), 0)
        col = lax.broadcasted_iota(jnp.int32, (blk, blk), 1)
        later_keys = (col > row).astype(BF16)
        earlier_keys = (col < row).astype(BF16)
        key = lax.broadcasted_iota(jnp.int32, (blk, bq), 0)
        qry = lax.broadcasted_iota(jnp.int32, (blk, bq), 1)
        halves = [slice(hf * hd, (hf + 1) * hd) for hf in range(2)]
        q_hs = [q_ref[:, lanes] for lanes in halves]
        do_bs = [do_ref[:, lanes].astype(BF16) for lanes in halves]

        def scores(kb, hf, key_offset):
            k_blk = k_ref[pl.ds(pl.multiple_of(kb * blk, blk), blk), halves[hf]]
            z = lax.dot_general(k_blk, q_hs[hf], nt_dims, preferred_element_type=F32)
            sp = _softplus(z)
            return k_blk, z, sp, (-sp if key_offset is None else jnp.where(key + key_offset < qry, -sp, 0.0))

        def add_column_sums(tots, kb, key_offset):
            return [tots[hf] + jnp.sum(scores(kb, hf, key_offset)[3], axis=0, keepdims=True) for hf in range(2)]

        def live(tots):
            return jnp.max(jnp.maximum(tots[0], tots[1])) > LOG_WEIGHT_UNDERFLOW

        def reach(state):
            s, _, tots = state
            tots = add_column_sums(tots, n_diag * i - 1 - s, None)
            return s + 1, live(tots), tots

        tots = [jnp.zeros((1, bq), F32)] * 2
        for j in reversed(range(n_diag)):
            tots = add_column_sums(tots, n_diag * i + j, j * blk)
        reached, _, tots = lax.while_loop(lambda st: (st[0] < n_diag * i) & st[1], reach,
                                          (jnp.int32(0), live(tots), tots))

        def tile(kb, carry, key_offset):
            out = []
            start = pl.multiple_of(kb * blk, blk)
            for hf, lanes in enumerate(halves):
                seen, gsum, dq = carry[hf]
                q_h, do_b = q_hs[hf], do_bs[hf]
                k_blk, z, sp, lm = scores(kb, hf, key_offset)
                blk_tot = jnp.sum(lm, axis=0, keepdims=True)
                after = _split_dot(lm, later_keys, LOGIT_SUM_PIECES, left=True) + (tots[hf] - seen - blk_tot)
                a = jnp.exp(z - sp + after)
                if key_offset is not None:
                    a = jnp.where(key + key_offset < qry, a, 0.0)
                da = lax.dot_general(v_ref[pl.ds(start, blk), lanes], do_b, nt_dims, preferred_element_type=F32)
                g = da * a
                before = _split_dot(g, earlier_keys, GRAD_SUM_PIECES, left=True) + gsum
                omb = jnp.exp(-sp)
                dz = g * omb - (1.0 - omb) * before
                if key_offset is not None:
                    dz = jnp.where(key + key_offset < qry, dz, 0.0)
                dz_b = dz.astype(BF16)
                dk_ref[pl.ds(start, blk), lanes] += jnp.dot(dz_b, q_h, preferred_element_type=F32)
                dv_ref[pl.ds(start, blk), lanes] += jnp.dot(a.astype(BF16), do_b, preferred_element_type=F32)
                dq = dq + lax.dot_general(dz_b, k_blk, tn_dims, preferred_element_type=F32)
                out.append((seen + blk_tot, gsum + jnp.sum(g, axis=0, keepdims=True), dq))
            return tuple(out)

        init = tuple((jnp.zeros((1, bq), F32), jnp.zeros((1, bq), F32), jnp.zeros((bq, hd), F32))
                     for _ in range(2))
        carry = lax.fori_loop(n_diag * i - reached, n_diag * i, lambda kb, c: tile(kb, c, None), init)
        for j in range(n_diag):
            carry = tile(n_diag * i + j, carry, j * blk)
        dq_ref[...] = jnp.concatenate([carry[0][2], carry[1][2]], axis=1)

    full = lambda off: pl.BlockSpec((None, t, LANES), lambda p, i: (off + p, 0, 0))
    q_blk = pl.BlockSpec((None, bq, LANES), lambda p, i: (p, i, 0))
    slab = jax.ShapeDtypeStruct((per, t, LANES), F32)
    return pl.pallas_call(
        body, name=name, grid=(per, nq),
        in_specs=[q_blk, full(per), full(2 * per), pl.BlockSpec((bq, LANES), lambda p, i: (i, p))],
        out_specs=[q_blk, full(0), full(0)],
        out_shape=[slab, slab, slab],
        compiler_params=_params("parallel", "arbitrary"),
    )(qkv_n, qkv_n, qkv_n, do)


def _sb_mixer_fwd(x, gain, w_qkv_t, qk_gains, w_out, tag):
    n_heads = w_out.shape[0] // SB_HEAD_DIM
    h = _rms_fwd(x, gain, name=f"sb_norm_{tag}")
    qkv = _matmul(h, w_qkv_t, "nt", out_slabs=True, tn_cap=256, name=f"sb_qkv_{tag}")
    qkv_n = _sb_qk_norm_fwd(qkv, qk_gains, name=f"sb_qknorm_{tag}")
    o = _sb_attn_fwd(qkv_n, n_heads, name=f"sb_attn_{tag}")
    x_new = _matmul(o, w_out, "nn", resid=x, name=f"sb_out_{tag}")
    return x_new, (x, h, qkv, qkv_n, o)


def _sb_mixer_bwd(dx, saved, gain, w_qkv_t, qk_gains, w_out, tag):
    x, h, qkv, qkv_n, o = saved
    do = _matmul(dx, w_out, "nt", name=f"sb_do_{tag}")
    dw_out = _matmul_tn(o, dx, name=f"sb_dwout_{tag}")
    dq, dk, dv = _sb_attn_bwd(do, qkv_n, name=f"sb_dattn_{tag}")
    dqkv, dqk_gains = _sb_qk_norm_bwd(dq, dk, dv, qkv, qk_gains, name=f"sb_dqknorm_{tag}")
    dw_qkv_t = _matmul_tn(dqkv, h, a_slabs=True, name=f"sb_dwqkv_{tag}")
    dh = _matmul(dqkv, w_qkv_t, "nn", a_slabs=True, name=f"sb_dh_{tag}")
    dx_in, dgain = _rms_bwd(x, gain, dh, dx, name=f"sb_dnorm_{tag}")
    return dx_in, dw_qkv_t, dqk_gains, dw_out, dgain


MESH = pl.DeviceIdType.MESH


def _position():
    return lax.axis_index("x"), lax.axis_index("y"), lax.axis_index("c")


def _all_gather(shard, *, name, in_vmem):
    rows, n = shard.shape
    space = pltpu.VMEM if in_vmem else pltpu.HBM

    def body(x_ref, out_ref, send_sems, recv_sems, local_sem):
        x, y, c = _position()
        me, sibling = (x, y, c), (x, y, 1 - c)
        chips = [(1 - x, y), (x, 1 - y), (1 - x, 1 - y)]

        def block(px, py, pc):
            return out_ref.at[4 * px + 2 * py + pc]

        def copy(k, blk, to, src=None):
            return pltpu.make_async_remote_copy(
                src_ref=block(*blk) if src is None else src, dst_ref=block(*blk),
                send_sem=send_sems.at[k], recv_sem=recv_sems.at[k], device_id=to, device_id_type=MESH)

        mine = pltpu.make_async_copy(x_ref, block(*me), local_sem)
        mine.start()
        first = [copy(0, me, sibling, src=x_ref)]
        first += [copy(1 + j, me, (*chip, c), src=x_ref) for j, chip in enumerate(chips)]
        for cp in first:
            cp.start()
        passed = [copy(4 + j, (*chip, c), sibling) for j, chip in enumerate(chips)]
        for j, chip in enumerate(chips):
            copy(1 + j, (*chip, c), me).wait_recv()
            passed[j].start()
        copy(0, sibling, me).wait_recv()
        for j, chip in enumerate(chips):
            copy(4 + j, (*chip, 1 - c), me).wait_recv()
        for cp in first + passed:
            cp.wait_send()
        mine.wait()

    return pl.pallas_call(
        body, name=name,
        out_shape=jax.ShapeDtypeStruct((N_DEV, rows, n), shard.dtype),
        in_specs=[pl.BlockSpec(memory_space=space)], out_specs=pl.BlockSpec(memory_space=space),
        scratch_shapes=[pltpu.SemaphoreType.DMA((7,)), pltpu.SemaphoreType.DMA((7,)), pltpu.SemaphoreType.DMA],
        compiler_params=pltpu.CompilerParams(vmem_limit_bytes=V7X_VMEM_LIMIT_BYTES),
    )(shard)


def _exchange_sibling(pieces, offsets, rows, *, name):
    nchip, _, _, n = pieces[0].shape
    count = len(pieces)

    def body(*refs):
        piece_refs, (own_ref, recv_ref, send_sems, recv_sems, local_sems) = refs[:count], refs[count:]
        x, y, c = _position()
        copies = []
        for i, (p_ref, off) in enumerate(zip(piece_refs, offsets)):
            rows_i = pl.ds(off, p_ref.shape[2])
            copies.append(pltpu.make_async_copy(p_ref.at[:, c], own_ref.at[:, rows_i], local_sems.at[i]))
            copies.append(pltpu.make_async_remote_copy(
                src_ref=p_ref.at[:, 1 - c], dst_ref=recv_ref.at[:, rows_i], send_sem=send_sems.at[i],
                recv_sem=recv_sems.at[i], device_id=(x, y, 1 - c), device_id_type=MESH))
        for cp in copies:
            cp.start()
        for cp in copies:
            cp.wait()

    hbm = pl.BlockSpec(memory_space=pltpu.HBM)
    out = jax.ShapeDtypeStruct((nchip, rows, n), pieces[0].dtype)
    return pl.pallas_call(
        body, name=name, out_shape=[out, out], in_specs=[hbm] * count, out_specs=[hbm, hbm],
        scratch_shapes=[pltpu.SemaphoreType.DMA((count,)), pltpu.SemaphoreType.DMA((count,)),
                        pltpu.SemaphoreType.DMA((count,))],
    )(*pieces)


def _exchange_chips(chip_sums, *, name):
    _, rows, n = chip_sums.shape

    def body(s_ref, recv_ref, send_sems, recv_sems):
        x, y, c = _position()
        chips = [(1 - x, y), (x, 1 - y), (1 - x, 1 - y)]
        copies = [pltpu.make_async_remote_copy(
            src_ref=s_ref.at[2 * cx + cy], dst_ref=recv_ref.at[j], send_sem=send_sems.at[j],
            recv_sem=recv_sems.at[j], device_id=(cx, cy, c), device_id_type=MESH)
            for j, (cx, cy) in enumerate(chips)]
        for cp in copies:
            cp.start()
        for cp in copies:
            cp.wait()

    return pl.pallas_call(
        body, name=name,
        out_shape=jax.ShapeDtypeStruct((3, rows, n), chip_sums.dtype),
        in_specs=[pl.BlockSpec(memory_space=pltpu.HBM)], out_specs=pl.BlockSpec(memory_space=pltpu.HBM),
        scratch_shapes=[pltpu.SemaphoreType.DMA((3,)), pltpu.SemaphoreType.DMA((3,))],
    )(chip_sums)


def _add_pairs(own, recv, *, name):
    nchip, rows, n = own.shape
    tr = _tile(rows, 512, SUBLANES)

    def body(a_ref, b_ref, o_ref, wire_ref):
        s = a_ref[...] + b_ref[...]
        o_ref[...] = s
        wire_ref[...] = s.astype(WIRE_DTYPE)

    blk = pl.BlockSpec((None, tr, n), lambda k, i: (k, i, 0))
    return pl.pallas_call(
        body, name=name, grid=(nchip, rows // tr), in_specs=[blk, blk], out_specs=[blk, blk],
        out_shape=[jax.ShapeDtypeStruct((nchip, rows, n), own.dtype),
                   jax.ShapeDtypeStruct((nchip, rows, n), WIRE_DTYPE)],
        compiler_params=_params("parallel", "parallel"),
    )(own, recv)


def _adamw_math(w, g, m, v):
    m = ADAM_B1 * m + (1.0 - ADAM_B1) * g
    v = ADAM_B2 * v + (1.0 - ADAM_B2) * (g * g)
    m_hat = m / (1.0 - ADAM_B1 ** ADAM_STEP)
    v_hat = v / (1.0 - ADAM_B2 ** ADAM_STEP)
    delta = -ADAM_LR * (m_hat / (jnp.sqrt(v_hat) + ADAM_EPS) + ADAM_WD * w)
    return delta, m, v


def _adamw_sharded(chip_sums, recv, k_mine, w, m, v, *, name):
    rows, n = w.shape
    tr = _tile(rows, 256, SUBLANES)

    def body(k_ref, s_ref, r_ref, w_ref, m_ref, v_ref, g_out, d_out, m_out, v_out):
        g = ((s_ref[...] + r_ref[0].astype(F32)) + r_ref[1].astype(F32)) + r_ref[2].astype(F32)
        delta, m_new, v_new = _adamw_math(w_ref[...], g, m_ref[...], v_ref[...])
        g_out[...] = g
        d_out[...] = delta
        m_out[...] = m_new
        v_out[...] = v_new

    blk = pl.BlockSpec((tr, n), lambda i, k: (i, 0))
    out = jax.ShapeDtypeStruct((rows, n), F32)
    return pl.pallas_call(
        body, name=name,
        grid_spec=pltpu.PrefetchScalarGridSpec(
            num_scalar_prefetch=1, grid=(rows // tr,),
            in_specs=[pl.BlockSpec((None, tr, n), lambda i, k: (k[0], i, 0)),
                      pl.BlockSpec((3, tr, n), lambda i, k: (0, i, 0)), blk, blk, blk],
            out_specs=[blk, blk, blk, blk]),
        out_shape=[out, out, out, out],
        compiler_params=_params("parallel"),
    )(k_mine, chip_sums, recv, w, m, v)


SMALL_ROWS = 40
ROW_MIX_NORM, ROW_FFN_NORM, ROW_CONV_B, ROW_OUT_NORM, ROW_POOL_SCALE, ROW_CONV_W = 0, 4, 8, 12, 14, 16
ROW_SSD_VEC, ROW_QK_GAIN, ROW_LOSS = 32, 33, 34


def _adamw_small(gathered, w, m, v, *, name):
    _, rows, n = gathered.shape

    def body(a_ref, w_ref, m_ref, v_ref, g_out, d_out, m_out, v_out):
        g = a_ref[0]
        for d in range(1, N_DEV):
            g = g + a_ref[d]
        row = lax.broadcasted_iota(jnp.int32, (rows, 1), 0)
        g = jnp.where(row == ROW_QK_GAIN, g + pltpu.roll(g, SB_HEAD_DIM, 1), g)
        g = jnp.where(row == ROW_LOSS, jnp.sum(g, axis=1, keepdims=True), g)
        g_out[...] = g
        delta, m_new, v_new = _adamw_math(w_ref[...], g, m_ref[...], v_ref[...])
        d_out[...] = delta
        m_out[...] = m_new
        v_out[...] = v_new

    out = jax.ShapeDtypeStruct((rows, n), F32)
    return pl.pallas_call(body, name=name, out_shape=[out, out, out, out])(gathered, w, m, v)


BIG_WEIGHTS = ("pool_in", "pool_group", "ssd_in", "ssd_out", "sb_qkv", "sb_out", "ffn_gate", "ffn_up", "ffn_down")
COLUMN_SHARDED = ("ssd_in", "sb_qkv", "ffn_gate", "ffn_up")
ROW_PAD = 512
WIRE_DTYPE = jnp.bfloat16


def _to_rows(name, shard, d):
    if name in COLUMN_SHARDED:
        shard = jnp.swapaxes(shard, -1, -2)
    return shard.reshape(-1, d)


def _from_rows(name, rows, shard_shape):
    if name in COLUMN_SHARDED:
        lead, k, n = shard_shape
        return jnp.swapaxes(rows.reshape(lead, n, k), -1, -2)
    return rows.reshape(shard_shape)


def _pad_rows(a, total):
    return jnp.pad(a, ((0, total - a.shape[0]),) + ((0, 0),) * (a.ndim - 1))


def _exact_bf16_rows(v, d):
    words = lax.bitcast_convert_type(v.reshape(-1), WIRE_DTYPE).reshape(-1)
    return _pad_rows(words, -(-words.shape[0] // d) * d).reshape(-1, d)


def _exact_f32(rows, count):
    words = rows.reshape(rows.shape[0], -1)[:, :2 * count].reshape(rows.shape[0], count, 2)
    return lax.bitcast_convert_type(words, F32)


def _device_blocks(full, d):
    return full.reshape(N_DEV, -1, d)


def kernel(x, mix_norm, pool_in, pool_group, pool_scale, ssd_in, ssd_conv_w, ssd_conv_b, ssd_dt_bias, ssd_a_log, ssd_d, ssd_out_norm, ssd_out, sb_qkv, sb_q_norm, sb_k_norm, sb_out, ffn_norm, ffn_gate, ffn_up, ffn_down, loss_target, m_mix_norm, m_pool_in, m_pool_group, m_pool_scale, m_ssd_in, m_ssd_conv_w, m_ssd_conv_b, m_ssd_dt_bias, m_ssd_a_log, m_ssd_d, m_ssd_out_norm, m_ssd_out, m_sb_qkv, m_sb_q_norm, m_sb_k_norm, m_sb_out, m_ffn_norm, m_ffn_gate, m_ffn_up, m_ffn_down, v_mix_norm, v_pool_in, v_pool_group, v_pool_scale, v_ssd_in, v_ssd_conv_w, v_ssd_conv_b, v_ssd_dt_bias, v_ssd_a_log, v_ssd_d, v_ssd_out_norm, v_ssd_out, v_sb_qkv, v_sb_q_norm, v_sb_k_norm, v_sb_out, v_ffn_norm, v_ffn_gate, v_ffn_up, v_ffn_down):
    weights = dict(mix_norm=mix_norm, pool_in=pool_in, pool_group=pool_group, pool_scale=pool_scale, ssd_in=ssd_in,
                   ssd_conv_w=ssd_conv_w, ssd_conv_b=ssd_conv_b, ssd_dt_bias=ssd_dt_bias, ssd_a_log=ssd_a_log,
                   ssd_d=ssd_d, ssd_out_norm=ssd_out_norm, ssd_out=ssd_out, sb_qkv=sb_qkv, sb_q_norm=sb_q_norm,
                   sb_k_norm=sb_k_norm, sb_out=sb_out, ffn_norm=ffn_norm, ffn_gate=ffn_gate, ffn_up=ffn_up,
                   ffn_down=ffn_down)
    mom1 = dict(mix_norm=m_mix_norm, pool_in=m_pool_in, pool_group=m_pool_group, pool_scale=m_pool_scale,
                ssd_in=m_ssd_in, ssd_conv_w=m_ssd_conv_w, ssd_conv_b=m_ssd_conv_b, ssd_dt_bias=m_ssd_dt_bias,
                ssd_a_log=m_ssd_a_log, ssd_d=m_ssd_d, ssd_out_norm=m_ssd_out_norm, ssd_out=m_ssd_out,
                sb_qkv=m_sb_qkv, sb_q_norm=m_sb_q_norm, sb_k_norm=m_sb_k_norm, sb_out=m_sb_out,
                ffn_norm=m_ffn_norm, ffn_gate=m_ffn_gate, ffn_up=m_ffn_up, ffn_down=m_ffn_down)
    mom2 = dict(mix_norm=v_mix_norm, pool_in=v_pool_in, pool_group=v_pool_group, pool_scale=v_pool_scale,
                ssd_in=v_ssd_in, ssd_conv_w=v_ssd_conv_w, ssd_conv_b=v_ssd_conv_b, ssd_dt_bias=v_ssd_dt_bias,
                ssd_a_log=v_ssd_a_log, ssd_d=v_ssd_d, ssd_out_norm=v_ssd_out_norm, ssd_out=v_ssd_out,
                sb_qkv=v_sb_qkv, sb_q_norm=v_sb_q_norm, sb_k_norm=v_sb_k_norm, sb_out=v_sb_out,
                ffn_norm=v_ffn_norm, ffn_gate=v_ffn_gate, ffn_up=v_ffn_up, ffn_down=v_ffn_down)
    names = list(weights)
    depth, d = mix_norm.shape
    xs, ys, cs = _position()
    dev = 4 * xs + 2 * ys + cs
    chip = 2 * xs + ys

    seg = {}
    row = 0
    for name in BIG_WEIGHTS:
        n_rows = weights[name].size // d
        seg[name] = (row, n_rows)
        row += -(-n_rows // SUBLANES) * SUBLANES
    big_rows = row
    n_scale, n_convw = pool_scale.size, ssd_conv_w.size
    exact = jnp.concatenate([_exact_bf16_rows(pool_scale, d), _exact_bf16_rows(ssd_conv_w, d)], axis=0)
    scale_rows = _exact_bf16_rows(pool_scale, d).shape[0]
    packed_rows = -(-(big_rows + exact.shape[0]) // ROW_PAD) * ROW_PAD

    def pack(tree, dtype):
        ends = [seg[n][0] for n in BIG_WEIGHTS[1:]] + [big_rows]
        return jnp.concatenate([_pad_rows(_to_rows(n, tree[n], d).astype(dtype), end - seg[n][0])
                                for n, end in zip(BIG_WEIGHTS, ends)], axis=0)

    w_wire = _pad_rows(jnp.concatenate([pack(weights, WIRE_DTYPE), exact], axis=0), packed_rows)
    gathered = _all_gather(w_wire, name="gather_weights", in_vmem=False)

    def seg_of(name):
        a, n = seg[name]
        return gathered[:, a:a + n]

    n_pool, n_ssd, n_sb = pool_in.shape[0], ssd_in.shape[0], sb_qkv.shape[0]
    assert n_ssd == 1 and n_sb == 1
    w_pool_in = seg_of("pool_in").reshape(N_DEV, n_pool, -1, d).transpose(1, 0, 2, 3).reshape(n_pool, d, d)
    grp = pool_group.shape
    w_pool_group = seg_of("pool_group").reshape(N_DEV, grp[0], grp[1], grp[2], grp[3]).transpose(1, 2, 0, 3, 4)
    w_pool_group = w_pool_group.reshape(grp[0], grp[1], grp[3], grp[3])
    w_ssd_in_t = seg_of("ssd_in").reshape(-1, d)
    w_ssd_out = seg_of("ssd_out").reshape(-1, d)
    w_sb_qkv_t = seg_of("sb_qkv").reshape(-1, d)
    w_sb_out = seg_of("sb_out").reshape(-1, d)
    hidden = ffn_down.shape[1] * N_DEV
    w_gate_t = seg_of("ffn_gate").reshape(N_DEV, depth, -1, d).transpose(1, 0, 2, 3).reshape(depth, hidden, d)
    w_up_t = seg_of("ffn_up").reshape(N_DEV, depth, -1, d).transpose(1, 0, 2, 3).reshape(depth, hidden, d)
    w_down = seg_of("ffn_down").reshape(N_DEV, depth, -1, d).transpose(1, 0, 2, 3).reshape(depth, hidden, d)
    exact_all = gathered[:, big_rows:big_rows + exact.shape[0]]
    scale_full = _exact_f32(exact_all[:, :scale_rows], n_scale).reshape(N_DEV, n_pool, -1)
    scale_full = scale_full.transpose(1, 0, 2).reshape(n_pool, d)
    convw_full = _exact_f32(exact_all[:, scale_rows:], n_convw).reshape(N_DEV, SSD_CONV, -1)
    convw_full = convw_full.transpose(1, 0, 2).reshape(SSD_CONV, -1)

    d_inner = w_ssd_out.shape[0]
    n_zx = w_ssd_in_t.shape[0] - ssd_dt_bias.shape[1]
    w_zx_t = w_ssd_in_t[:n_zx]
    n_ssd_heads = ssd_dt_bias.shape[1]
    n_ssd_groups = n_ssd_heads // SSD_HEADS_PER_GROUP
    w_dt_t = _ssd_group_pad(w_ssd_in_t[n_zx:].T, n_ssd_groups).T
    par = _pad_rows(_ssd_group_pad(jnp.concatenate([ssd_dt_bias, ssd_a_log, ssd_d], axis=0), n_ssd_groups), SUBLANES)
    d_rep = jnp.repeat(ssd_d[0], SSD_HEAD_DIM)[None]
    qk_gains = jnp.zeros((SUBLANES, LANES), F32).at[0].set(jnp.tile(sb_q_norm[0], 2)).at[1].set(jnp.tile(sb_k_norm[0], 2))

    act = x[0]
    saved = []
    for i in range(depth):
        kind, j = i % 3, i // 3
        gain = mix_norm[i:i + 1]
        if kind == 0:
            act, s = _pool_mixer_fwd(act, gain, w_pool_in[j], w_pool_group[j], scale_full[j:j + 1], f"l{i}")
        elif kind == 1:
            act, s = _ssd_mixer_fwd(act, gain, w_zx_t, w_dt_t, convw_full, ssd_conv_b, par, d_rep, ssd_out_norm,
                                    w_ssd_out, f"l{i}")
        else:
            act, s = _sb_mixer_fwd(act, gain, w_sb_qkv_t, qk_gains, w_sb_out, f"l{i}")
        act, f = _ffn_fwd(act, ffn_norm[i:i + 1], w_gate_t[i], w_up_t[i], w_down[i], f"l{i}")
        saved.append((s, f))
    dact, loss_cols = _loss_head(act, loss_target[0], name="loss_head")

    g_mix_norm, g_ffn_norm = [None] * depth, [None] * depth
    g_pool_in, g_pool_group, g_pool_scale = [None] * n_pool, [None] * n_pool, [None] * n_pool
    g_gate_t, g_up_t, g_down = [None] * depth, [None] * depth, [None] * depth
    for i in reversed(range(depth)):
        kind, j = i % 3, i // 3
        gain = mix_norm[i:i + 1]
        s, f = saved[i]
        dact, g_gate_t[i], g_up_t[i], g_down[i], g_ffn_norm[i] = _ffn_bwd(
            dact, f, ffn_norm[i:i + 1], w_gate_t[i], w_up_t[i], w_down[i], f"l{i}")
        if kind == 0:
            dact, g_pool_in[j], g_pool_group[j], g_pool_scale[j], g_mix_norm[i] = _pool_mixer_bwd(
                dact, s, gain, w_pool_in[j], w_pool_group[j], scale_full[j:j + 1], f"l{i}")
        elif kind == 1:
            (dact, g_zx_t, g_dt_t, g_conv_w, g_conv_b, g_par, g_out_norm, g_ssd_out,
             g_mix_norm[i]) = _ssd_mixer_bwd(dact, s, gain, w_zx_t, w_dt_t, convw_full, ssd_conv_b, par, d_rep,
                                             ssd_out_norm, w_ssd_out, f"l{i}")
        else:
            dact, g_qkv_t, g_qk_gains, g_sb_out, g_mix_norm[i] = _sb_mixer_bwd(
                dact, s, gain, w_sb_qkv_t, qk_gains, w_sb_out, f"l{i}")
    grad_x = dact[None]

    full_grads = dict(
        pool_in=g_pool_in, pool_group=[gg[k] for gg in g_pool_group for k in range(gg.shape[0])],
        ssd_in=[g_zx_t, _ssd_group_unpad(g_dt_t.T, n_ssd_groups).T], ssd_out=[g_ssd_out],
        sb_qkv=[g_qkv_t], sb_out=[g_sb_out], ffn_gate=g_gate_t, ffn_up=g_up_t, ffn_down=g_down)
    full_grads["ssd_in"] = [jnp.concatenate(full_grads["ssd_in"], axis=0)]
    pieces, offsets = [], []
    for n in BIG_WEIGHTS:
        off = seg[n][0]
        for g in full_grads[n]:
            blocks = _device_blocks(g, d)
            blocks = jnp.pad(blocks, ((0, 0), (0, -blocks.shape[1] % SUBLANES), (0, 0)))
            pieces.append(blocks.reshape(N_DEV // 2, 2, blocks.shape[1], d))
            offsets.append(off)
            off += blocks.shape[1]
    pieces.append(jnp.zeros((N_DEV // 2, 2, packed_rows - big_rows, d), F32))
    offsets.append(big_rows)
    own, from_sibling = _exchange_sibling(pieces, offsets, packed_rows, name="reduce_sibling")
    chip_sums, chip_sums_wire = _add_pairs(own, from_sibling, name="reduce_sibling_add")
    from_chips = _exchange_chips(chip_sums_wire, name="reduce_chips")

    def pack_f32(tree):
        return _pad_rows(pack(tree, F32), packed_rows)

    big_out = _adamw_sharded(chip_sums, from_chips, chip.reshape(1).astype(jnp.int32), pack_f32(weights),
                             pack_f32(mom1), pack_f32(mom2), name="adamw_sharded")

    def small_pack(mix, ffn, conv_b, out_norm, scale, conv_w, vec, qk, loss=None):
        buf = jnp.zeros((SMALL_ROWS, d), F32)
        buf = buf.at[ROW_MIX_NORM:ROW_MIX_NORM + depth].set(mix).at[ROW_FFN_NORM:ROW_FFN_NORM + depth].set(ffn)
        buf = buf.at[ROW_CONV_B:ROW_CONV_B + conv_b.size // d].set(conv_b.reshape(-1, d))
        buf = buf.at[ROW_OUT_NORM:ROW_OUT_NORM + out_norm.size // d].set(out_norm.reshape(-1, d))
        buf = buf.at[ROW_POOL_SCALE:ROW_POOL_SCALE + n_pool].set(scale)
        buf = buf.at[ROW_CONV_W:ROW_CONV_W + conv_w.size // d].set(conv_w.reshape(-1, d))
        buf = buf.at[ROW_SSD_VEC].set(vec.reshape(-1)).at[ROW_QK_GAIN].set(qk.reshape(-1))
        if loss is not None:
            buf = buf.at[ROW_LOSS].set(loss.reshape(-1))
        return buf

    def small_params(tree):
        scale = lax.dynamic_update_slice(jnp.zeros((n_pool, d), F32), tree["pool_scale"],
                                         (0, dev * tree["pool_scale"].shape[1]))
        conv_w = lax.dynamic_update_slice(jnp.zeros(convw_full.shape, F32), tree["ssd_conv_w"][0],
                                          (0, dev * tree["ssd_conv_w"].shape[2]))
        vec = jnp.zeros((SUBLANES, LANES), F32)
        vec = vec.at[0, :n_ssd_heads].set(tree["ssd_dt_bias"][0]).at[1, :n_ssd_heads].set(tree["ssd_a_log"][0])
        vec = vec.at[2, :n_ssd_heads].set(tree["ssd_d"][0])
        qk = jnp.zeros((SUBLANES, LANES), F32)
        qk = qk.at[0, SB_HEAD_DIM:].set(tree["sb_q_norm"][0]).at[1, SB_HEAD_DIM:].set(tree["sb_k_norm"][0])
        return small_pack(tree["mix_norm"], tree["ffn_norm"], tree["ssd_conv_b"], tree["ssd_out_norm"], scale,
                          conv_w, vec, qk)

    small_partial = small_pack(jnp.concatenate(g_mix_norm, axis=0), jnp.concatenate(g_ffn_norm, axis=0), g_conv_b,
                               g_out_norm, jnp.concatenate(g_pool_scale, axis=0), g_conv_w,
                               jnp.zeros((SUBLANES, LANES), F32).at[:3, :n_ssd_heads].set(
                                   _ssd_group_unpad(g_par[:3], n_ssd_groups)), g_qk_gains,
                               loss_cols)
    small_all = _all_gather(small_partial, name="gather_small", in_vmem=True)
    small_out = _adamw_small(small_all, small_params(weights), small_params(mom1), small_params(mom2),
                             name="adamw_small")
    loss = small_out[0][ROW_LOSS, 0]

    def unpack(big, small):
        out = {}
        for name in BIG_WEIGHTS:
            a, n = seg[name]
            out[name] = _from_rows(name, big[a:a + n], weights[name].shape)
        out["mix_norm"] = small[ROW_MIX_NORM:ROW_MIX_NORM + depth]
        out["ffn_norm"] = small[ROW_FFN_NORM:ROW_FFN_NORM + depth]
        out["ssd_conv_b"] = small[ROW_CONV_B:ROW_CONV_B + ssd_conv_b.size // d].reshape(ssd_conv_b.shape)
        out["ssd_out_norm"] = small[ROW_OUT_NORM:ROW_OUT_NORM + ssd_out_norm.size // d].reshape(ssd_out_norm.shape)
        out["pool_scale"] = lax.dynamic_slice(small[ROW_POOL_SCALE:ROW_POOL_SCALE + n_pool],
                                              (0, dev * pool_scale.shape[1]), pool_scale.shape)
        conv_w = small[ROW_CONV_W:ROW_CONV_W + convw_full.size // d].reshape(convw_full.shape)
        out["ssd_conv_w"] = lax.dynamic_slice(conv_w, (0, dev * ssd_conv_w.shape[2]), ssd_conv_w.shape[1:])[None]
        vec = small[ROW_SSD_VEC].reshape(SUBLANES, LANES)
        out["ssd_dt_bias"], out["ssd_a_log"], out["ssd_d"] = (vec[r:r + 1, :n_ssd_heads] for r in range(3))
        qk = small[ROW_QK_GAIN].reshape(SUBLANES, LANES)
        out["sb_q_norm"], out["sb_k_norm"] = qk[0:1, SB_HEAD_DIM:], qk[1:2, SB_HEAD_DIM:]
        return [out[n] for n in names]

    results = [unpack(b, s) for b, s in zip(big_out, small_out)]
    return (loss, grad_x, *results[0], *results[1], *results[2], *results[3])
```

```python
import math

import jax
import jax.numpy as jnp
from jax import lax
from jax.experimental import pallas as pl
from jax.experimental.pallas import tpu as pltpu

F32 = jnp.float32
BF16 = jnp.bfloat16

N_DEV = 8
NORM_EPS = 1e-6
V7X_VMEM_LIMIT_BYTES = 48 * 1024 * 1024
LANES = 128
SUBLANES = 8

POOL_WINDOWS = (2, 4, 8, 16)
SSD_CHUNK = 256
SSD_HEAD_DIM = 64
SSD_STATE = 128
SSD_HEADS_PER_GROUP = 4
SSD_CONV = 4
SB_HEAD_DIM = 64
SB_BLOCK = 128
SB_QUERY_BLOCK = 256

ADAM_LR = 0.001
ADAM_B1 = 0.9
ADAM_B2 = 0.999
ADAM_EPS = 1e-08
ADAM_WD = 0.01
ADAM_STEP = 10


def _params(*sem):
    return pltpu.CompilerParams(dimension_semantics=sem, vmem_limit_bytes=V7X_VMEM_LIMIT_BYTES)


def _tile(n, cap, mult):
    best = None
    for t in range(mult, min(n, cap) + 1, mult):
        if n % t == 0:
            best = t
    return best or n


def _load_slabs(ref, slabs):
    if not slabs:
        return ref[...]
    return jnp.concatenate([ref[p] for p in range(ref.shape[0])], axis=1)


def _matmul(a, b, mode, *, name, out_dtype=F32, resid=None, a_slabs=False, out_slabs=False,
            tm_cap=1024, tn_cap=1024, tk_cap=2048):
    pairs = list(zip(a, b)) if isinstance(a, (list, tuple)) else [(a, b)]
    a, b = pairs[0]
    if a_slabs:
        m, k = a.shape[1], a.shape[0] * LANES
    else:
        m, k = a.shape
    n = b.shape[1] if mode == "nn" else b.shape[0]
    assert (b.shape[0] if mode == "nn" else b.shape[1]) == k
    assert all(pa.shape == a.shape and pb.shape == b.shape for pa, pb in pairs)
    tm, tn, tk = _tile(m, tm_cap, SUBLANES), _tile(n, tn_cap, LANES), _tile(k, tk_cap, LANES)
    nk = k // tk
    dn = (((1,), (0,)), ((), ())) if mode == "nn" else (((1,), (1,)), ((), ()))
    has_resid = resid is not None
    n_pairs = len(pairs)

    def body(*refs):
        ab_refs, rest = refs[:2 * n_pairs], refs[2 * n_pairs:]
        r_ref = rest[0] if has_resid else None
        o_ref = rest[1] if has_resid else rest[0]
        kk = pl.program_id(2)

        def partial():
            total = None
            for p in range(n_pairs):
                d = lax.dot_general(_load_slabs(ab_refs[2 * p], a_slabs).astype(BF16),
                                    ab_refs[2 * p + 1][...].astype(BF16), dn, preferred_element_type=F32)
                total = d if total is None else total + d
            return total

        def finish(r):
            if has_resid:
                r = r + r_ref[...]
            if out_slabs:
                for p in range(tn // LANES):
                    o_ref[p] = r[:, p * LANES:(p + 1) * LANES].astype(out_dtype)
            else:
                o_ref[...] = r.astype(out_dtype)

        if nk == 1:
            finish(partial())
        else:
            acc = rest[-1]

            @pl.when(kk == 0)
            def _():
                acc[...] = jnp.zeros_like(acc)

            acc[...] += partial()

            @pl.when(kk == nk - 1)
            def _():
                finish(acc[...])

    b_spec = (pl.BlockSpec((tk, tn), lambda i, j, kk: (kk, j)) if mode == "nn"
              else pl.BlockSpec((tn, tk), lambda i, j, kk: (j, kk)))
    a_spec = (pl.BlockSpec((tk // LANES, tm, LANES), lambda i, j, kk: (kk, i, 0)) if a_slabs
              else pl.BlockSpec((tm, tk), lambda i, j, kk: (i, kk)))
    in_specs = [a_spec, b_spec] * n_pairs
    args = [t for pair in pairs for t in pair]
    if has_resid:
        in_specs.append(pl.BlockSpec((tm, tn), lambda i, j, kk: (i, j)))
        args.append(resid)
    if out_slabs:
        out_spec = pl.BlockSpec((tn // LANES, tm, LANES), lambda i, j, kk: (j, i, 0))
        out_shape = jax.ShapeDtypeStruct((n // LANES, m, LANES), out_dtype)
    else:
        out_spec = pl.BlockSpec((tm, tn), lambda i, j, kk: (i, j))
        out_shape = jax.ShapeDtypeStruct((m, n), out_dtype)
    return pl.pallas_call(
        body, name=name, grid=(m // tm, n // tn, nk),
        in_specs=in_specs, out_specs=out_spec, out_shape=out_shape,
        scratch_shapes=[pltpu.VMEM((tm, tn), F32)] if nk > 1 else [],
        compiler_params=_params("parallel", "parallel", "arbitrary"),
    )(*args)


def _matmul_tn(a, b, *, name, a_slabs=False, ta_cap=1024, tb_cap=1024, tr_cap=512):
    if a_slabs:
        r, ka = a.shape[1], a.shape[0] * LANES
    else:
        r, ka = a.shape
    nb = b.shape[1]
    assert b.shape[0] == r
    ta, tb, tr = _tile(ka, ta_cap, LANES), _tile(nb, tb_cap, LANES), _tile(r, tr_cap, SUBLANES)

    def body(a_ref, b_ref, o_ref):
        @pl.when(pl.program_id(2) == 0)
        def _():
            o_ref[...] = jnp.zeros_like(o_ref)

        o_ref[...] += lax.dot_general(_load_slabs(a_ref, a_slabs).astype(BF16), b_ref[...].astype(BF16),
                                      (((0,), (0,)), ((), ())), preferred_element_type=F32)

    a_spec = (pl.BlockSpec((ta // LANES, tr, LANES), lambda i, j, kk: (i, kk, 0)) if a_slabs
              else pl.BlockSpec((tr, ta), lambda i, j, kk: (kk, i)))
    return pl.pallas_call(
        body, name=name, grid=(ka // ta, nb // tb, r // tr),
        in_specs=[a_spec, pl.BlockSpec((tr, tb), lambda i, j, kk: (kk, j))],
        out_specs=pl.BlockSpec((ta, tb), lambda i, j, kk: (i, j)),
        out_shape=jax.ShapeDtypeStruct((ka, nb), F32),
        compiler_params=_params("parallel", "parallel", "arbitrary"),
    )(a, b)


def _matmul_tn_into(buf, a, b, row_off, *, name, a_slabs=False, tr_cap=512):
    if a_slabs:
        r, ka = a.shape[1], a.shape[0] * LANES
    else:
        r, ka = a.shape
    n_dev, _, n = buf.shape
    per = ka // n_dev
    assert b.shape == (r, n) and ka % n_dev == 0 and per % SUBLANES == 0 and row_off % per == 0
    tr = _tile(r, tr_cap, SUBLANES)

    def body(buf_ref, a_ref, b_ref, o_ref):
        prod = lax.dot_general(_load_slabs(a_ref, a_slabs).astype(BF16), b_ref[...].astype(BF16),
                               (((0,), (0,)), ((), ())), preferred_element_type=F32)
        @pl.when(pl.program_id(0) == 0)
        def _():
            for k in range(n_dev):
                o_ref[k] = prod[k * per:(k + 1) * per]

        @pl.when(pl.program_id(0) > 0)
        def _():
            for k in range(n_dev):
                o_ref[k] += prod[k * per:(k + 1) * per]

    a_spec = (pl.BlockSpec((ka // LANES, tr, LANES), lambda i: (0, i, 0)) if a_slabs
              else pl.BlockSpec((tr, ka), lambda i: (i, 0)))
    return pl.pallas_call(
        body, name=name, grid=(r // tr,),
        in_specs=[pl.BlockSpec(memory_space=pl.ANY), a_spec, pl.BlockSpec((tr, n), lambda i: (i, 0))],
        out_specs=pl.BlockSpec((n_dev, per, n), lambda i: (0, row_off // per, 0)),
        out_shape=jax.ShapeDtypeStruct(buf.shape, F32),
        input_output_aliases={0: 0},
        compiler_params=_params("arbitrary"),
    )(buf, a, b)


def _rms_fwd(x, gain, *, name):
    t, d = x.shape
    tm = _tile(t, 512, SUBLANES)

    def body(x_ref, g_ref, o_ref):
        xv = x_ref[...]
        r = lax.rsqrt(jnp.mean(xv * xv, axis=-1, keepdims=True) + NORM_EPS)
        o_ref[...] = (xv * r * g_ref[...]).astype(BF16)

    return pl.pallas_call(
        body, name=name, grid=(t // tm,),
        in_specs=[pl.BlockSpec((tm, d), lambda i: (i, 0)), pl.BlockSpec((1, d), lambda i: (0, 0))],
        out_specs=pl.BlockSpec((tm, d), lambda i: (i, 0)),
        out_shape=jax.ShapeDtypeStruct((t, d), BF16),
        compiler_params=_params("parallel"),
    )(x, gain)


def _rms_bwd(x, gain, dh, dres, *, name):
    t, d = x.shape
    tm = _tile(t, 512, SUBLANES)

    def body(x_ref, g_ref, dh_ref, dres_ref, dx_ref, dg_ref):
        @pl.when(pl.program_id(0) == 0)
        def _():
            dg_ref[...] = jnp.zeros_like(dg_ref)

        xv = x_ref[...]
        r = lax.rsqrt(jnp.mean(xv * xv, axis=-1, keepdims=True) + NORM_EPS)
        xhat = xv * r
        dhv = dh_ref[...]
        u = dhv * g_ref[...]
        dx_ref[...] = dres_ref[...] + r * (u - xhat * jnp.mean(u * xhat, axis=-1, keepdims=True))
        dg_ref[...] += jnp.sum(dhv * xhat, axis=0, keepdims=True)

    return pl.pallas_call(
        body, name=name, grid=(t // tm,),
        in_specs=[pl.BlockSpec((tm, d), lambda i: (i, 0)), pl.BlockSpec((1, d), lambda i: (0, 0)),
                  pl.BlockSpec((tm, d), lambda i: (i, 0)), pl.BlockSpec((tm, d), lambda i: (i, 0))],
        out_specs=[pl.BlockSpec((tm, d), lambda i: (i, 0)), pl.BlockSpec((1, d), lambda i: (0, 0))],
        out_shape=[jax.ShapeDtypeStruct((t, d), F32), jax.ShapeDtypeStruct((1, d), F32)],
        compiler_params=_params("arbitrary"),
    )(x, gain, dh, dres)


def _loss_head(y, target, *, name):
    t, d = y.shape
    tm = _tile(t, 512, SUBLANES)

    def body(y_ref, t_ref, dy_ref, l_ref):
        @pl.when(pl.program_id(0) == 0)
        def _():
            l_ref[...] = jnp.zeros_like(l_ref)

        e = y_ref[...] - t_ref[...]
        dy_ref[...] = e * (1.0 / d)
        l_ref[...] += jnp.sum(e * e, axis=0, keepdims=True) * (0.5 / d)

    return pl.pallas_call(
        body, name=name, grid=(t // tm,),
        in_specs=[pl.BlockSpec((tm, d), lambda i: (i, 0)), pl.BlockSpec((tm, d), lambda i: (i, 0))],
        out_specs=[pl.BlockSpec((tm, d), lambda i: (i, 0)), pl.BlockSpec((1, d), lambda i: (0, 0))],
        out_shape=[jax.ShapeDtypeStruct((t, d), F32), jax.ShapeDtypeStruct((1, d), F32)],
        compiler_params=_params("arbitrary"),
    )(y, target)


def _sigmoid(v):
    return 0.5 * jnp.tanh(0.5 * v) + 0.5


FFN_TOKEN_TILE = 512
FFN_HIDDEN_TILE = 1408
NT_DIMS = (((1,), (1,)), ((), ()))


def _ffn_up(h, w_gate_t, w_up_t, *, name):
    t, d = h.shape
    f = w_gate_t.shape[0]
    tm, tn = _tile(t, FFN_TOKEN_TILE, SUBLANES), _tile(f, FFN_HIDDEN_TILE, LANES)

    def body(h_ref, g_ref, u_ref, s_ref, a_ref, b_ref):
        hv = h_ref[...].astype(BF16)
        av = lax.dot_general(hv, g_ref[...].astype(BF16), NT_DIMS, preferred_element_type=F32)
        bv = lax.dot_general(hv, u_ref[...].astype(BF16), NT_DIMS, preferred_element_type=F32)
        s_ref[...] = (av * _sigmoid(av) * bv).astype(BF16)
        a_ref[...] = av.astype(BF16)
        b_ref[...] = bv.astype(BF16)

    w_spec = pl.BlockSpec((tn, d), lambda j, i: (j, 0))
    out_spec = pl.BlockSpec((tm, tn), lambda j, i: (i, j))
    out = jax.ShapeDtypeStruct((t, f), BF16)
    return pl.pallas_call(
        body, name=name, grid=(f // tn, t // tm),
        in_specs=[pl.BlockSpec((tm, d), lambda j, i: (i, 0)), w_spec, w_spec],
        out_specs=[out_spec, out_spec, out_spec], out_shape=[out, out, out],
        compiler_params=_params("parallel", "parallel"),
    )(h, w_gate_t, w_up_t)


def _ffn_dact(dx, w_down, a, b, *, name):
    t, d = dx.shape
    f = w_down.shape[0]
    tm, tn = _tile(t, FFN_TOKEN_TILE, SUBLANES), _tile(f, FFN_HIDDEN_TILE, LANES)

    def body(dx_ref, w_ref, a_ref, b_ref, da_ref, db_ref):
        ds = lax.dot_general(dx_ref[...].astype(BF16), w_ref[...].astype(BF16), NT_DIMS, preferred_element_type=F32)
        av = a_ref[...].astype(F32)
        sg = _sigmoid(av)
        da_ref[...] = (ds * b_ref[...].astype(F32) * (sg * (1.0 + av * (1.0 - sg)))).astype(BF16)
        db_ref[...] = (ds * av * sg).astype(BF16)

    blk = pl.BlockSpec((tm, tn), lambda j, i: (i, j))
    out = jax.ShapeDtypeStruct((t, f), BF16)
    return pl.pallas_call(
        body, name=name, grid=(f // tn, t // tm),
        in_specs=[pl.BlockSpec((tm, d), lambda j, i: (i, 0)), pl.BlockSpec((tn, d), lambda j, i: (j, 0)), blk, blk],
        out_specs=[blk, blk], out_shape=[out, out],
        compiler_params=_params("parallel", "parallel"),
    )(dx, w_down, a, b)


def _ffn_fwd(x, gain, w_gate_t, w_up_t, w_down, tag):
    h = _rms_fwd(x, gain, name=f"ffn_norm_{tag}")
    s, a, b = _ffn_up(h, w_gate_t, w_up_t, name=f"ffn_up_{tag}")
    x_new = _matmul(s, w_down, "nn", resid=x, tn_cap=1024, tk_cap=2816, name=f"ffn_down_{tag}")
    return x_new, (x, h, a, b, s)


def _ffn_bwd(dx, saved, gain, w_gate_t, w_up_t, w_down, grads, rows, tag):
    x, h, a, b, s = saved
    da, db = _ffn_dact(dx, w_down, a, b, name=f"ffn_dact_{tag}")
    grads = _matmul_tn_into(grads, da, h, rows[0], name=f"ffn_dwgate_{tag}")
    grads = _matmul_tn_into(grads, db, h, rows[1], name=f"ffn_dwup_{tag}")
    grads = _matmul_tn_into(grads, s, dx, rows[2], name=f"ffn_dwdown_{tag}")
    dh = _matmul([da, db], [w_gate_t, w_up_t], "nn", tm_cap=512, tn_cap=1024, tk_cap=2816, name=f"ffn_dh_{tag}")
    dx_in, dgain = _rms_bwd(x, gain, dh, dx, name=f"ffn_dnorm_{tag}")
    return dx_in, grads, dgain


POOL_HALO = 16


def _shift_rows(v, k):
    n = v.shape[0]
    return pltpu.roll(v, k % n, 0)


def _window_sum(v, w, direction):
    k = 1
    while k < w:
        v = v + _shift_rows(v, direction * k)
        k *= 2
    return v


def _pool_fwd(u, x, w_group, scale, *, name):
    t, d = u.shape
    ng, dg = w_group.shape[0], w_group.shape[1]
    tm = _tile(t, 512, POOL_HALO)
    hb = tm // POOL_HALO

    def body(u_ref, halo_ref, x_ref, w_ref, s_ref, xo_ref, p_ref, y_ref):
        i, g = pl.program_id(0), pl.program_id(1)
        halo = jnp.where(i > 0, halo_ref[...], 0.0)
        ext = jnp.concatenate([halo, u_ref[...]], axis=0)
        pos = i * tm + lax.broadcasted_iota(jnp.int32, (tm, 1), 0)
        for gi, win in enumerate(POOL_WINDOWS):
            @pl.when(g == gi)
            def _(win=win):
                tot = _window_sum(ext, win, 1)[POOL_HALO:]
                cnt = jnp.minimum(pos + 1, win).astype(F32)
                p = (tot / cnt - u_ref[...]).astype(BF16)
                p_ref[...] = p
                y = jnp.dot(p, w_ref[...].astype(BF16), preferred_element_type=F32)
                y_ref[...] = y
                xo_ref[...] = x_ref[...] + y * s_ref[...]

    blk = pl.BlockSpec((tm, dg), lambda i, g: (i, g))
    return pl.pallas_call(
        body, name=name, grid=(t // tm, ng),
        in_specs=[blk, pl.BlockSpec((POOL_HALO, dg), lambda i, g: (jnp.maximum(i * hb - 1, 0), g)), blk,
                  pl.BlockSpec((None, dg, dg), lambda i, g: (g, 0, 0)), pl.BlockSpec((1, dg), lambda i, g: (0, g))],
        out_specs=[blk, blk, blk],
        out_shape=[jax.ShapeDtypeStruct((t, d), F32), jax.ShapeDtypeStruct((t, d), BF16),
                   jax.ShapeDtypeStruct((t, d), F32)],
        compiler_params=_params("parallel", "parallel"),
    )(u, u, x, w_group, scale)


def _pool_bwd(dx, p, y_pre, w_group, scale, *, name):
    t, d = dx.shape
    ng, dg = w_group.shape[0], w_group.shape[1]
    tm = _tile(t, 512, POOL_HALO)
    hb = tm // POOL_HALO
    nt = t // tm

    def body(dx_ref, nxt_ref, p_ref, y_ref, w_ref, s_ref, du_ref, dw_ref, ds_ref):
        g, i = pl.program_id(0), pl.program_id(1)

        @pl.when(i == 0)
        def _():
            dw_ref[...] = jnp.zeros_like(dw_ref)
            ds_ref[...] = jnp.zeros_like(ds_ref)

        dxv = dx_ref[...]
        ds_ref[...] += jnp.sum(dxv * y_ref[...], axis=0, keepdims=True)
        nxt = jnp.where(i < nt - 1, nxt_ref[...], 0.0)
        dyp = (jnp.concatenate([dxv, nxt], axis=0) * s_ref[...]).astype(BF16)
        dw_ref[...] += lax.dot_general(p_ref[...], dyp[:tm], (((0,), (0,)), ((), ())), preferred_element_type=F32)
        dp = lax.dot_general(dyp, w_ref[...].astype(BF16), (((1,), (1,)), ((), ())), preferred_element_type=F32)
        pos = i * tm + lax.broadcasted_iota(jnp.int32, (tm + POOL_HALO, 1), 0)
        for gi, win in enumerate(POOL_WINDOWS):
            @pl.when(g == gi)
            def _(win=win):
                q = dp / jnp.minimum(pos + 1, win).astype(F32)
                du_ref[...] = (_window_sum(q, win, -1)[:tm] - dp[:tm]).astype(BF16)

    blk = pl.BlockSpec((tm, dg), lambda g, i: (i, g))
    return pl.pallas_call(
        body, name=name, grid=(ng, nt),
        in_specs=[blk, pl.BlockSpec((POOL_HALO, dg), lambda g, i: (jnp.minimum((i + 1) * hb, t // POOL_HALO - 1), g)),
                  blk, blk, pl.BlockSpec((None, dg, dg), lambda g, i: (g, 0, 0)),
                  pl.BlockSpec((1, dg), lambda g, i: (0, g))],
        out_specs=[blk, pl.BlockSpec((None, dg, dg), lambda g, i: (g, 0, 0)), pl.BlockSpec((1, dg), lambda g, i: (0, g))],
        out_shape=[jax.ShapeDtypeStruct((t, d), BF16), jax.ShapeDtypeStruct((ng, dg, dg), F32),
                   jax.ShapeDtypeStruct((1, d), F32)],
        compiler_params=_params("parallel", "arbitrary"),
    )(dx, dx, p, y_pre, w_group, scale)


def _pool_mixer_fwd(x, gain, w_in, w_group, scale, tag):
    h = _rms_fwd(x, gain, name=f"pool_norm_{tag}")
    u = _matmul(h, w_in, "nn", name=f"pool_in_{tag}")
    x_new, p, y_pre = _pool_fwd(u, x, w_group, scale, name=f"pool_mix_{tag}")
    return x_new, (x, h, p, y_pre)


def _pool_mixer_bwd(dx, saved, gain, w_in, w_group, scale, grads, row_in, tag):
    x, h, p, y_pre = saved
    du, dw_group, dscale = _pool_bwd(dx, p, y_pre, w_group, scale, name=f"pool_dmix_{tag}")
    grads = _matmul_tn_into(grads, h, du, row_in, name=f"pool_dwin_{tag}")
    dh = _matmul(du, w_in, "nt", name=f"pool_dh_{tag}")
    dx_in, dgain = _rms_bwd(x, gain, dh, dx, name=f"pool_dnorm_{tag}")
    return dx_in, grads, dw_group, dscale, dgain


CONV_HALO = 8
HIGHEST = lax.Precision.HIGHEST
NEG_BIG = -1e30


def _softplus(v):
    return jnp.maximum(v, 0.0) + jnp.log(1.0 + jnp.exp(-jnp.abs(v)))


def _dot_exact(a, b):
    return jnp.dot(a, b, precision=HIGHEST, preferred_element_type=F32)


def _conv_taps(ext, w_ref, off, rows):
    acc = None
    for k in range(SSD_CONV):
        shift = SSD_CONV - 1 - k
        v = (_shift_rows(ext, shift) if shift else ext)[off:off + rows] * w_ref[k:k + 1, :]
        acc = v if acc is None else acc + v
    return acc


def _ssd_conv_fwd(zx, conv_w, conv_b, col0, *, name):
    t = zx.shape[0]
    c = conv_w.shape[1]
    tm, tc = _tile(t, 512, CONV_HALO), _tile(c, 512, LANES)
    hb, cb0 = tm // CONV_HALO, col0 // tc
    assert col0 % tc == 0

    def body(x_ref, halo_ref, w_ref, b_ref, o_ref):
        halo = jnp.where(pl.program_id(0) > 0, halo_ref[...], 0.0)
        ext = jnp.concatenate([halo, x_ref[...]], axis=0)
        pre = _conv_taps(ext, w_ref, CONV_HALO, tm) + b_ref[...]
        o_ref[...] = pre * _sigmoid(pre)

    return pl.pallas_call(
        body, name=name, grid=(t // tm, c // tc),
        in_specs=[pl.BlockSpec((tm, tc), lambda i, j: (i, j + cb0)),
                  pl.BlockSpec((CONV_HALO, tc), lambda i, j: (jnp.maximum(i * hb - 1, 0), j + cb0)),
                  pl.BlockSpec((SSD_CONV, tc), lambda i, j: (0, j)), pl.BlockSpec((1, tc), lambda i, j: (0, j))],
        out_specs=pl.BlockSpec((tm, tc), lambda i, j: (i, j)),
        out_shape=jax.ShapeDtypeStruct((t, c), F32),
        compiler_params=_params("parallel", "parallel"),
    )(zx, zx, conv_w, conv_b)


def _ssd_conv_bwd(dxa, zx, conv_w, conv_b, col0, *, name):
    t = zx.shape[0]
    c = conv_w.shape[1]
    tm, tc = _tile(t, 512, CONV_HALO), _tile(c, 512, LANES)
    hb, cb0, nt = tm // CONV_HALO, col0 // tc, t // tm
    last_halo = t // CONV_HALO - 1

    def body(x_ref, prev_ref, nxt_ref, d_ref, dnxt_ref, w_ref, b_ref, dx_ref, dw_ref, db_ref):
        i = pl.program_id(1)

        @pl.when(i == 0)
        def _():
            dw_ref[...] = jnp.zeros_like(dw_ref)
            db_ref[...] = jnp.zeros_like(db_ref)

        prev = jnp.where(i > 0, prev_ref[...], 0.0)
        has_next = i < nt - 1
        ext = jnp.concatenate([prev, x_ref[...], jnp.where(has_next, nxt_ref[...], 0.0)], axis=0)
        pre = _conv_taps(ext, w_ref, CONV_HALO, tm + CONV_HALO) + b_ref[...]
        sg = _sigmoid(pre)
        dact = jnp.concatenate([d_ref[...], jnp.where(has_next, dnxt_ref[...], 0.0)], axis=0)
        dpre = dact * (sg * (1.0 + pre * (1.0 - sg)))
        db_ref[...] += jnp.sum(dpre[:tm], axis=0, keepdims=True)
        acc = None
        for k in range(SSD_CONV):
            shift = SSD_CONV - 1 - k
            src = (_shift_rows(ext, shift) if shift else ext)[CONV_HALO:CONV_HALO + tm]
            dw_ref[k:k + 1, :] += jnp.sum(dpre[:tm] * src, axis=0, keepdims=True)
            v = (_shift_rows(dpre, -shift) if shift else dpre)[:tm] * w_ref[k:k + 1, :]
            acc = v if acc is None else acc + v
        dx_ref[...] = acc.astype(BF16)

    main = lambda j, i: (i, j + cb0)
    return pl.pallas_call(
        body, name=name, grid=(c // tc, nt),
        in_specs=[pl.BlockSpec((tm, tc), main),
                  pl.BlockSpec((CONV_HALO, tc), lambda j, i: (jnp.maximum(i * hb - 1, 0), j + cb0)),
                  pl.BlockSpec((CONV_HALO, tc), lambda j, i: (jnp.minimum((i + 1) * hb, last_halo), j + cb0)),
                  pl.BlockSpec((tm, tc), lambda j, i: (i, j)),
                  pl.BlockSpec((CONV_HALO, tc), lambda j, i: (jnp.minimum((i + 1) * hb, last_halo), j)),
                  pl.BlockSpec((SSD_CONV, tc), lambda j, i: (0, j)), pl.BlockSpec((1, tc), lambda j, i: (0, j))],
        out_specs=[pl.BlockSpec((tm, tc), lambda j, i: (i, j)), pl.BlockSpec((SSD_CONV, tc), lambda j, i: (0, j)),
                   pl.BlockSpec((1, tc), lambda j, i: (0, j))],
        out_shape=[jax.ShapeDtypeStruct((t, c), BF16), jax.ShapeDtypeStruct((SSD_CONV, c), F32),
                   jax.ShapeDtypeStruct((1, c), F32)],
        compiler_params=_params("parallel", "arbitrary"),
    )(zx, zx, zx, dxa, dxa, conv_w, conv_b)


SSD_CUMSUM_PIECES = 3


def _ssd_group_pad(v, n_groups):
    lead = v.shape[:-1]
    v = v.reshape(*lead, n_groups, SSD_HEADS_PER_GROUP)
    v = jnp.pad(v, [(0, 0)] * (len(lead) + 1) + [(0, LANES - SSD_HEADS_PER_GROUP)])
    return v.reshape(*lead, n_groups * LANES)


def _ssd_group_unpad(v, n_groups):
    lead = v.shape[:-1]
    return v.reshape(*lead, n_groups, LANES)[..., :SSD_HEADS_PER_GROUP].reshape(*lead, -1)


def _ssd_chunk_common(dtp_ref, par_ref):
    ell = SSD_CHUNK
    dt = _softplus(dtp_ref[...] + par_ref[0:1, :])
    a = -jnp.exp(par_ref[1:2, :])
    row = lax.broadcasted_iota(jnp.int32, (ell, ell), 0)
    col = lax.broadcasted_iota(jnp.int32, (ell, ell), 1)
    acum = _split_dot(dt * a, (row >= col).astype(BF16), SSD_CUMSUM_PIECES, left=True)
    return dt, a, acum, acum.T, row, col


def _ssd_scan_fwd(xa, dtp, par, n_groups, *, name):
    t = xa.shape[0]
    ell, hd, hpg, ns = SSD_CHUNK, SSD_HEAD_DIM, SSD_HEADS_PER_GROUP, SSD_STATE
    gw = hpg * hd
    nc = t // ell
    b_blk0, c_blk0 = n_groups * gw // ns, n_groups * gw // ns + n_groups

    def body(xs_ref, b_ref, c_ref, dtp_ref, par_ref, y_ref, sin_ref, st):
        @pl.when(pl.program_id(1) == 0)
        def _():
            st[...] = jnp.zeros_like(st)

        dt, _, acum, acum_t, row, col = _ssd_chunk_common(dtp_ref, par_ref)
        bb, cc = b_ref[...].astype(BF16), c_ref[...].astype(BF16)
        cb = lax.dot_general(cc, bb, NT_DIMS, preferred_element_type=F32)
        for hh in range(hpg):
            lanes = slice(hh * hd, (hh + 1) * hd)
            col_a, row_a = acum[:, hh:hh + 1], acum_t[hh:hh + 1, :]
            decay = jnp.exp(jnp.where(row >= col, col_a - row_a, NEG_BIG))
            xdt = xs_ref[:, lanes] * dt[:, hh:hh + 1]
            s_h = st[hh]
            sin_ref[lanes, :] = s_h
            y = jnp.dot((cb * decay).astype(BF16), xdt.astype(BF16), preferred_element_type=F32)
            y += jnp.exp(col_a) * lax.dot_general(cc, s_h.astype(BF16), NT_DIMS, preferred_element_type=F32)
            y_ref[:, lanes] = y
            a_last = acum[ell - 1:ell, hh:hh + 1]
            w = jnp.exp(a_last - col_a)
            st[hh] = jnp.exp(a_last) * s_h + lax.dot_general(
                (xdt * w).astype(BF16), bb, (((0,), (0,)), ((), ())), preferred_element_type=F32)

    return pl.pallas_call(
        body, name=name, grid=(n_groups, nc),
        in_specs=[pl.BlockSpec((ell, gw), lambda g, c: (c, g)),
                  pl.BlockSpec((ell, ns), lambda g, c: (c, b_blk0 + g)),
                  pl.BlockSpec((ell, ns), lambda g, c: (c, c_blk0 + g)),
                  pl.BlockSpec((ell, LANES), lambda g, c: (c, g)),
                  pl.BlockSpec((SUBLANES, LANES), lambda g, c: (0, g))],
        out_specs=[pl.BlockSpec((ell, gw), lambda g, c: (c, g)),
                   pl.BlockSpec((None, None, gw, ns), lambda g, c: (c, g, 0, 0))],
        out_shape=[jax.ShapeDtypeStruct((t, n_groups * gw), F32),
                   jax.ShapeDtypeStruct((nc, n_groups, gw, ns), F32)],
        scratch_shapes=[pltpu.VMEM((hpg, hd, ns), F32)],
        compiler_params=_params("parallel", "arbitrary"),
    )(xa, xa, xa, dtp, par)


def _ssd_scan_bwd(dy, xa, dtp, par, s_in, n_groups, *, name):
    t = xa.shape[0]
    ell, hd, hpg, ns = SSD_CHUNK, SSD_HEAD_DIM, SSD_HEADS_PER_GROUP, SSD_STATE
    gw = hpg * hd
    nc = t // ell
    b_blk0, c_blk0 = n_groups * gw // ns, n_groups * gw // ns + n_groups
    nt_dims = (((1,), (1,)), ((), ()))
    tn_dims = (((0,), (0,)), ((), ()))

    def body(dy_ref, xs_ref, b_ref, c_ref, dtp_ref, par_ref, sin_ref,
             dxs_ref, db_ref, dc_ref, ddtp_ref, dpar_ref, dst):
        @pl.when(pl.program_id(1) == 0)
        def _():
            dst[...] = jnp.zeros_like(dst)
            dpar_ref[...] = jnp.zeros_like(dpar_ref)

        dtg, a_g, acum, acum_t, row, col = _ssd_chunk_common(dtp_ref, par_ref)
        bb, cc = b_ref[...].astype(BF16), c_ref[...].astype(BF16)
        cb = lax.dot_general(cc, bb, nt_dims, preferred_element_type=F32)
        cb_t = lax.dot_general(bb, cc, nt_dims, preferred_element_type=F32)
        lane = lax.broadcasted_iota(jnp.int32, (1, LANES), 1)
        dcb = jnp.zeros((ell, ell), F32)
        dcb_t = jnp.zeros((ell, ell), F32)
        dc = jnp.zeros((ell, ns), F32)
        db = jnp.zeros((ell, ns), F32)
        dacum = jnp.zeros((ell, LANES), F32)
        xsum = jnp.zeros((ell, LANES), F32)
        dsum = jnp.zeros((1, LANES), F32)
        for hh in range(hpg):
            lanes = slice(hh * hd, (hh + 1) * hd)
            onehot = (lane == hh).astype(F32)
            col_a, row_a = acum[:, hh:hh + 1], acum_t[hh:hh + 1, :]
            decay = jnp.exp(jnp.where(row >= col, col_a - row_a, NEG_BIG))
            decay_t = jnp.exp(jnp.where(col >= row, row_a - col_a, NEG_BIG))
            e_col = jnp.exp(col_a)
            a_last = acum[ell - 1:ell, hh:hh + 1]
            w = jnp.exp(a_last - col_a)
            e_last = jnp.exp(a_last)
            xs_h, dy_h = xs_ref[:, lanes], dy_ref[:, lanes]
            dt_h = dtg[:, hh:hh + 1]
            xdt = xs_h * dt_h
            xdt_b, dy_b = xdt.astype(BF16), dy_h.astype(BF16)
            s_h, ds_h = sin_ref[lanes, :], dst[hh]
            dm_decay = lax.dot_general(dy_b, xdt_b, nt_dims, preferred_element_type=F32) * decay
            dm_decay_t = lax.dot_general(xdt_b, dy_b, nt_dims, preferred_element_type=F32) * decay_t
            dcb += dm_decay
            dcb_t += dm_decay_t
            m_t = cb_t * decay_t
            dac = jnp.sum(dm_decay * cb, axis=1, keepdims=True) - jnp.sum(dm_decay_t * cb_t, axis=1, keepdims=True)
            b_ds = lax.dot_general(bb, ds_h.astype(BF16), nt_dims, preferred_element_type=F32)
            dxdt = jnp.dot(m_t.astype(BF16), dy_b, preferred_element_type=F32) + w * b_ds
            c_s = lax.dot_general(cc, s_h.astype(BF16), nt_dims, preferred_element_type=F32)
            dc += e_col * jnp.dot(dy_b, s_h.astype(BF16), preferred_element_type=F32)
            db += w * jnp.dot(xdt_b, ds_h.astype(BF16), preferred_element_type=F32)
            dac += jnp.sum(dy_h * c_s, axis=1, keepdims=True) * e_col
            q = jnp.sum(xdt * b_ds, axis=1, keepdims=True) * w
            dac -= q
            d_last = jnp.sum(q, axis=0, keepdims=True) + e_last * jnp.sum(
                jnp.sum(s_h * ds_h, axis=1, keepdims=True), axis=0, keepdims=True)
            is_last = lax.broadcasted_iota(jnp.int32, (ell, 1), 0) == ell - 1
            dac += jnp.where(is_last, d_last, 0.0)
            dacum += dac * onehot
            dst[hh] = e_last * ds_h + lax.dot_general((dy_h * e_col).astype(BF16), cc, tn_dims,
                                                      preferred_element_type=F32)
            dxs_ref[:, lanes] = dxdt * dt_h + dy_h * par_ref[2:3, hh:hh + 1]
            xsum += jnp.sum(dxdt * xs_h, axis=1, keepdims=True) * onehot
            dsum += jnp.sum(jnp.sum(dy_h * xs_h, axis=1, keepdims=True), axis=0, keepdims=True) * onehot
        dc_ref[...] = dc + jnp.dot(dcb.astype(BF16), bb, preferred_element_type=F32)
        db_ref[...] = db + jnp.dot(dcb_t.astype(BF16), cc, preferred_element_type=F32)
        dda = _split_dot(dacum, (col >= row).astype(BF16), SSD_CUMSUM_PIECES, left=True)
        ddtp = (xsum + dda * a_g) * _sigmoid(dtp_ref[...] + par_ref[0:1, :])
        ddtp_ref[...] = ddtp
        dpar_ref[0:1, :] += jnp.sum(ddtp, axis=0, keepdims=True)
        dpar_ref[1:2, :] += jnp.sum(dda * dtg, axis=0, keepdims=True) * a_g
        dpar_ref[2:3, :] += dsum

    rev = lambda i: nc - 1 - i
    return pl.pallas_call(
        body, name=name, grid=(n_groups, nc),
        in_specs=[pl.BlockSpec((ell, gw), lambda g, i: (rev(i), g)),
                  pl.BlockSpec((ell, gw), lambda g, i: (rev(i), g)),
                  pl.BlockSpec((ell, ns), lambda g, i: (rev(i), b_blk0 + g)),
                  pl.BlockSpec((ell, ns), lambda g, i: (rev(i), c_blk0 + g)),
                  pl.BlockSpec((ell, LANES), lambda g, i: (rev(i), g)),
                  pl.BlockSpec((SUBLANES, LANES), lambda g, i: (0, g)),
                  pl.BlockSpec((None, None, gw, ns), lambda g, i: (rev(i), g, 0, 0))],
        out_specs=[pl.BlockSpec((ell, gw), lambda g, i: (rev(i), g)),
                   pl.BlockSpec((ell, ns), lambda g, i: (rev(i), g)),
                   pl.BlockSpec((ell, ns), lambda g, i: (rev(i), g)),
                   pl.BlockSpec((ell, LANES), lambda g, i: (rev(i), g)),
                   pl.BlockSpec((SUBLANES, LANES), lambda g, i: (0, g))],
        out_shape=[jax.ShapeDtypeStruct((t, n_groups * gw), F32), jax.ShapeDtypeStruct((t, n_groups * ns), F32),
                   jax.ShapeDtypeStruct((t, n_groups * ns), F32), jax.ShapeDtypeStruct((t, n_groups * LANES), F32),
                   jax.ShapeDtypeStruct((SUBLANES, n_groups * LANES), F32)],
        scratch_shapes=[pltpu.VMEM((hpg, hd, ns), F32)],
        compiler_params=_params("parallel", "arbitrary"),
    )(dy, xa, xa, xa, dtp, par, s_in)


def _ssd_gate_fwd(y, xa, zx, d_rep, out_norm, *, name):
    t, di = y.shape
    gw = SSD_HEADS_PER_GROUP * SSD_HEAD_DIM
    tm = _tile(t, 512, SUBLANES)

    def body(y_ref, xs_ref, z_ref, d_ref, n_ref, o_ref):
        zv = z_ref[...]
        gt = (y_ref[...] + d_ref[...] * xs_ref[...]) * (zv * _sigmoid(zv))
        r = lax.rsqrt(jnp.mean(gt * gt, axis=-1, keepdims=True) + NORM_EPS)
        o_ref[...] = (gt * r * n_ref[...]).astype(BF16)

    blk = pl.BlockSpec((tm, gw), lambda i, g: (i, g))
    vec = pl.BlockSpec((1, gw), lambda i, g: (0, g))
    return pl.pallas_call(
        body, name=name, grid=(t // tm, di // gw),
        in_specs=[blk, blk, blk, vec, vec], out_specs=blk,
        out_shape=jax.ShapeDtypeStruct((t, di), BF16),
        compiler_params=_params("parallel", "parallel"),
    )(y, xa, zx, d_rep, out_norm)


def _ssd_gate_bwd(dgn, y, xa, zx, d_rep, out_norm, *, name):
    t, di = y.shape
    gw = SSD_HEADS_PER_GROUP * SSD_HEAD_DIM
    tm = _tile(t, 512, SUBLANES)

    def body(dg_ref, y_ref, xs_ref, z_ref, d_ref, n_ref, dy_ref, dz_ref, dn_ref):
        @pl.when(pl.program_id(1) == 0)
        def _():
            dn_ref[...] = jnp.zeros_like(dn_ref)

        zv = z_ref[...]
        sg = _sigmoid(zv)
        sz = zv * sg
        y2 = y_ref[...] + d_ref[...] * xs_ref[...]
        gt = y2 * sz
        r = lax.rsqrt(jnp.mean(gt * gt, axis=-1, keepdims=True) + NORM_EPS)
        ghat = gt * r
        dgv = dg_ref[...]
        dn_ref[...] += jnp.sum(dgv * ghat, axis=0, keepdims=True)
        u = dgv * n_ref[...]
        dgt = r * (u - ghat * jnp.mean(u * ghat, axis=-1, keepdims=True))
        dy_ref[...] = dgt * sz
        dz_ref[...] = (dgt * y2 * (sg * (1.0 + zv * (1.0 - sg)))).astype(BF16)

    blk = pl.BlockSpec((tm, gw), lambda g, i: (i, g))
    vec = pl.BlockSpec((1, gw), lambda g, i: (0, g))
    return pl.pallas_call(
        body, name=name, grid=(di // gw, t // tm),
        in_specs=[blk, blk, blk, blk, vec, vec], out_specs=[blk, blk, vec],
        out_shape=[jax.ShapeDtypeStruct((t, di), F32), jax.ShapeDtypeStruct((t, di), BF16),
                   jax.ShapeDtypeStruct((1, di), F32)],
        compiler_params=_params("parallel", "arbitrary"),
    )(dgn, y, xa, zx, d_rep, out_norm)


def _ssd_mixer_fwd(x, gain, w_zx_t, w_dt_t, conv_w, conv_b, par, d_rep, out_norm, w_out, tag):
    di = w_out.shape[0]
    n_groups = di // (SSD_HEADS_PER_GROUP * SSD_HEAD_DIM)
    h = _rms_fwd(x, gain, name=f"ssd_norm_{tag}")
    zx = _matmul(h, w_zx_t, "nt", name=f"ssd_in_{tag}")
    dtp = _matmul(h, w_dt_t, "nt", name=f"ssd_dt_{tag}")
    xa = _ssd_conv_fwd(zx, conv_w, conv_b, di, name=f"ssd_conv_{tag}")
    y, s_in = _ssd_scan_fwd(xa, dtp, par, n_groups, name=f"ssd_scan_{tag}")
    gn = _ssd_gate_fwd(y, xa, zx, d_rep, out_norm, name=f"ssd_gate_{tag}")
    x_new = _matmul(gn, w_out, "nn", resid=x, name=f"ssd_out_{tag}")
    return x_new, (x, h, zx, dtp, xa, y, s_in, gn)


def _ssd_mixer_bwd(dx, saved, gain, w_zx_t, w_dt_t, conv_w, conv_b, par, d_rep, out_norm, w_out, grads, row_out,
                   tag):
    x, h, zx, dtp, xa, y, s_in, gn = saved
    di = w_out.shape[0]
    n_groups = di // (SSD_HEADS_PER_GROUP * SSD_HEAD_DIM)
    dgn = _matmul(dx, w_out, "nt", name=f"ssd_dgn_{tag}")
    grads = _matmul_tn_into(grads, gn, dx, row_out, name=f"ssd_dwout_{tag}")
    dy2, dz, dnorm = _ssd_gate_bwd(dgn, y, xa, zx, d_rep, out_norm, name=f"ssd_dgate_{tag}")
    dxs, db, dc, ddtp, dpar = _ssd_scan_bwd(dy2, xa, dtp, par, s_in, n_groups, name=f"ssd_dscan_{tag}")
    dxa = jnp.concatenate([dxs, db, dc], axis=1)
    dxbc, dconv_w, dconv_b = _ssd_conv_bwd(dxa, zx, conv_w, conv_b, di, name=f"ssd_dconv_{tag}")
    dzx = jnp.concatenate([dz, dxbc], axis=1)
    dw_zx_t = _matmul_tn(dzx, h, name=f"ssd_dwin_{tag}")
    dw_dt_t = _matmul_tn(ddtp, h, name=f"ssd_dwdt_{tag}")
    dh = _matmul(dzx, w_zx_t, "nn", name=f"ssd_dh_{tag}")
    dh = _matmul(ddtp, w_dt_t, "nn", resid=dh, name=f"ssd_dhdt_{tag}")
    dx_in, dgain = _rms_bwd(x, gain, dh, dx, name=f"ssd_dnorm_{tag}")
    return dx_in, grads, dw_zx_t, dw_dt_t, dconv_w, dconv_b, dpar, dnorm, dgain


def _sb_qk_norm_fwd(qkv, gains, *, name):
    ns, t, _ = qkv.shape
    per = ns // 3
    tm = _tile(t, 1024, SUBLANES)
    inv_sqrt_d = 1.0 / math.sqrt(SB_HEAD_DIM)

    def body(x_ref, g_ref, o_ref):
        kind = pl.program_id(0) // per
        xv = x_ref[...]

        @pl.when(kind == 2)
        def _():
            o_ref[...] = xv.astype(BF16)

        @pl.when(kind < 2)
        def _():
            left = lax.broadcasted_iota(jnp.int32, (1, LANES), 1) < SB_HEAD_DIM
            sq = xv * xv
            ms = jnp.where(left, jnp.sum(jnp.where(left, sq, 0.0), axis=1, keepdims=True),
                           jnp.sum(jnp.where(left, 0.0, sq), axis=1, keepdims=True)) * (1.0 / SB_HEAD_DIM)
            y = xv * lax.rsqrt(ms + NORM_EPS) * g_ref[pl.ds(kind, 1), :]
            o_ref[...] = (y * jnp.where(kind == 0, inv_sqrt_d, 1.0)).astype(BF16)

    blk = pl.BlockSpec((None, tm, LANES), lambda s, i: (s, i, 0))
    return pl.pallas_call(
        body, name=name, grid=(ns, t // tm),
        in_specs=[blk, pl.BlockSpec((SUBLANES, LANES), lambda s, i: (0, 0))], out_specs=blk,
        out_shape=jax.ShapeDtypeStruct((ns, t, LANES), BF16),
        compiler_params=_params("parallel", "parallel"),
    )(qkv, gains)


def _sb_qk_norm_bwd(dq, dk, dv, qkv, gains, *, name):
    ns, t, _ = qkv.shape
    per = ns // 3
    tm = _tile(t, 1024, SUBLANES)
    inv_sqrt_d = 1.0 / math.sqrt(SB_HEAD_DIM)

    def body(dq_ref, dk_ref, dv_ref, x_ref, g_ref, o_ref, dg_ref):
        s = pl.program_id(0)
        kind = s // per

        @pl.when((s == 0) & (pl.program_id(1) == 0))
        def _():
            dg_ref[...] = jnp.zeros_like(dg_ref)

        @pl.when(kind == 2)
        def _():
            o_ref[...] = dv_ref[...].astype(BF16)

        @pl.when(kind < 2)
        def _():
            xv = x_ref[...]
            dy = jnp.where(kind == 0, dq_ref[...] * inv_sqrt_d, dk_ref[...])
            left = lax.broadcasted_iota(jnp.int32, (1, LANES), 1) < SB_HEAD_DIM

            def halves(v):
                return jnp.where(left, jnp.sum(jnp.where(left, v, 0.0), axis=1, keepdims=True),
                                 jnp.sum(jnp.where(left, 0.0, v), axis=1, keepdims=True))

            r = lax.rsqrt(halves(xv * xv) * (1.0 / SB_HEAD_DIM) + NORM_EPS)
            xhat = xv * r
            u = dy * g_ref[pl.ds(kind, 1), :]
            o_ref[...] = (r * (u - xhat * halves(u * xhat) * (1.0 / SB_HEAD_DIM))).astype(BF16)
            dg_ref[pl.ds(kind, 1), :] += jnp.sum(dy * xhat, axis=0, keepdims=True)

    def grad_blk(kind):
        def index(s, i):
            mine = (s >= kind * per) & (s < (kind + 1) * per)
            return jnp.where(mine, s - kind * per, 0), jnp.where(mine, i, 0), 0
        return pl.BlockSpec((None, tm, LANES), index)

    blk = pl.BlockSpec((None, tm, LANES), lambda s, i: (s, i, 0))
    vec = pl.BlockSpec((SUBLANES, LANES), lambda s, i: (0, 0))
    return pl.pallas_call(
        body, name=name, grid=(ns, t // tm),
        in_specs=[grad_blk(0), grad_blk(1), grad_blk(2), blk, vec], out_specs=[blk, vec],
        out_shape=[jax.ShapeDtypeStruct((ns, t, LANES), BF16), jax.ShapeDtypeStruct((SUBLANES, LANES), F32)],
        compiler_params=_params("arbitrary", "arbitrary"),
    )(dq, dk, dv, qkv, gains)


def _split_dot(v, ones_mat, pieces, left=False):
    total, rest = None, v
    for p in range(pieces):
        part = rest.astype(BF16)
        if p + 1 < pieces:
            rest = rest - part.astype(F32)
        d = (jnp.dot(ones_mat, part, preferred_element_type=F32) if left
             else jnp.dot(part, ones_mat, preferred_element_type=F32))
        total = d if total is None else total + d
    return total


LOGIT_SUM_PIECES = 2
GRAD_SUM_PIECES = 2
LOG_WEIGHT_UNDERFLOW = -105.0


def _sb_attn_fwd(qkv_n, n_heads, *, name):
    ns, t, _ = qkv_n.shape
    per = ns // 3
    bq, blk, hd = SB_QUERY_BLOCK, SB_BLOCK, SB_HEAD_DIM
    nq, n_diag = t // bq, bq // blk

    def body(q_ref, k_ref, v_ref, o_ref):
        i = pl.program_id(1)
        row = lax.broadcasted_iota(jnp.int32, (blk, blk), 0)
        col = lax.broadcasted_iota(jnp.int32, (blk, blk), 1)
        later_keys = (row > col).astype(BF16)
        qry = lax.broadcasted_iota(jnp.int32, (bq, blk), 0)
        key = lax.broadcasted_iota(jnp.int32, (bq, blk), 1)

        def tile(kb, carry, key_offset):
            out = []
            start = pl.multiple_of(kb * blk, blk)
            for hf in range(2):
                lanes = slice(hf * hd, (hf + 1) * hd)
                run, acc = carry[hf]
                z = lax.dot_general(q_ref[:, lanes], k_ref[pl.ds(start, blk), lanes], NT_DIMS,
                                    preferred_element_type=F32)
                sp = _softplus(z)
                lm = -sp if key_offset is None else jnp.where(key + key_offset < qry, -sp, 0.0)
                after = _split_dot(lm, later_keys, LOGIT_SUM_PIECES) + run
                a = jnp.exp(z - sp + after)
                if key_offset is not None:
                    a = jnp.where(key + key_offset < qry, a, 0.0)
                acc = acc + jnp.dot(a.astype(BF16), v_ref[pl.ds(start, blk), lanes], preferred_element_type=F32)
                out.append((run + jnp.sum(lm, axis=1, keepdims=True), acc))
            return tuple(out)

        def live(carry):
            return jnp.max(jnp.maximum(carry[0][0], carry[1][0])) > LOG_WEIGHT_UNDERFLOW

        def step(state):
            s, _, carry = state
            carry = tile(n_diag * i - 1 - s, carry, None)
            return s + 1, live(carry), carry

        carry = tuple((jnp.zeros((bq, 1), F32), jnp.zeros((bq, hd), F32)) for _ in range(2))
        for j in reversed(range(n_diag)):
            carry = tile(n_diag * i + j, carry, j * blk)
        _, _, carry = lax.while_loop(lambda st: (st[0] < n_diag * i) & st[1], step,
                                     (jnp.int32(0), live(carry), carry))
        o_ref[...] = jnp.concatenate([carry[0][1], carry[1][1]], axis=1)

    return pl.pallas_call(
        body, name=name, grid=(per, nq),
        in_specs=[pl.BlockSpec((None, bq, LANES), lambda p, i: (p, i, 0)),
                  pl.BlockSpec((None, t, LANES), lambda p, i: (per + p, 0, 0)),
                  pl.BlockSpec((None, t, LANES), lambda p, i: (2 * per + p, 0, 0))],
        out_specs=pl.BlockSpec((bq, LANES), lambda p, i: (i, p)),
        out_shape=jax.ShapeDtypeStruct((t, n_heads * hd), F32),
        compiler_params=_params("parallel", "arbitrary"),
    )(qkv_n, qkv_n, qkv_n)


def _sb_attn_bwd(do, qkv_n, *, name):
    ns, t, _ = qkv_n.shape
    per = ns // 3
    bq, blk, hd = SB_QUERY_BLOCK, SB_BLOCK, SB_HEAD_DIM
    nq, n_diag = t // bq, bq // blk
    nt_dims = (((1,), (1,)), ((), ()))
    tn_dims = (((0,), (0,)), ((), ()))

    def body(q_ref, k_ref, v_ref, do_ref, dq_ref, dk_ref, dv_ref):
        i = pl.program_id(1)

        @pl.when(i == 0)
        def _():
            dk_ref[...] = jnp.zeros_like(dk_ref)
            dv_ref[...] = jnp.zeros_like(dv_ref)

        row = lax.broadcasted_iota(jnp.int32, (blk, blk), 0)
        col = lax.broadcasted_iota(jnp.int32, (blk, blk), 1)
        later_keys = (col > row).astype(BF16)
        earlier_keys = (col < row).astype(BF16)
        key = lax.broadcasted_iota(jnp.int32, (blk, bq), 0)
        qry = lax.broadcasted_iota(jnp.int32, (blk, bq), 1)
        halves = [slice(hf * hd, (hf + 1) * hd) for hf in range(2)]
        q_hs = [q_ref[:, lanes] for lanes in halves]
        do_bs = [do_ref[:, lanes].astype(BF16) for lanes in halves]

        def scores(kb, hf, key_offset):
            k_blk = k_ref[pl.ds(pl.multiple_of(kb * blk, blk), blk), halves[hf]]
            z = lax.dot_general(k_blk, q_hs[hf], nt_dims, preferred_element_type=F32)
            sp = _softplus(z)
            return k_blk, z, sp, (-sp if key_offset is None else jnp.where(key + key_offset < qry, -sp, 0.0))

        def add_column_sums(tots, kb, key_offset):
            return [tots[hf] + jnp.sum(scores(kb, hf, key_offset)[3], axis=0, keepdims=True) for hf in range(2)]

        def live(tots):
            return jnp.max(jnp.maximum(tots[0], tots[1])) > LOG_WEIGHT_UNDERFLOW

        def reach(state):
            s, _, tots = state
            tots = add_column_sums(tots, n_diag * i - 1 - s, None)
            return s + 1, live(tots), tots

        tots = [jnp.zeros((1, bq), F32)] * 2
        for j in reversed(range(n_diag)):
            tots = add_column_sums(tots, n_diag * i + j, j * blk)
        reached, _, tots = lax.while_loop(lambda st: (st[0] < n_diag * i) & st[1], reach,
                                          (jnp.int32(0), live(tots), tots))

        def tile(kb, carry, key_offset):
            out = []
            start = pl.multiple_of(kb * blk, blk)
            for hf, lanes in enumerate(halves):
                seen, gsum, dq = carry[hf]
                q_h, do_b = q_hs[hf], do_bs[hf]
                k_blk, z, sp, lm = scores(kb, hf, key_offset)
                blk_tot = jnp.sum(lm, axis=0, keepdims=True)
                after = _split_dot(lm, later_keys, LOGIT_SUM_PIECES, left=True) + (tots[hf] - seen - blk_tot)
                a = jnp.exp(z - sp + after)
                if key_offset is not None:
                    a = jnp.where(key + key_offset < qry, a, 0.0)
                da = lax.dot_general(v_ref[pl.ds(start, blk), lanes], do_b, nt_dims, preferred_element_type=F32)
                g = da * a
                before = _split_dot(g, earlier_keys, GRAD_SUM_PIECES, left=True) + gsum
                omb = jnp.exp(-sp)
                dz = g * omb - (1.0 - omb) * before
                if key_offset is not None:
                    dz = jnp.where(key + key_offset < qry, dz, 0.0)
                dz_b = dz.astype(BF16)
                dk_ref[pl.ds(start, blk), lanes] += jnp.dot(dz_b, q_h, preferred_element_type=F32)
                dv_ref[pl.ds(start, blk), lanes] += jnp.dot(a.astype(BF16), do_b, preferred_element_type=F32)
                dq = dq + lax.dot_general(dz_b, k_blk, tn_dims, preferred_element_type=F32)
                out.append((seen + blk_tot, gsum + jnp.sum(g, axis=0, keepdims=True), dq))
            return tuple(out)

        init = tuple((jnp.zeros((1, bq), F32), jnp.zeros((1, bq), F32), jnp.zeros((bq, hd), F32))
                     for _ in range(2))
        carry = lax.fori_loop(n_diag * i - reached, n_diag * i, lambda kb, c: tile(kb, c, None), init)
        for j in range(n_diag):
            carry = tile(n_diag * i + j, carry, j * blk)
        dq_ref[...] = jnp.concatenate([carry[0][2], carry[1][2]], axis=1)

    full = lambda off: pl.BlockSpec((None, t, LANES), lambda p, i: (off + p, 0, 0))
    q_blk = pl.BlockSpec((None, bq, LANES), lambda p, i: (p, i, 0))
    slab = jax.ShapeDtypeStruct((per, t, LANES), F32)
    return pl.pallas_call(
        body, name=name, grid=(per, nq),
        in_specs=[q_blk, full(per), full(2 * per), pl.BlockSpec((bq, LANES), lambda p, i: (i, p))],
        out_specs=[q_blk, full(0), full(0)],
        out_shape=[slab, slab, slab],
        compiler_params=_params("parallel", "arbitrary"),
    )(qkv_n, qkv_n, qkv_n, do)


def _sb_mixer_fwd(x, gain, w_qkv_t, qk_gains, w_out, tag):
    n_heads = w_out.shape[0] // SB_HEAD_DIM
    h = _rms_fwd(x, gain, name=f"sb_norm_{tag}")
    qkv = _matmul(h, w_qkv_t, "nt", out_slabs=True, tn_cap=256, name=f"sb_qkv_{tag}")
    qkv_n = _sb_qk_norm_fwd(qkv, qk_gains, name=f"sb_qknorm_{tag}")
    o = _sb_attn_fwd(qkv_n, n_heads, name=f"sb_attn_{tag}")
    x_new = _matmul(o, w_out, "nn", resid=x, name=f"sb_out_{tag}")
    return x_new, (x, h, qkv, qkv_n, o)


def _sb_mixer_bwd(dx, saved, gain, w_qkv_t, qk_gains, w_out, grads, row_qkv, row_out, tag):
    x, h, qkv, qkv_n, o = saved
    do = _matmul(dx, w_out, "nt", name=f"sb_do_{tag}")
    grads = _matmul_tn_into(grads, o, dx, row_out, name=f"sb_dwout_{tag}")
    dq, dk, dv = _sb_attn_bwd(do, qkv_n, name=f"sb_dattn_{tag}")
    dqkv, dqk_gains = _sb_qk_norm_bwd(dq, dk, dv, qkv, qk_gains, name=f"sb_dqknorm_{tag}")
    grads = _matmul_tn_into(grads, dqkv, h, row_qkv, a_slabs=True, name=f"sb_dwqkv_{tag}")
    dh = _matmul(dqkv, w_qkv_t, "nn", a_slabs=True, name=f"sb_dh_{tag}")
    dx_in, dgain = _rms_bwd(x, gain, dh, dx, name=f"sb_dnorm_{tag}")
    return dx_in, grads, dqk_gains, dgain


MESH = pl.DeviceIdType.MESH


def _position():
    return lax.axis_index("x"), lax.axis_index("y"), lax.axis_index("c")


def _all_gather(shard, *, name, in_vmem):
    rows, n = shard.shape
    space = pltpu.VMEM if in_vmem else pltpu.HBM

    def body(x_ref, out_ref, send_sems, recv_sems, local_sem):
        x, y, c = _position()
        me, sibling = (x, y, c), (x, y, 1 - c)
        chips = [(1 - x, y), (x, 1 - y), (1 - x, 1 - y)]

        def block(px, py, pc):
            return out_ref.at[4 * px + 2 * py + pc]

        def copy(k, blk, to, src=None):
            return pltpu.make_async_remote_copy(
                src_ref=block(*blk) if src is None else src, dst_ref=block(*blk),
                send_sem=send_sems.at[k], recv_sem=recv_sems.at[k], device_id=to, device_id_type=MESH)

        mine = pltpu.make_async_copy(x_ref, block(*me), local_sem)
        mine.start()
        first = [copy(0, me, sibling, src=x_ref)]
        first += [copy(1 + j, me, (*chip, c), src=x_ref) for j, chip in enumerate(chips)]
        for cp in first:
            cp.start()
        passed = [copy(4 + j, (*chip, c), sibling) for j, chip in enumerate(chips)]
        for j, chip in enumerate(chips):
            copy(1 + j, (*chip, c), me).wait_recv()
            passed[j].start()
        copy(0, sibling, me).wait_recv()
        for j, chip in enumerate(chips):
            copy(4 + j, (*chip, 1 - c), me).wait_recv()
        for cp in first + passed:
            cp.wait_send()
        mine.wait()

    return pl.pallas_call(
        body, name=name,
        out_shape=jax.ShapeDtypeStruct((N_DEV, rows, n), shard.dtype),
        in_specs=[pl.BlockSpec(memory_space=space)], out_specs=pl.BlockSpec(memory_space=space),
        scratch_shapes=[pltpu.SemaphoreType.DMA((7,)), pltpu.SemaphoreType.DMA((7,)), pltpu.SemaphoreType.DMA],
        compiler_params=pltpu.CompilerParams(vmem_limit_bytes=V7X_VMEM_LIMIT_BYTES),
    )(shard)


def _exchange_sibling(parts, *, name):
    nchip, _, rows, n = parts.shape

    def body(p_ref, recv_ref, send_sem, recv_sem):
        x, y, c = _position()
        cp = pltpu.make_async_remote_copy(src_ref=p_ref.at[:, 1 - c], dst_ref=recv_ref, send_sem=send_sem,
                                          recv_sem=recv_sem, device_id=(x, y, 1 - c), device_id_type=MESH)
        cp.start()
        cp.wait()

    return pl.pallas_call(
        body, name=name,
        out_shape=jax.ShapeDtypeStruct((nchip, rows, n), parts.dtype),
        in_specs=[pl.BlockSpec(memory_space=pltpu.HBM)], out_specs=pl.BlockSpec(memory_space=pltpu.HBM),
        scratch_shapes=[pltpu.SemaphoreType.DMA, pltpu.SemaphoreType.DMA],
    )(parts)


def _exchange_chips(chip_sums, *, name):
    _, rows, n = chip_sums.shape

    def body(s_ref, recv_ref, send_sems, recv_sems):
        x, y, c = _position()
        chips = [(1 - x, y), (x, 1 - y), (1 - x, 1 - y)]
        copies = [pltpu.make_async_remote_copy(
            src_ref=s_ref.at[2 * cx + cy], dst_ref=recv_ref.at[j], send_sem=send_sems.at[j],
            recv_sem=recv_sems.at[j], device_id=(cx, cy, c), device_id_type=MESH)
            for j, (cx, cy) in enumerate(chips)]
        for cp in copies:
            cp.start()
        for cp in copies:
            cp.wait()

    return pl.pallas_call(
        body, name=name,
        out_shape=jax.ShapeDtypeStruct((3, rows, n), chip_sums.dtype),
        in_specs=[pl.BlockSpec(memory_space=pltpu.HBM)], out_specs=pl.BlockSpec(memory_space=pltpu.HBM),
        scratch_shapes=[pltpu.SemaphoreType.DMA((3,)), pltpu.SemaphoreType.DMA((3,))],
    )(chip_sums)


def _add_pairs(parts, recv, c_mine, *, name):
    nchip, _, rows, n = parts.shape
    tr = _tile(rows, 512, SUBLANES)

    def body(c_ref, a_ref, b_ref, o_ref, wire_ref):
        s = a_ref[...] + b_ref[...]
        o_ref[...] = s
        wire_ref[...] = s.astype(WIRE_DTYPE)

    out_blk = pl.BlockSpec((None, tr, n), lambda k, i, c: (k, i, 0))
    return pl.pallas_call(
        body, name=name,
        grid_spec=pltpu.PrefetchScalarGridSpec(
            num_scalar_prefetch=1, grid=(nchip, rows // tr),
            in_specs=[pl.BlockSpec((None, None, tr, n), lambda k, i, c: (k, c[0], i, 0)),
                      pl.BlockSpec((None, tr, n), lambda k, i, c: (k, i, 0))],
            out_specs=[out_blk, out_blk]),
        out_shape=[jax.ShapeDtypeStruct((nchip, rows, n), parts.dtype),
                   jax.ShapeDtypeStruct((nchip, rows, n), WIRE_DTYPE)],
        compiler_params=_params("parallel", "parallel"),
    )(c_mine, parts, recv)


def _adamw_math(w, g, m, v):
    m = ADAM_B1 * m + (1.0 - ADAM_B1) * g
    v = ADAM_B2 * v + (1.0 - ADAM_B2) * (g * g)
    m_hat = m / (1.0 - ADAM_B1 ** ADAM_STEP)
    v_hat = v / (1.0 - ADAM_B2 ** ADAM_STEP)
    delta = -ADAM_LR * (m_hat / (jnp.sqrt(v_hat) + ADAM_EPS) + ADAM_WD * w)
    return delta, m, v


def _adamw_sharded(chip_sums, recv, k_mine, w, m, v, *, name):
    rows, n = w.shape
    tr = _tile(rows, 256, SUBLANES)

    def body(k_ref, s_ref, r_ref, w_ref, m_ref, v_ref, g_out, d_out, m_out, v_out):
        g = ((s_ref[...] + r_ref[0].astype(F32)) + r_ref[1].astype(F32)) + r_ref[2].astype(F32)
        delta, m_new, v_new = _adamw_math(w_ref[...], g, m_ref[...], v_ref[...])
        g_out[...] = g
        d_out[...] = delta
        m_out[...] = m_new
        v_out[...] = v_new

    blk = pl.BlockSpec((tr, n), lambda i, k: (i, 0))
    out = jax.ShapeDtypeStruct((rows, n), F32)
    return pl.pallas_call(
        body, name=name,
        grid_spec=pltpu.PrefetchScalarGridSpec(
            num_scalar_prefetch=1, grid=(rows // tr,),
            in_specs=[pl.BlockSpec((None, tr, n), lambda i, k: (k[0], i, 0)),
                      pl.BlockSpec((3, tr, n), lambda i, k: (0, i, 0)), blk, blk, blk],
            out_specs=[blk, blk, blk, blk]),
        out_shape=[out, out, out, out],
        compiler_params=_params("parallel"),
    )(k_mine, chip_sums, recv, w, m, v)


SMALL_ROWS = 40
ROW_MIX_NORM, ROW_FFN_NORM, ROW_CONV_B, ROW_OUT_NORM, ROW_POOL_SCALE, ROW_CONV_W = 0, 4, 8, 12, 14, 16
ROW_SSD_VEC, ROW_QK_GAIN, ROW_LOSS = 32, 33, 34


def _adamw_small(gathered, w, m, v, *, name):
    _, rows, n = gathered.shape

    def body(a_ref, w_ref, m_ref, v_ref, g_out, d_out, m_out, v_out):
        g = a_ref[0]
        for d in range(1, N_DEV):
            g = g + a_ref[d]
        row = lax.broadcasted_iota(jnp.int32, (rows, 1), 0)
        g = jnp.where(row == ROW_QK_GAIN, g + pltpu.roll(g, SB_HEAD_DIM, 1), g)
        g = jnp.where(row == ROW_LOSS, jnp.sum(g, axis=1, keepdims=True), g)
        g_out[...] = g
        delta, m_new, v_new = _adamw_math(w_ref[...], g, m_ref[...], v_ref[...])
        d_out[...] = delta
        m_out[...] = m_new
        v_out[...] = v_new

    out = jax.ShapeDtypeStruct((rows, n), F32)
    return pl.pallas_call(body, name=name, out_shape=[out, out, out, out])(gathered, w, m, v)


BIG_WEIGHTS = ("ffn_gate", "ffn_up", "ffn_down", "sb_qkv", "ssd_out", "pool_in", "sb_out", "pool_group", "ssd_in")
COLUMN_SHARDED = ("ssd_in", "sb_qkv", "ffn_gate", "ffn_up")
ROW_PAD = 512
WIRE_DTYPE = jnp.bfloat16


def _to_rows(name, shard, d):
    if name in COLUMN_SHARDED:
        shard = jnp.swapaxes(shard, -1, -2)
    return shard.reshape(-1, d)


def _from_rows(name, rows, shard_shape):
    if name in COLUMN_SHARDED:
        lead, k, n = shard_shape
        return jnp.swapaxes(rows.reshape(lead, n, k), -1, -2)
    return rows.reshape(shard_shape)


def _pad_rows(a, total):
    return jnp.pad(a, ((0, total - a.shape[0]),) + ((0, 0),) * (a.ndim - 1))


def _exact_bf16_rows(v, d):
    words = lax.bitcast_convert_type(v.reshape(-1), WIRE_DTYPE).reshape(-1)
    return _pad_rows(words, -(-words.shape[0] // d) * d).reshape(-1, d)


def _exact_f32(rows, count):
    words = rows.reshape(rows.shape[0], -1)[:, :2 * count].reshape(rows.shape[0], count, 2)
    return lax.bitcast_convert_type(words, F32)


def _device_blocks(full, d):
    return full.reshape(N_DEV, -1, d)


def kernel(x, mix_norm, pool_in, pool_group, pool_scale, ssd_in, ssd_conv_w, ssd_conv_b, ssd_dt_bias, ssd_a_log, ssd_d, ssd_out_norm, ssd_out, sb_qkv, sb_q_norm, sb_k_norm, sb_out, ffn_norm, ffn_gate, ffn_up, ffn_down, loss_target, m_mix_norm, m_pool_in, m_pool_group, m_pool_scale, m_ssd_in, m_ssd_conv_w, m_ssd_conv_b, m_ssd_dt_bias, m_ssd_a_log, m_ssd_d, m_ssd_out_norm, m_ssd_out, m_sb_qkv, m_sb_q_norm, m_sb_k_norm, m_sb_out, m_ffn_norm, m_ffn_gate, m_ffn_up, m_ffn_down, v_mix_norm, v_pool_in, v_pool_group, v_pool_scale, v_ssd_in, v_ssd_conv_w, v_ssd_conv_b, v_ssd_dt_bias, v_ssd_a_log, v_ssd_d, v_ssd_out_norm, v_ssd_out, v_sb_qkv, v_sb_q_norm, v_sb_k_norm, v_sb_out, v_ffn_norm, v_ffn_gate, v_ffn_up, v_ffn_down):
    weights = dict(mix_norm=mix_norm, pool_in=pool_in, pool_group=pool_group, pool_scale=pool_scale, ssd_in=ssd_in,
                   ssd_conv_w=ssd_conv_w, ssd_conv_b=ssd_conv_b, ssd_dt_bias=ssd_dt_bias, ssd_a_log=ssd_a_log,
                   ssd_d=ssd_d, ssd_out_norm=ssd_out_norm, ssd_out=ssd_out, sb_qkv=sb_qkv, sb_q_norm=sb_q_norm,
                   sb_k_norm=sb_k_norm, sb_out=sb_out, ffn_norm=ffn_norm, ffn_gate=ffn_gate, ffn_up=ffn_up,
                   ffn_down=ffn_down)
    mom1 = dict(mix_norm=m_mix_norm, pool_in=m_pool_in, pool_group=m_pool_group, pool_scale=m_pool_scale,
                ssd_in=m_ssd_in, ssd_conv_w=m_ssd_conv_w, ssd_conv_b=m_ssd_conv_b, ssd_dt_bias=m_ssd_dt_bias,
                ssd_a_log=m_ssd_a_log, ssd_d=m_ssd_d, ssd_out_norm=m_ssd_out_norm, ssd_out=m_ssd_out,
                sb_qkv=m_sb_qkv, sb_q_norm=m_sb_q_norm, sb_k_norm=m_sb_k_norm, sb_out=m_sb_out,
                ffn_norm=m_ffn_norm, ffn_gate=m_ffn_gate, ffn_up=m_ffn_up, ffn_down=m_ffn_down)
    mom2 = dict(mix_norm=v_mix_norm, pool_in=v_pool_in, pool_group=v_pool_group, pool_scale=v_pool_scale,
                ssd_in=v_ssd_in, ssd_conv_w=v_ssd_conv_w, ssd_conv_b=v_ssd_conv_b, ssd_dt_bias=v_ssd_dt_bias,
                ssd_a_log=v_ssd_a_log, ssd_d=v_ssd_d, ssd_out_norm=v_ssd_out_norm, ssd_out=v_ssd_out,
                sb_qkv=v_sb_qkv, sb_q_norm=v_sb_q_norm, sb_k_norm=v_sb_k_norm, sb_out=v_sb_out,
                ffn_norm=v_ffn_norm, ffn_gate=v_ffn_gate, ffn_up=v_ffn_up, ffn_down=v_ffn_down)
    names = list(weights)
    depth, d = mix_norm.shape
    xs, ys, cs = _position()
    dev = 4 * xs + 2 * ys + cs
    chip = 2 * xs + ys

    seg = {}
    row = 0
    for name in BIG_WEIGHTS:
        n_rows = weights[name].size // d
        seg[name] = (row, n_rows)
        row += -(-n_rows // SUBLANES) * SUBLANES
    big_rows = row
    n_scale, n_convw = pool_scale.size, ssd_conv_w.size
    exact = jnp.concatenate([_exact_bf16_rows(pool_scale, d), _exact_bf16_rows(ssd_conv_w, d)], axis=0)
    scale_rows = _exact_bf16_rows(pool_scale, d).shape[0]
    packed_rows = -(-(big_rows + exact.shape[0]) // ROW_PAD) * ROW_PAD

    def pack(tree, dtype):
        ends = [seg[n][0] for n in BIG_WEIGHTS[1:]] + [big_rows]
        return jnp.concatenate([_pad_rows(_to_rows(n, tree[n], d).astype(dtype), end - seg[n][0])
                                for n, end in zip(BIG_WEIGHTS, ends)], axis=0)

    w_wire = _pad_rows(jnp.concatenate([pack(weights, WIRE_DTYPE), exact], axis=0), packed_rows)
    gathered = _all_gather(w_wire, name="gather_weights", in_vmem=False)

    def seg_of(name):
        a, n = seg[name]
        return gathered[:, a:a + n]

    n_pool, n_ssd, n_sb = pool_in.shape[0], ssd_in.shape[0], sb_qkv.shape[0]
    assert n_ssd == 1 and n_sb == 1
    w_pool_in = seg_of("pool_in").reshape(N_DEV, n_pool, -1, d).transpose(1, 0, 2, 3).reshape(n_pool, d, d)
    grp = pool_group.shape
    w_pool_group = seg_of("pool_group").reshape(N_DEV, grp[0], grp[1], grp[2], grp[3]).transpose(1, 2, 0, 3, 4)
    w_pool_group = w_pool_group.reshape(grp[0], grp[1], grp[3], grp[3])
    w_ssd_in_t = seg_of("ssd_in").reshape(-1, d)
    w_ssd_out = seg_of("ssd_out").reshape(-1, d)
    w_sb_qkv_t = seg_of("sb_qkv").reshape(-1, d)
    w_sb_out = seg_of("sb_out").reshape(-1, d)
    hidden = ffn_down.shape[1] * N_DEV
    w_gate_t = seg_of("ffn_gate").reshape(N_DEV, depth, -1, d).transpose(1, 0, 2, 3).reshape(depth, hidden, d)
    w_up_t = seg_of("ffn_up").reshape(N_DEV, depth, -1, d).transpose(1, 0, 2, 3).reshape(depth, hidden, d)
    w_down = seg_of("ffn_down").reshape(N_DEV, depth, -1, d).transpose(1, 0, 2, 3).reshape(depth, hidden, d)
    exact_all = gathered[:, big_rows:big_rows + exact.shape[0]]
    scale_full = _exact_f32(exact_all[:, :scale_rows], n_scale).reshape(N_DEV, n_pool, -1)
    scale_full = scale_full.transpose(1, 0, 2).reshape(n_pool, d)
    convw_full = _exact_f32(exact_all[:, scale_rows:], n_convw).reshape(N_DEV, SSD_CONV, -1)
    convw_full = convw_full.transpose(1, 0, 2).reshape(SSD_CONV, -1)

    d_inner = w_ssd_out.shape[0]
    n_zx = w_ssd_in_t.shape[0] - ssd_dt_bias.shape[1]
    w_zx_t = w_ssd_in_t[:n_zx]
    n_ssd_heads = ssd_dt_bias.shape[1]
    n_ssd_groups = n_ssd_heads // SSD_HEADS_PER_GROUP
    w_dt_t = _ssd_group_pad(w_ssd_in_t[n_zx:].T, n_ssd_groups).T
    par = _pad_rows(_ssd_group_pad(jnp.concatenate([ssd_dt_bias, ssd_a_log, ssd_d], axis=0), n_ssd_groups), SUBLANES)
    d_rep = jnp.repeat(ssd_d[0], SSD_HEAD_DIM)[None]
    qk_gains = jnp.zeros((SUBLANES, LANES), F32).at[0].set(jnp.tile(sb_q_norm[0], 2)).at[1].set(jnp.tile(sb_k_norm[0], 2))

    act = x[0]
    saved = []
    for i in range(depth):
        kind, j = i % 3, i // 3
        gain = mix_norm[i:i + 1]
        if kind == 0:
            act, s = _pool_mixer_fwd(act, gain, w_pool_in[j], w_pool_group[j], scale_full[j:j + 1], f"l{i}")
        elif kind == 1:
            act, s = _ssd_mixer_fwd(act, gain, w_zx_t, w_dt_t, convw_full, ssd_conv_b, par, d_rep, ssd_out_norm,
                                    w_ssd_out, f"l{i}")
        else:
            act, s = _sb_mixer_fwd(act, gain, w_sb_qkv_t, qk_gains, w_sb_out, f"l{i}")
        act, f = _ffn_fwd(act, ffn_norm[i:i + 1], w_gate_t[i], w_up_t[i], w_down[i], f"l{i}")
        saved.append((s, f))
    dact, loss_cols = _loss_head(act, loss_target[0], name="loss_head")

    def layer_row(name, layer):
        return seg[name][0] + layer * (seg[name][1] // weights[name].shape[0])

    grads = jnp.zeros((N_DEV, packed_rows, d), F32)
    g_mix_norm, g_ffn_norm = [None] * depth, [None] * depth
    g_pool_group, g_pool_scale = [None] * n_pool, [None] * n_pool
    for i in reversed(range(depth)):
        kind, j = i % 3, i // 3
        gain = mix_norm[i:i + 1]
        s, f = saved[i]
        dact, grads, g_ffn_norm[i] = _ffn_bwd(
            dact, f, ffn_norm[i:i + 1], w_gate_t[i], w_up_t[i], w_down[i], grads,
            [layer_row(n, i) for n in ("ffn_gate", "ffn_up", "ffn_down")], f"l{i}")
        if kind == 0:
            dact, grads, g_pool_group[j], g_pool_scale[j], g_mix_norm[i] = _pool_mixer_bwd(
                dact, s, gain, w_pool_in[j], w_pool_group[j], scale_full[j:j + 1], grads, layer_row("pool_in", j),
                f"l{i}")
        elif kind == 1:
            (dact, grads, g_zx_t, g_dt_t, g_conv_w, g_conv_b, g_par, g_out_norm,
             g_mix_norm[i]) = _ssd_mixer_bwd(dact, s, gain, w_zx_t, w_dt_t, convw_full, ssd_conv_b, par, d_rep,
                                             ssd_out_norm, w_ssd_out, grads, layer_row("ssd_out", j), f"l{i}")
        else:
            dact, grads, g_qk_gains, g_mix_norm[i] = _sb_mixer_bwd(
                dact, s, gain, w_sb_qkv_t, qk_gains, w_sb_out, grads, layer_row("sb_qkv", j),
                layer_row("sb_out", j), f"l{i}")
    grad_x = dact[None]

    g_ssd_in = jnp.concatenate([g_zx_t, _ssd_group_unpad(g_dt_t.T, n_ssd_groups).T], axis=0)
    g_group = jnp.concatenate([_device_blocks(gg[k], d) for gg in g_pool_group for k in range(gg.shape[0])], axis=1)
    for name, blocks in (("ssd_in", _device_blocks(g_ssd_in, d)), ("pool_group", g_group)):
        grads = lax.dynamic_update_slice(grads, blocks, (0, seg[name][0], 0))
    parts = grads.reshape(N_DEV // 2, 2, packed_rows, d)
    from_sibling = _exchange_sibling(parts, name="reduce_sibling")
    chip_sums, chip_sums_wire = _add_pairs(parts, from_sibling, cs.reshape(1).astype(jnp.int32),
                                           name="reduce_sibling_add")
    from_chips = _exchange_chips(chip_sums_wire, name="reduce_chips")

    def pack_f32(tree):
        return _pad_rows(pack(tree, F32), packed_rows)

    big_out = _adamw_sharded(chip_sums, from_chips, chip.reshape(1).astype(jnp.int32), pack_f32(weights),
                             pack_f32(mom1), pack_f32(mom2), name="adamw_sharded")

    def small_pack(mix, ffn, conv_b, out_norm, scale, conv_w, vec, qk, loss=None):
        buf = jnp.zeros((SMALL_ROWS, d), F32)
        buf = buf.at[ROW_MIX_NORM:ROW_MIX_NORM + depth].set(mix).at[ROW_FFN_NORM:ROW_FFN_NORM + depth].set(ffn)
        buf = buf.at[ROW_CONV_B:ROW_CONV_B + conv_b.size // d].set(conv_b.reshape(-1, d))
        buf = buf.at[ROW_OUT_NORM:ROW_OUT_NORM + out_norm.size // d].set(out_norm.reshape(-1, d))
        buf = buf.at[ROW_POOL_SCALE:ROW_POOL_SCALE + n_pool].set(scale)
        buf = buf.at[ROW_CONV_W:ROW_CONV_W + conv_w.size // d].set(conv_w.reshape(-1, d))
        buf = buf.at[ROW_SSD_VEC].set(vec.reshape(-1)).at[ROW_QK_GAIN].set(qk.reshape(-1))
        if loss is not None:
            buf = buf.at[ROW_LOSS].set(loss.reshape(-1))
        return buf

    def small_params(tree):
        scale = lax.dynamic_update_slice(jnp.zeros((n_pool, d), F32), tree["pool_scale"],
                                         (0, dev * tree["pool_scale"].shape[1]))
        conv_w = lax.dynamic_update_slice(jnp.zeros(convw_full.shape, F32), tree["ssd_conv_w"][0],
                                          (0, dev * tree["ssd_conv_w"].shape[2]))
        vec = jnp.zeros((SUBLANES, LANES), F32)
        vec = vec.at[0, :n_ssd_heads].set(tree["ssd_dt_bias"][0]).at[1, :n_ssd_heads].set(tree["ssd_a_log"][0])
        vec = vec.at[2, :n_ssd_heads].set(tree["ssd_d"][0])
        qk = jnp.zeros((SUBLANES, LANES), F32)
        qk = qk.at[0, SB_HEAD_DIM:].set(tree["sb_q_norm"][0]).at[1, SB_HEAD_DIM:].set(tree["sb_k_norm"][0])
        return small_pack(tree["mix_norm"], tree["ffn_norm"], tree["ssd_conv_b"], tree["ssd_out_norm"], scale,
                          conv_w, vec, qk)

    small_partial = small_pack(jnp.concatenate(g_mix_norm, axis=0), jnp.concatenate(g_ffn_norm, axis=0), g_conv_b,
                               g_out_norm, jnp.concatenate(g_pool_scale, axis=0), g_conv_w,
                               jnp.zeros((SUBLANES, LANES), F32).at[:3, :n_ssd_heads].set(
                                   _ssd_group_unpad(g_par[:3], n_ssd_groups)), g_qk_gains,
                               loss_cols)
    small_all = _all_gather(small_partial, name="gather_small", in_vmem=True)
    small_out = _adamw_small(small_all, small_params(weights), small_params(mom1), small_params(mom2),
                             name="adamw_small")
    loss = small_out[0][ROW_LOSS, 0]

    def unpack(big, small):
        out = {}
        for name in BIG_WEIGHTS:
            a, n = seg[name]
            out[name] = _from_rows(name, big[a:a + n], weights[name].shape)
        out["mix_norm"] = small[ROW_MIX_NORM:ROW_MIX_NORM + depth]
        out["ffn_norm"] = small[ROW_FFN_NORM:ROW_FFN_NORM + depth]
        out["ssd_conv_b"] = small[ROW_CONV_B:ROW_CONV_B + ssd_conv_b.size // d].reshape(ssd_conv_b.shape)
        out["ssd_out_norm"] = small[ROW_OUT_NORM:ROW_OUT_NORM + ssd_out_norm.size // d].reshape(ssd_out_norm.shape)
        out["pool_scale"] = lax.dynamic_slice(small[ROW_POOL_SCALE:ROW_POOL_SCALE + n_pool],
                                              (0, dev * pool_scale.shape[1]), pool_scale.shape)
        conv_w = small[ROW_CONV_W:ROW_CONV_W + convw_full.size // d].reshape(convw_full.shape)
        out["ssd_conv_w"] = lax.dynamic_slice(conv_w, (0, dev * ssd_conv_w.shape[2]), ssd_conv_w.shape[1:])[None]
        vec = small[ROW_SSD_VEC].reshape(SUBLANES, LANES)
        out["ssd_dt_bias"], out["ssd_a_log"], out["ssd_d"] = (vec[r:r + 1, :n_ssd_heads] for r in range(3))
        qk = small[ROW_QK_GAIN].reshape(SUBLANES, LANES)
        out["sb_q_norm"], out["sb_k_norm"] = qk[0:1, SB_HEAD_DIM:], qk[1:2, SB_HEAD_DIM:]
        return [out[n] for n in names]

    results = [unpack(b, s) for b, s in zip(big_out, small_out)]
    return (loss, grad_x, *results[0], *results[1], *results[2], *results[3])
```

```python
import math

import jax
import jax.numpy as jnp
from jax import lax
from jax.experimental import pallas as pl
from jax.experimental.pallas import tpu as pltpu

F32 = jnp.float32
BF16 = jnp.bfloat16

N_DEV = 8
NORM_EPS = 1e-6
V7X_VMEM_LIMIT_BYTES = 48 * 1024 * 1024
LANES = 128
SUBLANES = 8

POOL_WINDOWS = (2, 4, 8, 16)
SSD_CHUNK = 256
SSD_HEAD_DIM = 64
SSD_STATE = 128
SSD_HEADS_PER_GROUP = 4
SSD_CONV = 4
SB_HEAD_DIM = 64
SB_BLOCK = 256
SB_QUERY_BLOCK = 256

ADAM_LR = 0.001
ADAM_B1 = 0.9
ADAM_B2 = 0.999
ADAM_EPS = 1e-08
ADAM_WD = 0.01
ADAM_STEP = 10


def _params(*sem):
    return pltpu.CompilerParams(dimension_semantics=sem, vmem_limit_bytes=V7X_VMEM_LIMIT_BYTES)


def _tile(n, cap, mult):
    best = None
    for t in range(mult, min(n, cap) + 1, mult):
        if n % t == 0:
            best = t
    return best or n


def _load_slabs(ref, slabs):
    if not slabs:
        return ref[...]
    return jnp.concatenate([ref[p] for p in range(ref.shape[0])], axis=1)


def _matmul(a, b, mode, *, name, out_dtype=F32, resid=None, a_slabs=False, out_slabs=False,
            tm_cap=1024, tn_cap=1024, tk_cap=2048):
    pairs = list(zip(a, b)) if isinstance(a, (list, tuple)) else [(a, b)]
    a, b = pairs[0]
    if a_slabs:
        m, k = a.shape[1], a.shape[0] * LANES
    else:
        m, k = a.shape
    n = b.shape[1] if mode == "nn" else b.shape[0]
    assert (b.shape[0] if mode == "nn" else b.shape[1]) == k
    assert all(pa.shape == a.shape and pb.shape == b.shape for pa, pb in pairs)
    tm, tn, tk = _tile(m, tm_cap, SUBLANES), _tile(n, tn_cap, LANES), _tile(k, tk_cap, LANES)
    nk = k // tk
    dn = (((1,), (0,)), ((), ())) if mode == "nn" else (((1,), (1,)), ((), ()))
    has_resid = resid is not None
    n_pairs = len(pairs)

    def body(*refs):
        ab_refs, rest = refs[:2 * n_pairs], refs[2 * n_pairs:]
        r_ref = rest[0] if has_resid else None
        o_ref = rest[1] if has_resid else rest[0]
        kk = pl.program_id(2)

        def partial():
            total = None
            for p in range(n_pairs):
                d = lax.dot_general(_load_slabs(ab_refs[2 * p], a_slabs).astype(BF16),
                                    ab_refs[2 * p + 1][...].astype(BF16), dn, preferred_element_type=F32)
                total = d if total is None else total + d
            return total

        def finish(r):
            if has_resid:
                r = r + r_ref[...]
            if out_slabs:
                for p in range(tn // LANES):
                    o_ref[p] = r[:, p * LANES:(p + 1) * LANES].astype(out_dtype)
            else:
                o_ref[...] = r.astype(out_dtype)

        if nk == 1:
            finish(partial())
        else:
            acc = rest[-1]

            @pl.when(kk == 0)
            def _():
                acc[...] = jnp.zeros_like(acc)

            acc[...] += partial()

            @pl.when(kk == nk - 1)
            def _():
                finish(acc[...])

    b_spec = (pl.BlockSpec((tk, tn), lambda i, j, kk: (kk, j)) if mode == "nn"
              else pl.BlockSpec((tn, tk), lambda i, j, kk: (j, kk)))
    a_spec = (pl.BlockSpec((tk // LANES, tm, LANES), lambda i, j, kk: (kk, i, 0)) if a_slabs
              else pl.BlockSpec((tm, tk), lambda i, j, kk: (i, kk)))
    in_specs = [a_spec, b_spec] * n_pairs
    args = [t for pair in pairs for t in pair]
    if has_resid:
        in_specs.append(pl.BlockSpec((tm, tn), lambda i, j, kk: (i, j)))
        args.append(resid)
    if out_slabs:
        out_spec = pl.BlockSpec((tn // LANES, tm, LANES), lambda i, j, kk: (j, i, 0))
        out_shape = jax.ShapeDtypeStruct((n // LANES, m, LANES), out_dtype)
    else:
        out_spec = pl.BlockSpec((tm, tn), lambda i, j, kk: (i, j))
        out_shape = jax.ShapeDtypeStruct((m, n), out_dtype)
    return pl.pallas_call(
        body, name=name, grid=(m // tm, n // tn, nk),
        in_specs=in_specs, out_specs=out_spec, out_shape=out_shape,
        scratch_shapes=[pltpu.VMEM((tm, tn), F32)] if nk > 1 else [],
        compiler_params=_params("parallel", "parallel", "arbitrary"),
    )(*args)


def _matmul_tn(a, b, *, name, a_slabs=False, ta_cap=1024, tb_cap=1024, tr_cap=512):
    if a_slabs:
        r, ka = a.shape[1], a.shape[0] * LANES
    else:
        r, ka = a.shape
    nb = b.shape[1]
    assert b.shape[0] == r
    ta, tb, tr = _tile(ka, ta_cap, LANES), _tile(nb, tb_cap, LANES), _tile(r, tr_cap, SUBLANES)

    def body(a_ref, b_ref, o_ref):
        @pl.when(pl.program_id(2) == 0)
        def _():
            o_ref[...] = jnp.zeros_like(o_ref)

        o_ref[...] += lax.dot_general(_load_slabs(a_ref, a_slabs).astype(BF16), b_ref[...].astype(BF16),
                                      (((0,), (0,)), ((), ())), preferred_element_type=F32)

    a_spec = (pl.BlockSpec((ta // LANES, tr, LANES), lambda i, j, kk: (i, kk, 0)) if a_slabs
              else pl.BlockSpec((tr, ta), lambda i, j, kk: (kk, i)))
    return pl.pallas_call(
        body, name=name, grid=(ka // ta, nb // tb, r // tr),
        in_specs=[a_spec, pl.BlockSpec((tr, tb), lambda i, j, kk: (kk, j))],
        out_specs=pl.BlockSpec((ta, tb), lambda i, j, kk: (i, j)),
        out_shape=jax.ShapeDtypeStruct((ka, nb), F32),
        compiler_params=_params("parallel", "parallel", "arbitrary"),
    )(a, b)


def _matmul_tn_into(buf, a, b, row_off, *, name, a_slabs=False, tr_cap=512):
    if a_slabs:
        r, ka = a.shape[1], a.shape[0] * LANES
    else:
        r, ka = a.shape
    n_dev, _, n = buf.shape
    per = ka // n_dev
    assert b.shape == (r, n) and ka % n_dev == 0 and per % SUBLANES == 0 and row_off % per == 0
    tr = _tile(r, tr_cap, SUBLANES)

    def body(buf_ref, a_ref, b_ref, o_ref):
        prod = lax.dot_general(_load_slabs(a_ref, a_slabs).astype(BF16), b_ref[...].astype(BF16),
                               (((0,), (0,)), ((), ())), preferred_element_type=F32)
        @pl.when(pl.program_id(0) == 0)
        def _():
            for k in range(n_dev):
                o_ref[k] = prod[k * per:(k + 1) * per]

        @pl.when(pl.program_id(0) > 0)
        def _():
            for k in range(n_dev):
                o_ref[k] += prod[k * per:(k + 1) * per]

    a_spec = (pl.BlockSpec((ka // LANES, tr, LANES), lambda i: (0, i, 0)) if a_slabs
              else pl.BlockSpec((tr, ka), lambda i: (i, 0)))
    return pl.pallas_call(
        body, name=name, grid=(r // tr,),
        in_specs=[pl.BlockSpec(memory_space=pl.ANY), a_spec, pl.BlockSpec((tr, n), lambda i: (i, 0))],
        out_specs=pl.BlockSpec((n_dev, per, n), lambda i: (0, row_off // per, 0)),
        out_shape=jax.ShapeDtypeStruct(buf.shape, F32),
        input_output_aliases={0: 0},
        compiler_params=_params("arbitrary"),
    )(buf, a, b)


def _rms_fwd(x, gain, *, name):
    t, d = x.shape
    tm = _tile(t, 512, SUBLANES)

    def body(x_ref, g_ref, o_ref):
        xv = x_ref[...]
        r = lax.rsqrt(jnp.mean(xv * xv, axis=-1, keepdims=True) + NORM_EPS)
        o_ref[...] = (xv * r * g_ref[...]).astype(BF16)

    return pl.pallas_call(
        body, name=name, grid=(t // tm,),
        in_specs=[pl.BlockSpec((tm, d), lambda i: (i, 0)), pl.BlockSpec((1, d), lambda i: (0, 0))],
        out_specs=pl.BlockSpec((tm, d), lambda i: (i, 0)),
        out_shape=jax.ShapeDtypeStruct((t, d), BF16),
        compiler_params=_params("parallel"),
    )(x, gain)


def _rms_bwd(x, gain, dh, dres, *, name):
    t, d = x.shape
    tm = _tile(t, 512, SUBLANES)

    def body(x_ref, g_ref, dh_ref, dres_ref, dx_ref, dg_ref):
        @pl.when(pl.program_id(0) == 0)
        def _():
            dg_ref[...] = jnp.zeros_like(dg_ref)

        xv = x_ref[...]
        r = lax.rsqrt(jnp.mean(xv * xv, axis=-1, keepdims=True) + NORM_EPS)
        xhat = xv * r
        dhv = dh_ref[...]
        u = dhv * g_ref[...]
        dx_ref[...] = dres_ref[...] + r * (u - xhat * jnp.mean(u * xhat, axis=-1, keepdims=True))
        dg_ref[...] += jnp.sum(dhv * xhat, axis=0, keepdims=True)

    return pl.pallas_call(
        body, name=name, grid=(t // tm,),
        in_specs=[pl.BlockSpec((tm, d), lambda i: (i, 0)), pl.BlockSpec((1, d), lambda i: (0, 0)),
                  pl.BlockSpec((tm, d), lambda i: (i, 0)), pl.BlockSpec((tm, d), lambda i: (i, 0))],
        out_specs=[pl.BlockSpec((tm, d), lambda i: (i, 0)), pl.BlockSpec((1, d), lambda i: (0, 0))],
        out_shape=[jax.ShapeDtypeStruct((t, d), F32), jax.ShapeDtypeStruct((1, d), F32)],
        compiler_params=_params("arbitrary"),
    )(x, gain, dh, dres)


def _loss_head(y, target, *, name):
    t, d = y.shape
    tm = _tile(t, 512, SUBLANES)

    def body(y_ref, t_ref, dy_ref, l_ref):
        @pl.when(pl.program_id(0) == 0)
        def _():
            l_ref[...] = jnp.zeros_like(l_ref)

        e = y_ref[...] - t_ref[...]
        dy_ref[...] = e * (1.0 / d)
        l_ref[...] += jnp.sum(e * e, axis=0, keepdims=True) * (0.5 / d)

    return pl.pallas_call(
        body, name=name, grid=(t // tm,),
        in_specs=[pl.BlockSpec((tm, d), lambda i: (i, 0)), pl.BlockSpec((tm, d), lambda i: (i, 0))],
        out_specs=[pl.BlockSpec((tm, d), lambda i: (i, 0)), pl.BlockSpec((1, d), lambda i: (0, 0))],
        out_shape=[jax.ShapeDtypeStruct((t, d), F32), jax.ShapeDtypeStruct((1, d), F32)],
        compiler_params=_params("arbitrary"),
    )(y, target)


def _sigmoid(v):
    return 0.5 * jnp.tanh(0.5 * v) + 0.5


FFN_TOKEN_TILE = 512
FFN_HIDDEN_TILE = 1408
NT_DIMS = (((1,), (1,)), ((), ()))


def _ffn_up(h, w_gate_t, w_up_t, *, name):
    t, d = h.shape
    f = w_gate_t.shape[0]
    tm, tn = _tile(t, FFN_TOKEN_TILE, SUBLANES), _tile(f, FFN_HIDDEN_TILE, LANES)

    def body(h_ref, g_ref, u_ref, s_ref, a_ref, b_ref):
        hv = h_ref[...].astype(BF16)
        av = lax.dot_general(hv, g_ref[...].astype(BF16), NT_DIMS, preferred_element_type=F32)
        bv = lax.dot_general(hv, u_ref[...].astype(BF16), NT_DIMS, preferred_element_type=F32)
        s_ref[...] = (av * _sigmoid(av) * bv).astype(BF16)
        a_ref[...] = av.astype(BF16)
        b_ref[...] = bv.astype(BF16)

    w_spec = pl.BlockSpec((tn, d), lambda j, i: (j, 0))
    out_spec = pl.BlockSpec((tm, tn), lambda j, i: (i, j))
    out = jax.ShapeDtypeStruct((t, f), BF16)
    return pl.pallas_call(
        body, name=name, grid=(f // tn, t // tm),
        in_specs=[pl.BlockSpec((tm, d), lambda j, i: (i, 0)), w_spec, w_spec],
        out_specs=[out_spec, out_spec, out_spec], out_shape=[out, out, out],
        compiler_params=_params("parallel", "parallel"),
    )(h, w_gate_t, w_up_t)


def _ffn_dact(dx, w_down, a, b, *, name):
    t, d = dx.shape
    f = w_down.shape[0]
    tm, tn = _tile(t, FFN_TOKEN_TILE, SUBLANES), _tile(f, FFN_HIDDEN_TILE, LANES)

    def body(dx_ref, w_ref, a_ref, b_ref, da_ref, db_ref):
        ds = lax.dot_general(dx_ref[...].astype(BF16), w_ref[...].astype(BF16), NT_DIMS, preferred_element_type=F32)
        av = a_ref[...].astype(F32)
        sg = _sigmoid(av)
        da_ref[...] = (ds * b_ref[...].astype(F32) * (sg * (1.0 + av * (1.0 - sg)))).astype(BF16)
        db_ref[...] = (ds * av * sg).astype(BF16)

    blk = pl.BlockSpec((tm, tn), lambda j, i: (i, j))
    out = jax.ShapeDtypeStruct((t, f), BF16)
    return pl.pallas_call(
        body, name=name, grid=(f // tn, t // tm),
        in_specs=[pl.BlockSpec((tm, d), lambda j, i: (i, 0)), pl.BlockSpec((tn, d), lambda j, i: (j, 0)), blk, blk],
        out_specs=[blk, blk], out_shape=[out, out],
        compiler_params=_params("parallel", "parallel"),
    )(dx, w_down, a, b)


def _ffn_fwd(x, gain, w_gate_t, w_up_t, w_down, tag):
    h = _rms_fwd(x, gain, name=f"ffn_norm_{tag}")
    s, a, b = _ffn_up(h, w_gate_t, w_up_t, name=f"ffn_up_{tag}")
    x_new = _matmul(s, w_down, "nn", resid=x, tn_cap=1024, tk_cap=2816, name=f"ffn_down_{tag}")
    return x_new, (x, h, a, b, s)


def _ffn_bwd(dx, saved, gain, w_gate_t, w_up_t, w_down, grads, rows, tag):
    x, h, a, b, s = saved
    da, db = _ffn_dact(dx, w_down, a, b, name=f"ffn_dact_{tag}")
    grads = _matmul_tn_into(grads, da, h, rows[0], name=f"ffn_dwgate_{tag}")
    grads = _matmul_tn_into(grads, db, h, rows[1], name=f"ffn_dwup_{tag}")
    grads = _matmul_tn_into(grads, s, dx, rows[2], name=f"ffn_dwdown_{tag}")
    dh = _matmul([da, db], [w_gate_t, w_up_t], "nn", tm_cap=512, tn_cap=1024, tk_cap=2816, name=f"ffn_dh_{tag}")
    dx_in, dgain = _rms_bwd(x, gain, dh, dx, name=f"ffn_dnorm_{tag}")
    return dx_in, grads, dgain


POOL_HALO = 16


def _shift_rows(v, k):
    n = v.shape[0]
    return pltpu.roll(v, k % n, 0)


def _window_sum(v, w, direction):
    k = 1
    while k < w:
        v = v + _shift_rows(v, direction * k)
        k *= 2
    return v


def _pool_fwd(u, x, w_group, scale, *, name):
    t, d = u.shape
    ng, dg = w_group.shape[0], w_group.shape[1]
    tm = _tile(t, 512, POOL_HALO)
    hb = tm // POOL_HALO

    def body(u_ref, halo_ref, x_ref, w_ref, s_ref, xo_ref, p_ref, y_ref):
        i, g = pl.program_id(0), pl.program_id(1)
        halo = jnp.where(i > 0, halo_ref[...], 0.0)
        ext = jnp.concatenate([halo, u_ref[...]], axis=0)
        pos = i * tm + lax.broadcasted_iota(jnp.int32, (tm, 1), 0)
        for gi, win in enumerate(POOL_WINDOWS):
            @pl.when(g == gi)
            def _(win=win):
                tot = _window_sum(ext, win, 1)[POOL_HALO:]
                cnt = jnp.minimum(pos + 1, win).astype(F32)
                p = (tot / cnt - u_ref[...]).astype(BF16)
                p_ref[...] = p
                y = jnp.dot(p, w_ref[...].astype(BF16), preferred_element_type=F32)
                y_ref[...] = y
                xo_ref[...] = x_ref[...] + y * s_ref[...]

    blk = pl.BlockSpec((tm, dg), lambda i, g: (i, g))
    return pl.pallas_call(
        body, name=name, grid=(t // tm, ng),
        in_specs=[blk, pl.BlockSpec((POOL_HALO, dg), lambda i, g: (jnp.maximum(i * hb - 1, 0), g)), blk,
                  pl.BlockSpec((None, dg, dg), lambda i, g: (g, 0, 0)), pl.BlockSpec((1, dg), lambda i, g: (0, g))],
        out_specs=[blk, blk, blk],
        out_shape=[jax.ShapeDtypeStruct((t, d), F32), jax.ShapeDtypeStruct((t, d), BF16),
                   jax.ShapeDtypeStruct((t, d), F32)],
        compiler_params=_params("parallel", "parallel"),
    )(u, u, x, w_group, scale)


def _pool_bwd(dx, p, y_pre, w_group, scale, *, name):
    t, d = dx.shape
    ng, dg = w_group.shape[0], w_group.shape[1]
    tm = _tile(t, 512, POOL_HALO)
    hb = tm // POOL_HALO
    nt = t // tm

    def body(dx_ref, nxt_ref, p_ref, y_ref, w_ref, s_ref, du_ref, dw_ref, ds_ref):
        g, i = pl.program_id(0), pl.program_id(1)

        @pl.when(i == 0)
        def _():
            dw_ref[...] = jnp.zeros_like(dw_ref)
            ds_ref[...] = jnp.zeros_like(ds_ref)

        dxv = dx_ref[...]
        ds_ref[...] += jnp.sum(dxv * y_ref[...], axis=0, keepdims=True)
        nxt = jnp.where(i < nt - 1, nxt_ref[...], 0.0)
        dyp = (jnp.concatenate([dxv, nxt], axis=0) * s_ref[...]).astype(BF16)
        dw_ref[...] += lax.dot_general(p_ref[...], dyp[:tm], (((0,), (0,)), ((), ())), preferred_element_type=F32)
        dp = lax.dot_general(dyp, w_ref[...].astype(BF16), (((1,), (1,)), ((), ())), preferred_element_type=F32)
        pos = i * tm + lax.broadcasted_iota(jnp.int32, (tm + POOL_HALO, 1), 0)
        for gi, win in enumerate(POOL_WINDOWS):
            @pl.when(g == gi)
            def _(win=win):
                q = dp / jnp.minimum(pos + 1, win).astype(F32)
                du_ref[...] = (_window_sum(q, win, -1)[:tm] - dp[:tm]).astype(BF16)

    blk = pl.BlockSpec((tm, dg), lambda g, i: (i, g))
    return pl.pallas_call(
        body, name=name, grid=(ng, nt),
        in_specs=[blk, pl.BlockSpec((POOL_HALO, dg), lambda g, i: (jnp.minimum((i + 1) * hb, t // POOL_HALO - 1), g)),
                  blk, blk, pl.BlockSpec((None, dg, dg), lambda g, i: (g, 0, 0)),
                  pl.BlockSpec((1, dg), lambda g, i: (0, g))],
        out_specs=[blk, pl.BlockSpec((None, dg, dg), lambda g, i: (g, 0, 0)), pl.BlockSpec((1, dg), lambda g, i: (0, g))],
        out_shape=[jax.ShapeDtypeStruct((t, d), BF16), jax.ShapeDtypeStruct((ng, dg, dg), F32),
                   jax.ShapeDtypeStruct((1, d), F32)],
        compiler_params=_params("parallel", "arbitrary"),
    )(dx, dx, p, y_pre, w_group, scale)


def _pool_mixer_fwd(x, gain, w_in, w_group, scale, tag):
    h = _rms_fwd(x, gain, name=f"pool_norm_{tag}")
    u = _matmul(h, w_in, "nn", name=f"pool_in_{tag}")
    x_new, p, y_pre = _pool_fwd(u, x, w_group, scale, name=f"pool_mix_{tag}")
    return x_new, (x, h, p, y_pre)


def _pool_mixer_bwd(dx, saved, gain, w_in, w_group, scale, grads, row_in, tag):
    x, h, p, y_pre = saved
    du, dw_group, dscale = _pool_bwd(dx, p, y_pre, w_group, scale, name=f"pool_dmix_{tag}")
    grads = _matmul_tn_into(grads, h, du, row_in, name=f"pool_dwin_{tag}")
    dh = _matmul(du, w_in, "nt", name=f"pool_dh_{tag}")
    dx_in, dgain = _rms_bwd(x, gain, dh, dx, name=f"pool_dnorm_{tag}")
    return dx_in, grads, dw_group, dscale, dgain


CONV_HALO = 8
HIGHEST = lax.Precision.HIGHEST
NEG_BIG = -1e30


def _softplus(v):
    return jnp.maximum(v, 0.0) + jnp.log(1.0 + jnp.exp(-jnp.abs(v)))


def _dot_exact(a, b):
    return jnp.dot(a, b, precision=HIGHEST, preferred_element_type=F32)


def _conv_taps(ext, w_ref, off, rows):
    acc = None
    for k in range(SSD_CONV):
        shift = SSD_CONV - 1 - k
        v = (_shift_rows(ext, shift) if shift else ext)[off:off + rows] * w_ref[k:k + 1, :]
        acc = v if acc is None else acc + v
    return acc


def _ssd_conv_fwd(zx, conv_w, conv_b, col0, *, name):
    t = zx.shape[0]
    c = conv_w.shape[1]
    tm, tc = _tile(t, 512, CONV_HALO), _tile(c, 512, LANES)
    hb, cb0 = tm // CONV_HALO, col0 // tc
    assert col0 % tc == 0

    def body(x_ref, halo_ref, w_ref, b_ref, o_ref):
        halo = jnp.where(pl.program_id(0) > 0, halo_ref[...], 0.0)
        ext = jnp.concatenate([halo, x_ref[...]], axis=0)
        pre = _conv_taps(ext, w_ref, CONV_HALO, tm) + b_ref[...]
        o_ref[...] = pre * _sigmoid(pre)

    return pl.pallas_call(
        body, name=name, grid=(t // tm, c // tc),
        in_specs=[pl.BlockSpec((tm, tc), lambda i, j: (i, j + cb0)),
                  pl.BlockSpec((CONV_HALO, tc), lambda i, j: (jnp.maximum(i * hb - 1, 0), j + cb0)),
                  pl.BlockSpec((SSD_CONV, tc), lambda i, j: (0, j)), pl.BlockSpec((1, tc), lambda i, j: (0, j))],
        out_specs=pl.BlockSpec((tm, tc), lambda i, j: (i, j)),
        out_shape=jax.ShapeDtypeStruct((t, c), F32),
        compiler_params=_params("parallel", "parallel"),
    )(zx, zx, conv_w, conv_b)


def _ssd_conv_bwd(dxa, zx, conv_w, conv_b, col0, *, name):
    t = zx.shape[0]
    c = conv_w.shape[1]
    tm, tc = _tile(t, 512, CONV_HALO), _tile(c, 512, LANES)
    hb, cb0, nt = tm // CONV_HALO, col0 // tc, t // tm
    last_halo = t // CONV_HALO - 1

    def body(x_ref, prev_ref, nxt_ref, d_ref, dnxt_ref, w_ref, b_ref, dx_ref, dw_ref, db_ref):
        i = pl.program_id(1)

        @pl.when(i == 0)
        def _():
            dw_ref[...] = jnp.zeros_like(dw_ref)
            db_ref[...] = jnp.zeros_like(db_ref)

        prev = jnp.where(i > 0, prev_ref[...], 0.0)
        has_next = i < nt - 1
        ext = jnp.concatenate([prev, x_ref[...], jnp.where(has_next, nxt_ref[...], 0.0)], axis=0)
        pre = _conv_taps(ext, w_ref, CONV_HALO, tm + CONV_HALO) + b_ref[...]
        sg = _sigmoid(pre)
        dact = jnp.concatenate([d_ref[...], jnp.where(has_next, dnxt_ref[...], 0.0)], axis=0)
        dpre = dact * (sg * (1.0 + pre * (1.0 - sg)))
        db_ref[...] += jnp.sum(dpre[:tm], axis=0, keepdims=True)
        acc = None
        for k in range(SSD_CONV):
            shift = SSD_CONV - 1 - k
            src = (_shift_rows(ext, shift) if shift else ext)[CONV_HALO:CONV_HALO + tm]
            dw_ref[k:k + 1, :] += jnp.sum(dpre[:tm] * src, axis=0, keepdims=True)
            v = (_shift_rows(dpre, -shift) if shift else dpre)[:tm] * w_ref[k:k + 1, :]
            acc = v if acc is None else acc + v
        dx_ref[...] = acc.astype(BF16)

    main = lambda j, i: (i, j + cb0)
    return pl.pallas_call(
        body, name=name, grid=(c // tc, nt),
        in_specs=[pl.BlockSpec((tm, tc), main),
                  pl.BlockSpec((CONV_HALO, tc), lambda j, i: (jnp.maximum(i * hb - 1, 0), j + cb0)),
                  pl.BlockSpec((CONV_HALO, tc), lambda j, i: (jnp.minimum((i + 1) * hb, last_halo), j + cb0)),
                  pl.BlockSpec((tm, tc), lambda j, i: (i, j)),
                  pl.BlockSpec((CONV_HALO, tc), lambda j, i: (jnp.minimum((i + 1) * hb, last_halo), j)),
                  pl.BlockSpec((SSD_CONV, tc), lambda j, i: (0, j)), pl.BlockSpec((1, tc), lambda j, i: (0, j))],
        out_specs=[pl.BlockSpec((tm, tc), lambda j, i: (i, j)), pl.BlockSpec((SSD_CONV, tc), lambda j, i: (0, j)),
                   pl.BlockSpec((1, tc), lambda j, i: (0, j))],
        out_shape=[jax.ShapeDtypeStruct((t, c), BF16), jax.ShapeDtypeStruct((SSD_CONV, c), F32),
                   jax.ShapeDtypeStruct((1, c), F32)],
        compiler_params=_params("parallel", "arbitrary"),
    )(zx, zx, zx, dxa, dxa, conv_w, conv_b)


SSD_CUMSUM_PIECES = 3
SSD_GROUPS_PER_STEP = 1


def _ssd_group_pad(v, n_groups):
    lead = v.shape[:-1]
    v = v.reshape(*lead, n_groups, SSD_HEADS_PER_GROUP)
    v = jnp.pad(v, [(0, 0)] * (len(lead) + 1) + [(0, LANES - SSD_HEADS_PER_GROUP)])
    return v.reshape(*lead, n_groups * LANES)


def _ssd_group_unpad(v, n_groups):
    lead = v.shape[:-1]
    return v.reshape(*lead, n_groups, LANES)[..., :SSD_HEADS_PER_GROUP].reshape(*lead, -1)


def _ssd_chunk_common(dtp_ref, par_ref):
    ell = SSD_CHUNK
    dt = _softplus(dtp_ref[...] + par_ref[0:1, :])
    a = -jnp.exp(par_ref[1:2, :])
    row = lax.broadcasted_iota(jnp.int32, (ell, ell), 0)
    col = lax.broadcasted_iota(jnp.int32, (ell, ell), 1)
    acum = _split_dot(dt * a, (row >= col).astype(BF16), SSD_CUMSUM_PIECES, left=True)
    return dt, a, acum, acum.T, row, col


def _ssd_scan_fwd(xa, dtp, par, n_groups, *, name):
    t = xa.shape[0]
    ell, hd, hpg, ns, gps = SSD_CHUNK, SSD_HEAD_DIM, SSD_HEADS_PER_GROUP, SSD_STATE, SSD_GROUPS_PER_STEP
    gw = hpg * hd
    nc = t // ell
    b_blk0, c_blk0 = n_groups * gw // (ns * gps), (n_groups * gw // ns + n_groups) // gps

    def body(xs_ref, b_ref, c_ref, dtp_ref, par_ref, y_ref, sin_ref, st):
        @pl.when(pl.program_id(1) == 0)
        def _():
            st[...] = jnp.zeros_like(st)

        dt, _, acum, acum_t, row, col = _ssd_chunk_common(dtp_ref, par_ref)
        for gi in range(gps):
            bb = b_ref[:, gi * ns:(gi + 1) * ns].astype(BF16)
            cc = c_ref[:, gi * ns:(gi + 1) * ns].astype(BF16)
            cb = lax.dot_general(cc, bb, NT_DIMS, preferred_element_type=F32)
            for hh in range(hpg):
                lanes = slice(gi * gw + hh * hd, gi * gw + (hh + 1) * hd)
                hl = gi * LANES + hh
                col_a, row_a = acum[:, hl:hl + 1], acum_t[hl:hl + 1, :]
                decay = jnp.exp(jnp.where(row >= col, col_a - row_a, NEG_BIG))
                xdt = xs_ref[:, lanes] * dt[:, hl:hl + 1]
                s_h = st[gi * hpg + hh]
                sin_ref[gi, hh * hd:(hh + 1) * hd, :] = s_h
                y = jnp.dot((cb * decay).astype(BF16), xdt.astype(BF16), preferred_element_type=F32)
                y += jnp.exp(col_a) * lax.dot_general(cc, s_h.astype(BF16), NT_DIMS, preferred_element_type=F32)
                y_ref[:, lanes] = y
                a_last = acum[ell - 1:ell, hl:hl + 1]
                w = jnp.exp(a_last - col_a)
                st[gi * hpg + hh] = jnp.exp(a_last) * s_h + lax.dot_general(
                    (xdt * w).astype(BF16), bb, (((0,), (0,)), ((), ())), preferred_element_type=F32)

    return pl.pallas_call(
        body, name=name, grid=(n_groups // gps, nc),
        in_specs=[pl.BlockSpec((ell, gps * gw), lambda g, c: (c, g)),
                  pl.BlockSpec((ell, gps * ns), lambda g, c: (c, b_blk0 + g)),
                  pl.BlockSpec((ell, gps * ns), lambda g, c: (c, c_blk0 + g)),
                  pl.BlockSpec((ell, gps * LANES), lambda g, c: (c, g)),
                  pl.BlockSpec((SUBLANES, gps * LANES), lambda g, c: (0, g))],
        out_specs=[pl.BlockSpec((ell, gps * gw), lambda g, c: (c, g)),
                   pl.BlockSpec((None, gps, gw, ns), lambda g, c: (c, g, 0, 0))],
        out_shape=[jax.ShapeDtypeStruct((t, n_groups * gw), F32),
                   jax.ShapeDtypeStruct((nc, n_groups, gw, ns), F32)],
        scratch_shapes=[pltpu.VMEM((gps * hpg, hd, ns), F32)],
        compiler_params=_params("parallel", "arbitrary"),
    )(xa, xa, xa, dtp, par)


def _ssd_scan_bwd(dy, xa, dtp, par, s_in, n_groups, *, name):
    t = xa.shape[0]
    ell, hd, hpg, ns, gps = SSD_CHUNK, SSD_HEAD_DIM, SSD_HEADS_PER_GROUP, SSD_STATE, SSD_GROUPS_PER_STEP
    gw = hpg * hd
    nc = t // ell
    b_blk0, c_blk0 = n_groups * gw // (ns * gps), (n_groups * gw // ns + n_groups) // gps
    nt_dims = (((1,), (1,)), ((), ()))
    tn_dims = (((0,), (0,)), ((), ()))

    def body(dy_ref, xs_ref, b_ref, c_ref, dtp_ref, par_ref, sin_ref,
             dxs_ref, db_ref, dc_ref, ddtp_ref, dpar_ref, dst):
        @pl.when(pl.program_id(1) == 0)
        def _():
            dst[...] = jnp.zeros_like(dst)
            dpar_ref[...] = jnp.zeros_like(dpar_ref)

        dtg, a_g, acum, acum_t, row, col = _ssd_chunk_common(dtp_ref, par_ref)
        lane = lax.broadcasted_iota(jnp.int32, (1, gps * LANES), 1)
        dacum = jnp.zeros((ell, gps * LANES), F32)
        xsum = jnp.zeros((ell, gps * LANES), F32)
        dsum = jnp.zeros((1, gps * LANES), F32)
        for gi, hh in [(gi, hh) for gi in range(gps) for hh in range(hpg)]:
            if hh == 0:
                bb = b_ref[:, gi * ns:(gi + 1) * ns].astype(BF16)
                cc = c_ref[:, gi * ns:(gi + 1) * ns].astype(BF16)
                cb = lax.dot_general(cc, bb, nt_dims, preferred_element_type=F32)
                cb_t = lax.dot_general(bb, cc, nt_dims, preferred_element_type=F32)
                dcb = jnp.zeros((ell, ell), F32)
                dcb_t = jnp.zeros((ell, ell), F32)
                dc = jnp.zeros((ell, ns), F32)
                db = jnp.zeros((ell, ns), F32)
            lanes = slice(gi * gw + hh * hd, gi * gw + (hh + 1) * hd)
            hl = gi * LANES + hh
            onehot = (lane == hl).astype(F32)
            col_a, row_a = acum[:, hl:hl + 1], acum_t[hl:hl + 1, :]
            decay = jnp.exp(jnp.where(row >= col, col_a - row_a, NEG_BIG))
            decay_t = jnp.exp(jnp.where(col >= row, row_a - col_a, NEG_BIG))
            e_col = jnp.exp(col_a)
            a_last = acum[ell - 1:ell, hl:hl + 1]
            w = jnp.exp(a_last - col_a)
            e_last = jnp.exp(a_last)
            xs_h, dy_h = xs_ref[:, lanes], dy_ref[:, lanes]
            dt_h = dtg[:, hl:hl + 1]
            xdt = xs_h * dt_h
            xdt_b, dy_b = xdt.astype(BF16), dy_h.astype(BF16)
            s_h, ds_h = sin_ref[gi, hh * hd:(hh + 1) * hd, :], dst[gi * hpg + hh]
            dm_decay = lax.dot_general(dy_b, xdt_b, nt_dims, preferred_element_type=F32) * decay
            dm_decay_t = lax.dot_general(xdt_b, dy_b, nt_dims, preferred_element_type=F32) * decay_t
            dcb += dm_decay
            dcb_t += dm_decay_t
            m_t = cb_t * decay_t
            dac = jnp.sum(dm_decay * cb, axis=1, keepdims=True) - jnp.sum(dm_decay_t * cb_t, axis=1, keepdims=True)
            b_ds = lax.dot_general(bb, ds_h.astype(BF16), nt_dims, preferred_element_type=F32)
            dxdt = jnp.dot(m_t.astype(BF16), dy_b, preferred_element_type=F32) + w * b_ds
            c_s = lax.dot_general(cc, s_h.astype(BF16), nt_dims, preferred_element_type=F32)
            dc += e_col * jnp.dot(dy_b, s_h.astype(BF16), preferred_element_type=F32)
            db += w * jnp.dot(xdt_b, ds_h.astype(BF16), preferred_element_type=F32)
            dac += jnp.sum(dy_h * c_s, axis=1, keepdims=True) * e_col
            q = jnp.sum(xdt * b_ds, axis=1, keepdims=True) * w
            dac -= q
            d_last = jnp.sum(q, axis=0, keepdims=True) + e_last * jnp.sum(
                jnp.sum(s_h * ds_h, axis=1, keepdims=True), axis=0, keepdims=True)
            is_last = lax.broadcasted_iota(jnp.int32, (ell, 1), 0) == ell - 1
            dac += jnp.where(is_last, d_last, 0.0)
            dacum += dac * onehot
            dst[gi * hpg + hh] = e_last * ds_h + lax.dot_general((dy_h * e_col).astype(BF16), cc, tn_dims,
                                                                 preferred_element_type=F32)
            dxs_ref[:, lanes] = dxdt * dt_h + dy_h * par_ref[2:3, hl:hl + 1]
            xsum += jnp.sum(dxdt * xs_h, axis=1, keepdims=True) * onehot
            dsum += jnp.sum(jnp.sum(dy_h * xs_h, axis=1, keepdims=True), axis=0, keepdims=True) * onehot
            if hh == hpg - 1:
                group = slice(gi * ns, (gi + 1) * ns)
                dc_ref[:, group] = dc + jnp.dot(dcb.astype(BF16), bb, preferred_element_type=F32)
                db_ref[:, group] = db + jnp.dot(dcb_t.astype(BF16), cc, preferred_element_type=F32)
        dda = _split_dot(dacum, (col >= row).astype(BF16), SSD_CUMSUM_PIECES, left=True)
        ddtp = (xsum + dda * a_g) * _sigmoid(dtp_ref[...] + par_ref[0:1, :])
        ddtp_ref[...] = ddtp
        dpar_ref[0:1, :] += jnp.sum(ddtp, axis=0, keepdims=True)
        dpar_ref[1:2, :] += jnp.sum(dda * dtg, axis=0, keepdims=True) * a_g
        dpar_ref[2:3, :] += dsum

    rev = lambda i: nc - 1 - i
    return pl.pallas_call(
        body, name=name, grid=(n_groups // gps, nc),
        in_specs=[pl.BlockSpec((ell, gps * gw), lambda g, i: (rev(i), g)),
                  pl.BlockSpec((ell, gps * gw), lambda g, i: (rev(i), g)),
                  pl.BlockSpec((ell, gps * ns), lambda g, i: (rev(i), b_blk0 + g)),
                  pl.BlockSpec((ell, gps * ns), lambda g, i: (rev(i), c_blk0 + g)),
                  pl.BlockSpec((ell, gps * LANES), lambda g, i: (rev(i), g)),
                  pl.BlockSpec((SUBLANES, gps * LANES), lambda g, i: (0, g)),
                  pl.BlockSpec((None, gps, gw, ns), lambda g, i: (rev(i), g, 0, 0))],
        out_specs=[pl.BlockSpec((ell, gps * gw), lambda g, i: (rev(i), g)),
                   pl.BlockSpec((ell, gps * ns), lambda g, i: (rev(i), g)),
                   pl.BlockSpec((ell, gps * ns), lambda g, i: (rev(i), g)),
                   pl.BlockSpec((ell, gps * LANES), lambda g, i: (rev(i), g)),
                   pl.BlockSpec((SUBLANES, gps * LANES), lambda g, i: (0, g))],
        out_shape=[jax.ShapeDtypeStruct((t, n_groups * gw), F32), jax.ShapeDtypeStruct((t, n_groups * ns), F32),
                   jax.ShapeDtypeStruct((t, n_groups * ns), F32), jax.ShapeDtypeStruct((t, n_groups * LANES), F32),
                   jax.ShapeDtypeStruct((SUBLANES, n_groups * LANES), F32)],
        scratch_shapes=[pltpu.VMEM((gps * hpg, hd, ns), F32)],
        compiler_params=_params("parallel", "arbitrary"),
    )(dy, xa, xa, xa, dtp, par, s_in)


def _ssd_gate_fwd(y, xa, zx, d_rep, out_norm, *, name):
    t, di = y.shape
    gw = SSD_HEADS_PER_GROUP * SSD_HEAD_DIM
    tm = _tile(t, 512, SUBLANES)

    def body(y_ref, xs_ref, z_ref, d_ref, n_ref, o_ref):
        zv = z_ref[...]
        gt = (y_ref[...] + d_ref[...] * xs_ref[...]) * (zv * _sigmoid(zv))
        r = lax.rsqrt(jnp.mean(gt * gt, axis=-1, keepdims=True) + NORM_EPS)
        o_ref[...] = (gt * r * n_ref[...]).astype(BF16)

    blk = pl.BlockSpec((tm, gw), lambda i, g: (i, g))
    vec = pl.BlockSpec((1, gw), lambda i, g: (0, g))
    return pl.pallas_call(
        body, name=name, grid=(t // tm, di // gw),
        in_specs=[blk, blk, blk, vec, vec], out_specs=blk,
        out_shape=jax.ShapeDtypeStruct((t, di), BF16),
        compiler_params=_params("parallel", "parallel"),
    )(y, xa, zx, d_rep, out_norm)


def _ssd_gate_bwd(dgn, y, xa, zx, d_rep, out_norm, *, name):
    t, di = y.shape
    gw = SSD_HEADS_PER_GROUP * SSD_HEAD_DIM
    tm = _tile(t, 512, SUBLANES)

    def body(dg_ref, y_ref, xs_ref, z_ref, d_ref, n_ref, dy_ref, dz_ref, dn_ref):
        @pl.when(pl.program_id(1) == 0)
        def _():
            dn_ref[...] = jnp.zeros_like(dn_ref)

        zv = z_ref[...]
        sg = _sigmoid(zv)
        sz = zv * sg
        y2 = y_ref[...] + d_ref[...] * xs_ref[...]
        gt = y2 * sz
        r = lax.rsqrt(jnp.mean(gt * gt, axis=-1, keepdims=True) + NORM_EPS)
        ghat = gt * r
        dgv = dg_ref[...]
        dn_ref[...] += jnp.sum(dgv * ghat, axis=0, keepdims=True)
        u = dgv * n_ref[...]
        dgt = r * (u - ghat * jnp.mean(u * ghat, axis=-1, keepdims=True))
        dy_ref[...] = dgt * sz
        dz_ref[...] = (dgt * y2 * (sg * (1.0 + zv * (1.0 - sg)))).astype(BF16)

    blk = pl.BlockSpec((tm, gw), lambda g, i: (i, g))
    vec = pl.BlockSpec((1, gw), lambda g, i: (0, g))
    return pl.pallas_call(
        body, name=name, grid=(di // gw, t // tm),
        in_specs=[blk, blk, blk, blk, vec, vec], out_specs=[blk, blk, vec],
        out_shape=[jax.ShapeDtypeStruct((t, di), F32), jax.ShapeDtypeStruct((t, di), BF16),
                   jax.ShapeDtypeStruct((1, di), F32)],
        compiler_params=_params("parallel", "arbitrary"),
    )(dgn, y, xa, zx, d_rep, out_norm)


def _ssd_mixer_fwd(x, gain, w_zx_t, w_dt_t, conv_w, conv_b, par, d_rep, out_norm, w_out, tag):
    di = w_out.shape[0]
    n_groups = di // (SSD_HEADS_PER_GROUP * SSD_HEAD_DIM)
    h = _rms_fwd(x, gain, name=f"ssd_norm_{tag}")
    zx = _matmul(h, w_zx_t, "nt", name=f"ssd_in_{tag}")
    dtp = _matmul(h, w_dt_t, "nt", name=f"ssd_dt_{tag}")
    xa = _ssd_conv_fwd(zx, conv_w, conv_b, di, name=f"ssd_conv_{tag}")
    y, s_in = _ssd_scan_fwd(xa, dtp, par, n_groups, name=f"ssd_scan_{tag}")
    gn = _ssd_gate_fwd(y, xa, zx, d_rep, out_norm, name=f"ssd_gate_{tag}")
    x_new = _matmul(gn, w_out, "nn", resid=x, name=f"ssd_out_{tag}")
    return x_new, (x, h, zx, dtp, xa, y, s_in, gn)


def _ssd_mixer_bwd(dx, saved, gain, w_zx_t, w_dt_t, conv_w, conv_b, par, d_rep, out_norm, w_out, grads, row_out,
                   tag):
    x, h, zx, dtp, xa, y, s_in, gn = saved
    di = w_out.shape[0]
    n_groups = di // (SSD_HEADS_PER_GROUP * SSD_HEAD_DIM)
    dgn = _matmul(dx, w_out, "nt", name=f"ssd_dgn_{tag}")
    grads = _matmul_tn_into(grads, gn, dx, row_out, name=f"ssd_dwout_{tag}")
    dy2, dz, dnorm = _ssd_gate_bwd(dgn, y, xa, zx, d_rep, out_norm, name=f"ssd_dgate_{tag}")
    dxs, db, dc, ddtp, dpar = _ssd_scan_bwd(dy2, xa, dtp, par, s_in, n_groups, name=f"ssd_dscan_{tag}")
    dxa = jnp.concatenate([dxs, db, dc], axis=1)
    dxbc, dconv_w, dconv_b = _ssd_conv_bwd(dxa, zx, conv_w, conv_b, di, name=f"ssd_dconv_{tag}")
    dzx = jnp.concatenate([dz, dxbc], axis=1)
    dw_zx_t = _matmul_tn(dzx, h, name=f"ssd_dwin_{tag}")
    dw_dt_t = _matmul_tn(ddtp, h, name=f"ssd_dwdt_{tag}")
    dh = _matmul(dzx, w_zx_t, "nn", name=f"ssd_dh_{tag}")
    dh = _matmul(ddtp, w_dt_t, "nn", resid=dh, name=f"ssd_dhdt_{tag}")
    dx_in, dgain = _rms_bwd(x, gain, dh, dx, name=f"ssd_dnorm_{tag}")
    return dx_in, grads, dw_zx_t, dw_dt_t, dconv_w, dconv_b, dpar, dnorm, dgain


def _sb_qk_norm_fwd(qkv, gains, *, name):
    ns, t, _ = qkv.shape
    per = ns // 3
    tm = _tile(t, 1024, SUBLANES)
    inv_sqrt_d = 1.0 / math.sqrt(SB_HEAD_DIM)

    def body(x_ref, g_ref, o_ref):
        kind = pl.program_id(0) // per
        xv = x_ref[...]

        @pl.when(kind == 2)
        def _():
            o_ref[...] = xv.astype(BF16)

        @pl.when(kind < 2)
        def _():
            left = lax.broadcasted_iota(jnp.int32, (1, LANES), 1) < SB_HEAD_DIM
            sq = xv * xv
            ms = jnp.where(left, jnp.sum(jnp.where(left, sq, 0.0), axis=1, keepdims=True),
                           jnp.sum(jnp.where(left, 0.0, sq), axis=1, keepdims=True)) * (1.0 / SB_HEAD_DIM)
            y = xv * lax.rsqrt(ms + NORM_EPS) * g_ref[pl.ds(kind, 1), :]
            o_ref[...] = (y * jnp.where(kind == 0, inv_sqrt_d, 1.0)).astype(BF16)

    blk = pl.BlockSpec((None, tm, LANES), lambda s, i: (s, i, 0))
    return pl.pallas_call(
        body, name=name, grid=(ns, t // tm),
        in_specs=[blk, pl.BlockSpec((SUBLANES, LANES), lambda s, i: (0, 0))], out_specs=blk,
        out_shape=jax.ShapeDtypeStruct((ns, t, LANES), BF16),
        compiler_params=_params("parallel", "parallel"),
    )(qkv, gains)


def _sb_qk_norm_bwd(dq, dk, dv, qkv, gains, *, name):
    ns, t, _ = qkv.shape
    per = ns // 3
    tm = _tile(t, 1024, SUBLANES)
    inv_sqrt_d = 1.0 / math.sqrt(SB_HEAD_DIM)

    def body(dq_ref, dk_ref, dv_ref, x_ref, g_ref, o_ref, dg_ref):
        s = pl.program_id(0)
        kind = s // per

        @pl.when((s == 0) & (pl.program_id(1) == 0))
        def _():
            dg_ref[...] = jnp.zeros_like(dg_ref)

        @pl.when(kind == 2)
        def _():
            o_ref[...] = dv_ref[...].astype(BF16)

        @pl.when(kind < 2)
        def _():
            xv = x_ref[...]
            dy = jnp.where(kind == 0, dq_ref[...] * inv_sqrt_d, dk_ref[...])
            left = lax.broadcasted_iota(jnp.int32, (1, LANES), 1) < SB_HEAD_DIM

            def halves(v):
                return jnp.where(left, jnp.sum(jnp.where(left, v, 0.0), axis=1, keepdims=True),
                                 jnp.sum(jnp.where(left, 0.0, v), axis=1, keepdims=True))

            r = lax.rsqrt(halves(xv * xv) * (1.0 / SB_HEAD_DIM) + NORM_EPS)
            xhat = xv * r
            u = dy * g_ref[pl.ds(kind, 1), :]
            o_ref[...] = (r * (u - xhat * halves(u * xhat) * (1.0 / SB_HEAD_DIM))).astype(BF16)
            dg_ref[pl.ds(kind, 1), :] += jnp.sum(dy * xhat, axis=0, keepdims=True)

    def grad_blk(kind):
        def index(s, i):
            mine = (s >= kind * per) & (s < (kind + 1) * per)
            return jnp.where(mine, s - kind * per, 0), jnp.where(mine, i, 0), 0
        return pl.BlockSpec((None, tm, LANES), index)

    blk = pl.BlockSpec((None, tm, LANES), lambda s, i: (s, i, 0))
    vec = pl.BlockSpec((SUBLANES, LANES), lambda s, i: (0, 0))
    return pl.pallas_call(
        body, name=name, grid=(ns, t // tm),
        in_specs=[grad_blk(0), grad_blk(1), grad_blk(2), blk, vec], out_specs=[blk, vec],
        out_shape=[jax.ShapeDtypeStruct((ns, t, LANES), BF16), jax.ShapeDtypeStruct((SUBLANES, LANES), F32)],
        compiler_params=_params("arbitrary", "arbitrary"),
    )(dq, dk, dv, qkv, gains)


def _split_dot(v, ones_mat, pieces, left=False):
    total, rest = None, v
    for p in range(pieces):
        part = rest.astype(BF16)
        if p + 1 < pieces:
            rest = rest - part.astype(F32)
        d = (jnp.dot(ones_mat, part, preferred_element_type=F32) if left
             else jnp.dot(part, ones_mat, preferred_element_type=F32))
        total = d if total is None else total + d
    return total


LOGIT_SUM_PIECES = 2
GRAD_SUM_PIECES = 2
LOG_WEIGHT_UNDERFLOW = -105.0


def _sb_attn_fwd(qkv_n, n_heads, *, name):
    ns, t, _ = qkv_n.shape
    per = ns // 3
    bq, blk, hd = SB_QUERY_BLOCK, SB_BLOCK, SB_HEAD_DIM
    nq, n_diag = t // bq, bq // blk

    def body(q_ref, k_ref, v_ref, o_ref):
        i = pl.program_id(1)
        row = lax.broadcasted_iota(jnp.int32, (blk, blk), 0)
        col = lax.broadcasted_iota(jnp.int32, (blk, blk), 1)
        later_keys = (row > col).astype(BF16)
        qry = lax.broadcasted_iota(jnp.int32, (bq, blk), 0)
        key = lax.broadcasted_iota(jnp.int32, (bq, blk), 1)

        def tile(kb, carry, key_offset):
            out = []
            start = pl.multiple_of(kb * blk, blk)
            for hf in range(2):
                lanes = slice(hf * hd, (hf + 1) * hd)
                run, acc = carry[hf]
                z = lax.dot_general(q_ref[:, lanes], k_ref[pl.ds(start, blk), lanes], NT_DIMS,
                                    preferred_element_type=F32)
                sp = _softplus(z)
                lm = -sp if key_offset is None else jnp.where(key + key_offset < qry, -sp, 0.0)
                after = _split_dot(lm, later_keys, LOGIT_SUM_PIECES) + run
                a = jnp.exp(z - sp + after)
                if key_offset is not None:
                    a = jnp.where(key + key_offset < qry, a, 0.0)
                acc = acc + jnp.dot(a.astype(BF16), v_ref[pl.ds(start, blk), lanes], preferred_element_type=F32)
                out.append((run + jnp.sum(lm, axis=1, keepdims=True), acc))
            return tuple(out)

        def live(carry):
            return jnp.max(jnp.maximum(carry[0][0], carry[1][0])) > LOG_WEIGHT_UNDERFLOW

        def step(state):
            s, _, carry = state
            carry = tile(n_diag * i - 1 - s, carry, None)
            return s + 1, live(carry), carry

        carry = tuple((jnp.zeros((bq, 1), F32), jnp.zeros((bq, hd), F32)) for _ in range(2))
        for j in reversed(range(n_diag)):
            carry = tile(n_diag * i + j, carry, j * blk)
        _, _, carry = lax.while_loop(lambda st: (st[0] < n_diag * i) & st[1], step,
                                     (jnp.int32(0), live(carry), carry))
        o_ref[...] = jnp.concatenate([carry[0][1], carry[1][1]], axis=1)

    return pl.pallas_call(
        body, name=name, grid=(per, nq),
        in_specs=[pl.BlockSpec((None, bq, LANES), lambda p, i: (p, i, 0)),
                  pl.BlockSpec((None, t, LANES), lambda p, i: (per + p, 0, 0)),
                  pl.BlockSpec((None, t, LANES), lambda p, i: (2 * per + p, 0, 0))],
        out_specs=pl.BlockSpec((bq, LANES), lambda p, i: (i, p)),
        out_shape=jax.ShapeDtypeStruct((t, n_heads * hd), F32),
        compiler_params=_params("parallel", "arbitrary"),
    )(qkv_n, qkv_n, qkv_n)


def _sb_attn_bwd(do, qkv_n, *, name):
    ns, t, _ = qkv_n.shape
    per = ns // 3
    bq, blk, hd = SB_QUERY_BLOCK, SB_BLOCK, SB_HEAD_DIM
    nq, n_diag = t // bq, bq // blk
    nt_dims = (((1,), (1,)), ((), ()))
    tn_dims = (((0,), (0,)), ((), ()))

    def body(q_ref, k_ref, v_ref, do_ref, dq_ref, dk_ref, dv_ref):
        i = pl.program_id(1)

        @pl.when(i == 0)
        def _():
            dk_ref[...] = jnp.zeros_like(dk_ref)
            dv_ref[...] = jnp.zeros_like(dv_ref)

        row = lax.broadcasted_iota(jnp.int32, (blk, blk), 0)
        col = lax.broadcasted_iota(jnp.int32, (blk, blk), 1)
        later_keys = (col > row).astype(BF16)
        earlier_keys = (col < row).astype(BF16)
        key = lax.broadcasted_iota(jnp.int32, (blk, bq), 0)
        qry = lax.broadcasted_iota(jnp.int32, (blk, bq), 1)
        halves = [slice(hf * hd, (hf + 1) * hd) for hf in range(2)]
        q_hs = [q_ref[:, lanes] for lanes in halves]
        do_bs = [do_ref[:, lanes].astype(BF16) for lanes in halves]

        def scores(kb, hf, key_offset):
            k_blk = k_ref[pl.ds(pl.multiple_of(kb * blk, blk), blk), halves[hf]]
            z = lax.dot_general(k_blk, q_hs[hf], nt_dims, preferred_element_type=F32)
            sp = _softplus(z)
            return k_blk, z, sp, (-sp if key_offset is None else jnp.where(key + key_offset < qry, -sp, 0.0))

        def add_column_sums(tots, kb, key_offset):
            return [tots[hf] + jnp.sum(scores(kb, hf, key_offset)[3], axis=0, keepdims=True) for hf in range(2)]

        def live(tots):
            return jnp.max(jnp.maximum(tots[0], tots[1])) > LOG_WEIGHT_UNDERFLOW

        def reach(state):
            s, _, tots = state
            tots = add_column_sums(tots, n_diag * i - 1 - s, None)
            return s + 1, live(tots), tots

        tots = [jnp.zeros((1, bq), F32)] * 2
        for j in reversed(range(n_diag)):
            tots = add_column_sums(tots, n_diag * i + j, j * blk)
        reached, _, tots = lax.while_loop(lambda st: (st[0] < n_diag * i) & st[1], reach,
                                          (jnp.int32(0), live(tots), tots))

        def tile(kb, carry, key_offset):
            out = []
            start = pl.multiple_of(kb * blk, blk)
            for hf, lanes in enumerate(halves):
                seen, gsum, dq = carry[hf]
                q_h, do_b = q_hs[hf], do_bs[hf]
                k_blk, z, sp, lm = scores(kb, hf, key_offset)
                blk_tot = jnp.sum(lm, axis=0, keepdims=True)
                after = _split_dot(lm, later_keys, LOGIT_SUM_PIECES, left=True) + (tots[hf] - seen - blk_tot)
                a = jnp.exp(z - sp + after)
                if key_offset is not None:
                    a = jnp.where(key + key_offset < qry, a, 0.0)
                da = lax.dot_general(v_ref[pl.ds(start, blk), lanes], do_b, nt_dims, preferred_element_type=F32)
                g = da * a
                before = _split_dot(g, earlier_keys, GRAD_SUM_PIECES, left=True) + gsum
                omb = jnp.exp(-sp)
                dz = g * omb - (1.0 - omb) * before
                if key_offset is not None:
                    dz = jnp.where(key + key_offset < qry, dz, 0.0)
                dz_b = dz.astype(BF16)
                dk_ref[pl.ds(start, blk), lanes] += jnp.dot(dz_b, q_h, preferred_element_type=F32)
                dv_ref[pl.ds(start, blk), lanes] += jnp.dot(a.astype(BF16), do_b, preferred_element_type=F32)
                dq = dq + lax.dot_general(dz_b, k_blk, tn_dims, preferred_element_type=F32)
                out.append((seen + blk_tot, gsum + jnp.sum(g, axis=0, keepdims=True), dq))
            return tuple(out)

        init = tuple((jnp.zeros((1, bq), F32), jnp.zeros((1, bq), F32), jnp.zeros((bq, hd), F32))
                     for _ in range(2))
        carry = lax.fori_loop(n_diag * i - reached, n_diag * i, lambda kb, c: tile(kb, c, None), init)
        for j in range(n_diag):
            carry = tile(n_diag * i + j, carry, j * blk)
        dq_ref[...] = jnp.concatenate([carry[0][2], carry[1][2]], axis=1)

    full = lambda off: pl.BlockSpec((None, t, LANES), lambda p, i: (off + p, 0, 0))
    q_blk = pl.BlockSpec((None, bq, LANES), lambda p, i: (p, i, 0))
    slab = jax.ShapeDtypeStruct((per, t, LANES), F32)
    return pl.pallas_call(
        body, name=name, grid=(per, nq),
        in_specs=[q_blk, full(per), full(2 * per), pl.BlockSpec((bq, LANES), lambda p, i: (i, p))],
        out_specs=[q_blk, full(0), full(0)],
        out_shape=[slab, slab, slab],
        compiler_params=_params("parallel", "arbitrary"),
    )(qkv_n, qkv_n, qkv_n, do)


def _sb_mixer_fwd(x, gain, w_qkv_t, qk_gains, w_out, tag):
    n_heads = w_out.shape[0] // SB_HEAD_DIM
    h = _rms_fwd(x, gain, name=f"sb_norm_{tag}")
    qkv = _matmul(h, w_qkv_t, "nt", out_slabs=True, tn_cap=256, name=f"sb_qkv_{tag}")
    qkv_n = _sb_qk_norm_fwd(qkv, qk_gains, name=f"sb_qknorm_{tag}")
    o = _sb_attn_fwd(qkv_n, n_heads, name=f"sb_attn_{tag}")
    x_new = _matmul(o, w_out, "nn", resid=x, name=f"sb_out_{tag}")
    return x_new, (x, h, qkv, qkv_n, o)


def _sb_mixer_bwd(dx, saved, gain, w_qkv_t, qk_gains, w_out, grads, row_qkv, row_out, tag):
    x, h, qkv, qkv_n, o = saved
    do = _matmul(dx, w_out, "nt", name=f"sb_do_{tag}")
    grads = _matmul_tn_into(grads, o, dx, row_out, name=f"sb_dwout_{tag}")
    dq, dk, dv = _sb_attn_bwd(do, qkv_n, name=f"sb_dattn_{tag}")
    dqkv, dqk_gains = _sb_qk_norm_bwd(dq, dk, dv, qkv, qk_gains, name=f"sb_dqknorm_{tag}")
    grads = _matmul_tn_into(grads, dqkv, h, row_qkv, a_slabs=True, name=f"sb_dwqkv_{tag}")
    dh = _matmul(dqkv, w_qkv_t, "nn", a_slabs=True, name=f"sb_dh_{tag}")
    dx_in, dgain = _rms_bwd(x, gain, dh, dx, name=f"sb_dnorm_{tag}")
    return dx_in, grads, dqk_gains, dgain


MESH = pl.DeviceIdType.MESH


def _position():
    return lax.axis_index("x"), lax.axis_index("y"), lax.axis_index("c")


def _all_gather(shard, *, name, in_vmem):
    rows, n = shard.shape
    space = pltpu.VMEM if in_vmem else pltpu.HBM

    def body(x_ref, out_ref, send_sems, recv_sems, local_sem):
        x, y, c = _position()
        me, sibling = (x, y, c), (x, y, 1 - c)
        chips = [(1 - x, y), (x, 1 - y), (1 - x, 1 - y)]

        def block(px, py, pc):
            return out_ref.at[4 * px + 2 * py + pc]

        def copy(k, blk, to, src=None):
            return pltpu.make_async_remote_copy(
                src_ref=block(*blk) if src is None else src, dst_ref=block(*blk),
                send_sem=send_sems.at[k], recv_sem=recv_sems.at[k], device_id=to, device_id_type=MESH)

        mine = pltpu.make_async_copy(x_ref, block(*me), local_sem)
        mine.start()
        first = [copy(0, me, sibling, src=x_ref)]
        first += [copy(1 + j, me, (*chip, c), src=x_ref) for j, chip in enumerate(chips)]
        for cp in first:
            cp.start()
        passed = [copy(4 + j, (*chip, c), sibling) for j, chip in enumerate(chips)]
        for j, chip in enumerate(chips):
            copy(1 + j, (*chip, c), me).wait_recv()
            passed[j].start()
        copy(0, sibling, me).wait_recv()
        for j, chip in enumerate(chips):
            copy(4 + j, (*chip, 1 - c), me).wait_recv()
        for cp in first + passed:
            cp.wait_send()
        mine.wait()

    return pl.pallas_call(
        body, name=name,
        out_shape=jax.ShapeDtypeStruct((N_DEV, rows, n), shard.dtype),
        in_specs=[pl.BlockSpec(memory_space=space)], out_specs=pl.BlockSpec(memory_space=space),
        scratch_shapes=[pltpu.SemaphoreType.DMA((7,)), pltpu.SemaphoreType.DMA((7,)), pltpu.SemaphoreType.DMA],
        compiler_params=pltpu.CompilerParams(vmem_limit_bytes=V7X_VMEM_LIMIT_BYTES),
    )(shard)


def _exchange_sibling(parts, *, name):
    nchip, _, rows, n = parts.shape

    def body(p_ref, recv_ref, send_sem, recv_sem):
        x, y, c = _position()
        cp = pltpu.make_async_remote_copy(src_ref=p_ref.at[:, 1 - c], dst_ref=recv_ref, send_sem=send_sem,
                                          recv_sem=recv_sem, device_id=(x, y, 1 - c), device_id_type=MESH)
        cp.start()
        cp.wait()

    return pl.pallas_call(
        body, name=name,
        out_shape=jax.ShapeDtypeStruct((nchip, rows, n), parts.dtype),
        in_specs=[pl.BlockSpec(memory_space=pltpu.HBM)], out_specs=pl.BlockSpec(memory_space=pltpu.HBM),
        scratch_shapes=[pltpu.SemaphoreType.DMA, pltpu.SemaphoreType.DMA],
    )(parts)


def _exchange_chips(chip_sums, *, name):
    _, rows, n = chip_sums.shape

    def body(s_ref, recv_ref, send_sems, recv_sems):
        x, y, c = _position()
        chips = [(1 - x, y), (x, 1 - y), (1 - x, 1 - y)]
        copies = [pltpu.make_async_remote_copy(
            src_ref=s_ref.at[2 * cx + cy], dst_ref=recv_ref.at[j], send_sem=send_sems.at[j],
            recv_sem=recv_sems.at[j], device_id=(cx, cy, c), device_id_type=MESH)
            for j, (cx, cy) in enumerate(chips)]
        for cp in copies:
            cp.start()
        for cp in copies:
            cp.wait()

    return pl.pallas_call(
        body, name=name,
        out_shape=jax.ShapeDtypeStruct((3, rows, n), chip_sums.dtype),
        in_specs=[pl.BlockSpec(memory_space=pltpu.HBM)], out_specs=pl.BlockSpec(memory_space=pltpu.HBM),
        scratch_shapes=[pltpu.SemaphoreType.DMA((3,)), pltpu.SemaphoreType.DMA((3,))],
    )(chip_sums)


def _add_pairs(parts, recv, c_mine, *, name):
    nchip, _, rows, n = parts.shape
    tr = _tile(rows, 512, SUBLANES)

    def body(c_ref, a_ref, b_ref, o_ref, wire_ref):
        s = a_ref[...] + b_ref[...]
        o_ref[...] = s
        wire_ref[...] = s.astype(WIRE_DTYPE)

    out_blk = pl.BlockSpec((None, tr, n), lambda k, i, c: (k, i, 0))
    return pl.pallas_call(
        body, name=name,
        grid_spec=pltpu.PrefetchScalarGridSpec(
            num_scalar_prefetch=1, grid=(nchip, rows // tr),
            in_specs=[pl.BlockSpec((None, None, tr, n), lambda k, i, c: (k, c[0], i, 0)),
                      pl.BlockSpec((None, tr, n), lambda k, i, c: (k, i, 0))],
            out_specs=[out_blk, out_blk]),
        out_shape=[jax.ShapeDtypeStruct((nchip, rows, n), parts.dtype),
                   jax.ShapeDtypeStruct((nchip, rows, n), WIRE_DTYPE)],
        compiler_params=_params("parallel", "parallel"),
    )(c_mine, parts, recv)


def _adamw_math(w, g, m, v):
    m = ADAM_B1 * m + (1.0 - ADAM_B1) * g
    v = ADAM_B2 * v + (1.0 - ADAM_B2) * (g * g)
    m_hat = m / (1.0 - ADAM_B1 ** ADAM_STEP)
    v_hat = v / (1.0 - ADAM_B2 ** ADAM_STEP)
    delta = -ADAM_LR * (m_hat / (jnp.sqrt(v_hat) + ADAM_EPS) + ADAM_WD * w)
    return delta, m, v


def _adamw_sharded(chip_sums, recv, k_mine, w, m, v, *, name):
    rows, n = w.shape
    tr = _tile(rows, 256, SUBLANES)

    def body(k_ref, s_ref, r_ref, w_ref, m_ref, v_ref, g_out, d_out, m_out, v_out):
        g = ((s_ref[...] + r_ref[0].astype(F32)) + r_ref[1].astype(F32)) + r_ref[2].astype(F32)
        delta, m_new, v_new = _adamw_math(w_ref[...], g, m_ref[...], v_ref[...])
        g_out[...] = g
        d_out[...] = delta
        m_out[...] = m_new
        v_out[...] = v_new

    blk = pl.BlockSpec((tr, n), lambda i, k: (i, 0))
    out = jax.ShapeDtypeStruct((rows, n), F32)
    return pl.pallas_call(
        body, name=name,
        grid_spec=pltpu.PrefetchScalarGridSpec(
            num_scalar_prefetch=1, grid=(rows // tr,),
            in_specs=[pl.BlockSpec((None, tr, n), lambda i, k: (k[0], i, 0)),
                      pl.BlockSpec((3, tr, n), lambda i, k: (0, i, 0)), blk, blk, blk],
            out_specs=[blk, blk, blk, blk]),
        out_shape=[out, out, out, out],
        compiler_params=_params("parallel"),
    )(k_mine, chip_sums, recv, w, m, v)


SMALL_ROWS = 40
ROW_MIX_NORM, ROW_FFN_NORM, ROW_CONV_B, ROW_OUT_NORM, ROW_POOL_SCALE, ROW_CONV_W = 0, 4, 8, 12, 14, 16
ROW_SSD_VEC, ROW_QK_GAIN, ROW_LOSS = 32, 33, 34


def _adamw_small(gathered, w, m, v, *, name):
    _, rows, n = gathered.shape

    def body(a_ref, w_ref, m_ref, v_ref, g_out, d_out, m_out, v_out):
        g = a_ref[0]
        for d in range(1, N_DEV):
            g = g + a_ref[d]
        row = lax.broadcasted_iota(jnp.int32, (rows, 1), 0)
        g = jnp.where(row == ROW_QK_GAIN, g + pltpu.roll(g, SB_HEAD_DIM, 1), g)
        g = jnp.where(row == ROW_LOSS, jnp.sum(g, axis=1, keepdims=True), g)
        g_out[...] = g
        delta, m_new, v_new = _adamw_math(w_ref[...], g, m_ref[...], v_ref[...])
        d_out[...] = delta
        m_out[...] = m_new
        v_out[...] = v_new

    out = jax.ShapeDtypeStruct((rows, n), F32)
    return pl.pallas_call(body, name=name, out_shape=[out, out, out, out])(gathered, w, m, v)


BIG_WEIGHTS = ("ffn_gate", "ffn_up", "ffn_down", "sb_qkv", "ssd_out", "pool_in", "sb_out", "pool_group", "ssd_in")
COLUMN_SHARDED = ("ssd_in", "sb_qkv", "ffn_gate", "ffn_up")
ROW_PAD = 512
WIRE_DTYPE = jnp.bfloat16


def _to_rows(name, shard, d):
    if name in COLUMN_SHARDED:
        shard = jnp.swapaxes(shard, -1, -2)
    return shard.reshape(-1, d)


def _from_rows(name, rows, shard_shape):
    if name in COLUMN_SHARDED:
        lead, k, n = shard_shape
        return jnp.swapaxes(rows.reshape(lead, n, k), -1, -2)
    return rows.reshape(shard_shape)


def _pad_rows(a, total):
    return jnp.pad(a, ((0, total - a.shape[0]),) + ((0, 0),) * (a.ndim - 1))


def _exact_bf16_rows(v, d):
    words = lax.bitcast_convert_type(v.reshape(-1), WIRE_DTYPE).reshape(-1)
    return _pad_rows(words, -(-words.shape[0] // d) * d).reshape(-1, d)


def _exact_f32(rows, count):
    words = rows.reshape(rows.shape[0], -1)[:, :2 * count].reshape(rows.shape[0], count, 2)
    return lax.bitcast_convert_type(words, F32)


def _device_blocks(full, d):
    return full.reshape(N_DEV, -1, d)


def kernel(x, mix_norm, pool_in, pool_group, pool_scale, ssd_in, ssd_conv_w, ssd_conv_b, ssd_dt_bias, ssd_a_log, ssd_d, ssd_out_norm, ssd_out, sb_qkv, sb_q_norm, sb_k_norm, sb_out, ffn_norm, ffn_gate, ffn_up, ffn_down, loss_target, m_mix_norm, m_pool_in, m_pool_group, m_pool_scale, m_ssd_in, m_ssd_conv_w, m_ssd_conv_b, m_ssd_dt_bias, m_ssd_a_log, m_ssd_d, m_ssd_out_norm, m_ssd_out, m_sb_qkv, m_sb_q_norm, m_sb_k_norm, m_sb_out, m_ffn_norm, m_ffn_gate, m_ffn_up, m_ffn_down, v_mix_norm, v_pool_in, v_pool_group, v_pool_scale, v_ssd_in, v_ssd_conv_w, v_ssd_conv_b, v_ssd_dt_bias, v_ssd_a_log, v_ssd_d, v_ssd_out_norm, v_ssd_out, v_sb_qkv, v_sb_q_norm, v_sb_k_norm, v_sb_out, v_ffn_norm, v_ffn_gate, v_ffn_up, v_ffn_down):
    weights = dict(mix_norm=mix_norm, pool_in=pool_in, pool_group=pool_group, pool_scale=pool_scale, ssd_in=ssd_in,
                   ssd_conv_w=ssd_conv_w, ssd_conv_b=ssd_conv_b, ssd_dt_bias=ssd_dt_bias, ssd_a_log=ssd_a_log,
                   ssd_d=ssd_d, ssd_out_norm=ssd_out_norm, ssd_out=ssd_out, sb_qkv=sb_qkv, sb_q_norm=sb_q_norm,
                   sb_k_norm=sb_k_norm, sb_out=sb_out, ffn_norm=ffn_norm, ffn_gate=ffn_gate, ffn_up=ffn_up,
                   ffn_down=ffn_down)
    mom1 = dict(mix_norm=m_mix_norm, pool_in=m_pool_in, pool_group=m_pool_group, pool_scale=m_pool_scale,
                ssd_in=m_ssd_in, ssd_conv_w=m_ssd_conv_w, ssd_conv_b=m_ssd_conv_b, ssd_dt_bias=m_ssd_dt_bias,
                ssd_a_log=m_ssd_a_log, ssd_d=m_ssd_d, ssd_out_norm=m_ssd_out_norm, ssd_out=m_ssd_out,
                sb_qkv=m_sb_qkv, sb_q_norm=m_sb_q_norm, sb_k_norm=m_sb_k_norm, sb_out=m_sb_out,
                ffn_norm=m_ffn_norm, ffn_gate=m_ffn_gate, ffn_up=m_ffn_up, ffn_down=m_ffn_down)
    mom2 = dict(mix_norm=v_mix_norm, pool_in=v_pool_in, pool_group=v_pool_group, pool_scale=v_pool_scale,
                ssd_in=v_ssd_in, ssd_conv_w=v_ssd_conv_w, ssd_conv_b=v_ssd_conv_b, ssd_dt_bias=v_ssd_dt_bias,
                ssd_a_log=v_ssd_a_log, ssd_d=v_ssd_d, ssd_out_norm=v_ssd_out_norm, ssd_out=v_ssd_out,
                sb_qkv=v_sb_qkv, sb_q_norm=v_sb_q_norm, sb_k_norm=v_sb_k_norm, sb_out=v_sb_out,
                ffn_norm=v_ffn_norm, ffn_gate=v_ffn_gate, ffn_up=v_ffn_up, ffn_down=v_ffn_down)
    names = list(weights)
    depth, d = mix_norm.shape
    xs, ys, cs = _position()
    dev = 4 * xs + 2 * ys + cs
    chip = 2 * xs + ys

    seg = {}
    row = 0
    for name in BIG_WEIGHTS:
        n_rows = weights[name].size // d
        seg[name] = (row, n_rows)
        row += -(-n_rows // SUBLANES) * SUBLANES
    big_rows = row
    n_scale, n_convw = pool_scale.size, ssd_conv_w.size
    exact = jnp.concatenate([_exact_bf16_rows(pool_scale, d), _exact_bf16_rows(ssd_conv_w, d)], axis=0)
    scale_rows = _exact_bf16_rows(pool_scale, d).shape[0]
    packed_rows = -(-(big_rows + exact.shape[0]) // ROW_PAD) * ROW_PAD

    def pack(tree, dtype):
        ends = [seg[n][0] for n in BIG_WEIGHTS[1:]] + [big_rows]
        return jnp.concatenate([_pad_rows(_to_rows(n, tree[n], d).astype(dtype), end - seg[n][0])
                                for n, end in zip(BIG_WEIGHTS, ends)], axis=0)

    w_wire = _pad_rows(jnp.concatenate([pack(weights, WIRE_DTYPE), exact], axis=0), packed_rows)
    gathered = _all_gather(w_wire, name="gather_weights", in_vmem=False)

    def seg_of(name):
        a, n = seg[name]
        return gathered[:, a:a + n]

    n_pool, n_ssd, n_sb = pool_in.shape[0], ssd_in.shape[0], sb_qkv.shape[0]
    assert n_ssd == 1 and n_sb == 1
    w_pool_in = seg_of("pool_in").reshape(N_DEV, n_pool, -1, d).transpose(1, 0, 2, 3).reshape(n_pool, d, d)
    grp = pool_group.shape
    w_pool_group = seg_of("pool_group").reshape(N_DEV, grp[0], grp[1], grp[2], grp[3]).transpose(1, 2, 0, 3, 4)
    w_pool_group = w_pool_group.reshape(grp[0], grp[1], grp[3], grp[3])
    w_ssd_in_t = seg_of("ssd_in").reshape(-1, d)
    w_ssd_out = seg_of("ssd_out").reshape(-1, d)
    w_sb_qkv_t = seg_of("sb_qkv").reshape(-1, d)
    w_sb_out = seg_of("sb_out").reshape(-1, d)
    hidden = ffn_down.shape[1] * N_DEV
    w_gate_t = seg_of("ffn_gate").reshape(N_DEV, depth, -1, d).transpose(1, 0, 2, 3).reshape(depth, hidden, d)
    w_up_t = seg_of("ffn_up").reshape(N_DEV, depth, -1, d).transpose(1, 0, 2, 3).reshape(depth, hidden, d)
    w_down = seg_of("ffn_down").reshape(N_DEV, depth, -1, d).transpose(1, 0, 2, 3).reshape(depth, hidden, d)
    exact_all = gathered[:, big_rows:big_rows + exact.shape[0]]
    scale_full = _exact_f32(exact_all[:, :scale_rows], n_scale).reshape(N_DEV, n_pool, -1)
    scale_full = scale_full.transpose(1, 0, 2).reshape(n_pool, d)
    convw_full = _exact_f32(exact_all[:, scale_rows:], n_convw).reshape(N_DEV, SSD_CONV, -1)
    convw_full = convw_full.transpose(1, 0, 2).reshape(SSD_CONV, -1)

    d_inner = w_ssd_out.shape[0]
    n_zx = w_ssd_in_t.shape[0] - ssd_dt_bias.shape[1]
    w_zx_t = w_ssd_in_t[:n_zx]
    n_ssd_heads = ssd_dt_bias.shape[1]
    n_ssd_groups = n_ssd_heads // SSD_HEADS_PER_GROUP
    w_dt_t = _ssd_group_pad(w_ssd_in_t[n_zx:].T, n_ssd_groups).T
    par = _pad_rows(_ssd_group_pad(jnp.concatenate([ssd_dt_bias, ssd_a_log, ssd_d], axis=0), n_ssd_groups), SUBLANES)
    d_rep = jnp.repeat(ssd_d[0], SSD_HEAD_DIM)[None]
    qk_gains = jnp.zeros((SUBLANES, LANES), F32).at[0].set(jnp.tile(sb_q_norm[0], 2)).at[1].set(jnp.tile(sb_k_norm[0], 2))

    act = x[0]
    saved = []
    for i in range(depth):
        kind, j = i % 3, i // 3
        gain = mix_norm[i:i + 1]
        if kind == 0:
            act, s = _pool_mixer_fwd(act, gain, w_pool_in[j], w_pool_group[j], scale_full[j:j + 1], f"l{i}")
        elif kind == 1:
            act, s = _ssd_mixer_fwd(act, gain, w_zx_t, w_dt_t, convw_full, ssd_conv_b, par, d_rep, ssd_out_norm,
                                    w_ssd_out, f"l{i}")
        else:
            act, s = _sb_mixer_fwd(act, gain, w_sb_qkv_t, qk_gains, w_sb_out, f"l{i}")
        act, f = _ffn_fwd(act, ffn_norm[i:i + 1], w_gate_t[i], w_up_t[i], w_down[i], f"l{i}")
        saved.append((s, f))
    dact, loss_cols = _loss_head(act, loss_target[0], name="loss_head")

    def layer_row(name, layer):
        return seg[name][0] + layer * (seg[name][1] // weights[name].shape[0])

    grads = jnp.zeros((N_DEV, packed_rows, d), F32)
    g_mix_norm, g_ffn_norm = [None] * depth, [None] * depth
    g_pool_group, g_pool_scale = [None] * n_pool, [None] * n_pool
    for i in reversed(range(depth)):
        kind, j = i % 3, i // 3
        gain = mix_norm[i:i + 1]
        s, f = saved[i]
        dact, grads, g_ffn_norm[i] = _ffn_bwd(
            dact, f, ffn_norm[i:i + 1], w_gate_t[i], w_up_t[i], w_down[i], grads,
            [layer_row(n, i) for n in ("ffn_gate", "ffn_up", "ffn_down")], f"l{i}")
        if kind == 0:
            dact, grads, g_pool_group[j], g_pool_scale[j], g_mix_norm[i] = _pool_mixer_bwd(
                dact, s, gain, w_pool_in[j], w_pool_group[j], scale_full[j:j + 1], grads, layer_row("pool_in", j),
                f"l{i}")
        elif kind == 1:
            (dact, grads, g_zx_t, g_dt_t, g_conv_w, g_conv_b, g_par, g_out_norm,
             g_mix_norm[i]) = _ssd_mixer_bwd(dact, s, gain, w_zx_t, w_dt_t, convw_full, ssd_conv_b, par, d_rep,
                                             ssd_out_norm, w_ssd_out, grads, layer_row("ssd_out", j), f"l{i}")
        else:
            dact, grads, g_qk_gains, g_mix_norm[i] = _sb_mixer_bwd(
                dact, s, gain, w_sb_qkv_t, qk_gains, w_sb_out, grads, layer_row("sb_qkv", j),
                layer_row("sb_out", j), f"l{i}")
    grad_x = dact[None]

    g_ssd_in = jnp.concatenate([g_zx_t, _ssd_group_unpad(g_dt_t.T, n_ssd_groups).T], axis=0)
    g_group = jnp.concatenate([_device_blocks(gg[k], d) for gg in g_pool_group for k in range(gg.shape[0])], axis=1)
    for name, blocks in (("ssd_in", _device_blocks(g_ssd_in, d)), ("pool_group", g_group)):
        grads = lax.dynamic_update_slice(grads, blocks, (0, seg[name][0], 0))
    parts = grads.reshape(N_DEV // 2, 2, packed_rows, d)
    from_sibling = _exchange_sibling(parts, name="reduce_sibling")
    chip_sums, chip_sums_wire = _add_pairs(parts, from_sibling, cs.reshape(1).astype(jnp.int32),
                                           name="reduce_sibling_add")
    from_chips = _exchange_chips(chip_sums_wire, name="reduce_chips")

    def pack_f32(tree):
        return _pad_rows(pack(tree, F32), packed_rows)

    big_out = _adamw_sharded(chip_sums, from_chips, chip.reshape(1).astype(jnp.int32), pack_f32(weights),
                             pack_f32(mom1), pack_f32(mom2), name="adamw_sharded")

    def small_pack(mix, ffn, conv_b, out_norm, scale, conv_w, vec, qk, loss=None):
        buf = jnp.zeros((SMALL_ROWS, d), F32)
        buf = buf.at[ROW_MIX_NORM:ROW_MIX_NORM + depth].set(mix).at[ROW_FFN_NORM:ROW_FFN_NORM + depth].set(ffn)
        buf = buf.at[ROW_CONV_B:ROW_CONV_B + conv_b.size // d].set(conv_b.reshape(-1, d))
        buf = buf.at[ROW_OUT_NORM:ROW_OUT_NORM + out_norm.size // d].set(out_norm.reshape(-1, d))
        buf = buf.at[ROW_POOL_SCALE:ROW_POOL_SCALE + n_pool].set(scale)
        buf = buf.at[ROW_CONV_W:ROW_CONV_W + conv_w.size // d].set(conv_w.reshape(-1, d))
        buf = buf.at[ROW_SSD_VEC].set(vec.reshape(-1)).at[ROW_QK_GAIN].set(qk.reshape(-1))
        if loss is not None:
            buf = buf.at[ROW_LOSS].set(loss.reshape(-1))
        return buf

    def small_params(tree):
        scale = lax.dynamic_update_slice(jnp.zeros((n_pool, d), F32), tree["pool_scale"],
                                         (0, dev * tree["pool_scale"].shape[1]))
        conv_w = lax.dynamic_update_slice(jnp.zeros(convw_full.shape, F32), tree["ssd_conv_w"][0],
                                          (0, dev * tree["ssd_conv_w"].shape[2]))
        vec = jnp.zeros((SUBLANES, LANES), F32)
        vec = vec.at[0, :n_ssd_heads].set(tree["ssd_dt_bias"][0]).at[1, :n_ssd_heads].set(tree["ssd_a_log"][0])
        vec = vec.at[2, :n_ssd_heads].set(tree["ssd_d"][0])
        qk = jnp.zeros((SUBLANES, LANES), F32)
        qk = qk.at[0, SB_HEAD_DIM:].set(tree["sb_q_norm"][0]).at[1, SB_HEAD_DIM:].set(tree["sb_k_norm"][0])
        return small_pack(tree["mix_norm"], tree["ffn_norm"], tree["ssd_conv_b"], tree["ssd_out_norm"], scale,
                          conv_w, vec, qk)

    small_partial = small_pack(jnp.concatenate(g_mix_norm, axis=0), jnp.concatenate(g_ffn_norm, axis=0), g_conv_b,
                               g_out_norm, jnp.concatenate(g_pool_scale, axis=0), g_conv_w,
                               jnp.zeros((SUBLANES, LANES), F32).at[:3, :n_ssd_heads].set(
                                   _ssd_group_unpad(g_par[:3], n_ssd_groups)), g_qk_gains,
                               loss_cols)
    small_all = _all_gather(small_partial, name="gather_small", in_vmem=True)
    small_out = _adamw_small(small_all, small_params(weights), small_params(mom1), small_params(mom2),
                             name="adamw_small")
    loss = small_out[0][ROW_LOSS, 0]

    def unpack(big, small):
        out = {}
        for name in BIG_WEIGHTS:
            a, n = seg[name]
            out[name] = _from_rows(name, big[a:a + n], weights[name].shape)
        out["mix_norm"] = small[ROW_MIX_NORM:ROW_MIX_NORM + depth]
        out["ffn_norm"] = small[ROW_FFN_NORM:ROW_FFN_NORM + depth]
        out["ssd_conv_b"] = small[ROW_CONV_B:ROW_CONV_B + ssd_conv_b.size // d].reshape(ssd_conv_b.shape)
        out["ssd_out_norm"] = small[ROW_OUT_NORM:ROW_OUT_NORM + ssd_out_norm.size // d].reshape(ssd_out_norm.shape)
        out["pool_scale"] = lax.dynamic_slice(small[ROW_POOL_SCALE:ROW_POOL_SCALE + n_pool],
                                              (0, dev * pool_scale.shape[1]), pool_scale.shape)
        conv_w = small[ROW_CONV_W:ROW_CONV_W + convw_full.size // d].reshape(convw_full.shape)
        out["ssd_conv_w"] = lax.dynamic_slice(conv_w, (0, dev * ssd_conv_w.shape[2]), ssd_conv_w.shape[1:])[None]
        vec = small[ROW_SSD_VEC].reshape(SUBLANES, LANES)
        out["ssd_dt_bias"], out["ssd_a_log"], out["ssd_d"] = (vec[r:r + 1, :n_ssd_heads] for r in range(3))
        qk = small[ROW_QK_GAIN].reshape(SUBLANES, LANES)
        out["sb_q_norm"], out["sb_k_norm"] = qk[0:1, SB_HEAD_DIM:], qk[1:2, SB_HEAD_DIM:]
        return [out[n] for n in names]

    results = [unpack(b, s) for b, s in zip(big_out, small_out)]
    return (loss, grad_x, *results[0], *results[1], *results[2], *results[3])
```

```python
import math

import jax
import jax.numpy as jnp
from jax import lax
from jax.experimental import pallas as pl
from jax.experimental.pallas import tpu as pltpu

F32 = jnp.float32
BF16 = jnp.bfloat16

N_DEV = 8
NORM_EPS = 1e-6
V7X_VMEM_LIMIT_BYTES = 48 * 1024 * 1024
LANES = 128
SUBLANES = 8

POOL_WINDOWS = (2, 4, 8, 16)
SSD_CHUNK = 256
SSD_HEAD_DIM = 64
SSD_STATE = 128
SSD_HEADS_PER_GROUP = 4
SSD_CONV = 4
SB_HEAD_DIM = 64
SB_BLOCK = 256
SB_QUERY_BLOCK = 256

ADAM_LR = 0.001
ADAM_B1 = 0.9
ADAM_B2 = 0.999
ADAM_EPS = 1e-08
ADAM_WD = 0.01
ADAM_STEP = 10


def _params(*sem):
    return pltpu.CompilerParams(dimension_semantics=sem, vmem_limit_bytes=V7X_VMEM_LIMIT_BYTES)


def _tile(n, cap, mult):
    best = None
    for t in range(mult, min(n, cap) + 1, mult):
        if n % t == 0:
            best = t
    return best or n


def _load_slabs(ref, slabs):
    if not slabs:
        return ref[...]
    return jnp.concatenate([ref[p] for p in range(ref.shape[0])], axis=1)


def _matmul(a, b, mode, *, name, out_dtype=F32, resid=None, a_slabs=False, out_slabs=False,
            tm_cap=1024, tn_cap=1024, tk_cap=2048):
    pairs = list(zip(a, b)) if isinstance(a, (list, tuple)) else [(a, b)]
    a, b = pairs[0]
    if a_slabs:
        m, k = a.shape[1], a.shape[0] * LANES
    else:
        m, k = a.shape
    n = b.shape[1] if mode == "nn" else b.shape[0]
    assert (b.shape[0] if mode == "nn" else b.shape[1]) == k
    assert all(pa.shape == a.shape and pb.shape == b.shape for pa, pb in pairs)
    tm, tn, tk = _tile(m, tm_cap, SUBLANES), _tile(n, tn_cap, LANES), _tile(k, tk_cap, LANES)
    nk = k // tk
    dn = (((1,), (0,)), ((), ())) if mode == "nn" else (((1,), (1,)), ((), ()))
    has_resid = resid is not None
    n_pairs = len(pairs)

    def body(*refs):
        ab_refs, rest = refs[:2 * n_pairs], refs[2 * n_pairs:]
        r_ref = rest[0] if has_resid else None
        o_ref = rest[1] if has_resid else rest[0]
        kk = pl.program_id(2)

        def partial():
            total = None
            for p in range(n_pairs):
                d = lax.dot_general(_load_slabs(ab_refs[2 * p], a_slabs).astype(BF16),
                                    ab_refs[2 * p + 1][...].astype(BF16), dn, preferred_element_type=F32)
                total = d if total is None else total + d
            return total

        def finish(r):
            if has_resid:
                r = r + r_ref[...]
            if out_slabs:
                for p in range(tn // LANES):
                    o_ref[p] = r[:, p * LANES:(p + 1) * LANES].astype(out_dtype)
            else:
                o_ref[...] = r.astype(out_dtype)

        if nk == 1:
            finish(partial())
        else:
            acc = rest[-1]

            @pl.when(kk == 0)
            def _():
                acc[...] = jnp.zeros_like(acc)

            acc[...] += partial()

            @pl.when(kk == nk - 1)
            def _():
                finish(acc[...])

    b_spec = (pl.BlockSpec((tk, tn), lambda i, j, kk: (kk, j)) if mode == "nn"
              else pl.BlockSpec((tn, tk), lambda i, j, kk: (j, kk)))
    a_spec = (pl.BlockSpec((tk // LANES, tm, LANES), lambda i, j, kk: (kk, i, 0)) if a_slabs
              else pl.BlockSpec((tm, tk), lambda i, j, kk: (i, kk)))
    in_specs = [a_spec, b_spec] * n_pairs
    args = [t for pair in pairs for t in pair]
    if has_resid:
        in_specs.append(pl.BlockSpec((tm, tn), lambda i, j, kk: (i, j)))
        args.append(resid)
    if out_slabs:
        out_spec = pl.BlockSpec((tn // LANES, tm, LANES), lambda i, j, kk: (j, i, 0))
        out_shape = jax.ShapeDtypeStruct((n // LANES, m, LANES), out_dtype)
    else:
        out_spec = pl.BlockSpec((tm, tn), lambda i, j, kk: (i, j))
        out_shape = jax.ShapeDtypeStruct((m, n), out_dtype)
    return pl.pallas_call(
        body, name=name, grid=(m // tm, n // tn, nk),
        in_specs=in_specs, out_specs=out_spec, out_shape=out_shape,
        scratch_shapes=[pltpu.VMEM((tm, tn), F32)] if nk > 1 else [],
        compiler_params=_params("parallel", "parallel", "arbitrary"),
    )(*args)


def _matmul_tn(a, b, *, name, a_slabs=False, ta_cap=1024, tb_cap=1024, tr_cap=512):
    if a_slabs:
        r, ka = a.shape[1], a.shape[0] * LANES
    else:
        r, ka = a.shape
    nb = b.shape[1]
    assert b.shape[0] == r
    ta, tb, tr = _tile(ka, ta_cap, LANES), _tile(nb, tb_cap, LANES), _tile(r, tr_cap, SUBLANES)

    def body(a_ref, b_ref, o_ref):
        @pl.when(pl.program_id(2) == 0)
        def _():
            o_ref[...] = jnp.zeros_like(o_ref)

        o_ref[...] += lax.dot_general(_load_slabs(a_ref, a_slabs).astype(BF16), b_ref[...].astype(BF16),
                                      (((0,), (0,)), ((), ())), preferred_element_type=F32)

    a_spec = (pl.BlockSpec((ta // LANES, tr, LANES), lambda i, j, kk: (i, kk, 0)) if a_slabs
              else pl.BlockSpec((tr, ta), lambda i, j, kk: (kk, i)))
    return pl.pallas_call(
        body, name=name, grid=(ka // ta, nb // tb, r // tr),
        in_specs=[a_spec, pl.BlockSpec((tr, tb), lambda i, j, kk: (kk, j))],
        out_specs=pl.BlockSpec((ta, tb), lambda i, j, kk: (i, j)),
        out_shape=jax.ShapeDtypeStruct((ka, nb), F32),
        compiler_params=_params("parallel", "parallel", "arbitrary"),
    )(a, b)


def _core_major(k):
    return (k % 2) * (N_DEV // 2) + k // 2


def _matmul_tn_into(buf, a, b, row_off, *, name, a_slabs=False, tr_cap=1024):
    if a_slabs:
        r, ka = a.shape[1], a.shape[0] * LANES
    else:
        r, ka = a.shape
    n_dev, _, n = buf.shape
    per = ka // n_dev
    assert b.shape == (r, n) and ka % n_dev == 0 and per % SUBLANES == 0 and row_off % per == 0
    tr = _tile(r, tr_cap, SUBLANES)

    def body(buf_ref, a_ref, b_ref, o_ref):
        prod = lax.dot_general(_load_slabs(a_ref, a_slabs).astype(BF16), b_ref[...].astype(BF16),
                               (((0,), (0,)), ((), ())), preferred_element_type=F32)
        @pl.when(pl.program_id(0) == 0)
        def _():
            for k in range(n_dev):
                o_ref[_core_major(k)] = prod[k * per:(k + 1) * per]

        @pl.when(pl.program_id(0) > 0)
        def _():
            for k in range(n_dev):
                o_ref[_core_major(k)] += prod[k * per:(k + 1) * per]

    a_spec = (pl.BlockSpec((ka // LANES, tr, LANES), lambda i: (0, i, 0)) if a_slabs
              else pl.BlockSpec((tr, ka), lambda i: (i, 0)))
    return pl.pallas_call(
        body, name=name, grid=(r // tr,),
        in_specs=[pl.BlockSpec(memory_space=pl.ANY), a_spec, pl.BlockSpec((tr, n), lambda i: (i, 0))],
        out_specs=pl.BlockSpec((n_dev, per, n), lambda i: (0, row_off // per, 0)),
        out_shape=jax.ShapeDtypeStruct(buf.shape, F32),
        input_output_aliases={0: 0},
        compiler_params=_params("arbitrary"),
    )(buf, a, b)


def _rms_fwd(x, gain, *, name):
    t, d = x.shape
    tm = _tile(t, 512, SUBLANES)

    def body(x_ref, g_ref, o_ref):
        xv = x_ref[...]
        r = lax.rsqrt(jnp.mean(xv * xv, axis=-1, keepdims=True) + NORM_EPS)
        o_ref[...] = (xv * r * g_ref[...]).astype(BF16)

    return pl.pallas_call(
        body, name=name, grid=(t // tm,),
        in_specs=[pl.BlockSpec((tm, d), lambda i: (i, 0)), pl.BlockSpec((1, d), lambda i: (0, 0))],
        out_specs=pl.BlockSpec((tm, d), lambda i: (i, 0)),
        out_shape=jax.ShapeDtypeStruct((t, d), BF16),
        compiler_params=_params("parallel"),
    )(x, gain)


def _rms_bwd(x, gain, dh, dres, *, name):
    t, d = x.shape
    tm = _tile(t, 512, SUBLANES)

    def body(x_ref, g_ref, dh_ref, dres_ref, dx_ref, dg_ref):
        @pl.when(pl.program_id(0) == 0)
        def _():
            dg_ref[...] = jnp.zeros_like(dg_ref)

        xv = x_ref[...]
        r = lax.rsqrt(jnp.mean(xv * xv, axis=-1, keepdims=True) + NORM_EPS)
        xhat = xv * r
        dhv = dh_ref[...]
        u = dhv * g_ref[...]
        dx_ref[...] = dres_ref[...] + r * (u - xhat * jnp.mean(u * xhat, axis=-1, keepdims=True))
        dg_ref[...] += jnp.sum(dhv * xhat, axis=0, keepdims=True)

    return pl.pallas_call(
        body, name=name, grid=(t // tm,),
        in_specs=[pl.BlockSpec((tm, d), lambda i: (i, 0)), pl.BlockSpec((1, d), lambda i: (0, 0)),
                  pl.BlockSpec((tm, d), lambda i: (i, 0)), pl.BlockSpec((tm, d), lambda i: (i, 0))],
        out_specs=[pl.BlockSpec((tm, d), lambda i: (i, 0)), pl.BlockSpec((1, d), lambda i: (0, 0))],
        out_shape=[jax.ShapeDtypeStruct((t, d), F32), jax.ShapeDtypeStruct((1, d), F32)],
        compiler_params=_params("arbitrary"),
    )(x, gain, dh, dres)


def _loss_head(y, target, *, name):
    t, d = y.shape
    tm = _tile(t, 512, SUBLANES)

    def body(y_ref, t_ref, dy_ref, l_ref):
        @pl.when(pl.program_id(0) == 0)
        def _():
            l_ref[...] = jnp.zeros_like(l_ref)

        e = y_ref[...] - t_ref[...]
        dy_ref[...] = e * (1.0 / d)
        l_ref[...] += jnp.sum(e * e, axis=0, keepdims=True) * (0.5 / d)

    return pl.pallas_call(
        body, name=name, grid=(t // tm,),
        in_specs=[pl.BlockSpec((tm, d), lambda i: (i, 0)), pl.BlockSpec((tm, d), lambda i: (i, 0))],
        out_specs=[pl.BlockSpec((tm, d), lambda i: (i, 0)), pl.BlockSpec((1, d), lambda i: (0, 0))],
        out_shape=[jax.ShapeDtypeStruct((t, d), F32), jax.ShapeDtypeStruct((1, d), F32)],
        compiler_params=_params("arbitrary"),
    )(y, target)


def _sigmoid(v):
    return 0.5 * jnp.tanh(0.5 * v) + 0.5


FFN_TOKEN_TILE = 512
FFN_HIDDEN_TILE = 1408
NT_DIMS = (((1,), (1,)), ((), ()))


def _ffn_up(h, w_gate_t, w_up_t, *, name):
    t, d = h.shape
    f = w_gate_t.shape[0]
    tm, tn = _tile(t, FFN_TOKEN_TILE, SUBLANES), _tile(f, FFN_HIDDEN_TILE, LANES)

    def body(h_ref, g_ref, u_ref, s_ref, a_ref, b_ref):
        hv = h_ref[...].astype(BF16)
        av = lax.dot_general(hv, g_ref[...].astype(BF16), NT_DIMS, preferred_element_type=F32)
        bv = lax.dot_general(hv, u_ref[...].astype(BF16), NT_DIMS, preferred_element_type=F32)
        s_ref[...] = (av * _sigmoid(av) * bv).astype(BF16)
        a_ref[...] = av.astype(BF16)
        b_ref[...] = bv.astype(BF16)

    w_spec = pl.BlockSpec((tn, d), lambda j, i: (j, 0))
    out_spec = pl.BlockSpec((tm, tn), lambda j, i: (i, j))
    out = jax.ShapeDtypeStruct((t, f), BF16)
    return pl.pallas_call(
        body, name=name, grid=(f // tn, t // tm),
        in_specs=[pl.BlockSpec((tm, d), lambda j, i: (i, 0)), w_spec, w_spec],
        out_specs=[out_spec, out_spec, out_spec], out_shape=[out, out, out],
        compiler_params=_params("parallel", "parallel"),
    )(h, w_gate_t, w_up_t)


def _ffn_dact(dx, w_down, a, b, *, name):
    t, d = dx.shape
    f = w_down.shape[0]
    tm, tn = _tile(t, FFN_TOKEN_TILE, SUBLANES), _tile(f, FFN_HIDDEN_TILE, LANES)

    def body(dx_ref, w_ref, a_ref, b_ref, da_ref, db_ref):
        ds = lax.dot_general(dx_ref[...].astype(BF16), w_ref[...].astype(BF16), NT_DIMS, preferred_element_type=F32)
        av = a_ref[...].astype(F32)
        sg = _sigmoid(av)
        da_ref[...] = (ds * b_ref[...].astype(F32) * (sg * (1.0 + av * (1.0 - sg)))).astype(BF16)
        db_ref[...] = (ds * av * sg).astype(BF16)

    blk = pl.BlockSpec((tm, tn), lambda j, i: (i, j))
    out = jax.ShapeDtypeStruct((t, f), BF16)
    return pl.pallas_call(
        body, name=name, grid=(f // tn, t // tm),
        in_specs=[pl.BlockSpec((tm, d), lambda j, i: (i, 0)), pl.BlockSpec((tn, d), lambda j, i: (j, 0)), blk, blk],
        out_specs=[blk, blk], out_shape=[out, out],
        compiler_params=_params("parallel", "parallel"),
    )(dx, w_down, a, b)


def _ffn_fwd(x, gain, w_gate_t, w_up_t, w_down, tag):
    h = _rms_fwd(x, gain, name=f"ffn_norm_{tag}")
    s, a, b = _ffn_up(h, w_gate_t, w_up_t, name=f"ffn_up_{tag}")
    x_new = _matmul(s, w_down, "nn", resid=x, tn_cap=1024, tk_cap=2816, name=f"ffn_down_{tag}")
    return x_new, (x, h, a, b, s)


def _ffn_bwd(dx, saved, gain, w_gate_t, w_up_t, w_down, grads, rows, tag):
    x, h, a, b, s = saved
    da, db = _ffn_dact(dx, w_down, a, b, name=f"ffn_dact_{tag}")
    grads = _matmul_tn_into(grads, da, h, rows[0], name=f"ffn_dwgate_{tag}")
    grads = _matmul_tn_into(grads, db, h, rows[1], name=f"ffn_dwup_{tag}")
    grads = _matmul_tn_into(grads, s, dx, rows[2], name=f"ffn_dwdown_{tag}")
    dh = _matmul([da, db], [w_gate_t, w_up_t], "nn", tm_cap=512, tn_cap=1024, tk_cap=2816, name=f"ffn_dh_{tag}")
    dx_in, dgain = _rms_bwd(x, gain, dh, dx, name=f"ffn_dnorm_{tag}")
    return dx_in, grads, dgain


POOL_HALO = 16


def _shift_rows(v, k):
    n = v.shape[0]
    return pltpu.roll(v, k % n, 0)


def _window_sum(v, w, direction):
    k = 1
    while k < w:
        v = v + _shift_rows(v, direction * k)
        k *= 2
    return v


def _pool_fwd(u, x, w_group, scale, *, name):
    t, d = u.shape
    ng, dg = w_group.shape[0], w_group.shape[1]
    tm = _tile(t, 512, POOL_HALO)
    hb = tm // POOL_HALO

    def body(u_ref, halo_ref, x_ref, w_ref, s_ref, xo_ref, p_ref, y_ref):
        i, g = pl.program_id(0), pl.program_id(1)
        halo = jnp.where(i > 0, halo_ref[...], 0.0)
        ext = jnp.concatenate([halo, u_ref[...]], axis=0)
        pos = i * tm + lax.broadcasted_iota(jnp.int32, (tm, 1), 0)
        for gi, win in enumerate(POOL_WINDOWS):
            @pl.when(g == gi)
            def _(win=win):
                tot = _window_sum(ext, win, 1)[POOL_HALO:]
                cnt = jnp.minimum(pos + 1, win).astype(F32)
                p = (tot / cnt - u_ref[...]).astype(BF16)
                p_ref[...] = p
                y = jnp.dot(p, w_ref[...].astype(BF16), preferred_element_type=F32)
                y_ref[...] = y
                xo_ref[...] = x_ref[...] + y * s_ref[...]

    blk = pl.BlockSpec((tm, dg), lambda i, g: (i, g))
    return pl.pallas_call(
        body, name=name, grid=(t // tm, ng),
        in_specs=[blk, pl.BlockSpec((POOL_HALO, dg), lambda i, g: (jnp.maximum(i * hb - 1, 0), g)), blk,
                  pl.BlockSpec((None, dg, dg), lambda i, g: (g, 0, 0)), pl.BlockSpec((1, dg), lambda i, g: (0, g))],
        out_specs=[blk, blk, blk],
        out_shape=[jax.ShapeDtypeStruct((t, d), F32), jax.ShapeDtypeStruct((t, d), BF16),
                   jax.ShapeDtypeStruct((t, d), F32)],
        compiler_params=_params("parallel", "parallel"),
    )(u, u, x, w_group, scale)


def _pool_bwd(dx, p, y_pre, w_group, scale, *, name):
    t, d = dx.shape
    ng, dg = w_group.shape[0], w_group.shape[1]
    tm = _tile(t, 512, POOL_HALO)
    hb = tm // POOL_HALO
    nt = t // tm

    def body(dx_ref, nxt_ref, p_ref, y_ref, w_ref, s_ref, du_ref, dw_ref, ds_ref):
        g, i = pl.program_id(0), pl.program_id(1)

        @pl.when(i == 0)
        def _():
            dw_ref[...] = jnp.zeros_like(dw_ref)
            ds_ref[...] = jnp.zeros_like(ds_ref)

        dxv = dx_ref[...]
        ds_ref[...] += jnp.sum(dxv * y_ref[...], axis=0, keepdims=True)
        nxt = jnp.where(i < nt - 1, nxt_ref[...], 0.0)
        dyp = (jnp.concatenate([dxv, nxt], axis=0) * s_ref[...]).astype(BF16)
        dw_ref[...] += lax.dot_general(p_ref[...], dyp[:tm], (((0,), (0,)), ((), ())), preferred_element_type=F32)
        dp = lax.dot_general(dyp, w_ref[...].astype(BF16), (((1,), (1,)), ((), ())), preferred_element_type=F32)
        pos = i * tm + lax.broadcasted_iota(jnp.int32, (tm + POOL_HALO, 1), 0)
        for gi, win in enumerate(POOL_WINDOWS):
            @pl.when(g == gi)
            def _(win=win):
                q = dp / jnp.minimum(pos + 1, win).astype(F32)
                du_ref[...] = (_window_sum(q, win, -1)[:tm] - dp[:tm]).astype(BF16)

    blk = pl.BlockSpec((tm, dg), lambda g, i: (i, g))
    return pl.pallas_call(
        body, name=name, grid=(ng, nt),
        in_specs=[blk, pl.BlockSpec((POOL_HALO, dg), lambda g, i: (jnp.minimum((i + 1) * hb, t // POOL_HALO - 1), g)),
                  blk, blk, pl.BlockSpec((None, dg, dg), lambda g, i: (g, 0, 0)),
                  pl.BlockSpec((1, dg), lambda g, i: (0, g))],
        out_specs=[blk, pl.BlockSpec((None, dg, dg), lambda g, i: (g, 0, 0)), pl.BlockSpec((1, dg), lambda g, i: (0, g))],
        out_shape=[jax.ShapeDtypeStruct((t, d), BF16), jax.ShapeDtypeStruct((ng, dg, dg), F32),
                   jax.ShapeDtypeStruct((1, d), F32)],
        compiler_params=_params("parallel", "arbitrary"),
    )(dx, dx, p, y_pre, w_group, scale)


def _pool_mixer_fwd(x, gain, w_in, w_group, scale, tag):
    h = _rms_fwd(x, gain, name=f"pool_norm_{tag}")
    u = _matmul(h, w_in, "nn", name=f"pool_in_{tag}")
    x_new, p, y_pre = _pool_fwd(u, x, w_group, scale, name=f"pool_mix_{tag}")
    return x_new, (x, h, p, y_pre)


def _pool_mixer_bwd(dx, saved, gain, w_in, w_group, scale, grads, row_in, tag):
    x, h, p, y_pre = saved
    du, dw_group, dscale = _pool_bwd(dx, p, y_pre, w_group, scale, name=f"pool_dmix_{tag}")
    grads = _matmul_tn_into(grads, h, du, row_in, name=f"pool_dwin_{tag}")
    dh = _matmul(du, w_in, "nt", name=f"pool_dh_{tag}")
    dx_in, dgain = _rms_bwd(x, gain, dh, dx, name=f"pool_dnorm_{tag}")
    return dx_in, grads, dw_group, dscale, dgain


CONV_HALO = 8
HIGHEST = lax.Precision.HIGHEST
NEG_BIG = -1e30


def _softplus(v):
    return jnp.maximum(v, 0.0) + jnp.log(1.0 + jnp.exp(-jnp.abs(v)))


def _dot_exact(a, b):
    return jnp.dot(a, b, precision=HIGHEST, preferred_element_type=F32)


def _conv_taps(ext, w_ref, off, rows):
    acc = None
    for k in range(SSD_CONV):
        shift = SSD_CONV - 1 - k
        v = (_shift_rows(ext, shift) if shift else ext)[off:off + rows] * w_ref[k:k + 1, :]
        acc = v if acc is None else acc + v
    return acc


def _ssd_conv_fwd(zx, conv_w, conv_b, col0, *, name):
    t = zx.shape[0]
    c = conv_w.shape[1]
    tm, tc = _tile(t, 512, CONV_HALO), _tile(c, 512, LANES)
    hb, cb0 = tm // CONV_HALO, col0 // tc
    assert col0 % tc == 0

    def body(x_ref, halo_ref, w_ref, b_ref, o_ref):
        halo = jnp.where(pl.program_id(0) > 0, halo_ref[...], 0.0)
        ext = jnp.concatenate([halo, x_ref[...]], axis=0)
        pre = _conv_taps(ext, w_ref, CONV_HALO, tm) + b_ref[...]
        o_ref[...] = pre * _sigmoid(pre)

    return pl.pallas_call(
        body, name=name, grid=(t // tm, c // tc),
        in_specs=[pl.BlockSpec((tm, tc), lambda i, j: (i, j + cb0)),
                  pl.BlockSpec((CONV_HALO, tc), lambda i, j: (jnp.maximum(i * hb - 1, 0), j + cb0)),
                  pl.BlockSpec((SSD_CONV, tc), lambda i, j: (0, j)), pl.BlockSpec((1, tc), lambda i, j: (0, j))],
        out_specs=pl.BlockSpec((tm, tc), lambda i, j: (i, j)),
        out_shape=jax.ShapeDtypeStruct((t, c), F32),
        compiler_params=_params("parallel", "parallel"),
    )(zx, zx, conv_w, conv_b)


def _ssd_conv_bwd(dxa, zx, conv_w, conv_b, col0, *, name):
    t = zx.shape[0]
    c = conv_w.shape[1]
    tm, tc = _tile(t, 512, CONV_HALO), _tile(c, 512, LANES)
    hb, cb0, nt = tm // CONV_HALO, col0 // tc, t // tm
    last_halo = t // CONV_HALO - 1

    def body(x_ref, prev_ref, nxt_ref, d_ref, dnxt_ref, w_ref, b_ref, dx_ref, dw_ref, db_ref):
        i = pl.program_id(1)

        @pl.when(i == 0)
        def _():
            dw_ref[...] = jnp.zeros_like(dw_ref)
            db_ref[...] = jnp.zeros_like(db_ref)

        prev = jnp.where(i > 0, prev_ref[...], 0.0)
        has_next = i < nt - 1
        ext = jnp.concatenate([prev, x_ref[...], jnp.where(has_next, nxt_ref[...], 0.0)], axis=0)
        pre = _conv_taps(ext, w_ref, CONV_HALO, tm + CONV_HALO) + b_ref[...]
        sg = _sigmoid(pre)
        dact = jnp.concatenate([d_ref[...], jnp.where(has_next, dnxt_ref[...], 0.0)], axis=0)
        dpre = dact * (sg * (1.0 + pre * (1.0 - sg)))
        db_ref[...] += jnp.sum(dpre[:tm], axis=0, keepdims=True)
        acc = None
        for k in range(SSD_CONV):
            shift = SSD_CONV - 1 - k
            src = (_shift_rows(ext, shift) if shift else ext)[CONV_HALO:CONV_HALO + tm]
            dw_ref[k:k + 1, :] += jnp.sum(dpre[:tm] * src, axis=0, keepdims=True)
            v = (_shift_rows(dpre, -shift) if shift else dpre)[:tm] * w_ref[k:k + 1, :]
            acc = v if acc is None else acc + v
        dx_ref[...] = acc.astype(BF16)

    main = lambda j, i: (i, j + cb0)
    return pl.pallas_call(
        body, name=name, grid=(c // tc, nt),
        in_specs=[pl.BlockSpec((tm, tc), main),
                  pl.BlockSpec((CONV_HALO, tc), lambda j, i: (jnp.maximum(i * hb - 1, 0), j + cb0)),
                  pl.BlockSpec((CONV_HALO, tc), lambda j, i: (jnp.minimum((i + 1) * hb, last_halo), j + cb0)),
                  pl.BlockSpec((tm, tc), lambda j, i: (i, j)),
                  pl.BlockSpec((CONV_HALO, tc), lambda j, i: (jnp.minimum((i + 1) * hb, last_halo), j)),
                  pl.BlockSpec((SSD_CONV, tc), lambda j, i: (0, j)), pl.BlockSpec((1, tc), lambda j, i: (0, j))],
        out_specs=[pl.BlockSpec((tm, tc), lambda j, i: (i, j)), pl.BlockSpec((SSD_CONV, tc), lambda j, i: (0, j)),
                   pl.BlockSpec((1, tc), lambda j, i: (0, j))],
        out_shape=[jax.ShapeDtypeStruct((t, c), BF16), jax.ShapeDtypeStruct((SSD_CONV, c), F32),
                   jax.ShapeDtypeStruct((1, c), F32)],
        compiler_params=_params("parallel", "arbitrary"),
    )(zx, zx, zx, dxa, dxa, conv_w, conv_b)


SSD_CUMSUM_PIECES = 3
SSD_GROUPS_PER_STEP = 1


def _ssd_group_pad(v, n_groups):
    lead = v.shape[:-1]
    v = v.reshape(*lead, n_groups, SSD_HEADS_PER_GROUP)
    v = jnp.pad(v, [(0, 0)] * (len(lead) + 1) + [(0, LANES - SSD_HEADS_PER_GROUP)])
    return v.reshape(*lead, n_groups * LANES)


def _ssd_group_unpad(v, n_groups):
    lead = v.shape[:-1]
    return v.reshape(*lead, n_groups, LANES)[..., :SSD_HEADS_PER_GROUP].reshape(*lead, -1)


def _ssd_chunk_common(dtp_ref, par_ref):
    ell = SSD_CHUNK
    dt = _softplus(dtp_ref[...] + par_ref[0:1, :])
    a = -jnp.exp(par_ref[1:2, :])
    row = lax.broadcasted_iota(jnp.int32, (ell, ell), 0)
    col = lax.broadcasted_iota(jnp.int32, (ell, ell), 1)
    acum = _split_dot(dt * a, (row >= col).astype(BF16), SSD_CUMSUM_PIECES, left=True)
    return dt, a, acum, acum.T, row, col


def _ssd_scan_fwd(xa, dtp, par, n_groups, *, name):
    t = xa.shape[0]
    ell, hd, hpg, ns, gps = SSD_CHUNK, SSD_HEAD_DIM, SSD_HEADS_PER_GROUP, SSD_STATE, SSD_GROUPS_PER_STEP
    gw = hpg * hd
    nc = t // ell
    b_blk0, c_blk0 = n_groups * gw // (ns * gps), (n_groups * gw // ns + n_groups) // gps

    def body(xs_ref, b_ref, c_ref, dtp_ref, par_ref, y_ref, sin_ref, st):
        @pl.when(pl.program_id(1) == 0)
        def _():
            st[...] = jnp.zeros_like(st)

        dt, _, acum, acum_t, row, col = _ssd_chunk_common(dtp_ref, par_ref)
        for gi in range(gps):
            bb = b_ref[:, gi * ns:(gi + 1) * ns].astype(BF16)
            cc = c_ref[:, gi * ns:(gi + 1) * ns].astype(BF16)
            cb = lax.dot_general(cc, bb, NT_DIMS, preferred_element_type=F32)
            s_all = st[gi]
            sin_ref[gi] = s_all
            c_s = lax.dot_general(cc, s_all.astype(BF16), NT_DIMS, preferred_element_type=F32)
            weighted, keep = [], []
            for hh in range(hpg):
                lanes = slice(gi * gw + hh * hd, gi * gw + (hh + 1) * hd)
                hl = gi * LANES + hh
                col_a, row_a = acum[:, hl:hl + 1], acum_t[hl:hl + 1, :]
                decay = jnp.exp(jnp.where(row >= col, col_a - row_a, NEG_BIG))
                xdt = xs_ref[:, lanes] * dt[:, hl:hl + 1]
                y = jnp.dot((cb * decay).astype(BF16), xdt.astype(BF16), preferred_element_type=F32)
                y_ref[:, lanes] = y + jnp.exp(col_a) * c_s[:, hh * hd:(hh + 1) * hd]
                a_last = acum[ell - 1:ell, hl:hl + 1]
                weighted.append((xdt * jnp.exp(a_last - col_a)).astype(BF16))
                keep.append(jnp.broadcast_to(jnp.exp(a_last), (hd, 1)))
            st[gi] = jnp.concatenate(keep, axis=0) * s_all + lax.dot_general(
                jnp.concatenate(weighted, axis=1), bb, (((0,), (0,)), ((), ())), preferred_element_type=F32)

    return pl.pallas_call(
        body, name=name, grid=(n_groups // gps, nc),
        in_specs=[pl.BlockSpec((ell, gps * gw), lambda g, c: (c, g)),
                  pl.BlockSpec((ell, gps * ns), lambda g, c: (c, b_blk0 + g)),
                  pl.BlockSpec((ell, gps * ns), lambda g, c: (c, c_blk0 + g)),
                  pl.BlockSpec((ell, gps * LANES), lambda g, c: (c, g)),
                  pl.BlockSpec((SUBLANES, gps * LANES), lambda g, c: (0, g))],
        out_specs=[pl.BlockSpec((ell, gps * gw), lambda g, c: (c, g)),
                   pl.BlockSpec((None, gps, gw, ns), lambda g, c: (c, g, 0, 0))],
        out_shape=[jax.ShapeDtypeStruct((t, n_groups * gw), F32),
                   jax.ShapeDtypeStruct((nc, n_groups, gw, ns), F32)],
        scratch_shapes=[pltpu.VMEM((gps, gw, ns), F32)],
        compiler_params=_params("parallel", "arbitrary"),
    )(xa, xa, xa, dtp, par)


def _ssd_scan_bwd(dy, xa, dtp, par, s_in, n_groups, *, name):
    t = xa.shape[0]
    ell, hd, hpg, ns, gps = SSD_CHUNK, SSD_HEAD_DIM, SSD_HEADS_PER_GROUP, SSD_STATE, SSD_GROUPS_PER_STEP
    gw = hpg * hd
    nc = t // ell
    b_blk0, c_blk0 = n_groups * gw // (ns * gps), (n_groups * gw // ns + n_groups) // gps
    nt_dims = (((1,), (1,)), ((), ()))
    tn_dims = (((0,), (0,)), ((), ()))

    def body(dy_ref, xs_ref, b_ref, c_ref, dtp_ref, par_ref, sin_ref,
             dxs_ref, db_ref, dc_ref, ddtp_ref, dpar_ref, dst):
        @pl.when(pl.program_id(1) == 0)
        def _():
            dst[...] = jnp.zeros_like(dst)
            dpar_ref[...] = jnp.zeros_like(dpar_ref)

        dtg, a_g, acum, acum_t, row, col = _ssd_chunk_common(dtp_ref, par_ref)
        lane = lax.broadcasted_iota(jnp.int32, (1, gps * LANES), 1)
        dacum = jnp.zeros((ell, gps * LANES), F32)
        xsum = jnp.zeros((ell, gps * LANES), F32)
        dsum = jnp.zeros((1, gps * LANES), F32)
        for gi, hh in [(gi, hh) for gi in range(gps) for hh in range(hpg)]:
            if hh == 0:
                bb = b_ref[:, gi * ns:(gi + 1) * ns].astype(BF16)
                cc = c_ref[:, gi * ns:(gi + 1) * ns].astype(BF16)
                cb = lax.dot_general(cc, bb, nt_dims, preferred_element_type=F32)
                cb_t = lax.dot_general(bb, cc, nt_dims, preferred_element_type=F32)
                dcb = jnp.zeros((ell, ell), F32)
                dcb_t = jnp.zeros((ell, ell), F32)
                s_all, ds_all = sin_ref[gi], dst[gi]
                c_s_all = lax.dot_general(cc, s_all.astype(BF16), nt_dims, preferred_element_type=F32)
                b_ds_all = lax.dot_general(bb, ds_all.astype(BF16), nt_dims, preferred_element_type=F32)
                s_ds = jnp.sum(s_all * ds_all, axis=1, keepdims=True)
                dy_decayed, x_weighted, keep = [], [], []
            lanes = slice(gi * gw + hh * hd, gi * gw + (hh + 1) * hd)
            head = slice(hh * hd, (hh + 1) * hd)
            hl = gi * LANES + hh
            onehot = (lane == hl).astype(F32)
            col_a, row_a = acum[:, hl:hl + 1], acum_t[hl:hl + 1, :]
            decay = jnp.exp(jnp.where(row >= col, col_a - row_a, NEG_BIG))
            decay_t = jnp.exp(jnp.where(col >= row, row_a - col_a, NEG_BIG))
            e_col = jnp.exp(col_a)
            a_last = acum[ell - 1:ell, hl:hl + 1]
            w = jnp.exp(a_last - col_a)
            e_last = jnp.exp(a_last)
            xs_h, dy_h = xs_ref[:, lanes], dy_ref[:, lanes]
            dt_h = dtg[:, hl:hl + 1]
            xdt = xs_h * dt_h
            xdt_b, dy_b = xdt.astype(BF16), dy_h.astype(BF16)
            dm_decay = lax.dot_general(dy_b, xdt_b, nt_dims, preferred_element_type=F32) * decay
            dm_decay_t = lax.dot_general(xdt_b, dy_b, nt_dims, preferred_element_type=F32) * decay_t
            dcb += dm_decay
            dcb_t += dm_decay_t
            m_t = cb_t * decay_t
            dac = jnp.sum(dm_decay * cb, axis=1, keepdims=True) - jnp.sum(dm_decay_t * cb_t, axis=1, keepdims=True)
            b_ds = b_ds_all[:, head]
            dxdt = jnp.dot(m_t.astype(BF16), dy_b, preferred_element_type=F32) + w * b_ds
            dac += jnp.sum(dy_h * c_s_all[:, head], axis=1, keepdims=True) * e_col
            q = jnp.sum(xdt * b_ds, axis=1, keepdims=True) * w
            dac -= q
            d_last = jnp.sum(q, axis=0, keepdims=True) + e_last * jnp.sum(s_ds[head], axis=0, keepdims=True)
            is_last = lax.broadcasted_iota(jnp.int32, (ell, 1), 0) == ell - 1
            dac += jnp.where(is_last, d_last, 0.0)
            dacum += dac * onehot
            dy_decayed.append((dy_h * e_col).astype(BF16))
            x_weighted.append((xdt * w).astype(BF16))
            keep.append(jnp.broadcast_to(e_last, (hd, 1)))
            dxs_ref[:, lanes] = dxdt * dt_h + dy_h * par_ref[2:3, hl:hl + 1]
            xsum += jnp.sum(dxdt * xs_h, axis=1, keepdims=True) * onehot
            dsum += jnp.sum(jnp.sum(dy_h * xs_h, axis=1, keepdims=True), axis=0, keepdims=True) * onehot
            if hh == hpg - 1:
                group = slice(gi * ns, (gi + 1) * ns)
                dy_all, x_all = jnp.concatenate(dy_decayed, axis=1), jnp.concatenate(x_weighted, axis=1)
                dc_ref[:, group] = (jnp.dot(dy_all, s_all.astype(BF16), preferred_element_type=F32)
                                    + jnp.dot(dcb.astype(BF16), bb, preferred_element_type=F32))
                db_ref[:, group] = (jnp.dot(x_all, ds_all.astype(BF16), preferred_element_type=F32)
                                    + jnp.dot(dcb_t.astype(BF16), cc, preferred_element_type=F32))
                dst[gi] = jnp.concatenate(keep, axis=0) * ds_all + lax.dot_general(
                    dy_all, cc, tn_dims, preferred_element_type=F32)
        dda = _split_dot(dacum, (col >= row).astype(BF16), SSD_CUMSUM_PIECES, left=True)
        ddtp = (xsum + dda * a_g) * _sigmoid(dtp_ref[...] + par_ref[0:1, :])
        ddtp_ref[...] = ddtp
        dpar_ref[0:1, :] += jnp.sum(ddtp, axis=0, keepdims=True)
        dpar_ref[1:2, :] += jnp.sum(dda * dtg, axis=0, keepdims=True) * a_g
        dpar_ref[2:3, :] += dsum

    rev = lambda i: nc - 1 - i
    return pl.pallas_call(
        body, name=name, grid=(n_groups // gps, nc),
        in_specs=[pl.BlockSpec((ell, gps * gw), lambda g, i: (rev(i), g)),
                  pl.BlockSpec((ell, gps * gw), lambda g, i: (rev(i), g)),
                  pl.BlockSpec((ell, gps * ns), lambda g, i: (rev(i), b_blk0 + g)),
                  pl.BlockSpec((ell, gps * ns), lambda g, i: (rev(i), c_blk0 + g)),
                  pl.BlockSpec((ell, gps * LANES), lambda g, i: (rev(i), g)),
                  pl.BlockSpec((SUBLANES, gps * LANES), lambda g, i: (0, g)),
                  pl.BlockSpec((None, gps, gw, ns), lambda g, i: (rev(i), g, 0, 0))],
        out_specs=[pl.BlockSpec((ell, gps * gw), lambda g, i: (rev(i), g)),
                   pl.BlockSpec((ell, gps * ns), lambda g, i: (rev(i), g)),
                   pl.BlockSpec((ell, gps * ns), lambda g, i: (rev(i), g)),
                   pl.BlockSpec((ell, gps * LANES), lambda g, i: (rev(i), g)),
                   pl.BlockSpec((SUBLANES, gps * LANES), lambda g, i: (0, g))],
        out_shape=[jax.ShapeDtypeStruct((t, n_groups * gw), F32), jax.ShapeDtypeStruct((t, n_groups * ns), F32),
                   jax.ShapeDtypeStruct((t, n_groups * ns), F32), jax.ShapeDtypeStruct((t, n_groups * LANES), F32),
                   jax.ShapeDtypeStruct((SUBLANES, n_groups * LANES), F32)],
        scratch_shapes=[pltpu.VMEM((gps, gw, ns), F32)],
        compiler_params=_params("parallel", "arbitrary"),
    )(dy, xa, xa, xa, dtp, par, s_in)


def _ssd_gate_fwd(y, xa, zx, d_rep, out_norm, *, name):
    t, di = y.shape
    gw = SSD_HEADS_PER_GROUP * SSD_HEAD_DIM
    tm = _tile(t, 512, SUBLANES)

    def body(y_ref, xs_ref, z_ref, d_ref, n_ref, o_ref):
        zv = z_ref[...]
        gt = (y_ref[...] + d_ref[...] * xs_ref[...]) * (zv * _sigmoid(zv))
        r = lax.rsqrt(jnp.mean(gt * gt, axis=-1, keepdims=True) + NORM_EPS)
        o_ref[...] = (gt * r * n_ref[...]).astype(BF16)

    blk = pl.BlockSpec((tm, gw), lambda i, g: (i, g))
    vec = pl.BlockSpec((1, gw), lambda i, g: (0, g))
    return pl.pallas_call(
        body, name=name, grid=(t // tm, di // gw),
        in_specs=[blk, blk, blk, vec, vec], out_specs=blk,
        out_shape=jax.ShapeDtypeStruct((t, di), BF16),
        compiler_params=_params("parallel", "parallel"),
    )(y, xa, zx, d_rep, out_norm)


def _ssd_gate_bwd(dgn, y, xa, zx, d_rep, out_norm, *, name):
    t, di = y.shape
    gw = SSD_HEADS_PER_GROUP * SSD_HEAD_DIM
    tm = _tile(t, 512, SUBLANES)

    def body(dg_ref, y_ref, xs_ref, z_ref, d_ref, n_ref, dy_ref, dz_ref, dn_ref):
        @pl.when(pl.program_id(1) == 0)
        def _():
            dn_ref[...] = jnp.zeros_like(dn_ref)

        zv = z_ref[...]
        sg = _sigmoid(zv)
        sz = zv * sg
        y2 = y_ref[...] + d_ref[...] * xs_ref[...]
        gt = y2 * sz
        r = lax.rsqrt(jnp.mean(gt * gt, axis=-1, keepdims=True) + NORM_EPS)
        ghat = gt * r
        dgv = dg_ref[...]
        dn_ref[...] += jnp.sum(dgv * ghat, axis=0, keepdims=True)
        u = dgv * n_ref[...]
        dgt = r * (u - ghat * jnp.mean(u * ghat, axis=-1, keepdims=True))
        dy_ref[...] = dgt * sz
        dz_ref[...] = (dgt * y2 * (sg * (1.0 + zv * (1.0 - sg)))).astype(BF16)

    blk = pl.BlockSpec((tm, gw), lambda g, i: (i, g))
    vec = pl.BlockSpec((1, gw), lambda g, i: (0, g))
    return pl.pallas_call(
        body, name=name, grid=(di // gw, t // tm),
        in_specs=[blk, blk, blk, blk, vec, vec], out_specs=[blk, blk, vec],
        out_shape=[jax.ShapeDtypeStruct((t, di), F32), jax.ShapeDtypeStruct((t, di), BF16),
                   jax.ShapeDtypeStruct((1, di), F32)],
        compiler_params=_params("parallel", "arbitrary"),
    )(dgn, y, xa, zx, d_rep, out_norm)


def _ssd_mixer_fwd(x, gain, w_zx_t, w_dt_t, conv_w, conv_b, par, d_rep, out_norm, w_out, tag):
    di = w_out.shape[0]
    n_groups = di // (SSD_HEADS_PER_GROUP * SSD_HEAD_DIM)
    h = _rms_fwd(x, gain, name=f"ssd_norm_{tag}")
    zx = _matmul(h, w_zx_t, "nt", name=f"ssd_in_{tag}")
    dtp = _matmul(h, w_dt_t, "nt", name=f"ssd_dt_{tag}")
    xa = _ssd_conv_fwd(zx, conv_w, conv_b, di, name=f"ssd_conv_{tag}")
    y, s_in = _ssd_scan_fwd(xa, dtp, par, n_groups, name=f"ssd_scan_{tag}")
    gn = _ssd_gate_fwd(y, xa, zx, d_rep, out_norm, name=f"ssd_gate_{tag}")
    x_new = _matmul(gn, w_out, "nn", resid=x, name=f"ssd_out_{tag}")
    return x_new, (x, h, zx, dtp, xa, y, s_in, gn)


def _ssd_mixer_bwd(dx, saved, gain, w_zx_t, w_dt_t, conv_w, conv_b, par, d_rep, out_norm, w_out, grads, row_out,
                   tag):
    x, h, zx, dtp, xa, y, s_in, gn = saved
    di = w_out.shape[0]
    n_groups = di // (SSD_HEADS_PER_GROUP * SSD_HEAD_DIM)
    dgn = _matmul(dx, w_out, "nt", name=f"ssd_dgn_{tag}")
    grads = _matmul_tn_into(grads, gn, dx, row_out, name=f"ssd_dwout_{tag}")
    dy2, dz, dnorm = _ssd_gate_bwd(dgn, y, xa, zx, d_rep, out_norm, name=f"ssd_dgate_{tag}")
    dxs, db, dc, ddtp, dpar = _ssd_scan_bwd(dy2, xa, dtp, par, s_in, n_groups, name=f"ssd_dscan_{tag}")
    dxa = jnp.concatenate([dxs, db, dc], axis=1)
    dxbc, dconv_w, dconv_b = _ssd_conv_bwd(dxa, zx, conv_w, conv_b, di, name=f"ssd_dconv_{tag}")
    dzx = jnp.concatenate([dz, dxbc], axis=1)
    dw_zx_t = _matmul_tn(dzx, h, name=f"ssd_dwin_{tag}")
    dw_dt_t = _matmul_tn(ddtp, h, name=f"ssd_dwdt_{tag}")
    dh = _matmul(dzx, w_zx_t, "nn", name=f"ssd_dh_{tag}")
    dh = _matmul(ddtp, w_dt_t, "nn", resid=dh, name=f"ssd_dhdt_{tag}")
    dx_in, dgain = _rms_bwd(x, gain, dh, dx, name=f"ssd_dnorm_{tag}")
    return dx_in, grads, dw_zx_t, dw_dt_t, dconv_w, dconv_b, dpar, dnorm, dgain


def _sb_qk_norm_fwd(qkv, gains, *, name):
    ns, t, _ = qkv.shape
    per = ns // 3
    tm = _tile(t, 1024, SUBLANES)
    inv_sqrt_d = 1.0 / math.sqrt(SB_HEAD_DIM)

    def body(x_ref, g_ref, o_ref):
        kind = pl.program_id(0) // per
        xv = x_ref[...]

        @pl.when(kind == 2)
        def _():
            o_ref[...] = xv.astype(BF16)

        @pl.when(kind < 2)
        def _():
            left = lax.broadcasted_iota(jnp.int32, (1, LANES), 1) < SB_HEAD_DIM
            sq = xv * xv
            ms = jnp.where(left, jnp.sum(jnp.where(left, sq, 0.0), axis=1, keepdims=True),
                           jnp.sum(jnp.where(left, 0.0, sq), axis=1, keepdims=True)) * (1.0 / SB_HEAD_DIM)
            y = xv * lax.rsqrt(ms + NORM_EPS) * g_ref[pl.ds(kind, 1), :]
            o_ref[...] = (y * jnp.where(kind == 0, inv_sqrt_d, 1.0)).astype(BF16)

    blk = pl.BlockSpec((None, tm, LANES), lambda s, i: (s, i, 0))
    return pl.pallas_call(
        body, name=name, grid=(ns, t // tm),
        in_specs=[blk, pl.BlockSpec((SUBLANES, LANES), lambda s, i: (0, 0))], out_specs=blk,
        out_shape=jax.ShapeDtypeStruct((ns, t, LANES), BF16),
        compiler_params=_params("parallel", "parallel"),
    )(qkv, gains)


def _sb_qk_norm_bwd(dq, dk, dv, qkv, gains, *, name):
    ns, t, _ = qkv.shape
    per = ns // 3
    tm = _tile(t, 1024, SUBLANES)
    inv_sqrt_d = 1.0 / math.sqrt(SB_HEAD_DIM)

    def body(dq_ref, dk_ref, dv_ref, x_ref, g_ref, o_ref, dg_ref):
        s = pl.program_id(0)
        kind = s // per

        @pl.when((s == 0) & (pl.program_id(1) == 0))
        def _():
            dg_ref[...] = jnp.zeros_like(dg_ref)

        @pl.when(kind == 2)
        def _():
            o_ref[...] = dv_ref[...].astype(BF16)

        @pl.when(kind < 2)
        def _():
            xv = x_ref[...]
            dy = jnp.where(kind == 0, dq_ref[...] * inv_sqrt_d, dk_ref[...])
            left = lax.broadcasted_iota(jnp.int32, (1, LANES), 1) < SB_HEAD_DIM

            def halves(v):
                return jnp.where(left, jnp.sum(jnp.where(left, v, 0.0), axis=1, keepdims=True),
                                 jnp.sum(jnp.where(left, 0.0, v), axis=1, keepdims=True))

            r = lax.rsqrt(halves(xv * xv) * (1.0 / SB_HEAD_DIM) + NORM_EPS)
            xhat = xv * r
            u = dy * g_ref[pl.ds(kind, 1), :]
            o_ref[...] = (r * (u - xhat * halves(u * xhat) * (1.0 / SB_HEAD_DIM))).astype(BF16)
            dg_ref[pl.ds(kind, 1), :] += jnp.sum(dy * xhat, axis=0, keepdims=True)

    def grad_blk(kind):
        def index(s, i):
            mine = (s >= kind * per) & (s < (kind + 1) * per)
            return jnp.where(mine, s - kind * per, 0), jnp.where(mine, i, 0), 0
        return pl.BlockSpec((None, tm, LANES), index)

    blk = pl.BlockSpec((None, tm, LANES), lambda s, i: (s, i, 0))
    vec = pl.BlockSpec((SUBLANES, LANES), lambda s, i: (0, 0))
    return pl.pallas_call(
        body, name=name, grid=(ns, t // tm),
        in_specs=[grad_blk(0), grad_blk(1), grad_blk(2), blk, vec], out_specs=[blk, vec],
        out_shape=[jax.ShapeDtypeStruct((ns, t, LANES), BF16), jax.ShapeDtypeStruct((SUBLANES, LANES), F32)],
        compiler_params=_params("arbitrary", "arbitrary"),
    )(dq, dk, dv, qkv, gains)


def _split_dot(v, ones_mat, pieces, left=False):
    total, rest = None, v
    for p in range(pieces):
        part = rest.astype(BF16)
        if p + 1 < pieces:
            rest = rest - part.astype(F32)
        d = (jnp.dot(ones_mat, part, preferred_element_type=F32) if left
             else jnp.dot(part, ones_mat, preferred_element_type=F32))
        total = d if total is None else total + d
    return total


LOGIT_SUM_PIECES = 2
GRAD_SUM_PIECES = 2
LOG_WEIGHT_UNDERFLOW = -105.0


def _sb_attn_fwd(qkv_n, n_heads, *, name):
    ns, t, _ = qkv_n.shape
    per = ns // 3
    bq, blk, hd = SB_QUERY_BLOCK, SB_BLOCK, SB_HEAD_DIM
    nq, n_diag = t // bq, bq // blk

    def body(q_ref, k_ref, v_ref, o_ref):
        i = pl.program_id(1)
        row = lax.broadcasted_iota(jnp.int32, (blk, blk), 0)
        col = lax.broadcasted_iota(jnp.int32, (blk, blk), 1)
        later_keys = (row > col).astype(BF16)
        qry = lax.broadcasted_iota(jnp.int32, (bq, blk), 0)
        key = lax.broadcasted_iota(jnp.int32, (bq, blk), 1)

        def tile(kb, carry, key_offset):
            out = []
            start = pl.multiple_of(kb * blk, blk)
            for hf in range(2):
                lanes = slice(hf * hd, (hf + 1) * hd)
                run, acc = carry[hf]
                z = lax.dot_general(q_ref[:, lanes], k_ref[pl.ds(start, blk), lanes], NT_DIMS,
                                    preferred_element_type=F32)
                sp = _softplus(z)
                lm = -sp if key_offset is None else jnp.where(key + key_offset < qry, -sp, 0.0)
                after = _split_dot(lm, later_keys, LOGIT_SUM_PIECES) + run
                a = jnp.exp(z - sp + after)
                if key_offset is not None:
                    a = jnp.where(key + key_offset < qry, a, 0.0)
                acc = acc + jnp.dot(a.astype(BF16), v_ref[pl.ds(start, blk), lanes], preferred_element_type=F32)
                out.append((run + jnp.sum(lm, axis=1, keepdims=True), acc))
            return tuple(out)

        def live(carry):
            return jnp.max(jnp.maximum(carry[0][0], carry[1][0])) > LOG_WEIGHT_UNDERFLOW

        def step(state):
            s, _, carry = state
            carry = tile(n_diag * i - 1 - s, carry, None)
            return s + 1, live(carry), carry

        carry = tuple((jnp.zeros((bq, 1), F32), jnp.zeros((bq, hd), F32)) for _ in range(2))
        for j in reversed(range(n_diag)):
            carry = tile(n_diag * i + j, carry, j * blk)
        _, _, carry = lax.while_loop(lambda st: (st[0] < n_diag * i) & st[1], step,
                                     (jnp.int32(0), live(carry), carry))
        o_ref[...] = jnp.concatenate([carry[0][1], carry[1][1]], axis=1)

    return pl.pallas_call(
        body, name=name, grid=(per, nq),
        in_specs=[pl.BlockSpec((None, bq, LANES), lambda p, i: (p, i, 0)),
                  pl.BlockSpec((None, t, LANES), lambda p, i: (per + p, 0, 0)),
                  pl.BlockSpec((None, t, LANES), lambda p, i: (2 * per + p, 0, 0))],
        out_specs=pl.BlockSpec((bq, LANES), lambda p, i: (i, p)),
        out_shape=jax.ShapeDtypeStruct((t, n_heads * hd), F32),
        compiler_params=_params("parallel", "arbitrary"),
    )(qkv_n, qkv_n, qkv_n)


def _sb_attn_bwd(do, qkv_n, *, name):
    ns, t, _ = qkv_n.shape
    per = ns // 3
    bq, blk, hd = SB_QUERY_BLOCK, SB_BLOCK, SB_HEAD_DIM
    nq, n_diag = t // bq, bq // blk
    nt_dims = (((1,), (1,)), ((), ()))
    tn_dims = (((0,), (0,)), ((), ()))

    def body(q_ref, k_ref, v_ref, do_ref, dq_ref, dk_ref, dv_ref):
        i = pl.program_id(1)

        @pl.when(i == 0)
        def _():
            dk_ref[...] = jnp.zeros_like(dk_ref)
            dv_ref[...] = jnp.zeros_like(dv_ref)

        row = lax.broadcasted_iota(jnp.int32, (blk, blk), 0)
        col = lax.broadcasted_iota(jnp.int32, (blk, blk), 1)
        later_keys = (col > row).astype(BF16)
        earlier_keys = (col < row).astype(BF16)
        key = lax.broadcasted_iota(jnp.int32, (blk, bq), 0)
        qry = lax.broadcasted_iota(jnp.int32, (blk, bq), 1)
        halves = [slice(hf * hd, (hf + 1) * hd) for hf in range(2)]
        q_hs = [q_ref[:, lanes] for lanes in halves]
        do_bs = [do_ref[:, lanes].astype(BF16) for lanes in halves]

        def scores(kb, hf, key_offset):
            k_blk = k_ref[pl.ds(pl.multiple_of(kb * blk, blk), blk), halves[hf]]
            z = lax.dot_general(k_blk, q_hs[hf], nt_dims, preferred_element_type=F32)
            sp = _softplus(z)
            return k_blk, z, sp, (-sp if key_offset is None else jnp.where(key + key_offset < qry, -sp, 0.0))

        def add_column_sums(tots, kb, key_offset):
            return [tots[hf] + jnp.sum(scores(kb, hf, key_offset)[3], axis=0, keepdims=True) for hf in range(2)]

        def live(tots):
            return jnp.max(jnp.maximum(tots[0], tots[1])) > LOG_WEIGHT_UNDERFLOW

        def reach(state):
            s, _, tots = state
            tots = add_column_sums(tots, n_diag * i - 1 - s, None)
            return s + 1, live(tots), tots

        tots = [jnp.zeros((1, bq), F32)] * 2
        for j in reversed(range(n_diag)):
            tots = add_column_sums(tots, n_diag * i + j, j * blk)
        reached, _, tots = lax.while_loop(lambda st: (st[0] < n_diag * i) & st[1], reach,
                                          (jnp.int32(0), live(tots), tots))

        def tile(kb, carry, key_offset):
            out = []
            start = pl.multiple_of(kb * blk, blk)
            for hf, lanes in enumerate(halves):
                seen, gsum, dq = carry[hf]
                q_h, do_b = q_hs[hf], do_bs[hf]
                k_blk, z, sp, lm = scores(kb, hf, key_offset)
                blk_tot = jnp.sum(lm, axis=0, keepdims=True)
                after = _split_dot(lm, later_keys, LOGIT_SUM_PIECES, left=True) + (tots[hf] - seen - blk_tot)
                a = jnp.exp(z - sp + after)
                if key_offset is not None:
                    a = jnp.where(key + key_offset < qry, a, 0.0)
                da = lax.dot_general(v_ref[pl.ds(start, blk), lanes], do_b, nt_dims, preferred_element_type=F32)
                g = da * a
                before = _split_dot(g, earlier_keys, GRAD_SUM_PIECES, left=True) + gsum
                omb = jnp.exp(-sp)
                dz = g * omb - (1.0 - omb) * before
                if key_offset is not None:
                    dz = jnp.where(key + key_offset < qry, dz, 0.0)
                dz_b = dz.astype(BF16)
                dk_ref[pl.ds(start, blk), lanes] += jnp.dot(dz_b, q_h, preferred_element_type=F32)
                dv_ref[pl.ds(start, blk), lanes] += jnp.dot(a.astype(BF16), do_b, preferred_element_type=F32)
                dq = dq + lax.dot_general(dz_b, k_blk, tn_dims, preferred_element_type=F32)
                out.append((seen + blk_tot, gsum + jnp.sum(g, axis=0, keepdims=True), dq))
            return tuple(out)

        init = tuple((jnp.zeros((1, bq), F32), jnp.zeros((1, bq), F32), jnp.zeros((bq, hd), F32))
                     for _ in range(2))
        carry = lax.fori_loop(n_diag * i - reached, n_diag * i, lambda kb, c: tile(kb, c, None), init)
        for j in range(n_diag):
            carry = tile(n_diag * i + j, carry, j * blk)
        dq_ref[...] = jnp.concatenate([carry[0][2], carry[1][2]], axis=1)

    full = lambda off: pl.BlockSpec((None, t, LANES), lambda p, i: (off + p, 0, 0))
    q_blk = pl.BlockSpec((None, bq, LANES), lambda p, i: (p, i, 0))
    slab = jax.ShapeDtypeStruct((per, t, LANES), F32)
    return pl.pallas_call(
        body, name=name, grid=(per, nq),
        in_specs=[q_blk, full(per), full(2 * per), pl.BlockSpec((bq, LANES), lambda p, i: (i, p))],
        out_specs=[q_blk, full(0), full(0)],
        out_shape=[slab, slab, slab],
        compiler_params=_params("parallel", "arbitrary"),
    )(qkv_n, qkv_n, qkv_n, do)


def _sb_mixer_fwd(x, gain, w_qkv_t, qk_gains, w_out, tag):
    n_heads = w_out.shape[0] // SB_HEAD_DIM
    h = _rms_fwd(x, gain, name=f"sb_norm_{tag}")
    qkv = _matmul(h, w_qkv_t, "nt", out_slabs=True, tn_cap=256, name=f"sb_qkv_{tag}")
    qkv_n = _sb_qk_norm_fwd(qkv, qk_gains, name=f"sb_qknorm_{tag}")
    o = _sb_attn_fwd(qkv_n, n_heads, name=f"sb_attn_{tag}")
    x_new = _matmul(o, w_out, "nn", resid=x, name=f"sb_out_{tag}")
    return x_new, (x, h, qkv, qkv_n, o)


def _sb_mixer_bwd(dx, saved, gain, w_qkv_t, qk_gains, w_out, grads, row_qkv, row_out, tag):
    x, h, qkv, qkv_n, o = saved
    do = _matmul(dx, w_out, "nt", name=f"sb_do_{tag}")
    grads = _matmul_tn_into(grads, o, dx, row_out, name=f"sb_dwout_{tag}")
    dq, dk, dv = _sb_attn_bwd(do, qkv_n, name=f"sb_dattn_{tag}")
    dqkv, dqk_gains = _sb_qk_norm_bwd(dq, dk, dv, qkv, qk_gains, name=f"sb_dqknorm_{tag}")
    grads = _matmul_tn_into(grads, dqkv, h, row_qkv, a_slabs=True, name=f"sb_dwqkv_{tag}")
    dh = _matmul(dqkv, w_qkv_t, "nn", a_slabs=True, name=f"sb_dh_{tag}")
    dx_in, dgain = _rms_bwd(x, gain, dh, dx, name=f"sb_dnorm_{tag}")
    return dx_in, grads, dqk_gains, dgain


MESH = pl.DeviceIdType.MESH


def _position():
    return lax.axis_index("x"), lax.axis_index("y"), lax.axis_index("c")


def _all_gather(shard, *, name, in_vmem):
    rows, n = shard.shape
    space = pltpu.VMEM if in_vmem else pltpu.HBM

    def body(x_ref, out_ref, send_sems, recv_sems, local_sem):
        x, y, c = _position()
        me, sibling = (x, y, c), (x, y, 1 - c)
        chips = [(1 - x, y), (x, 1 - y), (1 - x, 1 - y)]

        def block(px, py, pc):
            return out_ref.at[4 * px + 2 * py + pc]

        def copy(k, blk, to, src=None):
            return pltpu.make_async_remote_copy(
                src_ref=block(*blk) if src is None else src, dst_ref=block(*blk),
                send_sem=send_sems.at[k], recv_sem=recv_sems.at[k], device_id=to, device_id_type=MESH)

        mine = pltpu.make_async_copy(x_ref, block(*me), local_sem)
        mine.start()
        first = [copy(0, me, sibling, src=x_ref)]
        first += [copy(1 + j, me, (*chip, c), src=x_ref) for j, chip in enumerate(chips)]
        for cp in first:
            cp.start()
        passed = [copy(4 + j, (*chip, c), sibling) for j, chip in enumerate(chips)]
        for j, chip in enumerate(chips):
            copy(1 + j, (*chip, c), me).wait_recv()
            passed[j].start()
        copy(0, sibling, me).wait_recv()
        for j, chip in enumerate(chips):
            copy(4 + j, (*chip, 1 - c), me).wait_recv()
        for cp in first + passed:
            cp.wait_send()
        mine.wait()

    return pl.pallas_call(
        body, name=name,
        out_shape=jax.ShapeDtypeStruct((N_DEV, rows, n), shard.dtype),
        in_specs=[pl.BlockSpec(memory_space=space)], out_specs=pl.BlockSpec(memory_space=space),
        scratch_shapes=[pltpu.SemaphoreType.DMA((7,)), pltpu.SemaphoreType.DMA((7,)), pltpu.SemaphoreType.DMA],
        compiler_params=pltpu.CompilerParams(vmem_limit_bytes=V7X_VMEM_LIMIT_BYTES),
    )(shard)


def _exchange_sibling(parts, *, name):
    _, nchip, rows, n = parts.shape

    def body(p_ref, recv_ref, send_sem, recv_sem):
        x, y, c = _position()
        cp = pltpu.make_async_remote_copy(src_ref=p_ref.at[1 - c], dst_ref=recv_ref, send_sem=send_sem,
                                          recv_sem=recv_sem, device_id=(x, y, 1 - c), device_id_type=MESH)
        cp.start()
        cp.wait()

    return pl.pallas_call(
        body, name=name,
        out_shape=jax.ShapeDtypeStruct((nchip, rows, n), parts.dtype),
        in_specs=[pl.BlockSpec(memory_space=pltpu.HBM)], out_specs=pl.BlockSpec(memory_space=pltpu.HBM),
        scratch_shapes=[pltpu.SemaphoreType.DMA, pltpu.SemaphoreType.DMA],
    )(parts)


def _exchange_chips(chip_sums, *, name):
    _, rows, n = chip_sums.shape

    def body(s_ref, recv_ref, send_sems, recv_sems):
        x, y, c = _position()
        chips = [(1 - x, y), (x, 1 - y), (1 - x, 1 - y)]
        copies = [pltpu.make_async_remote_copy(
            src_ref=s_ref.at[2 * cx + cy], dst_ref=recv_ref.at[j], send_sem=send_sems.at[j],
            recv_sem=recv_sems.at[j], device_id=(cx, cy, c), device_id_type=MESH)
            for j, (cx, cy) in enumerate(chips)]
        for cp in copies:
            cp.start()
        for cp in copies:
            cp.wait()

    return pl.pallas_call(
        body, name=name,
        out_shape=jax.ShapeDtypeStruct((3, rows, n), chip_sums.dtype),
        in_specs=[pl.BlockSpec(memory_space=pltpu.HBM)], out_specs=pl.BlockSpec(memory_space=pltpu.HBM),
        scratch_shapes=[pltpu.SemaphoreType.DMA((3,)), pltpu.SemaphoreType.DMA((3,))],
    )(chip_sums)


def _add_pairs(parts, recv, c_mine, *, name):
    _, nchip, rows, n = parts.shape
    tr = _tile(rows, 512, SUBLANES)

    def body(c_ref, a_ref, b_ref, o_ref, wire_ref):
        s = a_ref[...] + b_ref[...]
        o_ref[...] = s
        wire_ref[...] = s.astype(WIRE_DTYPE)

    out_blk = pl.BlockSpec((None, tr, n), lambda k, i, c: (k, i, 0))
    return pl.pallas_call(
        body, name=name,
        grid_spec=pltpu.PrefetchScalarGridSpec(
            num_scalar_prefetch=1, grid=(nchip, rows // tr),
            in_specs=[pl.BlockSpec((None, None, tr, n), lambda k, i, c: (c[0], k, i, 0)),
                      pl.BlockSpec((None, tr, n), lambda k, i, c: (k, i, 0))],
            out_specs=[out_blk, out_blk]),
        out_shape=[jax.ShapeDtypeStruct((nchip, rows, n), parts.dtype),
                   jax.ShapeDtypeStruct((nchip, rows, n), WIRE_DTYPE)],
        compiler_params=_params("parallel", "parallel"),
    )(c_mine, parts, recv)


def _adamw_math(w, g, m, v):
    m = ADAM_B1 * m + (1.0 - ADAM_B1) * g
    v = ADAM_B2 * v + (1.0 - ADAM_B2) * (g * g)
    m_hat = m / (1.0 - ADAM_B1 ** ADAM_STEP)
    v_hat = v / (1.0 - ADAM_B2 ** ADAM_STEP)
    delta = -ADAM_LR * (m_hat / (jnp.sqrt(v_hat) + ADAM_EPS) + ADAM_WD * w)
    return delta, m, v


def _adamw_sharded(chip_sums, recv, k_mine, w, m, v, *, name):
    rows, n = w.shape
    tr = _tile(rows, 256, SUBLANES)

    def body(k_ref, s_ref, r_ref, w_ref, m_ref, v_ref, g_out, d_out, m_out, v_out):
        g = ((s_ref[...] + r_ref[0].astype(F32)) + r_ref[1].astype(F32)) + r_ref[2].astype(F32)
        delta, m_new, v_new = _adamw_math(w_ref[...], g, m_ref[...], v_ref[...])
        g_out[...] = g
        d_out[...] = delta
        m_out[...] = m_new
        v_out[...] = v_new

    blk = pl.BlockSpec((tr, n), lambda i, k: (i, 0))
    out = jax.ShapeDtypeStruct((rows, n), F32)
    return pl.pallas_call(
        body, name=name,
        grid_spec=pltpu.PrefetchScalarGridSpec(
            num_scalar_prefetch=1, grid=(rows // tr,),
            in_specs=[pl.BlockSpec((None, tr, n), lambda i, k: (k[0], i, 0)),
                      pl.BlockSpec((3, tr, n), lambda i, k: (0, i, 0)), blk, blk, blk],
            out_specs=[blk, blk, blk, blk]),
        out_shape=[out, out, out, out],
        compiler_params=_params("parallel"),
    )(k_mine, chip_sums, recv, w, m, v)


SMALL_ROWS = 40
ROW_MIX_NORM, ROW_FFN_NORM, ROW_CONV_B, ROW_OUT_NORM, ROW_POOL_SCALE, ROW_CONV_W = 0, 4, 8, 12, 14, 16
ROW_SSD_VEC, ROW_QK_GAIN, ROW_LOSS = 32, 33, 34


def _adamw_small(gathered, w, m, v, *, name):
    _, rows, n = gathered.shape

    def body(a_ref, w_ref, m_ref, v_ref, g_out, d_out, m_out, v_out):
        g = a_ref[0]
        for d in range(1, N_DEV):
            g = g + a_ref[d]
        row = lax.broadcasted_iota(jnp.int32, (rows, 1), 0)
        g = jnp.where(row == ROW_QK_GAIN, g + pltpu.roll(g, SB_HEAD_DIM, 1), g)
        g = jnp.where(row == ROW_LOSS, jnp.sum(g, axis=1, keepdims=True), g)
        g_out[...] = g
        delta, m_new, v_new = _adamw_math(w_ref[...], g, m_ref[...], v_ref[...])
        d_out[...] = delta
        m_out[...] = m_new
        v_out[...] = v_new

    out = jax.ShapeDtypeStruct((rows, n), F32)
    return pl.pallas_call(body, name=name, out_shape=[out, out, out, out])(gathered, w, m, v)


BIG_WEIGHTS = ("ffn_gate", "ffn_up", "ffn_down", "sb_qkv", "ssd_out", "pool_in", "sb_out", "pool_group", "ssd_in")
COLUMN_SHARDED = ("ssd_in", "sb_qkv", "ffn_gate", "ffn_up")
ROW_PAD = 512
WIRE_DTYPE = jnp.bfloat16


def _to_rows(name, shard, d):
    if name in COLUMN_SHARDED:
        shard = jnp.swapaxes(shard, -1, -2)
    return shard.reshape(-1, d)


def _from_rows(name, rows, shard_shape):
    if name in COLUMN_SHARDED:
        lead, k, n = shard_shape
        return jnp.swapaxes(rows.reshape(lead, n, k), -1, -2)
    return rows.reshape(shard_shape)


def _pad_rows(a, total):
    return jnp.pad(a, ((0, total - a.shape[0]),) + ((0, 0),) * (a.ndim - 1))


def _exact_bf16_rows(v, d):
    words = lax.bitcast_convert_type(v.reshape(-1), WIRE_DTYPE).reshape(-1)
    return _pad_rows(words, -(-words.shape[0] // d) * d).reshape(-1, d)


def _exact_f32(rows, count):
    words = rows.reshape(rows.shape[0], -1)[:, :2 * count].reshape(rows.shape[0], count, 2)
    return lax.bitcast_convert_type(words, F32)


def _device_blocks(full, d):
    return full.reshape(N_DEV, -1, d)


def kernel(x, mix_norm, pool_in, pool_group, pool_scale, ssd_in, ssd_conv_w, ssd_conv_b, ssd_dt_bias, ssd_a_log, ssd_d, ssd_out_norm, ssd_out, sb_qkv, sb_q_norm, sb_k_norm, sb_out, ffn_norm, ffn_gate, ffn_up, ffn_down, loss_target, m_mix_norm, m_pool_in, m_pool_group, m_pool_scale, m_ssd_in, m_ssd_conv_w, m_ssd_conv_b, m_ssd_dt_bias, m_ssd_a_log, m_ssd_d, m_ssd_out_norm, m_ssd_out, m_sb_qkv, m_sb_q_norm, m_sb_k_norm, m_sb_out, m_ffn_norm, m_ffn_gate, m_ffn_up, m_ffn_down, v_mix_norm, v_pool_in, v_pool_group, v_pool_scale, v_ssd_in, v_ssd_conv_w, v_ssd_conv_b, v_ssd_dt_bias, v_ssd_a_log, v_ssd_d, v_ssd_out_norm, v_ssd_out, v_sb_qkv, v_sb_q_norm, v_sb_k_norm, v_sb_out, v_ffn_norm, v_ffn_gate, v_ffn_up, v_ffn_down):
    weights = dict(mix_norm=mix_norm, pool_in=pool_in, pool_group=pool_group, pool_scale=pool_scale, ssd_in=ssd_in,
                   ssd_conv_w=ssd_conv_w, ssd_conv_b=ssd_conv_b, ssd_dt_bias=ssd_dt_bias, ssd_a_log=ssd_a_log,
                   ssd_d=ssd_d, ssd_out_norm=ssd_out_norm, ssd_out=ssd_out, sb_qkv=sb_qkv, sb_q_norm=sb_q_norm,
                   sb_k_norm=sb_k_norm, sb_out=sb_out, ffn_norm=ffn_norm, ffn_gate=ffn_gate, ffn_up=ffn_up,
                   ffn_down=ffn_down)
    mom1 = dict(mix_norm=m_mix_norm, pool_in=m_pool_in, pool_group=m_pool_group, pool_scale=m_pool_scale,
                ssd_in=m_ssd_in, ssd_conv_w=m_ssd_conv_w, ssd_conv_b=m_ssd_conv_b, ssd_dt_bias=m_ssd_dt_bias,
                ssd_a_log=m_ssd_a_log, ssd_d=m_ssd_d, ssd_out_norm=m_ssd_out_norm, ssd_out=m_ssd_out,
                sb_qkv=m_sb_qkv, sb_q_norm=m_sb_q_norm, sb_k_norm=m_sb_k_norm, sb_out=m_sb_out,
                ffn_norm=m_ffn_norm, ffn_gate=m_ffn_gate, ffn_up=m_ffn_up, ffn_down=m_ffn_down)
    mom2 = dict(mix_norm=v_mix_norm, pool_in=v_pool_in, pool_group=v_pool_group, pool_scale=v_pool_scale,
                ssd_in=v_ssd_in, ssd_conv_w=v_ssd_conv_w, ssd_conv_b=v_ssd_conv_b, ssd_dt_bias=v_ssd_dt_bias,
                ssd_a_log=v_ssd_a_log, ssd_d=v_ssd_d, ssd_out_norm=v_ssd_out_norm, ssd_out=v_ssd_out,
                sb_qkv=v_sb_qkv, sb_q_norm=v_sb_q_norm, sb_k_norm=v_sb_k_norm, sb_out=v_sb_out,
                ffn_norm=v_ffn_norm, ffn_gate=v_ffn_gate, ffn_up=v_ffn_up, ffn_down=v_ffn_down)
    names = list(weights)
    depth, d = mix_norm.shape
    xs, ys, cs = _position()
    dev = 4 * xs + 2 * ys + cs
    chip = 2 * xs + ys

    seg = {}
    row = 0
    for name in BIG_WEIGHTS:
        n_rows = weights[name].size // d
        seg[name] = (row, n_rows)
        row += -(-n_rows // SUBLANES) * SUBLANES
    big_rows = row
    n_scale, n_convw = pool_scale.size, ssd_conv_w.size
    exact = jnp.concatenate([_exact_bf16_rows(pool_scale, d), _exact_bf16_rows(ssd_conv_w, d)], axis=0)
    scale_rows = _exact_bf16_rows(pool_scale, d).shape[0]
    packed_rows = -(-(big_rows + exact.shape[0]) // ROW_PAD) * ROW_PAD

    def pack(tree, dtype):
        ends = [seg[n][0] for n in BIG_WEIGHTS[1:]] + [big_rows]
        return jnp.concatenate([_pad_rows(_to_rows(n, tree[n], d).astype(dtype), end - seg[n][0])
                                for n, end in zip(BIG_WEIGHTS, ends)], axis=0)

    w_wire = _pad_rows(jnp.concatenate([pack(weights, WIRE_DTYPE), exact], axis=0), packed_rows)
    gathered = _all_gather(w_wire, name="gather_weights", in_vmem=False)

    def seg_of(name):
        a, n = seg[name]
        return gathered[:, a:a + n]

    n_pool, n_ssd, n_sb = pool_in.shape[0], ssd_in.shape[0], sb_qkv.shape[0]
    assert n_ssd == 1 and n_sb == 1
    w_pool_in = seg_of("pool_in").reshape(N_DEV, n_pool, -1, d).transpose(1, 0, 2, 3).reshape(n_pool, d, d)
    grp = pool_group.shape
    w_pool_group = seg_of("pool_group").reshape(N_DEV, grp[0], grp[1], grp[2], grp[3]).transpose(1, 2, 0, 3, 4)
    w_pool_group = w_pool_group.reshape(grp[0], grp[1], grp[3], grp[3])
    w_ssd_in_t = seg_of("ssd_in").reshape(-1, d)
    w_ssd_out = seg_of("ssd_out").reshape(-1, d)
    w_sb_qkv_t = seg_of("sb_qkv").reshape(-1, d)
    w_sb_out = seg_of("sb_out").reshape(-1, d)
    hidden = ffn_down.shape[1] * N_DEV
    w_gate_t = seg_of("ffn_gate").reshape(N_DEV, depth, -1, d).transpose(1, 0, 2, 3).reshape(depth, hidden, d)
    w_up_t = seg_of("ffn_up").reshape(N_DEV, depth, -1, d).transpose(1, 0, 2, 3).reshape(depth, hidden, d)
    w_down = seg_of("ffn_down").reshape(N_DEV, depth, -1, d).transpose(1, 0, 2, 3).reshape(depth, hidden, d)
    exact_all = gathered[:, big_rows:big_rows + exact.shape[0]]
    scale_full = _exact_f32(exact_all[:, :scale_rows], n_scale).reshape(N_DEV, n_pool, -1)
    scale_full = scale_full.transpose(1, 0, 2).reshape(n_pool, d)
    convw_full = _exact_f32(exact_all[:, scale_rows:], n_convw).reshape(N_DEV, SSD_CONV, -1)
    convw_full = convw_full.transpose(1, 0, 2).reshape(SSD_CONV, -1)

    d_inner = w_ssd_out.shape[0]
    n_zx = w_ssd_in_t.shape[0] - ssd_dt_bias.shape[1]
    w_zx_t = w_ssd_in_t[:n_zx]
    n_ssd_heads = ssd_dt_bias.shape[1]
    n_ssd_groups = n_ssd_heads // SSD_HEADS_PER_GROUP
    w_dt_t = _ssd_group_pad(w_ssd_in_t[n_zx:].T, n_ssd_groups).T
    par = _pad_rows(_ssd_group_pad(jnp.concatenate([ssd_dt_bias, ssd_a_log, ssd_d], axis=0), n_ssd_groups), SUBLANES)
    d_rep = jnp.repeat(ssd_d[0], SSD_HEAD_DIM)[None]
    qk_gains = jnp.zeros((SUBLANES, LANES), F32).at[0].set(jnp.tile(sb_q_norm[0], 2)).at[1].set(jnp.tile(sb_k_norm[0], 2))

    act = x[0]
    saved = []
    for i in range(depth):
        kind, j = i % 3, i // 3
        gain = mix_norm[i:i + 1]
        if kind == 0:
            act, s = _pool_mixer_fwd(act, gain, w_pool_in[j], w_pool_group[j], scale_full[j:j + 1], f"l{i}")
        elif kind == 1:
            act, s = _ssd_mixer_fwd(act, gain, w_zx_t, w_dt_t, convw_full, ssd_conv_b, par, d_rep, ssd_out_norm,
                                    w_ssd_out, f"l{i}")
        else:
            act, s = _sb_mixer_fwd(act, gain, w_sb_qkv_t, qk_gains, w_sb_out, f"l{i}")
        act, f = _ffn_fwd(act, ffn_norm[i:i + 1], w_gate_t[i], w_up_t[i], w_down[i], f"l{i}")
        saved.append((s, f))
    dact, loss_cols = _loss_head(act, loss_target[0], name="loss_head")

    def layer_row(name, layer):
        return seg[name][0] + layer * (seg[name][1] // weights[name].shape[0])

    grads = jnp.zeros((N_DEV, packed_rows, d), F32)
    g_mix_norm, g_ffn_norm = [None] * depth, [None] * depth
    g_pool_group, g_pool_scale = [None] * n_pool, [None] * n_pool
    for i in reversed(range(depth)):
        kind, j = i % 3, i // 3
        gain = mix_norm[i:i + 1]
        s, f = saved[i]
        dact, grads, g_ffn_norm[i] = _ffn_bwd(
            dact, f, ffn_norm[i:i + 1], w_gate_t[i], w_up_t[i], w_down[i], grads,
            [layer_row(n, i) for n in ("ffn_gate", "ffn_up", "ffn_down")], f"l{i}")
        if kind == 0:
            dact, grads, g_pool_group[j], g_pool_scale[j], g_mix_norm[i] = _pool_mixer_bwd(
                dact, s, gain, w_pool_in[j], w_pool_group[j], scale_full[j:j + 1], grads, layer_row("pool_in", j),
                f"l{i}")
        elif kind == 1:
            (dact, grads, g_zx_t, g_dt_t, g_conv_w, g_conv_b, g_par, g_out_norm,
             g_mix_norm[i]) = _ssd_mixer_bwd(dact, s, gain, w_zx_t, w_dt_t, convw_full, ssd_conv_b, par, d_rep,
                                             ssd_out_norm, w_ssd_out, grads, layer_row("ssd_out", j), f"l{i}")
        else:
            dact, grads, g_qk_gains, g_mix_norm[i] = _sb_mixer_bwd(
                dact, s, gain, w_sb_qkv_t, qk_gains, w_sb_out, grads, layer_row("sb_qkv", j),
                layer_row("sb_out", j), f"l{i}")
    grad_x = dact[None]

    g_ssd_in = jnp.concatenate([g_zx_t, _ssd_group_unpad(g_dt_t.T, n_ssd_groups).T], axis=0)
    g_group = jnp.concatenate([_device_blocks(gg[k], d) for gg in g_pool_group for k in range(gg.shape[0])], axis=1)
    for name, blocks in (("ssd_in", _device_blocks(g_ssd_in, d)), ("pool_group", g_group)):
        blocks = blocks.reshape(N_DEV // 2, 2, -1, d).swapaxes(0, 1).reshape(N_DEV, -1, d)
        grads = lax.dynamic_update_slice(grads, blocks, (0, seg[name][0], 0))
    parts = grads.reshape(2, N_DEV // 2, packed_rows, d)
    from_sibling = _exchange_sibling(parts, name="reduce_sibling")
    chip_sums, chip_sums_wire = _add_pairs(parts, from_sibling, cs.reshape(1).astype(jnp.int32),
                                           name="reduce_sibling_add")
    from_chips = _exchange_chips(chip_sums_wire, name="reduce_chips")

    def pack_f32(tree):
        return _pad_rows(pack(tree, F32), packed_rows)

    big_out = _adamw_sharded(chip_sums, from_chips, chip.reshape(1).astype(jnp.int32), pack_f32(weights),
                             pack_f32(mom1), pack_f32(mom2), name="adamw_sharded")

    def small_pack(mix, ffn, conv_b, out_norm, scale, conv_w, vec, qk, loss=None):
        buf = jnp.zeros((SMALL_ROWS, d), F32)
        buf = buf.at[ROW_MIX_NORM:ROW_MIX_NORM + depth].set(mix).at[ROW_FFN_NORM:ROW_FFN_NORM + depth].set(ffn)
        buf = buf.at[ROW_CONV_B:ROW_CONV_B + conv_b.size // d].set(conv_b.reshape(-1, d))
        buf = buf.at[ROW_OUT_NORM:ROW_OUT_NORM + out_norm.size // d].set(out_norm.reshape(-1, d))
        buf = buf.at[ROW_POOL_SCALE:ROW_POOL_SCALE + n_pool].set(scale)
        buf = buf.at[ROW_CONV_W:ROW_CONV_W + conv_w.size // d].set(conv_w.reshape(-1, d))
        buf = buf.at[ROW_SSD_VEC].set(vec.reshape(-1)).at[ROW_QK_GAIN].set(qk.reshape(-1))
        if loss is not None:
            buf = buf.at[ROW_LOSS].set(loss.reshape(-1))
        return buf

    def small_params(tree):
        scale = lax.dynamic_update_slice(jnp.zeros((n_pool, d), F32), tree["pool_scale"],
                                         (0, dev * tree["pool_scale"].shape[1]))
        conv_w = lax.dynamic_update_slice(jnp.zeros(convw_full.shape, F32), tree["ssd_conv_w"][0],
                                          (0, dev * tree["ssd_conv_w"].shape[2]))
        vec = jnp.zeros((SUBLANES, LANES), F32)
        vec = vec.at[0, :n_ssd_heads].set(tree["ssd_dt_bias"][0]).at[1, :n_ssd_heads].set(tree["ssd_a_log"][0])
        vec = vec.at[2, :n_ssd_heads].set(tree["ssd_d"][0])
        qk = jnp.zeros((SUBLANES, LANES), F32)
        qk = qk.at[0, SB_HEAD_DIM:].set(tree["sb_q_norm"][0]).at[1, SB_HEAD_DIM:].set(tree["sb_k_norm"][0])
        return small_pack(tree["mix_norm"], tree["ffn_norm"], tree["ssd_conv_b"], tree["ssd_out_norm"], scale,
                          conv_w, vec, qk)

    small_partial = small_pack(jnp.concatenate(g_mix_norm, axis=0), jnp.concatenate(g_ffn_norm, axis=0), g_conv_b,
                               g_out_norm, jnp.concatenate(g_pool_scale, axis=0), g_conv_w,
                               jnp.zeros((SUBLANES, LANES), F32).at[:3, :n_ssd_heads].set(
                                   _ssd_group_unpad(g_par[:3], n_ssd_groups)), g_qk_gains,
                               loss_cols)
    small_all = _all_gather(small_partial, name="gather_small", in_vmem=True)
    small_out = _adamw_small(small_all, small_params(weights), small_params(mom1), small_params(mom2),
                             name="adamw_small")
    loss = small_out[0][ROW_LOSS, 0]

    def unpack(big, small):
        out = {}
        for name in BIG_WEIGHTS:
            a, n = seg[name]
            out[name] = _from_rows(name, big[a:a + n], weights[name].shape)
        out["mix_norm"] = small[ROW_MIX_NORM:ROW_MIX_NORM + depth]
        out["ffn_norm"] = small[ROW_FFN_NORM:ROW_FFN_NORM + depth]
        out["ssd_conv_b"] = small[ROW_CONV_B:ROW_CONV_B + ssd_conv_b.size // d].reshape(ssd_conv_b.shape)
        out["ssd_out_norm"] = small[ROW_OUT_NORM:ROW_OUT_NORM + ssd_out_norm.size // d].reshape(ssd_out_norm.shape)
        out["pool_scale"] = lax.dynamic_slice(small[ROW_POOL_SCALE:ROW_POOL_SCALE + n_pool],
                                              (0, dev * pool_scale.shape[1]), pool_scale.shape)
        conv_w = small[ROW_CONV_W:ROW_CONV_W + convw_full.size // d].reshape(convw_full.shape)
        out["ssd_conv_w"] = lax.dynamic_slice(conv_w, (0, dev * ssd_conv_w.shape[2]), ssd_conv_w.shape[1:])[None]
        vec = small[ROW_SSD_VEC].reshape(SUBLANES, LANES)
        out["ssd_dt_bias"], out["ssd_a_log"], out["ssd_d"] = (vec[r:r + 1, :n_ssd_heads] for r in range(3))
        qk = small[ROW_QK_GAIN].reshape(SUBLANES, LANES)
        out["sb_q_norm"], out["sb_k_norm"] = qk[0:1, SB_HEAD_DIM:], qk[1:2, SB_HEAD_DIM:]
        return [out[n] for n in names]

    results = [unpack(b, s) for b, s in zip(big_out, small_out)]
    return (loss, grad_x, *results[0], *results[1], *results[2], *results[3])
```

```python
import math

import jax
import jax.numpy as jnp
from jax import lax
from jax.experimental import pallas as pl
from jax.experimental.pallas import tpu as pltpu

F32 = jnp.float32
BF16 = jnp.bfloat16

N_DEV = 8
NORM_EPS = 1e-6
V7X_VMEM_LIMIT_BYTES = 48 * 1024 * 1024
LANES = 128
SUBLANES = 8

POOL_WINDOWS = (2, 4, 8, 16)
SSD_CHUNK = 256
SSD_HEAD_DIM = 64
SSD_STATE = 128
SSD_HEADS_PER_GROUP = 4
SSD_CONV = 4
SB_HEAD_DIM = 64
SB_BLOCK = 256
SB_QUERY_BLOCK = 256

ADAM_LR = 0.001
ADAM_B1 = 0.9
ADAM_B2 = 0.999
ADAM_EPS = 1e-08
ADAM_WD = 0.01
ADAM_STEP = 10


def _params(*sem):
    return pltpu.CompilerParams(dimension_semantics=sem, vmem_limit_bytes=V7X_VMEM_LIMIT_BYTES)


def _tile(n, cap, mult):
    best = None
    for t in range(mult, min(n, cap) + 1, mult):
        if n % t == 0:
            best = t
    return best or n


def _load_slabs(ref, slabs):
    if not slabs:
        return ref[...]
    return jnp.concatenate([ref[p] for p in range(ref.shape[0])], axis=1)


def _matmul(a, b, mode, *, name, out_dtype=F32, resid=None, a_slabs=False, out_slabs=False,
            tm_cap=1024, tn_cap=1024, tk_cap=2048):
    pairs = list(zip(a, b)) if isinstance(a, (list, tuple)) else [(a, b)]
    a, b = pairs[0]
    if a_slabs:
        m, k = a.shape[1], a.shape[0] * LANES
    else:
        m, k = a.shape
    n = b.shape[1] if mode == "nn" else b.shape[0]
    assert (b.shape[0] if mode == "nn" else b.shape[1]) == k
    assert all(pa.shape == a.shape and pb.shape == b.shape for pa, pb in pairs)
    tm, tn, tk = _tile(m, tm_cap, SUBLANES), _tile(n, tn_cap, LANES), _tile(k, tk_cap, LANES)
    nk = k // tk
    dn = (((1,), (0,)), ((), ())) if mode == "nn" else (((1,), (1,)), ((), ()))
    has_resid = resid is not None
    n_pairs = len(pairs)

    def body(*refs):
        ab_refs, rest = refs[:2 * n_pairs], refs[2 * n_pairs:]
        r_ref = rest[0] if has_resid else None
        o_ref = rest[1] if has_resid else rest[0]
        kk = pl.program_id(2)

        def partial():
            total = None
            for p in range(n_pairs):
                d = lax.dot_general(_load_slabs(ab_refs[2 * p], a_slabs).astype(BF16),
                                    ab_refs[2 * p + 1][...].astype(BF16), dn, preferred_element_type=F32)
                total = d if total is None else total + d
            return total

        def finish(r):
            if has_resid:
                r = r + r_ref[...]
            if out_slabs:
                for p in range(tn // LANES):
                    o_ref[p] = r[:, p * LANES:(p + 1) * LANES].astype(out_dtype)
            else:
                o_ref[...] = r.astype(out_dtype)

        if nk == 1:
            finish(partial())
        else:
            acc = rest[-1]

            @pl.when(kk == 0)
            def _():
                acc[...] = jnp.zeros_like(acc)

            acc[...] += partial()

            @pl.when(kk == nk - 1)
            def _():
                finish(acc[...])

    b_spec = (pl.BlockSpec((tk, tn), lambda i, j, kk: (kk, j)) if mode == "nn"
              else pl.BlockSpec((tn, tk), lambda i, j, kk: (j, kk)))
    a_spec = (pl.BlockSpec((tk // LANES, tm, LANES), lambda i, j, kk: (kk, i, 0)) if a_slabs
              else pl.BlockSpec((tm, tk), lambda i, j, kk: (i, kk)))
    in_specs = [a_spec, b_spec] * n_pairs
    args = [t for pair in pairs for t in pair]
    if has_resid:
        in_specs.append(pl.BlockSpec((tm, tn), lambda i, j, kk: (i, j)))
        args.append(resid)
    if out_slabs:
        out_spec = pl.BlockSpec((tn // LANES, tm, LANES), lambda i, j, kk: (j, i, 0))
        out_shape = jax.ShapeDtypeStruct((n // LANES, m, LANES), out_dtype)
    else:
        out_spec = pl.BlockSpec((tm, tn), lambda i, j, kk: (i, j))
        out_shape = jax.ShapeDtypeStruct((m, n), out_dtype)
    return pl.pallas_call(
        body, name=name, grid=(m // tm, n // tn, nk),
        in_specs=in_specs, out_specs=out_spec, out_shape=out_shape,
        scratch_shapes=[pltpu.VMEM((tm, tn), F32)] if nk > 1 else [],
        compiler_params=_params("parallel", "parallel", "arbitrary"),
    )(*args)


def _matmul_tn(a, b, *, name, a_slabs=False, ta_cap=1024, tb_cap=1024, tr_cap=512):
    if a_slabs:
        r, ka = a.shape[1], a.shape[0] * LANES
    else:
        r, ka = a.shape
    nb = b.shape[1]
    assert b.shape[0] == r
    ta, tb, tr = _tile(ka, ta_cap, LANES), _tile(nb, tb_cap, LANES), _tile(r, tr_cap, SUBLANES)

    def body(a_ref, b_ref, o_ref):
        @pl.when(pl.program_id(2) == 0)
        def _():
            o_ref[...] = jnp.zeros_like(o_ref)

        o_ref[...] += lax.dot_general(_load_slabs(a_ref, a_slabs).astype(BF16), b_ref[...].astype(BF16),
                                      (((0,), (0,)), ((), ())), preferred_element_type=F32)

    a_spec = (pl.BlockSpec((ta // LANES, tr, LANES), lambda i, j, kk: (i, kk, 0)) if a_slabs
              else pl.BlockSpec((tr, ta), lambda i, j, kk: (kk, i)))
    return pl.pallas_call(
        body, name=name, grid=(ka // ta, nb // tb, r // tr),
        in_specs=[a_spec, pl.BlockSpec((tr, tb), lambda i, j, kk: (kk, j))],
        out_specs=pl.BlockSpec((ta, tb), lambda i, j, kk: (i, j)),
        out_shape=jax.ShapeDtypeStruct((ka, nb), F32),
        compiler_params=_params("parallel", "parallel", "arbitrary"),
    )(a, b)


def _core_major(k):
    return (k % 2) * (N_DEV // 2) + k // 2


def _matmul_tn_into(buf, a, b, row_off, *, name, a_slabs=False, tr_cap=1024):
    if a_slabs:
        r, ka = a.shape[1], a.shape[0] * LANES
    else:
        r, ka = a.shape
    n_dev, _, n = buf.shape
    per = ka // n_dev
    assert b.shape == (r, n) and ka % n_dev == 0 and per % SUBLANES == 0 and row_off % per == 0
    tr = _tile(r, tr_cap, SUBLANES)

    def body(buf_ref, a_ref, b_ref, o_ref):
        prod = lax.dot_general(_load_slabs(a_ref, a_slabs).astype(BF16), b_ref[...].astype(BF16),
                               (((0,), (0,)), ((), ())), preferred_element_type=F32)
        @pl.when(pl.program_id(0) == 0)
        def _():
            for k in range(n_dev):
                o_ref[_core_major(k)] = prod[k * per:(k + 1) * per]

        @pl.when(pl.program_id(0) > 0)
        def _():
            for k in range(n_dev):
                o_ref[_core_major(k)] += prod[k * per:(k + 1) * per]

    a_spec = (pl.BlockSpec((ka // LANES, tr, LANES), lambda i: (0, i, 0)) if a_slabs
              else pl.BlockSpec((tr, ka), lambda i: (i, 0)))
    return pl.pallas_call(
        body, name=name, grid=(r // tr,),
        in_specs=[pl.BlockSpec(memory_space=pl.ANY), a_spec, pl.BlockSpec((tr, n), lambda i: (i, 0))],
        out_specs=pl.BlockSpec((n_dev, per, n), lambda i: (0, row_off // per, 0)),
        out_shape=jax.ShapeDtypeStruct(buf.shape, F32),
        input_output_aliases={0: 0},
        compiler_params=_params("arbitrary"),
    )(buf, a, b)


def _rms_fwd(x, gain, *, name):
    t, d = x.shape
    tm = _tile(t, 512, SUBLANES)

    def body(x_ref, g_ref, o_ref):
        xv = x_ref[...]
        r = lax.rsqrt(jnp.mean(xv * xv, axis=-1, keepdims=True) + NORM_EPS)
        o_ref[...] = (xv * r * g_ref[...]).astype(BF16)

    return pl.pallas_call(
        body, name=name, grid=(t // tm,),
        in_specs=[pl.BlockSpec((tm, d), lambda i: (i, 0)), pl.BlockSpec((1, d), lambda i: (0, 0))],
        out_specs=pl.BlockSpec((tm, d), lambda i: (i, 0)),
        out_shape=jax.ShapeDtypeStruct((t, d), BF16),
        compiler_params=_params("parallel"),
    )(x, gain)


def _rms_bwd(x, gain, dh, dres, *, name):
    t, d = x.shape
    tm = _tile(t, 512, SUBLANES)

    def body(x_ref, g_ref, dh_ref, dres_ref, dx_ref, dg_ref):
        @pl.when(pl.program_id(0) == 0)
        def _():
            dg_ref[...] = jnp.zeros_like(dg_ref)

        xv = x_ref[...]
        r = lax.rsqrt(jnp.mean(xv * xv, axis=-1, keepdims=True) + NORM_EPS)
        xhat = xv * r
        dhv = dh_ref[...]
        u = dhv * g_ref[...]
        dx_ref[...] = dres_ref[...] + r * (u - xhat * jnp.mean(u * xhat, axis=-1, keepdims=True))
        dg_ref[...] += jnp.sum(dhv * xhat, axis=0, keepdims=True)

    return pl.pallas_call(
        body, name=name, grid=(t // tm,),
        in_specs=[pl.BlockSpec((tm, d), lambda i: (i, 0)), pl.BlockSpec((1, d), lambda i: (0, 0)),
                  pl.BlockSpec((tm, d), lambda i: (i, 0)), pl.BlockSpec((tm, d), lambda i: (i, 0))],
        out_specs=[pl.BlockSpec((tm, d), lambda i: (i, 0)), pl.BlockSpec((1, d), lambda i: (0, 0))],
        out_shape=[jax.ShapeDtypeStruct((t, d), F32), jax.ShapeDtypeStruct((1, d), F32)],
        compiler_params=_params("arbitrary"),
    )(x, gain, dh, dres)


def _loss_head(y, target, *, name):
    t, d = y.shape
    tm = _tile(t, 512, SUBLANES)

    def body(y_ref, t_ref, dy_ref, l_ref):
        @pl.when(pl.program_id(0) == 0)
        def _():
            l_ref[...] = jnp.zeros_like(l_ref)

        e = y_ref[...] - t_ref[...]
        dy_ref[...] = e * (1.0 / d)
        l_ref[...] += jnp.sum(e * e, axis=0, keepdims=True) * (0.5 / d)

    return pl.pallas_call(
        body, name=name, grid=(t // tm,),
        in_specs=[pl.BlockSpec((tm, d), lambda i: (i, 0)), pl.BlockSpec((tm, d), lambda i: (i, 0))],
        out_specs=[pl.BlockSpec((tm, d), lambda i: (i, 0)), pl.BlockSpec((1, d), lambda i: (0, 0))],
        out_shape=[jax.ShapeDtypeStruct((t, d), F32), jax.ShapeDtypeStruct((1, d), F32)],
        compiler_params=_params("arbitrary"),
    )(y, target)


def _sigmoid(v):
    return 0.5 * jnp.tanh(0.5 * v) + 0.5


FFN_TOKEN_TILE = 512
FFN_HIDDEN_TILE = 1408
NT_DIMS = (((1,), (1,)), ((), ()))


def _ffn_up(h, w_gate_t, w_up_t, *, name):
    t, d = h.shape
    f = w_gate_t.shape[0]
    tm, tn = _tile(t, FFN_TOKEN_TILE, SUBLANES), _tile(f, FFN_HIDDEN_TILE, LANES)

    def body(h_ref, g_ref, u_ref, s_ref, a_ref, b_ref):
        hv = h_ref[...].astype(BF16)
        av = lax.dot_general(hv, g_ref[...].astype(BF16), NT_DIMS, preferred_element_type=F32)
        bv = lax.dot_general(hv, u_ref[...].astype(BF16), NT_DIMS, preferred_element_type=F32)
        s_ref[...] = (av * _sigmoid(av) * bv).astype(BF16)
        a_ref[...] = av.astype(BF16)
        b_ref[...] = bv.astype(BF16)

    w_spec = pl.BlockSpec((tn, d), lambda j, i: (j, 0))
    out_spec = pl.BlockSpec((tm, tn), lambda j, i: (i, j))
    out = jax.ShapeDtypeStruct((t, f), BF16)
    return pl.pallas_call(
        body, name=name, grid=(f // tn, t // tm),
        in_specs=[pl.BlockSpec((tm, d), lambda j, i: (i, 0)), w_spec, w_spec],
        out_specs=[out_spec, out_spec, out_spec], out_shape=[out, out, out],
        compiler_params=_params("parallel", "parallel"),
    )(h, w_gate_t, w_up_t)


def _ffn_dact(dx, w_down, a, b, *, name):
    t, d = dx.shape
    f = w_down.shape[0]
    tm, tn = _tile(t, FFN_TOKEN_TILE, SUBLANES), _tile(f, FFN_HIDDEN_TILE, LANES)

    def body(dx_ref, w_ref, a_ref, b_ref, da_ref, db_ref):
        ds = lax.dot_general(dx_ref[...].astype(BF16), w_ref[...].astype(BF16), NT_DIMS, preferred_element_type=F32)
        av = a_ref[...].astype(F32)
        sg = _sigmoid(av)
        da_ref[...] = (ds * b_ref[...].astype(F32) * (sg * (1.0 + av * (1.0 - sg)))).astype(BF16)
        db_ref[...] = (ds * av * sg).astype(BF16)

    blk = pl.BlockSpec((tm, tn), lambda j, i: (i, j))
    out = jax.ShapeDtypeStruct((t, f), BF16)
    return pl.pallas_call(
        body, name=name, grid=(f // tn, t // tm),
        in_specs=[pl.BlockSpec((tm, d), lambda j, i: (i, 0)), pl.BlockSpec((tn, d), lambda j, i: (j, 0)), blk, blk],
        out_specs=[blk, blk], out_shape=[out, out],
        compiler_params=_params("parallel", "parallel"),
    )(dx, w_down, a, b)


def _ffn_fwd(x, gain, w_gate_t, w_up_t, w_down, tag):
    h = _rms_fwd(x, gain, name=f"ffn_norm_{tag}")
    s, a, b = _ffn_up(h, w_gate_t, w_up_t, name=f"ffn_up_{tag}")
    x_new = _matmul(s, w_down, "nn", resid=x, tn_cap=1024, tk_cap=2816, name=f"ffn_down_{tag}")
    return x_new, (x, h, a, b, s)


def _ffn_bwd(dx, saved, gain, w_gate_t, w_up_t, w_down, grads, rows, tag):
    x, h, a, b, s = saved
    da, db = _ffn_dact(dx, w_down, a, b, name=f"ffn_dact_{tag}")
    grads = _matmul_tn_into(grads, da, h, rows[0], name=f"ffn_dwgate_{tag}")
    grads = _matmul_tn_into(grads, db, h, rows[1], name=f"ffn_dwup_{tag}")
    grads = _matmul_tn_into(grads, s, dx, rows[2], name=f"ffn_dwdown_{tag}")
    dh = _matmul([da, db], [w_gate_t, w_up_t], "nn", tm_cap=512, tn_cap=1024, tk_cap=2816, name=f"ffn_dh_{tag}")
    dx_in, dgain = _rms_bwd(x, gain, dh, dx, name=f"ffn_dnorm_{tag}")
    return dx_in, grads, dgain


POOL_HALO = 16


def _shift_rows(v, k):
    n = v.shape[0]
    return pltpu.roll(v, k % n, 0)


def _window_sum(v, w, direction):
    k = 1
    while k < w:
        v = v + _shift_rows(v, direction * k)
        k *= 2
    return v


def _pool_fwd(u, x, w_group, scale, *, name):
    t, d = u.shape
    ng, dg = w_group.shape[0], w_group.shape[1]
    tm = _tile(t, 512, POOL_HALO)
    hb = tm // POOL_HALO

    def body(u_ref, halo_ref, x_ref, w_ref, s_ref, xo_ref, p_ref, y_ref):
        i, g = pl.program_id(0), pl.program_id(1)
        halo = jnp.where(i > 0, halo_ref[...], 0.0)
        ext = jnp.concatenate([halo, u_ref[...]], axis=0)
        pos = i * tm + lax.broadcasted_iota(jnp.int32, (tm, 1), 0)
        for gi, win in enumerate(POOL_WINDOWS):
            @pl.when(g == gi)
            def _(win=win):
                tot = _window_sum(ext, win, 1)[POOL_HALO:]
                cnt = jnp.minimum(pos + 1, win).astype(F32)
                p = (tot / cnt - u_ref[...]).astype(BF16)
                p_ref[...] = p
                y = jnp.dot(p, w_ref[...].astype(BF16), preferred_element_type=F32)
                y_ref[...] = y
                xo_ref[...] = x_ref[...] + y * s_ref[...]

    blk = pl.BlockSpec((tm, dg), lambda i, g: (i, g))
    return pl.pallas_call(
        body, name=name, grid=(t // tm, ng),
        in_specs=[blk, pl.BlockSpec((POOL_HALO, dg), lambda i, g: (jnp.maximum(i * hb - 1, 0), g)), blk,
                  pl.BlockSpec((None, dg, dg), lambda i, g: (g, 0, 0)), pl.BlockSpec((1, dg), lambda i, g: (0, g))],
        out_specs=[blk, blk, blk],
        out_shape=[jax.ShapeDtypeStruct((t, d), F32), jax.ShapeDtypeStruct((t, d), BF16),
                   jax.ShapeDtypeStruct((t, d), F32)],
        compiler_params=_params("parallel", "parallel"),
    )(u, u, x, w_group, scale)


def _pool_bwd(dx, p, y_pre, w_group, scale, *, name):
    t, d = dx.shape
    ng, dg = w_group.shape[0], w_group.shape[1]
    tm = _tile(t, 512, POOL_HALO)
    hb = tm // POOL_HALO
    nt = t // tm

    def body(dx_ref, nxt_ref, p_ref, y_ref, w_ref, s_ref, du_ref, dw_ref, ds_ref):
        g, i = pl.program_id(0), pl.program_id(1)

        @pl.when(i == 0)
        def _():
            dw_ref[...] = jnp.zeros_like(dw_ref)
            ds_ref[...] = jnp.zeros_like(ds_ref)

        dxv = dx_ref[...]
        ds_ref[...] += jnp.sum(dxv * y_ref[...], axis=0, keepdims=True)
        nxt = jnp.where(i < nt - 1, nxt_ref[...], 0.0)
        dyp = (jnp.concatenate([dxv, nxt], axis=0) * s_ref[...]).astype(BF16)
        dw_ref[...] += lax.dot_general(p_ref[...], dyp[:tm], (((0,), (0,)), ((), ())), preferred_element_type=F32)
        dp = lax.dot_general(dyp, w_ref[...].astype(BF16), (((1,), (1,)), ((), ())), preferred_element_type=F32)
        pos = i * tm + lax.broadcasted_iota(jnp.int32, (tm + POOL_HALO, 1), 0)
        for gi, win in enumerate(POOL_WINDOWS):
            @pl.when(g == gi)
            def _(win=win):
                q = dp / jnp.minimum(pos + 1, win).astype(F32)
                du_ref[...] = (_window_sum(q, win, -1)[:tm] - dp[:tm]).astype(BF16)

    blk = pl.BlockSpec((tm, dg), lambda g, i: (i, g))
    return pl.pallas_call(
        body, name=name, grid=(ng, nt),
        in_specs=[blk, pl.BlockSpec((POOL_HALO, dg), lambda g, i: (jnp.minimum((i + 1) * hb, t // POOL_HALO - 1), g)),
                  blk, blk, pl.BlockSpec((None, dg, dg), lambda g, i: (g, 0, 0)),
                  pl.BlockSpec((1, dg), lambda g, i: (0, g))],
        out_specs=[blk, pl.BlockSpec((None, dg, dg), lambda g, i: (g, 0, 0)), pl.BlockSpec((1, dg), lambda g, i: (0, g))],
        out_shape=[jax.ShapeDtypeStruct((t, d), BF16), jax.ShapeDtypeStruct((ng, dg, dg), F32),
                   jax.ShapeDtypeStruct((1, d), F32)],
        compiler_params=_params("parallel", "arbitrary"),
    )(dx, dx, p, y_pre, w_group, scale)


def _pool_mixer_fwd(x, gain, w_in, w_group, scale, tag):
    h = _rms_fwd(x, gain, name=f"pool_norm_{tag}")
    u = _matmul(h, w_in, "nn", name=f"pool_in_{tag}")
    x_new, p, y_pre = _pool_fwd(u, x, w_group, scale, name=f"pool_mix_{tag}")
    return x_new, (x, h, p, y_pre)


def _pool_mixer_bwd(dx, saved, gain, w_in, w_group, scale, grads, row_in, tag):
    x, h, p, y_pre = saved
    du, dw_group, dscale = _pool_bwd(dx, p, y_pre, w_group, scale, name=f"pool_dmix_{tag}")
    grads = _matmul_tn_into(grads, h, du, row_in, name=f"pool_dwin_{tag}")
    dh = _matmul(du, w_in, "nt", name=f"pool_dh_{tag}")
    dx_in, dgain = _rms_bwd(x, gain, dh, dx, name=f"pool_dnorm_{tag}")
    return dx_in, grads, dw_group, dscale, dgain


CONV_HALO = 8
HIGHEST = lax.Precision.HIGHEST
NEG_BIG = -1e30


def _softplus(v):
    return jnp.maximum(v, 0.0) + jnp.log(1.0 + jnp.exp(-jnp.abs(v)))


def _dot_exact(a, b):
    return jnp.dot(a, b, precision=HIGHEST, preferred_element_type=F32)


def _conv_taps(ext, w_ref, off, rows):
    acc = None
    for k in range(SSD_CONV):
        shift = SSD_CONV - 1 - k
        v = (_shift_rows(ext, shift) if shift else ext)[off:off + rows] * w_ref[k:k + 1, :]
        acc = v if acc is None else acc + v
    return acc


def _ssd_conv_fwd(zx, conv_w, conv_b, col0, *, name):
    t = zx.shape[0]
    c = conv_w.shape[1]
    tm, tc = _tile(t, 512, CONV_HALO), _tile(c, 512, LANES)
    hb, cb0 = tm // CONV_HALO, col0 // tc
    assert col0 % tc == 0

    def body(x_ref, halo_ref, w_ref, b_ref, o_ref):
        halo = jnp.where(pl.program_id(0) > 0, halo_ref[...], 0.0)
        ext = jnp.concatenate([halo, x_ref[...]], axis=0)
        pre = _conv_taps(ext, w_ref, CONV_HALO, tm) + b_ref[...]
        o_ref[...] = pre * _sigmoid(pre)

    return pl.pallas_call(
        body, name=name, grid=(t // tm, c // tc),
        in_specs=[pl.BlockSpec((tm, tc), lambda i, j: (i, j + cb0)),
                  pl.BlockSpec((CONV_HALO, tc), lambda i, j: (jnp.maximum(i * hb - 1, 0), j + cb0)),
                  pl.BlockSpec((SSD_CONV, tc), lambda i, j: (0, j)), pl.BlockSpec((1, tc), lambda i, j: (0, j))],
        out_specs=pl.BlockSpec((tm, tc), lambda i, j: (i, j)),
        out_shape=jax.ShapeDtypeStruct((t, c), F32),
        compiler_params=_params("parallel", "parallel"),
    )(zx, zx, conv_w, conv_b)


def _ssd_conv_bwd(dxa, zx, conv_w, conv_b, col0, *, name):
    t = zx.shape[0]
    c = conv_w.shape[1]
    tm, tc = _tile(t, 512, CONV_HALO), _tile(c, 512, LANES)
    hb, cb0, nt = tm // CONV_HALO, col0 // tc, t // tm
    last_halo = t // CONV_HALO - 1

    def body(x_ref, prev_ref, nxt_ref, d_ref, dnxt_ref, w_ref, b_ref, dx_ref, dw_ref, db_ref):
        i = pl.program_id(1)

        @pl.when(i == 0)
        def _():
            dw_ref[...] = jnp.zeros_like(dw_ref)
            db_ref[...] = jnp.zeros_like(db_ref)

        prev = jnp.where(i > 0, prev_ref[...], 0.0)
        has_next = i < nt - 1
        ext = jnp.concatenate([prev, x_ref[...], jnp.where(has_next, nxt_ref[...], 0.0)], axis=0)
        pre = _conv_taps(ext, w_ref, CONV_HALO, tm + CONV_HALO) + b_ref[...]
        sg = _sigmoid(pre)
        dact = jnp.concatenate([d_ref[...], jnp.where(has_next, dnxt_ref[...], 0.0)], axis=0)
        dpre = dact * (sg * (1.0 + pre * (1.0 - sg)))
        db_ref[...] += jnp.sum(dpre[:tm], axis=0, keepdims=True)
        acc = None
        for k in range(SSD_CONV):
            shift = SSD_CONV - 1 - k
            src = (_shift_rows(ext, shift) if shift else ext)[CONV_HALO:CONV_HALO + tm]
            dw_ref[k:k + 1, :] += jnp.sum(dpre[:tm] * src, axis=0, keepdims=True)
            v = (_shift_rows(dpre, -shift) if shift else dpre)[:tm] * w_ref[k:k + 1, :]
            acc = v if acc is None else acc + v
        dx_ref[...] = acc.astype(BF16)

    main = lambda j, i: (i, j + cb0)
    return pl.pallas_call(
        body, name=name, grid=(c // tc, nt),
        in_specs=[pl.BlockSpec((tm, tc), main),
                  pl.BlockSpec((CONV_HALO, tc), lambda j, i: (jnp.maximum(i * hb - 1, 0), j + cb0)),
                  pl.BlockSpec((CONV_HALO, tc), lambda j, i: (jnp.minimum((i + 1) * hb, last_halo), j + cb0)),
                  pl.BlockSpec((tm, tc), lambda j, i: (i, j)),
                  pl.BlockSpec((CONV_HALO, tc), lambda j, i: (jnp.minimum((i + 1) * hb, last_halo), j)),
                  pl.BlockSpec((SSD_CONV, tc), lambda j, i: (0, j)), pl.BlockSpec((1, tc), lambda j, i: (0, j))],
        out_specs=[pl.BlockSpec((tm, tc), lambda j, i: (i, j)), pl.BlockSpec((SSD_CONV, tc), lambda j, i: (0, j)),
                   pl.BlockSpec((1, tc), lambda j, i: (0, j))],
        out_shape=[jax.ShapeDtypeStruct((t, c), BF16), jax.ShapeDtypeStruct((SSD_CONV, c), F32),
                   jax.ShapeDtypeStruct((1, c), F32)],
        compiler_params=_params("parallel", "arbitrary"),
    )(zx, zx, zx, dxa, dxa, conv_w, conv_b)


SSD_CUMSUM_PIECES = 3
SSD_GROUPS_PER_STEP = 1


def _ssd_group_pad(v, n_groups):
    lead = v.shape[:-1]
    v = v.reshape(*lead, n_groups, SSD_HEADS_PER_GROUP)
    v = jnp.pad(v, [(0, 0)] * (len(lead) + 1) + [(0, LANES - SSD_HEADS_PER_GROUP)])
    return v.reshape(*lead, n_groups * LANES)


def _ssd_group_unpad(v, n_groups):
    lead = v.shape[:-1]
    return v.reshape(*lead, n_groups, LANES)[..., :SSD_HEADS_PER_GROUP].reshape(*lead, -1)


def _ssd_chunk_common(dtp_ref, par_ref):
    ell = SSD_CHUNK
    dt = _softplus(dtp_ref[...] + par_ref[0:1, :])
    a = -jnp.exp(par_ref[1:2, :])
    row = lax.broadcasted_iota(jnp.int32, (ell, ell), 0)
    col = lax.broadcasted_iota(jnp.int32, (ell, ell), 1)
    acum = _split_dot(dt * a, (row >= col).astype(BF16), SSD_CUMSUM_PIECES, left=True)
    return dt, a, acum, acum.T, row, col


def _ssd_scan_fwd(xa, dtp, par, n_groups, *, name):
    t = xa.shape[0]
    ell, hd, hpg, ns, gps = SSD_CHUNK, SSD_HEAD_DIM, SSD_HEADS_PER_GROUP, SSD_STATE, SSD_GROUPS_PER_STEP
    gw = hpg * hd
    nc = t // ell
    b_blk0, c_blk0 = n_groups * gw // (ns * gps), (n_groups * gw // ns + n_groups) // gps

    def body(xs_ref, b_ref, c_ref, dtp_ref, par_ref, y_ref, sin_ref, st):
        @pl.when(pl.program_id(1) == 0)
        def _():
            st[...] = jnp.zeros_like(st)

        dt, _, acum, acum_t, row, col = _ssd_chunk_common(dtp_ref, par_ref)
        for gi in range(gps):
            bb = b_ref[:, gi * ns:(gi + 1) * ns].astype(BF16)
            cc = c_ref[:, gi * ns:(gi + 1) * ns].astype(BF16)
            cb = lax.dot_general(cc, bb, NT_DIMS, preferred_element_type=F32)
            s_all = st[gi]
            sin_ref[gi] = s_all
            c_s = lax.dot_general(cc, s_all.astype(BF16), NT_DIMS, preferred_element_type=F32)
            weighted, keep = [], []
            for hh in range(hpg):
                lanes = slice(gi * gw + hh * hd, gi * gw + (hh + 1) * hd)
                hl = gi * LANES + hh
                col_a, row_a = acum[:, hl:hl + 1], acum_t[hl:hl + 1, :]
                decay = jnp.exp(jnp.where(row >= col, col_a - row_a, NEG_BIG))
                xdt = xs_ref[:, lanes] * dt[:, hl:hl + 1]
                y = jnp.dot((cb * decay).astype(BF16), xdt.astype(BF16), preferred_element_type=F32)
                y_ref[:, lanes] = y + jnp.exp(col_a) * c_s[:, hh * hd:(hh + 1) * hd]
                a_last = acum[ell - 1:ell, hl:hl + 1]
                weighted.append((xdt * jnp.exp(a_last - col_a)).astype(BF16))
                keep.append(jnp.broadcast_to(jnp.exp(a_last), (hd, 1)))
            st[gi] = jnp.concatenate(keep, axis=0) * s_all + lax.dot_general(
                jnp.concatenate(weighted, axis=1), bb, (((0,), (0,)), ((), ())), preferred_element_type=F32)

    return pl.pallas_call(
        body, name=name, grid=(n_groups // gps, nc),
        in_specs=[pl.BlockSpec((ell, gps * gw), lambda g, c: (c, g)),
                  pl.BlockSpec((ell, gps * ns), lambda g, c: (c, b_blk0 + g)),
                  pl.BlockSpec((ell, gps * ns), lambda g, c: (c, c_blk0 + g)),
                  pl.BlockSpec((ell, gps * LANES), lambda g, c: (c, g)),
                  pl.BlockSpec((SUBLANES, gps * LANES), lambda g, c: (0, g))],
        out_specs=[pl.BlockSpec((ell, gps * gw), lambda g, c: (c, g)),
                   pl.BlockSpec((None, gps, gw, ns), lambda g, c: (c, g, 0, 0))],
        out_shape=[jax.ShapeDtypeStruct((t, n_groups * gw), F32),
                   jax.ShapeDtypeStruct((nc, n_groups, gw, ns), F32)],
        scratch_shapes=[pltpu.VMEM((gps, gw, ns), F32)],
        compiler_params=_params("parallel", "arbitrary"),
    )(xa, xa, xa, dtp, par)


def _ssd_scan_bwd(dy, xa, dtp, par, s_in, n_groups, *, name):
    t = xa.shape[0]
    ell, hd, hpg, ns, gps = SSD_CHUNK, SSD_HEAD_DIM, SSD_HEADS_PER_GROUP, SSD_STATE, SSD_GROUPS_PER_STEP
    gw = hpg * hd
    nc = t // ell
    b_blk0, c_blk0 = n_groups * gw // (ns * gps), (n_groups * gw // ns + n_groups) // gps
    nt_dims = (((1,), (1,)), ((), ()))
    tn_dims = (((0,), (0,)), ((), ()))

    def body(dy_ref, xs_ref, b_ref, c_ref, dtp_ref, par_ref, sin_ref,
             dxs_ref, db_ref, dc_ref, ddtp_ref, dpar_ref, dst):
        @pl.when(pl.program_id(1) == 0)
        def _():
            dst[...] = jnp.zeros_like(dst)
            dpar_ref[...] = jnp.zeros_like(dpar_ref)

        dtg, a_g, acum, acum_t, row, col = _ssd_chunk_common(dtp_ref, par_ref)
        lane = lax.broadcasted_iota(jnp.int32, (1, gps * LANES), 1)
        dacum = jnp.zeros((ell, gps * LANES), F32)
        xsum = jnp.zeros((ell, gps * LANES), F32)
        dsum = jnp.zeros((1, gps * LANES), F32)
        for gi, hh in [(gi, hh) for gi in range(gps) for hh in range(hpg)]:
            if hh == 0:
                bb = b_ref[:, gi * ns:(gi + 1) * ns].astype(BF16)
                cc = c_ref[:, gi * ns:(gi + 1) * ns].astype(BF16)
                cb = lax.dot_general(cc, bb, nt_dims, preferred_element_type=F32)
                cb_t = lax.dot_general(bb, cc, nt_dims, preferred_element_type=F32)
                dcb = jnp.zeros((ell, ell), F32)
                dcb_t = jnp.zeros((ell, ell), F32)
                s_all, ds_all = sin_ref[gi], dst[gi]
                c_s_all = lax.dot_general(cc, s_all.astype(BF16), nt_dims, preferred_element_type=F32)
                b_ds_all = lax.dot_general(bb, ds_all.astype(BF16), nt_dims, preferred_element_type=F32)
                s_ds = jnp.sum(s_all * ds_all, axis=1, keepdims=True)
                dy_decayed, x_weighted, keep = [], [], []
            lanes = slice(gi * gw + hh * hd, gi * gw + (hh + 1) * hd)
            head = slice(hh * hd, (hh + 1) * hd)
            hl = gi * LANES + hh
            onehot = (lane == hl).astype(F32)
            col_a, row_a = acum[:, hl:hl + 1], acum_t[hl:hl + 1, :]
            decay = jnp.exp(jnp.where(row >= col, col_a - row_a, NEG_BIG))
            decay_t = jnp.exp(jnp.where(col >= row, row_a - col_a, NEG_BIG))
            e_col = jnp.exp(col_a)
            a_last = acum[ell - 1:ell, hl:hl + 1]
            w = jnp.exp(a_last - col_a)
            e_last = jnp.exp(a_last)
            xs_h, dy_h = xs_ref[:, lanes], dy_ref[:, lanes]
            dt_h = dtg[:, hl:hl + 1]
            xdt = xs_h * dt_h
            xdt_b, dy_b = xdt.astype(BF16), dy_h.astype(BF16)
            dm_decay = lax.dot_general(dy_b, xdt_b, nt_dims, preferred_element_type=F32) * decay
            dm_decay_t = lax.dot_general(xdt_b, dy_b, nt_dims, preferred_element_type=F32) * decay_t
            dcb += dm_decay
            dcb_t += dm_decay_t
            m_t = cb_t * decay_t
            dac = jnp.sum(dm_decay * cb, axis=1, keepdims=True) - jnp.sum(dm_decay_t * cb_t, axis=1, keepdims=True)
            b_ds = b_ds_all[:, head]
            dxdt = jnp.dot(m_t.astype(BF16), dy_b, preferred_element_type=F32) + w * b_ds
            dac += jnp.sum(dy_h * c_s_all[:, head], axis=1, keepdims=True) * e_col
            q = jnp.sum(xdt * b_ds, axis=1, keepdims=True) * w
            dac -= q
            d_last = jnp.sum(q, axis=0, keepdims=True) + e_last * jnp.sum(s_ds[head], axis=0, keepdims=True)
            is_last = lax.broadcasted_iota(jnp.int32, (ell, 1), 0) == ell - 1
            dac += jnp.where(is_last, d_last, 0.0)
            dacum += dac * onehot
            dy_decayed.append((dy_h * e_col).astype(BF16))
            x_weighted.append((xdt * w).astype(BF16))
            keep.append(jnp.broadcast_to(e_last, (hd, 1)))
            dxs_ref[:, lanes] = dxdt * dt_h + dy_h * par_ref[2:3, hl:hl + 1]
            xsum += jnp.sum(dxdt * xs_h, axis=1, keepdims=True) * onehot
            dsum += jnp.sum(jnp.sum(dy_h * xs_h, axis=1, keepdims=True), axis=0, keepdims=True) * onehot
            if hh == hpg - 1:
                group = slice(gi * ns, (gi + 1) * ns)
                dy_all, x_all = jnp.concatenate(dy_decayed, axis=1), jnp.concatenate(x_weighted, axis=1)
                dc_ref[:, group] = (jnp.dot(dy_all, s_all.astype(BF16), preferred_element_type=F32)
                                    + jnp.dot(dcb.astype(BF16), bb, preferred_element_type=F32))
                db_ref[:, group] = (jnp.dot(x_all, ds_all.astype(BF16), preferred_element_type=F32)
                                    + jnp.dot(dcb_t.astype(BF16), cc, preferred_element_type=F32))
                dst[gi] = jnp.concatenate(keep, axis=0) * ds_all + lax.dot_general(
                    dy_all, cc, tn_dims, preferred_element_type=F32)
        dda = _split_dot(dacum, (col >= row).astype(BF16), SSD_CUMSUM_PIECES, left=True)
        ddtp = (xsum + dda * a_g) * _sigmoid(dtp_ref[...] + par_ref[0:1, :])
        ddtp_ref[...] = ddtp
        dpar_ref[0:1, :] += jnp.sum(ddtp, axis=0, keepdims=True)
        dpar_ref[1:2, :] += jnp.sum(dda * dtg, axis=0, keepdims=True) * a_g
        dpar_ref[2:3, :] += dsum

    rev = lambda i: nc - 1 - i
    return pl.pallas_call(
        body, name=name, grid=(n_groups // gps, nc),
        in_specs=[pl.BlockSpec((ell, gps * gw), lambda g, i: (rev(i), g)),
                  pl.BlockSpec((ell, gps * gw), lambda g, i: (rev(i), g)),
                  pl.BlockSpec((ell, gps * ns), lambda g, i: (rev(i), b_blk0 + g)),
                  pl.BlockSpec((ell, gps * ns), lambda g, i: (rev(i), c_blk0 + g)),
                  pl.BlockSpec((ell, gps * LANES), lambda g, i: (rev(i), g)),
                  pl.BlockSpec((SUBLANES, gps * LANES), lambda g, i: (0, g)),
                  pl.BlockSpec((None, gps, gw, ns), lambda g, i: (rev(i), g, 0, 0))],
        out_specs=[pl.BlockSpec((ell, gps * gw), lambda g, i: (rev(i), g)),
                   pl.BlockSpec((ell, gps * ns), lambda g, i: (rev(i), g)),
                   pl.BlockSpec((ell, gps * ns), lambda g, i: (rev(i), g)),
                   pl.BlockSpec((ell, gps * LANES), lambda g, i: (rev(i), g)),
                   pl.BlockSpec((SUBLANES, gps * LANES), lambda g, i: (0, g))],
        out_shape=[jax.ShapeDtypeStruct((t, n_groups * gw), F32), jax.ShapeDtypeStruct((t, n_groups * ns), F32),
                   jax.ShapeDtypeStruct((t, n_groups * ns), F32), jax.ShapeDtypeStruct((t, n_groups * LANES), F32),
                   jax.ShapeDtypeStruct((SUBLANES, n_groups * LANES), F32)],
        scratch_shapes=[pltpu.VMEM((gps, gw, ns), F32)],
        compiler_params=_params("parallel", "arbitrary"),
    )(dy, xa, xa, xa, dtp, par, s_in)


def _ssd_gate_fwd(y, xa, zx, d_rep, out_norm, *, name):
    t, di = y.shape
    gw = SSD_HEADS_PER_GROUP * SSD_HEAD_DIM
    tm = _tile(t, 512, SUBLANES)

    def body(y_ref, xs_ref, z_ref, d_ref, n_ref, o_ref):
        zv = z_ref[...]
        gt = (y_ref[...] + d_ref[...] * xs_ref[...]) * (zv * _sigmoid(zv))
        r = lax.rsqrt(jnp.mean(gt * gt, axis=-1, keepdims=True) + NORM_EPS)
        o_ref[...] = (gt * r * n_ref[...]).astype(BF16)

    blk = pl.BlockSpec((tm, gw), lambda i, g: (i, g))
    vec = pl.BlockSpec((1, gw), lambda i, g: (0, g))
    return pl.pallas_call(
        body, name=name, grid=(t // tm, di // gw),
        in_specs=[blk, blk, blk, vec, vec], out_specs=blk,
        out_shape=jax.ShapeDtypeStruct((t, di), BF16),
        compiler_params=_params("parallel", "parallel"),
    )(y, xa, zx, d_rep, out_norm)


def _ssd_gate_bwd(dgn, y, xa, zx, d_rep, out_norm, *, name):
    t, di = y.shape
    gw = SSD_HEADS_PER_GROUP * SSD_HEAD_DIM
    tm = _tile(t, 512, SUBLANES)

    def body(dg_ref, y_ref, xs_ref, z_ref, d_ref, n_ref, dy_ref, dz_ref, dn_ref):
        @pl.when(pl.program_id(1) == 0)
        def _():
            dn_ref[...] = jnp.zeros_like(dn_ref)

        zv = z_ref[...]
        sg = _sigmoid(zv)
        sz = zv * sg
        y2 = y_ref[...] + d_ref[...] * xs_ref[...]
        gt = y2 * sz
        r = lax.rsqrt(jnp.mean(gt * gt, axis=-1, keepdims=True) + NORM_EPS)
        ghat = gt * r
        dgv = dg_ref[...]
        dn_ref[...] += jnp.sum(dgv * ghat, axis=0, keepdims=True)
        u = dgv * n_ref[...]
        dgt = r * (u - ghat * jnp.mean(u * ghat, axis=-1, keepdims=True))
        dy_ref[...] = dgt * sz
        dz_ref[...] = (dgt * y2 * (sg * (1.0 + zv * (1.0 - sg)))).astype(BF16)

    blk = pl.BlockSpec((tm, gw), lambda g, i: (i, g))
    vec = pl.BlockSpec((1, gw), lambda g, i: (0, g))
    return pl.pallas_call(
        body, name=name, grid=(di // gw, t // tm),
        in_specs=[blk, blk, blk, blk, vec, vec], out_specs=[blk, blk, vec],
        out_shape=[jax.ShapeDtypeStruct((t, di), F32), jax.ShapeDtypeStruct((t, di), BF16),
                   jax.ShapeDtypeStruct((1, di), F32)],
        compiler_params=_params("parallel", "arbitrary"),
    )(dgn, y, xa, zx, d_rep, out_norm)


def _ssd_mixer_fwd(x, gain, w_zx_t, w_dt_t, conv_w, conv_b, par, d_rep, out_norm, w_out, tag):
    di = w_out.shape[0]
    n_groups = di // (SSD_HEADS_PER_GROUP * SSD_HEAD_DIM)
    h = _rms_fwd(x, gain, name=f"ssd_norm_{tag}")
    zx = _matmul(h, w_zx_t, "nt", name=f"ssd_in_{tag}")
    dtp = _matmul(h, w_dt_t, "nt", name=f"ssd_dt_{tag}")
    xa = _ssd_conv_fwd(zx, conv_w, conv_b, di, name=f"ssd_conv_{tag}")
    y, s_in = _ssd_scan_fwd(xa, dtp, par, n_groups, name=f"ssd_scan_{tag}")
    gn = _ssd_gate_fwd(y, xa, zx, d_rep, out_norm, name=f"ssd_gate_{tag}")
    x_new = _matmul(gn, w_out, "nn", resid=x, name=f"ssd_out_{tag}")
    return x_new, (x, h, zx, dtp, xa, y, s_in, gn)


def _ssd_mixer_bwd(dx, saved, gain, w_zx_t, w_dt_t, conv_w, conv_b, par, d_rep, out_norm, w_out, grads, row_out,
                   tag):
    x, h, zx, dtp, xa, y, s_in, gn = saved
    di = w_out.shape[0]
    n_groups = di // (SSD_HEADS_PER_GROUP * SSD_HEAD_DIM)
    dgn = _matmul(dx, w_out, "nt", name=f"ssd_dgn_{tag}")
    grads = _matmul_tn_into(grads, gn, dx, row_out, name=f"ssd_dwout_{tag}")
    dy2, dz, dnorm = _ssd_gate_bwd(dgn, y, xa, zx, d_rep, out_norm, name=f"ssd_dgate_{tag}")
    dxs, db, dc, ddtp, dpar = _ssd_scan_bwd(dy2, xa, dtp, par, s_in, n_groups, name=f"ssd_dscan_{tag}")
    dxa = jnp.concatenate([dxs, db, dc], axis=1)
    dxbc, dconv_w, dconv_b = _ssd_conv_bwd(dxa, zx, conv_w, conv_b, di, name=f"ssd_dconv_{tag}")
    dzx = jnp.concatenate([dz, dxbc], axis=1)
    dw_zx_t = _matmul_tn(dzx, h, name=f"ssd_dwin_{tag}")
    dw_dt_t = _matmul_tn(ddtp, h, name=f"ssd_dwdt_{tag}")
    dh = _matmul(dzx, w_zx_t, "nn", name=f"ssd_dh_{tag}")
    dh = _matmul(ddtp, w_dt_t, "nn", resid=dh, name=f"ssd_dhdt_{tag}")
    dx_in, dgain = _rms_bwd(x, gain, dh, dx, name=f"ssd_dnorm_{tag}")
    return dx_in, grads, dw_zx_t, dw_dt_t, dconv_w, dconv_b, dpar, dnorm, dgain


def _sb_qk_norm_fwd(qkv, gains, *, name):
    ns, t, _ = qkv.shape
    per = ns // 3
    tm = _tile(t, 1024, SUBLANES)
    inv_sqrt_d = 1.0 / math.sqrt(SB_HEAD_DIM)

    def body(x_ref, g_ref, o_ref):
        kind = pl.program_id(0) // per
        xv = x_ref[...]

        @pl.when(kind == 2)
        def _():
            o_ref[...] = xv.astype(BF16)

        @pl.when(kind < 2)
        def _():
            left = lax.broadcasted_iota(jnp.int32, (1, LANES), 1) < SB_HEAD_DIM
            sq = xv * xv
            ms = jnp.where(left, jnp.sum(jnp.where(left, sq, 0.0), axis=1, keepdims=True),
                           jnp.sum(jnp.where(left, 0.0, sq), axis=1, keepdims=True)) * (1.0 / SB_HEAD_DIM)
            y = xv * lax.rsqrt(ms + NORM_EPS) * g_ref[pl.ds(kind, 1), :]
            o_ref[...] = (y * jnp.where(kind == 0, inv_sqrt_d, 1.0)).astype(BF16)

    blk = pl.BlockSpec((None, tm, LANES), lambda s, i: (s, i, 0))
    return pl.pallas_call(
        body, name=name, grid=(ns, t // tm),
        in_specs=[blk, pl.BlockSpec((SUBLANES, LANES), lambda s, i: (0, 0))], out_specs=blk,
        out_shape=jax.ShapeDtypeStruct((ns, t, LANES), BF16),
        compiler_params=_params("parallel", "parallel"),
    )(qkv, gains)


def _sb_qk_norm_bwd(dq, dk, dv, qkv, gains, *, name):
    ns, t, _ = qkv.shape
    per = ns // 3
    tm = _tile(t, 1024, SUBLANES)
    inv_sqrt_d = 1.0 / math.sqrt(SB_HEAD_DIM)

    def body(dq_ref, dk_ref, dv_ref, x_ref, g_ref, o_ref, dg_ref):
        s = pl.program_id(0)
        kind = s // per

        @pl.when((s == 0) & (pl.program_id(1) == 0))
        def _():
            dg_ref[...] = jnp.zeros_like(dg_ref)

        @pl.when(kind == 2)
        def _():
            o_ref[...] = dv_ref[...].astype(BF16)

        @pl.when(kind < 2)
        def _():
            xv = x_ref[...]
            dy = jnp.where(kind == 0, dq_ref[...] * inv_sqrt_d, dk_ref[...])
            left = lax.broadcasted_iota(jnp.int32, (1, LANES), 1) < SB_HEAD_DIM

            def halves(v):
                return jnp.where(left, jnp.sum(jnp.where(left, v, 0.0), axis=1, keepdims=True),
                                 jnp.sum(jnp.where(left, 0.0, v), axis=1, keepdims=True))

            r = lax.rsqrt(halves(xv * xv) * (1.0 / SB_HEAD_DIM) + NORM_EPS)
            xhat = xv * r
            u = dy * g_ref[pl.ds(kind, 1), :]
            o_ref[...] = (r * (u - xhat * halves(u * xhat) * (1.0 / SB_HEAD_DIM))).astype(BF16)
            dg_ref[pl.ds(kind, 1), :] += jnp.sum(dy * xhat, axis=0, keepdims=True)

    def grad_blk(kind):
        def index(s, i):
            mine = (s >= kind * per) & (s < (kind + 1) * per)
            return jnp.where(mine, s - kind * per, 0), jnp.where(mine, i, 0), 0
        return pl.BlockSpec((None, tm, LANES), index)

    blk = pl.BlockSpec((None, tm, LANES), lambda s, i: (s, i, 0))
    vec = pl.BlockSpec((SUBLANES, LANES), lambda s, i: (0, 0))
    return pl.pallas_call(
        body, name=name, grid=(ns, t // tm),
        in_specs=[grad_blk(0), grad_blk(1), grad_blk(2), blk, vec], out_specs=[blk, vec],
        out_shape=[jax.ShapeDtypeStruct((ns, t, LANES), BF16), jax.ShapeDtypeStruct((SUBLANES, LANES), F32)],
        compiler_params=_params("arbitrary", "arbitrary"),
    )(dq, dk, dv, qkv, gains)


def _split_dot(v, ones_mat, pieces, left=False):
    total, rest = None, v
    for p in range(pieces):
        part = rest.astype(BF16)
        if p + 1 < pieces:
            rest = rest - part.astype(F32)
        d = (jnp.dot(ones_mat, part, preferred_element_type=F32) if left
             else jnp.dot(part, ones_mat, preferred_element_type=F32))
        total = d if total is None else total + d
    return total


LOGIT_SUM_PIECES = 2
GRAD_SUM_PIECES = 2
LOG_WEIGHT_UNDERFLOW = -105.0


def _sb_attn_fwd(qkv_n, n_heads, *, name):
    ns, t, _ = qkv_n.shape
    per = ns // 3
    bq, blk, hd = SB_QUERY_BLOCK, SB_BLOCK, SB_HEAD_DIM
    nq, n_diag = t // bq, bq // blk

    def body(q_ref, k_ref, v_ref, o_ref):
        i = pl.program_id(1)
        row = lax.broadcasted_iota(jnp.int32, (blk, blk), 0)
        col = lax.broadcasted_iota(jnp.int32, (blk, blk), 1)
        later_keys = (row > col).astype(BF16)
        qry = lax.broadcasted_iota(jnp.int32, (bq, blk), 0)
        key = lax.broadcasted_iota(jnp.int32, (bq, blk), 1)

        def tile(kb, carry, key_offset):
            out = []
            start = pl.multiple_of(kb * blk, blk)
            for hf in range(2):
                lanes = slice(hf * hd, (hf + 1) * hd)
                run, acc = carry[hf]
                z = lax.dot_general(q_ref[:, lanes], k_ref[pl.ds(start, blk), lanes], NT_DIMS,
                                    preferred_element_type=F32)
                sp = _softplus(z)
                lm = -sp if key_offset is None else jnp.where(key + key_offset < qry, -sp, 0.0)
                after = _split_dot(lm, later_keys, LOGIT_SUM_PIECES) + run
                a = jnp.exp(z - sp + after)
                if key_offset is not None:
                    a = jnp.where(key + key_offset < qry, a, 0.0)
                acc = acc + jnp.dot(a.astype(BF16), v_ref[pl.ds(start, blk), lanes], preferred_element_type=F32)
                out.append((run + jnp.sum(lm, axis=1, keepdims=True), acc))
            return tuple(out)

        def live(carry):
            return jnp.max(jnp.maximum(carry[0][0], carry[1][0])) > LOG_WEIGHT_UNDERFLOW

        def step(state):
            s, _, carry = state
            carry = tile(n_diag * i - 1 - s, carry, None)
            return s + 1, live(carry), carry

        carry = tuple((jnp.zeros((bq, 1), F32), jnp.zeros((bq, hd), F32)) for _ in range(2))
        for j in reversed(range(n_diag)):
            carry = tile(n_diag * i + j, carry, j * blk)
        _, _, carry = lax.while_loop(lambda st: (st[0] < n_diag * i) & st[1], step,
                                     (jnp.int32(0), live(carry), carry))
        o_ref[...] = jnp.concatenate([carry[0][1], carry[1][1]], axis=1)

    return pl.pallas_call(
        body, name=name, grid=(per, nq),
        in_specs=[pl.BlockSpec((None, bq, LANES), lambda p, i: (p, i, 0)),
                  pl.BlockSpec((None, t, LANES), lambda p, i: (per + p, 0, 0)),
                  pl.BlockSpec((None, t, LANES), lambda p, i: (2 * per + p, 0, 0))],
        out_specs=pl.BlockSpec((bq, LANES), lambda p, i: (i, p)),
        out_shape=jax.ShapeDtypeStruct((t, n_heads * hd), F32),
        compiler_params=_params("parallel", "arbitrary"),
    )(qkv_n, qkv_n, qkv_n)


def _sb_attn_bwd(do, qkv_n, *, name):
    ns, t, _ = qkv_n.shape
    per = ns // 3
    bq, blk, hd = SB_QUERY_BLOCK, SB_BLOCK, SB_HEAD_DIM
    nq, n_diag = t // bq, bq // blk
    nt_dims = (((1,), (1,)), ((), ()))
    tn_dims = (((0,), (0,)), ((), ()))

    def body(q_ref, k_ref, v_ref, do_ref, dq_ref, dk_ref, dv_ref):
        i = pl.program_id(1)

        @pl.when(i == 0)
        def _():
            dk_ref[...] = jnp.zeros_like(dk_ref)
            dv_ref[...] = jnp.zeros_like(dv_ref)

        row = lax.broadcasted_iota(jnp.int32, (blk, blk), 0)
        col = lax.broadcasted_iota(jnp.int32, (blk, blk), 1)
        later_keys = (col > row).astype(BF16)
        earlier_keys = (col < row).astype(BF16)
        key = lax.broadcasted_iota(jnp.int32, (blk, bq), 0)
        qry = lax.broadcasted_iota(jnp.int32, (blk, bq), 1)
        halves = [slice(hf * hd, (hf + 1) * hd) for hf in range(2)]
        q_hs = [q_ref[:, lanes] for lanes in halves]
        do_bs = [do_ref[:, lanes].astype(BF16) for lanes in halves]

        def scores(kb, hf, key_offset):
            k_blk = k_ref[pl.ds(pl.multiple_of(kb * blk, blk), blk), halves[hf]]
            z = lax.dot_general(k_blk, q_hs[hf], nt_dims, preferred_element_type=F32)
            sp = _softplus(z)
            return k_blk, z, sp, (-sp if key_offset is None else jnp.where(key + key_offset < qry, -sp, 0.0))

        def add_column_sums(tots, kb, key_offset):
            return [tots[hf] + jnp.sum(scores(kb, hf, key_offset)[3], axis=0, keepdims=True) for hf in range(2)]

        def live(tots):
            return jnp.max(jnp.maximum(tots[0], tots[1])) > LOG_WEIGHT_UNDERFLOW

        def reach(state):
            s, _, tots = state
            tots = add_column_sums(tots, n_diag * i - 1 - s, None)
            return s + 1, live(tots), tots

        tots = [jnp.zeros((1, bq), F32)] * 2
        for j in reversed(range(n_diag)):
            tots = add_column_sums(tots, n_diag * i + j, j * blk)
        reached, _, tots = lax.while_loop(lambda st: (st[0] < n_diag * i) & st[1], reach,
                                          (jnp.int32(0), live(tots), tots))

        def tile(kb, carry, key_offset):
            out = []
            start = pl.multiple_of(kb * blk, blk)
            for hf, lanes in enumerate(halves):
                seen, gsum, dq = carry[hf]
                q_h, do_b = q_hs[hf], do_bs[hf]
                k_blk, z, sp, lm = scores(kb, hf, key_offset)
                blk_tot = jnp.sum(lm, axis=0, keepdims=True)
                after = _split_dot(lm, later_keys, LOGIT_SUM_PIECES, left=True) + (tots[hf] - seen - blk_tot)
                a = jnp.exp(z - sp + after)
                if key_offset is not None:
                    a = jnp.where(key + key_offset < qry, a, 0.0)
                da = lax.dot_general(v_ref[pl.ds(start, blk), lanes], do_b, nt_dims, preferred_element_type=F32)
                g = da * a
                before = _split_dot(g, earlier_keys, GRAD_SUM_PIECES, left=True) + gsum
                omb = jnp.exp(-sp)
                dz = g * omb - (1.0 - omb) * before
                if key_offset is not None:
                    dz = jnp.where(key + key_offset < qry, dz, 0.0)
                dz_b = dz.astype(BF16)
                dk_ref[pl.ds(start, blk), lanes] += jnp.dot(dz_b, q_h, preferred_element_type=F32)
                dv_ref[pl.ds(start, blk), lanes] += jnp.dot(a.astype(BF16), do_b, preferred_element_type=F32)
                dq = dq + lax.dot_general(dz_b, k_blk, tn_dims, preferred_element_type=F32)
                out.append((seen + blk_tot, gsum + jnp.sum(g, axis=0, keepdims=True), dq))
            return tuple(out)

        init = tuple((jnp.zeros((1, bq), F32), jnp.zeros((1, bq), F32), jnp.zeros((bq, hd), F32))
                     for _ in range(2))
        carry = lax.fori_loop(n_diag * i - reached, n_diag * i, lambda kb, c: tile(kb, c, None), init)
        for j in range(n_diag):
            carry = tile(n_diag * i + j, carry, j * blk)
        dq_ref[...] = jnp.concatenate([carry[0][2], carry[1][2]], axis=1)

    full = lambda off: pl.BlockSpec((None, t, LANES), lambda p, i: (off + p, 0, 0))
    q_blk = pl.BlockSpec((None, bq, LANES), lambda p, i: (p, i, 0))
    slab = jax.ShapeDtypeStruct((per, t, LANES), F32)
    return pl.pallas_call(
        body, name=name, grid=(per, nq),
        in_specs=[q_blk, full(per), full(2 * per), pl.BlockSpec((bq, LANES), lambda p, i: (i, p))],
        out_specs=[q_blk, full(0), full(0)],
        out_shape=[slab, slab, slab],
        compiler_params=_params("parallel", "arbitrary"),
    )(qkv_n, qkv_n, qkv_n, do)


def _sb_mixer_fwd(x, gain, w_qkv_t, qk_gains, w_out, tag):
    n_heads = w_out.shape[0] // SB_HEAD_DIM
    h = _rms_fwd(x, gain, name=f"sb_norm_{tag}")
    qkv = _matmul(h, w_qkv_t, "nt", out_slabs=True, tn_cap=512, name=f"sb_qkv_{tag}")
    qkv_n = _sb_qk_norm_fwd(qkv, qk_gains, name=f"sb_qknorm_{tag}")
    o = _sb_attn_fwd(qkv_n, n_heads, name=f"sb_attn_{tag}")
    x_new = _matmul(o, w_out, "nn", resid=x, name=f"sb_out_{tag}")
    return x_new, (x, h, qkv, qkv_n, o)


def _sb_mixer_bwd(dx, saved, gain, w_qkv_t, qk_gains, w_out, grads, row_qkv, row_out, tag):
    x, h, qkv, qkv_n, o = saved
    do = _matmul(dx, w_out, "nt", name=f"sb_do_{tag}")
    grads = _matmul_tn_into(grads, o, dx, row_out, name=f"sb_dwout_{tag}")
    dq, dk, dv = _sb_attn_bwd(do, qkv_n, name=f"sb_dattn_{tag}")
    dqkv, dqk_gains = _sb_qk_norm_bwd(dq, dk, dv, qkv, qk_gains, name=f"sb_dqknorm_{tag}")
    grads = _matmul_tn_into(grads, dqkv, h, row_qkv, a_slabs=True, name=f"sb_dwqkv_{tag}")
    dh = _matmul(dqkv, w_qkv_t, "nn", a_slabs=True, name=f"sb_dh_{tag}")
    dx_in, dgain = _rms_bwd(x, gain, dh, dx, name=f"sb_dnorm_{tag}")
    return dx_in, grads, dqk_gains, dgain


MESH = pl.DeviceIdType.MESH


def _position():
    return lax.axis_index("x"), lax.axis_index("y"), lax.axis_index("c")


def _all_gather(shard, *, name, in_vmem):
    rows, n = shard.shape
    space = pltpu.VMEM if in_vmem else pltpu.HBM

    def body(x_ref, out_ref, send_sems, recv_sems, local_sem):
        x, y, c = _position()
        me, sibling = (x, y, c), (x, y, 1 - c)
        chips = [(1 - x, y), (x, 1 - y), (1 - x, 1 - y)]

        def block(px, py, pc):
            return out_ref.at[4 * px + 2 * py + pc]

        def copy(k, blk, to, src=None):
            return pltpu.make_async_remote_copy(
                src_ref=block(*blk) if src is None else src, dst_ref=block(*blk),
                send_sem=send_sems.at[k], recv_sem=recv_sems.at[k], device_id=to, device_id_type=MESH)

        mine = pltpu.make_async_copy(x_ref, block(*me), local_sem)
        mine.start()
        first = [copy(0, me, sibling, src=x_ref)]
        first += [copy(1 + j, me, (*chip, c), src=x_ref) for j, chip in enumerate(chips)]
        for cp in first:
            cp.start()
        passed = [copy(4 + j, (*chip, c), sibling) for j, chip in enumerate(chips)]
        for j, chip in enumerate(chips):
            copy(1 + j, (*chip, c), me).wait_recv()
            passed[j].start()
        copy(0, sibling, me).wait_recv()
        for j, chip in enumerate(chips):
            copy(4 + j, (*chip, 1 - c), me).wait_recv()
        for cp in first + passed:
            cp.wait_send()
        mine.wait()

    return pl.pallas_call(
        body, name=name,
        out_shape=jax.ShapeDtypeStruct((N_DEV, rows, n), shard.dtype),
        in_specs=[pl.BlockSpec(memory_space=space)], out_specs=pl.BlockSpec(memory_space=space),
        scratch_shapes=[pltpu.SemaphoreType.DMA((7,)), pltpu.SemaphoreType.DMA((7,)), pltpu.SemaphoreType.DMA],
        compiler_params=pltpu.CompilerParams(vmem_limit_bytes=V7X_VMEM_LIMIT_BYTES),
    )(shard)


def _all_gather_forwarding(shard, *, name):
    rows, n = shard.shape
    half = rows // 2
    assert rows % (4 * SUBLANES) == 0

    def body(x_ref, out_ref, send_sems, recv_sems, local_sem):
        x, y, c = _position()
        me, sibling = (x, y, c), (x, y, 1 - c)
        x_nbr, y_nbr, diag = (1 - x, y), (x, 1 - y), (1 - x, 1 - y)
        lower, upper = pl.ds(0, half), pl.ds(half, half)

        def block(px, py, pc, part=None):
            ref = out_ref.at[4 * px + 2 * py + pc]
            return ref if part is None else ref.at[part]

        def copy(k, blk, to, src=None, part=None):
            return pltpu.make_async_remote_copy(
                src_ref=block(*blk, part) if src is None else src, dst_ref=block(*blk, part),
                send_sem=send_sems.at[k], recv_sem=recv_sems.at[k], device_id=to, device_id_type=MESH)

        mine = pltpu.make_async_copy(x_ref, block(*me), local_sem)
        mine.start()
        sent = [copy(0, me, sibling, src=x_ref), copy(1, me, (*x_nbr, c), src=x_ref),
                copy(2, me, (*y_nbr, c), src=x_ref)]
        for cp in sent:
            cp.start()
        copy(1, (*x_nbr, c), me).wait_recv()
        onward = [copy(3, (*x_nbr, c), (*y_nbr, c), part=lower), copy(5, (*x_nbr, c), sibling)]
        for cp in onward:
            cp.start()
        copy(2, (*y_nbr, c), me).wait_recv()
        onward += [copy(4, (*y_nbr, c), (*x_nbr, c), part=upper), copy(6, (*y_nbr, c), sibling)]
        for cp in onward[2:]:
            cp.start()
        copy(3, (*diag, c), me, part=lower).wait_recv()
        copy(4, (*diag, c), me, part=upper).wait_recv()
        onward.append(copy(7, (*diag, c), sibling))
        onward[-1].start()
        sent += onward
        copy(0, sibling, me).wait_recv()
        for k, chip in ((5, x_nbr), (6, y_nbr), (7, diag)):
            copy(k, (*chip, 1 - c), me).wait_recv()
        for cp in sent:
            cp.wait_send()
        mine.wait()

    hbm = pl.BlockSpec(memory_space=pltpu.HBM)
    return pl.pallas_call(
        body, name=name,
        out_shape=jax.ShapeDtypeStruct((N_DEV, rows, n), shard.dtype), in_specs=[hbm], out_specs=hbm,
        scratch_shapes=[pltpu.SemaphoreType.DMA((8,)), pltpu.SemaphoreType.DMA((8,)), pltpu.SemaphoreType.DMA],
    )(shard)


def _exchange_sibling(parts, *, name):
    _, nchip, rows, n = parts.shape

    def body(p_ref, recv_ref, send_sem, recv_sem):
        x, y, c = _position()
        cp = pltpu.make_async_remote_copy(src_ref=p_ref.at[1 - c], dst_ref=recv_ref, send_sem=send_sem,
                                          recv_sem=recv_sem, device_id=(x, y, 1 - c), device_id_type=MESH)
        cp.start()
        cp.wait()

    return pl.pallas_call(
        body, name=name,
        out_shape=jax.ShapeDtypeStruct((nchip, rows, n), parts.dtype),
        in_specs=[pl.BlockSpec(memory_space=pltpu.HBM)], out_specs=pl.BlockSpec(memory_space=pltpu.HBM),
        scratch_shapes=[pltpu.SemaphoreType.DMA, pltpu.SemaphoreType.DMA],
    )(parts)


def _exchange_chips(chip_sums, *, name):
    _, rows, n = chip_sums.shape

    def body(s_ref, recv_ref, send_sems, recv_sems):
        x, y, c = _position()
        chips = [(1 - x, y), (x, 1 - y), (1 - x, 1 - y)]
        copies = [pltpu.make_async_remote_copy(
            src_ref=s_ref.at[2 * cx + cy], dst_ref=recv_ref.at[j], send_sem=send_sems.at[j],
            recv_sem=recv_sems.at[j], device_id=(cx, cy, c), device_id_type=MESH)
            for j, (cx, cy) in enumerate(chips)]
        for cp in copies:
            cp.start()
        for cp in copies:
            cp.wait()

    return pl.pallas_call(
        body, name=name,
        out_shape=jax.ShapeDtypeStruct((3, rows, n), chip_sums.dtype),
        in_specs=[pl.BlockSpec(memory_space=pltpu.HBM)], out_specs=pl.BlockSpec(memory_space=pltpu.HBM),
        scratch_shapes=[pltpu.SemaphoreType.DMA((3,)), pltpu.SemaphoreType.DMA((3,))],
    )(chip_sums)


def _add_pairs(parts, recv, c_mine, *, name):
    _, nchip, rows, n = parts.shape
    tr = _tile(rows, 512, SUBLANES)

    def body(c_ref, a_ref, b_ref, o_ref, wire_ref):
        s = a_ref[...] + b_ref[...]
        o_ref[...] = s
        wire_ref[...] = s.astype(WIRE_DTYPE)

    out_blk = pl.BlockSpec((None, tr, n), lambda k, i, c: (k, i, 0))
    return pl.pallas_call(
        body, name=name,
        grid_spec=pltpu.PrefetchScalarGridSpec(
            num_scalar_prefetch=1, grid=(nchip, rows // tr),
            in_specs=[pl.BlockSpec((None, None, tr, n), lambda k, i, c: (c[0], k, i, 0)),
                      pl.BlockSpec((None, tr, n), lambda k, i, c: (k, i, 0))],
            out_specs=[out_blk, out_blk]),
        out_shape=[jax.ShapeDtypeStruct((nchip, rows, n), parts.dtype),
                   jax.ShapeDtypeStruct((nchip, rows, n), WIRE_DTYPE)],
        compiler_params=_params("parallel", "parallel"),
    )(c_mine, parts, recv)


def _adamw_math(w, g, m, v):
    m = ADAM_B1 * m + (1.0 - ADAM_B1) * g
    v = ADAM_B2 * v + (1.0 - ADAM_B2) * (g * g)
    m_hat = m / (1.0 - ADAM_B1 ** ADAM_STEP)
    v_hat = v / (1.0 - ADAM_B2 ** ADAM_STEP)
    delta = -ADAM_LR * (m_hat / (jnp.sqrt(v_hat) + ADAM_EPS) + ADAM_WD * w)
    return delta, m, v


def _adamw_sharded(chip_sums, recv, k_mine, w, m, v, *, name):
    rows, n = w.shape
    tr = _tile(rows, 256, SUBLANES)

    def body(k_ref, s_ref, r_ref, w_ref, m_ref, v_ref, g_out, d_out, m_out, v_out):
        g = ((s_ref[...] + r_ref[0].astype(F32)) + r_ref[1].astype(F32)) + r_ref[2].astype(F32)
        delta, m_new, v_new = _adamw_math(w_ref[...], g, m_ref[...], v_ref[...])
        g_out[...] = g
        d_out[...] = delta
        m_out[...] = m_new
        v_out[...] = v_new

    blk = pl.BlockSpec((tr, n), lambda i, k: (i, 0))
    out = jax.ShapeDtypeStruct((rows, n), F32)
    return pl.pallas_call(
        body, name=name,
        grid_spec=pltpu.PrefetchScalarGridSpec(
            num_scalar_prefetch=1, grid=(rows // tr,),
            in_specs=[pl.BlockSpec((None, tr, n), lambda i, k: (k[0], i, 0)),
                      pl.BlockSpec((3, tr, n), lambda i, k: (0, i, 0)), blk, blk, blk],
            out_specs=[blk, blk, blk, blk]),
        out_shape=[out, out, out, out],
        compiler_params=_params("parallel"),
    )(k_mine, chip_sums, recv, w, m, v)


SMALL_ROWS = 40
ROW_MIX_NORM, ROW_FFN_NORM, ROW_CONV_B, ROW_OUT_NORM, ROW_POOL_SCALE, ROW_CONV_W = 0, 4, 8, 12, 14, 16
ROW_SSD_VEC, ROW_QK_GAIN, ROW_LOSS = 32, 33, 34


def _adamw_small(gathered, w, m, v, *, name):
    _, rows, n = gathered.shape

    def body(a_ref, w_ref, m_ref, v_ref, g_out, d_out, m_out, v_out):
        g = a_ref[0]
        for d in range(1, N_DEV):
            g = g + a_ref[d]
        row = lax.broadcasted_iota(jnp.int32, (rows, 1), 0)
        g = jnp.where(row == ROW_QK_GAIN, g + pltpu.roll(g, SB_HEAD_DIM, 1), g)
        g = jnp.where(row == ROW_LOSS, jnp.sum(g, axis=1, keepdims=True), g)
        g_out[...] = g
        delta, m_new, v_new = _adamw_math(w_ref[...], g, m_ref[...], v_ref[...])
        d_out[...] = delta
        m_out[...] = m_new
        v_out[...] = v_new

    out = jax.ShapeDtypeStruct((rows, n), F32)
    return pl.pallas_call(body, name=name, out_shape=[out, out, out, out])(gathered, w, m, v)


BIG_WEIGHTS = ("ffn_gate", "ffn_up", "ffn_down", "sb_qkv", "ssd_out", "pool_in", "sb_out", "pool_group", "ssd_in")
COLUMN_SHARDED = ("ssd_in", "sb_qkv", "ffn_gate", "ffn_up")
ROW_PAD = 512
WIRE_DTYPE = jnp.bfloat16


def _to_rows(name, shard, d):
    if name in COLUMN_SHARDED:
        shard = jnp.swapaxes(shard, -1, -2)
    return shard.reshape(-1, d)


def _from_rows(name, rows, shard_shape):
    if name in COLUMN_SHARDED:
        lead, k, n = shard_shape
        return jnp.swapaxes(rows.reshape(lead, n, k), -1, -2)
    return rows.reshape(shard_shape)


def _pad_rows(a, total):
    return jnp.pad(a, ((0, total - a.shape[0]),) + ((0, 0),) * (a.ndim - 1))


def _exact_bf16_rows(v, d):
    words = lax.bitcast_convert_type(v.reshape(-1), WIRE_DTYPE).reshape(-1)
    return _pad_rows(words, -(-words.shape[0] // d) * d).reshape(-1, d)


def _exact_f32(rows, count):
    words = rows.reshape(rows.shape[0], -1)[:, :2 * count].reshape(rows.shape[0], count, 2)
    return lax.bitcast_convert_type(words, F32)


def _device_blocks(full, d):
    return full.reshape(N_DEV, -1, d)


def kernel(x, mix_norm, pool_in, pool_group, pool_scale, ssd_in, ssd_conv_w, ssd_conv_b, ssd_dt_bias, ssd_a_log, ssd_d, ssd_out_norm, ssd_out, sb_qkv, sb_q_norm, sb_k_norm, sb_out, ffn_norm, ffn_gate, ffn_up, ffn_down, loss_target, m_mix_norm, m_pool_in, m_pool_group, m_pool_scale, m_ssd_in, m_ssd_conv_w, m_ssd_conv_b, m_ssd_dt_bias, m_ssd_a_log, m_ssd_d, m_ssd_out_norm, m_ssd_out, m_sb_qkv, m_sb_q_norm, m_sb_k_norm, m_sb_out, m_ffn_norm, m_ffn_gate, m_ffn_up, m_ffn_down, v_mix_norm, v_pool_in, v_pool_group, v_pool_scale, v_ssd_in, v_ssd_conv_w, v_ssd_conv_b, v_ssd_dt_bias, v_ssd_a_log, v_ssd_d, v_ssd_out_norm, v_ssd_out, v_sb_qkv, v_sb_q_norm, v_sb_k_norm, v_sb_out, v_ffn_norm, v_ffn_gate, v_ffn_up, v_ffn_down):
    weights = dict(mix_norm=mix_norm, pool_in=pool_in, pool_group=pool_group, pool_scale=pool_scale, ssd_in=ssd_in,
                   ssd_conv_w=ssd_conv_w, ssd_conv_b=ssd_conv_b, ssd_dt_bias=ssd_dt_bias, ssd_a_log=ssd_a_log,
                   ssd_d=ssd_d, ssd_out_norm=ssd_out_norm, ssd_out=ssd_out, sb_qkv=sb_qkv, sb_q_norm=sb_q_norm,
                   sb_k_norm=sb_k_norm, sb_out=sb_out, ffn_norm=ffn_norm, ffn_gate=ffn_gate, ffn_up=ffn_up,
                   ffn_down=ffn_down)
    mom1 = dict(mix_norm=m_mix_norm, pool_in=m_pool_in, pool_group=m_pool_group, pool_scale=m_pool_scale,
                ssd_in=m_ssd_in, ssd_conv_w=m_ssd_conv_w, ssd_conv_b=m_ssd_conv_b, ssd_dt_bias=m_ssd_dt_bias,
                ssd_a_log=m_ssd_a_log, ssd_d=m_ssd_d, ssd_out_norm=m_ssd_out_norm, ssd_out=m_ssd_out,
                sb_qkv=m_sb_qkv, sb_q_norm=m_sb_q_norm, sb_k_norm=m_sb_k_norm, sb_out=m_sb_out,
                ffn_norm=m_ffn_norm, ffn_gate=m_ffn_gate, ffn_up=m_ffn_up, ffn_down=m_ffn_down)
    mom2 = dict(mix_norm=v_mix_norm, pool_in=v_pool_in, pool_group=v_pool_group, pool_scale=v_pool_scale,
                ssd_in=v_ssd_in, ssd_conv_w=v_ssd_conv_w, ssd_conv_b=v_ssd_conv_b, ssd_dt_bias=v_ssd_dt_bias,
                ssd_a_log=v_ssd_a_log, ssd_d=v_ssd_d, ssd_out_norm=v_ssd_out_norm, ssd_out=v_ssd_out,
                sb_qkv=v_sb_qkv, sb_q_norm=v_sb_q_norm, sb_k_norm=v_sb_k_norm, sb_out=v_sb_out,
                ffn_norm=v_ffn_norm, ffn_gate=v_ffn_gate, ffn_up=v_ffn_up, ffn_down=v_ffn_down)
    names = list(weights)
    depth, d = mix_norm.shape
    xs, ys, cs = _position()
    dev = 4 * xs + 2 * ys + cs
    chip = 2 * xs + ys

    seg = {}
    row = 0
    for name in BIG_WEIGHTS:
        n_rows = weights[name].size // d
        seg[name] = (row, n_rows)
        row += -(-n_rows // SUBLANES) * SUBLANES
    big_rows = row
    n_scale, n_convw = pool_scale.size, ssd_conv_w.size
    exact = jnp.concatenate([_exact_bf16_rows(pool_scale, d), _exact_bf16_rows(ssd_conv_w, d)], axis=0)
    scale_rows = _exact_bf16_rows(pool_scale, d).shape[0]
    packed_rows = -(-(big_rows + exact.shape[0]) // ROW_PAD) * ROW_PAD

    def pack(tree, dtype):
        ends = [seg[n][0] for n in BIG_WEIGHTS[1:]] + [big_rows]
        return jnp.concatenate([_pad_rows(_to_rows(n, tree[n], d).astype(dtype), end - seg[n][0])
                                for n, end in zip(BIG_WEIGHTS, ends)], axis=0)

    w_wire = _pad_rows(jnp.concatenate([pack(weights, WIRE_DTYPE), exact], axis=0), packed_rows)
    gathered = _all_gather_forwarding(w_wire, name="gather_weights")

    def seg_of(name):
        a, n = seg[name]
        return gathered[:, a:a + n]

    n_pool, n_ssd, n_sb = pool_in.shape[0], ssd_in.shape[0], sb_qkv.shape[0]
    assert n_ssd == 1 and n_sb == 1
    w_pool_in = seg_of("pool_in").reshape(N_DEV, n_pool, -1, d).transpose(1, 0, 2, 3).reshape(n_pool, d, d)
    grp = pool_group.shape
    w_pool_group = seg_of("pool_group").reshape(N_DEV, grp[0], grp[1], grp[2], grp[3]).transpose(1, 2, 0, 3, 4)
    w_pool_group = w_pool_group.reshape(grp[0], grp[1], grp[3], grp[3])
    w_ssd_in_t = seg_of("ssd_in").reshape(-1, d)
    w_ssd_out = seg_of("ssd_out").reshape(-1, d)
    w_sb_qkv_t = seg_of("sb_qkv").reshape(-1, d)
    w_sb_out = seg_of("sb_out").reshape(-1, d)
    hidden = ffn_down.shape[1] * N_DEV
    w_gate_t = seg_of("ffn_gate").reshape(N_DEV, depth, -1, d).transpose(1, 0, 2, 3).reshape(depth, hidden, d)
    w_up_t = seg_of("ffn_up").reshape(N_DEV, depth, -1, d).transpose(1, 0, 2, 3).reshape(depth, hidden, d)
    w_down = seg_of("ffn_down").reshape(N_DEV, depth, -1, d).transpose(1, 0, 2, 3).reshape(depth, hidden, d)
    exact_all = gathered[:, big_rows:big_rows + exact.shape[0]]
    scale_full = _exact_f32(exact_all[:, :scale_rows], n_scale).reshape(N_DEV, n_pool, -1)
    scale_full = scale_full.transpose(1, 0, 2).reshape(n_pool, d)
    convw_full = _exact_f32(exact_all[:, scale_rows:], n_convw).reshape(N_DEV, SSD_CONV, -1)
    convw_full = convw_full.transpose(1, 0, 2).reshape(SSD_CONV, -1)

    d_inner = w_ssd_out.shape[0]
    n_zx = w_ssd_in_t.shape[0] - ssd_dt_bias.shape[1]
    w_zx_t = w_ssd_in_t[:n_zx]
    n_ssd_heads = ssd_dt_bias.shape[1]
    n_ssd_groups = n_ssd_heads // SSD_HEADS_PER_GROUP
    w_dt_t = _ssd_group_pad(w_ssd_in_t[n_zx:].T, n_ssd_groups).T
    par = _pad_rows(_ssd_group_pad(jnp.concatenate([ssd_dt_bias, ssd_a_log, ssd_d], axis=0), n_ssd_groups), SUBLANES)
    d_rep = jnp.repeat(ssd_d[0], SSD_HEAD_DIM)[None]
    qk_gains = jnp.zeros((SUBLANES, LANES), F32).at[0].set(jnp.tile(sb_q_norm[0], 2)).at[1].set(jnp.tile(sb_k_norm[0], 2))

    act = x[0]
    saved = []
    for i in range(depth):
        kind, j = i % 3, i // 3
        gain = mix_norm[i:i + 1]
        if kind == 0:
            act, s = _pool_mixer_fwd(act, gain, w_pool_in[j], w_pool_group[j], scale_full[j:j + 1], f"l{i}")
        elif kind == 1:
            act, s = _ssd_mixer_fwd(act, gain, w_zx_t, w_dt_t, convw_full, ssd_conv_b, par, d_rep, ssd_out_norm,
                                    w_ssd_out, f"l{i}")
        else:
            act, s = _sb_mixer_fwd(act, gain, w_sb_qkv_t, qk_gains, w_sb_out, f"l{i}")
        act, f = _ffn_fwd(act, ffn_norm[i:i + 1], w_gate_t[i], w_up_t[i], w_down[i], f"l{i}")
        saved.append((s, f))
    dact, loss_cols = _loss_head(act, loss_target[0], name="loss_head")

    def layer_row(name, layer):
        return seg[name][0] + layer * (seg[name][1] // weights[name].shape[0])

    grads = jnp.zeros((N_DEV, packed_rows, d), F32)
    g_mix_norm, g_ffn_norm = [None] * depth, [None] * depth
    g_pool_group, g_pool_scale = [None] * n_pool, [None] * n_pool
    for i in reversed(range(depth)):
        kind, j = i % 3, i // 3
        gain = mix_norm[i:i + 1]
        s, f = saved[i]
        dact, grads, g_ffn_norm[i] = _ffn_bwd(
            dact, f, ffn_norm[i:i + 1], w_gate_t[i], w_up_t[i], w_down[i], grads,
            [layer_row(n, i) for n in ("ffn_gate", "ffn_up", "ffn_down")], f"l{i}")
        if kind == 0:
            dact, grads, g_pool_group[j], g_pool_scale[j], g_mix_norm[i] = _pool_mixer_bwd(
                dact, s, gain, w_pool_in[j], w_pool_group[j], scale_full[j:j + 1], grads, layer_row("pool_in", j),
                f"l{i}")
        elif kind == 1:
            (dact, grads, g_zx_t, g_dt_t, g_conv_w, g_conv_b, g_par, g_out_norm,
             g_mix_norm[i]) = _ssd_mixer_bwd(dact, s, gain, w_zx_t, w_dt_t, convw_full, ssd_conv_b, par, d_rep,
                                             ssd_out_norm, w_ssd_out, grads, layer_row("ssd_out", j), f"l{i}")
        else:
            dact, grads, g_qk_gains, g_mix_norm[i] = _sb_mixer_bwd(
                dact, s, gain, w_sb_qkv_t, qk_gains, w_sb_out, grads, layer_row("sb_qkv", j),
                layer_row("sb_out", j), f"l{i}")
    grad_x = dact[None]

    g_ssd_in = jnp.concatenate([g_zx_t, _ssd_group_unpad(g_dt_t.T, n_ssd_groups).T], axis=0)
    g_group = jnp.concatenate([_device_blocks(gg[k], d) for gg in g_pool_group for k in range(gg.shape[0])], axis=1)
    for name, blocks in (("ssd_in", _device_blocks(g_ssd_in, d)), ("pool_group", g_group)):
        blocks = blocks.reshape(N_DEV // 2, 2, -1, d).swapaxes(0, 1).reshape(N_DEV, -1, d)
        grads = lax.dynamic_update_slice(grads, blocks, (0, seg[name][0], 0))
    parts = grads.reshape(2, N_DEV // 2, packed_rows, d)
    from_sibling = _exchange_sibling(parts, name="reduce_sibling")
    chip_sums, chip_sums_wire = _add_pairs(parts, from_sibling, cs.reshape(1).astype(jnp.int32),
                                           name="reduce_sibling_add")
    from_chips = _exchange_chips(chip_sums_wire, name="reduce_chips")

    def pack_f32(tree):
        return _pad_rows(pack(tree, F32), packed_rows)

    big_out = _adamw_sharded(chip_sums, from_chips, chip.reshape(1).astype(jnp.int32), pack_f32(weights),
                             pack_f32(mom1), pack_f32(mom2), name="adamw_sharded")

    def small_pack(mix, ffn, conv_b, out_norm, scale, conv_w, vec, qk, loss=None):
        buf = jnp.zeros((SMALL_ROWS, d), F32)
        buf = buf.at[ROW_MIX_NORM:ROW_MIX_NORM + depth].set(mix).at[ROW_FFN_NORM:ROW_FFN_NORM + depth].set(ffn)
        buf = buf.at[ROW_CONV_B:ROW_CONV_B + conv_b.size // d].set(conv_b.reshape(-1, d))
        buf = buf.at[ROW_OUT_NORM:ROW_OUT_NORM + out_norm.size // d].set(out_norm.reshape(-1, d))
        buf = buf.at[ROW_POOL_SCALE:ROW_POOL_SCALE + n_pool].set(scale)
        buf = buf.at[ROW_CONV_W:ROW_CONV_W + conv_w.size // d].set(conv_w.reshape(-1, d))
        buf = buf.at[ROW_SSD_VEC].set(vec.reshape(-1)).at[ROW_QK_GAIN].set(qk.reshape(-1))
        if loss is not None:
            buf = buf.at[ROW_LOSS].set(loss.reshape(-1))
        return buf

    def small_params(tree):
        scale = lax.dynamic_update_slice(jnp.zeros((n_pool, d), F32), tree["pool_scale"],
                                         (0, dev * tree["pool_scale"].shape[1]))
        conv_w = lax.dynamic_update_slice(jnp.zeros(convw_full.shape, F32), tree["ssd_conv_w"][0],
                                          (0, dev * tree["ssd_conv_w"].shape[2]))
        vec = jnp.zeros((SUBLANES, LANES), F32)
        vec = vec.at[0, :n_ssd_heads].set(tree["ssd_dt_bias"][0]).at[1, :n_ssd_heads].set(tree["ssd_a_log"][0])
        vec = vec.at[2, :n_ssd_heads].set(tree["ssd_d"][0])
        qk = jnp.zeros((SUBLANES, LANES), F32)
        qk = qk.at[0, SB_HEAD_DIM:].set(tree["sb_q_norm"][0]).at[1, SB_HEAD_DIM:].set(tree["sb_k_norm"][0])
        return small_pack(tree["mix_norm"], tree["ffn_norm"], tree["ssd_conv_b"], tree["ssd_out_norm"], scale,
                          conv_w, vec, qk)

    small_partial = small_pack(jnp.concatenate(g_mix_norm, axis=0), jnp.concatenate(g_ffn_norm, axis=0), g_conv_b,
                               g_out_norm, jnp.concatenate(g_pool_scale, axis=0), g_conv_w,
                               jnp.zeros((SUBLANES, LANES), F32).at[:3, :n_ssd_heads].set(
                                   _ssd_group_unpad(g_par[:3], n_ssd_groups)), g_qk_gains,
                               loss_cols)
    small_all = _all_gather(small_partial, name="gather_small", in_vmem=True)
    small_out = _adamw_small(small_all, small_params(weights), small_params(mom1), small_params(mom2),
                             name="adamw_small")
    loss = small_out[0][ROW_LOSS, 0]

    def unpack(big, small):
        out = {}
        for name in BIG_WEIGHTS:
            a, n = seg[name]
            out[name] = _from_rows(name, big[a:a + n], weights[name].shape)
        out["mix_norm"] = small[ROW_MIX_NORM:ROW_MIX_NORM + depth]
        out["ffn_norm"] = small[ROW_FFN_NORM:ROW_FFN_NORM + depth]
        out["ssd_conv_b"] = small[ROW_CONV_B:ROW_CONV_B + ssd_conv_b.size // d].reshape(ssd_conv_b.shape)
        out["ssd_out_norm"] = small[ROW_OUT_NORM:ROW_OUT_NORM + ssd_out_norm.size // d].reshape(ssd_out_norm.shape)
        out["pool_scale"] = lax.dynamic_slice(small[ROW_POOL_SCALE:ROW_POOL_SCALE + n_pool],
                                              (0, dev * pool_scale.shape[1]), pool_scale.shape)
        conv_w = small[ROW_CONV_W:ROW_CONV_W + convw_full.size // d].reshape(convw_full.shape)
        out["ssd_conv_w"] = lax.dynamic_slice(conv_w, (0, dev * ssd_conv_w.shape[2]), ssd_conv_w.shape[1:])[None]
        vec = small[ROW_SSD_VEC].reshape(SUBLANES, LANES)
        out["ssd_dt_bias"], out["ssd_a_log"], out["ssd_d"] = (vec[r:r + 1, :n_ssd_heads] for r in range(3))
        qk = small[ROW_QK_GAIN].reshape(SUBLANES, LANES)
        out["sb_q_norm"], out["sb_k_norm"] = qk[0:1, SB_HEAD_DIM:], qk[1:2, SB_HEAD_DIM:]
        return [out[n] for n in names]

    results = [unpack(b, s) for b, s in zip(big_out, small_out)]
    return (loss, grad_x, *results[0], *results[1], *results[2], *results[3])
```

```python
import math

import jax
import jax.numpy as jnp
from jax import lax
from jax.experimental import pallas as pl
from jax.experimental.pallas import tpu as pltpu

F32 = jnp.float32
BF16 = jnp.bfloat16

N_DEV = 8
NORM_EPS = 1e-6
V7X_VMEM_LIMIT_BYTES = 48 * 1024 * 1024
LANES = 128
SUBLANES = 8

POOL_WINDOWS = (2, 4, 8, 16)
SSD_CHUNK = 256
SSD_HEAD_DIM = 64
SSD_STATE = 128
SSD_HEADS_PER_GROUP = 4
SSD_CONV = 4
SB_HEAD_DIM = 64
SB_BLOCK = 256
SB_QUERY_BLOCK = 256

ADAM_LR = 0.001
ADAM_B1 = 0.9
ADAM_B2 = 0.999
ADAM_EPS = 1e-08
ADAM_WD = 0.01
ADAM_STEP = 10


def _params(*sem):
    return pltpu.CompilerParams(dimension_semantics=sem, vmem_limit_bytes=V7X_VMEM_LIMIT_BYTES)


def _tile(n, cap, mult):
    best = None
    for t in range(mult, min(n, cap) + 1, mult):
        if n % t == 0:
            best = t
    return best or n


def _load_slabs(ref, slabs):
    if not slabs:
        return ref[...]
    return jnp.concatenate([ref[p] for p in range(ref.shape[0])], axis=1)


def _matmul(a, b, mode, *, name, out_dtype=F32, resid=None, a_slabs=False, out_slabs=False,
            tm_cap=1024, tn_cap=1024, tk_cap=2048):
    pairs = list(zip(a, b)) if isinstance(a, (list, tuple)) else [(a, b)]
    a, b = pairs[0]
    if a_slabs:
        m, k = a.shape[1], a.shape[0] * LANES
    else:
        m, k = a.shape
    n = b.shape[1] if mode == "nn" else b.shape[0]
    assert (b.shape[0] if mode == "nn" else b.shape[1]) == k
    assert all(pa.shape == a.shape and pb.shape == b.shape for pa, pb in pairs)
    tm, tn, tk = _tile(m, tm_cap, SUBLANES), _tile(n, tn_cap, LANES), _tile(k, tk_cap, LANES)
    nk = k // tk
    dn = (((1,), (0,)), ((), ())) if mode == "nn" else (((1,), (1,)), ((), ()))
    has_resid = resid is not None
    n_pairs = len(pairs)

    def body(*refs):
        ab_refs, rest = refs[:2 * n_pairs], refs[2 * n_pairs:]
        r_ref = rest[0] if has_resid else None
        o_ref = rest[1] if has_resid else rest[0]
        kk = pl.program_id(2)

        def partial():
            total = None
            for p in range(n_pairs):
                d = lax.dot_general(_load_slabs(ab_refs[2 * p], a_slabs).astype(BF16),
                                    ab_refs[2 * p + 1][...].astype(BF16), dn, preferred_element_type=F32)
                total = d if total is None else total + d
            return total

        def finish(r):
            if has_resid:
                r = r + r_ref[...]
            if out_slabs:
                for p in range(tn // LANES):
                    o_ref[p] = r[:, p * LANES:(p + 1) * LANES].astype(out_dtype)
            else:
                o_ref[...] = r.astype(out_dtype)

        if nk == 1:
            finish(partial())
        else:
            acc = rest[-1]

            @pl.when(kk == 0)
            def _():
                acc[...] = jnp.zeros_like(acc)

            acc[...] += partial()

            @pl.when(kk == nk - 1)
            def _():
                finish(acc[...])

    b_spec = (pl.BlockSpec((tk, tn), lambda i, j, kk: (kk, j)) if mode == "nn"
              else pl.BlockSpec((tn, tk), lambda i, j, kk: (j, kk)))
    a_spec = (pl.BlockSpec((tk // LANES, tm, LANES), lambda i, j, kk: (kk, i, 0)) if a_slabs
              else pl.BlockSpec((tm, tk), lambda i, j, kk: (i, kk)))
    in_specs = [a_spec, b_spec] * n_pairs
    args = [t for pair in pairs for t in pair]
    if has_resid:
        in_specs.append(pl.BlockSpec((tm, tn), lambda i, j, kk: (i, j)))
        args.append(resid)
    if out_slabs:
        out_spec = pl.BlockSpec((tn // LANES, tm, LANES), lambda i, j, kk: (j, i, 0))
        out_shape = jax.ShapeDtypeStruct((n // LANES, m, LANES), out_dtype)
    else:
        out_spec = pl.BlockSpec((tm, tn), lambda i, j, kk: (i, j))
        out_shape = jax.ShapeDtypeStruct((m, n), out_dtype)
    return pl.pallas_call(
        body, name=name, grid=(m // tm, n // tn, nk),
        in_specs=in_specs, out_specs=out_spec, out_shape=out_shape,
        scratch_shapes=[pltpu.VMEM((tm, tn), F32)] if nk > 1 else [],
        compiler_params=_params("parallel", "parallel", "arbitrary"),
    )(*args)


def _matmul_tn(a, b, *, name, a_slabs=False, ta_cap=1024, tb_cap=1024, tr_cap=512):
    if a_slabs:
        r, ka = a.shape[1], a.shape[0] * LANES
    else:
        r, ka = a.shape
    nb = b.shape[1]
    assert b.shape[0] == r
    ta, tb, tr = _tile(ka, ta_cap, LANES), _tile(nb, tb_cap, LANES), _tile(r, tr_cap, SUBLANES)

    def body(a_ref, b_ref, o_ref):
        @pl.when(pl.program_id(2) == 0)
        def _():
            o_ref[...] = jnp.zeros_like(o_ref)

        o_ref[...] += lax.dot_general(_load_slabs(a_ref, a_slabs).astype(BF16), b_ref[...].astype(BF16),
                                      (((0,), (0,)), ((), ())), preferred_element_type=F32)

    a_spec = (pl.BlockSpec((ta // LANES, tr, LANES), lambda i, j, kk: (i, kk, 0)) if a_slabs
              else pl.BlockSpec((tr, ta), lambda i, j, kk: (kk, i)))
    return pl.pallas_call(
        body, name=name, grid=(ka // ta, nb // tb, r // tr),
        in_specs=[a_spec, pl.BlockSpec((tr, tb), lambda i, j, kk: (kk, j))],
        out_specs=pl.BlockSpec((ta, tb), lambda i, j, kk: (i, j)),
        out_shape=jax.ShapeDtypeStruct((ka, nb), F32),
        compiler_params=_params("parallel", "parallel", "arbitrary"),
    )(a, b)


def _core_major(k):
    return (k % 2) * (N_DEV // 2) + k // 2


def _matmul_tn_into(buf, a, b, row_off, *, name, a_slabs=False, tr_cap=1024):
    if a_slabs:
        r, ka = a.shape[1], a.shape[0] * LANES
    else:
        r, ka = a.shape
    n_dev, _, n = buf.shape
    per = ka // n_dev
    assert b.shape == (r, n) and ka % n_dev == 0 and per % SUBLANES == 0 and row_off % per == 0
    tr = _tile(r, tr_cap, SUBLANES)

    def body(buf_ref, a_ref, b_ref, o_ref):
        prod = lax.dot_general(_load_slabs(a_ref, a_slabs).astype(BF16), b_ref[...].astype(BF16),
                               (((0,), (0,)), ((), ())), preferred_element_type=F32)
        @pl.when(pl.program_id(0) == 0)
        def _():
            for k in range(n_dev):
                o_ref[_core_major(k)] = prod[k * per:(k + 1) * per]

        @pl.when(pl.program_id(0) > 0)
        def _():
            for k in range(n_dev):
                o_ref[_core_major(k)] += prod[k * per:(k + 1) * per]

    a_spec = (pl.BlockSpec((ka // LANES, tr, LANES), lambda i: (0, i, 0)) if a_slabs
              else pl.BlockSpec((tr, ka), lambda i: (i, 0)))
    return pl.pallas_call(
        body, name=name, grid=(r // tr,),
        in_specs=[pl.BlockSpec(memory_space=pl.ANY), a_spec, pl.BlockSpec((tr, n), lambda i: (i, 0))],
        out_specs=pl.BlockSpec((n_dev, per, n), lambda i: (0, row_off // per, 0)),
        out_shape=jax.ShapeDtypeStruct(buf.shape, F32),
        input_output_aliases={0: 0},
        compiler_params=_params("arbitrary"),
    )(buf, a, b)


def _rms_fwd(x, gain, *, name):
    t, d = x.shape
    tm = _tile(t, 512, SUBLANES)

    def body(x_ref, g_ref, o_ref):
        xv = x_ref[...]
        r = lax.rsqrt(jnp.mean(xv * xv, axis=-1, keepdims=True) + NORM_EPS)
        o_ref[...] = (xv * r * g_ref[...]).astype(BF16)

    return pl.pallas_call(
        body, name=name, grid=(t // tm,),
        in_specs=[pl.BlockSpec((tm, d), lambda i: (i, 0)), pl.BlockSpec((1, d), lambda i: (0, 0))],
        out_specs=pl.BlockSpec((tm, d), lambda i: (i, 0)),
        out_shape=jax.ShapeDtypeStruct((t, d), BF16),
        compiler_params=_params("parallel"),
    )(x, gain)


def _rms_bwd(x, gain, dh, dres, *, name):
    t, d = x.shape
    tm = _tile(t, 512, SUBLANES)

    def body(x_ref, g_ref, dh_ref, dres_ref, dx_ref, dg_ref):
        @pl.when(pl.program_id(0) == 0)
        def _():
            dg_ref[...] = jnp.zeros_like(dg_ref)

        xv = x_ref[...]
        r = lax.rsqrt(jnp.mean(xv * xv, axis=-1, keepdims=True) + NORM_EPS)
        xhat = xv * r
        dhv = dh_ref[...]
        u = dhv * g_ref[...]
        dx_ref[...] = dres_ref[...] + r * (u - xhat * jnp.mean(u * xhat, axis=-1, keepdims=True))
        dg_ref[...] += jnp.sum(dhv * xhat, axis=0, keepdims=True)

    return pl.pallas_call(
        body, name=name, grid=(t // tm,),
        in_specs=[pl.BlockSpec((tm, d), lambda i: (i, 0)), pl.BlockSpec((1, d), lambda i: (0, 0)),
                  pl.BlockSpec((tm, d), lambda i: (i, 0)), pl.BlockSpec((tm, d), lambda i: (i, 0))],
        out_specs=[pl.BlockSpec((tm, d), lambda i: (i, 0)), pl.BlockSpec((1, d), lambda i: (0, 0))],
        out_shape=[jax.ShapeDtypeStruct((t, d), F32), jax.ShapeDtypeStruct((1, d), F32)],
        compiler_params=_params("arbitrary"),
    )(x, gain, dh, dres)


def _loss_head(y, target, *, name):
    t, d = y.shape
    tm = _tile(t, 512, SUBLANES)

    def body(y_ref, t_ref, dy_ref, l_ref):
        @pl.when(pl.program_id(0) == 0)
        def _():
            l_ref[...] = jnp.zeros_like(l_ref)

        e = y_ref[...] - t_ref[...]
        dy_ref[...] = e * (1.0 / d)
        l_ref[...] += jnp.sum(e * e, axis=0, keepdims=True) * (0.5 / d)

    return pl.pallas_call(
        body, name=name, grid=(t // tm,),
        in_specs=[pl.BlockSpec((tm, d), lambda i: (i, 0)), pl.BlockSpec((tm, d), lambda i: (i, 0))],
        out_specs=[pl.BlockSpec((tm, d), lambda i: (i, 0)), pl.BlockSpec((1, d), lambda i: (0, 0))],
        out_shape=[jax.ShapeDtypeStruct((t, d), F32), jax.ShapeDtypeStruct((1, d), F32)],
        compiler_params=_params("arbitrary"),
    )(y, target)


def _sigmoid(v):
    return 0.5 * jnp.tanh(0.5 * v) + 0.5


FFN_TOKEN_TILE = 512
FFN_HIDDEN_TILE = 1408
NT_DIMS = (((1,), (1,)), ((), ()))


def _ffn_up(h, w_gate_t, w_up_t, *, name):
    t, d = h.shape
    f = w_gate_t.shape[0]
    tm, tn = _tile(t, FFN_TOKEN_TILE, SUBLANES), _tile(f, FFN_HIDDEN_TILE, LANES)

    def body(h_ref, g_ref, u_ref, s_ref, a_ref, b_ref):
        hv = h_ref[...].astype(BF16)
        av = lax.dot_general(hv, g_ref[...].astype(BF16), NT_DIMS, preferred_element_type=F32)
        bv = lax.dot_general(hv, u_ref[...].astype(BF16), NT_DIMS, preferred_element_type=F32)
        s_ref[...] = (av * _sigmoid(av) * bv).astype(BF16)
        a_ref[...] = av.astype(BF16)
        b_ref[...] = bv.astype(BF16)

    w_spec = pl.BlockSpec((tn, d), lambda j, i: (j, 0))
    out_spec = pl.BlockSpec((tm, tn), lambda j, i: (i, j))
    out = jax.ShapeDtypeStruct((t, f), BF16)
    return pl.pallas_call(
        body, name=name, grid=(f // tn, t // tm),
        in_specs=[pl.BlockSpec((tm, d), lambda j, i: (i, 0)), w_spec, w_spec],
        out_specs=[out_spec, out_spec, out_spec], out_shape=[out, out, out],
        compiler_params=_params("parallel", "parallel"),
    )(h, w_gate_t, w_up_t)


def _ffn_dact(dx, w_down, a, b, *, name):
    t, d = dx.shape
    f = w_down.shape[0]
    tm, tn = _tile(t, FFN_TOKEN_TILE, SUBLANES), _tile(f, FFN_HIDDEN_TILE, LANES)

    def body(dx_ref, w_ref, a_ref, b_ref, da_ref, db_ref):
        ds = lax.dot_general(dx_ref[...].astype(BF16), w_ref[...].astype(BF16), NT_DIMS, preferred_element_type=F32)
        av = a_ref[...].astype(F32)
        sg = _sigmoid(av)
        da_ref[...] = (ds * b_ref[...].astype(F32) * (sg * (1.0 + av * (1.0 - sg)))).astype(BF16)
        db_ref[...] = (ds * av * sg).astype(BF16)

    blk = pl.BlockSpec((tm, tn), lambda j, i: (i, j))
    out = jax.ShapeDtypeStruct((t, f), BF16)
    return pl.pallas_call(
        body, name=name, grid=(f // tn, t // tm),
        in_specs=[pl.BlockSpec((tm, d), lambda j, i: (i, 0)), pl.BlockSpec((tn, d), lambda j, i: (j, 0)), blk, blk],
        out_specs=[blk, blk], out_shape=[out, out],
        compiler_params=_params("parallel", "parallel"),
    )(dx, w_down, a, b)


def _ffn_fwd(x, gain, w_gate_t, w_up_t, w_down, tag):
    h = _rms_fwd(x, gain, name=f"ffn_norm_{tag}")
    s, a, b = _ffn_up(h, w_gate_t, w_up_t, name=f"ffn_up_{tag}")
    x_new = _matmul(s, w_down, "nn", resid=x, tn_cap=1024, tk_cap=2816, name=f"ffn_down_{tag}")
    return x_new, (x, h, a, b, s)


def _ffn_bwd(dx, saved, gain, w_gate_t, w_up_t, w_down, grads, rows, tag):
    x, h, a, b, s = saved
    da, db = _ffn_dact(dx, w_down, a, b, name=f"ffn_dact_{tag}")
    grads = _matmul_tn_into(grads, da, h, rows[0], name=f"ffn_dwgate_{tag}")
    grads = _matmul_tn_into(grads, db, h, rows[1], name=f"ffn_dwup_{tag}")
    grads = _matmul_tn_into(grads, s, dx, rows[2], name=f"ffn_dwdown_{tag}")
    dh = _matmul([da, db], [w_gate_t, w_up_t], "nn", tm_cap=512, tn_cap=1024, tk_cap=2816, name=f"ffn_dh_{tag}")
    dx_in, dgain = _rms_bwd(x, gain, dh, dx, name=f"ffn_dnorm_{tag}")
    return dx_in, grads, dgain


POOL_HALO = 16


def _shift_rows(v, k):
    n = v.shape[0]
    return pltpu.roll(v, k % n, 0)


def _window_sum(v, w, direction):
    k = 1
    while k < w:
        v = v + _shift_rows(v, direction * k)
        k *= 2
    return v


def _pool_fwd(u, x, w_group, scale, *, name):
    t, d = u.shape
    ng, dg = w_group.shape[0], w_group.shape[1]
    tm = _tile(t, 512, POOL_HALO)
    hb = tm // POOL_HALO

    def body(u_ref, halo_ref, x_ref, w_ref, s_ref, xo_ref, p_ref, y_ref):
        i, g = pl.program_id(0), pl.program_id(1)
        halo = jnp.where(i > 0, halo_ref[...], 0.0)
        ext = jnp.concatenate([halo, u_ref[...]], axis=0)
        pos = i * tm + lax.broadcasted_iota(jnp.int32, (tm, 1), 0)
        for gi, win in enumerate(POOL_WINDOWS):
            @pl.when(g == gi)
            def _(win=win):
                tot = _window_sum(ext, win, 1)[POOL_HALO:]
                cnt = jnp.minimum(pos + 1, win).astype(F32)
                p = (tot / cnt - u_ref[...]).astype(BF16)
                p_ref[...] = p
                y = jnp.dot(p, w_ref[...].astype(BF16), preferred_element_type=F32)
                y_ref[...] = y
                xo_ref[...] = x_ref[...] + y * s_ref[...]

    blk = pl.BlockSpec((tm, dg), lambda i, g: (i, g))
    return pl.pallas_call(
        body, name=name, grid=(t // tm, ng),
        in_specs=[blk, pl.BlockSpec((POOL_HALO, dg), lambda i, g: (jnp.maximum(i * hb - 1, 0), g)), blk,
                  pl.BlockSpec((None, dg, dg), lambda i, g: (g, 0, 0)), pl.BlockSpec((1, dg), lambda i, g: (0, g))],
        out_specs=[blk, blk, blk],
        out_shape=[jax.ShapeDtypeStruct((t, d), F32), jax.ShapeDtypeStruct((t, d), BF16),
                   jax.ShapeDtypeStruct((t, d), F32)],
        compiler_params=_params("parallel", "parallel"),
    )(u, u, x, w_group, scale)


def _pool_bwd(dx, p, y_pre, w_group, scale, *, name):
    t, d = dx.shape
    ng, dg = w_group.shape[0], w_group.shape[1]
    tm = _tile(t, 512, POOL_HALO)
    hb = tm // POOL_HALO
    nt = t // tm

    def body(dx_ref, nxt_ref, p_ref, y_ref, w_ref, s_ref, du_ref, dw_ref, ds_ref):
        g, i = pl.program_id(0), pl.program_id(1)

        @pl.when(i == 0)
        def _():
            dw_ref[...] = jnp.zeros_like(dw_ref)
            ds_ref[...] = jnp.zeros_like(ds_ref)

        dxv = dx_ref[...]
        ds_ref[...] += jnp.sum(dxv * y_ref[...], axis=0, keepdims=True)
        nxt = jnp.where(i < nt - 1, nxt_ref[...], 0.0)
        dyp = (jnp.concatenate([dxv, nxt], axis=0) * s_ref[...]).astype(BF16)
        dw_ref[...] += lax.dot_general(p_ref[...], dyp[:tm], (((0,), (0,)), ((), ())), preferred_element_type=F32)
        dp = lax.dot_general(dyp, w_ref[...].astype(BF16), (((1,), (1,)), ((), ())), preferred_element_type=F32)
        pos = i * tm + lax.broadcasted_iota(jnp.int32, (tm + POOL_HALO, 1), 0)
        for gi, win in enumerate(POOL_WINDOWS):
            @pl.when(g == gi)
            def _(win=win):
                q = dp / jnp.minimum(pos + 1, win).astype(F32)
                du_ref[...] = (_window_sum(q, win, -1)[:tm] - dp[:tm]).astype(BF16)

    blk = pl.BlockSpec((tm, dg), lambda g, i: (i, g))
    return pl.pallas_call(
        body, name=name, grid=(ng, nt),
        in_specs=[blk, pl.BlockSpec((POOL_HALO, dg), lambda g, i: (jnp.minimum((i + 1) * hb, t // POOL_HALO - 1), g)),
                  blk, blk, pl.BlockSpec((None, dg, dg), lambda g, i: (g, 0, 0)),
                  pl.BlockSpec((1, dg), lambda g, i: (0, g))],
        out_specs=[blk, pl.BlockSpec((None, dg, dg), lambda g, i: (g, 0, 0)), pl.BlockSpec((1, dg), lambda g, i: (0, g))],
        out_shape=[jax.ShapeDtypeStruct((t, d), BF16), jax.ShapeDtypeStruct((ng, dg, dg), F32),
                   jax.ShapeDtypeStruct((1, d), F32)],
        compiler_params=_params("parallel", "arbitrary"),
    )(dx, dx, p, y_pre, w_group, scale)


def _pool_mixer_fwd(x, gain, w_in, w_group, scale, tag):
    h = _rms_fwd(x, gain, name=f"pool_norm_{tag}")
    u = _matmul(h, w_in, "nn", name=f"pool_in_{tag}")
    x_new, p, y_pre = _pool_fwd(u, x, w_group, scale, name=f"pool_mix_{tag}")
    return x_new, (x, h, p, y_pre)


def _pool_mixer_bwd(dx, saved, gain, w_in, w_group, scale, grads, row_in, tag):
    x, h, p, y_pre = saved
    du, dw_group, dscale = _pool_bwd(dx, p, y_pre, w_group, scale, name=f"pool_dmix_{tag}")
    grads = _matmul_tn_into(grads, h, du, row_in, name=f"pool_dwin_{tag}")
    dh = _matmul(du, w_in, "nt", name=f"pool_dh_{tag}")
    dx_in, dgain = _rms_bwd(x, gain, dh, dx, name=f"pool_dnorm_{tag}")
    return dx_in, grads, dw_group, dscale, dgain


CONV_HALO = 8
HIGHEST = lax.Precision.HIGHEST
NEG_BIG = -1e30


def _softplus(v):
    return jnp.maximum(v, 0.0) + jnp.log(1.0 + jnp.exp(-jnp.abs(v)))


def _dot_exact(a, b):
    return jnp.dot(a, b, precision=HIGHEST, preferred_element_type=F32)


def _conv_taps(ext, w_ref, off, rows):
    acc = None
    for k in range(SSD_CONV):
        shift = SSD_CONV - 1 - k
        v = (_shift_rows(ext, shift) if shift else ext)[off:off + rows] * w_ref[k:k + 1, :]
        acc = v if acc is None else acc + v
    return acc


def _ssd_conv_fwd(zx, conv_w, conv_b, col0, *, name):
    t = zx.shape[0]
    c = conv_w.shape[1]
    tm, tc = _tile(t, 512, CONV_HALO), _tile(c, 512, LANES)
    hb, cb0 = tm // CONV_HALO, col0 // tc
    assert col0 % tc == 0

    def body(x_ref, halo_ref, w_ref, b_ref, o_ref):
        halo = jnp.where(pl.program_id(0) > 0, halo_ref[...], 0.0)
        ext = jnp.concatenate([halo, x_ref[...]], axis=0)
        pre = _conv_taps(ext, w_ref, CONV_HALO, tm) + b_ref[...]
        o_ref[...] = pre * _sigmoid(pre)

    return pl.pallas_call(
        body, name=name, grid=(t // tm, c // tc),
        in_specs=[pl.BlockSpec((tm, tc), lambda i, j: (i, j + cb0)),
                  pl.BlockSpec((CONV_HALO, tc), lambda i, j: (jnp.maximum(i * hb - 1, 0), j + cb0)),
                  pl.BlockSpec((SSD_CONV, tc), lambda i, j: (0, j)), pl.BlockSpec((1, tc), lambda i, j: (0, j))],
        out_specs=pl.BlockSpec((tm, tc), lambda i, j: (i, j)),
        out_shape=jax.ShapeDtypeStruct((t, c), F32),
        compiler_params=_params("parallel", "parallel"),
    )(zx, zx, conv_w, conv_b)


def _ssd_conv_bwd(d_parts, zx, conv_w, conv_b, col0, *, name):
    t = zx.shape[0]
    c = conv_w.shape[1]
    tm, tc = _tile(t, 512, CONV_HALO), _tile(c, 512, LANES)
    hb, cb0, nt = tm // CONV_HALO, col0 // tc, t // tm
    last_halo = t // CONV_HALO - 1
    starts = [0]
    for part in d_parts:
        assert part.shape[1] % tc == 0
        starts.append(starts[-1] + part.shape[1] // tc)
    assert starts[-1] == c // tc
    n_parts = len(d_parts)

    def pick(refs, j):
        value = refs[-1][...]
        for p in reversed(range(n_parts - 1)):
            value = jnp.where(j < starts[p + 1], refs[p][...], value)
        return value

    def body(x_ref, prev_ref, nxt_ref, *rest):
        d_refs, dnxt_refs = rest[:n_parts], rest[n_parts:2 * n_parts]
        w_ref, b_ref, dx_ref, dw_ref, db_ref = rest[2 * n_parts:]
        j, i = pl.program_id(0), pl.program_id(1)

        @pl.when(i == 0)
        def _():
            dw_ref[...] = jnp.zeros_like(dw_ref)
            db_ref[...] = jnp.zeros_like(db_ref)

        prev = jnp.where(i > 0, prev_ref[...], 0.0)
        has_next = i < nt - 1
        ext = jnp.concatenate([prev, x_ref[...], jnp.where(has_next, nxt_ref[...], 0.0)], axis=0)
        pre = _conv_taps(ext, w_ref, CONV_HALO, tm + CONV_HALO) + b_ref[...]
        sg = _sigmoid(pre)
        dact = jnp.concatenate([pick(d_refs, j), jnp.where(has_next, pick(dnxt_refs, j), 0.0)], axis=0)
        dpre = dact * (sg * (1.0 + pre * (1.0 - sg)))
        db_ref[...] += jnp.sum(dpre[:tm], axis=0, keepdims=True)
        acc = None
        for k in range(SSD_CONV):
            shift = SSD_CONV - 1 - k
            src = (_shift_rows(ext, shift) if shift else ext)[CONV_HALO:CONV_HALO + tm]
            dw_ref[k:k + 1, :] += jnp.sum(dpre[:tm] * src, axis=0, keepdims=True)
            v = (_shift_rows(dpre, -shift) if shift else dpre)[:tm] * w_ref[k:k + 1, :]
            acc = v if acc is None else acc + v
        dx_ref[...] = acc.astype(BF16)

    def part_specs(rows, row_index):
        def spec(p):
            def index(j, i):
                mine = (j >= starts[p]) & (j < starts[p + 1])
                return jnp.where(mine, row_index(i), 0), jnp.where(mine, j - starts[p], 0)
            return pl.BlockSpec((rows, tc), index)
        return [spec(p) for p in range(n_parts)]

    main = lambda j, i: (i, j + cb0)
    next_halo = lambda i: jnp.minimum((i + 1) * hb, last_halo)
    return pl.pallas_call(
        body, name=name, grid=(c // tc, nt),
        in_specs=[pl.BlockSpec((tm, tc), main),
                  pl.BlockSpec((CONV_HALO, tc), lambda j, i: (jnp.maximum(i * hb - 1, 0), j + cb0)),
                  pl.BlockSpec((CONV_HALO, tc), lambda j, i: (next_halo(i), j + cb0)),
                  *part_specs(tm, lambda i: i), *part_specs(CONV_HALO, next_halo),
                  pl.BlockSpec((SSD_CONV, tc), lambda j, i: (0, j)), pl.BlockSpec((1, tc), lambda j, i: (0, j))],
        out_specs=[pl.BlockSpec((tm, tc), lambda j, i: (i, j)), pl.BlockSpec((SSD_CONV, tc), lambda j, i: (0, j)),
                   pl.BlockSpec((1, tc), lambda j, i: (0, j))],
        out_shape=[jax.ShapeDtypeStruct((t, c), BF16), jax.ShapeDtypeStruct((SSD_CONV, c), F32),
                   jax.ShapeDtypeStruct((1, c), F32)],
        compiler_params=_params("parallel", "arbitrary"),
    )(zx, zx, zx, *d_parts, *d_parts, conv_w, conv_b)


SSD_CUMSUM_PIECES = 3
SSD_GROUPS_PER_STEP = 1


def _ssd_group_pad(v, n_groups):
    lead = v.shape[:-1]
    v = v.reshape(*lead, n_groups, SSD_HEADS_PER_GROUP)
    v = jnp.pad(v, [(0, 0)] * (len(lead) + 1) + [(0, LANES - SSD_HEADS_PER_GROUP)])
    return v.reshape(*lead, n_groups * LANES)


def _ssd_group_unpad(v, n_groups):
    lead = v.shape[:-1]
    return v.reshape(*lead, n_groups, LANES)[..., :SSD_HEADS_PER_GROUP].reshape(*lead, -1)


def _ssd_chunk_common(dtp_ref, par_ref):
    ell = SSD_CHUNK
    dt = _softplus(dtp_ref[...] + par_ref[0:1, :])
    a = -jnp.exp(par_ref[1:2, :])
    row = lax.broadcasted_iota(jnp.int32, (ell, ell), 0)
    col = lax.broadcasted_iota(jnp.int32, (ell, ell), 1)
    acum = _split_dot(dt * a, (row >= col).astype(BF16), SSD_CUMSUM_PIECES, left=True)
    return dt, a, acum, acum.T, row, col


def _ssd_scan_fwd(xa, dtp, par, n_groups, *, name):
    t = xa.shape[0]
    ell, hd, hpg, ns, gps = SSD_CHUNK, SSD_HEAD_DIM, SSD_HEADS_PER_GROUP, SSD_STATE, SSD_GROUPS_PER_STEP
    gw = hpg * hd
    nc = t // ell
    b_blk0, c_blk0 = n_groups * gw // (ns * gps), (n_groups * gw // ns + n_groups) // gps

    def body(xs_ref, b_ref, c_ref, dtp_ref, par_ref, y_ref, sin_ref, st):
        @pl.when(pl.program_id(1) == 0)
        def _():
            st[...] = jnp.zeros_like(st)

        dt, _, acum, acum_t, row, col = _ssd_chunk_common(dtp_ref, par_ref)
        for gi in range(gps):
            bb = b_ref[:, gi * ns:(gi + 1) * ns].astype(BF16)
            cc = c_ref[:, gi * ns:(gi + 1) * ns].astype(BF16)
            cb = lax.dot_general(cc, bb, NT_DIMS, preferred_element_type=F32)
            s_all = st[gi]
            sin_ref[gi] = s_all
            c_s = lax.dot_general(cc, s_all.astype(BF16), NT_DIMS, preferred_element_type=F32)
            weighted, keep = [], []
            for hh in range(hpg):
                lanes = slice(gi * gw + hh * hd, gi * gw + (hh + 1) * hd)
                hl = gi * LANES + hh
                col_a, row_a = acum[:, hl:hl + 1], acum_t[hl:hl + 1, :]
                decay = jnp.exp(jnp.where(row >= col, col_a - row_a, NEG_BIG))
                xdt = xs_ref[:, lanes] * dt[:, hl:hl + 1]
                y = jnp.dot((cb * decay).astype(BF16), xdt.astype(BF16), preferred_element_type=F32)
                y_ref[:, lanes] = y + jnp.exp(col_a) * c_s[:, hh * hd:(hh + 1) * hd]
                a_last = acum[ell - 1:ell, hl:hl + 1]
                weighted.append((xdt * jnp.exp(a_last - col_a)).astype(BF16))
                keep.append(jnp.broadcast_to(jnp.exp(a_last), (hd, 1)))
            st[gi] = jnp.concatenate(keep, axis=0) * s_all + lax.dot_general(
                jnp.concatenate(weighted, axis=1), bb, (((0,), (0,)), ((), ())), preferred_element_type=F32)

    return pl.pallas_call(
        body, name=name, grid=(n_groups // gps, nc),
        in_specs=[pl.BlockSpec((ell, gps * gw), lambda g, c: (c, g)),
                  pl.BlockSpec((ell, gps * ns), lambda g, c: (c, b_blk0 + g)),
                  pl.BlockSpec((ell, gps * ns), lambda g, c: (c, c_blk0 + g)),
                  pl.BlockSpec((ell, gps * LANES), lambda g, c: (c, g)),
                  pl.BlockSpec((SUBLANES, gps * LANES), lambda g, c: (0, g))],
        out_specs=[pl.BlockSpec((ell, gps * gw), lambda g, c: (c, g)),
                   pl.BlockSpec((None, gps, gw, ns), lambda g, c: (c, g, 0, 0))],
        out_shape=[jax.ShapeDtypeStruct((t, n_groups * gw), F32),
                   jax.ShapeDtypeStruct((nc, n_groups, gw, ns), F32)],
        scratch_shapes=[pltpu.VMEM((gps, gw, ns), F32)],
        compiler_params=_params("parallel", "arbitrary"),
    )(xa, xa, xa, dtp, par)


def _ssd_scan_bwd(dy, xa, dtp, par, s_in, n_groups, *, name):
    t = xa.shape[0]
    ell, hd, hpg, ns, gps = SSD_CHUNK, SSD_HEAD_DIM, SSD_HEADS_PER_GROUP, SSD_STATE, SSD_GROUPS_PER_STEP
    gw = hpg * hd
    nc = t // ell
    b_blk0, c_blk0 = n_groups * gw // (ns * gps), (n_groups * gw // ns + n_groups) // gps
    nt_dims = (((1,), (1,)), ((), ()))
    tn_dims = (((0,), (0,)), ((), ()))

    def body(dy_ref, xs_ref, b_ref, c_ref, dtp_ref, par_ref, sin_ref,
             dxs_ref, db_ref, dc_ref, ddtp_ref, dpar_ref, dst):
        @pl.when(pl.program_id(1) == 0)
        def _():
            dst[...] = jnp.zeros_like(dst)
            dpar_ref[...] = jnp.zeros_like(dpar_ref)

        dtg, a_g, acum, acum_t, row, col = _ssd_chunk_common(dtp_ref, par_ref)
        lane = lax.broadcasted_iota(jnp.int32, (1, gps * LANES), 1)
        dacum = jnp.zeros((ell, gps * LANES), F32)
        xsum = jnp.zeros((ell, gps * LANES), F32)
        dsum = jnp.zeros((1, gps * LANES), F32)
        for gi, hh in [(gi, hh) for gi in range(gps) for hh in range(hpg)]:
            if hh == 0:
                bb = b_ref[:, gi * ns:(gi + 1) * ns].astype(BF16)
                cc = c_ref[:, gi * ns:(gi + 1) * ns].astype(BF16)
                cb = lax.dot_general(cc, bb, nt_dims, preferred_element_type=F32)
                cb_t = lax.dot_general(bb, cc, nt_dims, preferred_element_type=F32)
                dcb = jnp.zeros((ell, ell), F32)
                dcb_t = jnp.zeros((ell, ell), F32)
                s_all, ds_all = sin_ref[gi], dst[gi]
                c_s_all = lax.dot_general(cc, s_all.astype(BF16), nt_dims, preferred_element_type=F32)
                b_ds_all = lax.dot_general(bb, ds_all.astype(BF16), nt_dims, preferred_element_type=F32)
                s_ds = jnp.sum(s_all * ds_all, axis=1, keepdims=True)
                dy_decayed, x_weighted, keep = [], [], []
            lanes = slice(gi * gw + hh * hd, gi * gw + (hh + 1) * hd)
            head = slice(hh * hd, (hh + 1) * hd)
            hl = gi * LANES + hh
            onehot = (lane == hl).astype(F32)
            col_a, row_a = acum[:, hl:hl + 1], acum_t[hl:hl + 1, :]
            decay = jnp.exp(jnp.where(row >= col, col_a - row_a, NEG_BIG))
            decay_t = jnp.exp(jnp.where(col >= row, row_a - col_a, NEG_BIG))
            e_col = jnp.exp(col_a)
            a_last = acum[ell - 1:ell, hl:hl + 1]
            w = jnp.exp(a_last - col_a)
            e_last = jnp.exp(a_last)
            xs_h, dy_h = xs_ref[:, lanes], dy_ref[:, lanes]
            dt_h = dtg[:, hl:hl + 1]
            xdt = xs_h * dt_h
            xdt_b, dy_b = xdt.astype(BF16), dy_h.astype(BF16)
            dm_decay = lax.dot_general(dy_b, xdt_b, nt_dims, preferred_element_type=F32) * decay
            dm_decay_t = lax.dot_general(xdt_b, dy_b, nt_dims, preferred_element_type=F32) * decay_t
            dcb += dm_decay
            dcb_t += dm_decay_t
            m_t = cb_t * decay_t
            dac = jnp.sum(dm_decay * cb, axis=1, keepdims=True) - jnp.sum(dm_decay_t * cb_t, axis=1, keepdims=True)
            b_ds = b_ds_all[:, head]
            dxdt = jnp.dot(m_t.astype(BF16), dy_b, preferred_element_type=F32) + w * b_ds
            dac += jnp.sum(dy_h * c_s_all[:, head], axis=1, keepdims=True) * e_col
            q = jnp.sum(xdt * b_ds, axis=1, keepdims=True) * w
            dac -= q
            d_last = jnp.sum(q, axis=0, keepdims=True) + e_last * jnp.sum(s_ds[head], axis=0, keepdims=True)
            is_last = lax.broadcasted_iota(jnp.int32, (ell, 1), 0) == ell - 1
            dac += jnp.where(is_last, d_last, 0.0)
            dacum += dac * onehot
            dy_decayed.append((dy_h * e_col).astype(BF16))
            x_weighted.append((xdt * w).astype(BF16))
            keep.append(jnp.broadcast_to(e_last, (hd, 1)))
            dxs_ref[:, lanes] = dxdt * dt_h + dy_h * par_ref[2:3, hl:hl + 1]
            xsum += jnp.sum(dxdt * xs_h, axis=1, keepdims=True) * onehot
            dsum += jnp.sum(jnp.sum(dy_h * xs_h, axis=1, keepdims=True), axis=0, keepdims=True) * onehot
            if hh == hpg - 1:
                group = slice(gi * ns, (gi + 1) * ns)
                dy_all, x_all = jnp.concatenate(dy_decayed, axis=1), jnp.concatenate(x_weighted, axis=1)
                dc_ref[:, group] = (jnp.dot(dy_all, s_all.astype(BF16), preferred_element_type=F32)
                                    + jnp.dot(dcb.astype(BF16), bb, preferred_element_type=F32))
                db_ref[:, group] = (jnp.dot(x_all, ds_all.astype(BF16), preferred_element_type=F32)
                                    + jnp.dot(dcb_t.astype(BF16), cc, preferred_element_type=F32))
                dst[gi] = jnp.concatenate(keep, axis=0) * ds_all + lax.dot_general(
                    dy_all, cc, tn_dims, preferred_element_type=F32)
        dda = _split_dot(dacum, (col >= row).astype(BF16), SSD_CUMSUM_PIECES, left=True)
        ddtp = (xsum + dda * a_g) * _sigmoid(dtp_ref[...] + par_ref[0:1, :])
        ddtp_ref[...] = ddtp
        dpar_ref[0:1, :] += jnp.sum(ddtp, axis=0, keepdims=True)
        dpar_ref[1:2, :] += jnp.sum(dda * dtg, axis=0, keepdims=True) * a_g
        dpar_ref[2:3, :] += dsum

    rev = lambda i: nc - 1 - i
    return pl.pallas_call(
        body, name=name, grid=(n_groups // gps, nc),
        in_specs=[pl.BlockSpec((ell, gps * gw), lambda g, i: (rev(i), g)),
                  pl.BlockSpec((ell, gps * gw), lambda g, i: (rev(i), g)),
                  pl.BlockSpec((ell, gps * ns), lambda g, i: (rev(i), b_blk0 + g)),
                  pl.BlockSpec((ell, gps * ns), lambda g, i: (rev(i), c_blk0 + g)),
                  pl.BlockSpec((ell, gps * LANES), lambda g, i: (rev(i), g)),
                  pl.BlockSpec((SUBLANES, gps * LANES), lambda g, i: (0, g)),
                  pl.BlockSpec((None, gps, gw, ns), lambda g, i: (rev(i), g, 0, 0))],
        out_specs=[pl.BlockSpec((ell, gps * gw), lambda g, i: (rev(i), g)),
                   pl.BlockSpec((ell, gps * ns), lambda g, i: (rev(i), g)),
                   pl.BlockSpec((ell, gps * ns), lambda g, i: (rev(i), g)),
                   pl.BlockSpec((ell, gps * LANES), lambda g, i: (rev(i), g)),
                   pl.BlockSpec((SUBLANES, gps * LANES), lambda g, i: (0, g))],
        out_shape=[jax.ShapeDtypeStruct((t, n_groups * gw), F32), jax.ShapeDtypeStruct((t, n_groups * ns), F32),
                   jax.ShapeDtypeStruct((t, n_groups * ns), F32), jax.ShapeDtypeStruct((t, n_groups * LANES), F32),
                   jax.ShapeDtypeStruct((SUBLANES, n_groups * LANES), F32)],
        scratch_shapes=[pltpu.VMEM((gps, gw, ns), F32)],
        compiler_params=_params("parallel", "arbitrary"),
    )(dy, xa, xa, xa, dtp, par, s_in)


def _ssd_gate_fwd(y, xa, zx, d_rep, out_norm, *, name):
    t, di = y.shape
    gw = SSD_HEADS_PER_GROUP * SSD_HEAD_DIM
    tm = _tile(t, 512, SUBLANES)

    def body(y_ref, xs_ref, z_ref, d_ref, n_ref, o_ref):
        zv = z_ref[...]
        gt = (y_ref[...] + d_ref[...] * xs_ref[...]) * (zv * _sigmoid(zv))
        r = lax.rsqrt(jnp.mean(gt * gt, axis=-1, keepdims=True) + NORM_EPS)
        o_ref[...] = (gt * r * n_ref[...]).astype(BF16)

    blk = pl.BlockSpec((tm, gw), lambda i, g: (i, g))
    vec = pl.BlockSpec((1, gw), lambda i, g: (0, g))
    return pl.pallas_call(
        body, name=name, grid=(t // tm, di // gw),
        in_specs=[blk, blk, blk, vec, vec], out_specs=blk,
        out_shape=jax.ShapeDtypeStruct((t, di), BF16),
        compiler_params=_params("parallel", "parallel"),
    )(y, xa, zx, d_rep, out_norm)


def _ssd_gate_bwd(dgn, y, xa, zx, d_rep, out_norm, *, name):
    t, di = y.shape
    gw = SSD_HEADS_PER_GROUP * SSD_HEAD_DIM
    tm = _tile(t, 512, SUBLANES)

    def body(dg_ref, y_ref, xs_ref, z_ref, d_ref, n_ref, dy_ref, dz_ref, dn_ref):
        @pl.when(pl.program_id(1) == 0)
        def _():
            dn_ref[...] = jnp.zeros_like(dn_ref)

        zv = z_ref[...]
        sg = _sigmoid(zv)
        sz = zv * sg
        y2 = y_ref[...] + d_ref[...] * xs_ref[...]
        gt = y2 * sz
        r = lax.rsqrt(jnp.mean(gt * gt, axis=-1, keepdims=True) + NORM_EPS)
        ghat = gt * r
        dgv = dg_ref[...]
        dn_ref[...] += jnp.sum(dgv * ghat, axis=0, keepdims=True)
        u = dgv * n_ref[...]
        dgt = r * (u - ghat * jnp.mean(u * ghat, axis=-1, keepdims=True))
        dy_ref[...] = dgt * sz
        dz_ref[...] = (dgt * y2 * (sg * (1.0 + zv * (1.0 - sg)))).astype(BF16)

    blk = pl.BlockSpec((tm, gw), lambda g, i: (i, g))
    vec = pl.BlockSpec((1, gw), lambda g, i: (0, g))
    return pl.pallas_call(
        body, name=name, grid=(di // gw, t // tm),
        in_specs=[blk, blk, blk, blk, vec, vec], out_specs=[blk, blk, vec],
        out_shape=[jax.ShapeDtypeStruct((t, di), F32), jax.ShapeDtypeStruct((t, di), BF16),
                   jax.ShapeDtypeStruct((1, di), F32)],
        compiler_params=_params("parallel", "arbitrary"),
    )(dgn, y, xa, zx, d_rep, out_norm)


def _ssd_mixer_fwd(x, gain, w_zx_t, w_dt_t, conv_w, conv_b, par, d_rep, out_norm, w_out, tag):
    di = w_out.shape[0]
    n_groups = di // (SSD_HEADS_PER_GROUP * SSD_HEAD_DIM)
    h = _rms_fwd(x, gain, name=f"ssd_norm_{tag}")
    zx = _matmul(h, w_zx_t, "nt", name=f"ssd_in_{tag}")
    dtp = _matmul(h, w_dt_t, "nt", name=f"ssd_dt_{tag}")
    xa = _ssd_conv_fwd(zx, conv_w, conv_b, di, name=f"ssd_conv_{tag}")
    y, s_in = _ssd_scan_fwd(xa, dtp, par, n_groups, name=f"ssd_scan_{tag}")
    gn = _ssd_gate_fwd(y, xa, zx, d_rep, out_norm, name=f"ssd_gate_{tag}")
    x_new = _matmul(gn, w_out, "nn", resid=x, name=f"ssd_out_{tag}")
    return x_new, (x, h, zx, dtp, xa, y, s_in, gn)


def _ssd_mixer_bwd(dx, saved, gain, w_zx_t, w_dt_t, conv_w, conv_b, par, d_rep, out_norm, w_out, grads, row_out,
                   tag):
    x, h, zx, dtp, xa, y, s_in, gn = saved
    di = w_out.shape[0]
    n_groups = di // (SSD_HEADS_PER_GROUP * SSD_HEAD_DIM)
    dgn = _matmul(dx, w_out, "nt", name=f"ssd_dgn_{tag}")
    grads = _matmul_tn_into(grads, gn, dx, row_out, name=f"ssd_dwout_{tag}")
    dy2, dz, dnorm = _ssd_gate_bwd(dgn, y, xa, zx, d_rep, out_norm, name=f"ssd_dgate_{tag}")
    dxs, db, dc, ddtp, dpar = _ssd_scan_bwd(dy2, xa, dtp, par, s_in, n_groups, name=f"ssd_dscan_{tag}")
    dxbc, dconv_w, dconv_b = _ssd_conv_bwd([dxs, db, dc], zx, conv_w, conv_b, di, name=f"ssd_dconv_{tag}")
    dzx = jnp.concatenate([dz, dxbc], axis=1)
    dw_zx_t = _matmul_tn(dzx, h, name=f"ssd_dwin_{tag}")
    dw_dt_t = _matmul_tn(ddtp, h, name=f"ssd_dwdt_{tag}")
    dh = _matmul(dzx, w_zx_t, "nn", name=f"ssd_dh_{tag}")
    dh = _matmul(ddtp, w_dt_t, "nn", resid=dh, name=f"ssd_dhdt_{tag}")
    dx_in, dgain = _rms_bwd(x, gain, dh, dx, name=f"ssd_dnorm_{tag}")
    return dx_in, grads, dw_zx_t, dw_dt_t, dconv_w, dconv_b, dpar, dnorm, dgain


HEAD_SUM_PIECES = 3


def _head_sums(v):
    row = lax.broadcasted_iota(jnp.int32, (LANES, LANES), 0)
    col = lax.broadcasted_iota(jnp.int32, (LANES, LANES), 1)
    same_head = (row // SB_HEAD_DIM == col // SB_HEAD_DIM).astype(BF16)
    return _split_dot(v, same_head, HEAD_SUM_PIECES)


def _sb_qk_norm_fwd(qkv, gains, *, name):
    ns, t, _ = qkv.shape
    per = ns // 3
    tm = _tile(t, 1024, SUBLANES)
    inv_sqrt_d = 1.0 / math.sqrt(SB_HEAD_DIM)

    def body(x_ref, g_ref, o_ref):
        kind = pl.program_id(0) // per
        xv = x_ref[...]

        @pl.when(kind == 2)
        def _():
            o_ref[...] = xv.astype(BF16)

        @pl.when(kind < 2)
        def _():
            ms = _head_sums(xv * xv) * (1.0 / SB_HEAD_DIM)
            y = xv * lax.rsqrt(ms + NORM_EPS) * g_ref[pl.ds(kind, 1), :]
            o_ref[...] = (y * jnp.where(kind == 0, inv_sqrt_d, 1.0)).astype(BF16)

    blk = pl.BlockSpec((None, tm, LANES), lambda s, i: (s, i, 0))
    return pl.pallas_call(
        body, name=name, grid=(ns, t // tm),
        in_specs=[blk, pl.BlockSpec((SUBLANES, LANES), lambda s, i: (0, 0))], out_specs=blk,
        out_shape=jax.ShapeDtypeStruct((ns, t, LANES), BF16),
        compiler_params=_params("parallel", "parallel"),
    )(qkv, gains)


def _sb_qk_norm_bwd(dq, dk, dv, qkv, gains, *, name):
    ns, t, _ = qkv.shape
    per = ns // 3
    tm = _tile(t, 1024, SUBLANES)
    inv_sqrt_d = 1.0 / math.sqrt(SB_HEAD_DIM)

    def body(dq_ref, dk_ref, dv_ref, x_ref, g_ref, o_ref, dg_ref):
        s = pl.program_id(0)
        kind = s // per

        @pl.when((s == 0) & (pl.program_id(1) == 0))
        def _():
            dg_ref[...] = jnp.zeros_like(dg_ref)

        @pl.when(kind == 2)
        def _():
            o_ref[...] = dv_ref[...].astype(BF16)

        @pl.when(kind < 2)
        def _():
            xv = x_ref[...]
            dy = jnp.where(kind == 0, dq_ref[...] * inv_sqrt_d, dk_ref[...])
            r = lax.rsqrt(_head_sums(xv * xv) * (1.0 / SB_HEAD_DIM) + NORM_EPS)
            xhat = xv * r
            u = dy * g_ref[pl.ds(kind, 1), :]
            o_ref[...] = (r * (u - xhat * _head_sums(u * xhat) * (1.0 / SB_HEAD_DIM))).astype(BF16)
            dg_ref[pl.ds(kind, 1), :] += jnp.sum(dy * xhat, axis=0, keepdims=True)

    def grad_blk(kind):
        def index(s, i):
            mine = (s >= kind * per) & (s < (kind + 1) * per)
            return jnp.where(mine, s - kind * per, 0), jnp.where(mine, i, 0), 0
        return pl.BlockSpec((None, tm, LANES), index)

    blk = pl.BlockSpec((None, tm, LANES), lambda s, i: (s, i, 0))
    vec = pl.BlockSpec((SUBLANES, LANES), lambda s, i: (0, 0))
    return pl.pallas_call(
        body, name=name, grid=(ns, t // tm),
        in_specs=[grad_blk(0), grad_blk(1), grad_blk(2), blk, vec], out_specs=[blk, vec],
        out_shape=[jax.ShapeDtypeStruct((ns, t, LANES), BF16), jax.ShapeDtypeStruct((SUBLANES, LANES), F32)],
        compiler_params=_params("arbitrary", "arbitrary"),
    )(dq, dk, dv, qkv, gains)


def _split_dot(v, ones_mat, pieces, left=False):
    total, rest = None, v
    for p in range(pieces):
        part = rest.astype(BF16)
        if p + 1 < pieces:
            rest = rest - part.astype(F32)
        d = (jnp.dot(ones_mat, part, preferred_element_type=F32) if left
             else jnp.dot(part, ones_mat, preferred_element_type=F32))
        total = d if total is None else total + d
    return total


LOGIT_SUM_PIECES = 2
GRAD_SUM_PIECES = 2
LOG_WEIGHT_UNDERFLOW = -105.0


def _sb_attn_fwd(qkv_n, n_heads, *, name):
    ns, t, _ = qkv_n.shape
    per = ns // 3
    bq, blk, hd = SB_QUERY_BLOCK, SB_BLOCK, SB_HEAD_DIM
    nq, n_diag = t // bq, bq // blk

    def body(q_ref, k_ref, v_ref, o_ref):
        i = pl.program_id(1)
        row = lax.broadcasted_iota(jnp.int32, (blk, blk), 0)
        col = lax.broadcasted_iota(jnp.int32, (blk, blk), 1)
        later_keys = (row > col).astype(BF16)
        qry = lax.broadcasted_iota(jnp.int32, (bq, blk), 0)
        key = lax.broadcasted_iota(jnp.int32, (bq, blk), 1)

        def tile(kb, carry, key_offset):
            out = []
            start = pl.multiple_of(kb * blk, blk)
            for hf in range(2):
                lanes = slice(hf * hd, (hf + 1) * hd)
                run, acc = carry[hf]
                z = lax.dot_general(q_ref[:, lanes], k_ref[pl.ds(start, blk), lanes], NT_DIMS,
                                    preferred_element_type=F32)
                sp = _softplus(z)
                lm = -sp if key_offset is None else jnp.where(key + key_offset < qry, -sp, 0.0)
                after = _split_dot(lm, later_keys, LOGIT_SUM_PIECES) + run
                a = jnp.exp(z - sp + after)
                if key_offset is not None:
                    a = jnp.where(key + key_offset < qry, a, 0.0)
                acc = acc + jnp.dot(a.astype(BF16), v_ref[pl.ds(start, blk), lanes], preferred_element_type=F32)
                out.append((run + jnp.sum(lm, axis=1, keepdims=True), acc))
            return tuple(out)

        def live(carry):
            return jnp.max(jnp.maximum(carry[0][0], carry[1][0])) > LOG_WEIGHT_UNDERFLOW

        def step(state):
            s, _, carry = state
            carry = tile(n_diag * i - 1 - s, carry, None)
            return s + 1, live(carry), carry

        carry = tuple((jnp.zeros((bq, 1), F32), jnp.zeros((bq, hd), F32)) for _ in range(2))
        for j in reversed(range(n_diag)):
            carry = tile(n_diag * i + j, carry, j * blk)
        _, _, carry = lax.while_loop(lambda st: (st[0] < n_diag * i) & st[1], step,
                                     (jnp.int32(0), live(carry), carry))
        o_ref[...] = jnp.concatenate([carry[0][1], carry[1][1]], axis=1)

    return pl.pallas_call(
        body, name=name, grid=(per, nq),
        in_specs=[pl.BlockSpec((None, bq, LANES), lambda p, i: (p, i, 0)),
                  pl.BlockSpec((None, t, LANES), lambda p, i: (per + p, 0, 0)),
                  pl.BlockSpec((None, t, LANES), lambda p, i: (2 * per + p, 0, 0))],
        out_specs=pl.BlockSpec((bq, LANES), lambda p, i: (i, p)),
        out_shape=jax.ShapeDtypeStruct((t, n_heads * hd), F32),
        compiler_params=_params("parallel", "arbitrary"),
    )(qkv_n, qkv_n, qkv_n)


def _sb_attn_bwd(do, qkv_n, *, name):
    ns, t, _ = qkv_n.shape
    per = ns // 3
    bq, blk, hd = SB_QUERY_BLOCK, SB_BLOCK, SB_HEAD_DIM
    nq, n_diag = t // bq, bq // blk
    nt_dims = (((1,), (1,)), ((), ()))
    tn_dims = (((0,), (0,)), ((), ()))

    def body(q_ref, k_ref, v_ref, do_ref, dq_ref, dk_ref, dv_ref):
        i = pl.program_id(1)

        @pl.when(i == 0)
        def _():
            dk_ref[...] = jnp.zeros_like(dk_ref)
            dv_ref[...] = jnp.zeros_like(dv_ref)

        row = lax.broadcasted_iota(jnp.int32, (blk, blk), 0)
        col = lax.broadcasted_iota(jnp.int32, (blk, blk), 1)
        later_keys = (col > row).astype(BF16)
        earlier_keys = (col < row).astype(BF16)
        key = lax.broadcasted_iota(jnp.int32, (blk, bq), 0)
        qry = lax.broadcasted_iota(jnp.int32, (blk, bq), 1)
        halves = [slice(hf * hd, (hf + 1) * hd) for hf in range(2)]
        q_hs = [q_ref[:, lanes] for lanes in halves]
        do_bs = [do_ref[:, lanes].astype(BF16) for lanes in halves]

        def scores(kb, hf, key_offset):
            k_blk = k_ref[pl.ds(pl.multiple_of(kb * blk, blk), blk), halves[hf]]
            z = lax.dot_general(k_blk, q_hs[hf], nt_dims, preferred_element_type=F32)
            sp = _softplus(z)
            return k_blk, z, sp, (-sp if key_offset is None else jnp.where(key + key_offset < qry, -sp, 0.0))

        def add_column_sums(tots, kb, key_offset):
            return [tots[hf] + jnp.sum(scores(kb, hf, key_offset)[3], axis=0, keepdims=True) for hf in range(2)]

        def live(tots):
            return jnp.max(jnp.maximum(tots[0], tots[1])) > LOG_WEIGHT_UNDERFLOW

        def reach(state):
            s, _, tots = state
            tots = add_column_sums(tots, n_diag * i - 1 - s, None)
            return s + 1, live(tots), tots

        tots = [jnp.zeros((1, bq), F32)] * 2
        for j in reversed(range(n_diag)):
            tots = add_column_sums(tots, n_diag * i + j, j * blk)
        reached, _, tots = lax.while_loop(lambda st: (st[0] < n_diag * i) & st[1], reach,
                                          (jnp.int32(0), live(tots), tots))

        def tile(kb, carry, key_offset):
            out = []
            start = pl.multiple_of(kb * blk, blk)
            for hf, lanes in enumerate(halves):
                seen, gsum, dq = carry[hf]
                q_h, do_b = q_hs[hf], do_bs[hf]
                k_blk, z, sp, lm = scores(kb, hf, key_offset)
                blk_tot = jnp.sum(lm, axis=0, keepdims=True)
                after = _split_dot(lm, later_keys, LOGIT_SUM_PIECES, left=True) + (tots[hf] - seen - blk_tot)
                a = jnp.exp(z - sp + after)
                if key_offset is not None:
                    a = jnp.where(key + key_offset < qry, a, 0.0)
                da = lax.dot_general(v_ref[pl.ds(start, blk), lanes], do_b, nt_dims, preferred_element_type=F32)
                g = da * a
                before = _split_dot(g, earlier_keys, GRAD_SUM_PIECES, left=True) + gsum
                omb = jnp.exp(-sp)
                dz = g * omb - (1.0 - omb) * before
                if key_offset is not None:
                    dz = jnp.where(key + key_offset < qry, dz, 0.0)
                dz_b = dz.astype(BF16)
                dk_ref[pl.ds(start, blk), lanes] += jnp.dot(dz_b, q_h, preferred_element_type=F32)
                dv_ref[pl.ds(start, blk), lanes] += jnp.dot(a.astype(BF16), do_b, preferred_element_type=F32)
                dq = dq + lax.dot_general(dz_b, k_blk, tn_dims, preferred_element_type=F32)
                out.append((seen + blk_tot, gsum + jnp.sum(g, axis=0, keepdims=True), dq))
            return tuple(out)

        init = tuple((jnp.zeros((1, bq), F32), jnp.zeros((1, bq), F32), jnp.zeros((bq, hd), F32))
                     for _ in range(2))
        carry = lax.fori_loop(n_diag * i - reached, n_diag * i, lambda kb, c: tile(kb, c, None), init)
        for j in range(n_diag):
            carry = tile(n_diag * i + j, carry, j * blk)
        dq_ref[...] = jnp.concatenate([carry[0][2], carry[1][2]], axis=1)

    full = lambda off: pl.BlockSpec((None, t, LANES), lambda p, i: (off + p, 0, 0))
    q_blk = pl.BlockSpec((None, bq, LANES), lambda p, i: (p, i, 0))
    slab = jax.ShapeDtypeStruct((per, t, LANES), F32)
    return pl.pallas_call(
        body, name=name, grid=(per, nq),
        in_specs=[q_blk, full(per), full(2 * per), pl.BlockSpec((bq, LANES), lambda p, i: (i, p))],
        out_specs=[q_blk, full(0), full(0)],
        out_shape=[slab, slab, slab],
        compiler_params=_params("parallel", "arbitrary"),
    )(qkv_n, qkv_n, qkv_n, do)


def _sb_mixer_fwd(x, gain, w_qkv_t, qk_gains, w_out, tag):
    n_heads = w_out.shape[0] // SB_HEAD_DIM
    h = _rms_fwd(x, gain, name=f"sb_norm_{tag}")
    qkv = _matmul(h, w_qkv_t, "nt", out_slabs=True, tn_cap=512, name=f"sb_qkv_{tag}")
    qkv_n = _sb_qk_norm_fwd(qkv, qk_gains, name=f"sb_qknorm_{tag}")
    o = _sb_attn_fwd(qkv_n, n_heads, name=f"sb_attn_{tag}")
    x_new = _matmul(o, w_out, "nn", resid=x, name=f"sb_out_{tag}")
    return x_new, (x, h, qkv, qkv_n, o)


def _sb_mixer_bwd(dx, saved, gain, w_qkv_t, qk_gains, w_out, grads, row_qkv, row_out, tag):
    x, h, qkv, qkv_n, o = saved
    do = _matmul(dx, w_out, "nt", name=f"sb_do_{tag}")
    grads = _matmul_tn_into(grads, o, dx, row_out, name=f"sb_dwout_{tag}")
    dq, dk, dv = _sb_attn_bwd(do, qkv_n, name=f"sb_dattn_{tag}")
    dqkv, dqk_gains = _sb_qk_norm_bwd(dq, dk, dv, qkv, qk_gains, name=f"sb_dqknorm_{tag}")
    grads = _matmul_tn_into(grads, dqkv, h, row_qkv, a_slabs=True, name=f"sb_dwqkv_{tag}")
    dh = _matmul(dqkv, w_qkv_t, "nn", a_slabs=True, name=f"sb_dh_{tag}")
    dx_in, dgain = _rms_bwd(x, gain, dh, dx, name=f"sb_dnorm_{tag}")
    return dx_in, grads, dqk_gains, dgain


MESH = pl.DeviceIdType.MESH


def _position():
    return lax.axis_index("x"), lax.axis_index("y"), lax.axis_index("c")


def _all_gather(shard, *, name):
    rows, n = shard.shape
    space = pltpu.VMEM

    def body(x_ref, out_ref, send_sems, recv_sems, local_sem):
        x, y, c = _position()
        me, sibling = (x, y, c), (x, y, 1 - c)
        chips = [(1 - x, y), (x, 1 - y), (1 - x, 1 - y)]

        def block(px, py, pc):
            return out_ref.at[4 * px + 2 * py + pc]

        def copy(k, blk, to, src=None):
            return pltpu.make_async_remote_copy(
                src_ref=block(*blk) if src is None else src, dst_ref=block(*blk),
                send_sem=send_sems.at[k], recv_sem=recv_sems.at[k], device_id=to, device_id_type=MESH)

        mine = pltpu.make_async_copy(x_ref, block(*me), local_sem)
        mine.start()
        first = [copy(0, me, sibling, src=x_ref)]
        first += [copy(1 + j, me, (*chip, c), src=x_ref) for j, chip in enumerate(chips)]
        for cp in first:
            cp.start()
        passed = [copy(4 + j, (*chip, c), sibling) for j, chip in enumerate(chips)]
        for j, chip in enumerate(chips):
            copy(1 + j, (*chip, c), me).wait_recv()
            passed[j].start()
        copy(0, sibling, me).wait_recv()
        for j, chip in enumerate(chips):
            copy(4 + j, (*chip, 1 - c), me).wait_recv()
        for cp in first + passed:
            cp.wait_send()
        mine.wait()

    return pl.pallas_call(
        body, name=name,
        out_shape=jax.ShapeDtypeStruct((N_DEV, rows, n), shard.dtype),
        in_specs=[pl.BlockSpec(memory_space=space)], out_specs=pl.BlockSpec(memory_space=space),
        scratch_shapes=[pltpu.SemaphoreType.DMA((7,)), pltpu.SemaphoreType.DMA((7,)), pltpu.SemaphoreType.DMA],
        compiler_params=pltpu.CompilerParams(vmem_limit_bytes=V7X_VMEM_LIMIT_BYTES),
    )(shard)


def _all_gather_forwarding(shard, *, name):
    rows, n = shard.shape
    half = rows // 2
    assert rows % (4 * SUBLANES) == 0

    def body(x_ref, out_ref, send_sems, recv_sems, local_sem):
        x, y, c = _position()
        me, sibling = (x, y, c), (x, y, 1 - c)
        x_nbr, y_nbr, diag = (1 - x, y), (x, 1 - y), (1 - x, 1 - y)
        lower, upper = pl.ds(0, half), pl.ds(half, half)

        def block(px, py, pc, part=None):
            ref = out_ref.at[4 * px + 2 * py + pc]
            return ref if part is None else ref.at[part]

        def copy(k, blk, to, src=None, part=None):
            return pltpu.make_async_remote_copy(
                src_ref=block(*blk, part) if src is None else src, dst_ref=block(*blk, part),
                send_sem=send_sems.at[k], recv_sem=recv_sems.at[k], device_id=to, device_id_type=MESH)

        mine = pltpu.make_async_copy(x_ref, block(*me), local_sem)
        mine.start()
        sent = [copy(0, me, sibling, src=x_ref), copy(1, me, (*x_nbr, c), src=x_ref),
                copy(2, me, (*y_nbr, c), src=x_ref)]
        for cp in sent:
            cp.start()
        copy(1, (*x_nbr, c), me).wait_recv()
        onward = [copy(3, (*x_nbr, c), (*y_nbr, c), part=lower), copy(5, (*x_nbr, c), sibling)]
        for cp in onward:
            cp.start()
        copy(2, (*y_nbr, c), me).wait_recv()
        onward += [copy(4, (*y_nbr, c), (*x_nbr, c), part=upper), copy(6, (*y_nbr, c), sibling)]
        for cp in onward[2:]:
            cp.start()
        copy(3, (*diag, c), me, part=lower).wait_recv()
        copy(4, (*diag, c), me, part=upper).wait_recv()
        onward.append(copy(7, (*diag, c), sibling))
        onward[-1].start()
        sent += onward
        copy(0, sibling, me).wait_recv()
        for k, chip in ((5, x_nbr), (6, y_nbr), (7, diag)):
            copy(k, (*chip, 1 - c), me).wait_recv()
        for cp in sent:
            cp.wait_send()
        mine.wait()

    hbm = pl.BlockSpec(memory_space=pltpu.HBM)
    return pl.pallas_call(
        body, name=name,
        out_shape=jax.ShapeDtypeStruct((N_DEV, rows, n), shard.dtype), in_specs=[hbm], out_specs=hbm,
        scratch_shapes=[pltpu.SemaphoreType.DMA((8,)), pltpu.SemaphoreType.DMA((8,)), pltpu.SemaphoreType.DMA],
    )(shard)


def _exchange_sibling(parts, *, name):
    _, nchip, rows, n = parts.shape

    def body(p_ref, recv_ref, send_sem, recv_sem):
        x, y, c = _position()
        cp = pltpu.make_async_remote_copy(src_ref=p_ref.at[1 - c], dst_ref=recv_ref, send_sem=send_sem,
                                          recv_sem=recv_sem, device_id=(x, y, 1 - c), device_id_type=MESH)
        cp.start()
        cp.wait()

    return pl.pallas_call(
        body, name=name,
        out_shape=jax.ShapeDtypeStruct((nchip, rows, n), parts.dtype),
        in_specs=[pl.BlockSpec(memory_space=pltpu.HBM)], out_specs=pl.BlockSpec(memory_space=pltpu.HBM),
        scratch_shapes=[pltpu.SemaphoreType.DMA, pltpu.SemaphoreType.DMA],
    )(parts)


def _exchange_chips(chip_sums, *, name):
    _, rows, n = chip_sums.shape

    def body(s_ref, recv_ref, send_sems, recv_sems):
        x, y, c = _position()
        chips = [(1 - x, y), (x, 1 - y), (1 - x, 1 - y)]
        copies = [pltpu.make_async_remote_copy(
            src_ref=s_ref.at[2 * cx + cy], dst_ref=recv_ref.at[j], send_sem=send_sems.at[j],
            recv_sem=recv_sems.at[j], device_id=(cx, cy, c), device_id_type=MESH)
            for j, (cx, cy) in enumerate(chips)]
        for cp in copies:
            cp.start()
        for cp in copies:
            cp.wait()

    return pl.pallas_call(
        body, name=name,
        out_shape=jax.ShapeDtypeStruct((3, rows, n), chip_sums.dtype),
        in_specs=[pl.BlockSpec(memory_space=pltpu.HBM)], out_specs=pl.BlockSpec(memory_space=pltpu.HBM),
        scratch_shapes=[pltpu.SemaphoreType.DMA((3,)), pltpu.SemaphoreType.DMA((3,))],
    )(chip_sums)


def _add_pairs(parts, recv, c_mine, *, name):
    _, nchip, rows, n = parts.shape
    tr = _tile(rows, 512, SUBLANES)

    def body(c_ref, a_ref, b_ref, o_ref, wire_ref):
        s = a_ref[...] + b_ref[...]
        o_ref[...] = s
        wire_ref[...] = s.astype(WIRE_DTYPE)

    out_blk = pl.BlockSpec((None, tr, n), lambda k, i, c: (k, i, 0))
    return pl.pallas_call(
        body, name=name,
        grid_spec=pltpu.PrefetchScalarGridSpec(
            num_scalar_prefetch=1, grid=(nchip, rows // tr),
            in_specs=[pl.BlockSpec((None, None, tr, n), lambda k, i, c: (c[0], k, i, 0)),
                      pl.BlockSpec((None, tr, n), lambda k, i, c: (k, i, 0))],
            out_specs=[out_blk, out_blk]),
        out_shape=[jax.ShapeDtypeStruct((nchip, rows, n), parts.dtype),
                   jax.ShapeDtypeStruct((nchip, rows, n), WIRE_DTYPE)],
        compiler_params=_params("parallel", "parallel"),
    )(c_mine, parts, recv)


def _adamw_math(w, g, m, v):
    m = ADAM_B1 * m + (1.0 - ADAM_B1) * g
    v = ADAM_B2 * v + (1.0 - ADAM_B2) * (g * g)
    m_hat = m / (1.0 - ADAM_B1 ** ADAM_STEP)
    v_hat = v / (1.0 - ADAM_B2 ** ADAM_STEP)
    delta = -ADAM_LR * (m_hat / (jnp.sqrt(v_hat) + ADAM_EPS) + ADAM_WD * w)
    return delta, m, v


def _adamw_sharded(chip_sums, recv, k_mine, w, m, v, *, name):
    rows, n = w.shape
    tr = _tile(rows, 256, SUBLANES)

    def body(k_ref, s_ref, r_ref, w_ref, m_ref, v_ref, g_out, d_out, m_out, v_out):
        g = ((s_ref[...] + r_ref[0].astype(F32)) + r_ref[1].astype(F32)) + r_ref[2].astype(F32)
        delta, m_new, v_new = _adamw_math(w_ref[...], g, m_ref[...], v_ref[...])
        g_out[...] = g
        d_out[...] = delta
        m_out[...] = m_new
        v_out[...] = v_new

    blk = pl.BlockSpec((tr, n), lambda i, k: (i, 0))
    out = jax.ShapeDtypeStruct((rows, n), F32)
    return pl.pallas_call(
        body, name=name,
        grid_spec=pltpu.PrefetchScalarGridSpec(
            num_scalar_prefetch=1, grid=(rows // tr,),
            in_specs=[pl.BlockSpec((None, tr, n), lambda i, k: (k[0], i, 0)),
                      pl.BlockSpec((3, tr, n), lambda i, k: (0, i, 0)), blk, blk, blk],
            out_specs=[blk, blk, blk, blk]),
        out_shape=[out, out, out, out],
        compiler_params=_params("parallel"),
    )(k_mine, chip_sums, recv, w, m, v)


SMALL_ROWS = 40
ROW_MIX_NORM, ROW_FFN_NORM, ROW_CONV_B, ROW_OUT_NORM, ROW_POOL_SCALE, ROW_CONV_W = 0, 4, 8, 12, 14, 16
ROW_SSD_VEC, ROW_QK_GAIN, ROW_LOSS = 32, 33, 34


def _adamw_small(gathered, w, m, v, *, name):
    _, rows, n = gathered.shape

    def body(a_ref, w_ref, m_ref, v_ref, g_out, d_out, m_out, v_out):
        g = a_ref[0]
        for d in range(1, N_DEV):
            g = g + a_ref[d]
        row = lax.broadcasted_iota(jnp.int32, (rows, 1), 0)
        g = jnp.where(row == ROW_QK_GAIN, g + pltpu.roll(g, SB_HEAD_DIM, 1), g)
        g = jnp.where(row == ROW_LOSS, jnp.sum(g, axis=1, keepdims=True), g)
        g_out[...] = g
        delta, m_new, v_new = _adamw_math(w_ref[...], g, m_ref[...], v_ref[...])
        d_out[...] = delta
        m_out[...] = m_new
        v_out[...] = v_new

    out = jax.ShapeDtypeStruct((rows, n), F32)
    return pl.pallas_call(body, name=name, out_shape=[out, out, out, out])(gathered, w, m, v)


BIG_WEIGHTS = ("ffn_gate", "ffn_up", "ffn_down", "sb_qkv", "ssd_out", "pool_in", "sb_out", "pool_group", "ssd_in")
COLUMN_SHARDED = ("ssd_in", "sb_qkv", "ffn_gate", "ffn_up")
ROW_PAD = 512
WIRE_DTYPE = jnp.bfloat16


def _to_rows(name, shard, d):
    if name in COLUMN_SHARDED:
        shard = jnp.swapaxes(shard, -1, -2)
    return shard.reshape(-1, d)


def _from_rows(name, rows, shard_shape):
    if name in COLUMN_SHARDED:
        lead, k, n = shard_shape
        return jnp.swapaxes(rows.reshape(lead, n, k), -1, -2)
    return rows.reshape(shard_shape)


def _pad_rows(a, total):
    return jnp.pad(a, ((0, total - a.shape[0]),) + ((0, 0),) * (a.ndim - 1))


def _exact_bf16_rows(v, d):
    words = lax.bitcast_convert_type(v.reshape(-1), WIRE_DTYPE).reshape(-1)
    return _pad_rows(words, -(-words.shape[0] // d) * d).reshape(-1, d)


def _exact_f32(rows, count):
    words = rows.reshape(rows.shape[0], -1)[:, :2 * count].reshape(rows.shape[0], count, 2)
    return lax.bitcast_convert_type(words, F32)


def _device_blocks(full, d):
    return full.reshape(N_DEV, -1, d)


def kernel(x, mix_norm, pool_in, pool_group, pool_scale, ssd_in, ssd_conv_w, ssd_conv_b, ssd_dt_bias, ssd_a_log, ssd_d, ssd_out_norm, ssd_out, sb_qkv, sb_q_norm, sb_k_norm, sb_out, ffn_norm, ffn_gate, ffn_up, ffn_down, loss_target, m_mix_norm, m_pool_in, m_pool_group, m_pool_scale, m_ssd_in, m_ssd_conv_w, m_ssd_conv_b, m_ssd_dt_bias, m_ssd_a_log, m_ssd_d, m_ssd_out_norm, m_ssd_out, m_sb_qkv, m_sb_q_norm, m_sb_k_norm, m_sb_out, m_ffn_norm, m_ffn_gate, m_ffn_up, m_ffn_down, v_mix_norm, v_pool_in, v_pool_group, v_pool_scale, v_ssd_in, v_ssd_conv_w, v_ssd_conv_b, v_ssd_dt_bias, v_ssd_a_log, v_ssd_d, v_ssd_out_norm, v_ssd_out, v_sb_qkv, v_sb_q_norm, v_sb_k_norm, v_sb_out, v_ffn_norm, v_ffn_gate, v_ffn_up, v_ffn_down):
    weights = dict(mix_norm=mix_norm, pool_in=pool_in, pool_group=pool_group, pool_scale=pool_scale, ssd_in=ssd_in,
                   ssd_conv_w=ssd_conv_w, ssd_conv_b=ssd_conv_b, ssd_dt_bias=ssd_dt_bias, ssd_a_log=ssd_a_log,
                   ssd_d=ssd_d, ssd_out_norm=ssd_out_norm, ssd_out=ssd_out, sb_qkv=sb_qkv, sb_q_norm=sb_q_norm,
                   sb_k_norm=sb_k_norm, sb_out=sb_out, ffn_norm=ffn_norm, ffn_gate=ffn_gate, ffn_up=ffn_up,
                   ffn_down=ffn_down)
    mom1 = dict(mix_norm=m_mix_norm, pool_in=m_pool_in, pool_group=m_pool_group, pool_scale=m_pool_scale,
                ssd_in=m_ssd_in, ssd_conv_w=m_ssd_conv_w, ssd_conv_b=m_ssd_conv_b, ssd_dt_bias=m_ssd_dt_bias,
                ssd_a_log=m_ssd_a_log, ssd_d=m_ssd_d, ssd_out_norm=m_ssd_out_norm, ssd_out=m_ssd_out,
                sb_qkv=m_sb_qkv, sb_q_norm=m_sb_q_norm, sb_k_norm=m_sb_k_norm, sb_out=m_sb_out,
                ffn_norm=m_ffn_norm, ffn_gate=m_ffn_gate, ffn_up=m_ffn_up, ffn_down=m_ffn_down)
    mom2 = dict(mix_norm=v_mix_norm, pool_in=v_pool_in, pool_group=v_pool_group, pool_scale=v_pool_scale,
                ssd_in=v_ssd_in, ssd_conv_w=v_ssd_conv_w, ssd_conv_b=v_ssd_conv_b, ssd_dt_bias=v_ssd_dt_bias,
                ssd_a_log=v_ssd_a_log, ssd_d=v_ssd_d, ssd_out_norm=v_ssd_out_norm, ssd_out=v_ssd_out,
                sb_qkv=v_sb_qkv, sb_q_norm=v_sb_q_norm, sb_k_norm=v_sb_k_norm, sb_out=v_sb_out,
                ffn_norm=v_ffn_norm, ffn_gate=v_ffn_gate, ffn_up=v_ffn_up, ffn_down=v_ffn_down)
    names = list(weights)
    depth, d = mix_norm.shape
    xs, ys, cs = _position()
    dev = 4 * xs + 2 * ys + cs
    chip = 2 * xs + ys

    seg = {}
    row = 0
    for name in BIG_WEIGHTS:
        n_rows = weights[name].size // d
        seg[name] = (row, n_rows)
        row += -(-n_rows // SUBLANES) * SUBLANES
    big_rows = row
    n_scale, n_convw = pool_scale.size, ssd_conv_w.size
    exact = jnp.concatenate([_exact_bf16_rows(pool_scale, d), _exact_bf16_rows(ssd_conv_w, d)], axis=0)
    scale_rows = _exact_bf16_rows(pool_scale, d).shape[0]
    packed_rows = -(-(big_rows + exact.shape[0]) // ROW_PAD) * ROW_PAD

    def pack(tree, dtype):
        ends = [seg[n][0] for n in BIG_WEIGHTS[1:]] + [big_rows]
        return jnp.concatenate([_pad_rows(_to_rows(n, tree[n], d).astype(dtype), end - seg[n][0])
                                for n, end in zip(BIG_WEIGHTS, ends)], axis=0)

    w_wire = _pad_rows(jnp.concatenate([pack(weights, WIRE_DTYPE), exact], axis=0), packed_rows)
    gathered = _all_gather_forwarding(w_wire, name="gather_weights")

    def seg_of(name):
        a, n = seg[name]
        return gathered[:, a:a + n]

    n_pool, n_ssd, n_sb = pool_in.shape[0], ssd_in.shape[0], sb_qkv.shape[0]
    assert n_ssd == 1 and n_sb == 1
    w_pool_in = seg_of("pool_in").reshape(N_DEV, n_pool, -1, d).transpose(1, 0, 2, 3).reshape(n_pool, d, d)
    grp = pool_group.shape
    w_pool_group = seg_of("pool_group").reshape(N_DEV, grp[0], grp[1], grp[2], grp[3]).transpose(1, 2, 0, 3, 4)
    w_pool_group = w_pool_group.reshape(grp[0], grp[1], grp[3], grp[3])
    w_ssd_in_t = seg_of("ssd_in").reshape(-1, d)
    w_ssd_out = seg_of("ssd_out").reshape(-1, d)
    w_sb_qkv_t = seg_of("sb_qkv").reshape(-1, d)
    w_sb_out = seg_of("sb_out").reshape(-1, d)
    hidden = ffn_down.shape[1] * N_DEV
    w_gate_t = seg_of("ffn_gate").reshape(N_DEV, depth, -1, d).transpose(1, 0, 2, 3).reshape(depth, hidden, d)
    w_up_t = seg_of("ffn_up").reshape(N_DEV, depth, -1, d).transpose(1, 0, 2, 3).reshape(depth, hidden, d)
    w_down = seg_of("ffn_down").reshape(N_DEV, depth, -1, d).transpose(1, 0, 2, 3).reshape(depth, hidden, d)
    exact_all = gathered[:, big_rows:big_rows + exact.shape[0]]
    scale_full = _exact_f32(exact_all[:, :scale_rows], n_scale).reshape(N_DEV, n_pool, -1)
    scale_full = scale_full.transpose(1, 0, 2).reshape(n_pool, d)
    convw_full = _exact_f32(exact_all[:, scale_rows:], n_convw).reshape(N_DEV, SSD_CONV, -1)
    convw_full = convw_full.transpose(1, 0, 2).reshape(SSD_CONV, -1)

    d_inner = w_ssd_out.shape[0]
    n_zx = w_ssd_in_t.shape[0] - ssd_dt_bias.shape[1]
    w_zx_t = w_ssd_in_t[:n_zx]
    n_ssd_heads = ssd_dt_bias.shape[1]
    n_ssd_groups = n_ssd_heads // SSD_HEADS_PER_GROUP
    w_dt_t = _ssd_group_pad(w_ssd_in_t[n_zx:].T, n_ssd_groups).T
    par = _pad_rows(_ssd_group_pad(jnp.concatenate([ssd_dt_bias, ssd_a_log, ssd_d], axis=0), n_ssd_groups), SUBLANES)
    d_rep = jnp.repeat(ssd_d[0], SSD_HEAD_DIM)[None]
    qk_gains = jnp.zeros((SUBLANES, LANES), F32).at[0].set(jnp.tile(sb_q_norm[0], 2)).at[1].set(jnp.tile(sb_k_norm[0], 2))

    act = x[0]
    saved = []
    for i in range(depth):
        kind, j = i % 3, i // 3
        gain = mix_norm[i:i + 1]
        if kind == 0:
            act, s = _pool_mixer_fwd(act, gain, w_pool_in[j], w_pool_group[j], scale_full[j:j + 1], f"l{i}")
        elif kind == 1:
            act, s = _ssd_mixer_fwd(act, gain, w_zx_t, w_dt_t, convw_full, ssd_conv_b, par, d_rep, ssd_out_norm,
                                    w_ssd_out, f"l{i}")
        else:
            act, s = _sb_mixer_fwd(act, gain, w_sb_qkv_t, qk_gains, w_sb_out, f"l{i}")
        act, f = _ffn_fwd(act, ffn_norm[i:i + 1], w_gate_t[i], w_up_t[i], w_down[i], f"l{i}")
        saved.append((s, f))
    dact, loss_cols = _loss_head(act, loss_target[0], name="loss_head")

    def layer_row(name, layer):
        return seg[name][0] + layer * (seg[name][1] // weights[name].shape[0])

    grads = jnp.zeros((N_DEV, packed_rows, d), F32)
    g_mix_norm, g_ffn_norm = [None] * depth, [None] * depth
    g_pool_group, g_pool_scale = [None] * n_pool, [None] * n_pool
    for i in reversed(range(depth)):
        kind, j = i % 3, i // 3
        gain = mix_norm[i:i + 1]
        s, f = saved[i]
        dact, grads, g_ffn_norm[i] = _ffn_bwd(
            dact, f, ffn_norm[i:i + 1], w_gate_t[i], w_up_t[i], w_down[i], grads,
            [layer_row(n, i) for n in ("ffn_gate", "ffn_up", "ffn_down")], f"l{i}")
        if kind == 0:
            dact, grads, g_pool_group[j], g_pool_scale[j], g_mix_norm[i] = _pool_mixer_bwd(
                dact, s, gain, w_pool_in[j], w_pool_group[j], scale_full[j:j + 1], grads, layer_row("pool_in", j),
                f"l{i}")
        elif kind == 1:
            (dact, grads, g_zx_t, g_dt_t, g_conv_w, g_conv_b, g_par, g_out_norm,
             g_mix_norm[i]) = _ssd_mixer_bwd(dact, s, gain, w_zx_t, w_dt_t, convw_full, ssd_conv_b, par, d_rep,
                                             ssd_out_norm, w_ssd_out, grads, layer_row("ssd_out", j), f"l{i}")
        else:
            dact, grads, g_qk_gains, g_mix_norm[i] = _sb_mixer_bwd(
                dact, s, gain, w_sb_qkv_t, qk_gains, w_sb_out, grads, layer_row("sb_qkv", j),
                layer_row("sb_out", j), f"l{i}")
    grad_x = dact[None]

    g_ssd_in = jnp.concatenate([g_zx_t, _ssd_group_unpad(g_dt_t.T, n_ssd_groups).T], axis=0)
    g_group = jnp.concatenate([_device_blocks(gg[k], d) for gg in g_pool_group for k in range(gg.shape[0])], axis=1)
    for name, blocks in (("ssd_in", _device_blocks(g_ssd_in, d)), ("pool_group", g_group)):
        blocks = blocks.reshape(N_DEV // 2, 2, -1, d).swapaxes(0, 1).reshape(N_DEV, -1, d)
        grads = lax.dynamic_update_slice(grads, blocks, (0, seg[name][0], 0))
    parts = grads.reshape(2, N_DEV // 2, packed_rows, d)
    from_sibling = _exchange_sibling(parts, name="reduce_sibling")
    chip_sums, chip_sums_wire = _add_pairs(parts, from_sibling, cs.reshape(1).astype(jnp.int32),
                                           name="reduce_sibling_add")
    from_chips = _exchange_chips(chip_sums_wire, name="reduce_chips")

    def pack_f32(tree):
        return _pad_rows(pack(tree, F32), packed_rows)

    big_out = _adamw_sharded(chip_sums, from_chips, chip.reshape(1).astype(jnp.int32), pack_f32(weights),
                             pack_f32(mom1), pack_f32(mom2), name="adamw_sharded")

    def small_pack(mix, ffn, conv_b, out_norm, scale, conv_w, vec, qk, loss=None):
        buf = jnp.zeros((SMALL_ROWS, d), F32)
        buf = buf.at[ROW_MIX_NORM:ROW_MIX_NORM + depth].set(mix).at[ROW_FFN_NORM:ROW_FFN_NORM + depth].set(ffn)
        buf = buf.at[ROW_CONV_B:ROW_CONV_B + conv_b.size // d].set(conv_b.reshape(-1, d))
        buf = buf.at[ROW_OUT_NORM:ROW_OUT_NORM + out_norm.size // d].set(out_norm.reshape(-1, d))
        buf = buf.at[ROW_POOL_SCALE:ROW_POOL_SCALE + n_pool].set(scale)
        buf = buf.at[ROW_CONV_W:ROW_CONV_W + conv_w.size // d].set(conv_w.reshape(-1, d))
        buf = buf.at[ROW_SSD_VEC].set(vec.reshape(-1)).at[ROW_QK_GAIN].set(qk.reshape(-1))
        if loss is not None:
            buf = buf.at[ROW_LOSS].set(loss.reshape(-1))
        return buf

    def small_params(tree):
        scale = lax.dynamic_update_slice(jnp.zeros((n_pool, d), F32), tree["pool_scale"],
                                         (0, dev * tree["pool_scale"].shape[1]))
        conv_w = lax.dynamic_update_slice(jnp.zeros(convw_full.shape, F32), tree["ssd_conv_w"][0],
                                          (0, dev * tree["ssd_conv_w"].shape[2]))
        vec = jnp.zeros((SUBLANES, LANES), F32)
        vec = vec.at[0, :n_ssd_heads].set(tree["ssd_dt_bias"][0]).at[1, :n_ssd_heads].set(tree["ssd_a_log"][0])
        vec = vec.at[2, :n_ssd_heads].set(tree["ssd_d"][0])
        qk = jnp.zeros((SUBLANES, LANES), F32)
        qk = qk.at[0, SB_HEAD_DIM:].set(tree["sb_q_norm"][0]).at[1, SB_HEAD_DIM:].set(tree["sb_k_norm"][0])
        return small_pack(tree["mix_norm"], tree["ffn_norm"], tree["ssd_conv_b"], tree["ssd_out_norm"], scale,
                          conv_w, vec, qk)

    small_partial = small_pack(jnp.concatenate(g_mix_norm, axis=0), jnp.concatenate(g_ffn_norm, axis=0), g_conv_b,
                               g_out_norm, jnp.concatenate(g_pool_scale, axis=0), g_conv_w,
                               jnp.zeros((SUBLANES, LANES), F32).at[:3, :n_ssd_heads].set(
                                   _ssd_group_unpad(g_par[:3], n_ssd_groups)), g_qk_gains,
                               loss_cols)
    small_all = _all_gather(small_partial, name="gather_small")
    small_out = _adamw_small(small_all, small_params(weights), small_params(mom1), small_params(mom2),
                             name="adamw_small")
    loss = small_out[0][ROW_LOSS, 0]

    def unpack(big, small):
        out = {}
        for name in BIG_WEIGHTS:
            a, n = seg[name]
            out[name] = _from_rows(name, big[a:a + n], weights[name].shape)
        out["mix_norm"] = small[ROW_MIX_NORM:ROW_MIX_NORM + depth]
        out["ffn_norm"] = small[ROW_FFN_NORM:ROW_FFN_NORM + depth]
        out["ssd_conv_b"] = small[ROW_CONV_B:ROW_CONV_B + ssd_conv_b.size // d].reshape(ssd_conv_b.shape)
        out["ssd_out_norm"] = small[ROW_OUT_NORM:ROW_OUT_NORM + ssd_out_norm.size // d].reshape(ssd_out_norm.shape)
        out["pool_scale"] = lax.dynamic_slice(small[ROW_POOL_SCALE:ROW_POOL_SCALE + n_pool],
                                              (0, dev * pool_scale.shape[1]), pool_scale.shape)
        conv_w = small[ROW_CONV_W:ROW_CONV_W + convw_full.size // d].reshape(convw_full.shape)
        out["ssd_conv_w"] = lax.dynamic_slice(conv_w, (0, dev * ssd_conv_w.shape[2]), ssd_conv_w.shape[1:])[None]
        vec = small[ROW_SSD_VEC].reshape(SUBLANES, LANES)
        out["ssd_dt_bias"], out["ssd_a_log"], out["ssd_d"] = (vec[r:r + 1, :n_ssd_heads] for r in range(3))
        qk = small[ROW_QK_GAIN].reshape(SUBLANES, LANES)
        out["sb_q_norm"], out["sb_k_norm"] = qk[0:1, SB_HEAD_DIM:], qk[1:2, SB_HEAD_DIM:]
        return [out[n] for n in names]

    results = [unpack(b, s) for b, s in zip(big_out, small_out)]
    return (loss, grad_x, *results[0], *results[1], *results[2], *results[3])
```

```python
import math

import jax
import jax.numpy as jnp
from jax import lax
from jax.experimental import pallas as pl
from jax.experimental.pallas import tpu as pltpu

F32 = jnp.float32
BF16 = jnp.bfloat16

N_DEV = 8
NORM_EPS = 1e-6
V7X_VMEM_LIMIT_BYTES = 48 * 1024 * 1024
LANES = 128
SUBLANES = 8

POOL_WINDOWS = (2, 4, 8, 16)
SSD_CHUNK = 256
SSD_HEAD_DIM = 64
SSD_STATE = 128
SSD_HEADS_PER_GROUP = 4
SSD_CONV = 4
SB_HEAD_DIM = 64
SB_BLOCK = 256
SB_QUERY_BLOCK = 256

ADAM_LR = 0.001
ADAM_B1 = 0.9
ADAM_B2 = 0.999
ADAM_EPS = 1e-08
ADAM_WD = 0.01
ADAM_STEP = 10


def _params(*sem):
    return pltpu.CompilerParams(dimension_semantics=sem, vmem_limit_bytes=V7X_VMEM_LIMIT_BYTES)


def _tile(n, cap, mult):
    best = None
    for t in range(mult, min(n, cap) + 1, mult):
        if n % t == 0:
            best = t
    return best or n


def _load_slabs(ref, slabs):
    if not slabs:
        return ref[...]
    return jnp.concatenate([ref[p] for p in range(ref.shape[0])], axis=1)


def _matmul(a, b, mode, *, name, out_dtype=F32, resid=None, a_slabs=False, out_slabs=False,
            tm_cap=1024, tn_cap=1024, tk_cap=2048):
    pairs = list(zip(a, b)) if isinstance(a, (list, tuple)) else [(a, b)]
    a, b = pairs[0]
    if a_slabs:
        m, k = a.shape[1], a.shape[0] * LANES
    else:
        m, k = a.shape
    n = b.shape[1] if mode == "nn" else b.shape[0]
    assert (b.shape[0] if mode == "nn" else b.shape[1]) == k
    assert all(pa.shape == a.shape and pb.shape == b.shape for pa, pb in pairs)
    tm, tn, tk = _tile(m, tm_cap, SUBLANES), _tile(n, tn_cap, LANES), _tile(k, tk_cap, LANES)
    nk = k // tk
    dn = (((1,), (0,)), ((), ())) if mode == "nn" else (((1,), (1,)), ((), ()))
    has_resid = resid is not None
    n_pairs = len(pairs)

    def body(*refs):
        ab_refs, rest = refs[:2 * n_pairs], refs[2 * n_pairs:]
        r_ref = rest[0] if has_resid else None
        o_ref = rest[1] if has_resid else rest[0]
        kk = pl.program_id(2)

        def partial():
            total = None
            for p in range(n_pairs):
                d = lax.dot_general(_load_slabs(ab_refs[2 * p], a_slabs).astype(BF16),
                                    ab_refs[2 * p + 1][...].astype(BF16), dn, preferred_element_type=F32)
                total = d if total is None else total + d
            return total

        def finish(r):
            if has_resid:
                r = r + r_ref[...]
            if out_slabs:
                for p in range(tn // LANES):
                    o_ref[p] = r[:, p * LANES:(p + 1) * LANES].astype(out_dtype)
            else:
                o_ref[...] = r.astype(out_dtype)

        if nk == 1:
            finish(partial())
        else:
            acc = rest[-1]

            @pl.when(kk == 0)
            def _():
                acc[...] = jnp.zeros_like(acc)

            acc[...] += partial()

            @pl.when(kk == nk - 1)
            def _():
                finish(acc[...])

    b_spec = (pl.BlockSpec((tk, tn), lambda i, j, kk: (kk, j)) if mode == "nn"
              else pl.BlockSpec((tn, tk), lambda i, j, kk: (j, kk)))
    a_spec = (pl.BlockSpec((tk // LANES, tm, LANES), lambda i, j, kk: (kk, i, 0)) if a_slabs
              else pl.BlockSpec((tm, tk), lambda i, j, kk: (i, kk)))
    in_specs = [a_spec, b_spec] * n_pairs
    args = [t for pair in pairs for t in pair]
    if has_resid:
        in_specs.append(pl.BlockSpec((tm, tn), lambda i, j, kk: (i, j)))
        args.append(resid)
    if out_slabs:
        out_spec = pl.BlockSpec((tn // LANES, tm, LANES), lambda i, j, kk: (j, i, 0))
        out_shape = jax.ShapeDtypeStruct((n // LANES, m, LANES), out_dtype)
    else:
        out_spec = pl.BlockSpec((tm, tn), lambda i, j, kk: (i, j))
        out_shape = jax.ShapeDtypeStruct((m, n), out_dtype)
    return pl.pallas_call(
        body, name=name, grid=(m // tm, n // tn, nk),
        in_specs=in_specs, out_specs=out_spec, out_shape=out_shape,
        scratch_shapes=[pltpu.VMEM((tm, tn), F32)] if nk > 1 else [],
        compiler_params=_params("parallel", "parallel", "arbitrary"),
    )(*args)


def _matmul_tn(a, b, *, name, a_slabs=False, ta_cap=1024, tb_cap=1024, tr_cap=512):
    if a_slabs:
        r, ka = a.shape[1], a.shape[0] * LANES
    else:
        r, ka = a.shape
    nb = b.shape[1]
    assert b.shape[0] == r
    ta, tb, tr = _tile(ka, ta_cap, LANES), _tile(nb, tb_cap, LANES), _tile(r, tr_cap, SUBLANES)

    def body(a_ref, b_ref, o_ref):
        @pl.when(pl.program_id(2) == 0)
        def _():
            o_ref[...] = jnp.zeros_like(o_ref)

        o_ref[...] += lax.dot_general(_load_slabs(a_ref, a_slabs).astype(BF16), b_ref[...].astype(BF16),
                                      (((0,), (0,)), ((), ())), preferred_element_type=F32)

    a_spec = (pl.BlockSpec((ta // LANES, tr, LANES), lambda i, j, kk: (i, kk, 0)) if a_slabs
              else pl.BlockSpec((tr, ta), lambda i, j, kk: (kk, i)))
    return pl.pallas_call(
        body, name=name, grid=(ka // ta, nb // tb, r // tr),
        in_specs=[a_spec, pl.BlockSpec((tr, tb), lambda i, j, kk: (kk, j))],
        out_specs=pl.BlockSpec((ta, tb), lambda i, j, kk: (i, j)),
        out_shape=jax.ShapeDtypeStruct((ka, nb), F32),
        compiler_params=_params("parallel", "parallel", "arbitrary"),
    )(a, b)


def _core_major(k):
    return (k % 2) * (N_DEV // 2) + k // 2


def _matmul_tn_into(buf, a, b, row_off, *, name, a_slabs=False, tr_cap=1024):
    if a_slabs:
        r, ka = a.shape[1], a.shape[0] * LANES
    else:
        r, ka = a.shape
    n_dev, _, n = buf.shape
    per = ka // n_dev
    assert b.shape == (r, n) and ka % n_dev == 0 and per % SUBLANES == 0 and row_off % per == 0
    tr = _tile(r, tr_cap, SUBLANES)

    def body(buf_ref, a_ref, b_ref, o_ref):
        prod = lax.dot_general(_load_slabs(a_ref, a_slabs).astype(BF16), b_ref[...].astype(BF16),
                               (((0,), (0,)), ((), ())), preferred_element_type=F32)
        @pl.when(pl.program_id(0) == 0)
        def _():
            for k in range(n_dev):
                o_ref[_core_major(k)] = prod[k * per:(k + 1) * per]

        @pl.when(pl.program_id(0) > 0)
        def _():
            for k in range(n_dev):
                o_ref[_core_major(k)] += prod[k * per:(k + 1) * per]

    a_spec = (pl.BlockSpec((ka // LANES, tr, LANES), lambda i: (0, i, 0)) if a_slabs
              else pl.BlockSpec((tr, ka), lambda i: (i, 0)))
    return pl.pallas_call(
        body, name=name, grid=(r // tr,),
        in_specs=[pl.BlockSpec(memory_space=pl.ANY), a_spec, pl.BlockSpec((tr, n), lambda i: (i, 0))],
        out_specs=pl.BlockSpec((n_dev, per, n), lambda i: (0, row_off // per, 0)),
        out_shape=jax.ShapeDtypeStruct(buf.shape, F32),
        input_output_aliases={0: 0},
        compiler_params=_params("arbitrary"),
    )(buf, a, b)


def _rms_fwd(x, gain, *, name):
    t, d = x.shape
    tm = _tile(t, 512, SUBLANES)

    def body(x_ref, g_ref, o_ref):
        xv = x_ref[...]
        r = lax.rsqrt(jnp.mean(xv * xv, axis=-1, keepdims=True) + NORM_EPS)
        o_ref[...] = (xv * r * g_ref[...]).astype(BF16)

    return pl.pallas_call(
        body, name=name, grid=(t // tm,),
        in_specs=[pl.BlockSpec((tm, d), lambda i: (i, 0)), pl.BlockSpec((1, d), lambda i: (0, 0))],
        out_specs=pl.BlockSpec((tm, d), lambda i: (i, 0)),
        out_shape=jax.ShapeDtypeStruct((t, d), BF16),
        compiler_params=_params("parallel"),
    )(x, gain)


def _rms_bwd(x, gain, dh, dres, *, name):
    t, d = x.shape
    tm = _tile(t, 512, SUBLANES)

    def body(x_ref, g_ref, dh_ref, dres_ref, dx_ref, dg_ref):
        @pl.when(pl.program_id(0) == 0)
        def _():
            dg_ref[...] = jnp.zeros_like(dg_ref)

        xv = x_ref[...]
        r = lax.rsqrt(jnp.mean(xv * xv, axis=-1, keepdims=True) + NORM_EPS)
        xhat = xv * r
        dhv = dh_ref[...]
        u = dhv * g_ref[...]
        dx_ref[...] = dres_ref[...] + r * (u - xhat * jnp.mean(u * xhat, axis=-1, keepdims=True))
        dg_ref[...] += jnp.sum(dhv * xhat, axis=0, keepdims=True)

    return pl.pallas_call(
        body, name=name, grid=(t // tm,),
        in_specs=[pl.BlockSpec((tm, d), lambda i: (i, 0)), pl.BlockSpec((1, d), lambda i: (0, 0)),
                  pl.BlockSpec((tm, d), lambda i: (i, 0)), pl.BlockSpec((tm, d), lambda i: (i, 0))],
        out_specs=[pl.BlockSpec((tm, d), lambda i: (i, 0)), pl.BlockSpec((1, d), lambda i: (0, 0))],
        out_shape=[jax.ShapeDtypeStruct((t, d), F32), jax.ShapeDtypeStruct((1, d), F32)],
        compiler_params=_params("arbitrary"),
    )(x, gain, dh, dres)


def _loss_head(y, target, *, name):
    t, d = y.shape
    tm = _tile(t, 512, SUBLANES)

    def body(y_ref, t_ref, dy_ref, l_ref):
        @pl.when(pl.program_id(0) == 0)
        def _():
            l_ref[...] = jnp.zeros_like(l_ref)

        e = y_ref[...] - t_ref[...]
        dy_ref[...] = e * (1.0 / d)
        l_ref[...] += jnp.sum(e * e, axis=0, keepdims=True) * (0.5 / d)

    return pl.pallas_call(
        body, name=name, grid=(t // tm,),
        in_specs=[pl.BlockSpec((tm, d), lambda i: (i, 0)), pl.BlockSpec((tm, d), lambda i: (i, 0))],
        out_specs=[pl.BlockSpec((tm, d), lambda i: (i, 0)), pl.BlockSpec((1, d), lambda i: (0, 0))],
        out_shape=[jax.ShapeDtypeStruct((t, d), F32), jax.ShapeDtypeStruct((1, d), F32)],
        compiler_params=_params("arbitrary"),
    )(y, target)


def _sigmoid(v):
    return 0.5 * jnp.tanh(0.5 * v) + 0.5


FFN_TOKEN_TILE = 512
FFN_HIDDEN_TILE = 1408
NT_DIMS = (((1,), (1,)), ((), ()))


def _ffn_up(h, w_gate_t, w_up_t, *, name):
    t, d = h.shape
    f = w_gate_t.shape[0]
    tm, tn = _tile(t, FFN_TOKEN_TILE, SUBLANES), _tile(f, FFN_HIDDEN_TILE, LANES)

    def body(h_ref, g_ref, u_ref, s_ref, a_ref, b_ref):
        hv = h_ref[...].astype(BF16)
        av = lax.dot_general(hv, g_ref[...].astype(BF16), NT_DIMS, preferred_element_type=F32)
        bv = lax.dot_general(hv, u_ref[...].astype(BF16), NT_DIMS, preferred_element_type=F32)
        s_ref[...] = (av * _sigmoid(av) * bv).astype(BF16)
        a_ref[...] = av.astype(BF16)
        b_ref[...] = bv.astype(BF16)

    w_spec = pl.BlockSpec((tn, d), lambda j, i: (j, 0))
    out_spec = pl.BlockSpec((tm, tn), lambda j, i: (i, j))
    out = jax.ShapeDtypeStruct((t, f), BF16)
    return pl.pallas_call(
        body, name=name, grid=(f // tn, t // tm),
        in_specs=[pl.BlockSpec((tm, d), lambda j, i: (i, 0)), w_spec, w_spec],
        out_specs=[out_spec, out_spec, out_spec], out_shape=[out, out, out],
        compiler_params=_params("parallel", "parallel"),
    )(h, w_gate_t, w_up_t)


def _ffn_dact(dx, w_down, a, b, *, name):
    t, d = dx.shape
    f = w_down.shape[0]
    tm, tn = _tile(t, FFN_TOKEN_TILE, SUBLANES), _tile(f, FFN_HIDDEN_TILE, LANES)

    def body(dx_ref, w_ref, a_ref, b_ref, da_ref, db_ref):
        ds = lax.dot_general(dx_ref[...].astype(BF16), w_ref[...].astype(BF16), NT_DIMS, preferred_element_type=F32)
        av = a_ref[...].astype(F32)
        sg = _sigmoid(av)
        da_ref[...] = (ds * b_ref[...].astype(F32) * (sg * (1.0 + av * (1.0 - sg)))).astype(BF16)
        db_ref[...] = (ds * av * sg).astype(BF16)

    blk = pl.BlockSpec((tm, tn), lambda j, i: (i, j))
    out = jax.ShapeDtypeStruct((t, f), BF16)
    return pl.pallas_call(
        body, name=name, grid=(f // tn, t // tm),
        in_specs=[pl.BlockSpec((tm, d), lambda j, i: (i, 0)), pl.BlockSpec((tn, d), lambda j, i: (j, 0)), blk, blk],
        out_specs=[blk, blk], out_shape=[out, out],
        compiler_params=_params("parallel", "parallel"),
    )(dx, w_down, a, b)


def _ffn_fwd(x, gain, w_gate_t, w_up_t, w_down, tag):
    h = _rms_fwd(x, gain, name=f"ffn_norm_{tag}")
    s, a, b = _ffn_up(h, w_gate_t, w_up_t, name=f"ffn_up_{tag}")
    x_new = _matmul(s, w_down, "nn", resid=x, tn_cap=1024, tk_cap=2816, name=f"ffn_down_{tag}")
    return x_new, (x, h, a, b, s)


def _ffn_bwd(dx, saved, gain, w_gate_t, w_up_t, w_down, grads, rows, tag):
    x, h, a, b, s = saved
    da, db = _ffn_dact(dx, w_down, a, b, name=f"ffn_dact_{tag}")
    grads = _matmul_tn_into(grads, da, h, rows[0], name=f"ffn_dwgate_{tag}")
    grads = _matmul_tn_into(grads, db, h, rows[1], name=f"ffn_dwup_{tag}")
    grads = _matmul_tn_into(grads, s, dx, rows[2], name=f"ffn_dwdown_{tag}")
    dh = _matmul([da, db], [w_gate_t, w_up_t], "nn", tm_cap=512, tn_cap=1024, tk_cap=2816, name=f"ffn_dh_{tag}")
    dx_in, dgain = _rms_bwd(x, gain, dh, dx, name=f"ffn_dnorm_{tag}")
    return dx_in, grads, dgain


POOL_HALO = 16


def _shift_rows(v, k):
    n = v.shape[0]
    return pltpu.roll(v, k % n, 0)


def _window_sum(v, w, direction):
    k = 1
    while k < w:
        v = v + _shift_rows(v, direction * k)
        k *= 2
    return v


def _pool_fwd(u, x, w_group, scale, *, name):
    t, d = u.shape
    ng, dg = w_group.shape[0], w_group.shape[1]
    tm = _tile(t, 512, POOL_HALO)
    hb = tm // POOL_HALO

    def body(u_ref, halo_ref, x_ref, w_ref, s_ref, xo_ref, p_ref, y_ref):
        i, g = pl.program_id(0), pl.program_id(1)
        halo = jnp.where(i > 0, halo_ref[...], 0.0)
        ext = jnp.concatenate([halo, u_ref[...]], axis=0)
        pos = i * tm + lax.broadcasted_iota(jnp.int32, (tm, 1), 0)
        for gi, win in enumerate(POOL_WINDOWS):
            @pl.when(g == gi)
            def _(win=win):
                tot = _window_sum(ext, win, 1)[POOL_HALO:]
                cnt = jnp.minimum(pos + 1, win).astype(F32)
                p = (tot / cnt - u_ref[...]).astype(BF16)
                p_ref[...] = p
                y = jnp.dot(p, w_ref[...].astype(BF16), preferred_element_type=F32)
                y_ref[...] = y
                xo_ref[...] = x_ref[...] + y * s_ref[...]

    blk = pl.BlockSpec((tm, dg), lambda i, g: (i, g))
    return pl.pallas_call(
        body, name=name, grid=(t // tm, ng),
        in_specs=[blk, pl.BlockSpec((POOL_HALO, dg), lambda i, g: (jnp.maximum(i * hb - 1, 0), g)), blk,
                  pl.BlockSpec((None, dg, dg), lambda i, g: (g, 0, 0)), pl.BlockSpec((1, dg), lambda i, g: (0, g))],
        out_specs=[blk, blk, blk],
        out_shape=[jax.ShapeDtypeStruct((t, d), F32), jax.ShapeDtypeStruct((t, d), BF16),
                   jax.ShapeDtypeStruct((t, d), F32)],
        compiler_params=_params("parallel", "parallel"),
    )(u, u, x, w_group, scale)


def _pool_bwd(dx, p, y_pre, w_group, scale, *, name):
    t, d = dx.shape
    ng, dg = w_group.shape[0], w_group.shape[1]
    tm = _tile(t, 512, POOL_HALO)
    hb = tm // POOL_HALO
    nt = t // tm

    def body(dx_ref, nxt_ref, p_ref, y_ref, w_ref, s_ref, du_ref, dw_ref, ds_ref):
        g, i = pl.program_id(0), pl.program_id(1)

        @pl.when(i == 0)
        def _():
            dw_ref[...] = jnp.zeros_like(dw_ref)
            ds_ref[...] = jnp.zeros_like(ds_ref)

        dxv = dx_ref[...]
        ds_ref[...] += jnp.sum(dxv * y_ref[...], axis=0, keepdims=True)
        nxt = jnp.where(i < nt - 1, nxt_ref[...], 0.0)
        dyp = (jnp.concatenate([dxv, nxt], axis=0) * s_ref[...]).astype(BF16)
        dw_ref[...] += lax.dot_general(p_ref[...], dyp[:tm], (((0,), (0,)), ((), ())), preferred_element_type=F32)
        dp = lax.dot_general(dyp, w_ref[...].astype(BF16), (((1,), (1,)), ((), ())), preferred_element_type=F32)
        pos = i * tm + lax.broadcasted_iota(jnp.int32, (tm + POOL_HALO, 1), 0)
        for gi, win in enumerate(POOL_WINDOWS):
            @pl.when(g == gi)
            def _(win=win):
                q = dp / jnp.minimum(pos + 1, win).astype(F32)
                du_ref[...] = (_window_sum(q, win, -1)[:tm] - dp[:tm]).astype(BF16)

    blk = pl.BlockSpec((tm, dg), lambda g, i: (i, g))
    return pl.pallas_call(
        body, name=name, grid=(ng, nt),
        in_specs=[blk, pl.BlockSpec((POOL_HALO, dg), lambda g, i: (jnp.minimum((i + 1) * hb, t // POOL_HALO - 1), g)),
                  blk, blk, pl.BlockSpec((None, dg, dg), lambda g, i: (g, 0, 0)),
                  pl.BlockSpec((1, dg), lambda g, i: (0, g))],
        out_specs=[blk, pl.BlockSpec((None, dg, dg), lambda g, i: (g, 0, 0)), pl.BlockSpec((1, dg), lambda g, i: (0, g))],
        out_shape=[jax.ShapeDtypeStruct((t, d), BF16), jax.ShapeDtypeStruct((ng, dg, dg), F32),
                   jax.ShapeDtypeStruct((1, d), F32)],
        compiler_params=_params("parallel", "arbitrary"),
    )(dx, dx, p, y_pre, w_group, scale)


def _pool_mixer_fwd(x, gain, w_in, w_group, scale, tag):
    h = _rms_fwd(x, gain, name=f"pool_norm_{tag}")
    u = _matmul(h, w_in, "nn", name=f"pool_in_{tag}")
    x_new, p, y_pre = _pool_fwd(u, x, w_group, scale, name=f"pool_mix_{tag}")
    return x_new, (x, h, p, y_pre)


def _pool_mixer_bwd(dx, saved, gain, w_in, w_group, scale, grads, row_in, tag):
    x, h, p, y_pre = saved
    du, dw_group, dscale = _pool_bwd(dx, p, y_pre, w_group, scale, name=f"pool_dmix_{tag}")
    grads = _matmul_tn_into(grads, h, du, row_in, name=f"pool_dwin_{tag}")
    dh = _matmul(du, w_in, "nt", name=f"pool_dh_{tag}")
    dx_in, dgain = _rms_bwd(x, gain, dh, dx, name=f"pool_dnorm_{tag}")
    return dx_in, grads, dw_group, dscale, dgain


CONV_HALO = 8
HIGHEST = lax.Precision.HIGHEST
NEG_BIG = -1e30


def _softplus(v):
    return jnp.maximum(v, 0.0) + jnp.log(1.0 + jnp.exp(-jnp.abs(v)))


def _dot_exact(a, b):
    return jnp.dot(a, b, precision=HIGHEST, preferred_element_type=F32)


def _conv_taps(ext, w_ref, off, rows):
    acc = None
    for k in range(SSD_CONV):
        shift = SSD_CONV - 1 - k
        v = (_shift_rows(ext, shift) if shift else ext)[off:off + rows] * w_ref[k:k + 1, :]
        acc = v if acc is None else acc + v
    return acc


def _ssd_conv_fwd(zx, conv_w, conv_b, col0, *, name):
    t = zx.shape[0]
    c = conv_w.shape[1]
    tm, tc = _tile(t, 512, CONV_HALO), _tile(c, 512, LANES)
    hb, cb0 = tm // CONV_HALO, col0 // tc
    assert col0 % tc == 0

    def body(x_ref, halo_ref, w_ref, b_ref, o_ref):
        halo = jnp.where(pl.program_id(0) > 0, halo_ref[...], 0.0)
        ext = jnp.concatenate([halo, x_ref[...]], axis=0)
        pre = _conv_taps(ext, w_ref, CONV_HALO, tm) + b_ref[...]
        o_ref[...] = pre * _sigmoid(pre)

    return pl.pallas_call(
        body, name=name, grid=(t // tm, c // tc),
        in_specs=[pl.BlockSpec((tm, tc), lambda i, j: (i, j + cb0)),
                  pl.BlockSpec((CONV_HALO, tc), lambda i, j: (jnp.maximum(i * hb - 1, 0), j + cb0)),
                  pl.BlockSpec((SSD_CONV, tc), lambda i, j: (0, j)), pl.BlockSpec((1, tc), lambda i, j: (0, j))],
        out_specs=pl.BlockSpec((tm, tc), lambda i, j: (i, j)),
        out_shape=jax.ShapeDtypeStruct((t, c), F32),
        compiler_params=_params("parallel", "parallel"),
    )(zx, zx, conv_w, conv_b)


def _ssd_conv_bwd(d_parts, zx, conv_w, conv_b, col0, *, name):
    t = zx.shape[0]
    c = conv_w.shape[1]
    tm, tc = _tile(t, 512, CONV_HALO), _tile(c, 512, LANES)
    hb, cb0, nt = tm // CONV_HALO, col0 // tc, t // tm
    last_halo = t // CONV_HALO - 1
    starts = [0]
    for part in d_parts:
        assert part.shape[1] % tc == 0
        starts.append(starts[-1] + part.shape[1] // tc)
    assert starts[-1] == c // tc
    n_parts = len(d_parts)

    def pick(refs, j):
        value = refs[-1][...]
        for p in reversed(range(n_parts - 1)):
            value = jnp.where(j < starts[p + 1], refs[p][...], value)
        return value

    def body(x_ref, prev_ref, nxt_ref, *rest):
        d_refs, dnxt_refs = rest[:n_parts], rest[n_parts:2 * n_parts]
        w_ref, b_ref, dx_ref, dw_ref, db_ref = rest[2 * n_parts:]
        j, i = pl.program_id(0), pl.program_id(1)

        @pl.when(i == 0)
        def _():
            dw_ref[...] = jnp.zeros_like(dw_ref)
            db_ref[...] = jnp.zeros_like(db_ref)

        prev = jnp.where(i > 0, prev_ref[...], 0.0)
        has_next = i < nt - 1
        ext = jnp.concatenate([prev, x_ref[...], jnp.where(has_next, nxt_ref[...], 0.0)], axis=0)
        pre = _conv_taps(ext, w_ref, CONV_HALO, tm + CONV_HALO) + b_ref[...]
        sg = _sigmoid(pre)
        dact = jnp.concatenate([pick(d_refs, j), jnp.where(has_next, pick(dnxt_refs, j), 0.0)], axis=0)
        dpre = dact * (sg * (1.0 + pre * (1.0 - sg)))
        db_ref[...] += jnp.sum(dpre[:tm], axis=0, keepdims=True)
        acc = None
        for k in range(SSD_CONV):
            shift = SSD_CONV - 1 - k
            src = (_shift_rows(ext, shift) if shift else ext)[CONV_HALO:CONV_HALO + tm]
            dw_ref[k:k + 1, :] += jnp.sum(dpre[:tm] * src, axis=0, keepdims=True)
            v = (_shift_rows(dpre, -shift) if shift else dpre)[:tm] * w_ref[k:k + 1, :]
            acc = v if acc is None else acc + v
        dx_ref[...] = acc.astype(BF16)

    def part_specs(rows, row_index):
        def spec(p):
            def index(j, i):
                mine = (j >= starts[p]) & (j < starts[p + 1])
                return jnp.where(mine, row_index(i), 0), jnp.where(mine, j - starts[p], 0)
            return pl.BlockSpec((rows, tc), index)
        return [spec(p) for p in range(n_parts)]

    main = lambda j, i: (i, j + cb0)
    next_halo = lambda i: jnp.minimum((i + 1) * hb, last_halo)
    return pl.pallas_call(
        body, name=name, grid=(c // tc, nt),
        in_specs=[pl.BlockSpec((tm, tc), main),
                  pl.BlockSpec((CONV_HALO, tc), lambda j, i: (jnp.maximum(i * hb - 1, 0), j + cb0)),
                  pl.BlockSpec((CONV_HALO, tc), lambda j, i: (next_halo(i), j + cb0)),
                  *part_specs(tm, lambda i: i), *part_specs(CONV_HALO, next_halo),
                  pl.BlockSpec((SSD_CONV, tc), lambda j, i: (0, j)), pl.BlockSpec((1, tc), lambda j, i: (0, j))],
        out_specs=[pl.BlockSpec((tm, tc), lambda j, i: (i, j)), pl.BlockSpec((SSD_CONV, tc), lambda j, i: (0, j)),
                   pl.BlockSpec((1, tc), lambda j, i: (0, j))],
        out_shape=[jax.ShapeDtypeStruct((t, c), BF16), jax.ShapeDtypeStruct((SSD_CONV, c), F32),
                   jax.ShapeDtypeStruct((1, c), F32)],
        compiler_params=_params("parallel", "arbitrary"),
    )(zx, zx, zx, *d_parts, *d_parts, conv_w, conv_b)


SSD_CUMSUM_PIECES = 3
SSD_GROUPS_PER_STEP = 1


def _ssd_group_pad(v, n_groups):
    lead = v.shape[:-1]
    v = v.reshape(*lead, n_groups, SSD_HEADS_PER_GROUP)
    v = jnp.pad(v, [(0, 0)] * (len(lead) + 1) + [(0, LANES - SSD_HEADS_PER_GROUP)])
    return v.reshape(*lead, n_groups * LANES)


def _ssd_group_unpad(v, n_groups):
    lead = v.shape[:-1]
    return v.reshape(*lead, n_groups, LANES)[..., :SSD_HEADS_PER_GROUP].reshape(*lead, -1)


def _ssd_chunk_common(dtp_ref, par_ref):
    ell = SSD_CHUNK
    dt = _softplus(dtp_ref[...] + par_ref[0:1, :])
    a = -jnp.exp(par_ref[1:2, :])
    row = lax.broadcasted_iota(jnp.int32, (ell, ell), 0)
    col = lax.broadcasted_iota(jnp.int32, (ell, ell), 1)
    acum = _split_dot(dt * a, (row >= col).astype(BF16), SSD_CUMSUM_PIECES, left=True)
    return dt, a, acum, acum.T, row, col


def _ssd_scan_fwd(xa, dtp, par, n_groups, *, name):
    t = xa.shape[0]
    ell, hd, hpg, ns, gps = SSD_CHUNK, SSD_HEAD_DIM, SSD_HEADS_PER_GROUP, SSD_STATE, SSD_GROUPS_PER_STEP
    gw = hpg * hd
    nc = t // ell
    b_blk0, c_blk0 = n_groups * gw // (ns * gps), (n_groups * gw // ns + n_groups) // gps

    def body(xs_ref, b_ref, c_ref, dtp_ref, par_ref, y_ref, sin_ref, st):
        @pl.when(pl.program_id(1) == 0)
        def _():
            st[...] = jnp.zeros_like(st)

        dt, _, acum, acum_t, row, col = _ssd_chunk_common(dtp_ref, par_ref)
        for gi in range(gps):
            bb = b_ref[:, gi * ns:(gi + 1) * ns].astype(BF16)
            cc = c_ref[:, gi * ns:(gi + 1) * ns].astype(BF16)
            cb = lax.dot_general(cc, bb, NT_DIMS, preferred_element_type=F32)
            s_all = st[gi]
            sin_ref[gi] = s_all
            c_s = lax.dot_general(cc, s_all.astype(BF16), NT_DIMS, preferred_element_type=F32)
            weighted, keep = [], []
            for hh in range(hpg):
                lanes = slice(gi * gw + hh * hd, gi * gw + (hh + 1) * hd)
                hl = gi * LANES + hh
                col_a, row_a = acum[:, hl:hl + 1], acum_t[hl:hl + 1, :]
                decay = jnp.exp(jnp.where(row >= col, col_a - row_a, NEG_BIG))
                xdt = xs_ref[:, lanes] * dt[:, hl:hl + 1]
                y = jnp.dot((cb * decay).astype(BF16), xdt.astype(BF16), preferred_element_type=F32)
                y_ref[:, lanes] = y + jnp.exp(col_a) * c_s[:, hh * hd:(hh + 1) * hd]
                a_last = acum[ell - 1:ell, hl:hl + 1]
                weighted.append((xdt * jnp.exp(a_last - col_a)).astype(BF16))
                keep.append(jnp.broadcast_to(jnp.exp(a_last), (hd, 1)))
            st[gi] = jnp.concatenate(keep, axis=0) * s_all + lax.dot_general(
                jnp.concatenate(weighted, axis=1), bb, (((0,), (0,)), ((), ())), preferred_element_type=F32)

    return pl.pallas_call(
        body, name=name, grid=(n_groups // gps, nc),
        in_specs=[pl.BlockSpec((ell, gps * gw), lambda g, c: (c, g)),
                  pl.BlockSpec((ell, gps * ns), lambda g, c: (c, b_blk0 + g)),
                  pl.BlockSpec((ell, gps * ns), lambda g, c: (c, c_blk0 + g)),
                  pl.BlockSpec((ell, gps * LANES), lambda g, c: (c, g)),
                  pl.BlockSpec((SUBLANES, gps * LANES), lambda g, c: (0, g))],
        out_specs=[pl.BlockSpec((ell, gps * gw), lambda g, c: (c, g)),
                   pl.BlockSpec((None, gps, gw, ns), lambda g, c: (c, g, 0, 0))],
        out_shape=[jax.ShapeDtypeStruct((t, n_groups * gw), F32),
                   jax.ShapeDtypeStruct((nc, n_groups, gw, ns), F32)],
        scratch_shapes=[pltpu.VMEM((gps, gw, ns), F32)],
        compiler_params=_params("parallel", "arbitrary"),
    )(xa, xa, xa, dtp, par)


def _ssd_scan_bwd(dy, xa, dtp, par, s_in, n_groups, *, name):
    t = xa.shape[0]
    ell, hd, hpg, ns, gps = SSD_CHUNK, SSD_HEAD_DIM, SSD_HEADS_PER_GROUP, SSD_STATE, SSD_GROUPS_PER_STEP
    gw = hpg * hd
    nc = t // ell
    b_blk0, c_blk0 = n_groups * gw // (ns * gps), (n_groups * gw // ns + n_groups) // gps
    nt_dims = (((1,), (1,)), ((), ()))
    tn_dims = (((0,), (0,)), ((), ()))

    def body(dy_ref, xs_ref, b_ref, c_ref, dtp_ref, par_ref, sin_ref,
             dxs_ref, db_ref, dc_ref, ddtp_ref, dpar_ref, dst):
        @pl.when(pl.program_id(1) == 0)
        def _():
            dst[...] = jnp.zeros_like(dst)
            dpar_ref[...] = jnp.zeros_like(dpar_ref)

        dtg, a_g, acum, acum_t, row, col = _ssd_chunk_common(dtp_ref, par_ref)
        lane = lax.broadcasted_iota(jnp.int32, (1, gps * LANES), 1)
        dacum = jnp.zeros((ell, gps * LANES), F32)
        xsum = jnp.zeros((ell, gps * LANES), F32)
        dsum = jnp.zeros((1, gps * LANES), F32)
        for gi, hh in [(gi, hh) for gi in range(gps) for hh in range(hpg)]:
            if hh == 0:
                bb = b_ref[:, gi * ns:(gi + 1) * ns].astype(BF16)
                cc = c_ref[:, gi * ns:(gi + 1) * ns].astype(BF16)
                cb = lax.dot_general(cc, bb, nt_dims, preferred_element_type=F32)
                cb_t = lax.dot_general(bb, cc, nt_dims, preferred_element_type=F32)
                dcb = jnp.zeros((ell, ell), F32)
                dcb_t = jnp.zeros((ell, ell), F32)
                s_all, ds_all = sin_ref[gi], dst[gi]
                c_s_all = lax.dot_general(cc, s_all.astype(BF16), nt_dims, preferred_element_type=F32)
                b_ds_all = lax.dot_general(bb, ds_all.astype(BF16), nt_dims, preferred_element_type=F32)
                s_ds = jnp.sum(s_all * ds_all, axis=1, keepdims=True)
                dy_decayed, x_weighted, keep = [], [], []
            lanes = slice(gi * gw + hh * hd, gi * gw + (hh + 1) * hd)
            head = slice(hh * hd, (hh + 1) * hd)
            hl = gi * LANES + hh
            onehot = (lane == hl).astype(F32)
            col_a, row_a = acum[:, hl:hl + 1], acum_t[hl:hl + 1, :]
            decay = jnp.exp(jnp.where(row >= col, col_a - row_a, NEG_BIG))
            decay_t = jnp.exp(jnp.where(col >= row, row_a - col_a, NEG_BIG))
            e_col = jnp.exp(col_a)
            a_last = acum[ell - 1:ell, hl:hl + 1]
            w = jnp.exp(a_last - col_a)
            e_last = jnp.exp(a_last)
            xs_h, dy_h = xs_ref[:, lanes], dy_ref[:, lanes]
            dt_h = dtg[:, hl:hl + 1]
            xdt = xs_h * dt_h
            xdt_b, dy_b = xdt.astype(BF16), dy_h.astype(BF16)
            dm_decay = lax.dot_general(dy_b, xdt_b, nt_dims, preferred_element_type=F32) * decay
            dm_decay_t = lax.dot_general(xdt_b, dy_b, nt_dims, preferred_element_type=F32) * decay_t
            dcb += dm_decay
            dcb_t += dm_decay_t
            m_t = cb_t * decay_t
            dac = jnp.sum(dm_decay * cb, axis=1, keepdims=True) - jnp.sum(dm_decay_t * cb_t, axis=1, keepdims=True)
            b_ds = b_ds_all[:, head]
            dxdt = jnp.dot(m_t.astype(BF16), dy_b, preferred_element_type=F32) + w * b_ds
            dac += jnp.sum(dy_h * c_s_all[:, head], axis=1, keepdims=True) * e_col
            q = jnp.sum(xdt * b_ds, axis=1, keepdims=True) * w
            dac -= q
            d_last = jnp.sum(q, axis=0, keepdims=True) + e_last * jnp.sum(s_ds[head], axis=0, keepdims=True)
            is_last = lax.broadcasted_iota(jnp.int32, (ell, 1), 0) == ell - 1
            dac += jnp.where(is_last, d_last, 0.0)
            dacum += dac * onehot
            dy_decayed.append((dy_h * e_col).astype(BF16))
            x_weighted.append((xdt * w).astype(BF16))
            keep.append(jnp.broadcast_to(e_last, (hd, 1)))
            dxs_ref[:, lanes] = dxdt * dt_h + dy_h * par_ref[2:3, hl:hl + 1]
            xsum += jnp.sum(dxdt * xs_h, axis=1, keepdims=True) * onehot
            dsum += jnp.sum(jnp.sum(dy_h * xs_h, axis=1, keepdims=True), axis=0, keepdims=True) * onehot
            if hh == hpg - 1:
                group = slice(gi * ns, (gi + 1) * ns)
                dy_all, x_all = jnp.concatenate(dy_decayed, axis=1), jnp.concatenate(x_weighted, axis=1)
                dc_ref[:, group] = (jnp.dot(dy_all, s_all.astype(BF16), preferred_element_type=F32)
                                    + jnp.dot(dcb.astype(BF16), bb, preferred_element_type=F32))
                db_ref[:, group] = (jnp.dot(x_all, ds_all.astype(BF16), preferred_element_type=F32)
                                    + jnp.dot(dcb_t.astype(BF16), cc, preferred_element_type=F32))
                dst[gi] = jnp.concatenate(keep, axis=0) * ds_all + lax.dot_general(
                    dy_all, cc, tn_dims, preferred_element_type=F32)
        dda = _split_dot(dacum, (col >= row).astype(BF16), SSD_CUMSUM_PIECES, left=True)
        ddtp = (xsum + dda * a_g) * _sigmoid(dtp_ref[...] + par_ref[0:1, :])
        ddtp_ref[...] = ddtp
        dpar_ref[0:1, :] += jnp.sum(ddtp, axis=0, keepdims=True)
        dpar_ref[1:2, :] += jnp.sum(dda * dtg, axis=0, keepdims=True) * a_g
        dpar_ref[2:3, :] += dsum

    rev = lambda i: nc - 1 - i
    return pl.pallas_call(
        body, name=name, grid=(n_groups // gps, nc),
        in_specs=[pl.BlockSpec((ell, gps * gw), lambda g, i: (rev(i), g)),
                  pl.BlockSpec((ell, gps * gw), lambda g, i: (rev(i), g)),
                  pl.BlockSpec((ell, gps * ns), lambda g, i: (rev(i), b_blk0 + g)),
                  pl.BlockSpec((ell, gps * ns), lambda g, i: (rev(i), c_blk0 + g)),
                  pl.BlockSpec((ell, gps * LANES), lambda g, i: (rev(i), g)),
                  pl.BlockSpec((SUBLANES, gps * LANES), lambda g, i: (0, g)),
                  pl.BlockSpec((None, gps, gw, ns), lambda g, i: (rev(i), g, 0, 0))],
        out_specs=[pl.BlockSpec((ell, gps * gw), lambda g, i: (rev(i), g)),
                   pl.BlockSpec((ell, gps * ns), lambda g, i: (rev(i), g)),
                   pl.BlockSpec((ell, gps * ns), lambda g, i: (rev(i), g)),
                   pl.BlockSpec((ell, gps * LANES), lambda g, i: (rev(i), g)),
                   pl.BlockSpec((SUBLANES, gps * LANES), lambda g, i: (0, g))],
        out_shape=[jax.ShapeDtypeStruct((t, n_groups * gw), F32), jax.ShapeDtypeStruct((t, n_groups * ns), F32),
                   jax.ShapeDtypeStruct((t, n_groups * ns), F32), jax.ShapeDtypeStruct((t, n_groups * LANES), F32),
                   jax.ShapeDtypeStruct((SUBLANES, n_groups * LANES), F32)],
        scratch_shapes=[pltpu.VMEM((gps, gw, ns), F32)],
        compiler_params=_params("parallel", "arbitrary"),
    )(dy, xa, xa, xa, dtp, par, s_in)


def _ssd_gate_fwd(y, xa, zx, d_rep, out_norm, *, name):
    t, di = y.shape
    gw = SSD_HEADS_PER_GROUP * SSD_HEAD_DIM
    tm = _tile(t, 512, SUBLANES)

    def body(y_ref, xs_ref, z_ref, d_ref, n_ref, o_ref):
        zv = z_ref[...]
        gt = (y_ref[...] + d_ref[...] * xs_ref[...]) * (zv * _sigmoid(zv))
        r = lax.rsqrt(jnp.mean(gt * gt, axis=-1, keepdims=True) + NORM_EPS)
        o_ref[...] = (gt * r * n_ref[...]).astype(BF16)

    blk = pl.BlockSpec((tm, gw), lambda i, g: (i, g))
    vec = pl.BlockSpec((1, gw), lambda i, g: (0, g))
    return pl.pallas_call(
        body, name=name, grid=(t // tm, di // gw),
        in_specs=[blk, blk, blk, vec, vec], out_specs=blk,
        out_shape=jax.ShapeDtypeStruct((t, di), BF16),
        compiler_params=_params("parallel", "parallel"),
    )(y, xa, zx, d_rep, out_norm)


def _ssd_gate_bwd(dgn, y, xa, zx, d_rep, out_norm, *, name):
    t, di = y.shape
    gw = SSD_HEADS_PER_GROUP * SSD_HEAD_DIM
    tm = _tile(t, 512, SUBLANES)

    def body(dg_ref, y_ref, xs_ref, z_ref, d_ref, n_ref, dy_ref, dz_ref, dn_ref):
        @pl.when(pl.program_id(1) == 0)
        def _():
            dn_ref[...] = jnp.zeros_like(dn_ref)

        zv = z_ref[...]
        sg = _sigmoid(zv)
        sz = zv * sg
        y2 = y_ref[...] + d_ref[...] * xs_ref[...]
        gt = y2 * sz
        r = lax.rsqrt(jnp.mean(gt * gt, axis=-1, keepdims=True) + NORM_EPS)
        ghat = gt * r
        dgv = dg_ref[...]
        dn_ref[...] += jnp.sum(dgv * ghat, axis=0, keepdims=True)
        u = dgv * n_ref[...]
        dgt = r * (u - ghat * jnp.mean(u * ghat, axis=-1, keepdims=True))
        dy_ref[...] = dgt * sz
        dz_ref[...] = (dgt * y2 * (sg * (1.0 + zv * (1.0 - sg)))).astype(BF16)

    blk = pl.BlockSpec((tm, gw), lambda g, i: (i, g))
    vec = pl.BlockSpec((1, gw), lambda g, i: (0, g))
    return pl.pallas_call(
        body, name=name, grid=(di // gw, t // tm),
        in_specs=[blk, blk, blk, blk, vec, vec], out_specs=[blk, blk, vec],
        out_shape=[jax.ShapeDtypeStruct((t, di), F32), jax.ShapeDtypeStruct((t, di), BF16),
                   jax.ShapeDtypeStruct((1, di), F32)],
        compiler_params=_params("parallel", "arbitrary"),
    )(dgn, y, xa, zx, d_rep, out_norm)


def _ssd_mixer_fwd(x, gain, w_zx_t, w_dt_t, conv_w, conv_b, par, d_rep, out_norm, w_out, tag):
    di = w_out.shape[0]
    n_groups = di // (SSD_HEADS_PER_GROUP * SSD_HEAD_DIM)
    h = _rms_fwd(x, gain, name=f"ssd_norm_{tag}")
    zx = _matmul(h, w_zx_t, "nt", name=f"ssd_in_{tag}")
    dtp = _matmul(h, w_dt_t, "nt", name=f"ssd_dt_{tag}")
    xa = _ssd_conv_fwd(zx, conv_w, conv_b, di, name=f"ssd_conv_{tag}")
    y, s_in = _ssd_scan_fwd(xa, dtp, par, n_groups, name=f"ssd_scan_{tag}")
    gn = _ssd_gate_fwd(y, xa, zx, d_rep, out_norm, name=f"ssd_gate_{tag}")
    x_new = _matmul(gn, w_out, "nn", resid=x, name=f"ssd_out_{tag}")
    return x_new, (x, h, zx, dtp, xa, y, s_in, gn)


def _ssd_mixer_bwd(dx, saved, gain, w_zx_t, w_dt_t, conv_w, conv_b, par, d_rep, out_norm, w_out, grads, row_out,
                   tag):
    x, h, zx, dtp, xa, y, s_in, gn = saved
    di = w_out.shape[0]
    n_groups = di // (SSD_HEADS_PER_GROUP * SSD_HEAD_DIM)
    dgn = _matmul(dx, w_out, "nt", name=f"ssd_dgn_{tag}")
    grads = _matmul_tn_into(grads, gn, dx, row_out, name=f"ssd_dwout_{tag}")
    dy2, dz, dnorm = _ssd_gate_bwd(dgn, y, xa, zx, d_rep, out_norm, name=f"ssd_dgate_{tag}")
    dxs, db, dc, ddtp, dpar = _ssd_scan_bwd(dy2, xa, dtp, par, s_in, n_groups, name=f"ssd_dscan_{tag}")
    dxbc, dconv_w, dconv_b = _ssd_conv_bwd([dxs, db, dc], zx, conv_w, conv_b, di, name=f"ssd_dconv_{tag}")
    dzx = jnp.concatenate([dz, dxbc], axis=1)
    dw_zx_t = _matmul_tn(dzx, h, name=f"ssd_dwin_{tag}")
    dw_dt_t = _matmul_tn(ddtp, h, name=f"ssd_dwdt_{tag}")
    dh = _matmul(dzx, w_zx_t, "nn", name=f"ssd_dh_{tag}")
    dh = _matmul(ddtp, w_dt_t, "nn", resid=dh, name=f"ssd_dhdt_{tag}")
    dx_in, dgain = _rms_bwd(x, gain, dh, dx, name=f"ssd_dnorm_{tag}")
    return dx_in, grads, dw_zx_t, dw_dt_t, dconv_w, dconv_b, dpar, dnorm, dgain


HEAD_SUM_PIECES = 2


def _head_sums(v):
    row = lax.broadcasted_iota(jnp.int32, (LANES, LANES), 0)
    col = lax.broadcasted_iota(jnp.int32, (LANES, LANES), 1)
    same_head = (row // SB_HEAD_DIM == col // SB_HEAD_DIM).astype(BF16)
    return _split_dot(v, same_head, HEAD_SUM_PIECES)


def _sb_qk_norm_fwd(qkv, gains, *, name):
    ns, t, _ = qkv.shape
    per = ns // 3
    tm = _tile(t, 1024, SUBLANES)
    inv_sqrt_d = 1.0 / math.sqrt(SB_HEAD_DIM)

    def body(x_ref, g_ref, o_ref):
        kind = pl.program_id(0) // per
        xv = x_ref[...]

        @pl.when(kind == 2)
        def _():
            o_ref[...] = xv.astype(BF16)

        @pl.when(kind < 2)
        def _():
            ms = _head_sums(xv * xv) * (1.0 / SB_HEAD_DIM)
            y = xv * lax.rsqrt(ms + NORM_EPS) * g_ref[pl.ds(kind, 1), :]
            o_ref[...] = (y * jnp.where(kind == 0, inv_sqrt_d, 1.0)).astype(BF16)

    blk = pl.BlockSpec((None, tm, LANES), lambda s, i: (s, i, 0))
    return pl.pallas_call(
        body, name=name, grid=(ns, t // tm),
        in_specs=[blk, pl.BlockSpec((SUBLANES, LANES), lambda s, i: (0, 0))], out_specs=blk,
        out_shape=jax.ShapeDtypeStruct((ns, t, LANES), BF16),
        compiler_params=_params("parallel", "parallel"),
    )(qkv, gains)


def _sb_qk_norm_bwd(dq, dk, dv, qkv, gains, *, name):
    ns, t, _ = qkv.shape
    per = ns // 3
    tm = _tile(t, 1024, SUBLANES)
    inv_sqrt_d = 1.0 / math.sqrt(SB_HEAD_DIM)

    def body(dq_ref, dk_ref, dv_ref, x_ref, g_ref, o_ref, dg_ref):
        s = pl.program_id(0)
        kind = s // per

        @pl.when((s == 0) & (pl.program_id(1) == 0))
        def _():
            dg_ref[...] = jnp.zeros_like(dg_ref)

        @pl.when(kind == 2)
        def _():
            o_ref[...] = dv_ref[...].astype(BF16)

        @pl.when(kind < 2)
        def _():
            xv = x_ref[...]
            dy = jnp.where(kind == 0, dq_ref[...] * inv_sqrt_d, dk_ref[...])
            r = lax.rsqrt(_head_sums(xv * xv) * (1.0 / SB_HEAD_DIM) + NORM_EPS)
            xhat = xv * r
            u = dy * g_ref[pl.ds(kind, 1), :]
            o_ref[...] = (r * (u - xhat * _head_sums(u * xhat) * (1.0 / SB_HEAD_DIM))).astype(BF16)
            dg_ref[pl.ds(kind, 1), :] += jnp.sum(dy * xhat, axis=0, keepdims=True)

    def grad_blk(kind):
        def index(s, i):
            mine = (s >= kind * per) & (s < (kind + 1) * per)
            return jnp.where(mine, s - kind * per, 0), jnp.where(mine, i, 0), 0
        return pl.BlockSpec((None, tm, LANES), index)

    blk = pl.BlockSpec((None, tm, LANES), lambda s, i: (s, i, 0))
    vec = pl.BlockSpec((SUBLANES, LANES), lambda s, i: (0, 0))
    return pl.pallas_call(
        body, name=name, grid=(ns, t // tm),
        in_specs=[grad_blk(0), grad_blk(1), grad_blk(2), blk, vec], out_specs=[blk, vec],
        out_shape=[jax.ShapeDtypeStruct((ns, t, LANES), BF16), jax.ShapeDtypeStruct((SUBLANES, LANES), F32)],
        compiler_params=_params("arbitrary", "arbitrary"),
    )(dq, dk, dv, qkv, gains)


def _split_dot(v, ones_mat, pieces, left=False):
    total, rest = None, v
    for p in range(pieces):
        part = rest.astype(BF16)
        if p + 1 < pieces:
            rest = rest - part.astype(F32)
        d = (jnp.dot(ones_mat, part, preferred_element_type=F32) if left
             else jnp.dot(part, ones_mat, preferred_element_type=F32))
        total = d if total is None else total + d
    return total


LOGIT_SUM_PIECES = 2
GRAD_SUM_PIECES = 2
LOG_WEIGHT_UNDERFLOW = -105.0


def _sb_attn_fwd(qkv_n, n_heads, *, name):
    ns, t, _ = qkv_n.shape
    per = ns // 3
    bq, blk, hd = SB_QUERY_BLOCK, SB_BLOCK, SB_HEAD_DIM
    nq, n_diag = t // bq, bq // blk

    def body(q_ref, k_ref, v_ref, o_ref, walk_ref):
        i = pl.program_id(1)
        row = lax.broadcasted_iota(jnp.int32, (blk, blk), 0)
        col = lax.broadcasted_iota(jnp.int32, (blk, blk), 1)
        later_keys = (row > col).astype(BF16)
        qry = lax.broadcasted_iota(jnp.int32, (bq, blk), 0)
        key = lax.broadcasted_iota(jnp.int32, (bq, blk), 1)

        def tile(kb, carry, key_offset):
            out = []
            start = pl.multiple_of(kb * blk, blk)
            for hf in range(2):
                lanes = slice(hf * hd, (hf + 1) * hd)
                run, acc = carry[hf]
                z = lax.dot_general(q_ref[:, lanes], k_ref[pl.ds(start, blk), lanes], NT_DIMS,
                                    preferred_element_type=F32)
                sp = _softplus(z)
                lm = -sp if key_offset is None else jnp.where(key + key_offset < qry, -sp, 0.0)
                after = _split_dot(lm, later_keys, LOGIT_SUM_PIECES) + run
                a = jnp.exp(z - sp + after)
                if key_offset is not None:
                    a = jnp.where(key + key_offset < qry, a, 0.0)
                acc = acc + jnp.dot(a.astype(BF16), v_ref[pl.ds(start, blk), lanes], preferred_element_type=F32)
                out.append((run + jnp.sum(lm, axis=1, keepdims=True), acc))
            return tuple(out)

        def live(carry):
            return jnp.max(jnp.maximum(carry[0][0], carry[1][0])) > LOG_WEIGHT_UNDERFLOW

        def step(state):
            s, _, carry = state
            carry = tile(n_diag * i - 1 - s, carry, None)
            return s + 1, live(carry), carry

        carry = tuple((jnp.zeros((bq, 1), F32), jnp.zeros((bq, hd), F32)) for _ in range(2))
        for j in reversed(range(n_diag)):
            carry = tile(n_diag * i + j, carry, j * blk)
        walked, _, carry = lax.while_loop(lambda st: (st[0] < n_diag * i) & st[1], step,
                                          (jnp.int32(0), live(carry), carry))
        o_ref[...] = jnp.concatenate([carry[0][1], carry[1][1]], axis=1)
        lane = lax.broadcasted_iota(jnp.int32, (1, LANES), 1)
        walk_ref[...] = jnp.where(lane < WALK_LANES, carry[0][0],
                                  jnp.where(lane < 2 * WALK_LANES, carry[1][0], walked.astype(F32)))

    q_blk = pl.BlockSpec((None, bq, LANES), lambda p, i: (p, i, 0))
    return pl.pallas_call(
        body, name=name, grid=(per, nq),
        in_specs=[q_blk, pl.BlockSpec((None, t, LANES), lambda p, i: (per + p, 0, 0)),
                  pl.BlockSpec((None, t, LANES), lambda p, i: (2 * per + p, 0, 0))],
        out_specs=[pl.BlockSpec((bq, LANES), lambda p, i: (i, p)), q_blk],
        out_shape=[jax.ShapeDtypeStruct((t, n_heads * hd), F32), jax.ShapeDtypeStruct((per, t, LANES), F32)],
        compiler_params=_params("parallel", "arbitrary"),
    )(qkv_n, qkv_n, qkv_n)


WALK_LANES = 43


def _sb_attn_bwd(do, walk, qkv_n, *, name):
    ns, t, _ = qkv_n.shape
    per = ns // 3
    bq, blk, hd = SB_QUERY_BLOCK, SB_BLOCK, SB_HEAD_DIM
    nq, n_diag = t // bq, bq // blk
    nt_dims = (((1,), (1,)), ((), ()))
    tn_dims = (((0,), (0,)), ((), ()))

    def body(q_ref, k_ref, v_ref, do_ref, walk_ref, dq_ref, dk_ref, dv_ref):
        i = pl.program_id(1)

        @pl.when(i == 0)
        def _():
            dk_ref[...] = jnp.zeros_like(dk_ref)
            dv_ref[...] = jnp.zeros_like(dv_ref)

        walk_t = walk_ref[...].T
        tots = [walk_t[hf * WALK_LANES:hf * WALK_LANES + 1, :] for hf in range(2)]
        reached = jnp.clip(jnp.max(walk_t[2 * WALK_LANES:2 * WALK_LANES + 1, :]).astype(jnp.int32), 0, n_diag * i)

        row = lax.broadcasted_iota(jnp.int32, (blk, blk), 0)
        col = lax.broadcasted_iota(jnp.int32, (blk, blk), 1)
        later_keys = (col > row).astype(BF16)
        earlier_keys = (col < row).astype(BF16)
        key = lax.broadcasted_iota(jnp.int32, (blk, bq), 0)
        qry = lax.broadcasted_iota(jnp.int32, (blk, bq), 1)
        halves = [slice(hf * hd, (hf + 1) * hd) for hf in range(2)]
        q_hs = [q_ref[:, lanes] for lanes in halves]
        do_bs = [do_ref[:, lanes].astype(BF16) for lanes in halves]

        def scores(kb, hf, key_offset):
            k_blk = k_ref[pl.ds(pl.multiple_of(kb * blk, blk), blk), halves[hf]]
            z = lax.dot_general(k_blk, q_hs[hf], nt_dims, preferred_element_type=F32)
            sp = _softplus(z)
            return k_blk, z, sp, (-sp if key_offset is None else jnp.where(key + key_offset < qry, -sp, 0.0))

        def tile(kb, carry, key_offset):
            out = []
            start = pl.multiple_of(kb * blk, blk)
            for hf, lanes in enumerate(halves):
                seen, gsum, dq = carry[hf]
                q_h, do_b = q_hs[hf], do_bs[hf]
                k_blk, z, sp, lm = scores(kb, hf, key_offset)
                blk_tot = jnp.sum(lm, axis=0, keepdims=True)
                after = _split_dot(lm, later_keys, LOGIT_SUM_PIECES, left=True) + (tots[hf] - seen - blk_tot)
                a = jnp.exp(z - sp + after)
                if key_offset is not None:
                    a = jnp.where(key + key_offset < qry, a, 0.0)
                da = lax.dot_general(v_ref[pl.ds(start, blk), lanes], do_b, nt_dims, preferred_element_type=F32)
                g = da * a
                before = _split_dot(g, earlier_keys, GRAD_SUM_PIECES, left=True) + gsum
                omb = jnp.exp(-sp)
                dz = g * omb - (1.0 - omb) * before
                if key_offset is not None:
                    dz = jnp.where(key + key_offset < qry, dz, 0.0)
                dz_b = dz.astype(BF16)
                dk_ref[pl.ds(start, blk), lanes] += jnp.dot(dz_b, q_h, preferred_element_type=F32)
                dv_ref[pl.ds(start, blk), lanes] += jnp.dot(a.astype(BF16), do_b, preferred_element_type=F32)
                dq = dq + lax.dot_general(dz_b, k_blk, tn_dims, preferred_element_type=F32)
                out.append((seen + blk_tot, gsum + jnp.sum(g, axis=0, keepdims=True), dq))
            return tuple(out)

        init = tuple((jnp.zeros((1, bq), F32), jnp.zeros((1, bq), F32), jnp.zeros((bq, hd), F32))
                     for _ in range(2))
        carry = lax.fori_loop(n_diag * i - reached, n_diag * i, lambda kb, c: tile(kb, c, None), init)
        for j in range(n_diag):
            carry = tile(n_diag * i + j, carry, j * blk)
        dq_ref[...] = jnp.concatenate([carry[0][2], carry[1][2]], axis=1)

    full = lambda off: pl.BlockSpec((None, t, LANES), lambda p, i: (off + p, 0, 0))
    q_blk = pl.BlockSpec((None, bq, LANES), lambda p, i: (p, i, 0))
    slab = jax.ShapeDtypeStruct((per, t, LANES), F32)
    return pl.pallas_call(
        body, name=name, grid=(per, nq),
        in_specs=[q_blk, full(per), full(2 * per), pl.BlockSpec((bq, LANES), lambda p, i: (i, p)), q_blk],
        out_specs=[q_blk, full(0), full(0)],
        out_shape=[slab, slab, slab],
        compiler_params=_params("parallel", "arbitrary"),
    )(qkv_n, qkv_n, qkv_n, do, walk)


def _sb_mixer_fwd(x, gain, w_qkv_t, qk_gains, w_out, tag):
    n_heads = w_out.shape[0] // SB_HEAD_DIM
    h = _rms_fwd(x, gain, name=f"sb_norm_{tag}")
    qkv = _matmul(h, w_qkv_t, "nt", out_slabs=True, tn_cap=512, name=f"sb_qkv_{tag}")
    qkv_n = _sb_qk_norm_fwd(qkv, qk_gains, name=f"sb_qknorm_{tag}")
    o, walk = _sb_attn_fwd(qkv_n, n_heads, name=f"sb_attn_{tag}")
    x_new = _matmul(o, w_out, "nn", resid=x, name=f"sb_out_{tag}")
    return x_new, (x, h, qkv, qkv_n, o, walk)


def _sb_mixer_bwd(dx, saved, gain, w_qkv_t, qk_gains, w_out, grads, row_qkv, row_out, tag):
    x, h, qkv, qkv_n, o, walk = saved
    do = _matmul(dx, w_out, "nt", name=f"sb_do_{tag}")
    grads = _matmul_tn_into(grads, o, dx, row_out, name=f"sb_dwout_{tag}")
    dq, dk, dv = _sb_attn_bwd(do, walk, qkv_n, name=f"sb_dattn_{tag}")
    dqkv, dqk_gains = _sb_qk_norm_bwd(dq, dk, dv, qkv, qk_gains, name=f"sb_dqknorm_{tag}")
    grads = _matmul_tn_into(grads, dqkv, h, row_qkv, a_slabs=True, name=f"sb_dwqkv_{tag}")
    dh = _matmul(dqkv, w_qkv_t, "nn", a_slabs=True, name=f"sb_dh_{tag}")
    dx_in, dgain = _rms_bwd(x, gain, dh, dx, name=f"sb_dnorm_{tag}")
    return dx_in, grads, dqk_gains, dgain


MESH = pl.DeviceIdType.MESH


def _position():
    return lax.axis_index("x"), lax.axis_index("y"), lax.axis_index("c")


def _all_gather(shard, *, name):
    rows, n = shard.shape
    space = pltpu.VMEM

    def body(x_ref, out_ref, send_sems, recv_sems, local_sem):
        x, y, c = _position()
        me, sibling = (x, y, c), (x, y, 1 - c)
        chips = [(1 - x, y), (x, 1 - y), (1 - x, 1 - y)]

        def block(px, py, pc):
            return out_ref.at[4 * px + 2 * py + pc]

        def copy(k, blk, to, src=None):
            return pltpu.make_async_remote_copy(
                src_ref=block(*blk) if src is None else src, dst_ref=block(*blk),
                send_sem=send_sems.at[k], recv_sem=recv_sems.at[k], device_id=to, device_id_type=MESH)

        mine = pltpu.make_async_copy(x_ref, block(*me), local_sem)
        mine.start()
        first = [copy(0, me, sibling, src=x_ref)]
        first += [copy(1 + j, me, (*chip, c), src=x_ref) for j, chip in enumerate(chips)]
        for cp in first:
            cp.start()
        passed = [copy(4 + j, (*chip, c), sibling) for j, chip in enumerate(chips)]
        for j, chip in enumerate(chips):
            copy(1 + j, (*chip, c), me).wait_recv()
            passed[j].start()
        copy(0, sibling, me).wait_recv()
        for j, chip in enumerate(chips):
            copy(4 + j, (*chip, 1 - c), me).wait_recv()
        for cp in first + passed:
            cp.wait_send()
        mine.wait()

    return pl.pallas_call(
        body, name=name,
        out_shape=jax.ShapeDtypeStruct((N_DEV, rows, n), shard.dtype),
        in_specs=[pl.BlockSpec(memory_space=space)], out_specs=pl.BlockSpec(memory_space=space),
        scratch_shapes=[pltpu.SemaphoreType.DMA((7,)), pltpu.SemaphoreType.DMA((7,)), pltpu.SemaphoreType.DMA],
        compiler_params=pltpu.CompilerParams(vmem_limit_bytes=V7X_VMEM_LIMIT_BYTES),
    )(shard)


def _all_gather_forwarding(shard, *, name):
    rows, n = shard.shape
    half = rows // 2
    assert rows % (4 * SUBLANES) == 0

    def body(x_ref, out_ref, send_sems, recv_sems, local_sem):
        x, y, c = _position()
        me, sibling = (x, y, c), (x, y, 1 - c)
        x_nbr, y_nbr, diag = (1 - x, y), (x, 1 - y), (1 - x, 1 - y)
        lower, upper = pl.ds(0, half), pl.ds(half, half)

        def block(px, py, pc, part=None):
            ref = out_ref.at[4 * px + 2 * py + pc]
            return ref if part is None else ref.at[part]

        def copy(k, blk, to, src=None, part=None):
            return pltpu.make_async_remote_copy(
                src_ref=block(*blk, part) if src is None else src, dst_ref=block(*blk, part),
                send_sem=send_sems.at[k], recv_sem=recv_sems.at[k], device_id=to, device_id_type=MESH)

        mine = pltpu.make_async_copy(x_ref, block(*me), local_sem)
        mine.start()
        sent = [copy(0, me, sibling, src=x_ref), copy(1, me, (*x_nbr, c), src=x_ref),
                copy(2, me, (*y_nbr, c), src=x_ref)]
        for cp in sent:
            cp.start()
        copy(1, (*x_nbr, c), me).wait_recv()
        onward = [copy(3, (*x_nbr, c), (*y_nbr, c), part=lower), copy(5, (*x_nbr, c), sibling)]
        for cp in onward:
            cp.start()
        copy(2, (*y_nbr, c), me).wait_recv()
        onward += [copy(4, (*y_nbr, c), (*x_nbr, c), part=upper), copy(6, (*y_nbr, c), sibling)]
        for cp in onward[2:]:
            cp.start()
        copy(3, (*diag, c), me, part=lower).wait_recv()
        copy(4, (*diag, c), me, part=upper).wait_recv()
        onward.append(copy(7, (*diag, c), sibling))
        onward[-1].start()
        sent += onward
        copy(0, sibling, me).wait_recv()
        for k, chip in ((5, x_nbr), (6, y_nbr), (7, diag)):
            copy(k, (*chip, 1 - c), me).wait_recv()
        for cp in sent:
            cp.wait_send()
        mine.wait()

    hbm = pl.BlockSpec(memory_space=pltpu.HBM)
    return pl.pallas_call(
        body, name=name,
        out_shape=jax.ShapeDtypeStruct((N_DEV, rows, n), shard.dtype), in_specs=[hbm], out_specs=hbm,
        scratch_shapes=[pltpu.SemaphoreType.DMA((8,)), pltpu.SemaphoreType.DMA((8,)), pltpu.SemaphoreType.DMA],
    )(shard)


def _exchange_sibling(parts, *, name):
    _, nchip, rows, n = parts.shape

    def body(p_ref, recv_ref, send_sem, recv_sem):
        x, y, c = _position()
        cp = pltpu.make_async_remote_copy(src_ref=p_ref.at[1 - c], dst_ref=recv_ref, send_sem=send_sem,
                                          recv_sem=recv_sem, device_id=(x, y, 1 - c), device_id_type=MESH)
        cp.start()
        cp.wait()

    return pl.pallas_call(
        body, name=name,
        out_shape=jax.ShapeDtypeStruct((nchip, rows, n), parts.dtype),
        in_specs=[pl.BlockSpec(memory_space=pltpu.HBM)], out_specs=pl.BlockSpec(memory_space=pltpu.HBM),
        scratch_shapes=[pltpu.SemaphoreType.DMA, pltpu.SemaphoreType.DMA],
    )(parts)


def _exchange_chips(chip_sums, *, name):
    _, rows, n = chip_sums.shape

    def body(s_ref, recv_ref, send_sems, recv_sems):
        x, y, c = _position()
        chips = [(1 - x, y), (x, 1 - y), (1 - x, 1 - y)]
        copies = [pltpu.make_async_remote_copy(
            src_ref=s_ref.at[2 * cx + cy], dst_ref=recv_ref.at[j], send_sem=send_sems.at[j],
            recv_sem=recv_sems.at[j], device_id=(cx, cy, c), device_id_type=MESH)
            for j, (cx, cy) in enumerate(chips)]
        for cp in copies:
            cp.start()
        for cp in copies:
            cp.wait()

    return pl.pallas_call(
        body, name=name,
        out_shape=jax.ShapeDtypeStruct((3, rows, n), chip_sums.dtype),
        in_specs=[pl.BlockSpec(memory_space=pltpu.HBM)], out_specs=pl.BlockSpec(memory_space=pltpu.HBM),
        scratch_shapes=[pltpu.SemaphoreType.DMA((3,)), pltpu.SemaphoreType.DMA((3,))],
    )(chip_sums)


def _add_pairs(parts, recv, c_mine, *, name):
    _, nchip, rows, n = parts.shape
    tr = _tile(rows, 512, SUBLANES)

    def body(c_ref, a_ref, b_ref, o_ref, wire_ref):
        s = a_ref[...] + b_ref[...]
        o_ref[...] = s
        wire_ref[...] = s.astype(WIRE_DTYPE)

    out_blk = pl.BlockSpec((None, tr, n), lambda k, i, c: (k, i, 0))
    return pl.pallas_call(
        body, name=name,
        grid_spec=pltpu.PrefetchScalarGridSpec(
            num_scalar_prefetch=1, grid=(nchip, rows // tr),
            in_specs=[pl.BlockSpec((None, None, tr, n), lambda k, i, c: (c[0], k, i, 0)),
                      pl.BlockSpec((None, tr, n), lambda k, i, c: (k, i, 0))],
            out_specs=[out_blk, out_blk]),
        out_shape=[jax.ShapeDtypeStruct((nchip, rows, n), parts.dtype),
                   jax.ShapeDtypeStruct((nchip, rows, n), WIRE_DTYPE)],
        compiler_params=_params("parallel", "parallel"),
    )(c_mine, parts, recv)


def _adamw_math(w, g, m, v):
    m = ADAM_B1 * m + (1.0 - ADAM_B1) * g
    v = ADAM_B2 * v + (1.0 - ADAM_B2) * (g * g)
    m_hat = m / (1.0 - ADAM_B1 ** ADAM_STEP)
    v_hat = v / (1.0 - ADAM_B2 ** ADAM_STEP)
    delta = -ADAM_LR * (m_hat / (jnp.sqrt(v_hat) + ADAM_EPS) + ADAM_WD * w)
    return delta, m, v


def _adamw_sharded(chip_sums, recv, k_mine, w, m, v, *, name):
    rows, n = w.shape
    tr = _tile(rows, 256, SUBLANES)

    def body(k_ref, s_ref, r_ref, w_ref, m_ref, v_ref, g_out, d_out, m_out, v_out):
        g = ((s_ref[...] + r_ref[0].astype(F32)) + r_ref[1].astype(F32)) + r_ref[2].astype(F32)
        delta, m_new, v_new = _adamw_math(w_ref[...], g, m_ref[...], v_ref[...])
        g_out[...] = g
        d_out[...] = delta
        m_out[...] = m_new
        v_out[...] = v_new

    blk = pl.BlockSpec((tr, n), lambda i, k: (i, 0))
    out = jax.ShapeDtypeStruct((rows, n), F32)
    return pl.pallas_call(
        body, name=name,
        grid_spec=pltpu.PrefetchScalarGridSpec(
            num_scalar_prefetch=1, grid=(rows // tr,),
            in_specs=[pl.BlockSpec((None, tr, n), lambda i, k: (k[0], i, 0)),
                      pl.BlockSpec((3, tr, n), lambda i, k: (0, i, 0)), blk, blk, blk],
            out_specs=[blk, blk, blk, blk]),
        out_shape=[out, out, out, out],
        compiler_params=_params("parallel"),
    )(k_mine, chip_sums, recv, w, m, v)


SMALL_ROWS = 40
ROW_MIX_NORM, ROW_FFN_NORM, ROW_CONV_B, ROW_OUT_NORM, ROW_POOL_SCALE, ROW_CONV_W = 0, 4, 8, 12, 14, 16
ROW_SSD_VEC, ROW_QK_GAIN, ROW_LOSS = 32, 33, 34


def _adamw_small(gathered, w, m, v, *, name):
    _, rows, n = gathered.shape

    def body(a_ref, w_ref, m_ref, v_ref, g_out, d_out, m_out, v_out):
        g = a_ref[0]
        for d in range(1, N_DEV):
            g = g + a_ref[d]
        row = lax.broadcasted_iota(jnp.int32, (rows, 1), 0)
        g = jnp.where(row == ROW_QK_GAIN, g + pltpu.roll(g, SB_HEAD_DIM, 1), g)
        g = jnp.where(row == ROW_LOSS, jnp.sum(g, axis=1, keepdims=True), g)
        g_out[...] = g
        delta, m_new, v_new = _adamw_math(w_ref[...], g, m_ref[...], v_ref[...])
        d_out[...] = delta
        m_out[...] = m_new
        v_out[...] = v_new

    out = jax.ShapeDtypeStruct((rows, n), F32)
    return pl.pallas_call(body, name=name, out_shape=[out, out, out, out])(gathered, w, m, v)


BIG_WEIGHTS = ("ffn_gate", "ffn_up", "ffn_down", "sb_qkv", "ssd_out", "pool_in", "sb_out", "pool_group", "ssd_in")
COLUMN_SHARDED = ("ssd_in", "sb_qkv", "ffn_gate", "ffn_up")
ROW_PAD = 512
WIRE_DTYPE = jnp.bfloat16


def _to_rows(name, shard, d):
    if name in COLUMN_SHARDED:
        shard = jnp.swapaxes(shard, -1, -2)
    return shard.reshape(-1, d)


def _from_rows(name, rows, shard_shape):
    if name in COLUMN_SHARDED:
        lead, k, n = shard_shape
        return jnp.swapaxes(rows.reshape(lead, n, k), -1, -2)
    return rows.reshape(shard_shape)


def _pad_rows(a, total):
    return jnp.pad(a, ((0, total - a.shape[0]),) + ((0, 0),) * (a.ndim - 1))


def _exact_bf16_rows(v, d):
    words = lax.bitcast_convert_type(v.reshape(-1), WIRE_DTYPE).reshape(-1)
    return _pad_rows(words, -(-words.shape[0] // d) * d).reshape(-1, d)


def _exact_f32(rows, count):
    words = rows.reshape(rows.shape[0], -1)[:, :2 * count].reshape(rows.shape[0], count, 2)
    return lax.bitcast_convert_type(words, F32)


def _device_blocks(full, d):
    return full.reshape(N_DEV, -1, d)


def kernel(x, mix_norm, pool_in, pool_group, pool_scale, ssd_in, ssd_conv_w, ssd_conv_b, ssd_dt_bias, ssd_a_log, ssd_d, ssd_out_norm, ssd_out, sb_qkv, sb_q_norm, sb_k_norm, sb_out, ffn_norm, ffn_gate, ffn_up, ffn_down, loss_target, m_mix_norm, m_pool_in, m_pool_group, m_pool_scale, m_ssd_in, m_ssd_conv_w, m_ssd_conv_b, m_ssd_dt_bias, m_ssd_a_log, m_ssd_d, m_ssd_out_norm, m_ssd_out, m_sb_qkv, m_sb_q_norm, m_sb_k_norm, m_sb_out, m_ffn_norm, m_ffn_gate, m_ffn_up, m_ffn_down, v_mix_norm, v_pool_in, v_pool_group, v_pool_scale, v_ssd_in, v_ssd_conv_w, v_ssd_conv_b, v_ssd_dt_bias, v_ssd_a_log, v_ssd_d, v_ssd_out_norm, v_ssd_out, v_sb_qkv, v_sb_q_norm, v_sb_k_norm, v_sb_out, v_ffn_norm, v_ffn_gate, v_ffn_up, v_ffn_down):
    weights = dict(mix_norm=mix_norm, pool_in=pool_in, pool_group=pool_group, pool_scale=pool_scale, ssd_in=ssd_in,
                   ssd_conv_w=ssd_conv_w, ssd_conv_b=ssd_conv_b, ssd_dt_bias=ssd_dt_bias, ssd_a_log=ssd_a_log,
                   ssd_d=ssd_d, ssd_out_norm=ssd_out_norm, ssd_out=ssd_out, sb_qkv=sb_qkv, sb_q_norm=sb_q_norm,
                   sb_k_norm=sb_k_norm, sb_out=sb_out, ffn_norm=ffn_norm, ffn_gate=ffn_gate, ffn_up=ffn_up,
                   ffn_down=ffn_down)
    mom1 = dict(mix_norm=m_mix_norm, pool_in=m_pool_in, pool_group=m_pool_group, pool_scale=m_pool_scale,
                ssd_in=m_ssd_in, ssd_conv_w=m_ssd_conv_w, ssd_conv_b=m_ssd_conv_b, ssd_dt_bias=m_ssd_dt_bias,
                ssd_a_log=m_ssd_a_log, ssd_d=m_ssd_d, ssd_out_norm=m_ssd_out_norm, ssd_out=m_ssd_out,
                sb_qkv=m_sb_qkv, sb_q_norm=m_sb_q_norm, sb_k_norm=m_sb_k_norm, sb_out=m_sb_out,
                ffn_norm=m_ffn_norm, ffn_gate=m_ffn_gate, ffn_up=m_ffn_up, ffn_down=m_ffn_down)
    mom2 = dict(mix_norm=v_mix_norm, pool_in=v_pool_in, pool_group=v_pool_group, pool_scale=v_pool_scale,
                ssd_in=v_ssd_in, ssd_conv_w=v_ssd_conv_w, ssd_conv_b=v_ssd_conv_b, ssd_dt_bias=v_ssd_dt_bias,
                ssd_a_log=v_ssd_a_log, ssd_d=v_ssd_d, ssd_out_norm=v_ssd_out_norm, ssd_out=v_ssd_out,
                sb_qkv=v_sb_qkv, sb_q_norm=v_sb_q_norm, sb_k_norm=v_sb_k_norm, sb_out=v_sb_out,
                ffn_norm=v_ffn_norm, ffn_gate=v_ffn_gate, ffn_up=v_ffn_up, ffn_down=v_ffn_down)
    names = list(weights)
    depth, d = mix_norm.shape
    xs, ys, cs = _position()
    dev = 4 * xs + 2 * ys + cs
    chip = 2 * xs + ys

    seg = {}
    row = 0
    for name in BIG_WEIGHTS:
        n_rows = weights[name].size // d
        seg[name] = (row, n_rows)
        row += -(-n_rows // SUBLANES) * SUBLANES
    big_rows = row
    n_scale, n_convw = pool_scale.size, ssd_conv_w.size
    exact = jnp.concatenate([_exact_bf16_rows(pool_scale, d), _exact_bf16_rows(ssd_conv_w, d)], axis=0)
    scale_rows = _exact_bf16_rows(pool_scale, d).shape[0]
    packed_rows = -(-(big_rows + exact.shape[0]) // ROW_PAD) * ROW_PAD

    def pack(tree, dtype):
        ends = [seg[n][0] for n in BIG_WEIGHTS[1:]] + [big_rows]
        return jnp.concatenate([_pad_rows(_to_rows(n, tree[n], d).astype(dtype), end - seg[n][0])
                                for n, end in zip(BIG_WEIGHTS, ends)], axis=0)

    w_wire = _pad_rows(jnp.concatenate([pack(weights, WIRE_DTYPE), exact], axis=0), packed_rows)
    gathered = _all_gather_forwarding(w_wire, name="gather_weights")

    def seg_of(name):
        a, n = seg[name]
        return gathered[:, a:a + n]

    n_pool, n_ssd, n_sb = pool_in.shape[0], ssd_in.shape[0], sb_qkv.shape[0]
    assert n_ssd == 1 and n_sb == 1
    w_pool_in = seg_of("pool_in").reshape(N_DEV, n_pool, -1, d).transpose(1, 0, 2, 3).reshape(n_pool, d, d)
    grp = pool_group.shape
    w_pool_group = seg_of("pool_group").reshape(N_DEV, grp[0], grp[1], grp[2], grp[3]).transpose(1, 2, 0, 3, 4)
    w_pool_group = w_pool_group.reshape(grp[0], grp[1], grp[3], grp[3])
    w_ssd_in_t = seg_of("ssd_in").reshape(-1, d)
    w_ssd_out = seg_of("ssd_out").reshape(-1, d)
    w_sb_qkv_t = seg_of("sb_qkv").reshape(-1, d)
    w_sb_out = seg_of("sb_out").reshape(-1, d)
    hidden = ffn_down.shape[1] * N_DEV
    w_gate_t = seg_of("ffn_gate").reshape(N_DEV, depth, -1, d).transpose(1, 0, 2, 3).reshape(depth, hidden, d)
    w_up_t = seg_of("ffn_up").reshape(N_DEV, depth, -1, d).transpose(1, 0, 2, 3).reshape(depth, hidden, d)
    w_down = seg_of("ffn_down").reshape(N_DEV, depth, -1, d).transpose(1, 0, 2, 3).reshape(depth, hidden, d)
    exact_all = gathered[:, big_rows:big_rows + exact.shape[0]]
    scale_full = _exact_f32(exact_all[:, :scale_rows], n_scale).reshape(N_DEV, n_pool, -1)
    scale_full = scale_full.transpose(1, 0, 2).reshape(n_pool, d)
    convw_full = _exact_f32(exact_all[:, scale_rows:], n_convw).reshape(N_DEV, SSD_CONV, -1)
    convw_full = convw_full.transpose(1, 0, 2).reshape(SSD_CONV, -1)

    d_inner = w_ssd_out.shape[0]
    n_zx = w_ssd_in_t.shape[0] - ssd_dt_bias.shape[1]
    w_zx_t = w_ssd_in_t[:n_zx]
    n_ssd_heads = ssd_dt_bias.shape[1]
    n_ssd_groups = n_ssd_heads // SSD_HEADS_PER_GROUP
    w_dt_t = _ssd_group_pad(w_ssd_in_t[n_zx:].T, n_ssd_groups).T
    par = _pad_rows(_ssd_group_pad(jnp.concatenate([ssd_dt_bias, ssd_a_log, ssd_d], axis=0), n_ssd_groups), SUBLANES)
    d_rep = jnp.repeat(ssd_d[0], SSD_HEAD_DIM)[None]
    qk_gains = jnp.zeros((SUBLANES, LANES), F32).at[0].set(jnp.tile(sb_q_norm[0], 2)).at[1].set(jnp.tile(sb_k_norm[0], 2))

    act = x[0]
    saved = []
    for i in range(depth):
        kind, j = i % 3, i // 3
        gain = mix_norm[i:i + 1]
        if kind == 0:
            act, s = _pool_mixer_fwd(act, gain, w_pool_in[j], w_pool_group[j], scale_full[j:j + 1], f"l{i}")
        elif kind == 1:
            act, s = _ssd_mixer_fwd(act, gain, w_zx_t, w_dt_t, convw_full, ssd_conv_b, par, d_rep, ssd_out_norm,
                                    w_ssd_out, f"l{i}")
        else:
            act, s = _sb_mixer_fwd(act, gain, w_sb_qkv_t, qk_gains, w_sb_out, f"l{i}")
        act, f = _ffn_fwd(act, ffn_norm[i:i + 1], w_gate_t[i], w_up_t[i], w_down[i], f"l{i}")
        saved.append((s, f))
    dact, loss_cols = _loss_head(act, loss_target[0], name="loss_head")

    def layer_row(name, layer):
        return seg[name][0] + layer * (seg[name][1] // weights[name].shape[0])

    grads = jnp.zeros((N_DEV, packed_rows, d), F32)
    g_mix_norm, g_ffn_norm = [None] * depth, [None] * depth
    g_pool_group, g_pool_scale = [None] * n_pool, [None] * n_pool
    for i in reversed(range(depth)):
        kind, j = i % 3, i // 3
        gain = mix_norm[i:i + 1]
        s, f = saved[i]
        dact, grads, g_ffn_norm[i] = _ffn_bwd(
            dact, f, ffn_norm[i:i + 1], w_gate_t[i], w_up_t[i], w_down[i], grads,
            [layer_row(n, i) for n in ("ffn_gate", "ffn_up", "ffn_down")], f"l{i}")
        if kind == 0:
            dact, grads, g_pool_group[j], g_pool_scale[j], g_mix_norm[i] = _pool_mixer_bwd(
                dact, s, gain, w_pool_in[j], w_pool_group[j], scale_full[j:j + 1], grads, layer_row("pool_in", j),
                f"l{i}")
        elif kind == 1:
            (dact, grads, g_zx_t, g_dt_t, g_conv_w, g_conv_b, g_par, g_out_norm,
             g_mix_norm[i]) = _ssd_mixer_bwd(dact, s, gain, w_zx_t, w_dt_t, convw_full, ssd_conv_b, par, d_rep,
                                             ssd_out_norm, w_ssd_out, grads, layer_row("ssd_out", j), f"l{i}")
        else:
            dact, grads, g_qk_gains, g_mix_norm[i] = _sb_mixer_bwd(
                dact, s, gain, w_sb_qkv_t, qk_gains, w_sb_out, grads, layer_row("sb_qkv", j),
                layer_row("sb_out", j), f"l{i}")
    grad_x = dact[None]

    g_ssd_in = jnp.concatenate([g_zx_t, _ssd_group_unpad(g_dt_t.T, n_ssd_groups).T], axis=0)
    g_group = jnp.concatenate([_device_blocks(gg[k], d) for gg in g_pool_group for k in range(gg.shape[0])], axis=1)
    for name, blocks in (("ssd_in", _device_blocks(g_ssd_in, d)), ("pool_group", g_group)):
        blocks = blocks.reshape(N_DEV // 2, 2, -1, d).swapaxes(0, 1).reshape(N_DEV, -1, d)
        grads = lax.dynamic_update_slice(grads, blocks, (0, seg[name][0], 0))
    parts = grads.reshape(2, N_DEV // 2, packed_rows, d)
    from_sibling = _exchange_sibling(parts, name="reduce_sibling")
    chip_sums, chip_sums_wire = _add_pairs(parts, from_sibling, cs.reshape(1).astype(jnp.int32),
                                           name="reduce_sibling_add")
    from_chips = _exchange_chips(chip_sums_wire, name="reduce_chips")

    def pack_f32(tree):
        return _pad_rows(pack(tree, F32), packed_rows)

    big_out = _adamw_sharded(chip_sums, from_chips, chip.reshape(1).astype(jnp.int32), pack_f32(weights),
                             pack_f32(mom1), pack_f32(mom2), name="adamw_sharded")

    def small_pack(mix, ffn, conv_b, out_norm, scale, conv_w, vec, qk, loss=None):
        buf = jnp.zeros((SMALL_ROWS, d), F32)
        buf = buf.at[ROW_MIX_NORM:ROW_MIX_NORM + depth].set(mix).at[ROW_FFN_NORM:ROW_FFN_NORM + depth].set(ffn)
        buf = buf.at[ROW_CONV_B:ROW_CONV_B + conv_b.size // d].set(conv_b.reshape(-1, d))
        buf = buf.at[ROW_OUT_NORM:ROW_OUT_NORM + out_norm.size // d].set(out_norm.reshape(-1, d))
        buf = buf.at[ROW_POOL_SCALE:ROW_POOL_SCALE + n_pool].set(scale)
        buf = buf.at[ROW_CONV_W:ROW_CONV_W + conv_w.size // d].set(conv_w.reshape(-1, d))
        buf = buf.at[ROW_SSD_VEC].set(vec.reshape(-1)).at[ROW_QK_GAIN].set(qk.reshape(-1))
        if loss is not None:
            buf = buf.at[ROW_LOSS].set(loss.reshape(-1))
        return buf

    def small_params(tree):
        scale = lax.dynamic_update_slice(jnp.zeros((n_pool, d), F32), tree["pool_scale"],
                                         (0, dev * tree["pool_scale"].shape[1]))
        conv_w = lax.dynamic_update_slice(jnp.zeros(convw_full.shape, F32), tree["ssd_conv_w"][0],
                                          (0, dev * tree["ssd_conv_w"].shape[2]))
        vec = jnp.zeros((SUBLANES, LANES), F32)
        vec = vec.at[0, :n_ssd_heads].set(tree["ssd_dt_bias"][0]).at[1, :n_ssd_heads].set(tree["ssd_a_log"][0])
        vec = vec.at[2, :n_ssd_heads].set(tree["ssd_d"][0])
        qk = jnp.zeros((SUBLANES, LANES), F32)
        qk = qk.at[0, SB_HEAD_DIM:].set(tree["sb_q_norm"][0]).at[1, SB_HEAD_DIM:].set(tree["sb_k_norm"][0])
        return small_pack(tree["mix_norm"], tree["ffn_norm"], tree["ssd_conv_b"], tree["ssd_out_norm"], scale,
                          conv_w, vec, qk)

    small_partial = small_pack(jnp.concatenate(g_mix_norm, axis=0), jnp.concatenate(g_ffn_norm, axis=0), g_conv_b,
                               g_out_norm, jnp.concatenate(g_pool_scale, axis=0), g_conv_w,
                               jnp.zeros((SUBLANES, LANES), F32).at[:3, :n_ssd_heads].set(
                                   _ssd_group_unpad(g_par[:3], n_ssd_groups)), g_qk_gains,
                               loss_cols)
    small_all = _all_gather(small_partial, name="gather_small")
    small_out = _adamw_small(small_all, small_params(weights), small_params(mom1), small_params(mom2),
                             name="adamw_small")
    loss = small_out[0][ROW_LOSS, 0]

    def unpack(big, small):
        out = {}
        for name in BIG_WEIGHTS:
            a, n = seg[name]
            out[name] = _from_rows(name, big[a:a + n], weights[name].shape)
        out["mix_norm"] = small[ROW_MIX_NORM:ROW_MIX_NORM + depth]
        out["ffn_norm"] = small[ROW_FFN_NORM:ROW_FFN_NORM + depth]
        out["ssd_conv_b"] = small[ROW_CONV_B:ROW_CONV_B + ssd_conv_b.size // d].reshape(ssd_conv_b.shape)
        out["ssd_out_norm"] = small[ROW_OUT_NORM:ROW_OUT_NORM + ssd_out_norm.size // d].reshape(ssd_out_norm.shape)
        out["pool_scale"] = lax.dynamic_slice(small[ROW_POOL_SCALE:ROW_POOL_SCALE + n_pool],
                                              (0, dev * pool_scale.shape[1]), pool_scale.shape)
        conv_w = small[ROW_CONV_W:ROW_CONV_W + convw_full.size // d].reshape(convw_full.shape)
        out["ssd_conv_w"] = lax.dynamic_slice(conv_w, (0, dev * ssd_conv_w.shape[2]), ssd_conv_w.shape[1:])[None]
        vec = small[ROW_SSD_VEC].reshape(SUBLANES, LANES)
        out["ssd_dt_bias"], out["ssd_a_log"], out["ssd_d"] = (vec[r:r + 1, :n_ssd_heads] for r in range(3))
        qk = small[ROW_QK_GAIN].reshape(SUBLANES, LANES)
        out["sb_q_norm"], out["sb_k_norm"] = qk[0:1, SB_HEAD_DIM:], qk[1:2, SB_HEAD_DIM:]
        return [out[n] for n in names]

    results = [unpack(b, s) for b, s in zip(big_out, small_out)]
    return (loss, grad_x, *results[0], *results[1], *results[2], *results[3])
```

```python
import math

import jax
import jax.numpy as jnp
from jax import lax
from jax.experimental import pallas as pl
from jax.experimental.pallas import tpu as pltpu

F32 = jnp.float32
BF16 = jnp.bfloat16

N_DEV = 8
NORM_EPS = 1e-6
V7X_VMEM_LIMIT_BYTES = 48 * 1024 * 1024
LANES = 128
SUBLANES = 8

POOL_WINDOWS = (2, 4, 8, 16)
SSD_CHUNK = 256
SSD_HEAD_DIM = 64
SSD_STATE = 128
SSD_HEADS_PER_GROUP = 4
SSD_CONV = 4
SB_HEAD_DIM = 64
SB_BLOCK = 256
SB_QUERY_BLOCK = 256

ADAM_LR = 0.001
ADAM_B1 = 0.9
ADAM_B2 = 0.999
ADAM_EPS = 1e-08
ADAM_WD = 0.01
ADAM_STEP = 10


def _params(*sem):
    return pltpu.CompilerParams(dimension_semantics=sem, vmem_limit_bytes=V7X_VMEM_LIMIT_BYTES)


def _tile(n, cap, mult):
    best = None
    for t in range(mult, min(n, cap) + 1, mult):
        if n % t == 0:
            best = t
    return best or n


def _load_slabs(ref, slabs):
    if not slabs:
        return ref[...]
    return jnp.concatenate([ref[p] for p in range(ref.shape[0])], axis=1)


def _matmul(a, b, mode, *, name, out_dtype=F32, resid=None, a_slabs=False, out_slabs=False,
            tm_cap=1024, tn_cap=1024, tk_cap=2048):
    pairs = list(zip(a, b)) if isinstance(a, (list, tuple)) else [(a, b)]
    a, b = pairs[0]
    if a_slabs:
        m, k = a.shape[1], a.shape[0] * LANES
    else:
        m, k = a.shape
    n = b.shape[1] if mode == "nn" else b.shape[0]
    assert (b.shape[0] if mode == "nn" else b.shape[1]) == k
    assert all(pa.shape == a.shape and pb.shape == b.shape for pa, pb in pairs)
    tm, tn, tk = _tile(m, tm_cap, SUBLANES), _tile(n, tn_cap, LANES), _tile(k, tk_cap, LANES)
    nk = k // tk
    dn = (((1,), (0,)), ((), ())) if mode == "nn" else (((1,), (1,)), ((), ()))
    has_resid = resid is not None
    n_pairs = len(pairs)

    def body(*refs):
        ab_refs, rest = refs[:2 * n_pairs], refs[2 * n_pairs:]
        r_ref = rest[0] if has_resid else None
        o_ref = rest[1] if has_resid else rest[0]
        kk = pl.program_id(2)

        def partial():
            total = None
            for p in range(n_pairs):
                d = lax.dot_general(_load_slabs(ab_refs[2 * p], a_slabs).astype(BF16),
                                    ab_refs[2 * p + 1][...].astype(BF16), dn, preferred_element_type=F32)
                total = d if total is None else total + d
            return total

        def finish(r):
            if has_resid:
                r = r + r_ref[...]
            if out_slabs:
                for p in range(tn // LANES):
                    o_ref[p] = r[:, p * LANES:(p + 1) * LANES].astype(out_dtype)
            else:
                o_ref[...] = r.astype(out_dtype)

        if nk == 1:
            finish(partial())
        else:
            acc = rest[-1]

            @pl.when(kk == 0)
            def _():
                acc[...] = jnp.zeros_like(acc)

            acc[...] += partial()

            @pl.when(kk == nk - 1)
            def _():
                finish(acc[...])

    b_spec = (pl.BlockSpec((tk, tn), lambda i, j, kk: (kk, j)) if mode == "nn"
              else pl.BlockSpec((tn, tk), lambda i, j, kk: (j, kk)))
    a_spec = (pl.BlockSpec((tk // LANES, tm, LANES), lambda i, j, kk: (kk, i, 0)) if a_slabs
              else pl.BlockSpec((tm, tk), lambda i, j, kk: (i, kk)))
    in_specs = [a_spec, b_spec] * n_pairs
    args = [t for pair in pairs for t in pair]
    if has_resid:
        in_specs.append(pl.BlockSpec((tm, tn), lambda i, j, kk: (i, j)))
        args.append(resid)
    if out_slabs:
        out_spec = pl.BlockSpec((tn // LANES, tm, LANES), lambda i, j, kk: (j, i, 0))
        out_shape = jax.ShapeDtypeStruct((n // LANES, m, LANES), out_dtype)
    else:
        out_spec = pl.BlockSpec((tm, tn), lambda i, j, kk: (i, j))
        out_shape = jax.ShapeDtypeStruct((m, n), out_dtype)
    return pl.pallas_call(
        body, name=name, grid=(m // tm, n // tn, nk),
        in_specs=in_specs, out_specs=out_spec, out_shape=out_shape,
        scratch_shapes=[pltpu.VMEM((tm, tn), F32)] if nk > 1 else [],
        compiler_params=_params("parallel", "parallel", "arbitrary"),
    )(*args)


def _matmul_tn(a, b, *, name, a_slabs=False, ta_cap=1024, tb_cap=1024, tr_cap=512):
    if a_slabs:
        r, ka = a.shape[1], a.shape[0] * LANES
    else:
        r, ka = a.shape
    nb = b.shape[1]
    assert b.shape[0] == r
    ta, tb, tr = _tile(ka, ta_cap, LANES), _tile(nb, tb_cap, LANES), _tile(r, tr_cap, SUBLANES)

    def body(a_ref, b_ref, o_ref):
        @pl.when(pl.program_id(2) == 0)
        def _():
            o_ref[...] = jnp.zeros_like(o_ref)

        o_ref[...] += lax.dot_general(_load_slabs(a_ref, a_slabs).astype(BF16), b_ref[...].astype(BF16),
                                      (((0,), (0,)), ((), ())), preferred_element_type=F32)

    a_spec = (pl.BlockSpec((ta // LANES, tr, LANES), lambda i, j, kk: (i, kk, 0)) if a_slabs
              else pl.BlockSpec((tr, ta), lambda i, j, kk: (kk, i)))
    return pl.pallas_call(
        body, name=name, grid=(ka // ta, nb // tb, r // tr),
        in_specs=[a_spec, pl.BlockSpec((tr, tb), lambda i, j, kk: (kk, j))],
        out_specs=pl.BlockSpec((ta, tb), lambda i, j, kk: (i, j)),
        out_shape=jax.ShapeDtypeStruct((ka, nb), F32),
        compiler_params=_params("parallel", "parallel", "arbitrary"),
    )(a, b)


def _core_major(k):
    return (k % 2) * (N_DEV // 2) + k // 2


def _matmul_tn_into(buf, a, b, row_off, *, name, a_slabs=False, tr_cap=1024):
    if a_slabs:
        r, ka = a.shape[1], a.shape[0] * LANES
    else:
        r, ka = a.shape
    n_dev, _, n = buf.shape
    per = ka // n_dev
    assert b.shape == (r, n) and ka % n_dev == 0 and per % SUBLANES == 0 and row_off % per == 0
    tr = _tile(r, tr_cap, SUBLANES)

    def body(buf_ref, a_ref, b_ref, o_ref):
        prod = lax.dot_general(_load_slabs(a_ref, a_slabs).astype(BF16), b_ref[...].astype(BF16),
                               (((0,), (0,)), ((), ())), preferred_element_type=F32)
        @pl.when(pl.program_id(0) == 0)
        def _():
            for k in range(n_dev):
                o_ref[_core_major(k)] = prod[k * per:(k + 1) * per]

        @pl.when(pl.program_id(0) > 0)
        def _():
            for k in range(n_dev):
                o_ref[_core_major(k)] += prod[k * per:(k + 1) * per]

    a_spec = (pl.BlockSpec((ka // LANES, tr, LANES), lambda i: (0, i, 0)) if a_slabs
              else pl.BlockSpec((tr, ka), lambda i: (i, 0)))
    return pl.pallas_call(
        body, name=name, grid=(r // tr,),
        in_specs=[pl.BlockSpec(memory_space=pl.ANY), a_spec, pl.BlockSpec((tr, n), lambda i: (i, 0))],
        out_specs=pl.BlockSpec((n_dev, per, n), lambda i: (0, row_off // per, 0)),
        out_shape=jax.ShapeDtypeStruct(buf.shape, F32),
        input_output_aliases={0: 0},
        compiler_params=_params("arbitrary"),
    )(buf, a, b)


def _rms_fwd(x, gain, *, name):
    t, d = x.shape
    tm = _tile(t, 512, SUBLANES)

    def body(x_ref, g_ref, o_ref):
        xv = x_ref[...]
        r = lax.rsqrt(jnp.mean(xv * xv, axis=-1, keepdims=True) + NORM_EPS)
        o_ref[...] = (xv * r * g_ref[...]).astype(BF16)

    return pl.pallas_call(
        body, name=name, grid=(t // tm,),
        in_specs=[pl.BlockSpec((tm, d), lambda i: (i, 0)), pl.BlockSpec((1, d), lambda i: (0, 0))],
        out_specs=pl.BlockSpec((tm, d), lambda i: (i, 0)),
        out_shape=jax.ShapeDtypeStruct((t, d), BF16),
        compiler_params=_params("parallel"),
    )(x, gain)


def _rms_bwd(x, gain, dh, dres, *, name):
    t, d = x.shape
    tm = _tile(t, 512, SUBLANES)

    def body(x_ref, g_ref, dh_ref, dres_ref, dx_ref, dg_ref):
        @pl.when(pl.program_id(0) == 0)
        def _():
            dg_ref[...] = jnp.zeros_like(dg_ref)

        xv = x_ref[...]
        r = lax.rsqrt(jnp.mean(xv * xv, axis=-1, keepdims=True) + NORM_EPS)
        xhat = xv * r
        dhv = dh_ref[...]
        u = dhv * g_ref[...]
        dx_ref[...] = dres_ref[...] + r * (u - xhat * jnp.mean(u * xhat, axis=-1, keepdims=True))
        dg_ref[...] += jnp.sum(dhv * xhat, axis=0, keepdims=True)

    return pl.pallas_call(
        body, name=name, grid=(t // tm,),
        in_specs=[pl.BlockSpec((tm, d), lambda i: (i, 0)), pl.BlockSpec((1, d), lambda i: (0, 0)),
                  pl.BlockSpec((tm, d), lambda i: (i, 0)), pl.BlockSpec((tm, d), lambda i: (i, 0))],
        out_specs=[pl.BlockSpec((tm, d), lambda i: (i, 0)), pl.BlockSpec((1, d), lambda i: (0, 0))],
        out_shape=[jax.ShapeDtypeStruct((t, d), F32), jax.ShapeDtypeStruct((1, d), F32)],
        compiler_params=_params("arbitrary"),
    )(x, gain, dh, dres)


def _loss_head(y, target, *, name):
    t, d = y.shape
    tm = _tile(t, 512, SUBLANES)

    def body(y_ref, t_ref, dy_ref, l_ref):
        @pl.when(pl.program_id(0) == 0)
        def _():
            l_ref[...] = jnp.zeros_like(l_ref)

        e = y_ref[...] - t_ref[...]
        dy_ref[...] = e * (1.0 / d)
        l_ref[...] += jnp.sum(e * e, axis=0, keepdims=True) * (0.5 / d)

    return pl.pallas_call(
        body, name=name, grid=(t // tm,),
        in_specs=[pl.BlockSpec((tm, d), lambda i: (i, 0)), pl.BlockSpec((tm, d), lambda i: (i, 0))],
        out_specs=[pl.BlockSpec((tm, d), lambda i: (i, 0)), pl.BlockSpec((1, d), lambda i: (0, 0))],
        out_shape=[jax.ShapeDtypeStruct((t, d), F32), jax.ShapeDtypeStruct((1, d), F32)],
        compiler_params=_params("arbitrary"),
    )(y, target)


def _sigmoid(v):
    return 0.5 * jnp.tanh(0.5 * v) + 0.5


FFN_TOKEN_TILE = 512
FFN_HIDDEN_TILE = 1408
NT_DIMS = (((1,), (1,)), ((), ()))


def _ffn_up(h, w_gate_t, w_up_t, *, name):
    t, d = h.shape
    f = w_gate_t.shape[0]
    tm, tn = _tile(t, FFN_TOKEN_TILE, SUBLANES), _tile(f, FFN_HIDDEN_TILE, LANES)

    def body(h_ref, g_ref, u_ref, s_ref, a_ref, b_ref):
        hv = h_ref[...].astype(BF16)
        av = lax.dot_general(hv, g_ref[...].astype(BF16), NT_DIMS, preferred_element_type=F32)
        bv = lax.dot_general(hv, u_ref[...].astype(BF16), NT_DIMS, preferred_element_type=F32)
        s_ref[...] = (av * _sigmoid(av) * bv).astype(BF16)
        a_ref[...] = av.astype(BF16)
        b_ref[...] = bv.astype(BF16)

    w_spec = pl.BlockSpec((tn, d), lambda j, i: (j, 0))
    out_spec = pl.BlockSpec((tm, tn), lambda j, i: (i, j))
    out = jax.ShapeDtypeStruct((t, f), BF16)
    return pl.pallas_call(
        body, name=name, grid=(f // tn, t // tm),
        in_specs=[pl.BlockSpec((tm, d), lambda j, i: (i, 0)), w_spec, w_spec],
        out_specs=[out_spec, out_spec, out_spec], out_shape=[out, out, out],
        compiler_params=_params("parallel", "parallel"),
    )(h, w_gate_t, w_up_t)


def _ffn_dact(dx, w_down, a, b, *, name):
    t, d = dx.shape
    f = w_down.shape[0]
    tm, tn = _tile(t, FFN_TOKEN_TILE, SUBLANES), _tile(f, FFN_HIDDEN_TILE, LANES)

    def body(dx_ref, w_ref, a_ref, b_ref, da_ref, db_ref):
        ds = lax.dot_general(dx_ref[...].astype(BF16), w_ref[...].astype(BF16), NT_DIMS, preferred_element_type=F32)
        av = a_ref[...].astype(F32)
        sg = _sigmoid(av)
        da_ref[...] = (ds * b_ref[...].astype(F32) * (sg * (1.0 + av * (1.0 - sg)))).astype(BF16)
        db_ref[...] = (ds * av * sg).astype(BF16)

    blk = pl.BlockSpec((tm, tn), lambda j, i: (i, j))
    out = jax.ShapeDtypeStruct((t, f), BF16)
    return pl.pallas_call(
        body, name=name, grid=(f // tn, t // tm),
        in_specs=[pl.BlockSpec((tm, d), lambda j, i: (i, 0)), pl.BlockSpec((tn, d), lambda j, i: (j, 0)), blk, blk],
        out_specs=[blk, blk], out_shape=[out, out],
        compiler_params=_params("parallel", "parallel"),
    )(dx, w_down, a, b)


def _ffn_fwd(x, gain, w_gate_t, w_up_t, w_down, tag):
    h = _rms_fwd(x, gain, name=f"ffn_norm_{tag}")
    s, a, b = _ffn_up(h, w_gate_t, w_up_t, name=f"ffn_up_{tag}")
    x_new = _matmul(s, w_down, "nn", resid=x, tn_cap=1024, tk_cap=2816, name=f"ffn_down_{tag}")
    return x_new, (x, h, a, b, s)


def _ffn_bwd(dx, saved, gain, w_gate_t, w_up_t, w_down, grads, rows, tag):
    x, h, a, b, s = saved
    da, db = _ffn_dact(dx, w_down, a, b, name=f"ffn_dact_{tag}")
    grads = _matmul_tn_into(grads, da, h, rows[0], name=f"ffn_dwgate_{tag}")
    grads = _matmul_tn_into(grads, db, h, rows[1], name=f"ffn_dwup_{tag}")
    grads = _matmul_tn_into(grads, s, dx, rows[2], name=f"ffn_dwdown_{tag}")
    dh = _matmul([da, db], [w_gate_t, w_up_t], "nn", tm_cap=512, tn_cap=1024, tk_cap=2816, name=f"ffn_dh_{tag}")
    dx_in, dgain = _rms_bwd(x, gain, dh, dx, name=f"ffn_dnorm_{tag}")
    return dx_in, grads, dgain


POOL_HALO = 16


def _shift_rows(v, k):
    n = v.shape[0]
    return pltpu.roll(v, k % n, 0)


def _window_sum(v, w, direction):
    k = 1
    while k < w:
        v = v + _shift_rows(v, direction * k)
        k *= 2
    return v


def _pool_fwd(u, x, w_group, scale, *, name):
    t, d = u.shape
    ng, dg = w_group.shape[0], w_group.shape[1]
    tm = _tile(t, 512, POOL_HALO)
    hb = tm // POOL_HALO

    def body(u_ref, halo_ref, x_ref, w_ref, s_ref, xo_ref, p_ref, y_ref):
        i, g = pl.program_id(0), pl.program_id(1)
        halo = jnp.where(i > 0, halo_ref[...], 0.0)
        ext = jnp.concatenate([halo, u_ref[...]], axis=0)
        pos = i * tm + lax.broadcasted_iota(jnp.int32, (tm, 1), 0)
        for gi, win in enumerate(POOL_WINDOWS):
            @pl.when(g == gi)
            def _(win=win):
                tot = _window_sum(ext, win, 1)[POOL_HALO:]
                cnt = jnp.minimum(pos + 1, win).astype(F32)
                p = (tot / cnt - u_ref[...]).astype(BF16)
                p_ref[...] = p
                y = jnp.dot(p, w_ref[...].astype(BF16), preferred_element_type=F32)
                y_ref[...] = y
                xo_ref[...] = x_ref[...] + y * s_ref[...]

    blk = pl.BlockSpec((tm, dg), lambda i, g: (i, g))
    return pl.pallas_call(
        body, name=name, grid=(t // tm, ng),
        in_specs=[blk, pl.BlockSpec((POOL_HALO, dg), lambda i, g: (jnp.maximum(i * hb - 1, 0), g)), blk,
                  pl.BlockSpec((None, dg, dg), lambda i, g: (g, 0, 0)), pl.BlockSpec((1, dg), lambda i, g: (0, g))],
        out_specs=[blk, blk, blk],
        out_shape=[jax.ShapeDtypeStruct((t, d), F32), jax.ShapeDtypeStruct((t, d), BF16),
                   jax.ShapeDtypeStruct((t, d), F32)],
        compiler_params=_params("parallel", "parallel"),
    )(u, u, x, w_group, scale)


def _pool_bwd(dx, p, y_pre, w_group, scale, *, name):
    t, d = dx.shape
    ng, dg = w_group.shape[0], w_group.shape[1]
    tm = _tile(t, 512, POOL_HALO)
    hb = tm // POOL_HALO
    nt = t // tm

    def body(dx_ref, nxt_ref, p_ref, y_ref, w_ref, s_ref, du_ref, dw_ref, ds_ref):
        g, i = pl.program_id(0), pl.program_id(1)

        @pl.when(i == 0)
        def _():
            dw_ref[...] = jnp.zeros_like(dw_ref)
            ds_ref[...] = jnp.zeros_like(ds_ref)

        dxv = dx_ref[...]
        ds_ref[...] += jnp.sum(dxv * y_ref[...], axis=0, keepdims=True)
        nxt = jnp.where(i < nt - 1, nxt_ref[...], 0.0)
        dyp = (jnp.concatenate([dxv, nxt], axis=0) * s_ref[...]).astype(BF16)
        dw_ref[...] += lax.dot_general(p_ref[...], dyp[:tm], (((0,), (0,)), ((), ())), preferred_element_type=F32)
        dp = lax.dot_general(dyp, w_ref[...].astype(BF16), (((1,), (1,)), ((), ())), preferred_element_type=F32)
        pos = i * tm + lax.broadcasted_iota(jnp.int32, (tm + POOL_HALO, 1), 0)
        for gi, win in enumerate(POOL_WINDOWS):
            @pl.when(g == gi)
            def _(win=win):
                q = dp / jnp.minimum(pos + 1, win).astype(F32)
                du_ref[...] = (_window_sum(q, win, -1)[:tm] - dp[:tm]).astype(BF16)

    blk = pl.BlockSpec((tm, dg), lambda g, i: (i, g))
    return pl.pallas_call(
        body, name=name, grid=(ng, nt),
        in_specs=[blk, pl.BlockSpec((POOL_HALO, dg), lambda g, i: (jnp.minimum((i + 1) * hb, t // POOL_HALO - 1), g)),
                  blk, blk, pl.BlockSpec((None, dg, dg), lambda g, i: (g, 0, 0)),
                  pl.BlockSpec((1, dg), lambda g, i: (0, g))],
        out_specs=[blk, pl.BlockSpec((None, dg, dg), lambda g, i: (g, 0, 0)), pl.BlockSpec((1, dg), lambda g, i: (0, g))],
        out_shape=[jax.ShapeDtypeStruct((t, d), BF16), jax.ShapeDtypeStruct((ng, dg, dg), F32),
                   jax.ShapeDtypeStruct((1, d), F32)],
        compiler_params=_params("parallel", "arbitrary"),
    )(dx, dx, p, y_pre, w_group, scale)


def _pool_mixer_fwd(x, gain, w_in, w_group, scale, tag):
    h = _rms_fwd(x, gain, name=f"pool_norm_{tag}")
    u = _matmul(h, w_in, "nn", name=f"pool_in_{tag}")
    x_new, p, y_pre = _pool_fwd(u, x, w_group, scale, name=f"pool_mix_{tag}")
    return x_new, (x, h, p, y_pre)


def _pool_mixer_bwd(dx, saved, gain, w_in, w_group, scale, grads, row_in, tag):
    x, h, p, y_pre = saved
    du, dw_group, dscale = _pool_bwd(dx, p, y_pre, w_group, scale, name=f"pool_dmix_{tag}")
    grads = _matmul_tn_into(grads, h, du, row_in, name=f"pool_dwin_{tag}")
    dh = _matmul(du, w_in, "nt", name=f"pool_dh_{tag}")
    dx_in, dgain = _rms_bwd(x, gain, dh, dx, name=f"pool_dnorm_{tag}")
    return dx_in, grads, dw_group, dscale, dgain


CONV_HALO = 8
NEG_BIG = -1e30


def _softplus(v):
    return jnp.maximum(v, 0.0) + jnp.log(1.0 + jnp.exp(-jnp.abs(v)))


def _conv_taps(ext, w_ref, off, rows):
    acc = None
    for k in range(SSD_CONV):
        shift = SSD_CONV - 1 - k
        v = (_shift_rows(ext, shift) if shift else ext)[off:off + rows] * w_ref[k:k + 1, :]
        acc = v if acc is None else acc + v
    return acc


def _ssd_conv_fwd(zx, conv_w, conv_b, col0, *, name):
    t = zx.shape[0]
    c = conv_w.shape[1]
    tm, tc = _tile(t, 512, CONV_HALO), _tile(c, 512, LANES)
    hb, cb0 = tm // CONV_HALO, col0 // tc
    assert col0 % tc == 0

    def body(x_ref, halo_ref, w_ref, b_ref, o_ref):
        halo = jnp.where(pl.program_id(0) > 0, halo_ref[...], 0.0)
        ext = jnp.concatenate([halo, x_ref[...]], axis=0)
        pre = _conv_taps(ext, w_ref, CONV_HALO, tm) + b_ref[...]
        o_ref[...] = pre * _sigmoid(pre)

    return pl.pallas_call(
        body, name=name, grid=(t // tm, c // tc),
        in_specs=[pl.BlockSpec((tm, tc), lambda i, j: (i, j + cb0)),
                  pl.BlockSpec((CONV_HALO, tc), lambda i, j: (jnp.maximum(i * hb - 1, 0), j + cb0)),
                  pl.BlockSpec((SSD_CONV, tc), lambda i, j: (0, j)), pl.BlockSpec((1, tc), lambda i, j: (0, j))],
        out_specs=pl.BlockSpec((tm, tc), lambda i, j: (i, j)),
        out_shape=jax.ShapeDtypeStruct((t, c), F32),
        compiler_params=_params("parallel", "parallel"),
    )(zx, zx, conv_w, conv_b)


def _ssd_conv_bwd(d_parts, zx, conv_w, conv_b, col0, *, name):
    t = zx.shape[0]
    c = conv_w.shape[1]
    tm, tc = _tile(t, 512, CONV_HALO), _tile(c, 512, LANES)
    hb, cb0, nt = tm // CONV_HALO, col0 // tc, t // tm
    last_halo = t // CONV_HALO - 1
    starts = [0]
    for part in d_parts:
        assert part.shape[1] % tc == 0
        starts.append(starts[-1] + part.shape[1] // tc)
    assert starts[-1] == c // tc
    n_parts = len(d_parts)

    def pick(refs, j):
        value = refs[-1][...]
        for p in reversed(range(n_parts - 1)):
            value = jnp.where(j < starts[p + 1], refs[p][...], value)
        return value

    def body(x_ref, prev_ref, nxt_ref, *rest):
        d_refs, dnxt_refs = rest[:n_parts], rest[n_parts:2 * n_parts]
        w_ref, b_ref, dx_ref, dw_ref, db_ref = rest[2 * n_parts:]
        j, i = pl.program_id(0), pl.program_id(1)

        @pl.when(i == 0)
        def _():
            dw_ref[...] = jnp.zeros_like(dw_ref)
            db_ref[...] = jnp.zeros_like(db_ref)

        prev = jnp.where(i > 0, prev_ref[...], 0.0)
        has_next = i < nt - 1
        ext = jnp.concatenate([prev, x_ref[...], jnp.where(has_next, nxt_ref[...], 0.0)], axis=0)
        pre = _conv_taps(ext, w_ref, CONV_HALO, tm + CONV_HALO) + b_ref[...]
        sg = _sigmoid(pre)
        dact = jnp.concatenate([pick(d_refs, j), jnp.where(has_next, pick(dnxt_refs, j), 0.0)], axis=0)
        dpre = dact * (sg * (1.0 + pre * (1.0 - sg)))
        db_ref[...] += jnp.sum(dpre[:tm], axis=0, keepdims=True)
        acc = None
        for k in range(SSD_CONV):
            shift = SSD_CONV - 1 - k
            src = (_shift_rows(ext, shift) if shift else ext)[CONV_HALO:CONV_HALO + tm]
            dw_ref[k:k + 1, :] += jnp.sum(dpre[:tm] * src, axis=0, keepdims=True)
            v = (_shift_rows(dpre, -shift) if shift else dpre)[:tm] * w_ref[k:k + 1, :]
            acc = v if acc is None else acc + v
        dx_ref[...] = acc.astype(BF16)

    def part_specs(rows, row_index):
        def spec(p):
            def index(j, i):
                mine = (j >= starts[p]) & (j < starts[p + 1])
                return jnp.where(mine, row_index(i), 0), jnp.where(mine, j - starts[p], 0)
            return pl.BlockSpec((rows, tc), index)
        return [spec(p) for p in range(n_parts)]

    main = lambda j, i: (i, j + cb0)
    next_halo = lambda i: jnp.minimum((i + 1) * hb, last_halo)
    return pl.pallas_call(
        body, name=name, grid=(c // tc, nt),
        in_specs=[pl.BlockSpec((tm, tc), main),
                  pl.BlockSpec((CONV_HALO, tc), lambda j, i: (jnp.maximum(i * hb - 1, 0), j + cb0)),
                  pl.BlockSpec((CONV_HALO, tc), lambda j, i: (next_halo(i), j + cb0)),
                  *part_specs(tm, lambda i: i), *part_specs(CONV_HALO, next_halo),
                  pl.BlockSpec((SSD_CONV, tc), lambda j, i: (0, j)), pl.BlockSpec((1, tc), lambda j, i: (0, j))],
        out_specs=[pl.BlockSpec((tm, tc), lambda j, i: (i, j)), pl.BlockSpec((SSD_CONV, tc), lambda j, i: (0, j)),
                   pl.BlockSpec((1, tc), lambda j, i: (0, j))],
        out_shape=[jax.ShapeDtypeStruct((t, c), BF16), jax.ShapeDtypeStruct((SSD_CONV, c), F32),
                   jax.ShapeDtypeStruct((1, c), F32)],
        compiler_params=_params("parallel", "arbitrary"),
    )(zx, zx, zx, *d_parts, *d_parts, conv_w, conv_b)


SSD_CUMSUM_PIECES = 2
SSD_GROUPS_PER_STEP = 1


def _ssd_group_pad(v, n_groups):
    lead = v.shape[:-1]
    v = v.reshape(*lead, n_groups, SSD_HEADS_PER_GROUP)
    v = jnp.pad(v, [(0, 0)] * (len(lead) + 1) + [(0, LANES - SSD_HEADS_PER_GROUP)])
    return v.reshape(*lead, n_groups * LANES)


def _ssd_group_unpad(v, n_groups):
    lead = v.shape[:-1]
    return v.reshape(*lead, n_groups, LANES)[..., :SSD_HEADS_PER_GROUP].reshape(*lead, -1)


def _ssd_chunk_common(dtp_ref, par_ref):
    ell = SSD_CHUNK
    dt = _softplus(dtp_ref[...] + par_ref[0:1, :])
    a = -jnp.exp(par_ref[1:2, :])
    row = lax.broadcasted_iota(jnp.int32, (ell, ell), 0)
    col = lax.broadcasted_iota(jnp.int32, (ell, ell), 1)
    acum = _split_dot(dt * a, (row >= col).astype(BF16), SSD_CUMSUM_PIECES, left=True)
    return dt, a, acum, acum.T, row, col


def _ssd_scan_fwd(xa, dtp, par, n_groups, *, name):
    t = xa.shape[0]
    ell, hd, hpg, ns, gps = SSD_CHUNK, SSD_HEAD_DIM, SSD_HEADS_PER_GROUP, SSD_STATE, SSD_GROUPS_PER_STEP
    gw = hpg * hd
    nc = t // ell
    b_blk0, c_blk0 = n_groups * gw // (ns * gps), (n_groups * gw // ns + n_groups) // gps

    def body(xs_ref, b_ref, c_ref, dtp_ref, par_ref, y_ref, sin_ref, st):
        @pl.when(pl.program_id(1) == 0)
        def _():
            st[...] = jnp.zeros_like(st)

        dt, _, acum, acum_t, row, col = _ssd_chunk_common(dtp_ref, par_ref)
        for gi in range(gps):
            bb = b_ref[:, gi * ns:(gi + 1) * ns].astype(BF16)
            cc = c_ref[:, gi * ns:(gi + 1) * ns].astype(BF16)
            cb = lax.dot_general(cc, bb, NT_DIMS, preferred_element_type=F32)
            s_all = st[gi]
            sin_ref[gi] = s_all
            c_s = lax.dot_general(cc, s_all.astype(BF16), NT_DIMS, preferred_element_type=F32)
            weighted, keep = [], []
            for hh in range(hpg):
                lanes = slice(gi * gw + hh * hd, gi * gw + (hh + 1) * hd)
                hl = gi * LANES + hh
                col_a, row_a = acum[:, hl:hl + 1], acum_t[hl:hl + 1, :]
                decay = jnp.exp(jnp.where(row >= col, col_a - row_a, NEG_BIG))
                xdt = xs_ref[:, lanes] * dt[:, hl:hl + 1]
                y = jnp.dot((cb * decay).astype(BF16), xdt.astype(BF16), preferred_element_type=F32)
                y_ref[:, lanes] = y + jnp.exp(col_a) * c_s[:, hh * hd:(hh + 1) * hd]
                a_last = acum[ell - 1:ell, hl:hl + 1]
                weighted.append((xdt * jnp.exp(a_last - col_a)).astype(BF16))
                keep.append(jnp.broadcast_to(jnp.exp(a_last), (hd, 1)))
            st[gi] = jnp.concatenate(keep, axis=0) * s_all + lax.dot_general(
                jnp.concatenate(weighted, axis=1), bb, (((0,), (0,)), ((), ())), preferred_element_type=F32)

    return pl.pallas_call(
        body, name=name, grid=(n_groups // gps, nc),
        in_specs=[pl.BlockSpec((ell, gps * gw), lambda g, c: (c, g)),
                  pl.BlockSpec((ell, gps * ns), lambda g, c: (c, b_blk0 + g)),
                  pl.BlockSpec((ell, gps * ns), lambda g, c: (c, c_blk0 + g)),
                  pl.BlockSpec((ell, gps * LANES), lambda g, c: (c, g)),
                  pl.BlockSpec((SUBLANES, gps * LANES), lambda g, c: (0, g))],
        out_specs=[pl.BlockSpec((ell, gps * gw), lambda g, c: (c, g)),
                   pl.BlockSpec((None, gps, gw, ns), lambda g, c: (c, g, 0, 0))],
        out_shape=[jax.ShapeDtypeStruct((t, n_groups * gw), F32),
                   jax.ShapeDtypeStruct((nc, n_groups, gw, ns), F32)],
        scratch_shapes=[pltpu.VMEM((gps, gw, ns), F32)],
        compiler_params=_params("parallel", "arbitrary"),
    )(xa, xa, xa, dtp, par)


def _ssd_scan_bwd(dy, xa, dtp, par, s_in, n_groups, *, name):
    t = xa.shape[0]
    ell, hd, hpg, ns, gps = SSD_CHUNK, SSD_HEAD_DIM, SSD_HEADS_PER_GROUP, SSD_STATE, SSD_GROUPS_PER_STEP
    gw = hpg * hd
    nc = t // ell
    b_blk0, c_blk0 = n_groups * gw // (ns * gps), (n_groups * gw // ns + n_groups) // gps
    nt_dims = (((1,), (1,)), ((), ()))
    tn_dims = (((0,), (0,)), ((), ()))

    def body(dy_ref, xs_ref, b_ref, c_ref, dtp_ref, par_ref, sin_ref,
             dxs_ref, db_ref, dc_ref, ddtp_ref, dpar_ref, dst):
        @pl.when(pl.program_id(1) == 0)
        def _():
            dst[...] = jnp.zeros_like(dst)
            dpar_ref[...] = jnp.zeros_like(dpar_ref)

        dtg, a_g, acum, acum_t, row, col = _ssd_chunk_common(dtp_ref, par_ref)
        lane = lax.broadcasted_iota(jnp.int32, (1, gps * LANES), 1)
        dacum = jnp.zeros((ell, gps * LANES), F32)
        xsum = jnp.zeros((ell, gps * LANES), F32)
        dsum = jnp.zeros((1, gps * LANES), F32)
        for gi, hh in [(gi, hh) for gi in range(gps) for hh in range(hpg)]:
            if hh == 0:
                bb = b_ref[:, gi * ns:(gi + 1) * ns].astype(BF16)
                cc = c_ref[:, gi * ns:(gi + 1) * ns].astype(BF16)
                cb = lax.dot_general(cc, bb, nt_dims, preferred_element_type=F32)
                cb_t = lax.dot_general(bb, cc, nt_dims, preferred_element_type=F32)
                dcb = jnp.zeros((ell, ell), F32)
                dcb_t = jnp.zeros((ell, ell), F32)
                s_all, ds_all = sin_ref[gi], dst[gi]
                c_s_all = lax.dot_general(cc, s_all.astype(BF16), nt_dims, preferred_element_type=F32)
                b_ds_all = lax.dot_general(bb, ds_all.astype(BF16), nt_dims, preferred_element_type=F32)
                s_ds = jnp.sum(s_all * ds_all, axis=1, keepdims=True)
                dy_decayed, x_weighted, keep = [], [], []
            lanes = slice(gi * gw + hh * hd, gi * gw + (hh + 1) * hd)
            head = slice(hh * hd, (hh + 1) * hd)
            hl = gi * LANES + hh
            onehot = (lane == hl).astype(F32)
            col_a, row_a = acum[:, hl:hl + 1], acum_t[hl:hl + 1, :]
            decay = jnp.exp(jnp.where(row >= col, col_a - row_a, NEG_BIG))
            decay_t = jnp.exp(jnp.where(col >= row, row_a - col_a, NEG_BIG))
            e_col = jnp.exp(col_a)
            a_last = acum[ell - 1:ell, hl:hl + 1]
            w = jnp.exp(a_last - col_a)
            e_last = jnp.exp(a_last)
            xs_h, dy_h = xs_ref[:, lanes], dy_ref[:, lanes]
            dt_h = dtg[:, hl:hl + 1]
            xdt = xs_h * dt_h
            xdt_b, dy_b = xdt.astype(BF16), dy_h.astype(BF16)
            dm_decay = lax.dot_general(dy_b, xdt_b, nt_dims, preferred_element_type=F32) * decay
            dm_decay_t = lax.dot_general(xdt_b, dy_b, nt_dims, preferred_element_type=F32) * decay_t
            dcb += dm_decay
            dcb_t += dm_decay_t
            m_t = cb_t * decay_t
            dac = jnp.sum(dm_decay * cb, axis=1, keepdims=True) - jnp.sum(dm_decay_t * cb_t, axis=1, keepdims=True)
            b_ds = b_ds_all[:, head]
            dxdt = jnp.dot(m_t.astype(BF16), dy_b, preferred_element_type=F32) + w * b_ds
            dac += jnp.sum(dy_h * c_s_all[:, head], axis=1, keepdims=True) * e_col
            q = jnp.sum(xdt * b_ds, axis=1, keepdims=True) * w
            dac -= q
            d_last = jnp.sum(q, axis=0, keepdims=True) + e_last * jnp.sum(s_ds[head], axis=0, keepdims=True)
            is_last = lax.broadcasted_iota(jnp.int32, (ell, 1), 0) == ell - 1
            dac += jnp.where(is_last, d_last, 0.0)
            dacum += dac * onehot
            dy_decayed.append((dy_h * e_col).astype(BF16))
            x_weighted.append((xdt * w).astype(BF16))
            keep.append(jnp.broadcast_to(e_last, (hd, 1)))
            dxs_ref[:, lanes] = dxdt * dt_h + dy_h * par_ref[2:3, hl:hl + 1]
            xsum += jnp.sum(dxdt * xs_h, axis=1, keepdims=True) * onehot
            dsum += jnp.sum(jnp.sum(dy_h * xs_h, axis=1, keepdims=True), axis=0, keepdims=True) * onehot
            if hh == hpg - 1:
                group = slice(gi * ns, (gi + 1) * ns)
                dy_all, x_all = jnp.concatenate(dy_decayed, axis=1), jnp.concatenate(x_weighted, axis=1)
                dc_ref[:, group] = (jnp.dot(dy_all, s_all.astype(BF16), preferred_element_type=F32)
                                    + jnp.dot(dcb.astype(BF16), bb, preferred_element_type=F32))
                db_ref[:, group] = (jnp.dot(x_all, ds_all.astype(BF16), preferred_element_type=F32)
                                    + jnp.dot(dcb_t.astype(BF16), cc, preferred_element_type=F32))
                dst[gi] = jnp.concatenate(keep, axis=0) * ds_all + lax.dot_general(
                    dy_all, cc, tn_dims, preferred_element_type=F32)
        dda = _split_dot(dacum, (col >= row).astype(BF16), SSD_CUMSUM_PIECES, left=True)
        ddtp = (xsum + dda * a_g) * _sigmoid(dtp_ref[...] + par_ref[0:1, :])
        ddtp_ref[...] = ddtp
        dpar_ref[0:1, :] += jnp.sum(ddtp, axis=0, keepdims=True)
        dpar_ref[1:2, :] += jnp.sum(dda * dtg, axis=0, keepdims=True) * a_g
        dpar_ref[2:3, :] += dsum

    rev = lambda i: nc - 1 - i
    return pl.pallas_call(
        body, name=name, grid=(n_groups // gps, nc),
        in_specs=[pl.BlockSpec((ell, gps * gw), lambda g, i: (rev(i), g)),
                  pl.BlockSpec((ell, gps * gw), lambda g, i: (rev(i), g)),
                  pl.BlockSpec((ell, gps * ns), lambda g, i: (rev(i), b_blk0 + g)),
                  pl.BlockSpec((ell, gps * ns), lambda g, i: (rev(i), c_blk0 + g)),
                  pl.BlockSpec((ell, gps * LANES), lambda g, i: (rev(i), g)),
                  pl.BlockSpec((SUBLANES, gps * LANES), lambda g, i: (0, g)),
                  pl.BlockSpec((None, gps, gw, ns), lambda g, i: (rev(i), g, 0, 0))],
        out_specs=[pl.BlockSpec((ell, gps * gw), lambda g, i: (rev(i), g)),
                   pl.BlockSpec((ell, gps * ns), lambda g, i: (rev(i), g)),
                   pl.BlockSpec((ell, gps * ns), lambda g, i: (rev(i), g)),
                   pl.BlockSpec((ell, gps * LANES), lambda g, i: (rev(i), g)),
                   pl.BlockSpec((SUBLANES, gps * LANES), lambda g, i: (0, g))],
        out_shape=[jax.ShapeDtypeStruct((t, n_groups * gw), F32), jax.ShapeDtypeStruct((t, n_groups * ns), F32),
                   jax.ShapeDtypeStruct((t, n_groups * ns), F32), jax.ShapeDtypeStruct((t, n_groups * LANES), F32),
                   jax.ShapeDtypeStruct((SUBLANES, n_groups * LANES), F32)],
        scratch_shapes=[pltpu.VMEM((gps, gw, ns), F32)],
        compiler_params=_params("parallel", "arbitrary"),
    )(dy, xa, xa, xa, dtp, par, s_in)


def _ssd_gate_fwd(y, xa, zx, d_rep, out_norm, *, name):
    t, di = y.shape
    gw = SSD_HEADS_PER_GROUP * SSD_HEAD_DIM
    tm = _tile(t, 512, SUBLANES)

    def body(y_ref, xs_ref, z_ref, d_ref, n_ref, o_ref):
        zv = z_ref[...]
        gt = (y_ref[...] + d_ref[...] * xs_ref[...]) * (zv * _sigmoid(zv))
        r = lax.rsqrt(jnp.mean(gt * gt, axis=-1, keepdims=True) + NORM_EPS)
        o_ref[...] = (gt * r * n_ref[...]).astype(BF16)

    blk = pl.BlockSpec((tm, gw), lambda i, g: (i, g))
    vec = pl.BlockSpec((1, gw), lambda i, g: (0, g))
    return pl.pallas_call(
        body, name=name, grid=(t // tm, di // gw),
        in_specs=[blk, blk, blk, vec, vec], out_specs=blk,
        out_shape=jax.ShapeDtypeStruct((t, di), BF16),
        compiler_params=_params("parallel", "parallel"),
    )(y, xa, zx, d_rep, out_norm)


def _ssd_gate_bwd(dgn, y, xa, zx, d_rep, out_norm, *, name):
    t, di = y.shape
    gw = SSD_HEADS_PER_GROUP * SSD_HEAD_DIM
    tm = _tile(t, 512, SUBLANES)

    def body(dg_ref, y_ref, xs_ref, z_ref, d_ref, n_ref, dy_ref, dz_ref, dn_ref):
        @pl.when(pl.program_id(1) == 0)
        def _():
            dn_ref[...] = jnp.zeros_like(dn_ref)

        zv = z_ref[...]
        sg = _sigmoid(zv)
        sz = zv * sg
        y2 = y_ref[...] + d_ref[...] * xs_ref[...]
        gt = y2 * sz
        r = lax.rsqrt(jnp.mean(gt * gt, axis=-1, keepdims=True) + NORM_EPS)
        ghat = gt * r
        dgv = dg_ref[...]
        dn_ref[...] += jnp.sum(dgv * ghat, axis=0, keepdims=True)
        u = dgv * n_ref[...]
        dgt = r * (u - ghat * jnp.mean(u * ghat, axis=-1, keepdims=True))
        dy_ref[...] = dgt * sz
        dz_ref[...] = (dgt * y2 * (sg * (1.0 + zv * (1.0 - sg)))).astype(BF16)

    blk = pl.BlockSpec((tm, gw), lambda g, i: (i, g))
    vec = pl.BlockSpec((1, gw), lambda g, i: (0, g))
    return pl.pallas_call(
        body, name=name, grid=(di // gw, t // tm),
        in_specs=[blk, blk, blk, blk, vec, vec], out_specs=[blk, blk, vec],
        out_shape=[jax.ShapeDtypeStruct((t, di), F32), jax.ShapeDtypeStruct((t, di), BF16),
                   jax.ShapeDtypeStruct((1, di), F32)],
        compiler_params=_params("parallel", "arbitrary"),
    )(dgn, y, xa, zx, d_rep, out_norm)


def _ssd_mixer_fwd(x, gain, w_zx_t, w_dt_t, conv_w, conv_b, par, d_rep, out_norm, w_out, tag):
    di = w_out.shape[0]
    n_groups = di // (SSD_HEADS_PER_GROUP * SSD_HEAD_DIM)
    h = _rms_fwd(x, gain, name=f"ssd_norm_{tag}")
    zx = _matmul(h, w_zx_t, "nt", name=f"ssd_in_{tag}")
    dtp = _matmul(h, w_dt_t, "nt", name=f"ssd_dt_{tag}")
    xa = _ssd_conv_fwd(zx, conv_w, conv_b, di, name=f"ssd_conv_{tag}")
    y, s_in = _ssd_scan_fwd(xa, dtp, par, n_groups, name=f"ssd_scan_{tag}")
    gn = _ssd_gate_fwd(y, xa, zx, d_rep, out_norm, name=f"ssd_gate_{tag}")
    x_new = _matmul(gn, w_out, "nn", resid=x, name=f"ssd_out_{tag}")
    return x_new, (x, h, zx, dtp, xa, y, s_in, gn)


def _ssd_mixer_bwd(dx, saved, gain, w_zx_t, w_dt_t, conv_w, conv_b, par, d_rep, out_norm, w_out, grads, row_out,
                   tag):
    x, h, zx, dtp, xa, y, s_in, gn = saved
    di = w_out.shape[0]
    n_groups = di // (SSD_HEADS_PER_GROUP * SSD_HEAD_DIM)
    dgn = _matmul(dx, w_out, "nt", name=f"ssd_dgn_{tag}")
    grads = _matmul_tn_into(grads, gn, dx, row_out, name=f"ssd_dwout_{tag}")
    dy2, dz, dnorm = _ssd_gate_bwd(dgn, y, xa, zx, d_rep, out_norm, name=f"ssd_dgate_{tag}")
    dxs, db, dc, ddtp, dpar = _ssd_scan_bwd(dy2, xa, dtp, par, s_in, n_groups, name=f"ssd_dscan_{tag}")
    dxbc, dconv_w, dconv_b = _ssd_conv_bwd([dxs, db, dc], zx, conv_w, conv_b, di, name=f"ssd_dconv_{tag}")
    dzx = jnp.concatenate([dz, dxbc], axis=1)
    dw_zx_t = _matmul_tn(dzx, h, name=f"ssd_dwin_{tag}")
    dw_dt_t = _matmul_tn(ddtp, h, name=f"ssd_dwdt_{tag}")
    dh = _matmul(dzx, w_zx_t, "nn", name=f"ssd_dh_{tag}")
    dh = _matmul(ddtp, w_dt_t, "nn", resid=dh, name=f"ssd_dhdt_{tag}")
    dx_in, dgain = _rms_bwd(x, gain, dh, dx, name=f"ssd_dnorm_{tag}")
    return dx_in, grads, dw_zx_t, dw_dt_t, dconv_w, dconv_b, dpar, dnorm, dgain


HEAD_SUM_PIECES = 2


def _head_sums(v):
    row = lax.broadcasted_iota(jnp.int32, (LANES, LANES), 0)
    col = lax.broadcasted_iota(jnp.int32, (LANES, LANES), 1)
    same_head = (row // SB_HEAD_DIM == col // SB_HEAD_DIM).astype(BF16)
    return _split_dot(v, same_head, HEAD_SUM_PIECES)


def _sb_qk_norm_fwd(qkv, gains, *, name):
    ns, t, _ = qkv.shape
    per = ns // 3
    tm = _tile(t, 1024, SUBLANES)
    inv_sqrt_d = 1.0 / math.sqrt(SB_HEAD_DIM)

    def body(x_ref, g_ref, o_ref):
        kind = pl.program_id(0) // per
        xv = x_ref[...]

        @pl.when(kind == 2)
        def _():
            o_ref[...] = xv.astype(BF16)

        @pl.when(kind < 2)
        def _():
            ms = _head_sums(xv * xv) * (1.0 / SB_HEAD_DIM)
            y = xv * lax.rsqrt(ms + NORM_EPS) * g_ref[pl.ds(kind, 1), :]
            o_ref[...] = (y * jnp.where(kind == 0, inv_sqrt_d, 1.0)).astype(BF16)

    blk = pl.BlockSpec((None, tm, LANES), lambda s, i: (s, i, 0))
    return pl.pallas_call(
        body, name=name, grid=(ns, t // tm),
        in_specs=[blk, pl.BlockSpec((SUBLANES, LANES), lambda s, i: (0, 0))], out_specs=blk,
        out_shape=jax.ShapeDtypeStruct((ns, t, LANES), BF16),
        compiler_params=_params("parallel", "parallel"),
    )(qkv, gains)


def _sb_qk_norm_bwd(dq, dk, dv, qkv, gains, *, name):
    ns, t, _ = qkv.shape
    per = ns // 3
    tm = _tile(t, 1024, SUBLANES)
    inv_sqrt_d = 1.0 / math.sqrt(SB_HEAD_DIM)

    def body(dq_ref, dk_ref, dv_ref, x_ref, g_ref, o_ref, dg_ref):
        s = pl.program_id(0)
        kind = s // per

        @pl.when((s == 0) & (pl.program_id(1) == 0))
        def _():
            dg_ref[...] = jnp.zeros_like(dg_ref)

        @pl.when(kind == 2)
        def _():
            o_ref[...] = dv_ref[...].astype(BF16)

        @pl.when(kind < 2)
        def _():
            xv = x_ref[...]
            dy = jnp.where(kind == 0, dq_ref[...] * inv_sqrt_d, dk_ref[...])
            r = lax.rsqrt(_head_sums(xv * xv) * (1.0 / SB_HEAD_DIM) + NORM_EPS)
            xhat = xv * r
            u = dy * g_ref[pl.ds(kind, 1), :]
            o_ref[...] = (r * (u - xhat * _head_sums(u * xhat) * (1.0 / SB_HEAD_DIM))).astype(BF16)
            dg_ref[pl.ds(kind, 1), :] += jnp.sum(dy * xhat, axis=0, keepdims=True)

    def grad_blk(kind):
        def index(s, i):
            mine = (s >= kind * per) & (s < (kind + 1) * per)
            return jnp.where(mine, s - kind * per, 0), jnp.where(mine, i, 0), 0
        return pl.BlockSpec((None, tm, LANES), index)

    blk = pl.BlockSpec((None, tm, LANES), lambda s, i: (s, i, 0))
    vec = pl.BlockSpec((SUBLANES, LANES), lambda s, i: (0, 0))
    return pl.pallas_call(
        body, name=name, grid=(ns, t // tm),
        in_specs=[grad_blk(0), grad_blk(1), grad_blk(2), blk, vec], out_specs=[blk, vec],
        out_shape=[jax.ShapeDtypeStruct((ns, t, LANES), BF16), jax.ShapeDtypeStruct((SUBLANES, LANES), F32)],
        compiler_params=_params("arbitrary", "arbitrary"),
    )(dq, dk, dv, qkv, gains)


def _split_dot(v, ones_mat, pieces, left=False):
    total, rest = None, v
    for p in range(pieces):
        part = rest.astype(BF16)
        if p + 1 < pieces:
            rest = rest - part.astype(F32)
        d = (jnp.dot(ones_mat, part, preferred_element_type=F32) if left
             else jnp.dot(part, ones_mat, preferred_element_type=F32))
        total = d if total is None else total + d
    return total


LOGIT_SUM_PIECES = 2
GRAD_SUM_PIECES = 2
LOG_WEIGHT_UNDERFLOW = -105.0


def _sb_attn_fwd(qkv_n, n_heads, *, name):
    ns, t, _ = qkv_n.shape
    per = ns // 3
    bq, blk, hd = SB_QUERY_BLOCK, SB_BLOCK, SB_HEAD_DIM
    nq, n_diag = t // bq, bq // blk

    def body(q_ref, k_ref, v_ref, o_ref, walk_ref):
        i = pl.program_id(1)
        row = lax.broadcasted_iota(jnp.int32, (blk, blk), 0)
        col = lax.broadcasted_iota(jnp.int32, (blk, blk), 1)
        later_keys = (row > col).astype(BF16)
        qry = lax.broadcasted_iota(jnp.int32, (bq, blk), 0)
        key = lax.broadcasted_iota(jnp.int32, (bq, blk), 1)

        def tile(kb, carry, key_offset):
            out = []
            start = pl.multiple_of(kb * blk, blk)
            for hf in range(2):
                lanes = slice(hf * hd, (hf + 1) * hd)
                run, acc = carry[hf]
                z = lax.dot_general(q_ref[:, lanes], k_ref[pl.ds(start, blk), lanes], NT_DIMS,
                                    preferred_element_type=F32)
                sp = _softplus(z)
                lm = -sp if key_offset is None else jnp.where(key + key_offset < qry, -sp, 0.0)
                after = _split_dot(lm, later_keys, LOGIT_SUM_PIECES) + run
                a = jnp.exp(z - sp + after)
                if key_offset is not None:
                    a = jnp.where(key + key_offset < qry, a, 0.0)
                acc = acc + jnp.dot(a.astype(BF16), v_ref[pl.ds(start, blk), lanes], preferred_element_type=F32)
                out.append((run + jnp.sum(lm, axis=1, keepdims=True), acc))
            return tuple(out)

        def live(carry):
            return jnp.max(jnp.maximum(carry[0][0], carry[1][0])) > LOG_WEIGHT_UNDERFLOW

        def step(state):
            s, _, carry = state
            carry = tile(n_diag * i - 1 - s, carry, None)
            return s + 1, live(carry), carry

        carry = tuple((jnp.zeros((bq, 1), F32), jnp.zeros((bq, hd), F32)) for _ in range(2))
        for j in reversed(range(n_diag)):
            carry = tile(n_diag * i + j, carry, j * blk)
        walked, _, carry = lax.while_loop(lambda st: (st[0] < n_diag * i) & st[1], step,
                                          (jnp.int32(0), live(carry), carry))
        o_ref[...] = jnp.concatenate([carry[0][1], carry[1][1]], axis=1)
        lane = lax.broadcasted_iota(jnp.int32, (1, LANES), 1)
        walk_ref[...] = jnp.where(lane < WALK_LANES, carry[0][0],
                                  jnp.where(lane < 2 * WALK_LANES, carry[1][0], walked.astype(F32)))

    q_blk = pl.BlockSpec((None, bq, LANES), lambda p, i: (p, i, 0))
    return pl.pallas_call(
        body, name=name, grid=(per, nq),
        in_specs=[q_blk, pl.BlockSpec((None, t, LANES), lambda p, i: (per + p, 0, 0)),
                  pl.BlockSpec((None, t, LANES), lambda p, i: (2 * per + p, 0, 0))],
        out_specs=[pl.BlockSpec((bq, LANES), lambda p, i: (i, p)), q_blk],
        out_shape=[jax.ShapeDtypeStruct((t, n_heads * hd), F32), jax.ShapeDtypeStruct((per, t, LANES), F32)],
        compiler_params=_params("parallel", "arbitrary"),
    )(qkv_n, qkv_n, qkv_n)


WALK_LANES = 43


def _sb_attn_bwd(do, walk, qkv_n, *, name):
    ns, t, _ = qkv_n.shape
    per = ns // 3
    bq, blk, hd = SB_QUERY_BLOCK, SB_BLOCK, SB_HEAD_DIM
    nq, n_diag = t // bq, bq // blk
    nt_dims = (((1,), (1,)), ((), ()))
    tn_dims = (((0,), (0,)), ((), ()))

    def body(q_ref, k_ref, v_ref, do_ref, walk_ref, dq_ref, dk_ref, dv_ref):
        i = pl.program_id(1)

        @pl.when(i == 0)
        def _():
            dk_ref[...] = jnp.zeros_like(dk_ref)
            dv_ref[...] = jnp.zeros_like(dv_ref)

        walk_t = walk_ref[...].T
        tots = [walk_t[hf * WALK_LANES:hf * WALK_LANES + 1, :] for hf in range(2)]
        reached = jnp.clip(jnp.max(walk_t[2 * WALK_LANES:2 * WALK_LANES + 1, :]).astype(jnp.int32), 0, n_diag * i)

        row = lax.broadcasted_iota(jnp.int32, (blk, blk), 0)
        col = lax.broadcasted_iota(jnp.int32, (blk, blk), 1)
        later_keys = (col > row).astype(BF16)
        earlier_keys = (col < row).astype(BF16)
        key = lax.broadcasted_iota(jnp.int32, (blk, bq), 0)
        qry = lax.broadcasted_iota(jnp.int32, (blk, bq), 1)
        halves = [slice(hf * hd, (hf + 1) * hd) for hf in range(2)]
        q_hs = [q_ref[:, lanes] for lanes in halves]
        do_bs = [do_ref[:, lanes].astype(BF16) for lanes in halves]

        def scores(kb, hf, key_offset):
            k_blk = k_ref[pl.ds(pl.multiple_of(kb * blk, blk), blk), halves[hf]]
            z = lax.dot_general(k_blk, q_hs[hf], nt_dims, preferred_element_type=F32)
            sp = _softplus(z)
            return k_blk, z, sp, (-sp if key_offset is None else jnp.where(key + key_offset < qry, -sp, 0.0))

        def tile(kb, carry, key_offset):
            out = []
            start = pl.multiple_of(kb * blk, blk)
            for hf, lanes in enumerate(halves):
                seen, gsum, dq = carry[hf]
                q_h, do_b = q_hs[hf], do_bs[hf]
                k_blk, z, sp, lm = scores(kb, hf, key_offset)
                blk_tot = jnp.sum(lm, axis=0, keepdims=True)
                after = _split_dot(lm, later_keys, LOGIT_SUM_PIECES, left=True) + (tots[hf] - seen - blk_tot)
                a = jnp.exp(z - sp + after)
                if key_offset is not None:
                    a = jnp.where(key + key_offset < qry, a, 0.0)
                da = lax.dot_general(v_ref[pl.ds(start, blk), lanes], do_b, nt_dims, preferred_element_type=F32)
                g = da * a
                before = _split_dot(g, earlier_keys, GRAD_SUM_PIECES, left=True) + gsum
                omb = jnp.exp(-sp)
                dz = g * omb - (1.0 - omb) * before
                if key_offset is not None:
                    dz = jnp.where(key + key_offset < qry, dz, 0.0)
                dz_b = dz.astype(BF16)
                dk_ref[pl.ds(start, blk), lanes] += jnp.dot(dz_b, q_h, preferred_element_type=F32)
                dv_ref[pl.ds(start, blk), lanes] += jnp.dot(a.astype(BF16), do_b, preferred_element_type=F32)
                dq = dq + lax.dot_general(dz_b, k_blk, tn_dims, preferred_element_type=F32)
                out.append((seen + blk_tot, gsum + jnp.sum(g, axis=0, keepdims=True), dq))
            return tuple(out)

        init = tuple((jnp.zeros((1, bq), F32), jnp.zeros((1, bq), F32), jnp.zeros((bq, hd), F32))
                     for _ in range(2))
        carry = lax.fori_loop(n_diag * i - reached, n_diag * i, lambda kb, c: tile(kb, c, None), init)
        for j in range(n_diag):
            carry = tile(n_diag * i + j, carry, j * blk)
        dq_ref[...] = jnp.concatenate([carry[0][2], carry[1][2]], axis=1)

    full = lambda off: pl.BlockSpec((None, t, LANES), lambda p, i: (off + p, 0, 0))
    q_blk = pl.BlockSpec((None, bq, LANES), lambda p, i: (p, i, 0))
    slab = jax.ShapeDtypeStruct((per, t, LANES), F32)
    return pl.pallas_call(
        body, name=name, grid=(per, nq),
        in_specs=[q_blk, full(per), full(2 * per), pl.BlockSpec((bq, LANES), lambda p, i: (i, p)), q_blk],
        out_specs=[q_blk, full(0), full(0)],
        out_shape=[slab, slab, slab],
        compiler_params=_params("parallel", "arbitrary"),
    )(qkv_n, qkv_n, qkv_n, do, walk)


def _sb_mixer_fwd(x, gain, w_qkv_t, qk_gains, w_out, tag):
    n_heads = w_out.shape[0] // SB_HEAD_DIM
    h = _rms_fwd(x, gain, name=f"sb_norm_{tag}")
    qkv = _matmul(h, w_qkv_t, "nt", out_slabs=True, tn_cap=512, name=f"sb_qkv_{tag}")
    qkv_n = _sb_qk_norm_fwd(qkv, qk_gains, name=f"sb_qknorm_{tag}")
    o, walk = _sb_attn_fwd(qkv_n, n_heads, name=f"sb_attn_{tag}")
    x_new = _matmul(o, w_out, "nn", resid=x, name=f"sb_out_{tag}")
    return x_new, (x, h, qkv, qkv_n, o, walk)


def _sb_mixer_bwd(dx, saved, gain, w_qkv_t, qk_gains, w_out, grads, row_qkv, row_out, tag):
    x, h, qkv, qkv_n, o, walk = saved
    do = _matmul(dx, w_out, "nt", name=f"sb_do_{tag}")
    grads = _matmul_tn_into(grads, o, dx, row_out, name=f"sb_dwout_{tag}")
    dq, dk, dv = _sb_attn_bwd(do, walk, qkv_n, name=f"sb_dattn_{tag}")
    dqkv, dqk_gains = _sb_qk_norm_bwd(dq, dk, dv, qkv, qk_gains, name=f"sb_dqknorm_{tag}")
    grads = _matmul_tn_into(grads, dqkv, h, row_qkv, a_slabs=True, name=f"sb_dwqkv_{tag}")
    dh = _matmul(dqkv, w_qkv_t, "nn", a_slabs=True, name=f"sb_dh_{tag}")
    dx_in, dgain = _rms_bwd(x, gain, dh, dx, name=f"sb_dnorm_{tag}")
    return dx_in, grads, dqk_gains, dgain


MESH = pl.DeviceIdType.MESH


def _position():
    return lax.axis_index("x"), lax.axis_index("y"), lax.axis_index("c")


def _all_gather(shard, *, name):
    rows, n = shard.shape
    space = pltpu.VMEM

    def body(x_ref, out_ref, send_sems, recv_sems, local_sem):
        x, y, c = _position()
        me, sibling = (x, y, c), (x, y, 1 - c)
        chips = [(1 - x, y), (x, 1 - y), (1 - x, 1 - y)]

        def block(px, py, pc):
            return out_ref.at[4 * px + 2 * py + pc]

        def copy(k, blk, to, src=None):
            return pltpu.make_async_remote_copy(
                src_ref=block(*blk) if src is None else src, dst_ref=block(*blk),
                send_sem=send_sems.at[k], recv_sem=recv_sems.at[k], device_id=to, device_id_type=MESH)

        mine = pltpu.make_async_copy(x_ref, block(*me), local_sem)
        mine.start()
        first = [copy(0, me, sibling, src=x_ref)]
        first += [copy(1 + j, me, (*chip, c), src=x_ref) for j, chip in enumerate(chips)]
        for cp in first:
            cp.start()
        passed = [copy(4 + j, (*chip, c), sibling) for j, chip in enumerate(chips)]
        for j, chip in enumerate(chips):
            copy(1 + j, (*chip, c), me).wait_recv()
            passed[j].start()
        copy(0, sibling, me).wait_recv()
        for j, chip in enumerate(chips):
            copy(4 + j, (*chip, 1 - c), me).wait_recv()
        for cp in first + passed:
            cp.wait_send()
        mine.wait()

    return pl.pallas_call(
        body, name=name,
        out_shape=jax.ShapeDtypeStruct((N_DEV, rows, n), shard.dtype),
        in_specs=[pl.BlockSpec(memory_space=space)], out_specs=pl.BlockSpec(memory_space=space),
        scratch_shapes=[pltpu.SemaphoreType.DMA((7,)), pltpu.SemaphoreType.DMA((7,)), pltpu.SemaphoreType.DMA],
        compiler_params=pltpu.CompilerParams(vmem_limit_bytes=V7X_VMEM_LIMIT_BYTES),
    )(shard)


def _all_gather_forwarding(shard, *, name):
    rows, n = shard.shape
    half = rows // 2
    assert rows % (4 * SUBLANES) == 0

    def body(x_ref, out_ref, send_sems, recv_sems, local_sem):
        x, y, c = _position()
        me, sibling = (x, y, c), (x, y, 1 - c)
        x_nbr, y_nbr, diag = (1 - x, y), (x, 1 - y), (1 - x, 1 - y)
        lower, upper = pl.ds(0, half), pl.ds(half, half)

        def block(px, py, pc, part=None):
            ref = out_ref.at[4 * px + 2 * py + pc]
            return ref if part is None else ref.at[part]

        def copy(k, blk, to, src=None, part=None):
            return pltpu.make_async_remote_copy(
                src_ref=block(*blk, part) if src is None else src, dst_ref=block(*blk, part),
                send_sem=send_sems.at[k], recv_sem=recv_sems.at[k], device_id=to, device_id_type=MESH)

        mine = pltpu.make_async_copy(x_ref, block(*me), local_sem)
        mine.start()
        sent = [copy(0, me, sibling, src=x_ref), copy(1, me, (*x_nbr, c), src=x_ref),
                copy(2, me, (*y_nbr, c), src=x_ref)]
        for cp in sent:
            cp.start()
        copy(1, (*x_nbr, c), me).wait_recv()
        onward = [copy(3, (*x_nbr, c), (*y_nbr, c), part=lower), copy(5, (*x_nbr, c), sibling)]
        for cp in onward:
            cp.start()
        copy(2, (*y_nbr, c), me).wait_recv()
        onward += [copy(4, (*y_nbr, c), (*x_nbr, c), part=upper), copy(6, (*y_nbr, c), sibling)]
        for cp in onward[2:]:
            cp.start()
        copy(3, (*diag, c), me, part=lower).wait_recv()
        copy(4, (*diag, c), me, part=upper).wait_recv()
        onward.append(copy(7, (*diag, c), sibling))
        onward[-1].start()
        sent += onward
        copy(0, sibling, me).wait_recv()
        for k, chip in ((5, x_nbr), (6, y_nbr), (7, diag)):
            copy(k, (*chip, 1 - c), me).wait_recv()
        for cp in sent:
            cp.wait_send()
        mine.wait()

    hbm = pl.BlockSpec(memory_space=pltpu.HBM)
    return pl.pallas_call(
        body, name=name,
        out_shape=jax.ShapeDtypeStruct((N_DEV, rows, n), shard.dtype), in_specs=[hbm], out_specs=hbm,
        scratch_shapes=[pltpu.SemaphoreType.DMA((8,)), pltpu.SemaphoreType.DMA((8,)), pltpu.SemaphoreType.DMA],
    )(shard)


def _exchange_sibling(parts, *, name):
    _, nchip, rows, n = parts.shape

    def body(p_ref, recv_ref, send_sem, recv_sem):
        x, y, c = _position()
        cp = pltpu.make_async_remote_copy(src_ref=p_ref.at[1 - c], dst_ref=recv_ref, send_sem=send_sem,
                                          recv_sem=recv_sem, device_id=(x, y, 1 - c), device_id_type=MESH)
        cp.start()
        cp.wait()

    return pl.pallas_call(
        body, name=name,
        out_shape=jax.ShapeDtypeStruct((nchip, rows, n), parts.dtype),
        in_specs=[pl.BlockSpec(memory_space=pltpu.HBM)], out_specs=pl.BlockSpec(memory_space=pltpu.HBM),
        scratch_shapes=[pltpu.SemaphoreType.DMA, pltpu.SemaphoreType.DMA],
    )(parts)


def _exchange_chips(chip_sums, *, name):
    _, rows, n = chip_sums.shape

    def body(s_ref, recv_ref, send_sems, recv_sems):
        x, y, c = _position()
        chips = [(1 - x, y), (x, 1 - y), (1 - x, 1 - y)]
        copies = [pltpu.make_async_remote_copy(
            src_ref=s_ref.at[2 * cx + cy], dst_ref=recv_ref.at[j], send_sem=send_sems.at[j],
            recv_sem=recv_sems.at[j], device_id=(cx, cy, c), device_id_type=MESH)
            for j, (cx, cy) in enumerate(chips)]
        for cp in copies:
            cp.start()
        for cp in copies:
            cp.wait()

    return pl.pallas_call(
        body, name=name,
        out_shape=jax.ShapeDtypeStruct((3, rows, n), chip_sums.dtype),
        in_specs=[pl.BlockSpec(memory_space=pltpu.HBM)], out_specs=pl.BlockSpec(memory_space=pltpu.HBM),
        scratch_shapes=[pltpu.SemaphoreType.DMA((3,)), pltpu.SemaphoreType.DMA((3,))],
    )(chip_sums)


def _add_pairs(parts, recv, c_and_chip, *, name):
    _, nchip, rows, n = parts.shape
    tr = _tile(rows, 512, SUBLANES)

    def body(pos_ref, a_ref, b_ref, own_ref, wire_ref):
        s = a_ref[...] + b_ref[...]
        wire_ref[...] = s.astype(WIRE_DTYPE)

        @pl.when(pl.program_id(1) == pos_ref[1])
        def _():
            own_ref[...] = s

    return pl.pallas_call(
        body, name=name,
        grid_spec=pltpu.PrefetchScalarGridSpec(
            num_scalar_prefetch=1, grid=(rows // tr, nchip),
            in_specs=[pl.BlockSpec((None, None, tr, n), lambda i, k, pos: (pos[0], k, i, 0)),
                      pl.BlockSpec((None, tr, n), lambda i, k, pos: (k, i, 0))],
            out_specs=[pl.BlockSpec((tr, n), lambda i, k, pos: (i, 0)),
                       pl.BlockSpec((None, tr, n), lambda i, k, pos: (k, i, 0))]),
        out_shape=[jax.ShapeDtypeStruct((rows, n), parts.dtype),
                   jax.ShapeDtypeStruct((nchip, rows, n), WIRE_DTYPE)],
        compiler_params=_params("parallel", "arbitrary"),
    )(c_and_chip, parts, recv)


def _adamw_math(w, g, m, v):
    m = ADAM_B1 * m + (1.0 - ADAM_B1) * g
    v = ADAM_B2 * v + (1.0 - ADAM_B2) * (g * g)
    m_hat = m / (1.0 - ADAM_B1 ** ADAM_STEP)
    v_hat = v / (1.0 - ADAM_B2 ** ADAM_STEP)
    delta = -ADAM_LR * (m_hat / (jnp.sqrt(v_hat) + ADAM_EPS) + ADAM_WD * w)
    return delta, m, v


def _adamw_sharded(own_sum, recv, w, m, v, *, name):
    rows, n = w.shape
    tr = _tile(rows, 256, SUBLANES)

    def body(s_ref, r_ref, w_ref, m_ref, v_ref, g_out, d_out, m_out, v_out):
        g = ((s_ref[...] + r_ref[0].astype(F32)) + r_ref[1].astype(F32)) + r_ref[2].astype(F32)
        delta, m_new, v_new = _adamw_math(w_ref[...], g, m_ref[...], v_ref[...])
        g_out[...] = g
        d_out[...] = delta
        m_out[...] = m_new
        v_out[...] = v_new

    blk = pl.BlockSpec((tr, n), lambda i: (i, 0))
    out = jax.ShapeDtypeStruct((rows, n), F32)
    return pl.pallas_call(
        body, name=name, grid=(rows // tr,),
        in_specs=[blk, pl.BlockSpec((3, tr, n), lambda i: (0, i, 0)), blk, blk, blk],
        out_specs=[blk, blk, blk, blk], out_shape=[out, out, out, out],
        compiler_params=_params("parallel"),
    )(own_sum, recv, w, m, v)


SMALL_ROWS = 40
ROW_MIX_NORM, ROW_FFN_NORM, ROW_CONV_B, ROW_OUT_NORM, ROW_POOL_SCALE, ROW_CONV_W = 0, 4, 8, 12, 14, 16
ROW_SSD_VEC, ROW_QK_GAIN, ROW_LOSS = 32, 33, 34


def _adamw_small(gathered, w, m, v, *, name):
    _, rows, n = gathered.shape

    def body(a_ref, w_ref, m_ref, v_ref, g_out, d_out, m_out, v_out):
        g = a_ref[0]
        for d in range(1, N_DEV):
            g = g + a_ref[d]
        row = lax.broadcasted_iota(jnp.int32, (rows, 1), 0)
        g = jnp.where(row == ROW_QK_GAIN, g + pltpu.roll(g, SB_HEAD_DIM, 1), g)
        g = jnp.where(row == ROW_LOSS, jnp.sum(g, axis=1, keepdims=True), g)
        g_out[...] = g
        delta, m_new, v_new = _adamw_math(w_ref[...], g, m_ref[...], v_ref[...])
        d_out[...] = delta
        m_out[...] = m_new
        v_out[...] = v_new

    out = jax.ShapeDtypeStruct((rows, n), F32)
    return pl.pallas_call(body, name=name, out_shape=[out, out, out, out])(gathered, w, m, v)


BIG_WEIGHTS = ("ffn_gate", "ffn_up", "ffn_down", "sb_qkv", "ssd_out", "pool_in", "sb_out", "pool_group", "ssd_in")
COLUMN_SHARDED = ("ssd_in", "sb_qkv", "ffn_gate", "ffn_up")
ROW_PAD = 512
WIRE_DTYPE = jnp.bfloat16


def _to_rows(name, shard, d):
    if name in COLUMN_SHARDED:
        shard = jnp.swapaxes(shard, -1, -2)
    return shard.reshape(-1, d)


def _from_rows(name, rows, shard_shape):
    if name in COLUMN_SHARDED:
        lead, k, n = shard_shape
        return jnp.swapaxes(rows.reshape(lead, n, k), -1, -2)
    return rows.reshape(shard_shape)


def _pad_rows(a, total):
    return jnp.pad(a, ((0, total - a.shape[0]),) + ((0, 0),) * (a.ndim - 1))


def _exact_bf16_rows(v, d):
    words = lax.bitcast_convert_type(v.reshape(-1), WIRE_DTYPE).reshape(-1)
    return _pad_rows(words, -(-words.shape[0] // d) * d).reshape(-1, d)


def _exact_f32(rows, count):
    words = rows.reshape(rows.shape[0], -1)[:, :2 * count].reshape(rows.shape[0], count, 2)
    return lax.bitcast_convert_type(words, F32)


def _device_blocks(full, d):
    return full.reshape(N_DEV, -1, d)


def kernel(x, mix_norm, pool_in, pool_group, pool_scale, ssd_in, ssd_conv_w, ssd_conv_b, ssd_dt_bias, ssd_a_log, ssd_d, ssd_out_norm, ssd_out, sb_qkv, sb_q_norm, sb_k_norm, sb_out, ffn_norm, ffn_gate, ffn_up, ffn_down, loss_target, m_mix_norm, m_pool_in, m_pool_group, m_pool_scale, m_ssd_in, m_ssd_conv_w, m_ssd_conv_b, m_ssd_dt_bias, m_ssd_a_log, m_ssd_d, m_ssd_out_norm, m_ssd_out, m_sb_qkv, m_sb_q_norm, m_sb_k_norm, m_sb_out, m_ffn_norm, m_ffn_gate, m_ffn_up, m_ffn_down, v_mix_norm, v_pool_in, v_pool_group, v_pool_scale, v_ssd_in, v_ssd_conv_w, v_ssd_conv_b, v_ssd_dt_bias, v_ssd_a_log, v_ssd_d, v_ssd_out_norm, v_ssd_out, v_sb_qkv, v_sb_q_norm, v_sb_k_norm, v_sb_out, v_ffn_norm, v_ffn_gate, v_ffn_up, v_ffn_down):
    weights = dict(mix_norm=mix_norm, pool_in=pool_in, pool_group=pool_group, pool_scale=pool_scale, ssd_in=ssd_in,
                   ssd_conv_w=ssd_conv_w, ssd_conv_b=ssd_conv_b, ssd_dt_bias=ssd_dt_bias, ssd_a_log=ssd_a_log,
                   ssd_d=ssd_d, ssd_out_norm=ssd_out_norm, ssd_out=ssd_out, sb_qkv=sb_qkv, sb_q_norm=sb_q_norm,
                   sb_k_norm=sb_k_norm, sb_out=sb_out, ffn_norm=ffn_norm, ffn_gate=ffn_gate, ffn_up=ffn_up,
                   ffn_down=ffn_down)
    mom1 = dict(mix_norm=m_mix_norm, pool_in=m_pool_in, pool_group=m_pool_group, pool_scale=m_pool_scale,
                ssd_in=m_ssd_in, ssd_conv_w=m_ssd_conv_w, ssd_conv_b=m_ssd_conv_b, ssd_dt_bias=m_ssd_dt_bias,
                ssd_a_log=m_ssd_a_log, ssd_d=m_ssd_d, ssd_out_norm=m_ssd_out_norm, ssd_out=m_ssd_out,
                sb_qkv=m_sb_qkv, sb_q_norm=m_sb_q_norm, sb_k_norm=m_sb_k_norm, sb_out=m_sb_out,
                ffn_norm=m_ffn_norm, ffn_gate=m_ffn_gate, ffn_up=m_ffn_up, ffn_down=m_ffn_down)
    mom2 = dict(mix_norm=v_mix_norm, pool_in=v_pool_in, pool_group=v_pool_group, pool_scale=v_pool_scale,
                ssd_in=v_ssd_in, ssd_conv_w=v_ssd_conv_w, ssd_conv_b=v_ssd_conv_b, ssd_dt_bias=v_ssd_dt_bias,
                ssd_a_log=v_ssd_a_log, ssd_d=v_ssd_d, ssd_out_norm=v_ssd_out_norm, ssd_out=v_ssd_out,
                sb_qkv=v_sb_qkv, sb_q_norm=v_sb_q_norm, sb_k_norm=v_sb_k_norm, sb_out=v_sb_out,
                ffn_norm=v_ffn_norm, ffn_gate=v_ffn_gate, ffn_up=v_ffn_up, ffn_down=v_ffn_down)
    names = list(weights)
    depth, d = mix_norm.shape
    xs, ys, cs = _position()
    dev = 4 * xs + 2 * ys + cs
    chip = 2 * xs + ys

    seg = {}
    row = 0
    for name in BIG_WEIGHTS:
        n_rows = weights[name].size // d
        seg[name] = (row, n_rows)
        row += -(-n_rows // SUBLANES) * SUBLANES
    big_rows = row
    n_scale, n_convw = pool_scale.size, ssd_conv_w.size
    exact = jnp.concatenate([_exact_bf16_rows(pool_scale, d), _exact_bf16_rows(ssd_conv_w, d)], axis=0)
    scale_rows = _exact_bf16_rows(pool_scale, d).shape[0]
    packed_rows = -(-(big_rows + exact.shape[0]) // ROW_PAD) * ROW_PAD

    def pack(tree, dtype):
        ends = [seg[n][0] for n in BIG_WEIGHTS[1:]] + [big_rows]
        return jnp.concatenate([_pad_rows(_to_rows(n, tree[n], d).astype(dtype), end - seg[n][0])
                                for n, end in zip(BIG_WEIGHTS, ends)], axis=0)

    w_wire = _pad_rows(jnp.concatenate([pack(weights, WIRE_DTYPE), exact], axis=0), packed_rows)
    gathered = _all_gather_forwarding(w_wire, name="gather_weights")

    def seg_of(name):
        a, n = seg[name]
        return gathered[:, a:a + n]

    n_pool, n_ssd, n_sb = pool_in.shape[0], ssd_in.shape[0], sb_qkv.shape[0]
    assert n_ssd == 1 and n_sb == 1
    w_pool_in = seg_of("pool_in").reshape(N_DEV, n_pool, -1, d).transpose(1, 0, 2, 3).reshape(n_pool, d, d)
    grp = pool_group.shape
    w_pool_group = seg_of("pool_group").reshape(N_DEV, grp[0], grp[1], grp[2], grp[3]).transpose(1, 2, 0, 3, 4)
    w_pool_group = w_pool_group.reshape(grp[0], grp[1], grp[3], grp[3])
    w_ssd_in_t = seg_of("ssd_in").reshape(-1, d)
    w_ssd_out = seg_of("ssd_out").reshape(-1, d)
    w_sb_qkv_t = seg_of("sb_qkv").reshape(-1, d)
    w_sb_out = seg_of("sb_out").reshape(-1, d)
    hidden = ffn_down.shape[1] * N_DEV
    w_gate_t = seg_of("ffn_gate").reshape(N_DEV, depth, -1, d).transpose(1, 0, 2, 3).reshape(depth, hidden, d)
    w_up_t = seg_of("ffn_up").reshape(N_DEV, depth, -1, d).transpose(1, 0, 2, 3).reshape(depth, hidden, d)
    w_down = seg_of("ffn_down").reshape(N_DEV, depth, -1, d).transpose(1, 0, 2, 3).reshape(depth, hidden, d)
    exact_all = gathered[:, big_rows:big_rows + exact.shape[0]]
    scale_full = _exact_f32(exact_all[:, :scale_rows], n_scale).reshape(N_DEV, n_pool, -1)
    scale_full = scale_full.transpose(1, 0, 2).reshape(n_pool, d)
    convw_full = _exact_f32(exact_all[:, scale_rows:], n_convw).reshape(N_DEV, SSD_CONV, -1)
    convw_full = convw_full.transpose(1, 0, 2).reshape(SSD_CONV, -1)

    d_inner = w_ssd_out.shape[0]
    n_zx = w_ssd_in_t.shape[0] - ssd_dt_bias.shape[1]
    w_zx_t = w_ssd_in_t[:n_zx]
    n_ssd_heads = ssd_dt_bias.shape[1]
    n_ssd_groups = n_ssd_heads // SSD_HEADS_PER_GROUP
    w_dt_t = _ssd_group_pad(w_ssd_in_t[n_zx:].T, n_ssd_groups).T
    par = _pad_rows(_ssd_group_pad(jnp.concatenate([ssd_dt_bias, ssd_a_log, ssd_d], axis=0), n_ssd_groups), SUBLANES)
    d_rep = jnp.repeat(ssd_d[0], SSD_HEAD_DIM)[None]
    qk_gains = jnp.zeros((SUBLANES, LANES), F32).at[0].set(jnp.tile(sb_q_norm[0], 2)).at[1].set(jnp.tile(sb_k_norm[0], 2))

    act = x[0]
    saved = []
    for i in range(depth):
        kind, j = i % 3, i // 3
        gain = mix_norm[i:i + 1]
        if kind == 0:
            act, s = _pool_mixer_fwd(act, gain, w_pool_in[j], w_pool_group[j], scale_full[j:j + 1], f"l{i}")
        elif kind == 1:
            act, s = _ssd_mixer_fwd(act, gain, w_zx_t, w_dt_t, convw_full, ssd_conv_b, par, d_rep, ssd_out_norm,
                                    w_ssd_out, f"l{i}")
        else:
            act, s = _sb_mixer_fwd(act, gain, w_sb_qkv_t, qk_gains, w_sb_out, f"l{i}")
        act, f = _ffn_fwd(act, ffn_norm[i:i + 1], w_gate_t[i], w_up_t[i], w_down[i], f"l{i}")
        saved.append((s, f))
    dact, loss_cols = _loss_head(act, loss_target[0], name="loss_head")

    def layer_row(name, layer):
        return seg[name][0] + layer * (seg[name][1] // weights[name].shape[0])

    grads = jnp.zeros((N_DEV, packed_rows, d), F32)
    g_mix_norm, g_ffn_norm = [None] * depth, [None] * depth
    g_pool_group, g_pool_scale = [None] * n_pool, [None] * n_pool
    for i in reversed(range(depth)):
        kind, j = i % 3, i // 3
        gain = mix_norm[i:i + 1]
        s, f = saved[i]
        dact, grads, g_ffn_norm[i] = _ffn_bwd(
            dact, f, ffn_norm[i:i + 1], w_gate_t[i], w_up_t[i], w_down[i], grads,
            [layer_row(n, i) for n in ("ffn_gate", "ffn_up", "ffn_down")], f"l{i}")
        if kind == 0:
            dact, grads, g_pool_group[j], g_pool_scale[j], g_mix_norm[i] = _pool_mixer_bwd(
                dact, s, gain, w_pool_in[j], w_pool_group[j], scale_full[j:j + 1], grads, layer_row("pool_in", j),
                f"l{i}")
        elif kind == 1:
            (dact, grads, g_zx_t, g_dt_t, g_conv_w, g_conv_b, g_par, g_out_norm,
             g_mix_norm[i]) = _ssd_mixer_bwd(dact, s, gain, w_zx_t, w_dt_t, convw_full, ssd_conv_b, par, d_rep,
                                             ssd_out_norm, w_ssd_out, grads, layer_row("ssd_out", j), f"l{i}")
        else:
            dact, grads, g_qk_gains, g_mix_norm[i] = _sb_mixer_bwd(
                dact, s, gain, w_sb_qkv_t, qk_gains, w_sb_out, grads, layer_row("sb_qkv", j),
                layer_row("sb_out", j), f"l{i}")
    grad_x = dact[None]

    g_ssd_in = jnp.concatenate([g_zx_t, _ssd_group_unpad(g_dt_t.T, n_ssd_groups).T], axis=0)
    g_group = jnp.concatenate([_device_blocks(gg[k], d) for gg in g_pool_group for k in range(gg.shape[0])], axis=1)
    for name, blocks in (("ssd_in", _device_blocks(g_ssd_in, d)), ("pool_group", g_group)):
        blocks = blocks.reshape(N_DEV // 2, 2, -1, d).swapaxes(0, 1).reshape(N_DEV, -1, d)
        grads = lax.dynamic_update_slice(grads, blocks, (0, seg[name][0], 0))
    parts = grads.reshape(2, N_DEV // 2, packed_rows, d)
    from_sibling = _exchange_sibling(parts, name="reduce_sibling")
    own_sum, chip_sums_wire = _add_pairs(parts, from_sibling, jnp.stack([cs, chip]).astype(jnp.int32),
                                         name="reduce_sibling_add")
    from_chips = _exchange_chips(chip_sums_wire, name="reduce_chips")

    def pack_f32(tree):
        return _pad_rows(pack(tree, F32), packed_rows)

    big_out = _adamw_sharded(own_sum, from_chips, pack_f32(weights), pack_f32(mom1), pack_f32(mom2),
                             name="adamw_sharded")

    def small_pack(mix, ffn, conv_b, out_norm, scale, conv_w, vec, qk, loss=None):
        buf = jnp.zeros((SMALL_ROWS, d), F32)
        buf = buf.at[ROW_MIX_NORM:ROW_MIX_NORM + depth].set(mix).at[ROW_FFN_NORM:ROW_FFN_NORM + depth].set(ffn)
        buf = buf.at[ROW_CONV_B:ROW_CONV_B + conv_b.size // d].set(conv_b.reshape(-1, d))
        buf = buf.at[ROW_OUT_NORM:ROW_OUT_NORM + out_norm.size // d].set(out_norm.reshape(-1, d))
        buf = buf.at[ROW_POOL_SCALE:ROW_POOL_SCALE + n_pool].set(scale)
        buf = buf.at[ROW_CONV_W:ROW_CONV_W + conv_w.size // d].set(conv_w.reshape(-1, d))
        buf = buf.at[ROW_SSD_VEC].set(vec.reshape(-1)).at[ROW_QK_GAIN].set(qk.reshape(-1))
        if loss is not None:
            buf = buf.at[ROW_LOSS].set(loss.reshape(-1))
        return buf

    def small_params(tree):
        scale = lax.dynamic_update_slice(jnp.zeros((n_pool, d), F32), tree["pool_scale"],
                                         (0, dev * tree["pool_scale"].shape[1]))
        conv_w = lax.dynamic_update_slice(jnp.zeros(convw_full.shape, F32), tree["ssd_conv_w"][0],
                                          (0, dev * tree["ssd_conv_w"].shape[2]))
        vec = jnp.zeros((SUBLANES, LANES), F32)
        vec = vec.at[0, :n_ssd_heads].set(tree["ssd_dt_bias"][0]).at[1, :n_ssd_heads].set(tree["ssd_a_log"][0])
        vec = vec.at[2, :n_ssd_heads].set(tree["ssd_d"][0])
        qk = jnp.zeros((SUBLANES, LANES), F32)
        qk = qk.at[0, SB_HEAD_DIM:].set(tree["sb_q_norm"][0]).at[1, SB_HEAD_DIM:].set(tree["sb_k_norm"][0])
        return small_pack(tree["mix_norm"], tree["ffn_norm"], tree["ssd_conv_b"], tree["ssd_out_norm"], scale,
                          conv_w, vec, qk)

    small_partial = small_pack(jnp.concatenate(g_mix_norm, axis=0), jnp.concatenate(g_ffn_norm, axis=0), g_conv_b,
                               g_out_norm, jnp.concatenate(g_pool_scale, axis=0), g_conv_w,
                               jnp.zeros((SUBLANES, LANES), F32).at[:3, :n_ssd_heads].set(
                                   _ssd_group_unpad(g_par[:3], n_ssd_groups)), g_qk_gains,
                               loss_cols)
    small_all = _all_gather(small_partial, name="gather_small")
    small_out = _adamw_small(small_all, small_params(weights), small_params(mom1), small_params(mom2),
                             name="adamw_small")
    loss = small_out[0][ROW_LOSS, 0]

    def unpack(big, small):
        out = {}
        for name in BIG_WEIGHTS:
            a, n = seg[name]
            out[name] = _from_rows(name, big[a:a + n], weights[name].shape)
        out["mix_norm"] = small[ROW_MIX_NORM:ROW_MIX_NORM + depth]
        out["ffn_norm"] = small[ROW_FFN_NORM:ROW_FFN_NORM + depth]
        out["ssd_conv_b"] = small[ROW_CONV_B:ROW_CONV_B + ssd_conv_b.size // d].reshape(ssd_conv_b.shape)
        out["ssd_out_norm"] = small[ROW_OUT_NORM:ROW_OUT_NORM + ssd_out_norm.size // d].reshape(ssd_out_norm.shape)
        out["pool_scale"] = lax.dynamic_slice(small[ROW_POOL_SCALE:ROW_POOL_SCALE + n_pool],
                                              (0, dev * pool_scale.shape[1]), pool_scale.shape)
        conv_w = small[ROW_CONV_W:ROW_CONV_W + convw_full.size // d].reshape(convw_full.shape)
        out["ssd_conv_w"] = lax.dynamic_slice(conv_w, (0, dev * ssd_conv_w.shape[2]), ssd_conv_w.shape[1:])[None]
        vec = small[ROW_SSD_VEC].reshape(SUBLANES, LANES)
        out["ssd_dt_bias"], out["ssd_a_log"], out["ssd_d"] = (vec[r:r + 1, :n_ssd_heads] for r in range(3))
        qk = small[ROW_QK_GAIN].reshape(SUBLANES, LANES)
        out["sb_q_norm"], out["sb_k_norm"] = qk[0:1, SB_HEAD_DIM:], qk[1:2, SB_HEAD_DIM:]
        return [out[n] for n in names]

    results = [unpack(b, s) for b, s in zip(big_out, small_out)]
    return (loss, grad_x, *results[0], *results[1], *results[2], *results[3])
```

```python
import math

import jax
import jax.numpy as jnp
from jax import lax
from jax.experimental import pallas as pl
from jax.experimental.pallas import tpu as pltpu

F32 = jnp.float32
BF16 = jnp.bfloat16

N_DEV = 8
NORM_EPS = 1e-6
V7X_VMEM_LIMIT_BYTES = 48 * 1024 * 1024
LANES = 128
SUBLANES = 8

POOL_WINDOWS = (2, 4, 8, 16)
SSD_CHUNK = 256
SSD_HEAD_DIM = 64
SSD_STATE = 128
SSD_HEADS_PER_GROUP = 4
SSD_CONV = 4
SB_HEAD_DIM = 64
SB_BLOCK = 256
SB_QUERY_BLOCK = 256

ADAM_LR = 0.001
ADAM_B1 = 0.9
ADAM_B2 = 0.999
ADAM_EPS = 1e-08
ADAM_WD = 0.01
ADAM_STEP = 10


def _params(*sem):
    return pltpu.CompilerParams(dimension_semantics=sem, vmem_limit_bytes=V7X_VMEM_LIMIT_BYTES)


def _tile(n, cap, mult):
    best = None
    for t in range(mult, min(n, cap) + 1, mult):
        if n % t == 0:
            best = t
    return best or n


def _load_slabs(ref, slabs):
    if not slabs:
        return ref[...]
    return jnp.concatenate([ref[p] for p in range(ref.shape[0])], axis=1)


def _matmul(a, b, mode, *, name, out_dtype=F32, resid=None, a_slabs=False, out_slabs=False,
            tm_cap=1024, tn_cap=1024, tk_cap=2048):
    pairs = list(zip(a, b)) if isinstance(a, (list, tuple)) else [(a, b)]
    a, b = pairs[0]
    if a_slabs:
        m, k = a.shape[1], a.shape[0] * LANES
    else:
        m, k = a.shape
    n = b.shape[1] if mode == "nn" else b.shape[0]
    assert (b.shape[0] if mode == "nn" else b.shape[1]) == k
    assert all(pa.shape == a.shape and pb.shape == b.shape for pa, pb in pairs)
    tm, tn, tk = _tile(m, tm_cap, SUBLANES), _tile(n, tn_cap, LANES), _tile(k, tk_cap, LANES)
    nk = k // tk
    dn = (((1,), (0,)), ((), ())) if mode == "nn" else (((1,), (1,)), ((), ()))
    has_resid = resid is not None
    n_pairs = len(pairs)

    def body(*refs):
        ab_refs, rest = refs[:2 * n_pairs], refs[2 * n_pairs:]
        r_ref = rest[0] if has_resid else None
        o_ref = rest[1] if has_resid else rest[0]
        kk = pl.program_id(2)

        def partial():
            total = None
            for p in range(n_pairs):
                d = lax.dot_general(_load_slabs(ab_refs[2 * p], a_slabs).astype(BF16),
                                    ab_refs[2 * p + 1][...].astype(BF16), dn, preferred_element_type=F32)
                total = d if total is None else total + d
            return total

        def finish(r):
            if has_resid:
                r = r + r_ref[...]
            if out_slabs:
                for p in range(tn // LANES):
                    o_ref[p] = r[:, p * LANES:(p + 1) * LANES].astype(out_dtype)
            else:
                o_ref[...] = r.astype(out_dtype)

        if nk == 1:
            finish(partial())
        else:
            acc = rest[-1]

            @pl.when(kk == 0)
            def _():
                acc[...] = jnp.zeros_like(acc)

            acc[...] += partial()

            @pl.when(kk == nk - 1)
            def _():
                finish(acc[...])

    b_spec = (pl.BlockSpec((tk, tn), lambda i, j, kk: (kk, j)) if mode == "nn"
              else pl.BlockSpec((tn, tk), lambda i, j, kk: (j, kk)))
    a_spec = (pl.BlockSpec((tk // LANES, tm, LANES), lambda i, j, kk: (kk, i, 0)) if a_slabs
              else pl.BlockSpec((tm, tk), lambda i, j, kk: (i, kk)))
    in_specs = [a_spec, b_spec] * n_pairs
    args = [t for pair in pairs for t in pair]
    if has_resid:
        in_specs.append(pl.BlockSpec((tm, tn), lambda i, j, kk: (i, j)))
        args.append(resid)
    if out_slabs:
        out_spec = pl.BlockSpec((tn // LANES, tm, LANES), lambda i, j, kk: (j, i, 0))
        out_shape = jax.ShapeDtypeStruct((n // LANES, m, LANES), out_dtype)
    else:
        out_spec = pl.BlockSpec((tm, tn), lambda i, j, kk: (i, j))
        out_shape = jax.ShapeDtypeStruct((m, n), out_dtype)
    return pl.pallas_call(
        body, name=name, grid=(m // tm, n // tn, nk),
        in_specs=in_specs, out_specs=out_spec, out_shape=out_shape,
        scratch_shapes=[pltpu.VMEM((tm, tn), F32)] if nk > 1 else [],
        compiler_params=_params("parallel", "parallel", "arbitrary"),
    )(*args)


def _matmul_tn(a, b, *, name, a_slabs=False, ta_cap=1024, tb_cap=1024, tr_cap=512):
    if a_slabs:
        r, ka = a.shape[1], a.shape[0] * LANES
    else:
        r, ka = a.shape
    nb = b.shape[1]
    assert b.shape[0] == r
    ta, tb, tr = _tile(ka, ta_cap, LANES), _tile(nb, tb_cap, LANES), _tile(r, tr_cap, SUBLANES)

    def body(a_ref, b_ref, o_ref):
        @pl.when(pl.program_id(2) == 0)
        def _():
            o_ref[...] = jnp.zeros_like(o_ref)

        o_ref[...] += lax.dot_general(_load_slabs(a_ref, a_slabs).astype(BF16), b_ref[...].astype(BF16),
                                      (((0,), (0,)), ((), ())), preferred_element_type=F32)

    a_spec = (pl.BlockSpec((ta // LANES, tr, LANES), lambda i, j, kk: (i, kk, 0)) if a_slabs
              else pl.BlockSpec((tr, ta), lambda i, j, kk: (kk, i)))
    return pl.pallas_call(
        body, name=name, grid=(ka // ta, nb // tb, r // tr),
        in_specs=[a_spec, pl.BlockSpec((tr, tb), lambda i, j, kk: (kk, j))],
        out_specs=pl.BlockSpec((ta, tb), lambda i, j, kk: (i, j)),
        out_shape=jax.ShapeDtypeStruct((ka, nb), F32),
        compiler_params=_params("parallel", "parallel", "arbitrary"),
    )(a, b)


def _core_major(k):
    return (k % 2) * (N_DEV // 2) + k // 2


def _matmul_tn_into(buf, a, b, row_off, *, name, a_slabs=False, tr_cap=1024):
    if a_slabs:
        r, ka = a.shape[1], a.shape[0] * LANES
    else:
        r, ka = a.shape
    n_dev, _, n = buf.shape
    per = ka // n_dev
    assert b.shape == (r, n) and ka % n_dev == 0 and per % SUBLANES == 0 and row_off % per == 0
    tr = _tile(r, tr_cap, SUBLANES)

    def body(buf_ref, a_ref, b_ref, o_ref):
        prod = lax.dot_general(_load_slabs(a_ref, a_slabs).astype(BF16), b_ref[...].astype(BF16),
                               (((0,), (0,)), ((), ())), preferred_element_type=F32)
        @pl.when(pl.program_id(0) == 0)
        def _():
            for k in range(n_dev):
                o_ref[_core_major(k)] = prod[k * per:(k + 1) * per]

        @pl.when(pl.program_id(0) > 0)
        def _():
            for k in range(n_dev):
                o_ref[_core_major(k)] += prod[k * per:(k + 1) * per]

    a_spec = (pl.BlockSpec((ka // LANES, tr, LANES), lambda i: (0, i, 0)) if a_slabs
              else pl.BlockSpec((tr, ka), lambda i: (i, 0)))
    return pl.pallas_call(
        body, name=name, grid=(r // tr,),
        in_specs=[pl.BlockSpec(memory_space=pl.ANY), a_spec, pl.BlockSpec((tr, n), lambda i: (i, 0))],
        out_specs=pl.BlockSpec((n_dev, per, n), lambda i: (0, row_off // per, 0)),
        out_shape=jax.ShapeDtypeStruct(buf.shape, F32),
        input_output_aliases={0: 0},
        compiler_params=_params("arbitrary"),
    )(buf, a, b)


def _rms_fwd(x, gain, *, name):
    t, d = x.shape
    tm = _tile(t, 512, SUBLANES)

    def body(x_ref, g_ref, o_ref):
        xv = x_ref[...]
        r = lax.rsqrt(jnp.mean(xv * xv, axis=-1, keepdims=True) + NORM_EPS)
        o_ref[...] = (xv * r * g_ref[...]).astype(BF16)

    return pl.pallas_call(
        body, name=name, grid=(t // tm,),
        in_specs=[pl.BlockSpec((tm, d), lambda i: (i, 0)), pl.BlockSpec((1, d), lambda i: (0, 0))],
        out_specs=pl.BlockSpec((tm, d), lambda i: (i, 0)),
        out_shape=jax.ShapeDtypeStruct((t, d), BF16),
        compiler_params=_params("parallel"),
    )(x, gain)


def _rms_bwd(x, gain, dh, dres, *, name):
    t, d = x.shape
    tm = _tile(t, 512, SUBLANES)

    def body(x_ref, g_ref, dh_ref, dres_ref, dx_ref, dg_ref):
        @pl.when(pl.program_id(0) == 0)
        def _():
            dg_ref[...] = jnp.zeros_like(dg_ref)

        xv = x_ref[...]
        r = lax.rsqrt(jnp.mean(xv * xv, axis=-1, keepdims=True) + NORM_EPS)
        xhat = xv * r
        dhv = dh_ref[...]
        u = dhv * g_ref[...]
        dx_ref[...] = dres_ref[...] + r * (u - xhat * jnp.mean(u * xhat, axis=-1, keepdims=True))
        dg_ref[...] += jnp.sum(dhv * xhat, axis=0, keepdims=True)

    return pl.pallas_call(
        body, name=name, grid=(t // tm,),
        in_specs=[pl.BlockSpec((tm, d), lambda i: (i, 0)), pl.BlockSpec((1, d), lambda i: (0, 0)),
                  pl.BlockSpec((tm, d), lambda i: (i, 0)), pl.BlockSpec((tm, d), lambda i: (i, 0))],
        out_specs=[pl.BlockSpec((tm, d), lambda i: (i, 0)), pl.BlockSpec((1, d), lambda i: (0, 0))],
        out_shape=[jax.ShapeDtypeStruct((t, d), F32), jax.ShapeDtypeStruct((1, d), F32)],
        compiler_params=_params("arbitrary"),
    )(x, gain, dh, dres)


def _loss_head(y, target, *, name):
    t, d = y.shape
    tm = _tile(t, 512, SUBLANES)

    def body(y_ref, t_ref, dy_ref, l_ref):
        @pl.when(pl.program_id(0) == 0)
        def _():
            l_ref[...] = jnp.zeros_like(l_ref)

        e = y_ref[...] - t_ref[...]
        dy_ref[...] = e * (1.0 / d)
        l_ref[...] += jnp.sum(e * e, axis=0, keepdims=True) * (0.5 / d)

    return pl.pallas_call(
        body, name=name, grid=(t // tm,),
        in_specs=[pl.BlockSpec((tm, d), lambda i: (i, 0)), pl.BlockSpec((tm, d), lambda i: (i, 0))],
        out_specs=[pl.BlockSpec((tm, d), lambda i: (i, 0)), pl.BlockSpec((1, d), lambda i: (0, 0))],
        out_shape=[jax.ShapeDtypeStruct((t, d), F32), jax.ShapeDtypeStruct((1, d), F32)],
        compiler_params=_params("arbitrary"),
    )(y, target)


def _sigmoid(v):
    return 0.5 * jnp.tanh(0.5 * v) + 0.5


FFN_TOKEN_TILE = 512
FFN_HIDDEN_TILE = 1408
NT_DIMS = (((1,), (1,)), ((), ()))


def _ffn_up(h, w_gate_t, w_up_t, *, name):
    t, d = h.shape
    f = w_gate_t.shape[0]
    tm, tn = _tile(t, FFN_TOKEN_TILE, SUBLANES), _tile(f, FFN_HIDDEN_TILE, LANES)

    def body(h_ref, g_ref, u_ref, s_ref, a_ref, b_ref):
        hv = h_ref[...].astype(BF16)
        av = lax.dot_general(hv, g_ref[...].astype(BF16), NT_DIMS, preferred_element_type=F32)
        bv = lax.dot_general(hv, u_ref[...].astype(BF16), NT_DIMS, preferred_element_type=F32)
        s_ref[...] = (av * _sigmoid(av) * bv).astype(BF16)
        a_ref[...] = av.astype(BF16)
        b_ref[...] = bv.astype(BF16)

    w_spec = pl.BlockSpec((tn, d), lambda j, i: (j, 0))
    out_spec = pl.BlockSpec((tm, tn), lambda j, i: (i, j))
    out = jax.ShapeDtypeStruct((t, f), BF16)
    return pl.pallas_call(
        body, name=name, grid=(f // tn, t // tm),
        in_specs=[pl.BlockSpec((tm, d), lambda j, i: (i, 0)), w_spec, w_spec],
        out_specs=[out_spec, out_spec, out_spec], out_shape=[out, out, out],
        compiler_params=_params("parallel", "parallel"),
    )(h, w_gate_t, w_up_t)


def _ffn_dact(dx, w_down, a, b, *, name):
    t, d = dx.shape
    f = w_down.shape[0]
    tm, tn = _tile(t, FFN_TOKEN_TILE, SUBLANES), _tile(f, FFN_HIDDEN_TILE, LANES)

    def body(dx_ref, w_ref, a_ref, b_ref, da_ref, db_ref):
        ds = lax.dot_general(dx_ref[...].astype(BF16), w_ref[...].astype(BF16), NT_DIMS, preferred_element_type=F32)
        av = a_ref[...].astype(F32)
        sg = _sigmoid(av)
        da_ref[...] = (ds * b_ref[...].astype(F32) * (sg * (1.0 + av * (1.0 - sg)))).astype(BF16)
        db_ref[...] = (ds * av * sg).astype(BF16)

    blk = pl.BlockSpec((tm, tn), lambda j, i: (i, j))
    out = jax.ShapeDtypeStruct((t, f), BF16)
    return pl.pallas_call(
        body, name=name, grid=(f // tn, t // tm),
        in_specs=[pl.BlockSpec((tm, d), lambda j, i: (i, 0)), pl.BlockSpec((tn, d), lambda j, i: (j, 0)), blk, blk],
        out_specs=[blk, blk], out_shape=[out, out],
        compiler_params=_params("parallel", "parallel"),
    )(dx, w_down, a, b)


def _ffn_fwd(x, gain, w_gate_t, w_up_t, w_down, tag):
    h = _rms_fwd(x, gain, name=f"ffn_norm_{tag}")
    s, a, b = _ffn_up(h, w_gate_t, w_up_t, name=f"ffn_up_{tag}")
    x_new = _matmul(s, w_down, "nn", resid=x, tn_cap=1024, tk_cap=2816, name=f"ffn_down_{tag}")
    return x_new, (x, h, a, b, s)


def _ffn_bwd(dx, saved, gain, w_gate_t, w_up_t, w_down, grads, rows, tag):
    x, h, a, b, s = saved
    da, db = _ffn_dact(dx, w_down, a, b, name=f"ffn_dact_{tag}")
    grads = _matmul_tn_into(grads, da, h, rows[0], name=f"ffn_dwgate_{tag}")
    grads = _matmul_tn_into(grads, db, h, rows[1], name=f"ffn_dwup_{tag}")
    grads = _matmul_tn_into(grads, s, dx, rows[2], name=f"ffn_dwdown_{tag}")
    dh = _matmul([da, db], [w_gate_t, w_up_t], "nn", tm_cap=512, tn_cap=1024, tk_cap=2816, name=f"ffn_dh_{tag}")
    dx_in, dgain = _rms_bwd(x, gain, dh, dx, name=f"ffn_dnorm_{tag}")
    return dx_in, grads, dgain


POOL_HALO = 16


def _shift_rows(v, k):
    n = v.shape[0]
    return pltpu.roll(v, k % n, 0)


def _window_sum(v, w, direction):
    k = 1
    while k < w:
        v = v + _shift_rows(v, direction * k)
        k *= 2
    return v


def _pool_fwd(u, x, w_group, scale, *, name):
    t, d = u.shape
    ng, dg = w_group.shape[0], w_group.shape[1]
    tm = _tile(t, 512, POOL_HALO)
    hb = tm // POOL_HALO

    def body(u_ref, halo_ref, x_ref, w_ref, s_ref, xo_ref, p_ref, y_ref):
        i, g = pl.program_id(0), pl.program_id(1)
        halo = jnp.where(i > 0, halo_ref[...], 0.0)
        ext = jnp.concatenate([halo, u_ref[...]], axis=0)
        pos = i * tm + lax.broadcasted_iota(jnp.int32, (tm, 1), 0)
        for gi, win in enumerate(POOL_WINDOWS):
            @pl.when(g == gi)
            def _(win=win):
                tot = _window_sum(ext, win, 1)[POOL_HALO:]
                cnt = jnp.minimum(pos + 1, win).astype(F32)
                p = (tot / cnt - u_ref[...]).astype(BF16)
                p_ref[...] = p
                y = jnp.dot(p, w_ref[...].astype(BF16), preferred_element_type=F32)
                y_ref[...] = y
                xo_ref[...] = x_ref[...] + y * s_ref[...]

    blk = pl.BlockSpec((tm, dg), lambda i, g: (i, g))
    return pl.pallas_call(
        body, name=name, grid=(t // tm, ng),
        in_specs=[blk, pl.BlockSpec((POOL_HALO, dg), lambda i, g: (jnp.maximum(i * hb - 1, 0), g)), blk,
                  pl.BlockSpec((None, dg, dg), lambda i, g: (g, 0, 0)), pl.BlockSpec((1, dg), lambda i, g: (0, g))],
        out_specs=[blk, blk, blk],
        out_shape=[jax.ShapeDtypeStruct((t, d), F32), jax.ShapeDtypeStruct((t, d), BF16),
                   jax.ShapeDtypeStruct((t, d), F32)],
        compiler_params=_params("parallel", "parallel"),
    )(u, u, x, w_group, scale)


def _pool_bwd(dx, p, y_pre, w_group, scale, *, name):
    t, d = dx.shape
    ng, dg = w_group.shape[0], w_group.shape[1]
    tm = _tile(t, 512, POOL_HALO)
    hb = tm // POOL_HALO
    nt = t // tm

    def body(dx_ref, nxt_ref, p_ref, y_ref, w_ref, s_ref, du_ref, dw_ref, ds_ref):
        g, i = pl.program_id(0), pl.program_id(1)

        @pl.when(i == 0)
        def _():
            dw_ref[...] = jnp.zeros_like(dw_ref)
            ds_ref[...] = jnp.zeros_like(ds_ref)

        dxv = dx_ref[...]
        ds_ref[...] += jnp.sum(dxv * y_ref[...], axis=0, keepdims=True)
        nxt = jnp.where(i < nt - 1, nxt_ref[...], 0.0)
        dyp = (jnp.concatenate([dxv, nxt], axis=0) * s_ref[...]).astype(BF16)
        dw_ref[...] += lax.dot_general(p_ref[...], dyp[:tm], (((0,), (0,)), ((), ())), preferred_element_type=F32)
        dp = lax.dot_general(dyp, w_ref[...].astype(BF16), (((1,), (1,)), ((), ())), preferred_element_type=F32)
        pos = i * tm + lax.broadcasted_iota(jnp.int32, (tm + POOL_HALO, 1), 0)
        for gi, win in enumerate(POOL_WINDOWS):
            @pl.when(g == gi)
            def _(win=win):
                q = dp / jnp.minimum(pos + 1, win).astype(F32)
                du_ref[...] = (_window_sum(q, win, -1)[:tm] - dp[:tm]).astype(BF16)

    blk = pl.BlockSpec((tm, dg), lambda g, i: (i, g))
    return pl.pallas_call(
        body, name=name, grid=(ng, nt),
        in_specs=[blk, pl.BlockSpec((POOL_HALO, dg), lambda g, i: (jnp.minimum((i + 1) * hb, t // POOL_HALO - 1), g)),
                  blk, blk, pl.BlockSpec((None, dg, dg), lambda g, i: (g, 0, 0)),
                  pl.BlockSpec((1, dg), lambda g, i: (0, g))],
        out_specs=[blk, pl.BlockSpec((None, dg, dg), lambda g, i: (g, 0, 0)), pl.BlockSpec((1, dg), lambda g, i: (0, g))],
        out_shape=[jax.ShapeDtypeStruct((t, d), BF16), jax.ShapeDtypeStruct((ng, dg, dg), F32),
                   jax.ShapeDtypeStruct((1, d), F32)],
        compiler_params=_params("parallel", "arbitrary"),
    )(dx, dx, p, y_pre, w_group, scale)


def _pool_mixer_fwd(x, gain, w_in, w_group, scale, tag):
    h = _rms_fwd(x, gain, name=f"pool_norm_{tag}")
    u = _matmul(h, w_in, "nn", name=f"pool_in_{tag}")
    x_new, p, y_pre = _pool_fwd(u, x, w_group, scale, name=f"pool_mix_{tag}")
    return x_new, (x, h, p, y_pre)


def _pool_mixer_bwd(dx, saved, gain, w_in, w_group, scale, grads, row_in, tag):
    x, h, p, y_pre = saved
    du, dw_group, dscale = _pool_bwd(dx, p, y_pre, w_group, scale, name=f"pool_dmix_{tag}")
    grads = _matmul_tn_into(grads, h, du, row_in, name=f"pool_dwin_{tag}")
    dh = _matmul(du, w_in, "nt", name=f"pool_dh_{tag}")
    dx_in, dgain = _rms_bwd(x, gain, dh, dx, name=f"pool_dnorm_{tag}")
    return dx_in, grads, dw_group, dscale, dgain


CONV_HALO = 8
NEG_BIG = -1e30


def _softplus(v):
    return jnp.maximum(v, 0.0) + jnp.log(1.0 + jnp.exp(-jnp.abs(v)))


def _conv_taps(ext, w_ref, off, rows):
    acc = None
    for k in range(SSD_CONV):
        shift = SSD_CONV - 1 - k
        v = (_shift_rows(ext, shift) if shift else ext)[off:off + rows] * w_ref[k:k + 1, :]
        acc = v if acc is None else acc + v
    return acc


def _ssd_conv_fwd(zx, conv_w, conv_b, col0, *, name):
    t = zx.shape[0]
    c = conv_w.shape[1]
    tm, tc = _tile(t, 512, CONV_HALO), _tile(c, 512, LANES)
    hb, cb0 = tm // CONV_HALO, col0 // tc
    assert col0 % tc == 0

    def body(x_ref, halo_ref, w_ref, b_ref, o_ref):
        halo = jnp.where(pl.program_id(0) > 0, halo_ref[...], 0.0)
        ext = jnp.concatenate([halo, x_ref[...]], axis=0)
        pre = _conv_taps(ext, w_ref, CONV_HALO, tm) + b_ref[...]
        o_ref[...] = pre * _sigmoid(pre)

    return pl.pallas_call(
        body, name=name, grid=(t // tm, c // tc),
        in_specs=[pl.BlockSpec((tm, tc), lambda i, j: (i, j + cb0)),
                  pl.BlockSpec((CONV_HALO, tc), lambda i, j: (jnp.maximum(i * hb - 1, 0), j + cb0)),
                  pl.BlockSpec((SSD_CONV, tc), lambda i, j: (0, j)), pl.BlockSpec((1, tc), lambda i, j: (0, j))],
        out_specs=pl.BlockSpec((tm, tc), lambda i, j: (i, j)),
        out_shape=jax.ShapeDtypeStruct((t, c), F32),
        compiler_params=_params("parallel", "parallel"),
    )(zx, zx, conv_w, conv_b)


def _ssd_conv_bwd(d_parts, zx, conv_w, conv_b, col0, *, name):
    t = zx.shape[0]
    c = conv_w.shape[1]
    tm, tc = _tile(t, 512, CONV_HALO), _tile(c, 512, LANES)
    hb, cb0, nt = tm // CONV_HALO, col0 // tc, t // tm
    last_halo = t // CONV_HALO - 1
    starts = [0]
    for part in d_parts:
        assert part.shape[1] % tc == 0
        starts.append(starts[-1] + part.shape[1] // tc)
    assert starts[-1] == c // tc
    n_parts = len(d_parts)

    def pick(refs, j):
        value = refs[-1][...]
        for p in reversed(range(n_parts - 1)):
            value = jnp.where(j < starts[p + 1], refs[p][...], value)
        return value

    def body(x_ref, prev_ref, nxt_ref, *rest):
        d_refs, dnxt_refs = rest[:n_parts], rest[n_parts:2 * n_parts]
        w_ref, b_ref, dx_ref, dw_ref, db_ref = rest[2 * n_parts:]
        j, i = pl.program_id(0), pl.program_id(1)

        @pl.when(i == 0)
        def _():
            dw_ref[...] = jnp.zeros_like(dw_ref)
            db_ref[...] = jnp.zeros_like(db_ref)

        prev = jnp.where(i > 0, prev_ref[...], 0.0)
        has_next = i < nt - 1
        ext = jnp.concatenate([prev, x_ref[...], jnp.where(has_next, nxt_ref[...], 0.0)], axis=0)
        pre = _conv_taps(ext, w_ref, CONV_HALO, tm + CONV_HALO) + b_ref[...]
        sg = _sigmoid(pre)
        dact = jnp.concatenate([pick(d_refs, j), jnp.where(has_next, pick(dnxt_refs, j), 0.0)], axis=0)
        dpre = dact * (sg * (1.0 + pre * (1.0 - sg)))
        db_ref[...] += jnp.sum(dpre[:tm], axis=0, keepdims=True)
        acc = None
        for k in range(SSD_CONV):
            shift = SSD_CONV - 1 - k
            src = (_shift_rows(ext, shift) if shift else ext)[CONV_HALO:CONV_HALO + tm]
            dw_ref[k:k + 1, :] += jnp.sum(dpre[:tm] * src, axis=0, keepdims=True)
            v = (_shift_rows(dpre, -shift) if shift else dpre)[:tm] * w_ref[k:k + 1, :]
            acc = v if acc is None else acc + v
        dx_ref[...] = acc.astype(BF16)

    def part_specs(rows, row_index):
        def spec(p):
            def index(j, i):
                mine = (j >= starts[p]) & (j < starts[p + 1])
                return jnp.where(mine, row_index(i), 0), jnp.where(mine, j - starts[p], 0)
            return pl.BlockSpec((rows, tc), index)
        return [spec(p) for p in range(n_parts)]

    main = lambda j, i: (i, j + cb0)
    next_halo = lambda i: jnp.minimum((i + 1) * hb, last_halo)
    return pl.pallas_call(
        body, name=name, grid=(c // tc, nt),
        in_specs=[pl.BlockSpec((tm, tc), main),
                  pl.BlockSpec((CONV_HALO, tc), lambda j, i: (jnp.maximum(i * hb - 1, 0), j + cb0)),
                  pl.BlockSpec((CONV_HALO, tc), lambda j, i: (next_halo(i), j + cb0)),
                  *part_specs(tm, lambda i: i), *part_specs(CONV_HALO, next_halo),
                  pl.BlockSpec((SSD_CONV, tc), lambda j, i: (0, j)), pl.BlockSpec((1, tc), lambda j, i: (0, j))],
        out_specs=[pl.BlockSpec((tm, tc), lambda j, i: (i, j)), pl.BlockSpec((SSD_CONV, tc), lambda j, i: (0, j)),
                   pl.BlockSpec((1, tc), lambda j, i: (0, j))],
        out_shape=[jax.ShapeDtypeStruct((t, c), BF16), jax.ShapeDtypeStruct((SSD_CONV, c), F32),
                   jax.ShapeDtypeStruct((1, c), F32)],
        compiler_params=_params("parallel", "arbitrary"),
    )(zx, zx, zx, *d_parts, *d_parts, conv_w, conv_b)


SSD_CUMSUM_PIECES = 2
SSD_GROUPS_PER_STEP = 1


def _ssd_group_pad(v, n_groups):
    lead = v.shape[:-1]
    v = v.reshape(*lead, n_groups, SSD_HEADS_PER_GROUP)
    v = jnp.pad(v, [(0, 0)] * (len(lead) + 1) + [(0, LANES - SSD_HEADS_PER_GROUP)])
    return v.reshape(*lead, n_groups * LANES)


def _ssd_group_unpad(v, n_groups):
    lead = v.shape[:-1]
    return v.reshape(*lead, n_groups, LANES)[..., :SSD_HEADS_PER_GROUP].reshape(*lead, -1)


def _ssd_chunk_common(dtp_ref, par_ref):
    ell = SSD_CHUNK
    dt = _softplus(dtp_ref[...] + par_ref[0:1, :])
    a = -jnp.exp(par_ref[1:2, :])
    row = lax.broadcasted_iota(jnp.int32, (ell, ell), 0)
    col = lax.broadcasted_iota(jnp.int32, (ell, ell), 1)
    acum = _split_dot(dt * a, (row >= col).astype(BF16), SSD_CUMSUM_PIECES, left=True)
    return dt, a, acum, acum.T, row, col


def _ssd_scan_fwd(xa, dtp, par, n_groups, *, name):
    t = xa.shape[0]
    ell, hd, hpg, ns, gps = SSD_CHUNK, SSD_HEAD_DIM, SSD_HEADS_PER_GROUP, SSD_STATE, SSD_GROUPS_PER_STEP
    gw = hpg * hd
    nc = t // ell
    b_blk0, c_blk0 = n_groups * gw // (ns * gps), (n_groups * gw // ns + n_groups) // gps

    def body(xs_ref, b_ref, c_ref, dtp_ref, par_ref, y_ref, sin_ref, st):
        @pl.when(pl.program_id(1) == 0)
        def _():
            st[...] = jnp.zeros_like(st)

        dt, _, acum, acum_t, row, col = _ssd_chunk_common(dtp_ref, par_ref)
        for gi in range(gps):
            bb = b_ref[:, gi * ns:(gi + 1) * ns].astype(BF16)
            cc = c_ref[:, gi * ns:(gi + 1) * ns].astype(BF16)
            cb = lax.dot_general(cc, bb, NT_DIMS, preferred_element_type=F32)
            s_all = st[gi]
            sin_ref[gi] = s_all
            c_s = lax.dot_general(cc, s_all.astype(BF16), NT_DIMS, preferred_element_type=F32)
            weighted, keep = [], []
            for hh in range(hpg):
                lanes = slice(gi * gw + hh * hd, gi * gw + (hh + 1) * hd)
                hl = gi * LANES + hh
                col_a, row_a = acum[:, hl:hl + 1], acum_t[hl:hl + 1, :]
                decay = jnp.exp(jnp.where(row >= col, col_a - row_a, NEG_BIG))
                xdt = xs_ref[:, lanes] * dt[:, hl:hl + 1]
                y = jnp.dot((cb * decay).astype(BF16), xdt.astype(BF16), preferred_element_type=F32)
                y_ref[:, lanes] = y + jnp.exp(col_a) * c_s[:, hh * hd:(hh + 1) * hd]
                a_last = acum[ell - 1:ell, hl:hl + 1]
                weighted.append((xdt * jnp.exp(a_last - col_a)).astype(BF16))
                keep.append(jnp.broadcast_to(jnp.exp(a_last), (hd, 1)))
            st[gi] = jnp.concatenate(keep, axis=0) * s_all + lax.dot_general(
                jnp.concatenate(weighted, axis=1), bb, (((0,), (0,)), ((), ())), preferred_element_type=F32)

    return pl.pallas_call(
        body, name=name, grid=(n_groups // gps, nc),
        in_specs=[pl.BlockSpec((ell, gps * gw), lambda g, c: (c, g)),
                  pl.BlockSpec((ell, gps * ns), lambda g, c: (c, b_blk0 + g)),
                  pl.BlockSpec((ell, gps * ns), lambda g, c: (c, c_blk0 + g)),
                  pl.BlockSpec((ell, gps * LANES), lambda g, c: (c, g)),
                  pl.BlockSpec((SUBLANES, gps * LANES), lambda g, c: (0, g))],
        out_specs=[pl.BlockSpec((ell, gps * gw), lambda g, c: (c, g)),
                   pl.BlockSpec((None, gps, gw, ns), lambda g, c: (c, g, 0, 0))],
        out_shape=[jax.ShapeDtypeStruct((t, n_groups * gw), F32),
                   jax.ShapeDtypeStruct((nc, n_groups, gw, ns), F32)],
        scratch_shapes=[pltpu.VMEM((gps, gw, ns), F32)],
        compiler_params=_params("parallel", "arbitrary"),
    )(xa, xa, xa, dtp, par)


def _ssd_scan_bwd(dy, xa, dtp, par, s_in, n_groups, *, name):
    t = xa.shape[0]
    ell, hd, hpg, ns, gps = SSD_CHUNK, SSD_HEAD_DIM, SSD_HEADS_PER_GROUP, SSD_STATE, SSD_GROUPS_PER_STEP
    gw = hpg * hd
    nc = t // ell
    b_blk0, c_blk0 = n_groups * gw // (ns * gps), (n_groups * gw // ns + n_groups) // gps
    nt_dims = (((1,), (1,)), ((), ()))
    tn_dims = (((0,), (0,)), ((), ()))

    def body(dy_ref, xs_ref, b_ref, c_ref, dtp_ref, par_ref, sin_ref,
             dxs_ref, db_ref, dc_ref, ddtp_ref, dpar_ref, dst):
        @pl.when(pl.program_id(1) == 0)
        def _():
            dst[...] = jnp.zeros_like(dst)
            dpar_ref[...] = jnp.zeros_like(dpar_ref)

        dtg, a_g, acum, acum_t, row, col = _ssd_chunk_common(dtp_ref, par_ref)
        lane = lax.broadcasted_iota(jnp.int32, (1, gps * LANES), 1)
        dacum = jnp.zeros((ell, gps * LANES), F32)
        xsum = jnp.zeros((ell, gps * LANES), F32)
        dsum = jnp.zeros((1, gps * LANES), F32)
        for gi, hh in [(gi, hh) for gi in range(gps) for hh in range(hpg)]:
            if hh == 0:
                bb = b_ref[:, gi * ns:(gi + 1) * ns].astype(BF16)
                cc = c_ref[:, gi * ns:(gi + 1) * ns].astype(BF16)
                cb = lax.dot_general(cc, bb, nt_dims, preferred_element_type=F32)
                cb_t = lax.dot_general(bb, cc, nt_dims, preferred_element_type=F32)
                dcb = jnp.zeros((ell, ell), F32)
                dcb_t = jnp.zeros((ell, ell), F32)
                s_all, ds_all = sin_ref[gi], dst[gi]
                c_s_all = lax.dot_general(cc, s_all.astype(BF16), nt_dims, preferred_element_type=F32)
                b_ds_all = lax.dot_general(bb, ds_all.astype(BF16), nt_dims, preferred_element_type=F32)
                s_ds = jnp.sum(s_all * ds_all, axis=1, keepdims=True)
                dy_decayed, x_weighted, keep = [], [], []
            lanes = slice(gi * gw + hh * hd, gi * gw + (hh + 1) * hd)
            head = slice(hh * hd, (hh + 1) * hd)
            hl = gi * LANES + hh
            onehot = (lane == hl).astype(F32)
            col_a, row_a = acum[:, hl:hl + 1], acum_t[hl:hl + 1, :]
            decay = jnp.exp(jnp.where(row >= col, col_a - row_a, NEG_BIG))
            decay_t = jnp.exp(jnp.where(col >= row, row_a - col_a, NEG_BIG))
            e_col = jnp.exp(col_a)
            a_last = acum[ell - 1:ell, hl:hl + 1]
            w = jnp.exp(a_last - col_a)
            e_last = jnp.exp(a_last)
            xs_h, dy_h = xs_ref[:, lanes], dy_ref[:, lanes]
            dt_h = dtg[:, hl:hl + 1]
            xdt = xs_h * dt_h
            xdt_b, dy_b = xdt.astype(BF16), dy_h.astype(BF16)
            dm_decay = lax.dot_general(dy_b, xdt_b, nt_dims, preferred_element_type=F32) * decay
            dm_decay_t = lax.dot_general(xdt_b, dy_b, nt_dims, preferred_element_type=F32) * decay_t
            dcb += dm_decay
            dcb_t += dm_decay_t
            m_t = cb_t * decay_t
            dac = jnp.sum(dm_decay * cb, axis=1, keepdims=True) - jnp.sum(dm_decay_t * cb_t, axis=1, keepdims=True)
            b_ds = b_ds_all[:, head]
            dxdt = jnp.dot(m_t.astype(BF16), dy_b, preferred_element_type=F32) + w * b_ds
            dac += jnp.sum(dy_h * c_s_all[:, head], axis=1, keepdims=True) * e_col
            q = jnp.sum(xdt * b_ds, axis=1, keepdims=True) * w
            dac -= q
            d_last = jnp.sum(q, axis=0, keepdims=True) + e_last * jnp.sum(s_ds[head], axis=0, keepdims=True)
            is_last = lax.broadcasted_iota(jnp.int32, (ell, 1), 0) == ell - 1
            dac += jnp.where(is_last, d_last, 0.0)
            dacum += dac * onehot
            dy_decayed.append((dy_h * e_col).astype(BF16))
            x_weighted.append((xdt * w).astype(BF16))
            keep.append(jnp.broadcast_to(e_last, (hd, 1)))
            dxs_ref[:, lanes] = dxdt * dt_h + dy_h * par_ref[2:3, hl:hl + 1]
            xsum += jnp.sum(dxdt * xs_h, axis=1, keepdims=True) * onehot
            dsum += jnp.sum(jnp.sum(dy_h * xs_h, axis=1, keepdims=True), axis=0, keepdims=True) * onehot
            if hh == hpg - 1:
                group = slice(gi * ns, (gi + 1) * ns)
                dy_all, x_all = jnp.concatenate(dy_decayed, axis=1), jnp.concatenate(x_weighted, axis=1)
                dc_ref[:, group] = (jnp.dot(dy_all, s_all.astype(BF16), preferred_element_type=F32)
                                    + jnp.dot(dcb.astype(BF16), bb, preferred_element_type=F32))
                db_ref[:, group] = (jnp.dot(x_all, ds_all.astype(BF16), preferred_element_type=F32)
                                    + jnp.dot(dcb_t.astype(BF16), cc, preferred_element_type=F32))
                dst[gi] = jnp.concatenate(keep, axis=0) * ds_all + lax.dot_general(
                    dy_all, cc, tn_dims, preferred_element_type=F32)
        dda = _split_dot(dacum, (col >= row).astype(BF16), SSD_CUMSUM_PIECES, left=True)
        ddtp = (xsum + dda * a_g) * _sigmoid(dtp_ref[...] + par_ref[0:1, :])
        ddtp_ref[...] = ddtp
        dpar_ref[0:1, :] += jnp.sum(ddtp, axis=0, keepdims=True)
        dpar_ref[1:2, :] += jnp.sum(dda * dtg, axis=0, keepdims=True) * a_g
        dpar_ref[2:3, :] += dsum

    rev = lambda i: nc - 1 - i
    return pl.pallas_call(
        body, name=name, grid=(n_groups // gps, nc),
        in_specs=[pl.BlockSpec((ell, gps * gw), lambda g, i: (rev(i), g)),
                  pl.BlockSpec((ell, gps * gw), lambda g, i: (rev(i), g)),
                  pl.BlockSpec((ell, gps * ns), lambda g, i: (rev(i), b_blk0 + g)),
                  pl.BlockSpec((ell, gps * ns), lambda g, i: (rev(i), c_blk0 + g)),
                  pl.BlockSpec((ell, gps * LANES), lambda g, i: (rev(i), g)),
                  pl.BlockSpec((SUBLANES, gps * LANES), lambda g, i: (0, g)),
                  pl.BlockSpec((None, gps, gw, ns), lambda g, i: (rev(i), g, 0, 0))],
        out_specs=[pl.BlockSpec((ell, gps * gw), lambda g, i: (rev(i), g)),
                   pl.BlockSpec((ell, gps * ns), lambda g, i: (rev(i), g)),
                   pl.BlockSpec((ell, gps * ns), lambda g, i: (rev(i), g)),
                   pl.BlockSpec((ell, gps * LANES), lambda g, i: (rev(i), g)),
                   pl.BlockSpec((SUBLANES, gps * LANES), lambda g, i: (0, g))],
        out_shape=[jax.ShapeDtypeStruct((t, n_groups * gw), F32), jax.ShapeDtypeStruct((t, n_groups * ns), F32),
                   jax.ShapeDtypeStruct((t, n_groups * ns), F32), jax.ShapeDtypeStruct((t, n_groups * LANES), F32),
                   jax.ShapeDtypeStruct((SUBLANES, n_groups * LANES), F32)],
        scratch_shapes=[pltpu.VMEM((gps, gw, ns), F32)],
        compiler_params=_params("parallel", "arbitrary"),
    )(dy, xa, xa, xa, dtp, par, s_in)


def _ssd_gate_fwd(y, xa, zx, d_rep, out_norm, *, name):
    t, di = y.shape
    gw = SSD_HEADS_PER_GROUP * SSD_HEAD_DIM
    tm = _tile(t, 512, SUBLANES)

    def body(y_ref, xs_ref, z_ref, d_ref, n_ref, o_ref):
        zv = z_ref[...]
        gt = (y_ref[...] + d_ref[...] * xs_ref[...]) * (zv * _sigmoid(zv))
        r = lax.rsqrt(jnp.mean(gt * gt, axis=-1, keepdims=True) + NORM_EPS)
        o_ref[...] = (gt * r * n_ref[...]).astype(BF16)

    blk = pl.BlockSpec((tm, gw), lambda i, g: (i, g))
    vec = pl.BlockSpec((1, gw), lambda i, g: (0, g))
    return pl.pallas_call(
        body, name=name, grid=(t // tm, di // gw),
        in_specs=[blk, blk, blk, vec, vec], out_specs=blk,
        out_shape=jax.ShapeDtypeStruct((t, di), BF16),
        compiler_params=_params("parallel", "parallel"),
    )(y, xa, zx, d_rep, out_norm)


def _ssd_gate_bwd(dgn, y, xa, zx, d_rep, out_norm, *, name):
    t, di = y.shape
    gw = SSD_HEADS_PER_GROUP * SSD_HEAD_DIM
    tm = _tile(t, 512, SUBLANES)

    def body(dg_ref, y_ref, xs_ref, z_ref, d_ref, n_ref, dy_ref, dz_ref, dn_ref):
        @pl.when(pl.program_id(1) == 0)
        def _():
            dn_ref[...] = jnp.zeros_like(dn_ref)

        zv = z_ref[...]
        sg = _sigmoid(zv)
        sz = zv * sg
        y2 = y_ref[...] + d_ref[...] * xs_ref[...]
        gt = y2 * sz
        r = lax.rsqrt(jnp.mean(gt * gt, axis=-1, keepdims=True) + NORM_EPS)
        ghat = gt * r
        dgv = dg_ref[...]
        dn_ref[...] += jnp.sum(dgv * ghat, axis=0, keepdims=True)
        u = dgv * n_ref[...]
        dgt = r * (u - ghat * jnp.mean(u * ghat, axis=-1, keepdims=True))
        dy_ref[...] = dgt * sz
        dz_ref[...] = (dgt * y2 * (sg * (1.0 + zv * (1.0 - sg)))).astype(BF16)

    blk = pl.BlockSpec((tm, gw), lambda g, i: (i, g))
    vec = pl.BlockSpec((1, gw), lambda g, i: (0, g))
    return pl.pallas_call(
        body, name=name, grid=(di // gw, t // tm),
        in_specs=[blk, blk, blk, blk, vec, vec], out_specs=[blk, blk, vec],
        out_shape=[jax.ShapeDtypeStruct((t, di), F32), jax.ShapeDtypeStruct((t, di), BF16),
                   jax.ShapeDtypeStruct((1, di), F32)],
        compiler_params=_params("parallel", "arbitrary"),
    )(dgn, y, xa, zx, d_rep, out_norm)


def _ssd_mixer_fwd(x, gain, w_zx_t, w_dt_t, conv_w, conv_b, par, d_rep, out_norm, w_out, tag):
    di = w_out.shape[0]
    n_groups = di // (SSD_HEADS_PER_GROUP * SSD_HEAD_DIM)
    h = _rms_fwd(x, gain, name=f"ssd_norm_{tag}")
    zx = _matmul(h, w_zx_t, "nt", name=f"ssd_in_{tag}")
    dtp = _matmul(h, w_dt_t, "nt", name=f"ssd_dt_{tag}")
    xa = _ssd_conv_fwd(zx, conv_w, conv_b, di, name=f"ssd_conv_{tag}")
    y, s_in = _ssd_scan_fwd(xa, dtp, par, n_groups, name=f"ssd_scan_{tag}")
    gn = _ssd_gate_fwd(y, xa, zx, d_rep, out_norm, name=f"ssd_gate_{tag}")
    x_new = _matmul(gn, w_out, "nn", resid=x, name=f"ssd_out_{tag}")
    return x_new, (x, h, zx, dtp, xa, y, s_in, gn)


def _ssd_mixer_bwd(dx, saved, gain, w_zx_t, w_dt_t, conv_w, conv_b, par, d_rep, out_norm, w_out, grads, row_out,
                   tag):
    x, h, zx, dtp, xa, y, s_in, gn = saved
    di = w_out.shape[0]
    n_groups = di // (SSD_HEADS_PER_GROUP * SSD_HEAD_DIM)
    dgn = _matmul(dx, w_out, "nt", name=f"ssd_dgn_{tag}")
    grads = _matmul_tn_into(grads, gn, dx, row_out, name=f"ssd_dwout_{tag}")
    dy2, dz, dnorm = _ssd_gate_bwd(dgn, y, xa, zx, d_rep, out_norm, name=f"ssd_dgate_{tag}")
    dxs, db, dc, ddtp, dpar = _ssd_scan_bwd(dy2, xa, dtp, par, s_in, n_groups, name=f"ssd_dscan_{tag}")
    dxbc, dconv_w, dconv_b = _ssd_conv_bwd([dxs, db, dc], zx, conv_w, conv_b, di, name=f"ssd_dconv_{tag}")
    dzx = jnp.concatenate([dz, dxbc], axis=1)
    dw_zx_t = _matmul_tn(dzx, h, name=f"ssd_dwin_{tag}")
    dw_dt_t = _matmul_tn(ddtp, h, name=f"ssd_dwdt_{tag}")
    dh = _matmul(dzx, w_zx_t, "nn", name=f"ssd_dh_{tag}")
    dh = _matmul(ddtp, w_dt_t, "nn", resid=dh, name=f"ssd_dhdt_{tag}")
    dx_in, dgain = _rms_bwd(x, gain, dh, dx, name=f"ssd_dnorm_{tag}")
    return dx_in, grads, dw_zx_t, dw_dt_t, dconv_w, dconv_b, dpar, dnorm, dgain


HEAD_SUM_PIECES = 2


def _head_sums(v):
    row = lax.broadcasted_iota(jnp.int32, (LANES, LANES), 0)
    col = lax.broadcasted_iota(jnp.int32, (LANES, LANES), 1)
    same_head = (row // SB_HEAD_DIM == col // SB_HEAD_DIM).astype(BF16)
    return _split_dot(v, same_head, HEAD_SUM_PIECES)


def _sb_qk_norm_fwd(qkv, gains, *, name):
    ns, t, _ = qkv.shape
    per = ns // 3
    tm = _tile(t, 1024, SUBLANES)
    inv_sqrt_d = 1.0 / math.sqrt(SB_HEAD_DIM)

    def body(x_ref, g_ref, o_ref):
        kind = pl.program_id(0) // per
        xv = x_ref[...]

        @pl.when(kind == 2)
        def _():
            o_ref[...] = xv.astype(BF16)

        @pl.when(kind < 2)
        def _():
            ms = _head_sums(xv * xv) * (1.0 / SB_HEAD_DIM)
            y = xv * lax.rsqrt(ms + NORM_EPS) * g_ref[pl.ds(kind, 1), :]
            o_ref[...] = (y * jnp.where(kind == 0, inv_sqrt_d, 1.0)).astype(BF16)

    blk = pl.BlockSpec((None, tm, LANES), lambda s, i: (s, i, 0))
    return pl.pallas_call(
        body, name=name, grid=(ns, t // tm),
        in_specs=[blk, pl.BlockSpec((SUBLANES, LANES), lambda s, i: (0, 0))], out_specs=blk,
        out_shape=jax.ShapeDtypeStruct((ns, t, LANES), BF16),
        compiler_params=_params("parallel", "parallel"),
    )(qkv, gains)


def _sb_qk_norm_bwd(dq, dk, dv, qkv, gains, *, name):
    ns, t, _ = qkv.shape
    per = ns // 3
    tm = _tile(t, 1024, SUBLANES)
    inv_sqrt_d = 1.0 / math.sqrt(SB_HEAD_DIM)

    def body(dq_ref, dk_ref, dv_ref, x_ref, g_ref, o_ref, dg_ref):
        s = pl.program_id(0)
        kind = s // per

        @pl.when((s == 0) & (pl.program_id(1) == 0))
        def _():
            dg_ref[...] = jnp.zeros_like(dg_ref)

        @pl.when(kind == 2)
        def _():
            o_ref[...] = dv_ref[...].astype(BF16)

        @pl.when(kind < 2)
        def _():
            xv = x_ref[...]
            dy = jnp.where(kind == 0, dq_ref[...] * inv_sqrt_d, dk_ref[...])
            r = lax.rsqrt(_head_sums(xv * xv) * (1.0 / SB_HEAD_DIM) + NORM_EPS)
            xhat = xv * r
            u = dy * g_ref[pl.ds(kind, 1), :]
            o_ref[...] = (r * (u - xhat * _head_sums(u * xhat) * (1.0 / SB_HEAD_DIM))).astype(BF16)
            dg_ref[pl.ds(kind, 1), :] += jnp.sum(dy * xhat, axis=0, keepdims=True)

    def grad_blk(kind):
        def index(s, i):
            mine = (s >= kind * per) & (s < (kind + 1) * per)
            return jnp.where(mine, s - kind * per, 0), jnp.where(mine, i, 0), 0
        return pl.BlockSpec((None, tm, LANES), index)

    blk = pl.BlockSpec((None, tm, LANES), lambda s, i: (s, i, 0))
    vec = pl.BlockSpec((SUBLANES, LANES), lambda s, i: (0, 0))
    return pl.pallas_call(
        body, name=name, grid=(ns, t // tm),
        in_specs=[grad_blk(0), grad_blk(1), grad_blk(2), blk, vec], out_specs=[blk, vec],
        out_shape=[jax.ShapeDtypeStruct((ns, t, LANES), BF16), jax.ShapeDtypeStruct((SUBLANES, LANES), F32)],
        compiler_params=_params("arbitrary", "arbitrary"),
    )(dq, dk, dv, qkv, gains)


def _split_dot(v, ones_mat, pieces, left=False):
    total, rest = None, v
    for p in range(pieces):
        part = rest.astype(BF16)
        if p + 1 < pieces:
            rest = rest - part.astype(F32)
        d = (jnp.dot(ones_mat, part, preferred_element_type=F32) if left
             else jnp.dot(part, ones_mat, preferred_element_type=F32))
        total = d if total is None else total + d
    return total


LOGIT_SUM_PIECES = 2
GRAD_SUM_PIECES = 1
LOG_WEIGHT_UNDERFLOW = -105.0


def _sb_attn_fwd(qkv_n, n_heads, *, name):
    ns, t, _ = qkv_n.shape
    per = ns // 3
    bq, blk, hd = SB_QUERY_BLOCK, SB_BLOCK, SB_HEAD_DIM
    nq, n_diag = t // bq, bq // blk

    def body(q_ref, k_ref, v_ref, o_ref, walk_ref):
        i = pl.program_id(1)
        row = lax.broadcasted_iota(jnp.int32, (blk, blk), 0)
        col = lax.broadcasted_iota(jnp.int32, (blk, blk), 1)
        later_keys = (row > col).astype(BF16)
        qry = lax.broadcasted_iota(jnp.int32, (bq, blk), 0)
        key = lax.broadcasted_iota(jnp.int32, (bq, blk), 1)

        def tile(kb, carry, key_offset):
            out = []
            start = pl.multiple_of(kb * blk, blk)
            for hf in range(2):
                lanes = slice(hf * hd, (hf + 1) * hd)
                run, acc = carry[hf]
                z = lax.dot_general(q_ref[:, lanes], k_ref[pl.ds(start, blk), lanes], NT_DIMS,
                                    preferred_element_type=F32)
                sp = _softplus(z)
                lm = -sp if key_offset is None else jnp.where(key + key_offset < qry, -sp, 0.0)
                after = _split_dot(lm, later_keys, LOGIT_SUM_PIECES) + run
                a = jnp.exp(z - sp + after)
                if key_offset is not None:
                    a = jnp.where(key + key_offset < qry, a, 0.0)
                acc = acc + jnp.dot(a.astype(BF16), v_ref[pl.ds(start, blk), lanes], preferred_element_type=F32)
                out.append((run + jnp.sum(lm, axis=1, keepdims=True), acc))
            return tuple(out)

        def live(carry):
            return jnp.max(jnp.maximum(carry[0][0], carry[1][0])) > LOG_WEIGHT_UNDERFLOW

        def step(state):
            s, _, carry = state
            carry = tile(n_diag * i - 1 - s, carry, None)
            return s + 1, live(carry), carry

        carry = tuple((jnp.zeros((bq, 1), F32), jnp.zeros((bq, hd), F32)) for _ in range(2))
        for j in reversed(range(n_diag)):
            carry = tile(n_diag * i + j, carry, j * blk)
        walked, _, carry = lax.while_loop(lambda st: (st[0] < n_diag * i) & st[1], step,
                                          (jnp.int32(0), live(carry), carry))
        o_ref[...] = jnp.concatenate([carry[0][1], carry[1][1]], axis=1)
        lane = lax.broadcasted_iota(jnp.int32, (1, LANES), 1)
        walk_ref[...] = jnp.where(lane < WALK_LANES, carry[0][0],
                                  jnp.where(lane < 2 * WALK_LANES, carry[1][0], walked.astype(F32)))

    q_blk = pl.BlockSpec((None, bq, LANES), lambda p, i: (p, i, 0))
    return pl.pallas_call(
        body, name=name, grid=(per, nq),
        in_specs=[q_blk, pl.BlockSpec((None, t, LANES), lambda p, i: (per + p, 0, 0)),
                  pl.BlockSpec((None, t, LANES), lambda p, i: (2 * per + p, 0, 0))],
        out_specs=[pl.BlockSpec((bq, LANES), lambda p, i: (i, p)), q_blk],
        out_shape=[jax.ShapeDtypeStruct((t, n_heads * hd), F32), jax.ShapeDtypeStruct((per, t, LANES), F32)],
        compiler_params=_params("parallel", "arbitrary"),
    )(qkv_n, qkv_n, qkv_n)


WALK_LANES = 43


def _sb_attn_bwd(do, walk, qkv_n, *, name):
    ns, t, _ = qkv_n.shape
    per = ns // 3
    bq, blk, hd = SB_QUERY_BLOCK, SB_BLOCK, SB_HEAD_DIM
    nq, n_diag = t // bq, bq // blk
    nt_dims = (((1,), (1,)), ((), ()))
    tn_dims = (((0,), (0,)), ((), ()))

    def body(q_ref, k_ref, v_ref, do_ref, walk_ref, dq_ref, dk_ref, dv_ref):
        i = pl.program_id(1)

        @pl.when(i == 0)
        def _():
            dk_ref[...] = jnp.zeros_like(dk_ref)
            dv_ref[...] = jnp.zeros_like(dv_ref)

        walk_t = walk_ref[...].T
        tots = [walk_t[hf * WALK_LANES:hf * WALK_LANES + 1, :] for hf in range(2)]
        reached = jnp.clip(jnp.max(walk_t[2 * WALK_LANES:2 * WALK_LANES + 1, :]).astype(jnp.int32), 0, n_diag * i)

        row = lax.broadcasted_iota(jnp.int32, (blk, blk), 0)
        col = lax.broadcasted_iota(jnp.int32, (blk, blk), 1)
        later_keys = (col > row).astype(BF16)
        earlier_keys = (col < row).astype(BF16)
        key = lax.broadcasted_iota(jnp.int32, (blk, bq), 0)
        qry = lax.broadcasted_iota(jnp.int32, (blk, bq), 1)
        halves = [slice(hf * hd, (hf + 1) * hd) for hf in range(2)]
        q_hs = [q_ref[:, lanes] for lanes in halves]
        do_bs = [do_ref[:, lanes].astype(BF16) for lanes in halves]

        def scores(kb, hf, key_offset):
            k_blk = k_ref[pl.ds(pl.multiple_of(kb * blk, blk), blk), halves[hf]]
            z = lax.dot_general(k_blk, q_hs[hf], nt_dims, preferred_element_type=F32)
            sp = _softplus(z)
            return k_blk, z, sp, (-sp if key_offset is None else jnp.where(key + key_offset < qry, -sp, 0.0))

        def tile(kb, carry, key_offset):
            out = []
            start = pl.multiple_of(kb * blk, blk)
            for hf, lanes in enumerate(halves):
                seen, gsum, dq = carry[hf]
                q_h, do_b = q_hs[hf], do_bs[hf]
                k_blk, z, sp, lm = scores(kb, hf, key_offset)
                blk_tot = jnp.sum(lm, axis=0, keepdims=True)
                after = _split_dot(lm, later_keys, LOGIT_SUM_PIECES, left=True) + (tots[hf] - seen - blk_tot)
                a = jnp.exp(z - sp + after)
                if key_offset is not None:
                    a = jnp.where(key + key_offset < qry, a, 0.0)
                da = lax.dot_general(v_ref[pl.ds(start, blk), lanes], do_b, nt_dims, preferred_element_type=F32)
                g = da * a
                before = _split_dot(g, earlier_keys, GRAD_SUM_PIECES, left=True) + gsum
                omb = jnp.exp(-sp)
                dz = g * omb - (1.0 - omb) * before
                if key_offset is not None:
                    dz = jnp.where(key + key_offset < qry, dz, 0.0)
                dz_b = dz.astype(BF16)
                dk_ref[pl.ds(start, blk), lanes] += jnp.dot(dz_b, q_h, preferred_element_type=F32)
                dv_ref[pl.ds(start, blk), lanes] += jnp.dot(a.astype(BF16), do_b, preferred_element_type=F32)
                dq = dq + lax.dot_general(dz_b, k_blk, tn_dims, preferred_element_type=F32)
                out.append((seen + blk_tot, gsum + jnp.sum(g, axis=0, keepdims=True), dq))
            return tuple(out)

        init = tuple((jnp.zeros((1, bq), F32), jnp.zeros((1, bq), F32), jnp.zeros((bq, hd), F32))
                     for _ in range(2))
        carry = lax.fori_loop(n_diag * i - reached, n_diag * i, lambda kb, c: tile(kb, c, None), init)
        for j in range(n_diag):
            carry = tile(n_diag * i + j, carry, j * blk)
        dq_ref[...] = jnp.concatenate([carry[0][2], carry[1][2]], axis=1)

    full = lambda off: pl.BlockSpec((None, t, LANES), lambda p, i: (off + p, 0, 0))
    q_blk = pl.BlockSpec((None, bq, LANES), lambda p, i: (p, i, 0))
    slab = jax.ShapeDtypeStruct((per, t, LANES), F32)
    return pl.pallas_call(
        body, name=name, grid=(per, nq),
        in_specs=[q_blk, full(per), full(2 * per), pl.BlockSpec((bq, LANES), lambda p, i: (i, p)), q_blk],
        out_specs=[q_blk, full(0), full(0)],
        out_shape=[slab, slab, slab],
        compiler_params=_params("parallel", "arbitrary"),
    )(qkv_n, qkv_n, qkv_n, do, walk)


def _sb_mixer_fwd(x, gain, w_qkv_t, qk_gains, w_out, tag):
    n_heads = w_out.shape[0] // SB_HEAD_DIM
    h = _rms_fwd(x, gain, name=f"sb_norm_{tag}")
    qkv = _matmul(h, w_qkv_t, "nt", out_slabs=True, tn_cap=512, name=f"sb_qkv_{tag}")
    qkv_n = _sb_qk_norm_fwd(qkv, qk_gains, name=f"sb_qknorm_{tag}")
    o, walk = _sb_attn_fwd(qkv_n, n_heads, name=f"sb_attn_{tag}")
    x_new = _matmul(o, w_out, "nn", resid=x, name=f"sb_out_{tag}")
    return x_new, (x, h, qkv, qkv_n, o, walk)


def _sb_mixer_bwd(dx, saved, gain, w_qkv_t, qk_gains, w_out, grads, row_qkv, row_out, tag):
    x, h, qkv, qkv_n, o, walk = saved
    do = _matmul(dx, w_out, "nt", name=f"sb_do_{tag}")
    grads = _matmul_tn_into(grads, o, dx, row_out, name=f"sb_dwout_{tag}")
    dq, dk, dv = _sb_attn_bwd(do, walk, qkv_n, name=f"sb_dattn_{tag}")
    dqkv, dqk_gains = _sb_qk_norm_bwd(dq, dk, dv, qkv, qk_gains, name=f"sb_dqknorm_{tag}")
    grads = _matmul_tn_into(grads, dqkv, h, row_qkv, a_slabs=True, name=f"sb_dwqkv_{tag}")
    dh = _matmul(dqkv, w_qkv_t, "nn", a_slabs=True, name=f"sb_dh_{tag}")
    dx_in, dgain = _rms_bwd(x, gain, dh, dx, name=f"sb_dnorm_{tag}")
    return dx_in, grads, dqk_gains, dgain


MESH = pl.DeviceIdType.MESH


def _position():
    return lax.axis_index("x"), lax.axis_index("y"), lax.axis_index("c")


def _all_gather(shard, *, name):
    rows, n = shard.shape
    space = pltpu.VMEM

    def body(x_ref, out_ref, send_sems, recv_sems, local_sem):
        x, y, c = _position()
        me, sibling = (x, y, c), (x, y, 1 - c)
        chips = [(1 - x, y), (x, 1 - y), (1 - x, 1 - y)]

        def block(px, py, pc):
            return out_ref.at[4 * px + 2 * py + pc]

        def copy(k, blk, to, src=None):
            return pltpu.make_async_remote_copy(
                src_ref=block(*blk) if src is None else src, dst_ref=block(*blk),
                send_sem=send_sems.at[k], recv_sem=recv_sems.at[k], device_id=to, device_id_type=MESH)

        mine = pltpu.make_async_copy(x_ref, block(*me), local_sem)
        mine.start()
        first = [copy(0, me, sibling, src=x_ref)]
        first += [copy(1 + j, me, (*chip, c), src=x_ref) for j, chip in enumerate(chips)]
        for cp in first:
            cp.start()
        passed = [copy(4 + j, (*chip, c), sibling) for j, chip in enumerate(chips)]
        for j, chip in enumerate(chips):
            copy(1 + j, (*chip, c), me).wait_recv()
            passed[j].start()
        copy(0, sibling, me).wait_recv()
        for j, chip in enumerate(chips):
            copy(4 + j, (*chip, 1 - c), me).wait_recv()
        for cp in first + passed:
            cp.wait_send()
        mine.wait()

    return pl.pallas_call(
        body, name=name,
        out_shape=jax.ShapeDtypeStruct((N_DEV, rows, n), shard.dtype),
        in_specs=[pl.BlockSpec(memory_space=space)], out_specs=pl.BlockSpec(memory_space=space),
        scratch_shapes=[pltpu.SemaphoreType.DMA((7,)), pltpu.SemaphoreType.DMA((7,)), pltpu.SemaphoreType.DMA],
        compiler_params=pltpu.CompilerParams(vmem_limit_bytes=V7X_VMEM_LIMIT_BYTES),
    )(shard)


def _all_gather_forwarding(shard, *, name):
    rows, n = shard.shape
    half = rows // 2
    assert rows % (4 * SUBLANES) == 0

    def body(x_ref, out_ref, send_sems, recv_sems, local_sem):
        x, y, c = _position()
        me, sibling = (x, y, c), (x, y, 1 - c)
        x_nbr, y_nbr, diag = (1 - x, y), (x, 1 - y), (1 - x, 1 - y)
        lower, upper = pl.ds(0, half), pl.ds(half, half)

        def block(px, py, pc, part=None):
            ref = out_ref.at[4 * px + 2 * py + pc]
            return ref if part is None else ref.at[part]

        def copy(k, blk, to, src=None, part=None):
            return pltpu.make_async_remote_copy(
                src_ref=block(*blk, part) if src is None else src, dst_ref=block(*blk, part),
                send_sem=send_sems.at[k], recv_sem=recv_sems.at[k], device_id=to, device_id_type=MESH)

        mine = pltpu.make_async_copy(x_ref, block(*me), local_sem)
        mine.start()
        sent = [copy(0, me, sibling, src=x_ref), copy(1, me, (*x_nbr, c), src=x_ref),
                copy(2, me, (*y_nbr, c), src=x_ref)]
        for cp in sent:
            cp.start()
        copy(1, (*x_nbr, c), me).wait_recv()
        onward = [copy(3, (*x_nbr, c), (*y_nbr, c), part=lower), copy(5, (*x_nbr, c), sibling)]
        for cp in onward:
            cp.start()
        copy(2, (*y_nbr, c), me).wait_recv()
        onward += [copy(4, (*y_nbr, c), (*x_nbr, c), part=upper), copy(6, (*y_nbr, c), sibling)]
        for cp in onward[2:]:
            cp.start()
        copy(3, (*diag, c), me, part=lower).wait_recv()
        copy(4, (*diag, c), me, part=upper).wait_recv()
        onward.append(copy(7, (*diag, c), sibling))
        onward[-1].start()
        sent += onward
        copy(0, sibling, me).wait_recv()
        for k, chip in ((5, x_nbr), (6, y_nbr), (7, diag)):
            copy(k, (*chip, 1 - c), me).wait_recv()
        for cp in sent:
            cp.wait_send()
        mine.wait()

    hbm = pl.BlockSpec(memory_space=pltpu.HBM)
    return pl.pallas_call(
        body, name=name,
        out_shape=jax.ShapeDtypeStruct((N_DEV, rows, n), shard.dtype), in_specs=[hbm], out_specs=hbm,
        scratch_shapes=[pltpu.SemaphoreType.DMA((8,)), pltpu.SemaphoreType.DMA((8,)), pltpu.SemaphoreType.DMA],
    )(shard)


def _exchange_sibling(parts, *, name):
    _, nchip, rows, n = parts.shape

    def body(p_ref, recv_ref, send_sem, recv_sem):
        x, y, c = _position()
        cp = pltpu.make_async_remote_copy(src_ref=p_ref.at[1 - c], dst_ref=recv_ref, send_sem=send_sem,
                                          recv_sem=recv_sem, device_id=(x, y, 1 - c), device_id_type=MESH)
        cp.start()
        cp.wait()

    return pl.pallas_call(
        body, name=name,
        out_shape=jax.ShapeDtypeStruct((nchip, rows, n), parts.dtype),
        in_specs=[pl.BlockSpec(memory_space=pltpu.HBM)], out_specs=pl.BlockSpec(memory_space=pltpu.HBM),
        scratch_shapes=[pltpu.SemaphoreType.DMA, pltpu.SemaphoreType.DMA],
    )(parts)


def _exchange_chips(chip_sums, *, name):
    _, rows, n = chip_sums.shape

    def body(s_ref, recv_ref, send_sems, recv_sems):
        x, y, c = _position()
        chips = [(1 - x, y), (x, 1 - y), (1 - x, 1 - y)]
        copies = [pltpu.make_async_remote_copy(
            src_ref=s_ref.at[2 * cx + cy], dst_ref=recv_ref.at[j], send_sem=send_sems.at[j],
            recv_sem=recv_sems.at[j], device_id=(cx, cy, c), device_id_type=MESH)
            for j, (cx, cy) in enumerate(chips)]
        for cp in copies:
            cp.start()
        for cp in copies:
            cp.wait()

    return pl.pallas_call(
        body, name=name,
        out_shape=jax.ShapeDtypeStruct((3, rows, n), chip_sums.dtype),
        in_specs=[pl.BlockSpec(memory_space=pltpu.HBM)], out_specs=pl.BlockSpec(memory_space=pltpu.HBM),
        scratch_shapes=[pltpu.SemaphoreType.DMA((3,)), pltpu.SemaphoreType.DMA((3,))],
    )(chip_sums)


def _add_pairs(parts, recv, c_and_chip, *, name):
    _, nchip, rows, n = parts.shape
    tr = _tile(rows, 512, SUBLANES)

    def body(pos_ref, a_ref, b_ref, own_ref, wire_ref):
        s = a_ref[...] + b_ref[...]
        wire_ref[...] = s.astype(WIRE_DTYPE)

        @pl.when(pl.program_id(1) == pos_ref[1])
        def _():
            own_ref[...] = s

    return pl.pallas_call(
        body, name=name,
        grid_spec=pltpu.PrefetchScalarGridSpec(
            num_scalar_prefetch=1, grid=(rows // tr, nchip),
            in_specs=[pl.BlockSpec((None, None, tr, n), lambda i, k, pos: (pos[0], k, i, 0)),
                      pl.BlockSpec((None, tr, n), lambda i, k, pos: (k, i, 0))],
            out_specs=[pl.BlockSpec((tr, n), lambda i, k, pos: (i, 0)),
                       pl.BlockSpec((None, tr, n), lambda i, k, pos: (k, i, 0))]),
        out_shape=[jax.ShapeDtypeStruct((rows, n), parts.dtype),
                   jax.ShapeDtypeStruct((nchip, rows, n), WIRE_DTYPE)],
        compiler_params=_params("parallel", "arbitrary"),
    )(c_and_chip, parts, recv)


def _adamw_math(w, g, m, v):
    m = ADAM_B1 * m + (1.0 - ADAM_B1) * g
    v = ADAM_B2 * v + (1.0 - ADAM_B2) * (g * g)
    m_hat = m / (1.0 - ADAM_B1 ** ADAM_STEP)
    v_hat = v / (1.0 - ADAM_B2 ** ADAM_STEP)
    delta = -ADAM_LR * (m_hat / (jnp.sqrt(v_hat) + ADAM_EPS) + ADAM_WD * w)
    return delta, m, v


def _adamw_sharded(own_sum, recv, w, m, v, *, name):
    rows, n = w.shape
    tr = _tile(rows, 256, SUBLANES)

    def body(s_ref, r_ref, w_ref, m_ref, v_ref, g_out, d_out, m_out, v_out):
        g = ((s_ref[...] + r_ref[0].astype(F32)) + r_ref[1].astype(F32)) + r_ref[2].astype(F32)
        delta, m_new, v_new = _adamw_math(w_ref[...], g, m_ref[...], v_ref[...])
        g_out[...] = g
        d_out[...] = delta
        m_out[...] = m_new
        v_out[...] = v_new

    blk = pl.BlockSpec((tr, n), lambda i: (i, 0))
    out = jax.ShapeDtypeStruct((rows, n), F32)
    return pl.pallas_call(
        body, name=name, grid=(rows // tr,),
        in_specs=[blk, pl.BlockSpec((3, tr, n), lambda i: (0, i, 0)), blk, blk, blk],
        out_specs=[blk, blk, blk, blk], out_shape=[out, out, out, out],
        compiler_params=_params("parallel"),
    )(own_sum, recv, w, m, v)


SMALL_ROWS = 40
ROW_MIX_NORM, ROW_FFN_NORM, ROW_CONV_B, ROW_OUT_NORM, ROW_POOL_SCALE, ROW_CONV_W = 0, 4, 8, 12, 14, 16
ROW_SSD_VEC, ROW_QK_GAIN, ROW_LOSS = 32, 33, 34


def _adamw_small(gathered, w, m, v, *, name):
    _, rows, n = gathered.shape

    def body(a_ref, w_ref, m_ref, v_ref, g_out, d_out, m_out, v_out):
        g = a_ref[0]
        for d in range(1, N_DEV):
            g = g + a_ref[d]
        row = lax.broadcasted_iota(jnp.int32, (rows, 1), 0)
        g = jnp.where(row == ROW_QK_GAIN, g + pltpu.roll(g, SB_HEAD_DIM, 1), g)
        g = jnp.where(row == ROW_LOSS, jnp.sum(g, axis=1, keepdims=True), g)
        g_out[...] = g
        delta, m_new, v_new = _adamw_math(w_ref[...], g, m_ref[...], v_ref[...])
        d_out[...] = delta
        m_out[...] = m_new
        v_out[...] = v_new

    out = jax.ShapeDtypeStruct((rows, n), F32)
    return pl.pallas_call(body, name=name, out_shape=[out, out, out, out])(gathered, w, m, v)


BIG_WEIGHTS = ("ffn_gate", "ffn_up", "ffn_down", "sb_qkv", "ssd_out", "pool_in", "sb_out", "pool_group", "ssd_in")
COLUMN_SHARDED = ("ssd_in", "sb_qkv", "ffn_gate", "ffn_up")
ROW_PAD = 512
WIRE_DTYPE = jnp.bfloat16


def _to_rows(name, shard, d):
    if name in COLUMN_SHARDED:
        shard = jnp.swapaxes(shard, -1, -2)
    return shard.reshape(-1, d)


def _from_rows(name, rows, shard_shape):
    if name in COLUMN_SHARDED:
        lead, k, n = shard_shape
        return jnp.swapaxes(rows.reshape(lead, n, k), -1, -2)
    return rows.reshape(shard_shape)


def _pad_rows(a, total):
    return jnp.pad(a, ((0, total - a.shape[0]),) + ((0, 0),) * (a.ndim - 1))


def _exact_bf16_rows(v, d):
    words = lax.bitcast_convert_type(v.reshape(-1), WIRE_DTYPE).reshape(-1)
    return _pad_rows(words, -(-words.shape[0] // d) * d).reshape(-1, d)


def _exact_f32(rows, count):
    words = rows.reshape(rows.shape[0], -1)[:, :2 * count].reshape(rows.shape[0], count, 2)
    return lax.bitcast_convert_type(words, F32)


def _device_blocks(full, d):
    return full.reshape(N_DEV, -1, d)


def kernel(x, mix_norm, pool_in, pool_group, pool_scale, ssd_in, ssd_conv_w, ssd_conv_b, ssd_dt_bias, ssd_a_log, ssd_d, ssd_out_norm, ssd_out, sb_qkv, sb_q_norm, sb_k_norm, sb_out, ffn_norm, ffn_gate, ffn_up, ffn_down, loss_target, m_mix_norm, m_pool_in, m_pool_group, m_pool_scale, m_ssd_in, m_ssd_conv_w, m_ssd_conv_b, m_ssd_dt_bias, m_ssd_a_log, m_ssd_d, m_ssd_out_norm, m_ssd_out, m_sb_qkv, m_sb_q_norm, m_sb_k_norm, m_sb_out, m_ffn_norm, m_ffn_gate, m_ffn_up, m_ffn_down, v_mix_norm, v_pool_in, v_pool_group, v_pool_scale, v_ssd_in, v_ssd_conv_w, v_ssd_conv_b, v_ssd_dt_bias, v_ssd_a_log, v_ssd_d, v_ssd_out_norm, v_ssd_out, v_sb_qkv, v_sb_q_norm, v_sb_k_norm, v_sb_out, v_ffn_norm, v_ffn_gate, v_ffn_up, v_ffn_down):
    weights = dict(mix_norm=mix_norm, pool_in=pool_in, pool_group=pool_group, pool_scale=pool_scale, ssd_in=ssd_in,
                   ssd_conv_w=ssd_conv_w, ssd_conv_b=ssd_conv_b, ssd_dt_bias=ssd_dt_bias, ssd_a_log=ssd_a_log,
                   ssd_d=ssd_d, ssd_out_norm=ssd_out_norm, ssd_out=ssd_out, sb_qkv=sb_qkv, sb_q_norm=sb_q_norm,
                   sb_k_norm=sb_k_norm, sb_out=sb_out, ffn_norm=ffn_norm, ffn_gate=ffn_gate, ffn_up=ffn_up,
                   ffn_down=ffn_down)
    mom1 = dict(mix_norm=m_mix_norm, pool_in=m_pool_in, pool_group=m_pool_group, pool_scale=m_pool_scale,
                ssd_in=m_ssd_in, ssd_conv_w=m_ssd_conv_w, ssd_conv_b=m_ssd_conv_b, ssd_dt_bias=m_ssd_dt_bias,
                ssd_a_log=m_ssd_a_log, ssd_d=m_ssd_d, ssd_out_norm=m_ssd_out_norm, ssd_out=m_ssd_out,
                sb_qkv=m_sb_qkv, sb_q_norm=m_sb_q_norm, sb_k_norm=m_sb_k_norm, sb_out=m_sb_out,
                ffn_norm=m_ffn_norm, ffn_gate=m_ffn_gate, ffn_up=m_ffn_up, ffn_down=m_ffn_down)
    mom2 = dict(mix_norm=v_mix_norm, pool_in=v_pool_in, pool_group=v_pool_group, pool_scale=v_pool_scale,
                ssd_in=v_ssd_in, ssd_conv_w=v_ssd_conv_w, ssd_conv_b=v_ssd_conv_b, ssd_dt_bias=v_ssd_dt_bias,
                ssd_a_log=v_ssd_a_log, ssd_d=v_ssd_d, ssd_out_norm=v_ssd_out_norm, ssd_out=v_ssd_out,
                sb_qkv=v_sb_qkv, sb_q_norm=v_sb_q_norm, sb_k_norm=v_sb_k_norm, sb_out=v_sb_out,
                ffn_norm=v_ffn_norm, ffn_gate=v_ffn_gate, ffn_up=v_ffn_up, ffn_down=v_ffn_down)
    names = list(weights)
    depth, d = mix_norm.shape
    xs, ys, cs = _position()
    dev = 4 * xs + 2 * ys + cs
    chip = 2 * xs + ys

    seg = {}
    row = 0
    for name in BIG_WEIGHTS:
        n_rows = weights[name].size // d
        seg[name] = (row, n_rows)
        row += -(-n_rows // SUBLANES) * SUBLANES
    big_rows = row
    n_scale, n_convw = pool_scale.size, ssd_conv_w.size
    exact = jnp.concatenate([_exact_bf16_rows(pool_scale, d), _exact_bf16_rows(ssd_conv_w, d)], axis=0)
    scale_rows = _exact_bf16_rows(pool_scale, d).shape[0]
    packed_rows = -(-(big_rows + exact.shape[0]) // ROW_PAD) * ROW_PAD

    def pack(tree, dtype):
        ends = [seg[n][0] for n in BIG_WEIGHTS[1:]] + [big_rows]
        return jnp.concatenate([_pad_rows(_to_rows(n, tree[n], d).astype(dtype), end - seg[n][0])
                                for n, end in zip(BIG_WEIGHTS, ends)], axis=0)

    w_wire = _pad_rows(jnp.concatenate([pack(weights, WIRE_DTYPE), exact], axis=0), packed_rows)
    gathered = _all_gather_forwarding(w_wire, name="gather_weights")

    def seg_of(name):
        a, n = seg[name]
        return gathered[:, a:a + n]

    n_pool, n_ssd, n_sb = pool_in.shape[0], ssd_in.shape[0], sb_qkv.shape[0]
    assert n_ssd == 1 and n_sb == 1
    w_pool_in = seg_of("pool_in").reshape(N_DEV, n_pool, -1, d).transpose(1, 0, 2, 3).reshape(n_pool, d, d)
    grp = pool_group.shape
    w_pool_group = seg_of("pool_group").reshape(N_DEV, grp[0], grp[1], grp[2], grp[3]).transpose(1, 2, 0, 3, 4)
    w_pool_group = w_pool_group.reshape(grp[0], grp[1], grp[3], grp[3])
    w_ssd_in_t = seg_of("ssd_in").reshape(-1, d)
    w_ssd_out = seg_of("ssd_out").reshape(-1, d)
    w_sb_qkv_t = seg_of("sb_qkv").reshape(-1, d)
    w_sb_out = seg_of("sb_out").reshape(-1, d)
    hidden = ffn_down.shape[1] * N_DEV
    w_gate_t = seg_of("ffn_gate").reshape(N_DEV, depth, -1, d).transpose(1, 0, 2, 3).reshape(depth, hidden, d)
    w_up_t = seg_of("ffn_up").reshape(N_DEV, depth, -1, d).transpose(1, 0, 2, 3).reshape(depth, hidden, d)
    w_down = seg_of("ffn_down").reshape(N_DEV, depth, -1, d).transpose(1, 0, 2, 3).reshape(depth, hidden, d)
    exact_all = gathered[:, big_rows:big_rows + exact.shape[0]]
    scale_full = _exact_f32(exact_all[:, :scale_rows], n_scale).reshape(N_DEV, n_pool, -1)
    scale_full = scale_full.transpose(1, 0, 2).reshape(n_pool, d)
    convw_full = _exact_f32(exact_all[:, scale_rows:], n_convw).reshape(N_DEV, SSD_CONV, -1)
    convw_full = convw_full.transpose(1, 0, 2).reshape(SSD_CONV, -1)

    d_inner = w_ssd_out.shape[0]
    n_zx = w_ssd_in_t.shape[0] - ssd_dt_bias.shape[1]
    w_zx_t = w_ssd_in_t[:n_zx]
    n_ssd_heads = ssd_dt_bias.shape[1]
    n_ssd_groups = n_ssd_heads // SSD_HEADS_PER_GROUP
    w_dt_t = _ssd_group_pad(w_ssd_in_t[n_zx:].T, n_ssd_groups).T
    par = _pad_rows(_ssd_group_pad(jnp.concatenate([ssd_dt_bias, ssd_a_log, ssd_d], axis=0), n_ssd_groups), SUBLANES)
    d_rep = jnp.repeat(ssd_d[0], SSD_HEAD_DIM)[None]
    qk_gains = jnp.zeros((SUBLANES, LANES), F32).at[0].set(jnp.tile(sb_q_norm[0], 2)).at[1].set(jnp.tile(sb_k_norm[0], 2))

    act = x[0]
    saved = []
    for i in range(depth):
        kind, j = i % 3, i // 3
        gain = mix_norm[i:i + 1]
        if kind == 0:
            act, s = _pool_mixer_fwd(act, gain, w_pool_in[j], w_pool_group[j], scale_full[j:j + 1], f"l{i}")
        elif kind == 1:
            act, s = _ssd_mixer_fwd(act, gain, w_zx_t, w_dt_t, convw_full, ssd_conv_b, par, d_rep, ssd_out_norm,
                                    w_ssd_out, f"l{i}")
        else:
            act, s = _sb_mixer_fwd(act, gain, w_sb_qkv_t, qk_gains, w_sb_out, f"l{i}")
        act, f = _ffn_fwd(act, ffn_norm[i:i + 1], w_gate_t[i], w_up_t[i], w_down[i], f"l{i}")
        saved.append((s, f))
    dact, loss_cols = _loss_head(act, loss_target[0], name="loss_head")

    def layer_row(name, layer):
        return seg[name][0] + layer * (seg[name][1] // weights[name].shape[0])

    grads = jnp.zeros((N_DEV, packed_rows, d), F32)
    g_mix_norm, g_ffn_norm = [None] * depth, [None] * depth
    g_pool_group, g_pool_scale = [None] * n_pool, [None] * n_pool
    for i in reversed(range(depth)):
        kind, j = i % 3, i // 3
        gain = mix_norm[i:i + 1]
        s, f = saved[i]
        dact, grads, g_ffn_norm[i] = _ffn_bwd(
            dact, f, ffn_norm[i:i + 1], w_gate_t[i], w_up_t[i], w_down[i], grads,
            [layer_row(n, i) for n in ("ffn_gate", "ffn_up", "ffn_down")], f"l{i}")
        if kind == 0:
            dact, grads, g_pool_group[j], g_pool_scale[j], g_mix_norm[i] = _pool_mixer_bwd(
                dact, s, gain, w_pool_in[j], w_pool_group[j], scale_full[j:j + 1], grads, layer_row("pool_in", j),
                f"l{i}")
        elif kind == 1:
            (dact, grads, g_zx_t, g_dt_t, g_conv_w, g_conv_b, g_par, g_out_norm,
             g_mix_norm[i]) = _ssd_mixer_bwd(dact, s, gain, w_zx_t, w_dt_t, convw_full, ssd_conv_b, par, d_rep,
                                             ssd_out_norm, w_ssd_out, grads, layer_row("ssd_out", j), f"l{i}")
        else:
            dact, grads, g_qk_gains, g_mix_norm[i] = _sb_mixer_bwd(
                dact, s, gain, w_sb_qkv_t, qk_gains, w_sb_out, grads, layer_row("sb_qkv", j),
                layer_row("sb_out", j), f"l{i}")
    grad_x = dact[None]

    g_ssd_in = jnp.concatenate([g_zx_t, _ssd_group_unpad(g_dt_t.T, n_ssd_groups).T], axis=0)
    g_group = jnp.concatenate([_device_blocks(gg[k], d) for gg in g_pool_group for k in range(gg.shape[0])], axis=1)
    for name, blocks in (("ssd_in", _device_blocks(g_ssd_in, d)), ("pool_group", g_group)):
        blocks = blocks.reshape(N_DEV // 2, 2, -1, d).swapaxes(0, 1).reshape(N_DEV, -1, d)
        grads = lax.dynamic_update_slice(grads, blocks, (0, seg[name][0], 0))
    parts = grads.reshape(2, N_DEV // 2, packed_rows, d)
    from_sibling = _exchange_sibling(parts, name="reduce_sibling")
    own_sum, chip_sums_wire = _add_pairs(parts, from_sibling, jnp.stack([cs, chip]).astype(jnp.int32),
                                         name="reduce_sibling_add")
    from_chips = _exchange_chips(chip_sums_wire, name="reduce_chips")

    def pack_f32(tree):
        return _pad_rows(pack(tree, F32), packed_rows)

    big_out = _adamw_sharded(own_sum, from_chips, pack_f32(weights), pack_f32(mom1), pack_f32(mom2),
                             name="adamw_sharded")

    def small_pack(mix, ffn, conv_b, out_norm, scale, conv_w, vec, qk, loss=None):
        buf = jnp.zeros((SMALL_ROWS, d), F32)
        buf = buf.at[ROW_MIX_NORM:ROW_MIX_NORM + depth].set(mix).at[ROW_FFN_NORM:ROW_FFN_NORM + depth].set(ffn)
        buf = buf.at[ROW_CONV_B:ROW_CONV_B + conv_b.size // d].set(conv_b.reshape(-1, d))
        buf = buf.at[ROW_OUT_NORM:ROW_OUT_NORM + out_norm.size // d].set(out_norm.reshape(-1, d))
        buf = buf.at[ROW_POOL_SCALE:ROW_POOL_SCALE + n_pool].set(scale)
        buf = buf.at[ROW_CONV_W:ROW_CONV_W + conv_w.size // d].set(conv_w.reshape(-1, d))
        buf = buf.at[ROW_SSD_VEC].set(vec.reshape(-1)).at[ROW_QK_GAIN].set(qk.reshape(-1))
        if loss is not None:
            buf = buf.at[ROW_LOSS].set(loss.reshape(-1))
        return buf

    def small_params(tree):
        scale = lax.dynamic_update_slice(jnp.zeros((n_pool, d), F32), tree["pool_scale"],
                                         (0, dev * tree["pool_scale"].shape[1]))
        conv_w = lax.dynamic_update_slice(jnp.zeros(convw_full.shape, F32), tree["ssd_conv_w"][0],
                                          (0, dev * tree["ssd_conv_w"].shape[2]))
        vec = jnp.zeros((SUBLANES, LANES), F32)
        vec = vec.at[0, :n_ssd_heads].set(tree["ssd_dt_bias"][0]).at[1, :n_ssd_heads].set(tree["ssd_a_log"][0])
        vec = vec.at[2, :n_ssd_heads].set(tree["ssd_d"][0])
        qk = jnp.zeros((SUBLANES, LANES), F32)
        qk = qk.at[0, SB_HEAD_DIM:].set(tree["sb_q_norm"][0]).at[1, SB_HEAD_DIM:].set(tree["sb_k_norm"][0])
        return small_pack(tree["mix_norm"], tree["ffn_norm"], tree["ssd_conv_b"], tree["ssd_out_norm"], scale,
                          conv_w, vec, qk)

    small_partial = small_pack(jnp.concatenate(g_mix_norm, axis=0), jnp.concatenate(g_ffn_norm, axis=0), g_conv_b,
                               g_out_norm, jnp.concatenate(g_pool_scale, axis=0), g_conv_w,
                               jnp.zeros((SUBLANES, LANES), F32).at[:3, :n_ssd_heads].set(
                                   _ssd_group_unpad(g_par[:3], n_ssd_groups)), g_qk_gains,
                               loss_cols)
    small_all = _all_gather(small_partial, name="gather_small")
    small_out = _adamw_small(small_all, small_params(weights), small_params(mom1), small_params(mom2),
                             name="adamw_small")
    loss = small_out[0][ROW_LOSS, 0]

    def unpack(big, small):
        out = {}
        for name in BIG_WEIGHTS:
            a, n = seg[name]
            out[name] = _from_rows(name, big[a:a + n], weights[name].shape)
        out["mix_norm"] = small[ROW_MIX_NORM:ROW_MIX_NORM + depth]
        out["ffn_norm"] = small[ROW_FFN_NORM:ROW_FFN_NORM + depth]
        out["ssd_conv_b"] = small[ROW_CONV_B:ROW_CONV_B + ssd_conv_b.size // d].reshape(ssd_conv_b.shape)
        out["ssd_out_norm"] = small[ROW_OUT_NORM:ROW_OUT_NORM + ssd_out_norm.size // d].reshape(ssd_out_norm.shape)
        out["pool_scale"] = lax.dynamic_slice(small[ROW_POOL_SCALE:ROW_POOL_SCALE + n_pool],
                                              (0, dev * pool_scale.shape[1]), pool_scale.shape)
        conv_w = small[ROW_CONV_W:ROW_CONV_W + convw_full.size // d].reshape(convw_full.shape)
        out["ssd_conv_w"] = lax.dynamic_slice(conv_w, (0, dev * ssd_conv_w.shape[2]), ssd_conv_w.shape[1:])[None]
        vec = small[ROW_SSD_VEC].reshape(SUBLANES, LANES)
        out["ssd_dt_bias"], out["ssd_a_log"], out["ssd_d"] = (vec[r:r + 1, :n_ssd_heads] for r in range(3))
        qk = small[ROW_QK_GAIN].reshape(SUBLANES, LANES)
        out["sb_q_norm"], out["sb_k_norm"] = qk[0:1, SB_HEAD_DIM:], qk[1:2, SB_HEAD_DIM:]
        return [out[n] for n in names]

    results = [unpack(b, s) for b, s in zip(big_out, small_out)]
    return (loss, grad_x, *results[0], *results[1], *results[2], *results[3])
```

```python
import math

import jax
import jax.numpy as jnp
from jax import lax
from jax.experimental import pallas as pl
from jax.experimental.pallas import tpu as pltpu

F32 = jnp.float32
BF16 = jnp.bfloat16

N_DEV = 8
NORM_EPS = 1e-6
V7X_VMEM_LIMIT_BYTES = 48 * 1024 * 1024
LANES = 128
SUBLANES = 8

POOL_WINDOWS = (2, 4, 8, 16)
SSD_CHUNK = 256
SSD_HEAD_DIM = 64
SSD_STATE = 128
SSD_HEADS_PER_GROUP = 4
SSD_CONV = 4
SB_HEAD_DIM = 64
SB_BLOCK = 256
SB_QUERY_BLOCK = 256

ADAM_LR = 0.001
ADAM_B1 = 0.9
ADAM_B2 = 0.999
ADAM_EPS = 1e-08
ADAM_WD = 0.01
ADAM_STEP = 10


def _params(*sem):
    return pltpu.CompilerParams(dimension_semantics=sem, vmem_limit_bytes=V7X_VMEM_LIMIT_BYTES)


def _tile(n, cap, mult):
    best = None
    for t in range(mult, min(n, cap) + 1, mult):
        if n % t == 0:
            best = t
    return best or n


def _load_slabs(ref, slabs):
    if not slabs:
        return ref[...]
    return jnp.concatenate([ref[p] for p in range(ref.shape[0])], axis=1)


def _matmul(a, b, mode, *, name, out_dtype=F32, resid=None, a_slabs=False, out_slabs=False,
            tm_cap=1024, tn_cap=1024, tk_cap=2048):
    pairs = list(zip(a, b)) if isinstance(a, (list, tuple)) else [(a, b)]
    a, b = pairs[0]
    if a_slabs:
        m, k = a.shape[1], a.shape[0] * LANES
    else:
        m, k = a.shape
    n = b.shape[1] if mode == "nn" else b.shape[0]
    assert (b.shape[0] if mode == "nn" else b.shape[1]) == k
    assert all(pa.shape == a.shape and pb.shape == b.shape for pa, pb in pairs)
    tm, tn, tk = _tile(m, tm_cap, SUBLANES), _tile(n, tn_cap, LANES), _tile(k, tk_cap, LANES)
    nk = k // tk
    dn = (((1,), (0,)), ((), ())) if mode == "nn" else (((1,), (1,)), ((), ()))
    has_resid = resid is not None
    n_pairs = len(pairs)

    def body(*refs):
        ab_refs, rest = refs[:2 * n_pairs], refs[2 * n_pairs:]
        r_ref = rest[0] if has_resid else None
        o_ref = rest[1] if has_resid else rest[0]
        kk = pl.program_id(2)

        def partial():
            total = None
            for p in range(n_pairs):
                d = lax.dot_general(_load_slabs(ab_refs[2 * p], a_slabs).astype(BF16),
                                    ab_refs[2 * p + 1][...].astype(BF16), dn, preferred_element_type=F32)
                total = d if total is None else total + d
            return total

        def finish(r):
            if has_resid:
                r = r + r_ref[...]
            if out_slabs:
                for p in range(tn // LANES):
                    o_ref[p] = r[:, p * LANES:(p + 1) * LANES].astype(out_dtype)
            else:
                o_ref[...] = r.astype(out_dtype)

        if nk == 1:
            finish(partial())
        else:
            acc = rest[-1]

            @pl.when(kk == 0)
            def _():
                acc[...] = jnp.zeros_like(acc)

            acc[...] += partial()

            @pl.when(kk == nk - 1)
            def _():
                finish(acc[...])

    b_spec = (pl.BlockSpec((tk, tn), lambda i, j, kk: (kk, j)) if mode == "nn"
              else pl.BlockSpec((tn, tk), lambda i, j, kk: (j, kk)))
    a_spec = (pl.BlockSpec((tk // LANES, tm, LANES), lambda i, j, kk: (kk, i, 0)) if a_slabs
              else pl.BlockSpec((tm, tk), lambda i, j, kk: (i, kk)))
    in_specs = [a_spec, b_spec] * n_pairs
    args = [t for pair in pairs for t in pair]
    if has_resid:
        in_specs.append(pl.BlockSpec((tm, tn), lambda i, j, kk: (i, j)))
        args.append(resid)
    if out_slabs:
        out_spec = pl.BlockSpec((tn // LANES, tm, LANES), lambda i, j, kk: (j, i, 0))
        out_shape = jax.ShapeDtypeStruct((n // LANES, m, LANES), out_dtype)
    else:
        out_spec = pl.BlockSpec((tm, tn), lambda i, j, kk: (i, j))
        out_shape = jax.ShapeDtypeStruct((m, n), out_dtype)
    return pl.pallas_call(
        body, name=name, grid=(m // tm, n // tn, nk),
        in_specs=in_specs, out_specs=out_spec, out_shape=out_shape,
        scratch_shapes=[pltpu.VMEM((tm, tn), F32)] if nk > 1 else [],
        compiler_params=_params("parallel", "parallel", "arbitrary"),
    )(*args)


def _matmul_tn(a, b, *, name, a_slabs=False, ta_cap=1024, tb_cap=1024, tr_cap=512):
    if a_slabs:
        r, ka = a.shape[1], a.shape[0] * LANES
    else:
        r, ka = a.shape
    nb = b.shape[1]
    assert b.shape[0] == r
    ta, tb, tr = _tile(ka, ta_cap, LANES), _tile(nb, tb_cap, LANES), _tile(r, tr_cap, SUBLANES)

    def body(a_ref, b_ref, o_ref):
        @pl.when(pl.program_id(2) == 0)
        def _():
            o_ref[...] = jnp.zeros_like(o_ref)

        o_ref[...] += lax.dot_general(_load_slabs(a_ref, a_slabs).astype(BF16), b_ref[...].astype(BF16),
                                      (((0,), (0,)), ((), ())), preferred_element_type=F32)

    a_spec = (pl.BlockSpec((ta // LANES, tr, LANES), lambda i, j, kk: (i, kk, 0)) if a_slabs
              else pl.BlockSpec((tr, ta), lambda i, j, kk: (kk, i)))
    return pl.pallas_call(
        body, name=name, grid=(ka // ta, nb // tb, r // tr),
        in_specs=[a_spec, pl.BlockSpec((tr, tb), lambda i, j, kk: (kk, j))],
        out_specs=pl.BlockSpec((ta, tb), lambda i, j, kk: (i, j)),
        out_shape=jax.ShapeDtypeStruct((ka, nb), F32),
        compiler_params=_params("parallel", "parallel", "arbitrary"),
    )(a, b)


def _core_major(k):
    return (k % 2) * (N_DEV // 2) + k // 2


def _matmul_tn_into(buf, a, b, row_off, *, name, a_slabs=False, tr_cap=1024):
    if a_slabs:
        r, ka = a.shape[1], a.shape[0] * LANES
    else:
        r, ka = a.shape
    n_dev, _, n = buf.shape
    per = ka // n_dev
    assert b.shape == (r, n) and ka % n_dev == 0 and per % SUBLANES == 0 and row_off % per == 0
    tr = _tile(r, tr_cap, SUBLANES)
    fresh = isinstance(buf, jax.ShapeDtypeStruct)

    def body(*refs):
        a_ref, b_ref, o_ref = refs[-3:]
        prod = lax.dot_general(_load_slabs(a_ref, a_slabs).astype(BF16), b_ref[...].astype(BF16),
                               (((0,), (0,)), ((), ())), preferred_element_type=F32)
        @pl.when(pl.program_id(0) == 0)
        def _():
            for k in range(n_dev):
                o_ref[_core_major(k)] = prod[k * per:(k + 1) * per]

        @pl.when(pl.program_id(0) > 0)
        def _():
            for k in range(n_dev):
                o_ref[_core_major(k)] += prod[k * per:(k + 1) * per]

    a_spec = (pl.BlockSpec((ka // LANES, tr, LANES), lambda i: (0, i, 0)) if a_slabs
              else pl.BlockSpec((tr, ka), lambda i: (i, 0)))
    return pl.pallas_call(
        body, name=name, grid=(r // tr,),
        in_specs=([] if fresh else [pl.BlockSpec(memory_space=pl.ANY)]) + [a_spec, pl.BlockSpec((tr, n), lambda i: (i, 0))],
        out_specs=pl.BlockSpec((n_dev, per, n), lambda i: (0, row_off // per, 0)),
        out_shape=jax.ShapeDtypeStruct(buf.shape, F32),
        input_output_aliases={} if fresh else {0: 0},
        compiler_params=_params("arbitrary"),
    )(*(() if fresh else (buf,)), a, b)


def _rms_fwd(x, gain, *, name):
    t, d = x.shape
    tm = _tile(t, 512, SUBLANES)

    def body(x_ref, g_ref, o_ref):
        xv = x_ref[...]
        r = lax.rsqrt(jnp.mean(xv * xv, axis=-1, keepdims=True) + NORM_EPS)
        o_ref[...] = (xv * r * g_ref[...]).astype(BF16)

    return pl.pallas_call(
        body, name=name, grid=(t // tm,),
        in_specs=[pl.BlockSpec((tm, d), lambda i: (i, 0)), pl.BlockSpec((1, d), lambda i: (0, 0))],
        out_specs=pl.BlockSpec((tm, d), lambda i: (i, 0)),
        out_shape=jax.ShapeDtypeStruct((t, d), BF16),
        compiler_params=_params("parallel"),
    )(x, gain)


def _rms_bwd(x, gain, dh, dres, *, name):
    t, d = x.shape
    tm = _tile(t, 512, SUBLANES)

    def body(x_ref, g_ref, dh_ref, dres_ref, dx_ref, dg_ref):
        @pl.when(pl.program_id(0) == 0)
        def _():
            dg_ref[...] = jnp.zeros_like(dg_ref)

        xv = x_ref[...]
        r = lax.rsqrt(jnp.mean(xv * xv, axis=-1, keepdims=True) + NORM_EPS)
        xhat = xv * r
        dhv = dh_ref[...]
        u = dhv * g_ref[...]
        dx_ref[...] = dres_ref[...] + r * (u - xhat * jnp.mean(u * xhat, axis=-1, keepdims=True))
        dg_ref[...] += jnp.sum(dhv * xhat, axis=0, keepdims=True)

    return pl.pallas_call(
        body, name=name, grid=(t // tm,),
        in_specs=[pl.BlockSpec((tm, d), lambda i: (i, 0)), pl.BlockSpec((1, d), lambda i: (0, 0)),
                  pl.BlockSpec((tm, d), lambda i: (i, 0)), pl.BlockSpec((tm, d), lambda i: (i, 0))],
        out_specs=[pl.BlockSpec((tm, d), lambda i: (i, 0)), pl.BlockSpec((1, d), lambda i: (0, 0))],
        out_shape=[jax.ShapeDtypeStruct((t, d), F32), jax.ShapeDtypeStruct((1, d), F32)],
        compiler_params=_params("arbitrary"),
    )(x, gain, dh, dres)


def _loss_head(y, target, *, name):
    t, d = y.shape
    tm = _tile(t, 512, SUBLANES)

    def body(y_ref, t_ref, dy_ref, l_ref):
        @pl.when(pl.program_id(0) == 0)
        def _():
            l_ref[...] = jnp.zeros_like(l_ref)

        e = y_ref[...] - t_ref[...]
        dy_ref[...] = e * (1.0 / d)
        l_ref[...] += jnp.sum(e * e, axis=0, keepdims=True) * (0.5 / d)

    return pl.pallas_call(
        body, name=name, grid=(t // tm,),
        in_specs=[pl.BlockSpec((tm, d), lambda i: (i, 0)), pl.BlockSpec((tm, d), lambda i: (i, 0))],
        out_specs=[pl.BlockSpec((tm, d), lambda i: (i, 0)), pl.BlockSpec((1, d), lambda i: (0, 0))],
        out_shape=[jax.ShapeDtypeStruct((t, d), F32), jax.ShapeDtypeStruct((1, d), F32)],
        compiler_params=_params("arbitrary"),
    )(y, target)


def _sigmoid(v):
    return 0.5 * jnp.tanh(0.5 * v) + 0.5


FFN_TOKEN_TILE = 512
FFN_HIDDEN_TILE = 1408
NT_DIMS = (((1,), (1,)), ((), ()))


def _ffn_up(h, w_gate_t, w_up_t, *, name):
    t, d = h.shape
    f = w_gate_t.shape[0]
    tm, tn = _tile(t, FFN_TOKEN_TILE, SUBLANES), _tile(f, FFN_HIDDEN_TILE, LANES)

    def body(h_ref, g_ref, u_ref, s_ref, a_ref, b_ref):
        hv = h_ref[...].astype(BF16)
        av = lax.dot_general(hv, g_ref[...].astype(BF16), NT_DIMS, preferred_element_type=F32)
        bv = lax.dot_general(hv, u_ref[...].astype(BF16), NT_DIMS, preferred_element_type=F32)
        s_ref[...] = (av * _sigmoid(av) * bv).astype(BF16)
        a_ref[...] = av.astype(BF16)
        b_ref[...] = bv.astype(BF16)

    w_spec = pl.BlockSpec((tn, d), lambda j, i: (j, 0))
    out_spec = pl.BlockSpec((tm, tn), lambda j, i: (i, j))
    out = jax.ShapeDtypeStruct((t, f), BF16)
    return pl.pallas_call(
        body, name=name, grid=(f // tn, t // tm),
        in_specs=[pl.BlockSpec((tm, d), lambda j, i: (i, 0)), w_spec, w_spec],
        out_specs=[out_spec, out_spec, out_spec], out_shape=[out, out, out],
        compiler_params=_params("parallel", "parallel"),
    )(h, w_gate_t, w_up_t)


def _ffn_dact(dx, w_down, a, b, *, name):
    t, d = dx.shape
    f = w_down.shape[0]
    tm, tn = _tile(t, FFN_TOKEN_TILE, SUBLANES), _tile(f, FFN_HIDDEN_TILE, LANES)

    def body(dx_ref, w_ref, a_ref, b_ref, da_ref, db_ref):
        ds = lax.dot_general(dx_ref[...].astype(BF16), w_ref[...].astype(BF16), NT_DIMS, preferred_element_type=F32)
        av = a_ref[...].astype(F32)
        sg = _sigmoid(av)
        da_ref[...] = (ds * b_ref[...].astype(F32) * (sg * (1.0 + av * (1.0 - sg)))).astype(BF16)
        db_ref[...] = (ds * av * sg).astype(BF16)

    blk = pl.BlockSpec((tm, tn), lambda j, i: (i, j))
    out = jax.ShapeDtypeStruct((t, f), BF16)
    return pl.pallas_call(
        body, name=name, grid=(f // tn, t // tm),
        in_specs=[pl.BlockSpec((tm, d), lambda j, i: (i, 0)), pl.BlockSpec((tn, d), lambda j, i: (j, 0)), blk, blk],
        out_specs=[blk, blk], out_shape=[out, out],
        compiler_params=_params("parallel", "parallel"),
    )(dx, w_down, a, b)


def _ffn_fwd(x, gain, w_gate_t, w_up_t, w_down, tag):
    h = _rms_fwd(x, gain, name=f"ffn_norm_{tag}")
    s, a, b = _ffn_up(h, w_gate_t, w_up_t, name=f"ffn_up_{tag}")
    x_new = _matmul(s, w_down, "nn", resid=x, tn_cap=1024, tk_cap=2816, name=f"ffn_down_{tag}")
    return x_new, (x, h, a, b, s)


def _ffn_bwd(dx, saved, gain, w_gate_t, w_up_t, w_down, grads, rows, tag):
    x, h, a, b, s = saved
    da, db = _ffn_dact(dx, w_down, a, b, name=f"ffn_dact_{tag}")
    grads = _matmul_tn_into(grads, da, h, rows[0], name=f"ffn_dwgate_{tag}")
    grads = _matmul_tn_into(grads, db, h, rows[1], name=f"ffn_dwup_{tag}")
    grads = _matmul_tn_into(grads, s, dx, rows[2], name=f"ffn_dwdown_{tag}")
    dh = _matmul([da, db], [w_gate_t, w_up_t], "nn", tm_cap=512, tn_cap=1024, tk_cap=2816, name=f"ffn_dh_{tag}")
    dx_in, dgain = _rms_bwd(x, gain, dh, dx, name=f"ffn_dnorm_{tag}")
    return dx_in, grads, dgain


POOL_HALO = 16


def _shift_rows(v, k):
    n = v.shape[0]
    return pltpu.roll(v, k % n, 0)


def _window_sum(v, w, direction):
    k = 1
    while k < w:
        v = v + _shift_rows(v, direction * k)
        k *= 2
    return v


def _pool_fwd(u, x, w_group, scale, *, name):
    t, d = u.shape
    ng, dg = w_group.shape[0], w_group.shape[1]
    tm = _tile(t, 512, POOL_HALO)
    hb = tm // POOL_HALO

    def body(u_ref, halo_ref, x_ref, w_ref, s_ref, xo_ref, p_ref, y_ref):
        i, g = pl.program_id(0), pl.program_id(1)
        halo = jnp.where(i > 0, halo_ref[...], 0.0)
        ext = jnp.concatenate([halo, u_ref[...]], axis=0)
        pos = i * tm + lax.broadcasted_iota(jnp.int32, (tm, 1), 0)
        for gi, win in enumerate(POOL_WINDOWS):
            @pl.when(g == gi)
            def _(win=win):
                tot = _window_sum(ext, win, 1)[POOL_HALO:]
                cnt = jnp.minimum(pos + 1, win).astype(F32)
                p = (tot / cnt - u_ref[...]).astype(BF16)
                p_ref[...] = p
                y = jnp.dot(p, w_ref[...].astype(BF16), preferred_element_type=F32)
                y_ref[...] = y
                xo_ref[...] = x_ref[...] + y * s_ref[...]

    blk = pl.BlockSpec((tm, dg), lambda i, g: (i, g))
    return pl.pallas_call(
        body, name=name, grid=(t // tm, ng),
        in_specs=[blk, pl.BlockSpec((POOL_HALO, dg), lambda i, g: (jnp.maximum(i * hb - 1, 0), g)), blk,
                  pl.BlockSpec((None, dg, dg), lambda i, g: (g, 0, 0)), pl.BlockSpec((1, dg), lambda i, g: (0, g))],
        out_specs=[blk, blk, blk],
        out_shape=[jax.ShapeDtypeStruct((t, d), F32), jax.ShapeDtypeStruct((t, d), BF16),
                   jax.ShapeDtypeStruct((t, d), F32)],
        compiler_params=_params("parallel", "parallel"),
    )(u, u, x, w_group, scale)


def _pool_bwd(dx, p, y_pre, w_group, scale, *, name):
    t, d = dx.shape
    ng, dg = w_group.shape[0], w_group.shape[1]
    tm = _tile(t, 512, POOL_HALO)
    hb = tm // POOL_HALO
    nt = t // tm

    def body(dx_ref, nxt_ref, p_ref, y_ref, w_ref, s_ref, du_ref, dw_ref, ds_ref):
        g, i = pl.program_id(0), pl.program_id(1)

        @pl.when(i == 0)
        def _():
            dw_ref[...] = jnp.zeros_like(dw_ref)
            ds_ref[...] = jnp.zeros_like(ds_ref)

        dxv = dx_ref[...]
        ds_ref[...] += jnp.sum(dxv * y_ref[...], axis=0, keepdims=True)
        nxt = jnp.where(i < nt - 1, nxt_ref[...], 0.0)
        dyp = (jnp.concatenate([dxv, nxt], axis=0) * s_ref[...]).astype(BF16)
        dw_ref[...] += lax.dot_general(p_ref[...], dyp[:tm], (((0,), (0,)), ((), ())), preferred_element_type=F32)
        dp = lax.dot_general(dyp, w_ref[...].astype(BF16), (((1,), (1,)), ((), ())), preferred_element_type=F32)
        pos = i * tm + lax.broadcasted_iota(jnp.int32, (tm + POOL_HALO, 1), 0)
        for gi, win in enumerate(POOL_WINDOWS):
            @pl.when(g == gi)
            def _(win=win):
                q = dp / jnp.minimum(pos + 1, win).astype(F32)
                du_ref[...] = (_window_sum(q, win, -1)[:tm] - dp[:tm]).astype(BF16)

    blk = pl.BlockSpec((tm, dg), lambda g, i: (i, g))
    return pl.pallas_call(
        body, name=name, grid=(ng, nt),
        in_specs=[blk, pl.BlockSpec((POOL_HALO, dg), lambda g, i: (jnp.minimum((i + 1) * hb, t // POOL_HALO - 1), g)),
                  blk, blk, pl.BlockSpec((None, dg, dg), lambda g, i: (g, 0, 0)),
                  pl.BlockSpec((1, dg), lambda g, i: (0, g))],
        out_specs=[blk, pl.BlockSpec((None, dg, dg), lambda g, i: (g, 0, 0)), pl.BlockSpec((1, dg), lambda g, i: (0, g))],
        out_shape=[jax.ShapeDtypeStruct((t, d), BF16), jax.ShapeDtypeStruct((ng, dg, dg), F32),
                   jax.ShapeDtypeStruct((1, d), F32)],
        compiler_params=_params("parallel", "arbitrary"),
    )(dx, dx, p, y_pre, w_group, scale)


def _pool_mixer_fwd(x, gain, w_in, w_group, scale, tag):
    h = _rms_fwd(x, gain, name=f"pool_norm_{tag}")
    u = _matmul(h, w_in, "nn", name=f"pool_in_{tag}")
    x_new, p, y_pre = _pool_fwd(u, x, w_group, scale, name=f"pool_mix_{tag}")
    return x_new, (x, h, p, y_pre)


def _pool_mixer_bwd(dx, saved, gain, w_in, w_group, scale, grads, row_in, tag):
    x, h, p, y_pre = saved
    du, dw_group, dscale = _pool_bwd(dx, p, y_pre, w_group, scale, name=f"pool_dmix_{tag}")
    grads = _matmul_tn_into(grads, h, du, row_in, name=f"pool_dwin_{tag}")
    dh = _matmul(du, w_in, "nt", name=f"pool_dh_{tag}")
    dx_in, dgain = _rms_bwd(x, gain, dh, dx, name=f"pool_dnorm_{tag}")
    return dx_in, grads, dw_group, dscale, dgain


CONV_HALO = 8
NEG_BIG = -1e30


def _softplus(v):
    return jnp.maximum(v, 0.0) + jnp.log(1.0 + jnp.exp(-jnp.abs(v)))


def _conv_taps(ext, w_ref, off, rows):
    acc = None
    for k in range(SSD_CONV):
        shift = SSD_CONV - 1 - k
        v = (_shift_rows(ext, shift) if shift else ext)[off:off + rows] * w_ref[k:k + 1, :]
        acc = v if acc is None else acc + v
    return acc


def _ssd_conv_fwd(zx, conv_w, conv_b, col0, *, name):
    t = zx.shape[0]
    c = conv_w.shape[1]
    tm, tc = _tile(t, 512, CONV_HALO), _tile(c, 512, LANES)
    hb, cb0 = tm // CONV_HALO, col0 // tc
    assert col0 % tc == 0

    def body(x_ref, halo_ref, w_ref, b_ref, o_ref):
        halo = jnp.where(pl.program_id(0) > 0, halo_ref[...], 0.0)
        ext = jnp.concatenate([halo, x_ref[...]], axis=0)
        pre = _conv_taps(ext, w_ref, CONV_HALO, tm) + b_ref[...]
        o_ref[...] = pre * _sigmoid(pre)

    return pl.pallas_call(
        body, name=name, grid=(t // tm, c // tc),
        in_specs=[pl.BlockSpec((tm, tc), lambda i, j: (i, j + cb0)),
                  pl.BlockSpec((CONV_HALO, tc), lambda i, j: (jnp.maximum(i * hb - 1, 0), j + cb0)),
                  pl.BlockSpec((SSD_CONV, tc), lambda i, j: (0, j)), pl.BlockSpec((1, tc), lambda i, j: (0, j))],
        out_specs=pl.BlockSpec((tm, tc), lambda i, j: (i, j)),
        out_shape=jax.ShapeDtypeStruct((t, c), F32),
        compiler_params=_params("parallel", "parallel"),
    )(zx, zx, conv_w, conv_b)


def _ssd_conv_bwd(d_parts, zx, conv_w, conv_b, col0, *, name):
    t = zx.shape[0]
    c = conv_w.shape[1]
    tm, tc = _tile(t, 512, CONV_HALO), _tile(c, 512, LANES)
    hb, cb0, nt = tm // CONV_HALO, col0 // tc, t // tm
    last_halo = t // CONV_HALO - 1
    starts = [0]
    for part in d_parts:
        assert part.shape[1] % tc == 0
        starts.append(starts[-1] + part.shape[1] // tc)
    assert starts[-1] == c // tc
    n_parts = len(d_parts)

    def pick(refs, j):
        value = refs[-1][...]
        for p in reversed(range(n_parts - 1)):
            value = jnp.where(j < starts[p + 1], refs[p][...], value)
        return value

    def body(x_ref, prev_ref, nxt_ref, *rest):
        d_refs, dnxt_refs = rest[:n_parts], rest[n_parts:2 * n_parts]
        w_ref, b_ref, dx_ref, dw_ref, db_ref = rest[2 * n_parts:]
        j, i = pl.program_id(0), pl.program_id(1)

        @pl.when(i == 0)
        def _():
            dw_ref[...] = jnp.zeros_like(dw_ref)
            db_ref[...] = jnp.zeros_like(db_ref)

        prev = jnp.where(i > 0, prev_ref[...], 0.0)
        has_next = i < nt - 1
        ext = jnp.concatenate([prev, x_ref[...], jnp.where(has_next, nxt_ref[...], 0.0)], axis=0)
        pre = _conv_taps(ext, w_ref, CONV_HALO, tm + CONV_HALO) + b_ref[...]
        sg = _sigmoid(pre)
        dact = jnp.concatenate([pick(d_refs, j), jnp.where(has_next, pick(dnxt_refs, j), 0.0)], axis=0)
        dpre = dact * (sg * (1.0 + pre * (1.0 - sg)))
        db_ref[...] += jnp.sum(dpre[:tm], axis=0, keepdims=True)
        acc = None
        for k in range(SSD_CONV):
            shift = SSD_CONV - 1 - k
            src = (_shift_rows(ext, shift) if shift else ext)[CONV_HALO:CONV_HALO + tm]
            dw_ref[k:k + 1, :] += jnp.sum(dpre[:tm] * src, axis=0, keepdims=True)
            v = (_shift_rows(dpre, -shift) if shift else dpre)[:tm] * w_ref[k:k + 1, :]
            acc = v if acc is None else acc + v
        dx_ref[...] = acc.astype(BF16)

    def part_specs(rows, row_index):
        def spec(p):
            def index(j, i):
                mine = (j >= starts[p]) & (j < starts[p + 1])
                return jnp.where(mine, row_index(i), 0), jnp.where(mine, j - starts[p], 0)
            return pl.BlockSpec((rows, tc), index)
        return [spec(p) for p in range(n_parts)]

    main = lambda j, i: (i, j + cb0)
    next_halo = lambda i: jnp.minimum((i + 1) * hb, last_halo)
    return pl.pallas_call(
        body, name=name, grid=(c // tc, nt),
        in_specs=[pl.BlockSpec((tm, tc), main),
                  pl.BlockSpec((CONV_HALO, tc), lambda j, i: (jnp.maximum(i * hb - 1, 0), j + cb0)),
                  pl.BlockSpec((CONV_HALO, tc), lambda j, i: (next_halo(i), j + cb0)),
                  *part_specs(tm, lambda i: i), *part_specs(CONV_HALO, next_halo),
                  pl.BlockSpec((SSD_CONV, tc), lambda j, i: (0, j)), pl.BlockSpec((1, tc), lambda j, i: (0, j))],
        out_specs=[pl.BlockSpec((tm, tc), lambda j, i: (i, j)), pl.BlockSpec((SSD_CONV, tc), lambda j, i: (0, j)),
                   pl.BlockSpec((1, tc), lambda j, i: (0, j))],
        out_shape=[jax.ShapeDtypeStruct((t, c), BF16), jax.ShapeDtypeStruct((SSD_CONV, c), F32),
                   jax.ShapeDtypeStruct((1, c), F32)],
        compiler_params=_params("parallel", "arbitrary"),
    )(zx, zx, zx, *d_parts, *d_parts, conv_w, conv_b)


SSD_CUMSUM_PIECES = 2
SSD_GROUPS_PER_STEP = 1


def _ssd_group_pad(v, n_groups):
    lead = v.shape[:-1]
    v = v.reshape(*lead, n_groups, SSD_HEADS_PER_GROUP)
    v = jnp.pad(v, [(0, 0)] * (len(lead) + 1) + [(0, LANES - SSD_HEADS_PER_GROUP)])
    return v.reshape(*lead, n_groups * LANES)


def _ssd_group_unpad(v, n_groups):
    lead = v.shape[:-1]
    return v.reshape(*lead, n_groups, LANES)[..., :SSD_HEADS_PER_GROUP].reshape(*lead, -1)


def _ssd_chunk_common(dtp_ref, par_ref):
    ell = SSD_CHUNK
    dt = _softplus(dtp_ref[...] + par_ref[0:1, :])
    a = -jnp.exp(par_ref[1:2, :])
    row = lax.broadcasted_iota(jnp.int32, (ell, ell), 0)
    col = lax.broadcasted_iota(jnp.int32, (ell, ell), 1)
    acum = _split_dot(dt * a, (row >= col).astype(BF16), SSD_CUMSUM_PIECES, left=True)
    return dt, a, acum, acum.T, row, col


def _ssd_scan_fwd(xa, dtp, par, n_groups, *, name):
    t = xa.shape[0]
    ell, hd, hpg, ns, gps = SSD_CHUNK, SSD_HEAD_DIM, SSD_HEADS_PER_GROUP, SSD_STATE, SSD_GROUPS_PER_STEP
    gw = hpg * hd
    nc = t // ell
    b_blk0, c_blk0 = n_groups * gw // (ns * gps), (n_groups * gw // ns + n_groups) // gps

    def body(xs_ref, b_ref, c_ref, dtp_ref, par_ref, y_ref, sin_ref, st):
        @pl.when(pl.program_id(1) == 0)
        def _():
            st[...] = jnp.zeros_like(st)

        dt, _, acum, acum_t, row, col = _ssd_chunk_common(dtp_ref, par_ref)
        for gi in range(gps):
            bb = b_ref[:, gi * ns:(gi + 1) * ns].astype(BF16)
            cc = c_ref[:, gi * ns:(gi + 1) * ns].astype(BF16)
            cb = lax.dot_general(cc, bb, NT_DIMS, preferred_element_type=F32)
            s_all = st[gi]
            sin_ref[gi] = s_all
            c_s = lax.dot_general(cc, s_all.astype(BF16), NT_DIMS, preferred_element_type=F32)
            weighted, keep = [], []
            for hh in range(hpg):
                lanes = slice(gi * gw + hh * hd, gi * gw + (hh + 1) * hd)
                hl = gi * LANES + hh
                col_a, row_a = acum[:, hl:hl + 1], acum_t[hl:hl + 1, :]
                decay = jnp.exp(jnp.where(row >= col, col_a - row_a, NEG_BIG))
                xdt = xs_ref[:, lanes] * dt[:, hl:hl + 1]
                y = jnp.dot((cb * decay).astype(BF16), xdt.astype(BF16), preferred_element_type=F32)
                y_ref[:, lanes] = y + jnp.exp(col_a) * c_s[:, hh * hd:(hh + 1) * hd]
                a_last = acum[ell - 1:ell, hl:hl + 1]
                weighted.append((xdt * jnp.exp(a_last - col_a)).astype(BF16))
                keep.append(jnp.broadcast_to(jnp.exp(a_last), (hd, 1)))
            st[gi] = jnp.concatenate(keep, axis=0) * s_all + lax.dot_general(
                jnp.concatenate(weighted, axis=1), bb, (((0,), (0,)), ((), ())), preferred_element_type=F32)

    return pl.pallas_call(
        body, name=name, grid=(n_groups // gps, nc),
        in_specs=[pl.BlockSpec((ell, gps * gw), lambda g, c: (c, g)),
                  pl.BlockSpec((ell, gps * ns), lambda g, c: (c, b_blk0 + g)),
                  pl.BlockSpec((ell, gps * ns), lambda g, c: (c, c_blk0 + g)),
                  pl.BlockSpec((ell, gps * LANES), lambda g, c: (c, g)),
                  pl.BlockSpec((SUBLANES, gps * LANES), lambda g, c: (0, g))],
        out_specs=[pl.BlockSpec((ell, gps * gw), lambda g, c: (c, g)),
                   pl.BlockSpec((None, gps, gw, ns), lambda g, c: (c, g, 0, 0))],
        out_shape=[jax.ShapeDtypeStruct((t, n_groups * gw), F32),
                   jax.ShapeDtypeStruct((nc, n_groups, gw, ns), F32)],
        scratch_shapes=[pltpu.VMEM((gps, gw, ns), F32)],
        compiler_params=_params("parallel", "arbitrary"),
    )(xa, xa, xa, dtp, par)


def _ssd_scan_bwd(dy, xa, dtp, par, s_in, n_groups, *, name):
    t = xa.shape[0]
    ell, hd, hpg, ns, gps = SSD_CHUNK, SSD_HEAD_DIM, SSD_HEADS_PER_GROUP, SSD_STATE, SSD_GROUPS_PER_STEP
    gw = hpg * hd
    nc = t // ell
    b_blk0, c_blk0 = n_groups * gw // (ns * gps), (n_groups * gw // ns + n_groups) // gps
    nt_dims = (((1,), (1,)), ((), ()))
    tn_dims = (((0,), (0,)), ((), ()))

    def body(dy_ref, xs_ref, b_ref, c_ref, dtp_ref, par_ref, sin_ref,
             dxs_ref, db_ref, dc_ref, ddtp_ref, dpar_ref, dst):
        @pl.when(pl.program_id(1) == 0)
        def _():
            dst[...] = jnp.zeros_like(dst)
            dpar_ref[...] = jnp.zeros_like(dpar_ref)

        dtg, a_g, acum, acum_t, row, col = _ssd_chunk_common(dtp_ref, par_ref)
        lane = lax.broadcasted_iota(jnp.int32, (1, gps * LANES), 1)
        dacum = jnp.zeros((ell, gps * LANES), F32)
        xsum = jnp.zeros((ell, gps * LANES), F32)
        dsum = jnp.zeros((1, gps * LANES), F32)
        for gi, hh in [(gi, hh) for gi in range(gps) for hh in range(hpg)]:
            if hh == 0:
                bb = b_ref[:, gi * ns:(gi + 1) * ns].astype(BF16)
                cc = c_ref[:, gi * ns:(gi + 1) * ns].astype(BF16)
                cb = lax.dot_general(cc, bb, nt_dims, preferred_element_type=F32)
                cb_t = lax.dot_general(bb, cc, nt_dims, preferred_element_type=F32)
                dcb = jnp.zeros((ell, ell), F32)
                dcb_t = jnp.zeros((ell, ell), F32)
                s_all, ds_all = sin_ref[gi], dst[gi]
                c_s_all = lax.dot_general(cc, s_all.astype(BF16), nt_dims, preferred_element_type=F32)
                b_ds_all = lax.dot_general(bb, ds_all.astype(BF16), nt_dims, preferred_element_type=F32)
                s_ds = jnp.sum(s_all * ds_all, axis=1, keepdims=True)
                dy_decayed, x_weighted, keep = [], [], []
            lanes = slice(gi * gw + hh * hd, gi * gw + (hh + 1) * hd)
            head = slice(hh * hd, (hh + 1) * hd)
            hl = gi * LANES + hh
            onehot = (lane == hl).astype(F32)
            col_a, row_a = acum[:, hl:hl + 1], acum_t[hl:hl + 1, :]
            decay = jnp.exp(jnp.where(row >= col, col_a - row_a, NEG_BIG))
            decay_t = jnp.exp(jnp.where(col >= row, row_a - col_a, NEG_BIG))
            e_col = jnp.exp(col_a)
            a_last = acum[ell - 1:ell, hl:hl + 1]
            w = jnp.exp(a_last - col_a)
            e_last = jnp.exp(a_last)
            xs_h, dy_h = xs_ref[:, lanes], dy_ref[:, lanes]
            dt_h = dtg[:, hl:hl + 1]
            xdt = xs_h * dt_h
            xdt_b, dy_b = xdt.astype(BF16), dy_h.astype(BF16)
            dm_decay = lax.dot_general(dy_b, xdt_b, nt_dims, preferred_element_type=F32) * decay
            dm_decay_t = lax.dot_general(xdt_b, dy_b, nt_dims, preferred_element_type=F32) * decay_t
            dcb += dm_decay
            dcb_t += dm_decay_t
            m_t = cb_t * decay_t
            dac = jnp.sum(dm_decay * cb, axis=1, keepdims=True) - jnp.sum(dm_decay_t * cb_t, axis=1, keepdims=True)
            b_ds = b_ds_all[:, head]
            dxdt = jnp.dot(m_t.astype(BF16), dy_b, preferred_element_type=F32) + w * b_ds
            dac += jnp.sum(dy_h * c_s_all[:, head], axis=1, keepdims=True) * e_col
            q = jnp.sum(xdt * b_ds, axis=1, keepdims=True) * w
            dac -= q
            d_last = jnp.sum(q, axis=0, keepdims=True) + e_last * jnp.sum(s_ds[head], axis=0, keepdims=True)
            is_last = lax.broadcasted_iota(jnp.int32, (ell, 1), 0) == ell - 1
            dac += jnp.where(is_last, d_last, 0.0)
            dacum += dac * onehot
            dy_decayed.append((dy_h * e_col).astype(BF16))
            x_weighted.append((xdt * w).astype(BF16))
            keep.append(jnp.broadcast_to(e_last, (hd, 1)))
            dxs_ref[:, lanes] = dxdt * dt_h + dy_h * par_ref[2:3, hl:hl + 1]
            xsum += jnp.sum(dxdt * xs_h, axis=1, keepdims=True) * onehot
            dsum += jnp.sum(jnp.sum(dy_h * xs_h, axis=1, keepdims=True), axis=0, keepdims=True) * onehot
            if hh == hpg - 1:
                group = slice(gi * ns, (gi + 1) * ns)
                dy_all, x_all = jnp.concatenate(dy_decayed, axis=1), jnp.concatenate(x_weighted, axis=1)
                dc_ref[:, group] = (jnp.dot(dy_all, s_all.astype(BF16), preferred_element_type=F32)
                                    + jnp.dot(dcb.astype(BF16), bb, preferred_element_type=F32))
                db_ref[:, group] = (jnp.dot(x_all, ds_all.astype(BF16), preferred_element_type=F32)
                                    + jnp.dot(dcb_t.astype(BF16), cc, preferred_element_type=F32))
                dst[gi] = jnp.concatenate(keep, axis=0) * ds_all + lax.dot_general(
                    dy_all, cc, tn_dims, preferred_element_type=F32)
        dda = _split_dot(dacum, (col >= row).astype(BF16), SSD_CUMSUM_PIECES, left=True)
        ddtp = (xsum + dda * a_g) * _sigmoid(dtp_ref[...] + par_ref[0:1, :])
        ddtp_ref[...] = ddtp
        dpar_ref[0:1, :] += jnp.sum(ddtp, axis=0, keepdims=True)
        dpar_ref[1:2, :] += jnp.sum(dda * dtg, axis=0, keepdims=True) * a_g
        dpar_ref[2:3, :] += dsum

    rev = lambda i: nc - 1 - i
    return pl.pallas_call(
        body, name=name, grid=(n_groups // gps, nc),
        in_specs=[pl.BlockSpec((ell, gps * gw), lambda g, i: (rev(i), g)),
                  pl.BlockSpec((ell, gps * gw), lambda g, i: (rev(i), g)),
                  pl.BlockSpec((ell, gps * ns), lambda g, i: (rev(i), b_blk0 + g)),
                  pl.BlockSpec((ell, gps * ns), lambda g, i: (rev(i), c_blk0 + g)),
                  pl.BlockSpec((ell, gps * LANES), lambda g, i: (rev(i), g)),
                  pl.BlockSpec((SUBLANES, gps * LANES), lambda g, i: (0, g)),
                  pl.BlockSpec((None, gps, gw, ns), lambda g, i: (rev(i), g, 0, 0))],
        out_specs=[pl.BlockSpec((ell, gps * gw), lambda g, i: (rev(i), g)),
                   pl.BlockSpec((ell, gps * ns), lambda g, i: (rev(i), g)),
                   pl.BlockSpec((ell, gps * ns), lambda g, i: (rev(i), g)),
                   pl.BlockSpec((ell, gps * LANES), lambda g, i: (rev(i), g)),
                   pl.BlockSpec((SUBLANES, gps * LANES), lambda g, i: (0, g))],
        out_shape=[jax.ShapeDtypeStruct((t, n_groups * gw), F32), jax.ShapeDtypeStruct((t, n_groups * ns), F32),
                   jax.ShapeDtypeStruct((t, n_groups * ns), F32), jax.ShapeDtypeStruct((t, n_groups * LANES), F32),
                   jax.ShapeDtypeStruct((SUBLANES, n_groups * LANES), F32)],
        scratch_shapes=[pltpu.VMEM((gps, gw, ns), F32)],
        compiler_params=_params("parallel", "arbitrary"),
    )(dy, xa, xa, xa, dtp, par, s_in)


def _ssd_gate_fwd(y, xa, zx, d_rep, out_norm, *, name):
    t, di = y.shape
    gw = SSD_HEADS_PER_GROUP * SSD_HEAD_DIM
    tm = _tile(t, 512, SUBLANES)

    def body(y_ref, xs_ref, z_ref, d_ref, n_ref, o_ref):
        zv = z_ref[...]
        gt = (y_ref[...] + d_ref[...] * xs_ref[...]) * (zv * _sigmoid(zv))
        r = lax.rsqrt(jnp.mean(gt * gt, axis=-1, keepdims=True) + NORM_EPS)
        o_ref[...] = (gt * r * n_ref[...]).astype(BF16)

    blk = pl.BlockSpec((tm, gw), lambda i, g: (i, g))
    vec = pl.BlockSpec((1, gw), lambda i, g: (0, g))
    return pl.pallas_call(
        body, name=name, grid=(t // tm, di // gw),
        in_specs=[blk, blk, blk, vec, vec], out_specs=blk,
        out_shape=jax.ShapeDtypeStruct((t, di), BF16),
        compiler_params=_params("parallel", "parallel"),
    )(y, xa, zx, d_rep, out_norm)


def _ssd_gate_bwd(dgn, y, xa, zx, d_rep, out_norm, *, name):
    t, di = y.shape
    gw = SSD_HEADS_PER_GROUP * SSD_HEAD_DIM
    tm = _tile(t, 512, SUBLANES)

    def body(dg_ref, y_ref, xs_ref, z_ref, d_ref, n_ref, dy_ref, dz_ref, dn_ref):
        @pl.when(pl.program_id(1) == 0)
        def _():
            dn_ref[...] = jnp.zeros_like(dn_ref)

        zv = z_ref[...]
        sg = _sigmoid(zv)
        sz = zv * sg
        y2 = y_ref[...] + d_ref[...] * xs_ref[...]
        gt = y2 * sz
        r = lax.rsqrt(jnp.mean(gt * gt, axis=-1, keepdims=True) + NORM_EPS)
        ghat = gt * r
        dgv = dg_ref[...]
        dn_ref[...] += jnp.sum(dgv * ghat, axis=0, keepdims=True)
        u = dgv * n_ref[...]
        dgt = r * (u - ghat * jnp.mean(u * ghat, axis=-1, keepdims=True))
        dy_ref[...] = dgt * sz
        dz_ref[...] = (dgt * y2 * (sg * (1.0 + zv * (1.0 - sg)))).astype(BF16)

    blk = pl.BlockSpec((tm, gw), lambda g, i: (i, g))
    vec = pl.BlockSpec((1, gw), lambda g, i: (0, g))
    return pl.pallas_call(
        body, name=name, grid=(di // gw, t // tm),
        in_specs=[blk, blk, blk, blk, vec, vec], out_specs=[blk, blk, vec],
        out_shape=[jax.ShapeDtypeStruct((t, di), F32), jax.ShapeDtypeStruct((t, di), BF16),
                   jax.ShapeDtypeStruct((1, di), F32)],
        compiler_params=_params("parallel", "arbitrary"),
    )(dgn, y, xa, zx, d_rep, out_norm)


def _ssd_mixer_fwd(x, gain, w_zx_t, w_dt_t, conv_w, conv_b, par, d_rep, out_norm, w_out, tag):
    di = w_out.shape[0]
    n_groups = di // (SSD_HEADS_PER_GROUP * SSD_HEAD_DIM)
    h = _rms_fwd(x, gain, name=f"ssd_norm_{tag}")
    zx = _matmul(h, w_zx_t, "nt", name=f"ssd_in_{tag}")
    dtp = _matmul(h, w_dt_t, "nt", name=f"ssd_dt_{tag}")
    xa = _ssd_conv_fwd(zx, conv_w, conv_b, di, name=f"ssd_conv_{tag}")
    y, s_in = _ssd_scan_fwd(xa, dtp, par, n_groups, name=f"ssd_scan_{tag}")
    gn = _ssd_gate_fwd(y, xa, zx, d_rep, out_norm, name=f"ssd_gate_{tag}")
    x_new = _matmul(gn, w_out, "nn", resid=x, name=f"ssd_out_{tag}")
    return x_new, (x, h, zx, dtp, xa, y, s_in, gn)


def _ssd_mixer_bwd(dx, saved, gain, w_zx_t, w_dt_t, conv_w, conv_b, par, d_rep, out_norm, w_out, grads, row_out,
                   tag):
    x, h, zx, dtp, xa, y, s_in, gn = saved
    di = w_out.shape[0]
    n_groups = di // (SSD_HEADS_PER_GROUP * SSD_HEAD_DIM)
    dgn = _matmul(dx, w_out, "nt", name=f"ssd_dgn_{tag}")
    grads = _matmul_tn_into(grads, gn, dx, row_out, name=f"ssd_dwout_{tag}")
    dy2, dz, dnorm = _ssd_gate_bwd(dgn, y, xa, zx, d_rep, out_norm, name=f"ssd_dgate_{tag}")
    dxs, db, dc, ddtp, dpar = _ssd_scan_bwd(dy2, xa, dtp, par, s_in, n_groups, name=f"ssd_dscan_{tag}")
    dxbc, dconv_w, dconv_b = _ssd_conv_bwd([dxs, db, dc], zx, conv_w, conv_b, di, name=f"ssd_dconv_{tag}")
    dzx = jnp.concatenate([dz, dxbc], axis=1)
    dw_zx_t = _matmul_tn(dzx, h, name=f"ssd_dwin_{tag}")
    dw_dt_t = _matmul_tn(ddtp, h, name=f"ssd_dwdt_{tag}")
    dh = _matmul(dzx, w_zx_t, "nn", name=f"ssd_dh_{tag}")
    dh = _matmul(ddtp, w_dt_t, "nn", resid=dh, name=f"ssd_dhdt_{tag}")
    dx_in, dgain = _rms_bwd(x, gain, dh, dx, name=f"ssd_dnorm_{tag}")
    return dx_in, grads, dw_zx_t, dw_dt_t, dconv_w, dconv_b, dpar, dnorm, dgain


HEAD_SUM_PIECES = 2


def _head_sums(v):
    row = lax.broadcasted_iota(jnp.int32, (LANES, LANES), 0)
    col = lax.broadcasted_iota(jnp.int32, (LANES, LANES), 1)
    same_head = (row // SB_HEAD_DIM == col // SB_HEAD_DIM).astype(BF16)
    return _split_dot(v, same_head, HEAD_SUM_PIECES)


def _sb_qk_norm_fwd(qkv, gains, *, name):
    ns, t, _ = qkv.shape
    per = ns // 3
    tm = _tile(t, 1024, SUBLANES)
    inv_sqrt_d = 1.0 / math.sqrt(SB_HEAD_DIM)

    def body(x_ref, g_ref, o_ref):
        kind = pl.program_id(0) // per
        xv = x_ref[...]

        @pl.when(kind == 2)
        def _():
            o_ref[...] = xv.astype(BF16)

        @pl.when(kind < 2)
        def _():
            ms = _head_sums(xv * xv) * (1.0 / SB_HEAD_DIM)
            y = xv * lax.rsqrt(ms + NORM_EPS) * g_ref[pl.ds(kind, 1), :]
            o_ref[...] = (y * jnp.where(kind == 0, inv_sqrt_d, 1.0)).astype(BF16)

    blk = pl.BlockSpec((None, tm, LANES), lambda s, i: (s, i, 0))
    return pl.pallas_call(
        body, name=name, grid=(ns, t // tm),
        in_specs=[blk, pl.BlockSpec((SUBLANES, LANES), lambda s, i: (0, 0))], out_specs=blk,
        out_shape=jax.ShapeDtypeStruct((ns, t, LANES), BF16),
        compiler_params=_params("parallel", "parallel"),
    )(qkv, gains)


def _sb_qk_norm_bwd(dq, dk, dv, qkv, gains, *, name):
    ns, t, _ = qkv.shape
    per = ns // 3
    tm = _tile(t, 1024, SUBLANES)
    inv_sqrt_d = 1.0 / math.sqrt(SB_HEAD_DIM)

    def body(dq_ref, dk_ref, dv_ref, x_ref, g_ref, o_ref, dg_ref):
        s = pl.program_id(0)
        kind = s // per

        @pl.when((s == 0) & (pl.program_id(1) == 0))
        def _():
            dg_ref[...] = jnp.zeros_like(dg_ref)

        @pl.when(kind == 2)
        def _():
            o_ref[...] = dv_ref[...].astype(BF16)

        @pl.when(kind < 2)
        def _():
            xv = x_ref[...]
            dy = jnp.where(kind == 0, dq_ref[...] * inv_sqrt_d, dk_ref[...])
            r = lax.rsqrt(_head_sums(xv * xv) * (1.0 / SB_HEAD_DIM) + NORM_EPS)
            xhat = xv * r
            u = dy * g_ref[pl.ds(kind, 1), :]
            o_ref[...] = (r * (u - xhat * _head_sums(u * xhat) * (1.0 / SB_HEAD_DIM))).astype(BF16)
            dg_ref[pl.ds(kind, 1), :] += jnp.sum(dy * xhat, axis=0, keepdims=True)

    def grad_blk(kind):
        def index(s, i):
            mine = (s >= kind * per) & (s < (kind + 1) * per)
            return jnp.where(mine, s - kind * per, 0), jnp.where(mine, i, 0), 0
        return pl.BlockSpec((None, tm, LANES), index)

    blk = pl.BlockSpec((None, tm, LANES), lambda s, i: (s, i, 0))
    vec = pl.BlockSpec((SUBLANES, LANES), lambda s, i: (0, 0))
    return pl.pallas_call(
        body, name=name, grid=(ns, t // tm),
        in_specs=[grad_blk(0), grad_blk(1), grad_blk(2), blk, vec], out_specs=[blk, vec],
        out_shape=[jax.ShapeDtypeStruct((ns, t, LANES), BF16), jax.ShapeDtypeStruct((SUBLANES, LANES), F32)],
        compiler_params=_params("arbitrary", "arbitrary"),
    )(dq, dk, dv, qkv, gains)


def _split_dot(v, ones_mat, pieces, left=False):
    total, rest = None, v
    for p in range(pieces):
        part = rest.astype(BF16)
        if p + 1 < pieces:
            rest = rest - part.astype(F32)
        d = (jnp.dot(ones_mat, part, preferred_element_type=F32) if left
             else jnp.dot(part, ones_mat, preferred_element_type=F32))
        total = d if total is None else total + d
    return total


LOGIT_SUM_PIECES = 2
GRAD_SUM_PIECES = 1
LOG_WEIGHT_UNDERFLOW = -105.0


def _sb_attn_fwd(qkv_n, n_heads, *, name):
    ns, t, _ = qkv_n.shape
    per = ns // 3
    bq, blk, hd = SB_QUERY_BLOCK, SB_BLOCK, SB_HEAD_DIM
    nq, n_diag = t // bq, bq // blk

    def body(q_ref, k_ref, v_ref, o_ref, walk_ref):
        i = pl.program_id(1)
        row = lax.broadcasted_iota(jnp.int32, (blk, blk), 0)
        col = lax.broadcasted_iota(jnp.int32, (blk, blk), 1)
        later_keys = (row > col).astype(BF16)
        qry = lax.broadcasted_iota(jnp.int32, (bq, blk), 0)
        key = lax.broadcasted_iota(jnp.int32, (bq, blk), 1)

        def tile(kb, carry, key_offset):
            out = []
            start = pl.multiple_of(kb * blk, blk)
            for hf in range(2):
                lanes = slice(hf * hd, (hf + 1) * hd)
                run, acc = carry[hf]
                z = lax.dot_general(q_ref[:, lanes], k_ref[pl.ds(start, blk), lanes], NT_DIMS,
                                    preferred_element_type=F32)
                sp = _softplus(z)
                lm = -sp if key_offset is None else jnp.where(key + key_offset < qry, -sp, 0.0)
                after = _split_dot(lm, later_keys, LOGIT_SUM_PIECES) + run
                a = jnp.exp(z - sp + after)
                if key_offset is not None:
                    a = jnp.where(key + key_offset < qry, a, 0.0)
                acc = acc + jnp.dot(a.astype(BF16), v_ref[pl.ds(start, blk), lanes], preferred_element_type=F32)
                out.append((run + jnp.sum(lm, axis=1, keepdims=True), acc))
            return tuple(out)

        def live(carry):
            return jnp.max(jnp.maximum(carry[0][0], carry[1][0])) > LOG_WEIGHT_UNDERFLOW

        def step(state):
            s, _, carry = state
            carry = tile(n_diag * i - 1 - s, carry, None)
            return s + 1, live(carry), carry

        carry = tuple((jnp.zeros((bq, 1), F32), jnp.zeros((bq, hd), F32)) for _ in range(2))
        for j in reversed(range(n_diag)):
            carry = tile(n_diag * i + j, carry, j * blk)
        walked, _, carry = lax.while_loop(lambda st: (st[0] < n_diag * i) & st[1], step,
                                          (jnp.int32(0), live(carry), carry))
        o_ref[...] = jnp.concatenate([carry[0][1], carry[1][1]], axis=1)
        lane = lax.broadcasted_iota(jnp.int32, (1, LANES), 1)
        walk_ref[...] = jnp.where(lane < WALK_LANES, carry[0][0],
                                  jnp.where(lane < 2 * WALK_LANES, carry[1][0], walked.astype(F32)))

    q_blk = pl.BlockSpec((None, bq, LANES), lambda p, i: (p, i, 0))
    return pl.pallas_call(
        body, name=name, grid=(per, nq),
        in_specs=[q_blk, pl.BlockSpec((None, t, LANES), lambda p, i: (per + p, 0, 0)),
                  pl.BlockSpec((None, t, LANES), lambda p, i: (2 * per + p, 0, 0))],
        out_specs=[pl.BlockSpec((bq, LANES), lambda p, i: (i, p)), q_blk],
        out_shape=[jax.ShapeDtypeStruct((t, n_heads * hd), F32), jax.ShapeDtypeStruct((per, t, LANES), F32)],
        compiler_params=_params("parallel", "arbitrary"),
    )(qkv_n, qkv_n, qkv_n)


WALK_LANES = 43


def _sb_attn_bwd(do, walk, qkv_n, *, name):
    ns, t, _ = qkv_n.shape
    per = ns // 3
    bq, blk, hd = SB_QUERY_BLOCK, SB_BLOCK, SB_HEAD_DIM
    nq, n_diag = t // bq, bq // blk
    nt_dims = (((1,), (1,)), ((), ()))
    tn_dims = (((0,), (0,)), ((), ()))

    def body(q_ref, k_ref, v_ref, do_ref, walk_ref, dq_ref, dk_ref, dv_ref):
        i = pl.program_id(1)

        @pl.when(i == 0)
        def _():
            dk_ref[...] = jnp.zeros_like(dk_ref)
            dv_ref[...] = jnp.zeros_like(dv_ref)

        walk_t = walk_ref[...].T
        tots = [walk_t[hf * WALK_LANES:hf * WALK_LANES + 1, :] for hf in range(2)]
        reached = jnp.clip(jnp.max(walk_t[2 * WALK_LANES:2 * WALK_LANES + 1, :]).astype(jnp.int32), 0, n_diag * i)

        row = lax.broadcasted_iota(jnp.int32, (blk, blk), 0)
        col = lax.broadcasted_iota(jnp.int32, (blk, blk), 1)
        later_keys = (col > row).astype(BF16)
        earlier_keys = (col < row).astype(BF16)
        key = lax.broadcasted_iota(jnp.int32, (blk, bq), 0)
        qry = lax.broadcasted_iota(jnp.int32, (blk, bq), 1)
        halves = [slice(hf * hd, (hf + 1) * hd) for hf in range(2)]
        q_hs = [q_ref[:, lanes] for lanes in halves]
        do_bs = [do_ref[:, lanes].astype(BF16) for lanes in halves]

        def scores(kb, hf, key_offset):
            k_blk = k_ref[pl.ds(pl.multiple_of(kb * blk, blk), blk), halves[hf]]
            z = lax.dot_general(k_blk, q_hs[hf], nt_dims, preferred_element_type=F32)
            sp = _softplus(z)
            return k_blk, z, sp, (-sp if key_offset is None else jnp.where(key + key_offset < qry, -sp, 0.0))

        def tile(kb, carry, key_offset):
            out = []
            start = pl.multiple_of(kb * blk, blk)
            for hf, lanes in enumerate(halves):
                seen, gsum, dq = carry[hf]
                q_h, do_b = q_hs[hf], do_bs[hf]
                k_blk, z, sp, lm = scores(kb, hf, key_offset)
                blk_tot = jnp.sum(lm, axis=0, keepdims=True)
                after = _split_dot(lm, later_keys, LOGIT_SUM_PIECES, left=True) + (tots[hf] - seen - blk_tot)
                a = jnp.exp(z - sp + after)
                if key_offset is not None:
                    a = jnp.where(key + key_offset < qry, a, 0.0)
                da = lax.dot_general(v_ref[pl.ds(start, blk), lanes], do_b, nt_dims, preferred_element_type=F32)
                g = da * a
                before = _split_dot(g, earlier_keys, GRAD_SUM_PIECES, left=True) + gsum
                omb = jnp.exp(-sp)
                dz = g * omb - (1.0 - omb) * before
                if key_offset is not None:
                    dz = jnp.where(key + key_offset < qry, dz, 0.0)
                dz_b = dz.astype(BF16)
                dk_ref[pl.ds(start, blk), lanes] += jnp.dot(dz_b, q_h, preferred_element_type=F32)
                dv_ref[pl.ds(start, blk), lanes] += jnp.dot(a.astype(BF16), do_b, preferred_element_type=F32)
                dq = dq + lax.dot_general(dz_b, k_blk, tn_dims, preferred_element_type=F32)
                out.append((seen + blk_tot, gsum + jnp.sum(g, axis=0, keepdims=True), dq))
            return tuple(out)

        init = tuple((jnp.zeros((1, bq), F32), jnp.zeros((1, bq), F32), jnp.zeros((bq, hd), F32))
                     for _ in range(2))
        carry = lax.fori_loop(n_diag * i - reached, n_diag * i, lambda kb, c: tile(kb, c, None), init)
        for j in range(n_diag):
            carry = tile(n_diag * i + j, carry, j * blk)
        dq_ref[...] = jnp.concatenate([carry[0][2], carry[1][2]], axis=1)

    full = lambda off: pl.BlockSpec((None, t, LANES), lambda p, i: (off + p, 0, 0))
    q_blk = pl.BlockSpec((None, bq, LANES), lambda p, i: (p, i, 0))
    slab = jax.ShapeDtypeStruct((per, t, LANES), F32)
    return pl.pallas_call(
        body, name=name, grid=(per, nq),
        in_specs=[q_blk, full(per), full(2 * per), pl.BlockSpec((bq, LANES), lambda p, i: (i, p)), q_blk],
        out_specs=[q_blk, full(0), full(0)],
        out_shape=[slab, slab, slab],
        compiler_params=_params("parallel", "arbitrary"),
    )(qkv_n, qkv_n, qkv_n, do, walk)


def _sb_mixer_fwd(x, gain, w_qkv_t, qk_gains, w_out, tag):
    n_heads = w_out.shape[0] // SB_HEAD_DIM
    h = _rms_fwd(x, gain, name=f"sb_norm_{tag}")
    qkv = _matmul(h, w_qkv_t, "nt", out_slabs=True, tn_cap=512, name=f"sb_qkv_{tag}")
    qkv_n = _sb_qk_norm_fwd(qkv, qk_gains, name=f"sb_qknorm_{tag}")
    o, walk = _sb_attn_fwd(qkv_n, n_heads, name=f"sb_attn_{tag}")
    x_new = _matmul(o, w_out, "nn", resid=x, name=f"sb_out_{tag}")
    return x_new, (x, h, qkv, qkv_n, o, walk)


def _sb_mixer_bwd(dx, saved, gain, w_qkv_t, qk_gains, w_out, grads, row_qkv, row_out, tag):
    x, h, qkv, qkv_n, o, walk = saved
    do = _matmul(dx, w_out, "nt", name=f"sb_do_{tag}")
    grads = _matmul_tn_into(grads, o, dx, row_out, name=f"sb_dwout_{tag}")
    dq, dk, dv = _sb_attn_bwd(do, walk, qkv_n, name=f"sb_dattn_{tag}")
    dqkv, dqk_gains = _sb_qk_norm_bwd(dq, dk, dv, qkv, qk_gains, name=f"sb_dqknorm_{tag}")
    grads = _matmul_tn_into(grads, dqkv, h, row_qkv, a_slabs=True, name=f"sb_dwqkv_{tag}")
    dh = _matmul(dqkv, w_qkv_t, "nn", a_slabs=True, name=f"sb_dh_{tag}")
    dx_in, dgain = _rms_bwd(x, gain, dh, dx, name=f"sb_dnorm_{tag}")
    return dx_in, grads, dqk_gains, dgain


MESH = pl.DeviceIdType.MESH


def _position():
    return lax.axis_index("x"), lax.axis_index("y"), lax.axis_index("c")


def _all_gather(shard, *, name):
    rows, n = shard.shape
    space = pltpu.VMEM

    def body(x_ref, out_ref, send_sems, recv_sems, local_sem):
        x, y, c = _position()
        me, sibling = (x, y, c), (x, y, 1 - c)
        chips = [(1 - x, y), (x, 1 - y), (1 - x, 1 - y)]

        def block(px, py, pc):
            return out_ref.at[4 * px + 2 * py + pc]

        def copy(k, blk, to, src=None):
            return pltpu.make_async_remote_copy(
                src_ref=block(*blk) if src is None else src, dst_ref=block(*blk),
                send_sem=send_sems.at[k], recv_sem=recv_sems.at[k], device_id=to, device_id_type=MESH)

        mine = pltpu.make_async_copy(x_ref, block(*me), local_sem)
        mine.start()
        first = [copy(0, me, sibling, src=x_ref)]
        first += [copy(1 + j, me, (*chip, c), src=x_ref) for j, chip in enumerate(chips)]
        for cp in first:
            cp.start()
        passed = [copy(4 + j, (*chip, c), sibling) for j, chip in enumerate(chips)]
        for j, chip in enumerate(chips):
            copy(1 + j, (*chip, c), me).wait_recv()
            passed[j].start()
        copy(0, sibling, me).wait_recv()
        for j, chip in enumerate(chips):
            copy(4 + j, (*chip, 1 - c), me).wait_recv()
        for cp in first + passed:
            cp.wait_send()
        mine.wait()

    return pl.pallas_call(
        body, name=name,
        out_shape=jax.ShapeDtypeStruct((N_DEV, rows, n), shard.dtype),
        in_specs=[pl.BlockSpec(memory_space=space)], out_specs=pl.BlockSpec(memory_space=space),
        scratch_shapes=[pltpu.SemaphoreType.DMA((7,)), pltpu.SemaphoreType.DMA((7,)), pltpu.SemaphoreType.DMA],
        compiler_params=pltpu.CompilerParams(vmem_limit_bytes=V7X_VMEM_LIMIT_BYTES),
    )(shard)


def _all_gather_forwarding(shard, *, name):
    rows, n = shard.shape
    half = rows // 2
    assert rows % (4 * SUBLANES) == 0

    def body(x_ref, out_ref, send_sems, recv_sems, local_sem):
        x, y, c = _position()
        me, sibling = (x, y, c), (x, y, 1 - c)
        x_nbr, y_nbr, diag = (1 - x, y), (x, 1 - y), (1 - x, 1 - y)
        lower, upper = pl.ds(0, half), pl.ds(half, half)

        def block(px, py, pc, part=None):
            ref = out_ref.at[4 * px + 2 * py + pc]
            return ref if part is None else ref.at[part]

        def copy(k, blk, to, src=None, part=None):
            return pltpu.make_async_remote_copy(
                src_ref=block(*blk, part) if src is None else src, dst_ref=block(*blk, part),
                send_sem=send_sems.at[k], recv_sem=recv_sems.at[k], device_id=to, device_id_type=MESH)

        mine = pltpu.make_async_copy(x_ref, block(*me), local_sem)
        mine.start()
        sent = [copy(0, me, sibling, src=x_ref), copy(1, me, (*x_nbr, c), src=x_ref),
                copy(2, me, (*y_nbr, c), src=x_ref)]
        for cp in sent:
            cp.start()
        copy(1, (*x_nbr, c), me).wait_recv()
        onward = [copy(3, (*x_nbr, c), (*y_nbr, c), part=lower), copy(5, (*x_nbr, c), sibling)]
        for cp in onward:
            cp.start()
        copy(2, (*y_nbr, c), me).wait_recv()
        onward += [copy(4, (*y_nbr, c), (*x_nbr, c), part=upper), copy(6, (*y_nbr, c), sibling)]
        for cp in onward[2:]:
            cp.start()
        copy(3, (*diag, c), me, part=lower).wait_recv()
        copy(4, (*diag, c), me, part=upper).wait_recv()
        onward.append(copy(7, (*diag, c), sibling))
        onward[-1].start()
        sent += onward
        copy(0, sibling, me).wait_recv()
        for k, chip in ((5, x_nbr), (6, y_nbr), (7, diag)):
            copy(k, (*chip, 1 - c), me).wait_recv()
        for cp in sent:
            cp.wait_send()
        mine.wait()

    hbm = pl.BlockSpec(memory_space=pltpu.HBM)
    return pl.pallas_call(
        body, name=name,
        out_shape=jax.ShapeDtypeStruct((N_DEV, rows, n), shard.dtype), in_specs=[hbm], out_specs=hbm,
        scratch_shapes=[pltpu.SemaphoreType.DMA((8,)), pltpu.SemaphoreType.DMA((8,)), pltpu.SemaphoreType.DMA],
    )(shard)


def _exchange_sibling(parts, *, name):
    _, nchip, rows, n = parts.shape

    def body(p_ref, recv_ref, send_sem, recv_sem):
        x, y, c = _position()
        cp = pltpu.make_async_remote_copy(src_ref=p_ref.at[1 - c], dst_ref=recv_ref, send_sem=send_sem,
                                          recv_sem=recv_sem, device_id=(x, y, 1 - c), device_id_type=MESH)
        cp.start()
        cp.wait()

    return pl.pallas_call(
        body, name=name,
        out_shape=jax.ShapeDtypeStruct((nchip, rows, n), parts.dtype),
        in_specs=[pl.BlockSpec(memory_space=pltpu.HBM)], out_specs=pl.BlockSpec(memory_space=pltpu.HBM),
        scratch_shapes=[pltpu.SemaphoreType.DMA, pltpu.SemaphoreType.DMA],
    )(parts)


def _exchange_chips(chip_sums, *, name):
    _, rows, n = chip_sums.shape

    def body(s_ref, recv_ref, send_sems, recv_sems):
        x, y, c = _position()
        chips = [(1 - x, y), (x, 1 - y), (1 - x, 1 - y)]
        copies = [pltpu.make_async_remote_copy(
            src_ref=s_ref.at[2 * cx + cy], dst_ref=recv_ref.at[j], send_sem=send_sems.at[j],
            recv_sem=recv_sems.at[j], device_id=(cx, cy, c), device_id_type=MESH)
            for j, (cx, cy) in enumerate(chips)]
        for cp in copies:
            cp.start()
        for cp in copies:
            cp.wait()

    return pl.pallas_call(
        body, name=name,
        out_shape=jax.ShapeDtypeStruct((3, rows, n), chip_sums.dtype),
        in_specs=[pl.BlockSpec(memory_space=pltpu.HBM)], out_specs=pl.BlockSpec(memory_space=pltpu.HBM),
        scratch_shapes=[pltpu.SemaphoreType.DMA((3,)), pltpu.SemaphoreType.DMA((3,))],
    )(chip_sums)


def _add_pairs(parts, recv, c_and_chip, *, name):
    _, nchip, rows, n = parts.shape
    tr = _tile(rows, 512, SUBLANES)

    def body(pos_ref, a_ref, b_ref, own_ref, wire_ref):
        s = a_ref[...] + b_ref[...]
        wire_ref[...] = s.astype(WIRE_DTYPE)

        @pl.when(pl.program_id(1) == pos_ref[1])
        def _():
            own_ref[...] = s

    return pl.pallas_call(
        body, name=name,
        grid_spec=pltpu.PrefetchScalarGridSpec(
            num_scalar_prefetch=1, grid=(rows // tr, nchip),
            in_specs=[pl.BlockSpec((None, None, tr, n), lambda i, k, pos: (pos[0], k, i, 0)),
                      pl.BlockSpec((None, tr, n), lambda i, k, pos: (k, i, 0))],
            out_specs=[pl.BlockSpec((tr, n), lambda i, k, pos: (i, 0)),
                       pl.BlockSpec((None, tr, n), lambda i, k, pos: (k, i, 0))]),
        out_shape=[jax.ShapeDtypeStruct((rows, n), parts.dtype),
                   jax.ShapeDtypeStruct((nchip, rows, n), WIRE_DTYPE)],
        compiler_params=_params("parallel", "arbitrary"),
    )(c_and_chip, parts, recv)


def _adamw_math(w, g, m, v):
    m = ADAM_B1 * m + (1.0 - ADAM_B1) * g
    v = ADAM_B2 * v + (1.0 - ADAM_B2) * (g * g)
    m_hat = m / (1.0 - ADAM_B1 ** ADAM_STEP)
    v_hat = v / (1.0 - ADAM_B2 ** ADAM_STEP)
    delta = -ADAM_LR * (m_hat / (jnp.sqrt(v_hat) + ADAM_EPS) + ADAM_WD * w)
    return delta, m, v


def _adamw_sharded(own_sum, recv, w, m, v, *, name):
    rows, n = w.shape
    tr = _tile(rows, 256, SUBLANES)

    def body(s_ref, r_ref, w_ref, m_ref, v_ref, g_out, d_out, m_out, v_out):
        g = ((s_ref[...] + r_ref[0].astype(F32)) + r_ref[1].astype(F32)) + r_ref[2].astype(F32)
        delta, m_new, v_new = _adamw_math(w_ref[...], g, m_ref[...], v_ref[...])
        g_out[...] = g
        d_out[...] = delta
        m_out[...] = m_new
        v_out[...] = v_new

    blk = pl.BlockSpec((tr, n), lambda i: (i, 0))
    out = jax.ShapeDtypeStruct((rows, n), F32)
    return pl.pallas_call(
        body, name=name, grid=(rows // tr,),
        in_specs=[blk, pl.BlockSpec((3, tr, n), lambda i: (0, i, 0)), blk, blk, blk],
        out_specs=[blk, blk, blk, blk], out_shape=[out, out, out, out],
        compiler_params=_params("parallel"),
    )(own_sum, recv, w, m, v)


SMALL_ROWS = 40
ROW_MIX_NORM, ROW_FFN_NORM, ROW_CONV_B, ROW_OUT_NORM, ROW_POOL_SCALE, ROW_CONV_W = 0, 4, 8, 12, 14, 16
ROW_SSD_VEC, ROW_QK_GAIN, ROW_LOSS = 32, 33, 34


def _adamw_small(gathered, w, m, v, *, name):
    _, rows, n = gathered.shape

    def body(a_ref, w_ref, m_ref, v_ref, g_out, d_out, m_out, v_out):
        g = a_ref[0]
        for d in range(1, N_DEV):
            g = g + a_ref[d]
        row = lax.broadcasted_iota(jnp.int32, (rows, 1), 0)
        g = jnp.where(row == ROW_QK_GAIN, g + pltpu.roll(g, SB_HEAD_DIM, 1), g)
        g = jnp.where(row == ROW_LOSS, jnp.sum(g, axis=1, keepdims=True), g)
        g_out[...] = g
        delta, m_new, v_new = _adamw_math(w_ref[...], g, m_ref[...], v_ref[...])
        d_out[...] = delta
        m_out[...] = m_new
        v_out[...] = v_new

    out = jax.ShapeDtypeStruct((rows, n), F32)
    return pl.pallas_call(body, name=name, out_shape=[out, out, out, out])(gathered, w, m, v)


BIG_WEIGHTS = ("ffn_gate", "ffn_up", "ffn_down", "sb_qkv", "ssd_out", "pool_in", "sb_out", "pool_group", "ssd_in")
COLUMN_SHARDED = ("ssd_in", "sb_qkv", "ffn_gate", "ffn_up")
ROW_PAD = 512
WIRE_DTYPE = jnp.bfloat16


def _to_rows(name, shard, d):
    if name in COLUMN_SHARDED:
        shard = jnp.swapaxes(shard, -1, -2)
    return shard.reshape(-1, d)


def _from_rows(name, rows, shard_shape):
    if name in COLUMN_SHARDED:
        lead, k, n = shard_shape
        return jnp.swapaxes(rows.reshape(lead, n, k), -1, -2)
    return rows.reshape(shard_shape)


def _pad_rows(a, total):
    return jnp.pad(a, ((0, total - a.shape[0]),) + ((0, 0),) * (a.ndim - 1))


def _exact_bf16_rows(v, d):
    words = lax.bitcast_convert_type(v.reshape(-1), WIRE_DTYPE).reshape(-1)
    return _pad_rows(words, -(-words.shape[0] // d) * d).reshape(-1, d)


def _exact_f32(rows, count):
    words = rows.reshape(rows.shape[0], -1)[:, :2 * count].reshape(rows.shape[0], count, 2)
    return lax.bitcast_convert_type(words, F32)


def _device_blocks(full, d):
    return full.reshape(N_DEV, -1, d)


def kernel(x, mix_norm, pool_in, pool_group, pool_scale, ssd_in, ssd_conv_w, ssd_conv_b, ssd_dt_bias, ssd_a_log, ssd_d, ssd_out_norm, ssd_out, sb_qkv, sb_q_norm, sb_k_norm, sb_out, ffn_norm, ffn_gate, ffn_up, ffn_down, loss_target, m_mix_norm, m_pool_in, m_pool_group, m_pool_scale, m_ssd_in, m_ssd_conv_w, m_ssd_conv_b, m_ssd_dt_bias, m_ssd_a_log, m_ssd_d, m_ssd_out_norm, m_ssd_out, m_sb_qkv, m_sb_q_norm, m_sb_k_norm, m_sb_out, m_ffn_norm, m_ffn_gate, m_ffn_up, m_ffn_down, v_mix_norm, v_pool_in, v_pool_group, v_pool_scale, v_ssd_in, v_ssd_conv_w, v_ssd_conv_b, v_ssd_dt_bias, v_ssd_a_log, v_ssd_d, v_ssd_out_norm, v_ssd_out, v_sb_qkv, v_sb_q_norm, v_sb_k_norm, v_sb_out, v_ffn_norm, v_ffn_gate, v_ffn_up, v_ffn_down):
    weights = dict(mix_norm=mix_norm, pool_in=pool_in, pool_group=pool_group, pool_scale=pool_scale, ssd_in=ssd_in,
                   ssd_conv_w=ssd_conv_w, ssd_conv_b=ssd_conv_b, ssd_dt_bias=ssd_dt_bias, ssd_a_log=ssd_a_log,
                   ssd_d=ssd_d, ssd_out_norm=ssd_out_norm, ssd_out=ssd_out, sb_qkv=sb_qkv, sb_q_norm=sb_q_norm,
                   sb_k_norm=sb_k_norm, sb_out=sb_out, ffn_norm=ffn_norm, ffn_gate=ffn_gate, ffn_up=ffn_up,
                   ffn_down=ffn_down)
    mom1 = dict(mix_norm=m_mix_norm, pool_in=m_pool_in, pool_group=m_pool_group, pool_scale=m_pool_scale,
                ssd_in=m_ssd_in, ssd_conv_w=m_ssd_conv_w, ssd_conv_b=m_ssd_conv_b, ssd_dt_bias=m_ssd_dt_bias,
                ssd_a_log=m_ssd_a_log, ssd_d=m_ssd_d, ssd_out_norm=m_ssd_out_norm, ssd_out=m_ssd_out,
                sb_qkv=m_sb_qkv, sb_q_norm=m_sb_q_norm, sb_k_norm=m_sb_k_norm, sb_out=m_sb_out,
                ffn_norm=m_ffn_norm, ffn_gate=m_ffn_gate, ffn_up=m_ffn_up, ffn_down=m_ffn_down)
    mom2 = dict(mix_norm=v_mix_norm, pool_in=v_pool_in, pool_group=v_pool_group, pool_scale=v_pool_scale,
                ssd_in=v_ssd_in, ssd_conv_w=v_ssd_conv_w, ssd_conv_b=v_ssd_conv_b, ssd_dt_bias=v_ssd_dt_bias,
                ssd_a_log=v_ssd_a_log, ssd_d=v_ssd_d, ssd_out_norm=v_ssd_out_norm, ssd_out=v_ssd_out,
                sb_qkv=v_sb_qkv, sb_q_norm=v_sb_q_norm, sb_k_norm=v_sb_k_norm, sb_out=v_sb_out,
                ffn_norm=v_ffn_norm, ffn_gate=v_ffn_gate, ffn_up=v_ffn_up, ffn_down=v_ffn_down)
    names = list(weights)
    depth, d = mix_norm.shape
    xs, ys, cs = _position()
    dev = 4 * xs + 2 * ys + cs
    chip = 2 * xs + ys

    seg = {}
    row = 0
    for name in BIG_WEIGHTS:
        n_rows = weights[name].size // d
        seg[name] = (row, n_rows)
        row += -(-n_rows // SUBLANES) * SUBLANES
    big_rows = row
    n_scale, n_convw = pool_scale.size, ssd_conv_w.size
    exact = jnp.concatenate([_exact_bf16_rows(pool_scale, d), _exact_bf16_rows(ssd_conv_w, d)], axis=0)
    scale_rows = _exact_bf16_rows(pool_scale, d).shape[0]
    packed_rows = -(-(big_rows + exact.shape[0]) // ROW_PAD) * ROW_PAD

    def pack(tree, dtype):
        ends = [seg[n][0] for n in BIG_WEIGHTS[1:]] + [big_rows]
        return jnp.concatenate([_pad_rows(_to_rows(n, tree[n], d).astype(dtype), end - seg[n][0])
                                for n, end in zip(BIG_WEIGHTS, ends)], axis=0)

    w_wire = _pad_rows(jnp.concatenate([pack(weights, WIRE_DTYPE), exact], axis=0), packed_rows)
    gathered = _all_gather_forwarding(w_wire, name="gather_weights")

    def seg_of(name):
        a, n = seg[name]
        return gathered[:, a:a + n]

    n_pool, n_ssd, n_sb = pool_in.shape[0], ssd_in.shape[0], sb_qkv.shape[0]
    assert n_ssd == 1 and n_sb == 1
    w_pool_in = seg_of("pool_in").reshape(N_DEV, n_pool, -1, d).transpose(1, 0, 2, 3).reshape(n_pool, d, d)
    grp = pool_group.shape
    w_pool_group = seg_of("pool_group").reshape(N_DEV, grp[0], grp[1], grp[2], grp[3]).transpose(1, 2, 0, 3, 4)
    w_pool_group = w_pool_group.reshape(grp[0], grp[1], grp[3], grp[3])
    w_ssd_in_t = seg_of("ssd_in").reshape(-1, d)
    w_ssd_out = seg_of("ssd_out").reshape(-1, d)
    w_sb_qkv_t = seg_of("sb_qkv").reshape(-1, d)
    w_sb_out = seg_of("sb_out").reshape(-1, d)
    hidden = ffn_down.shape[1] * N_DEV
    w_gate_t = seg_of("ffn_gate").reshape(N_DEV, depth, -1, d).transpose(1, 0, 2, 3).reshape(depth, hidden, d)
    w_up_t = seg_of("ffn_up").reshape(N_DEV, depth, -1, d).transpose(1, 0, 2, 3).reshape(depth, hidden, d)
    w_down = seg_of("ffn_down").reshape(N_DEV, depth, -1, d).transpose(1, 0, 2, 3).reshape(depth, hidden, d)
    exact_all = gathered[:, big_rows:big_rows + exact.shape[0]]
    scale_full = _exact_f32(exact_all[:, :scale_rows], n_scale).reshape(N_DEV, n_pool, -1)
    scale_full = scale_full.transpose(1, 0, 2).reshape(n_pool, d)
    convw_full = _exact_f32(exact_all[:, scale_rows:], n_convw).reshape(N_DEV, SSD_CONV, -1)
    convw_full = convw_full.transpose(1, 0, 2).reshape(SSD_CONV, -1)

    d_inner = w_ssd_out.shape[0]
    n_zx = w_ssd_in_t.shape[0] - ssd_dt_bias.shape[1]
    w_zx_t = w_ssd_in_t[:n_zx]
    n_ssd_heads = ssd_dt_bias.shape[1]
    n_ssd_groups = n_ssd_heads // SSD_HEADS_PER_GROUP
    w_dt_t = _ssd_group_pad(w_ssd_in_t[n_zx:].T, n_ssd_groups).T
    par = _pad_rows(_ssd_group_pad(jnp.concatenate([ssd_dt_bias, ssd_a_log, ssd_d], axis=0), n_ssd_groups), SUBLANES)
    d_rep = jnp.repeat(ssd_d[0], SSD_HEAD_DIM)[None]
    qk_gains = jnp.zeros((SUBLANES, LANES), F32).at[0].set(jnp.tile(sb_q_norm[0], 2)).at[1].set(jnp.tile(sb_k_norm[0], 2))

    act = x[0]
    saved = []
    for i in range(depth):
        kind, j = i % 3, i // 3
        gain = mix_norm[i:i + 1]
        if kind == 0:
            act, s = _pool_mixer_fwd(act, gain, w_pool_in[j], w_pool_group[j], scale_full[j:j + 1], f"l{i}")
        elif kind == 1:
            act, s = _ssd_mixer_fwd(act, gain, w_zx_t, w_dt_t, convw_full, ssd_conv_b, par, d_rep, ssd_out_norm,
                                    w_ssd_out, f"l{i}")
        else:
            act, s = _sb_mixer_fwd(act, gain, w_sb_qkv_t, qk_gains, w_sb_out, f"l{i}")
        act, f = _ffn_fwd(act, ffn_norm[i:i + 1], w_gate_t[i], w_up_t[i], w_down[i], f"l{i}")
        saved.append((s, f))
    dact, loss_cols = _loss_head(act, loss_target[0], name="loss_head")

    def layer_row(name, layer):
        return seg[name][0] + layer * (seg[name][1] // weights[name].shape[0])

    grads = jax.ShapeDtypeStruct((N_DEV, packed_rows, d), F32)
    g_mix_norm, g_ffn_norm = [None] * depth, [None] * depth
    g_pool_group, g_pool_scale = [None] * n_pool, [None] * n_pool
    for i in reversed(range(depth)):
        kind, j = i % 3, i // 3
        gain = mix_norm[i:i + 1]
        s, f = saved[i]
        dact, grads, g_ffn_norm[i] = _ffn_bwd(
            dact, f, ffn_norm[i:i + 1], w_gate_t[i], w_up_t[i], w_down[i], grads,
            [layer_row(n, i) for n in ("ffn_gate", "ffn_up", "ffn_down")], f"l{i}")
        if kind == 0:
            dact, grads, g_pool_group[j], g_pool_scale[j], g_mix_norm[i] = _pool_mixer_bwd(
                dact, s, gain, w_pool_in[j], w_pool_group[j], scale_full[j:j + 1], grads, layer_row("pool_in", j),
                f"l{i}")
        elif kind == 1:
            (dact, grads, g_zx_t, g_dt_t, g_conv_w, g_conv_b, g_par, g_out_norm,
             g_mix_norm[i]) = _ssd_mixer_bwd(dact, s, gain, w_zx_t, w_dt_t, convw_full, ssd_conv_b, par, d_rep,
                                             ssd_out_norm, w_ssd_out, grads, layer_row("ssd_out", j), f"l{i}")
        else:
            dact, grads, g_qk_gains, g_mix_norm[i] = _sb_mixer_bwd(
                dact, s, gain, w_sb_qkv_t, qk_gains, w_sb_out, grads, layer_row("sb_qkv", j),
                layer_row("sb_out", j), f"l{i}")
    grad_x = dact[None]

    g_ssd_in = jnp.concatenate([g_zx_t, _ssd_group_unpad(g_dt_t.T, n_ssd_groups).T], axis=0)
    g_group = jnp.concatenate([_device_blocks(gg[k], d) for gg in g_pool_group for k in range(gg.shape[0])], axis=1)
    for name, blocks in (("ssd_in", _device_blocks(g_ssd_in, d)), ("pool_group", g_group)):
        blocks = blocks.reshape(N_DEV // 2, 2, -1, d).swapaxes(0, 1).reshape(N_DEV, -1, d)
        blocks = jnp.pad(blocks, ((0, 0), (0, -blocks.shape[1] % SUBLANES), (0, 0)))
        grads = lax.dynamic_update_slice(grads, blocks, (0, seg[name][0], 0))
    grads = lax.dynamic_update_slice(grads, jnp.zeros((N_DEV, packed_rows - big_rows, d), F32), (0, big_rows, 0))
    parts = grads.reshape(2, N_DEV // 2, packed_rows, d)
    from_sibling = _exchange_sibling(parts, name="reduce_sibling")
    own_sum, chip_sums_wire = _add_pairs(parts, from_sibling, jnp.stack([cs, chip]).astype(jnp.int32),
                                         name="reduce_sibling_add")
    from_chips = _exchange_chips(chip_sums_wire, name="reduce_chips")

    def pack_f32(tree):
        return _pad_rows(pack(tree, F32), packed_rows)

    big_out = _adamw_sharded(own_sum, from_chips, pack_f32(weights), pack_f32(mom1), pack_f32(mom2),
                             name="adamw_sharded")

    def small_pack(mix, ffn, conv_b, out_norm, scale, conv_w, vec, qk, loss=None):
        buf = jnp.zeros((SMALL_ROWS, d), F32)
        buf = buf.at[ROW_MIX_NORM:ROW_MIX_NORM + depth].set(mix).at[ROW_FFN_NORM:ROW_FFN_NORM + depth].set(ffn)
        buf = buf.at[ROW_CONV_B:ROW_CONV_B + conv_b.size // d].set(conv_b.reshape(-1, d))
        buf = buf.at[ROW_OUT_NORM:ROW_OUT_NORM + out_norm.size // d].set(out_norm.reshape(-1, d))
        buf = buf.at[ROW_POOL_SCALE:ROW_POOL_SCALE + n_pool].set(scale)
        buf = buf.at[ROW_CONV_W:ROW_CONV_W + conv_w.size // d].set(conv_w.reshape(-1, d))
        buf = buf.at[ROW_SSD_VEC].set(vec.reshape(-1)).at[ROW_QK_GAIN].set(qk.reshape(-1))
        if loss is not None:
            buf = buf.at[ROW_LOSS].set(loss.reshape(-1))
        return buf

    def small_params(tree):
        scale = lax.dynamic_update_slice(jnp.zeros((n_pool, d), F32), tree["pool_scale"],
                                         (0, dev * tree["pool_scale"].shape[1]))
        conv_w = lax.dynamic_update_slice(jnp.zeros(convw_full.shape, F32), tree["ssd_conv_w"][0],
                                          (0, dev * tree["ssd_conv_w"].shape[2]))
        vec = jnp.zeros((SUBLANES, LANES), F32)
        vec = vec.at[0, :n_ssd_heads].set(tree["ssd_dt_bias"][0]).at[1, :n_ssd_heads].set(tree["ssd_a_log"][0])
        vec = vec.at[2, :n_ssd_heads].set(tree["ssd_d"][0])
        qk = jnp.zeros((SUBLANES, LANES), F32)
        qk = qk.at[0, SB_HEAD_DIM:].set(tree["sb_q_norm"][0]).at[1, SB_HEAD_DIM:].set(tree["sb_k_norm"][0])
        return small_pack(tree["mix_norm"], tree["ffn_norm"], tree["ssd_conv_b"], tree["ssd_out_norm"], scale,
                          conv_w, vec, qk)

    small_partial = small_pack(jnp.concatenate(g_mix_norm, axis=0), jnp.concatenate(g_ffn_norm, axis=0), g_conv_b,
                               g_out_norm, jnp.concatenate(g_pool_scale, axis=0), g_conv_w,
                               jnp.zeros((SUBLANES, LANES), F32).at[:3, :n_ssd_heads].set(
                                   _ssd_group_unpad(g_par[:3], n_ssd_groups)), g_qk_gains,
                               loss_cols)
    small_all = _all_gather(small_partial, name="gather_small")
    small_out = _adamw_small(small_all, small_params(weights), small_params(mom1), small_params(mom2),
                             name="adamw_small")
    loss = small_out[0][ROW_LOSS, 0]

    def unpack(big, small):
        out = {}
        for name in BIG_WEIGHTS:
            a, n = seg[name]
            out[name] = _from_rows(name, big[a:a + n], weights[name].shape)
        out["mix_norm"] = small[ROW_MIX_NORM:ROW_MIX_NORM + depth]
        out["ffn_norm"] = small[ROW_FFN_NORM:ROW_FFN_NORM + depth]
        out["ssd_conv_b"] = small[ROW_CONV_B:ROW_CONV_B + ssd_conv_b.size // d].reshape(ssd_conv_b.shape)
        out["ssd_out_norm"] = small[ROW_OUT_NORM:ROW_OUT_NORM + ssd_out_norm.size // d].reshape(ssd_out_norm.shape)
        out["pool_scale"] = lax.dynamic_slice(small[ROW_POOL_SCALE:ROW_POOL_SCALE + n_pool],
                                              (0, dev * pool_scale.shape[1]), pool_scale.shape)
        conv_w = small[ROW_CONV_W:ROW_CONV_W + convw_full.size // d].reshape(convw_full.shape)
        out["ssd_conv_w"] = lax.dynamic_slice(conv_w, (0, dev * ssd_conv_w.shape[2]), ssd_conv_w.shape[1:])[None]
        vec = small[ROW_SSD_VEC].reshape(SUBLANES, LANES)
        out["ssd_dt_bias"], out["ssd_a_log"], out["ssd_d"] = (vec[r:r + 1, :n_ssd_heads] for r in range(3))
        qk = small[ROW_QK_GAIN].reshape(SUBLANES, LANES)
        out["sb_q_norm"], out["sb_k_norm"] = qk[0:1, SB_HEAD_DIM:], qk[1:2, SB_HEAD_DIM:]
        return [out[n] for n in names]

    results = [unpack(b, s) for b, s in zip(big_out, small_out)]
    return (loss, grad_x, *results[0], *results[1], *results[2], *results[3])
```

```python
import math

import jax
import jax.numpy as jnp
from jax import lax
from jax.experimental import pallas as pl
from jax.experimental.pallas import tpu as pltpu

F32 = jnp.float32
BF16 = jnp.bfloat16

N_DEV = 8
NORM_EPS = 1e-6
V7X_VMEM_LIMIT_BYTES = 48 * 1024 * 1024
LANES = 128
SUBLANES = 8

POOL_WINDOWS = (2, 4, 8, 16)
SSD_CHUNK = 256
SSD_HEAD_DIM = 64
SSD_STATE = 128
SSD_HEADS_PER_GROUP = 4
SSD_CONV = 4
SB_HEAD_DIM = 64
SB_BLOCK = 256
SB_QUERY_BLOCK = 256

ADAM_LR = 0.001
ADAM_B1 = 0.9
ADAM_B2 = 0.999
ADAM_EPS = 1e-08
ADAM_WD = 0.01
ADAM_STEP = 10


def _params(*sem):
    return pltpu.CompilerParams(dimension_semantics=sem, vmem_limit_bytes=V7X_VMEM_LIMIT_BYTES)


def _tile(n, cap, mult):
    best = None
    for t in range(mult, min(n, cap) + 1, mult):
        if n % t == 0:
            best = t
    return best or n


def _load_slabs(ref, slabs):
    if not slabs:
        return ref[...]
    return jnp.concatenate([ref[p] for p in range(ref.shape[0])], axis=1)


def _matmul(a, b, mode, *, name, out_dtype=F32, resid=None, a_slabs=False, out_slabs=False,
            tm_cap=1024, tn_cap=1024, tk_cap=2048):
    pairs = list(zip(a, b)) if isinstance(a, (list, tuple)) else [(a, b)]
    a, b = pairs[0]
    if a_slabs:
        m, k = a.shape[1], a.shape[0] * LANES
    else:
        m, k = a.shape
    n = b.shape[1] if mode == "nn" else b.shape[0]
    assert (b.shape[0] if mode == "nn" else b.shape[1]) == k
    assert all(pa.shape == a.shape and pb.shape == b.shape for pa, pb in pairs)
    tm, tn, tk = _tile(m, tm_cap, SUBLANES), _tile(n, tn_cap, LANES), _tile(k, tk_cap, LANES)
    nk = k // tk
    dn = (((1,), (0,)), ((), ())) if mode == "nn" else (((1,), (1,)), ((), ()))
    has_resid = resid is not None
    n_pairs = len(pairs)

    def body(*refs):
        ab_refs, rest = refs[:2 * n_pairs], refs[2 * n_pairs:]
        r_ref = rest[0] if has_resid else None
        o_ref = rest[1] if has_resid else rest[0]
        kk = pl.program_id(2)

        def partial():
            total = None
            for p in range(n_pairs):
                d = lax.dot_general(_load_slabs(ab_refs[2 * p], a_slabs).astype(BF16),
                                    ab_refs[2 * p + 1][...].astype(BF16), dn, preferred_element_type=F32)
                total = d if total is None else total + d
            return total

        def finish(r):
            if has_resid:
                r = r + r_ref[...]
            if out_slabs:
                for p in range(tn // LANES):
                    o_ref[p] = r[:, p * LANES:(p + 1) * LANES].astype(out_dtype)
            else:
                o_ref[...] = r.astype(out_dtype)

        if nk == 1:
            finish(partial())
        else:
            acc = rest[-1]

            @pl.when(kk == 0)
            def _():
                acc[...] = jnp.zeros_like(acc)

            acc[...] += partial()

            @pl.when(kk == nk - 1)
            def _():
                finish(acc[...])

    b_spec = (pl.BlockSpec((tk, tn), lambda i, j, kk: (kk, j)) if mode == "nn"
              else pl.BlockSpec((tn, tk), lambda i, j, kk: (j, kk)))
    a_spec = (pl.BlockSpec((tk // LANES, tm, LANES), lambda i, j, kk: (kk, i, 0)) if a_slabs
              else pl.BlockSpec((tm, tk), lambda i, j, kk: (i, kk)))
    in_specs = [a_spec, b_spec] * n_pairs
    args = [t for pair in pairs for t in pair]
    if has_resid:
        in_specs.append(pl.BlockSpec((tm, tn), lambda i, j, kk: (i, j)))
        args.append(resid)
    if out_slabs:
        out_spec = pl.BlockSpec((tn // LANES, tm, LANES), lambda i, j, kk: (j, i, 0))
        out_shape = jax.ShapeDtypeStruct((n // LANES, m, LANES), out_dtype)
    else:
        out_spec = pl.BlockSpec((tm, tn), lambda i, j, kk: (i, j))
        out_shape = jax.ShapeDtypeStruct((m, n), out_dtype)
    return pl.pallas_call(
        body, name=name, grid=(m // tm, n // tn, nk),
        in_specs=in_specs, out_specs=out_spec, out_shape=out_shape,
        scratch_shapes=[pltpu.VMEM((tm, tn), F32)] if nk > 1 else [],
        compiler_params=_params("parallel", "parallel", "arbitrary"),
    )(*args)


def _matmul_tn(a, b, *, name, a_slabs=False, ta_cap=1024, tb_cap=1024, tr_cap=1024):
    if a_slabs:
        r, ka = a.shape[1], a.shape[0] * LANES
    else:
        r, ka = a.shape
    nb = b.shape[1]
    assert b.shape[0] == r
    ta, tb, tr = _tile(ka, ta_cap, LANES), _tile(nb, tb_cap, LANES), _tile(r, tr_cap, SUBLANES)

    def body(a_ref, b_ref, o_ref):
        @pl.when(pl.program_id(2) == 0)
        def _():
            o_ref[...] = jnp.zeros_like(o_ref)

        o_ref[...] += lax.dot_general(_load_slabs(a_ref, a_slabs).astype(BF16), b_ref[...].astype(BF16),
                                      (((0,), (0,)), ((), ())), preferred_element_type=F32)

    a_spec = (pl.BlockSpec((ta // LANES, tr, LANES), lambda i, j, kk: (i, kk, 0)) if a_slabs
              else pl.BlockSpec((tr, ta), lambda i, j, kk: (kk, i)))
    return pl.pallas_call(
        body, name=name, grid=(ka // ta, nb // tb, r // tr),
        in_specs=[a_spec, pl.BlockSpec((tr, tb), lambda i, j, kk: (kk, j))],
        out_specs=pl.BlockSpec((ta, tb), lambda i, j, kk: (i, j)),
        out_shape=jax.ShapeDtypeStruct((ka, nb), F32),
        compiler_params=_params("parallel", "parallel", "arbitrary"),
    )(a, b)


def _core_major(k):
    return (k % 2) * (N_DEV // 2) + k // 2


def _matmul_tn_into(buf, a, b, row_off, *, name, a_slabs=False, tr_cap=1024):
    if a_slabs:
        r, ka = a.shape[1], a.shape[0] * LANES
    else:
        r, ka = a.shape
    n_dev, _, n = buf.shape
    per = ka // n_dev
    assert b.shape == (r, n) and ka % n_dev == 0 and per % SUBLANES == 0 and row_off % per == 0
    tr = _tile(r, tr_cap, SUBLANES)
    fresh = isinstance(buf, jax.ShapeDtypeStruct)

    def body(*refs):
        a_ref, b_ref, o_ref = refs[-3:]
        prod = lax.dot_general(_load_slabs(a_ref, a_slabs).astype(BF16), b_ref[...].astype(BF16),
                               (((0,), (0,)), ((), ())), preferred_element_type=F32)
        @pl.when(pl.program_id(0) == 0)
        def _():
            for k in range(n_dev):
                o_ref[_core_major(k)] = prod[k * per:(k + 1) * per]

        @pl.when(pl.program_id(0) > 0)
        def _():
            for k in range(n_dev):
                o_ref[_core_major(k)] += prod[k * per:(k + 1) * per]

    a_spec = (pl.BlockSpec((ka // LANES, tr, LANES), lambda i: (0, i, 0)) if a_slabs
              else pl.BlockSpec((tr, ka), lambda i: (i, 0)))
    return pl.pallas_call(
        body, name=name, grid=(r // tr,),
        in_specs=([] if fresh else [pl.BlockSpec(memory_space=pl.ANY)]) + [a_spec, pl.BlockSpec((tr, n), lambda i: (i, 0))],
        out_specs=pl.BlockSpec((n_dev, per, n), lambda i: (0, row_off // per, 0)),
        out_shape=jax.ShapeDtypeStruct(buf.shape, F32),
        input_output_aliases={} if fresh else {0: 0},
        compiler_params=_params("arbitrary"),
    )(*(() if fresh else (buf,)), a, b)


def _rms_fwd(x, gain, *, name):
    t, d = x.shape
    tm = _tile(t, 512, SUBLANES)

    def body(x_ref, g_ref, o_ref):
        xv = x_ref[...]
        r = lax.rsqrt(jnp.mean(xv * xv, axis=-1, keepdims=True) + NORM_EPS)
        o_ref[...] = (xv * r * g_ref[...]).astype(BF16)

    return pl.pallas_call(
        body, name=name, grid=(t // tm,),
        in_specs=[pl.BlockSpec((tm, d), lambda i: (i, 0)), pl.BlockSpec((1, d), lambda i: (0, 0))],
        out_specs=pl.BlockSpec((tm, d), lambda i: (i, 0)),
        out_shape=jax.ShapeDtypeStruct((t, d), BF16),
        compiler_params=_params("parallel"),
    )(x, gain)


def _rms_bwd(x, gain, dh, dres, *, name):
    t, d = x.shape
    tm = _tile(t, 512, SUBLANES)

    def body(x_ref, g_ref, dh_ref, dres_ref, dx_ref, dg_ref):
        @pl.when(pl.program_id(0) == 0)
        def _():
            dg_ref[...] = jnp.zeros_like(dg_ref)

        xv = x_ref[...]
        r = lax.rsqrt(jnp.mean(xv * xv, axis=-1, keepdims=True) + NORM_EPS)
        xhat = xv * r
        dhv = dh_ref[...]
        u = dhv * g_ref[...]
        dx_ref[...] = dres_ref[...] + r * (u - xhat * jnp.mean(u * xhat, axis=-1, keepdims=True))
        dg_ref[...] += jnp.sum(dhv * xhat, axis=0, keepdims=True)

    return pl.pallas_call(
        body, name=name, grid=(t // tm,),
        in_specs=[pl.BlockSpec((tm, d), lambda i: (i, 0)), pl.BlockSpec((1, d), lambda i: (0, 0)),
                  pl.BlockSpec((tm, d), lambda i: (i, 0)), pl.BlockSpec((tm, d), lambda i: (i, 0))],
        out_specs=[pl.BlockSpec((tm, d), lambda i: (i, 0)), pl.BlockSpec((1, d), lambda i: (0, 0))],
        out_shape=[jax.ShapeDtypeStruct((t, d), F32), jax.ShapeDtypeStruct((1, d), F32)],
        compiler_params=_params("arbitrary"),
    )(x, gain, dh, dres)


def _loss_head(y, target, *, name):
    t, d = y.shape
    tm = _tile(t, 512, SUBLANES)

    def body(y_ref, t_ref, dy_ref, l_ref):
        @pl.when(pl.program_id(0) == 0)
        def _():
            l_ref[...] = jnp.zeros_like(l_ref)

        e = y_ref[...] - t_ref[...]
        dy_ref[...] = e * (1.0 / d)
        l_ref[...] += jnp.sum(e * e, axis=0, keepdims=True) * (0.5 / d)

    return pl.pallas_call(
        body, name=name, grid=(t // tm,),
        in_specs=[pl.BlockSpec((tm, d), lambda i: (i, 0)), pl.BlockSpec((tm, d), lambda i: (i, 0))],
        out_specs=[pl.BlockSpec((tm, d), lambda i: (i, 0)), pl.BlockSpec((1, d), lambda i: (0, 0))],
        out_shape=[jax.ShapeDtypeStruct((t, d), F32), jax.ShapeDtypeStruct((1, d), F32)],
        compiler_params=_params("arbitrary"),
    )(y, target)


def _sigmoid(v):
    return 0.5 * jnp.tanh(0.5 * v) + 0.5


FFN_TOKEN_TILE = 512
FFN_HIDDEN_TILE = 1408
NT_DIMS = (((1,), (1,)), ((), ()))


def _ffn_up(h, w_gate_t, w_up_t, *, name):
    t, d = h.shape
    f = w_gate_t.shape[0]
    tm, tn = _tile(t, FFN_TOKEN_TILE, SUBLANES), _tile(f, FFN_HIDDEN_TILE, LANES)

    def body(h_ref, g_ref, u_ref, s_ref, a_ref, b_ref):
        hv = h_ref[...].astype(BF16)
        av = lax.dot_general(hv, g_ref[...].astype(BF16), NT_DIMS, preferred_element_type=F32)
        bv = lax.dot_general(hv, u_ref[...].astype(BF16), NT_DIMS, preferred_element_type=F32)
        s_ref[...] = (av * _sigmoid(av) * bv).astype(BF16)
        a_ref[...] = av.astype(BF16)
        b_ref[...] = bv.astype(BF16)

    w_spec = pl.BlockSpec((tn, d), lambda j, i: (j, 0))
    out_spec = pl.BlockSpec((tm, tn), lambda j, i: (i, j))
    out = jax.ShapeDtypeStruct((t, f), BF16)
    return pl.pallas_call(
        body, name=name, grid=(f // tn, t // tm),
        in_specs=[pl.BlockSpec((tm, d), lambda j, i: (i, 0)), w_spec, w_spec],
        out_specs=[out_spec, out_spec, out_spec], out_shape=[out, out, out],
        compiler_params=_params("parallel", "parallel"),
    )(h, w_gate_t, w_up_t)


def _ffn_dact(dx, w_down, a, b, *, name):
    t, d = dx.shape
    f = w_down.shape[0]
    tm, tn = _tile(t, FFN_TOKEN_TILE, SUBLANES), _tile(f, FFN_HIDDEN_TILE, LANES)

    def body(dx_ref, w_ref, a_ref, b_ref, da_ref, db_ref):
        ds = lax.dot_general(dx_ref[...].astype(BF16), w_ref[...].astype(BF16), NT_DIMS, preferred_element_type=F32)
        av = a_ref[...].astype(F32)
        sg = _sigmoid(av)
        da_ref[...] = (ds * b_ref[...].astype(F32) * (sg * (1.0 + av * (1.0 - sg)))).astype(BF16)
        db_ref[...] = (ds * av * sg).astype(BF16)

    blk = pl.BlockSpec((tm, tn), lambda j, i: (i, j))
    out = jax.ShapeDtypeStruct((t, f), BF16)
    return pl.pallas_call(
        body, name=name, grid=(f // tn, t // tm),
        in_specs=[pl.BlockSpec((tm, d), lambda j, i: (i, 0)), pl.BlockSpec((tn, d), lambda j, i: (j, 0)), blk, blk],
        out_specs=[blk, blk], out_shape=[out, out],
        compiler_params=_params("parallel", "parallel"),
    )(dx, w_down, a, b)


def _ffn_fwd(x, gain, w_gate_t, w_up_t, w_down, tag):
    h = _rms_fwd(x, gain, name=f"ffn_norm_{tag}")
    s, a, b = _ffn_up(h, w_gate_t, w_up_t, name=f"ffn_up_{tag}")
    x_new = _matmul(s, w_down, "nn", resid=x, tn_cap=1024, tk_cap=2816, name=f"ffn_down_{tag}")
    return x_new, (x, h, a, b, s)


def _ffn_bwd(dx, saved, gain, w_gate_t, w_up_t, w_down, grads, rows, tag):
    x, h, a, b, s = saved
    da, db = _ffn_dact(dx, w_down, a, b, name=f"ffn_dact_{tag}")
    grads = _matmul_tn_into(grads, da, h, rows[0], name=f"ffn_dwgate_{tag}")
    grads = _matmul_tn_into(grads, db, h, rows[1], name=f"ffn_dwup_{tag}")
    grads = _matmul_tn_into(grads, s, dx, rows[2], name=f"ffn_dwdown_{tag}")
    dh = _matmul([da, db], [w_gate_t, w_up_t], "nn", tm_cap=512, tn_cap=1024, tk_cap=2816, name=f"ffn_dh_{tag}")
    dx_in, dgain = _rms_bwd(x, gain, dh, dx, name=f"ffn_dnorm_{tag}")
    return dx_in, grads, dgain


POOL_HALO = 16


def _shift_rows(v, k):
    n = v.shape[0]
    return pltpu.roll(v, k % n, 0)


def _window_sum(v, w, direction):
    k = 1
    while k < w:
        v = v + _shift_rows(v, direction * k)
        k *= 2
    return v


def _pool_fwd(u, x, w_group, scale, *, name):
    t, d = u.shape
    ng, dg = w_group.shape[0], w_group.shape[1]
    tm = _tile(t, 512, POOL_HALO)
    hb = tm // POOL_HALO

    def body(u_ref, halo_ref, x_ref, w_ref, s_ref, xo_ref, p_ref, y_ref):
        i, g = pl.program_id(0), pl.program_id(1)
        halo = jnp.where(i > 0, halo_ref[...], 0.0)
        ext = jnp.concatenate([halo, u_ref[...]], axis=0)
        pos = i * tm + lax.broadcasted_iota(jnp.int32, (tm, 1), 0)
        for gi, win in enumerate(POOL_WINDOWS):
            @pl.when(g == gi)
            def _(win=win):
                tot = _window_sum(ext, win, 1)[POOL_HALO:]
                cnt = jnp.minimum(pos + 1, win).astype(F32)
                p = (tot / cnt - u_ref[...]).astype(BF16)
                p_ref[...] = p
                y = jnp.dot(p, w_ref[...].astype(BF16), preferred_element_type=F32)
                y_ref[...] = y
                xo_ref[...] = x_ref[...] + y * s_ref[...]

    blk = pl.BlockSpec((tm, dg), lambda i, g: (i, g))
    return pl.pallas_call(
        body, name=name, grid=(t // tm, ng),
        in_specs=[blk, pl.BlockSpec((POOL_HALO, dg), lambda i, g: (jnp.maximum(i * hb - 1, 0), g)), blk,
                  pl.BlockSpec((None, dg, dg), lambda i, g: (g, 0, 0)), pl.BlockSpec((1, dg), lambda i, g: (0, g))],
        out_specs=[blk, blk, blk],
        out_shape=[jax.ShapeDtypeStruct((t, d), F32), jax.ShapeDtypeStruct((t, d), BF16),
                   jax.ShapeDtypeStruct((t, d), F32)],
        compiler_params=_params("parallel", "parallel"),
    )(u, u, x, w_group, scale)


def _pool_bwd(dx, p, y_pre, w_group, scale, *, name):
    t, d = dx.shape
    ng, dg = w_group.shape[0], w_group.shape[1]
    tm = _tile(t, 512, POOL_HALO)
    hb = tm // POOL_HALO
    nt = t // tm

    def body(dx_ref, nxt_ref, p_ref, y_ref, w_ref, s_ref, du_ref, dw_ref, ds_ref):
        g, i = pl.program_id(0), pl.program_id(1)

        @pl.when(i == 0)
        def _():
            dw_ref[...] = jnp.zeros_like(dw_ref)
            ds_ref[...] = jnp.zeros_like(ds_ref)

        dxv = dx_ref[...]
        ds_ref[...] += jnp.sum(dxv * y_ref[...], axis=0, keepdims=True)
        nxt = jnp.where(i < nt - 1, nxt_ref[...], 0.0)
        dyp = (jnp.concatenate([dxv, nxt], axis=0) * s_ref[...]).astype(BF16)
        dw_ref[...] += lax.dot_general(p_ref[...], dyp[:tm], (((0,), (0,)), ((), ())), preferred_element_type=F32)
        dp = lax.dot_general(dyp, w_ref[...].astype(BF16), (((1,), (1,)), ((), ())), preferred_element_type=F32)
        pos = i * tm + lax.broadcasted_iota(jnp.int32, (tm + POOL_HALO, 1), 0)
        for gi, win in enumerate(POOL_WINDOWS):
            @pl.when(g == gi)
            def _(win=win):
                q = dp / jnp.minimum(pos + 1, win).astype(F32)
                du_ref[...] = (_window_sum(q, win, -1)[:tm] - dp[:tm]).astype(BF16)

    blk = pl.BlockSpec((tm, dg), lambda g, i: (i, g))
    return pl.pallas_call(
        body, name=name, grid=(ng, nt),
        in_specs=[blk, pl.BlockSpec((POOL_HALO, dg), lambda g, i: (jnp.minimum((i + 1) * hb, t // POOL_HALO - 1), g)),
                  blk, blk, pl.BlockSpec((None, dg, dg), lambda g, i: (g, 0, 0)),
                  pl.BlockSpec((1, dg), lambda g, i: (0, g))],
        out_specs=[blk, pl.BlockSpec((None, dg, dg), lambda g, i: (g, 0, 0)), pl.BlockSpec((1, dg), lambda g, i: (0, g))],
        out_shape=[jax.ShapeDtypeStruct((t, d), BF16), jax.ShapeDtypeStruct((ng, dg, dg), F32),
                   jax.ShapeDtypeStruct((1, d), F32)],
        compiler_params=_params("parallel", "arbitrary"),
    )(dx, dx, p, y_pre, w_group, scale)


def _pool_mixer_fwd(x, gain, w_in, w_group, scale, tag):
    h = _rms_fwd(x, gain, name=f"pool_norm_{tag}")
    u = _matmul(h, w_in, "nn", name=f"pool_in_{tag}")
    x_new, p, y_pre = _pool_fwd(u, x, w_group, scale, name=f"pool_mix_{tag}")
    return x_new, (x, h, p, y_pre)


def _pool_mixer_bwd(dx, saved, gain, w_in, w_group, scale, grads, row_in, tag):
    x, h, p, y_pre = saved
    du, dw_group, dscale = _pool_bwd(dx, p, y_pre, w_group, scale, name=f"pool_dmix_{tag}")
    grads = _matmul_tn_into(grads, h, du, row_in, name=f"pool_dwin_{tag}")
    dh = _matmul(du, w_in, "nt", name=f"pool_dh_{tag}")
    dx_in, dgain = _rms_bwd(x, gain, dh, dx, name=f"pool_dnorm_{tag}")
    return dx_in, grads, dw_group, dscale, dgain


CONV_HALO = 8
NEG_BIG = -1e30


def _softplus(v):
    return jnp.maximum(v, 0.0) + jnp.log(1.0 + jnp.exp(-jnp.abs(v)))


def _conv_taps(ext, w_ref, off, rows):
    acc = None
    for k in range(SSD_CONV):
        shift = SSD_CONV - 1 - k
        v = (_shift_rows(ext, shift) if shift else ext)[off:off + rows] * w_ref[k:k + 1, :]
        acc = v if acc is None else acc + v
    return acc


def _ssd_conv_fwd(zx, conv_w, conv_b, col0, *, name):
    t = zx.shape[0]
    c = conv_w.shape[1]
    tm, tc = _tile(t, 512, CONV_HALO), _tile(c, 512, LANES)
    hb, cb0 = tm // CONV_HALO, col0 // tc
    assert col0 % tc == 0

    def body(x_ref, halo_ref, w_ref, b_ref, o_ref):
        halo = jnp.where(pl.program_id(0) > 0, halo_ref[...], 0.0)
        ext = jnp.concatenate([halo, x_ref[...]], axis=0)
        pre = _conv_taps(ext, w_ref, CONV_HALO, tm) + b_ref[...]
        o_ref[...] = pre * _sigmoid(pre)

    return pl.pallas_call(
        body, name=name, grid=(t // tm, c // tc),
        in_specs=[pl.BlockSpec((tm, tc), lambda i, j: (i, j + cb0)),
                  pl.BlockSpec((CONV_HALO, tc), lambda i, j: (jnp.maximum(i * hb - 1, 0), j + cb0)),
                  pl.BlockSpec((SSD_CONV, tc), lambda i, j: (0, j)), pl.BlockSpec((1, tc), lambda i, j: (0, j))],
        out_specs=pl.BlockSpec((tm, tc), lambda i, j: (i, j)),
        out_shape=jax.ShapeDtypeStruct((t, c), F32),
        compiler_params=_params("parallel", "parallel"),
    )(zx, zx, conv_w, conv_b)


def _ssd_conv_bwd(d_parts, zx, conv_w, conv_b, col0, *, name):
    t = zx.shape[0]
    c = conv_w.shape[1]
    tm, tc = _tile(t, 512, CONV_HALO), _tile(c, 512, LANES)
    hb, cb0, nt = tm // CONV_HALO, col0 // tc, t // tm
    last_halo = t // CONV_HALO - 1
    starts = [0]
    for part in d_parts:
        assert part.shape[1] % tc == 0
        starts.append(starts[-1] + part.shape[1] // tc)
    assert starts[-1] == c // tc
    n_parts = len(d_parts)

    def pick(refs, j):
        value = refs[-1][...]
        for p in reversed(range(n_parts - 1)):
            value = jnp.where(j < starts[p + 1], refs[p][...], value)
        return value

    def body(x_ref, prev_ref, nxt_ref, *rest):
        d_refs, dnxt_refs = rest[:n_parts], rest[n_parts:2 * n_parts]
        w_ref, b_ref, dx_ref, dw_ref, db_ref = rest[2 * n_parts:]
        j, i = pl.program_id(0), pl.program_id(1)

        @pl.when(i == 0)
        def _():
            dw_ref[...] = jnp.zeros_like(dw_ref)
            db_ref[...] = jnp.zeros_like(db_ref)

        prev = jnp.where(i > 0, prev_ref[...], 0.0)
        has_next = i < nt - 1
        ext = jnp.concatenate([prev, x_ref[...], jnp.where(has_next, nxt_ref[...], 0.0)], axis=0)
        pre = _conv_taps(ext, w_ref, CONV_HALO, tm + CONV_HALO) + b_ref[...]
        sg = _sigmoid(pre)
        dact = jnp.concatenate([pick(d_refs, j), jnp.where(has_next, pick(dnxt_refs, j), 0.0)], axis=0)
        dpre = dact * (sg * (1.0 + pre * (1.0 - sg)))
        db_ref[...] += jnp.sum(dpre[:tm], axis=0, keepdims=True)
        acc = None
        for k in range(SSD_CONV):
            shift = SSD_CONV - 1 - k
            src = (_shift_rows(ext, shift) if shift else ext)[CONV_HALO:CONV_HALO + tm]
            dw_ref[k:k + 1, :] += jnp.sum(dpre[:tm] * src, axis=0, keepdims=True)
            v = (_shift_rows(dpre, -shift) if shift else dpre)[:tm] * w_ref[k:k + 1, :]
            acc = v if acc is None else acc + v
        dx_ref[...] = acc.astype(BF16)

    def part_specs(rows, row_index):
        def spec(p):
            def index(j, i):
                mine = (j >= starts[p]) & (j < starts[p + 1])
                return jnp.where(mine, row_index(i), 0), jnp.where(mine, j - starts[p], 0)
            return pl.BlockSpec((rows, tc), index)
        return [spec(p) for p in range(n_parts)]

    main = lambda j, i: (i, j + cb0)
    next_halo = lambda i: jnp.minimum((i + 1) * hb, last_halo)
    return pl.pallas_call(
        body, name=name, grid=(c // tc, nt),
        in_specs=[pl.BlockSpec((tm, tc), main),
                  pl.BlockSpec((CONV_HALO, tc), lambda j, i: (jnp.maximum(i * hb - 1, 0), j + cb0)),
                  pl.BlockSpec((CONV_HALO, tc), lambda j, i: (next_halo(i), j + cb0)),
                  *part_specs(tm, lambda i: i), *part_specs(CONV_HALO, next_halo),
                  pl.BlockSpec((SSD_CONV, tc), lambda j, i: (0, j)), pl.BlockSpec((1, tc), lambda j, i: (0, j))],
        out_specs=[pl.BlockSpec((tm, tc), lambda j, i: (i, j)), pl.BlockSpec((SSD_CONV, tc), lambda j, i: (0, j)),
                   pl.BlockSpec((1, tc), lambda j, i: (0, j))],
        out_shape=[jax.ShapeDtypeStruct((t, c), BF16), jax.ShapeDtypeStruct((SSD_CONV, c), F32),
                   jax.ShapeDtypeStruct((1, c), F32)],
        compiler_params=_params("parallel", "arbitrary"),
    )(zx, zx, zx, *d_parts, *d_parts, conv_w, conv_b)


SSD_CUMSUM_PIECES = 2
SSD_GROUPS_PER_STEP = 1


def _ssd_group_pad(v, n_groups):
    lead = v.shape[:-1]
    v = v.reshape(*lead, n_groups, SSD_HEADS_PER_GROUP)
    v = jnp.pad(v, [(0, 0)] * (len(lead) + 1) + [(0, LANES - SSD_HEADS_PER_GROUP)])
    return v.reshape(*lead, n_groups * LANES)


def _ssd_group_unpad(v, n_groups):
    lead = v.shape[:-1]
    return v.reshape(*lead, n_groups, LANES)[..., :SSD_HEADS_PER_GROUP].reshape(*lead, -1)


def _ssd_chunk_common(dtp_ref, par_ref):
    ell = SSD_CHUNK
    dt = _softplus(dtp_ref[...] + par_ref[0:1, :])
    a = -jnp.exp(par_ref[1:2, :])
    row = lax.broadcasted_iota(jnp.int32, (ell, ell), 0)
    col = lax.broadcasted_iota(jnp.int32, (ell, ell), 1)
    acum = _split_dot(dt * a, (row >= col).astype(BF16), SSD_CUMSUM_PIECES, left=True)
    return dt, a, acum, acum.T, row, col


def _ssd_scan_fwd(xa, dtp, par, n_groups, *, name):
    t = xa.shape[0]
    ell, hd, hpg, ns, gps = SSD_CHUNK, SSD_HEAD_DIM, SSD_HEADS_PER_GROUP, SSD_STATE, SSD_GROUPS_PER_STEP
    gw = hpg * hd
    nc = t // ell
    b_blk0, c_blk0 = n_groups * gw // (ns * gps), (n_groups * gw // ns + n_groups) // gps

    def body(xs_ref, b_ref, c_ref, dtp_ref, par_ref, y_ref, sin_ref, st):
        @pl.when(pl.program_id(1) == 0)
        def _():
            st[...] = jnp.zeros_like(st)

        dt, _, acum, acum_t, row, col = _ssd_chunk_common(dtp_ref, par_ref)
        for gi in range(gps):
            bb = b_ref[:, gi * ns:(gi + 1) * ns].astype(BF16)
            cc = c_ref[:, gi * ns:(gi + 1) * ns].astype(BF16)
            cb = lax.dot_general(cc, bb, NT_DIMS, preferred_element_type=F32)
            s_all = st[gi]
            sin_ref[gi] = s_all
            c_s = lax.dot_general(cc, s_all.astype(BF16), NT_DIMS, preferred_element_type=F32)
            weighted, keep = [], []
            for hh in range(hpg):
                lanes = slice(gi * gw + hh * hd, gi * gw + (hh + 1) * hd)
                hl = gi * LANES + hh
                col_a, row_a = acum[:, hl:hl + 1], acum_t[hl:hl + 1, :]
                decay = jnp.exp(jnp.where(row >= col, col_a - row_a, NEG_BIG))
                xdt = xs_ref[:, lanes] * dt[:, hl:hl + 1]
                y = jnp.dot((cb * decay).astype(BF16), xdt.astype(BF16), preferred_element_type=F32)
                y_ref[:, lanes] = y + jnp.exp(col_a) * c_s[:, hh * hd:(hh + 1) * hd]
                a_last = acum[ell - 1:ell, hl:hl + 1]
                weighted.append((xdt * jnp.exp(a_last - col_a)).astype(BF16))
                keep.append(jnp.broadcast_to(jnp.exp(a_last), (hd, 1)))
            st[gi] = jnp.concatenate(keep, axis=0) * s_all + lax.dot_general(
                jnp.concatenate(weighted, axis=1), bb, (((0,), (0,)), ((), ())), preferred_element_type=F32)

    return pl.pallas_call(
        body, name=name, grid=(n_groups // gps, nc),
        in_specs=[pl.BlockSpec((ell, gps * gw), lambda g, c: (c, g)),
                  pl.BlockSpec((ell, gps * ns), lambda g, c: (c, b_blk0 + g)),
                  pl.BlockSpec((ell, gps * ns), lambda g, c: (c, c_blk0 + g)),
                  pl.BlockSpec((ell, gps * LANES), lambda g, c: (c, g)),
                  pl.BlockSpec((SUBLANES, gps * LANES), lambda g, c: (0, g))],
        out_specs=[pl.BlockSpec((ell, gps * gw), lambda g, c: (c, g)),
                   pl.BlockSpec((None, gps, gw, ns), lambda g, c: (c, g, 0, 0))],
        out_shape=[jax.ShapeDtypeStruct((t, n_groups * gw), F32),
                   jax.ShapeDtypeStruct((nc, n_groups, gw, ns), F32)],
        scratch_shapes=[pltpu.VMEM((gps, gw, ns), F32)],
        compiler_params=_params("parallel", "arbitrary"),
    )(xa, xa, xa, dtp, par)


def _ssd_scan_bwd(dy, xa, dtp, par, s_in, n_groups, *, name):
    t = xa.shape[0]
    ell, hd, hpg, ns, gps = SSD_CHUNK, SSD_HEAD_DIM, SSD_HEADS_PER_GROUP, SSD_STATE, SSD_GROUPS_PER_STEP
    gw = hpg * hd
    nc = t // ell
    b_blk0, c_blk0 = n_groups * gw // (ns * gps), (n_groups * gw // ns + n_groups) // gps
    nt_dims = (((1,), (1,)), ((), ()))
    tn_dims = (((0,), (0,)), ((), ()))

    def body(dy_ref, xs_ref, b_ref, c_ref, dtp_ref, par_ref, sin_ref,
             dxs_ref, db_ref, dc_ref, ddtp_ref, dpar_ref, dst):
        @pl.when(pl.program_id(1) == 0)
        def _():
            dst[...] = jnp.zeros_like(dst)
            dpar_ref[...] = jnp.zeros_like(dpar_ref)

        dtg, a_g, acum, acum_t, row, col = _ssd_chunk_common(dtp_ref, par_ref)
        lane = lax.broadcasted_iota(jnp.int32, (1, gps * LANES), 1)
        dacum = jnp.zeros((ell, gps * LANES), F32)
        xsum = jnp.zeros((ell, gps * LANES), F32)
        dsum = jnp.zeros((1, gps * LANES), F32)
        for gi, hh in [(gi, hh) for gi in range(gps) for hh in range(hpg)]:
            if hh == 0:
                bb = b_ref[:, gi * ns:(gi + 1) * ns].astype(BF16)
                cc = c_ref[:, gi * ns:(gi + 1) * ns].astype(BF16)
                cb = lax.dot_general(cc, bb, nt_dims, preferred_element_type=F32)
                cb_t = lax.dot_general(bb, cc, nt_dims, preferred_element_type=F32)
                dcb = jnp.zeros((ell, ell), F32)
                dcb_t = jnp.zeros((ell, ell), F32)
                s_all, ds_all = sin_ref[gi], dst[gi]
                c_s_all = lax.dot_general(cc, s_all.astype(BF16), nt_dims, preferred_element_type=F32)
                b_ds_all = lax.dot_general(bb, ds_all.astype(BF16), nt_dims, preferred_element_type=F32)
                s_ds = jnp.sum(s_all * ds_all, axis=1, keepdims=True)
                dy_decayed, x_weighted, keep = [], [], []
            lanes = slice(gi * gw + hh * hd, gi * gw + (hh + 1) * hd)
            head = slice(hh * hd, (hh + 1) * hd)
            hl = gi * LANES + hh
            onehot = (lane == hl).astype(F32)
            col_a, row_a = acum[:, hl:hl + 1], acum_t[hl:hl + 1, :]
            decay = jnp.exp(jnp.where(row >= col, col_a - row_a, NEG_BIG))
            decay_t = jnp.exp(jnp.where(col >= row, row_a - col_a, NEG_BIG))
            e_col = jnp.exp(col_a)
            a_last = acum[ell - 1:ell, hl:hl + 1]
            w = jnp.exp(a_last - col_a)
            e_last = jnp.exp(a_last)
            xs_h, dy_h = xs_ref[:, lanes], dy_ref[:, lanes]
            dt_h = dtg[:, hl:hl + 1]
            xdt = xs_h * dt_h
            xdt_b, dy_b = xdt.astype(BF16), dy_h.astype(BF16)
            dm_decay = lax.dot_general(dy_b, xdt_b, nt_dims, preferred_element_type=F32) * decay
            dm_decay_t = lax.dot_general(xdt_b, dy_b, nt_dims, preferred_element_type=F32) * decay_t
            dcb += dm_decay
            dcb_t += dm_decay_t
            m_t = cb_t * decay_t
            dac = jnp.sum(dm_decay * cb, axis=1, keepdims=True) - jnp.sum(dm_decay_t * cb_t, axis=1, keepdims=True)
            b_ds = b_ds_all[:, head]
            dxdt = jnp.dot(m_t.astype(BF16), dy_b, preferred_element_type=F32) + w * b_ds
            dac += jnp.sum(dy_h * c_s_all[:, head], axis=1, keepdims=True) * e_col
            q = jnp.sum(xdt * b_ds, axis=1, keepdims=True) * w
            dac -= q
            d_last = jnp.sum(q, axis=0, keepdims=True) + e_last * jnp.sum(s_ds[head], axis=0, keepdims=True)
            is_last = lax.broadcasted_iota(jnp.int32, (ell, 1), 0) == ell - 1
            dac += jnp.where(is_last, d_last, 0.0)
            dacum += dac * onehot
            dy_decayed.append((dy_h * e_col).astype(BF16))
            x_weighted.append((xdt * w).astype(BF16))
            keep.append(jnp.broadcast_to(e_last, (hd, 1)))
            dxs_ref[:, lanes] = dxdt * dt_h + dy_h * par_ref[2:3, hl:hl + 1]
            xsum += jnp.sum(dxdt * xs_h, axis=1, keepdims=True) * onehot
            dsum += jnp.sum(jnp.sum(dy_h * xs_h, axis=1, keepdims=True), axis=0, keepdims=True) * onehot
            if hh == hpg - 1:
                group = slice(gi * ns, (gi + 1) * ns)
                dy_all, x_all = jnp.concatenate(dy_decayed, axis=1), jnp.concatenate(x_weighted, axis=1)
                dc_ref[:, group] = (jnp.dot(dy_all, s_all.astype(BF16), preferred_element_type=F32)
                                    + jnp.dot(dcb.astype(BF16), bb, preferred_element_type=F32))
                db_ref[:, group] = (jnp.dot(x_all, ds_all.astype(BF16), preferred_element_type=F32)
                                    + jnp.dot(dcb_t.astype(BF16), cc, preferred_element_type=F32))
                dst[gi] = jnp.concatenate(keep, axis=0) * ds_all + lax.dot_general(
                    dy_all, cc, tn_dims, preferred_element_type=F32)
        dda = _split_dot(dacum, (col >= row).astype(BF16), SSD_CUMSUM_PIECES, left=True)
        ddtp = (xsum + dda * a_g) * _sigmoid(dtp_ref[...] + par_ref[0:1, :])
        ddtp_ref[...] = ddtp
        dpar_ref[0:1, :] += jnp.sum(ddtp, axis=0, keepdims=True)
        dpar_ref[1:2, :] += jnp.sum(dda * dtg, axis=0, keepdims=True) * a_g
        dpar_ref[2:3, :] += dsum

    rev = lambda i: nc - 1 - i
    return pl.pallas_call(
        body, name=name, grid=(n_groups // gps, nc),
        in_specs=[pl.BlockSpec((ell, gps * gw), lambda g, i: (rev(i), g)),
                  pl.BlockSpec((ell, gps * gw), lambda g, i: (rev(i), g)),
                  pl.BlockSpec((ell, gps * ns), lambda g, i: (rev(i), b_blk0 + g)),
                  pl.BlockSpec((ell, gps * ns), lambda g, i: (rev(i), c_blk0 + g)),
                  pl.BlockSpec((ell, gps * LANES), lambda g, i: (rev(i), g)),
                  pl.BlockSpec((SUBLANES, gps * LANES), lambda g, i: (0, g)),
                  pl.BlockSpec((None, gps, gw, ns), lambda g, i: (rev(i), g, 0, 0))],
        out_specs=[pl.BlockSpec((ell, gps * gw), lambda g, i: (rev(i), g)),
                   pl.BlockSpec((ell, gps * ns), lambda g, i: (rev(i), g)),
                   pl.BlockSpec((ell, gps * ns), lambda g, i: (rev(i), g)),
                   pl.BlockSpec((ell, gps * LANES), lambda g, i: (rev(i), g)),
                   pl.BlockSpec((SUBLANES, gps * LANES), lambda g, i: (0, g))],
        out_shape=[jax.ShapeDtypeStruct((t, n_groups * gw), F32), jax.ShapeDtypeStruct((t, n_groups * ns), F32),
                   jax.ShapeDtypeStruct((t, n_groups * ns), F32), jax.ShapeDtypeStruct((t, n_groups * LANES), F32),
                   jax.ShapeDtypeStruct((SUBLANES, n_groups * LANES), F32)],
        scratch_shapes=[pltpu.VMEM((gps, gw, ns), F32)],
        compiler_params=_params("parallel", "arbitrary"),
    )(dy, xa, xa, xa, dtp, par, s_in)


def _ssd_gate_fwd(y, xa, zx, d_rep, out_norm, *, name):
    t, di = y.shape
    gw = SSD_HEADS_PER_GROUP * SSD_HEAD_DIM
    tm = _tile(t, 512, SUBLANES)

    def body(y_ref, xs_ref, z_ref, d_ref, n_ref, o_ref):
        zv = z_ref[...]
        gt = (y_ref[...] + d_ref[...] * xs_ref[...]) * (zv * _sigmoid(zv))
        r = lax.rsqrt(jnp.mean(gt * gt, axis=-1, keepdims=True) + NORM_EPS)
        o_ref[...] = (gt * r * n_ref[...]).astype(BF16)

    blk = pl.BlockSpec((tm, gw), lambda i, g: (i, g))
    vec = pl.BlockSpec((1, gw), lambda i, g: (0, g))
    return pl.pallas_call(
        body, name=name, grid=(t // tm, di // gw),
        in_specs=[blk, blk, blk, vec, vec], out_specs=blk,
        out_shape=jax.ShapeDtypeStruct((t, di), BF16),
        compiler_params=_params("parallel", "parallel"),
    )(y, xa, zx, d_rep, out_norm)


def _ssd_gate_bwd(dgn, y, xa, zx, d_rep, out_norm, *, name):
    t, di = y.shape
    gw = SSD_HEADS_PER_GROUP * SSD_HEAD_DIM
    tm = _tile(t, 512, SUBLANES)

    def body(dg_ref, y_ref, xs_ref, z_ref, d_ref, n_ref, dy_ref, dz_ref, dn_ref):
        @pl.when(pl.program_id(1) == 0)
        def _():
            dn_ref[...] = jnp.zeros_like(dn_ref)

        zv = z_ref[...]
        sg = _sigmoid(zv)
        sz = zv * sg
        y2 = y_ref[...] + d_ref[...] * xs_ref[...]
        gt = y2 * sz
        r = lax.rsqrt(jnp.mean(gt * gt, axis=-1, keepdims=True) + NORM_EPS)
        ghat = gt * r
        dgv = dg_ref[...]
        dn_ref[...] += jnp.sum(dgv * ghat, axis=0, keepdims=True)
        u = dgv * n_ref[...]
        dgt = r * (u - ghat * jnp.mean(u * ghat, axis=-1, keepdims=True))
        dy_ref[...] = dgt * sz
        dz_ref[...] = (dgt * y2 * (sg * (1.0 + zv * (1.0 - sg)))).astype(BF16)

    blk = pl.BlockSpec((tm, gw), lambda g, i: (i, g))
    vec = pl.BlockSpec((1, gw), lambda g, i: (0, g))
    return pl.pallas_call(
        body, name=name, grid=(di // gw, t // tm),
        in_specs=[blk, blk, blk, blk, vec, vec], out_specs=[blk, blk, vec],
        out_shape=[jax.ShapeDtypeStruct((t, di), F32), jax.ShapeDtypeStruct((t, di), BF16),
                   jax.ShapeDtypeStruct((1, di), F32)],
        compiler_params=_params("parallel", "arbitrary"),
    )(dgn, y, xa, zx, d_rep, out_norm)


def _ssd_mixer_fwd(x, gain, w_zx_t, w_dt_t, conv_w, conv_b, par, d_rep, out_norm, w_out, tag):
    di = w_out.shape[0]
    n_groups = di // (SSD_HEADS_PER_GROUP * SSD_HEAD_DIM)
    h = _rms_fwd(x, gain, name=f"ssd_norm_{tag}")
    zx = _matmul(h, w_zx_t, "nt", name=f"ssd_in_{tag}")
    dtp = _matmul(h, w_dt_t, "nt", name=f"ssd_dt_{tag}")
    xa = _ssd_conv_fwd(zx, conv_w, conv_b, di, name=f"ssd_conv_{tag}")
    y, s_in = _ssd_scan_fwd(xa, dtp, par, n_groups, name=f"ssd_scan_{tag}")
    gn = _ssd_gate_fwd(y, xa, zx, d_rep, out_norm, name=f"ssd_gate_{tag}")
    x_new = _matmul(gn, w_out, "nn", resid=x, name=f"ssd_out_{tag}")
    return x_new, (x, h, zx, dtp, xa, y, s_in, gn)


def _ssd_mixer_bwd(dx, saved, gain, w_zx_t, w_dt_t, conv_w, conv_b, par, d_rep, out_norm, w_out, grads, row_out,
                   tag):
    x, h, zx, dtp, xa, y, s_in, gn = saved
    di = w_out.shape[0]
    n_groups = di // (SSD_HEADS_PER_GROUP * SSD_HEAD_DIM)
    dgn = _matmul(dx, w_out, "nt", name=f"ssd_dgn_{tag}")
    grads = _matmul_tn_into(grads, gn, dx, row_out, name=f"ssd_dwout_{tag}")
    dy2, dz, dnorm = _ssd_gate_bwd(dgn, y, xa, zx, d_rep, out_norm, name=f"ssd_dgate_{tag}")
    dxs, db, dc, ddtp, dpar = _ssd_scan_bwd(dy2, xa, dtp, par, s_in, n_groups, name=f"ssd_dscan_{tag}")
    dxbc, dconv_w, dconv_b = _ssd_conv_bwd([dxs, db, dc], zx, conv_w, conv_b, di, name=f"ssd_dconv_{tag}")
    dzx = jnp.concatenate([dz, dxbc], axis=1)
    dw_zx_t = _matmul_tn(dzx, h, name=f"ssd_dwin_{tag}")
    dw_dt_t = _matmul_tn(ddtp, h, name=f"ssd_dwdt_{tag}")
    dh = _matmul(dzx, w_zx_t, "nn", name=f"ssd_dh_{tag}")
    dh = _matmul(ddtp, w_dt_t, "nn", resid=dh, name=f"ssd_dhdt_{tag}")
    dx_in, dgain = _rms_bwd(x, gain, dh, dx, name=f"ssd_dnorm_{tag}")
    return dx_in, grads, dw_zx_t, dw_dt_t, dconv_w, dconv_b, dpar, dnorm, dgain


HEAD_SUM_PIECES = 2


def _head_sums(v):
    row = lax.broadcasted_iota(jnp.int32, (LANES, LANES), 0)
    col = lax.broadcasted_iota(jnp.int32, (LANES, LANES), 1)
    same_head = (row // SB_HEAD_DIM == col // SB_HEAD_DIM).astype(BF16)
    return _split_dot(v, same_head, HEAD_SUM_PIECES)


def _sb_qk_norm_fwd(qkv, gains, *, name):
    ns, t, _ = qkv.shape
    per = ns // 3
    tm = _tile(t, 1024, SUBLANES)
    inv_sqrt_d = 1.0 / math.sqrt(SB_HEAD_DIM)

    def body(x_ref, g_ref, o_ref):
        kind = pl.program_id(0) // per
        xv = x_ref[...]

        @pl.when(kind == 2)
        def _():
            o_ref[...] = xv.astype(BF16)

        @pl.when(kind < 2)
        def _():
            ms = _head_sums(xv * xv) * (1.0 / SB_HEAD_DIM)
            y = xv * lax.rsqrt(ms + NORM_EPS) * g_ref[pl.ds(kind, 1), :]
            o_ref[...] = (y * jnp.where(kind == 0, inv_sqrt_d, 1.0)).astype(BF16)

    blk = pl.BlockSpec((None, tm, LANES), lambda s, i: (s, i, 0))
    return pl.pallas_call(
        body, name=name, grid=(ns, t // tm),
        in_specs=[blk, pl.BlockSpec((SUBLANES, LANES), lambda s, i: (0, 0))], out_specs=blk,
        out_shape=jax.ShapeDtypeStruct((ns, t, LANES), BF16),
        compiler_params=_params("parallel", "parallel"),
    )(qkv, gains)


def _sb_qk_norm_bwd(dq, dk, dv, qkv, gains, *, name):
    ns, t, _ = qkv.shape
    per = ns // 3
    tm = _tile(t, 1024, SUBLANES)
    inv_sqrt_d = 1.0 / math.sqrt(SB_HEAD_DIM)

    def body(dq_ref, dk_ref, dv_ref, x_ref, g_ref, o_ref, dg_ref):
        s = pl.program_id(0)
        kind = s // per

        @pl.when((s == 0) & (pl.program_id(1) == 0))
        def _():
            dg_ref[...] = jnp.zeros_like(dg_ref)

        @pl.when(kind == 2)
        def _():
            o_ref[...] = dv_ref[...].astype(BF16)

        @pl.when(kind < 2)
        def _():
            xv = x_ref[...]
            dy = jnp.where(kind == 0, dq_ref[...] * inv_sqrt_d, dk_ref[...])
            r = lax.rsqrt(_head_sums(xv * xv) * (1.0 / SB_HEAD_DIM) + NORM_EPS)
            xhat = xv * r
            u = dy * g_ref[pl.ds(kind, 1), :]
            o_ref[...] = (r * (u - xhat * _head_sums(u * xhat) * (1.0 / SB_HEAD_DIM))).astype(BF16)
            dg_ref[pl.ds(kind, 1), :] += jnp.sum(dy * xhat, axis=0, keepdims=True)

    def grad_blk(kind):
        def index(s, i):
            mine = (s >= kind * per) & (s < (kind + 1) * per)
            return jnp.where(mine, s - kind * per, 0), jnp.where(mine, i, 0), 0
        return pl.BlockSpec((None, tm, LANES), index)

    blk = pl.BlockSpec((None, tm, LANES), lambda s, i: (s, i, 0))
    vec = pl.BlockSpec((SUBLANES, LANES), lambda s, i: (0, 0))
    return pl.pallas_call(
        body, name=name, grid=(ns, t // tm),
        in_specs=[grad_blk(0), grad_blk(1), grad_blk(2), blk, vec], out_specs=[blk, vec],
        out_shape=[jax.ShapeDtypeStruct((ns, t, LANES), BF16), jax.ShapeDtypeStruct((SUBLANES, LANES), F32)],
        compiler_params=_params("arbitrary", "arbitrary"),
    )(dq, dk, dv, qkv, gains)


def _split_dot(v, ones_mat, pieces, left=False):
    total, rest = None, v
    for p in range(pieces):
        part = rest.astype(BF16)
        if p + 1 < pieces:
            rest = rest - part.astype(F32)
        d = (jnp.dot(ones_mat, part, preferred_element_type=F32) if left
             else jnp.dot(part, ones_mat, preferred_element_type=F32))
        total = d if total is None else total + d
    return total


LOGIT_SUM_PIECES = 2
GRAD_SUM_PIECES = 1
LOG_WEIGHT_UNDERFLOW = -105.0


def _sb_attn_fwd(qkv_n, n_heads, *, name):
    ns, t, _ = qkv_n.shape
    per = ns // 3
    bq, blk, hd = SB_QUERY_BLOCK, SB_BLOCK, SB_HEAD_DIM
    nq, n_diag = t // bq, bq // blk

    def body(q_ref, k_ref, v_ref, o_ref, walk_ref):
        i = pl.program_id(1)
        row = lax.broadcasted_iota(jnp.int32, (blk, blk), 0)
        col = lax.broadcasted_iota(jnp.int32, (blk, blk), 1)
        later_keys = (row > col).astype(BF16)
        qry = lax.broadcasted_iota(jnp.int32, (bq, blk), 0)
        key = lax.broadcasted_iota(jnp.int32, (bq, blk), 1)

        def tile(kb, carry, key_offset):
            out = []
            start = pl.multiple_of(kb * blk, blk)
            for hf in range(2):
                lanes = slice(hf * hd, (hf + 1) * hd)
                run, acc = carry[hf]
                z = lax.dot_general(q_ref[:, lanes], k_ref[pl.ds(start, blk), lanes], NT_DIMS,
                                    preferred_element_type=F32)
                sp = _softplus(z)
                lm = -sp if key_offset is None else jnp.where(key + key_offset < qry, -sp, 0.0)
                after = _split_dot(lm, later_keys, LOGIT_SUM_PIECES) + run
                a = jnp.exp(z - sp + after)
                if key_offset is not None:
                    a = jnp.where(key + key_offset < qry, a, 0.0)
                acc = acc + jnp.dot(a.astype(BF16), v_ref[pl.ds(start, blk), lanes], preferred_element_type=F32)
                out.append((run + jnp.sum(lm, axis=1, keepdims=True), acc))
            return tuple(out)

        def live(carry):
            return jnp.max(jnp.maximum(carry[0][0], carry[1][0])) > LOG_WEIGHT_UNDERFLOW

        def step(state):
            s, _, carry = state
            carry = tile(n_diag * i - 1 - s, carry, None)
            return s + 1, live(carry), carry

        carry = tuple((jnp.zeros((bq, 1), F32), jnp.zeros((bq, hd), F32)) for _ in range(2))
        for j in reversed(range(n_diag)):
            carry = tile(n_diag * i + j, carry, j * blk)
        walked, _, carry = lax.while_loop(lambda st: (st[0] < n_diag * i) & st[1], step,
                                          (jnp.int32(0), live(carry), carry))
        o_ref[...] = jnp.concatenate([carry[0][1], carry[1][1]], axis=1)
        lane = lax.broadcasted_iota(jnp.int32, (1, LANES), 1)
        walk_ref[...] = jnp.where(lane < WALK_LANES, carry[0][0],
                                  jnp.where(lane < 2 * WALK_LANES, carry[1][0], walked.astype(F32)))

    q_blk = pl.BlockSpec((None, bq, LANES), lambda p, i: (p, i, 0))
    return pl.pallas_call(
        body, name=name, grid=(per, nq),
        in_specs=[q_blk, pl.BlockSpec((None, t, LANES), lambda p, i: (per + p, 0, 0)),
                  pl.BlockSpec((None, t, LANES), lambda p, i: (2 * per + p, 0, 0))],
        out_specs=[pl.BlockSpec((bq, LANES), lambda p, i: (i, p)), q_blk],
        out_shape=[jax.ShapeDtypeStruct((t, n_heads * hd), F32), jax.ShapeDtypeStruct((per, t, LANES), F32)],
        compiler_params=_params("parallel", "arbitrary"),
    )(qkv_n, qkv_n, qkv_n)


WALK_LANES = 43


def _sb_attn_bwd(do, walk, qkv_n, *, name):
    ns, t, _ = qkv_n.shape
    per = ns // 3
    bq, blk, hd = SB_QUERY_BLOCK, SB_BLOCK, SB_HEAD_DIM
    nq, n_diag = t // bq, bq // blk
    nt_dims = (((1,), (1,)), ((), ()))
    tn_dims = (((0,), (0,)), ((), ()))

    def body(q_ref, k_ref, v_ref, do_ref, walk_ref, dq_ref, dk_ref, dv_ref):
        i = pl.program_id(1)

        @pl.when(i == 0)
        def _():
            dk_ref[...] = jnp.zeros_like(dk_ref)
            dv_ref[...] = jnp.zeros_like(dv_ref)

        walk_t = walk_ref[...].T
        tots = [walk_t[hf * WALK_LANES:hf * WALK_LANES + 1, :] for hf in range(2)]
        reached = jnp.clip(jnp.max(walk_t[2 * WALK_LANES:2 * WALK_LANES + 1, :]).astype(jnp.int32), 0, n_diag * i)

        row = lax.broadcasted_iota(jnp.int32, (blk, blk), 0)
        col = lax.broadcasted_iota(jnp.int32, (blk, blk), 1)
        later_keys = (col > row).astype(BF16)
        earlier_keys = (col < row).astype(BF16)
        key = lax.broadcasted_iota(jnp.int32, (blk, bq), 0)
        qry = lax.broadcasted_iota(jnp.int32, (blk, bq), 1)
        halves = [slice(hf * hd, (hf + 1) * hd) for hf in range(2)]
        q_hs = [q_ref[:, lanes] for lanes in halves]
        do_bs = [do_ref[:, lanes].astype(BF16) for lanes in halves]

        def scores(kb, hf, key_offset):
            k_blk = k_ref[pl.ds(pl.multiple_of(kb * blk, blk), blk), halves[hf]]
            z = lax.dot_general(k_blk, q_hs[hf], nt_dims, preferred_element_type=F32)
            sp = _softplus(z)
            return k_blk, z, sp, (-sp if key_offset is None else jnp.where(key + key_offset < qry, -sp, 0.0))

        def tile(kb, carry, key_offset):
            out = []
            start = pl.multiple_of(kb * blk, blk)
            for hf, lanes in enumerate(halves):
                seen, gsum, dq = carry[hf]
                q_h, do_b = q_hs[hf], do_bs[hf]
                k_blk, z, sp, lm = scores(kb, hf, key_offset)
                blk_tot = jnp.sum(lm, axis=0, keepdims=True)
                after = _split_dot(lm, later_keys, LOGIT_SUM_PIECES, left=True) + (tots[hf] - seen - blk_tot)
                a = jnp.exp(z - sp + after)
                if key_offset is not None:
                    a = jnp.where(key + key_offset < qry, a, 0.0)
                da = lax.dot_general(v_ref[pl.ds(start, blk), lanes], do_b, nt_dims, preferred_element_type=F32)
                g = da * a
                before = _split_dot(g, earlier_keys, GRAD_SUM_PIECES, left=True) + gsum
                omb = jnp.exp(-sp)
                dz = g * omb - (1.0 - omb) * before
                if key_offset is not None:
                    dz = jnp.where(key + key_offset < qry, dz, 0.0)
                dz_b = dz.astype(BF16)
                dk_ref[pl.ds(start, blk), lanes] += jnp.dot(dz_b, q_h, preferred_element_type=F32)
                dv_ref[pl.ds(start, blk), lanes] += jnp.dot(a.astype(BF16), do_b, preferred_element_type=F32)
                dq = dq + lax.dot_general(dz_b, k_blk, tn_dims, preferred_element_type=F32)
                out.append((seen + blk_tot, gsum + jnp.sum(g, axis=0, keepdims=True), dq))
            return tuple(out)

        init = tuple((jnp.zeros((1, bq), F32), jnp.zeros((1, bq), F32), jnp.zeros((bq, hd), F32))
                     for _ in range(2))
        carry = lax.fori_loop(n_diag * i - reached, n_diag * i, lambda kb, c: tile(kb, c, None), init)
        for j in range(n_diag):
            carry = tile(n_diag * i + j, carry, j * blk)
        dq_ref[...] = jnp.concatenate([carry[0][2], carry[1][2]], axis=1)

    full = lambda off: pl.BlockSpec((None, t, LANES), lambda p, i: (off + p, 0, 0))
    q_blk = pl.BlockSpec((None, bq, LANES), lambda p, i: (p, i, 0))
    slab = jax.ShapeDtypeStruct((per, t, LANES), F32)
    return pl.pallas_call(
        body, name=name, grid=(per, nq),
        in_specs=[q_blk, full(per), full(2 * per), pl.BlockSpec((bq, LANES), lambda p, i: (i, p)), q_blk],
        out_specs=[q_blk, full(0), full(0)],
        out_shape=[slab, slab, slab],
        compiler_params=_params("parallel", "arbitrary"),
    )(qkv_n, qkv_n, qkv_n, do, walk)


def _sb_mixer_fwd(x, gain, w_qkv_t, qk_gains, w_out, tag):
    n_heads = w_out.shape[0] // SB_HEAD_DIM
    h = _rms_fwd(x, gain, name=f"sb_norm_{tag}")
    qkv = _matmul(h, w_qkv_t, "nt", out_slabs=True, tn_cap=512, name=f"sb_qkv_{tag}")
    qkv_n = _sb_qk_norm_fwd(qkv, qk_gains, name=f"sb_qknorm_{tag}")
    o, walk = _sb_attn_fwd(qkv_n, n_heads, name=f"sb_attn_{tag}")
    x_new = _matmul(o, w_out, "nn", resid=x, name=f"sb_out_{tag}")
    return x_new, (x, h, qkv, qkv_n, o, walk)


def _sb_mixer_bwd(dx, saved, gain, w_qkv_t, qk_gains, w_out, grads, row_qkv, row_out, tag):
    x, h, qkv, qkv_n, o, walk = saved
    do = _matmul(dx, w_out, "nt", name=f"sb_do_{tag}")
    grads = _matmul_tn_into(grads, o, dx, row_out, name=f"sb_dwout_{tag}")
    dq, dk, dv = _sb_attn_bwd(do, walk, qkv_n, name=f"sb_dattn_{tag}")
    dqkv, dqk_gains = _sb_qk_norm_bwd(dq, dk, dv, qkv, qk_gains, name=f"sb_dqknorm_{tag}")
    grads = _matmul_tn_into(grads, dqkv, h, row_qkv, a_slabs=True, name=f"sb_dwqkv_{tag}")
    dh = _matmul(dqkv, w_qkv_t, "nn", a_slabs=True, name=f"sb_dh_{tag}")
    dx_in, dgain = _rms_bwd(x, gain, dh, dx, name=f"sb_dnorm_{tag}")
    return dx_in, grads, dqk_gains, dgain


MESH = pl.DeviceIdType.MESH


def _position():
    return lax.axis_index("x"), lax.axis_index("y"), lax.axis_index("c")


def _all_gather(shard, *, name):
    rows, n = shard.shape
    space = pltpu.VMEM

    def body(x_ref, out_ref, send_sems, recv_sems, local_sem):
        x, y, c = _position()
        me, sibling = (x, y, c), (x, y, 1 - c)
        chips = [(1 - x, y), (x, 1 - y), (1 - x, 1 - y)]

        def block(px, py, pc):
            return out_ref.at[4 * px + 2 * py + pc]

        def copy(k, blk, to, src=None):
            return pltpu.make_async_remote_copy(
                src_ref=block(*blk) if src is None else src, dst_ref=block(*blk),
                send_sem=send_sems.at[k], recv_sem=recv_sems.at[k], device_id=to, device_id_type=MESH)

        mine = pltpu.make_async_copy(x_ref, block(*me), local_sem)
        mine.start()
        first = [copy(0, me, sibling, src=x_ref)]
        first += [copy(1 + j, me, (*chip, c), src=x_ref) for j, chip in enumerate(chips)]
        for cp in first:
            cp.start()
        passed = [copy(4 + j, (*chip, c), sibling) for j, chip in enumerate(chips)]
        for j, chip in enumerate(chips):
            copy(1 + j, (*chip, c), me).wait_recv()
            passed[j].start()
        copy(0, sibling, me).wait_recv()
        for j, chip in enumerate(chips):
            copy(4 + j, (*chip, 1 - c), me).wait_recv()
        for cp in first + passed:
            cp.wait_send()
        mine.wait()

    return pl.pallas_call(
        body, name=name,
        out_shape=jax.ShapeDtypeStruct((N_DEV, rows, n), shard.dtype),
        in_specs=[pl.BlockSpec(memory_space=space)], out_specs=pl.BlockSpec(memory_space=space),
        scratch_shapes=[pltpu.SemaphoreType.DMA((7,)), pltpu.SemaphoreType.DMA((7,)), pltpu.SemaphoreType.DMA],
        compiler_params=pltpu.CompilerParams(vmem_limit_bytes=V7X_VMEM_LIMIT_BYTES),
    )(shard)


def _all_gather_forwarding(shard, *, name):
    rows, n = shard.shape
    half = rows // 2
    assert rows % (4 * SUBLANES) == 0

    def body(x_ref, out_ref, send_sems, recv_sems, local_sem):
        x, y, c = _position()
        me, sibling = (x, y, c), (x, y, 1 - c)
        x_nbr, y_nbr, diag = (1 - x, y), (x, 1 - y), (1 - x, 1 - y)
        lower, upper = pl.ds(0, half), pl.ds(half, half)

        def block(px, py, pc, part=None):
            ref = out_ref.at[4 * px + 2 * py + pc]
            return ref if part is None else ref.at[part]

        def copy(k, blk, to, src=None, part=None):
            return pltpu.make_async_remote_copy(
                src_ref=block(*blk, part) if src is None else src, dst_ref=block(*blk, part),
                send_sem=send_sems.at[k], recv_sem=recv_sems.at[k], device_id=to, device_id_type=MESH)

        mine = pltpu.make_async_copy(x_ref, block(*me), local_sem)
        mine.start()
        sent = [copy(0, me, sibling, src=x_ref), copy(1, me, (*x_nbr, c), src=x_ref),
                copy(2, me, (*y_nbr, c), src=x_ref)]
        for cp in sent:
            cp.start()
        copy(1, (*x_nbr, c), me).wait_recv()
        onward = [copy(3, (*x_nbr, c), (*y_nbr, c), part=lower), copy(5, (*x_nbr, c), sibling)]
        for cp in onward:
            cp.start()
        copy(2, (*y_nbr, c), me).wait_recv()
        onward += [copy(4, (*y_nbr, c), (*x_nbr, c), part=upper), copy(6, (*y_nbr, c), sibling)]
        for cp in onward[2:]:
            cp.start()
        copy(3, (*diag, c), me, part=lower).wait_recv()
        copy(4, (*diag, c), me, part=upper).wait_recv()
        onward.append(copy(7, (*diag, c), sibling))
        onward[-1].start()
        sent += onward
        copy(0, sibling, me).wait_recv()
        for k, chip in ((5, x_nbr), (6, y_nbr), (7, diag)):
            copy(k, (*chip, 1 - c), me).wait_recv()
        for cp in sent:
            cp.wait_send()
        mine.wait()

    hbm = pl.BlockSpec(memory_space=pltpu.HBM)
    return pl.pallas_call(
        body, name=name,
        out_shape=jax.ShapeDtypeStruct((N_DEV, rows, n), shard.dtype), in_specs=[hbm], out_specs=hbm,
        scratch_shapes=[pltpu.SemaphoreType.DMA((8,)), pltpu.SemaphoreType.DMA((8,)), pltpu.SemaphoreType.DMA],
    )(shard)


def _exchange_sibling(parts, *, name):
    _, nchip, rows, n = parts.shape

    def body(p_ref, recv_ref, send_sem, recv_sem):
        x, y, c = _position()
        cp = pltpu.make_async_remote_copy(src_ref=p_ref.at[1 - c], dst_ref=recv_ref, send_sem=send_sem,
                                          recv_sem=recv_sem, device_id=(x, y, 1 - c), device_id_type=MESH)
        cp.start()
        cp.wait()

    return pl.pallas_call(
        body, name=name,
        out_shape=jax.ShapeDtypeStruct((nchip, rows, n), parts.dtype),
        in_specs=[pl.BlockSpec(memory_space=pltpu.HBM)], out_specs=pl.BlockSpec(memory_space=pltpu.HBM),
        scratch_shapes=[pltpu.SemaphoreType.DMA, pltpu.SemaphoreType.DMA],
    )(parts)


def _exchange_chips(chip_sums, *, name):
    _, rows, n = chip_sums.shape

    def body(s_ref, recv_ref, send_sems, recv_sems):
        x, y, c = _position()
        chips = [(1 - x, y), (x, 1 - y), (1 - x, 1 - y)]
        copies = [pltpu.make_async_remote_copy(
            src_ref=s_ref.at[2 * cx + cy], dst_ref=recv_ref.at[j], send_sem=send_sems.at[j],
            recv_sem=recv_sems.at[j], device_id=(cx, cy, c), device_id_type=MESH)
            for j, (cx, cy) in enumerate(chips)]
        for cp in copies:
            cp.start()
        for cp in copies:
            cp.wait()

    return pl.pallas_call(
        body, name=name,
        out_shape=jax.ShapeDtypeStruct((3, rows, n), chip_sums.dtype),
        in_specs=[pl.BlockSpec(memory_space=pltpu.HBM)], out_specs=pl.BlockSpec(memory_space=pltpu.HBM),
        scratch_shapes=[pltpu.SemaphoreType.DMA((3,)), pltpu.SemaphoreType.DMA((3,))],
    )(chip_sums)


def _add_pairs(parts, recv, c_and_chip, *, name):
    _, nchip, rows, n = parts.shape
    tr = _tile(rows, 512, SUBLANES)

    def body(pos_ref, a_ref, b_ref, own_ref, wire_ref):
        s = a_ref[...] + b_ref[...]
        wire_ref[...] = s.astype(WIRE_DTYPE)

        @pl.when(pl.program_id(1) == pos_ref[1])
        def _():
            own_ref[...] = s

    return pl.pallas_call(
        body, name=name,
        grid_spec=pltpu.PrefetchScalarGridSpec(
            num_scalar_prefetch=1, grid=(rows // tr, nchip),
            in_specs=[pl.BlockSpec((None, None, tr, n), lambda i, k, pos: (pos[0], k, i, 0)),
                      pl.BlockSpec((None, tr, n), lambda i, k, pos: (k, i, 0))],
            out_specs=[pl.BlockSpec((tr, n), lambda i, k, pos: (i, 0)),
                       pl.BlockSpec((None, tr, n), lambda i, k, pos: (k, i, 0))]),
        out_shape=[jax.ShapeDtypeStruct((rows, n), parts.dtype),
                   jax.ShapeDtypeStruct((nchip, rows, n), WIRE_DTYPE)],
        compiler_params=_params("parallel", "arbitrary"),
    )(c_and_chip, parts, recv)


def _adamw_math(w, g, m, v):
    m = ADAM_B1 * m + (1.0 - ADAM_B1) * g
    v = ADAM_B2 * v + (1.0 - ADAM_B2) * (g * g)
    m_hat = m / (1.0 - ADAM_B1 ** ADAM_STEP)
    v_hat = v / (1.0 - ADAM_B2 ** ADAM_STEP)
    delta = -ADAM_LR * (m_hat / (jnp.sqrt(v_hat) + ADAM_EPS) + ADAM_WD * w)
    return delta, m, v


def _adamw_sharded(own_sum, recv, w, m, v, *, name):
    rows, n = w.shape
    tr = _tile(rows, 256, SUBLANES)

    def body(s_ref, r_ref, w_ref, m_ref, v_ref, g_out, d_out, m_out, v_out):
        g = ((s_ref[...] + r_ref[0].astype(F32)) + r_ref[1].astype(F32)) + r_ref[2].astype(F32)
        delta, m_new, v_new = _adamw_math(w_ref[...], g, m_ref[...], v_ref[...])
        g_out[...] = g
        d_out[...] = delta
        m_out[...] = m_new
        v_out[...] = v_new

    blk = pl.BlockSpec((tr, n), lambda i: (i, 0))
    out = jax.ShapeDtypeStruct((rows, n), F32)
    return pl.pallas_call(
        body, name=name, grid=(rows // tr,),
        in_specs=[blk, pl.BlockSpec((3, tr, n), lambda i: (0, i, 0)), blk, blk, blk],
        out_specs=[blk, blk, blk, blk], out_shape=[out, out, out, out],
        compiler_params=_params("parallel"),
    )(own_sum, recv, w, m, v)


SMALL_ROWS = 40
ROW_MIX_NORM, ROW_FFN_NORM, ROW_CONV_B, ROW_OUT_NORM, ROW_POOL_SCALE, ROW_CONV_W = 0, 4, 8, 12, 14, 16
ROW_SSD_VEC, ROW_QK_GAIN, ROW_LOSS = 32, 33, 34


def _adamw_small(gathered, w, m, v, *, name):
    _, rows, n = gathered.shape

    def body(a_ref, w_ref, m_ref, v_ref, g_out, d_out, m_out, v_out):
        g = a_ref[0]
        for d in range(1, N_DEV):
            g = g + a_ref[d]
        row = lax.broadcasted_iota(jnp.int32, (rows, 1), 0)
        g = jnp.where(row == ROW_QK_GAIN, g + pltpu.roll(g, SB_HEAD_DIM, 1), g)
        g = jnp.where(row == ROW_LOSS, jnp.sum(g, axis=1, keepdims=True), g)
        g_out[...] = g
        delta, m_new, v_new = _adamw_math(w_ref[...], g, m_ref[...], v_ref[...])
        d_out[...] = delta
        m_out[...] = m_new
        v_out[...] = v_new

    out = jax.ShapeDtypeStruct((rows, n), F32)
    return pl.pallas_call(body, name=name, out_shape=[out, out, out, out])(gathered, w, m, v)


BIG_WEIGHTS = ("ffn_gate", "ffn_up", "ffn_down", "sb_qkv", "ssd_out", "pool_in", "sb_out", "pool_group", "ssd_in")
COLUMN_SHARDED = ("ssd_in", "sb_qkv", "ffn_gate", "ffn_up")
ROW_PAD = 512
WIRE_DTYPE = jnp.bfloat16


def _to_rows(name, shard, d):
    if name in COLUMN_SHARDED:
        shard = jnp.swapaxes(shard, -1, -2)
    return shard.reshape(-1, d)


def _from_rows(name, rows, shard_shape):
    if name in COLUMN_SHARDED:
        lead, k, n = shard_shape
        return jnp.swapaxes(rows.reshape(lead, n, k), -1, -2)
    return rows.reshape(shard_shape)


def _pad_rows(a, total):
    return jnp.pad(a, ((0, total - a.shape[0]),) + ((0, 0),) * (a.ndim - 1))


def _exact_bf16_rows(v, d):
    words = lax.bitcast_convert_type(v.reshape(-1), WIRE_DTYPE).reshape(-1)
    return _pad_rows(words, -(-words.shape[0] // d) * d).reshape(-1, d)


def _exact_f32(rows, count):
    words = rows.reshape(rows.shape[0], -1)[:, :2 * count].reshape(rows.shape[0], count, 2)
    return lax.bitcast_convert_type(words, F32)


def _device_blocks(full, d):
    return full.reshape(N_DEV, -1, d)


def kernel(x, mix_norm, pool_in, pool_group, pool_scale, ssd_in, ssd_conv_w, ssd_conv_b, ssd_dt_bias, ssd_a_log, ssd_d, ssd_out_norm, ssd_out, sb_qkv, sb_q_norm, sb_k_norm, sb_out, ffn_norm, ffn_gate, ffn_up, ffn_down, loss_target, m_mix_norm, m_pool_in, m_pool_group, m_pool_scale, m_ssd_in, m_ssd_conv_w, m_ssd_conv_b, m_ssd_dt_bias, m_ssd_a_log, m_ssd_d, m_ssd_out_norm, m_ssd_out, m_sb_qkv, m_sb_q_norm, m_sb_k_norm, m_sb_out, m_ffn_norm, m_ffn_gate, m_ffn_up, m_ffn_down, v_mix_norm, v_pool_in, v_pool_group, v_pool_scale, v_ssd_in, v_ssd_conv_w, v_ssd_conv_b, v_ssd_dt_bias, v_ssd_a_log, v_ssd_d, v_ssd_out_norm, v_ssd_out, v_sb_qkv, v_sb_q_norm, v_sb_k_norm, v_sb_out, v_ffn_norm, v_ffn_gate, v_ffn_up, v_ffn_down):
    weights = dict(mix_norm=mix_norm, pool_in=pool_in, pool_group=pool_group, pool_scale=pool_scale, ssd_in=ssd_in,
                   ssd_conv_w=ssd_conv_w, ssd_conv_b=ssd_conv_b, ssd_dt_bias=ssd_dt_bias, ssd_a_log=ssd_a_log,
                   ssd_d=ssd_d, ssd_out_norm=ssd_out_norm, ssd_out=ssd_out, sb_qkv=sb_qkv, sb_q_norm=sb_q_norm,
                   sb_k_norm=sb_k_norm, sb_out=sb_out, ffn_norm=ffn_norm, ffn_gate=ffn_gate, ffn_up=ffn_up,
                   ffn_down=ffn_down)
    mom1 = dict(mix_norm=m_mix_norm, pool_in=m_pool_in, pool_group=m_pool_group, pool_scale=m_pool_scale,
                ssd_in=m_ssd_in, ssd_conv_w=m_ssd_conv_w, ssd_conv_b=m_ssd_conv_b, ssd_dt_bias=m_ssd_dt_bias,
                ssd_a_log=m_ssd_a_log, ssd_d=m_ssd_d, ssd_out_norm=m_ssd_out_norm, ssd_out=m_ssd_out,
                sb_qkv=m_sb_qkv, sb_q_norm=m_sb_q_norm, sb_k_norm=m_sb_k_norm, sb_out=m_sb_out,
                ffn_norm=m_ffn_norm, ffn_gate=m_ffn_gate, ffn_up=m_ffn_up, ffn_down=m_ffn_down)
    mom2 = dict(mix_norm=v_mix_norm, pool_in=v_pool_in, pool_group=v_pool_group, pool_scale=v_pool_scale,
                ssd_in=v_ssd_in, ssd_conv_w=v_ssd_conv_w, ssd_conv_b=v_ssd_conv_b, ssd_dt_bias=v_ssd_dt_bias,
                ssd_a_log=v_ssd_a_log, ssd_d=v_ssd_d, ssd_out_norm=v_ssd_out_norm, ssd_out=v_ssd_out,
                sb_qkv=v_sb_qkv, sb_q_norm=v_sb_q_norm, sb_k_norm=v_sb_k_norm, sb_out=v_sb_out,
                ffn_norm=v_ffn_norm, ffn_gate=v_ffn_gate, ffn_up=v_ffn_up, ffn_down=v_ffn_down)
    names = list(weights)
    depth, d = mix_norm.shape
    xs, ys, cs = _position()
    dev = 4 * xs + 2 * ys + cs
    chip = 2 * xs + ys

    seg = {}
    row = 0
    for name in BIG_WEIGHTS:
        n_rows = weights[name].size // d
        seg[name] = (row, n_rows)
        row += -(-n_rows // SUBLANES) * SUBLANES
    big_rows = row
    n_scale, n_convw = pool_scale.size, ssd_conv_w.size
    exact = jnp.concatenate([_exact_bf16_rows(pool_scale, d), _exact_bf16_rows(ssd_conv_w, d)], axis=0)
    scale_rows = _exact_bf16_rows(pool_scale, d).shape[0]
    packed_rows = -(-(big_rows + exact.shape[0]) // ROW_PAD) * ROW_PAD

    def pack(tree, dtype):
        ends = [seg[n][0] for n in BIG_WEIGHTS[1:]] + [big_rows]
        return jnp.concatenate([_pad_rows(_to_rows(n, tree[n], d).astype(dtype), end - seg[n][0])
                                for n, end in zip(BIG_WEIGHTS, ends)], axis=0)

    w_wire = _pad_rows(jnp.concatenate([pack(weights, WIRE_DTYPE), exact], axis=0), packed_rows)
    gathered = _all_gather_forwarding(w_wire, name="gather_weights")

    def seg_of(name):
        a, n = seg[name]
        return gathered[:, a:a + n]

    n_pool, n_ssd, n_sb = pool_in.shape[0], ssd_in.shape[0], sb_qkv.shape[0]
    assert n_ssd == 1 and n_sb == 1
    w_pool_in = seg_of("pool_in").reshape(N_DEV, n_pool, -1, d).transpose(1, 0, 2, 3).reshape(n_pool, d, d)
    grp = pool_group.shape
    w_pool_group = seg_of("pool_group").reshape(N_DEV, grp[0], grp[1], grp[2], grp[3]).transpose(1, 2, 0, 3, 4)
    w_pool_group = w_pool_group.reshape(grp[0], grp[1], grp[3], grp[3])
    w_ssd_in_t = seg_of("ssd_in").reshape(-1, d)
    w_ssd_out = seg_of("ssd_out").reshape(-1, d)
    w_sb_qkv_t = seg_of("sb_qkv").reshape(-1, d)
    w_sb_out = seg_of("sb_out").reshape(-1, d)
    hidden = ffn_down.shape[1] * N_DEV
    w_gate_t = seg_of("ffn_gate").reshape(N_DEV, depth, -1, d).transpose(1, 0, 2, 3).reshape(depth, hidden, d)
    w_up_t = seg_of("ffn_up").reshape(N_DEV, depth, -1, d).transpose(1, 0, 2, 3).reshape(depth, hidden, d)
    w_down = seg_of("ffn_down").reshape(N_DEV, depth, -1, d).transpose(1, 0, 2, 3).reshape(depth, hidden, d)
    exact_all = gathered[:, big_rows:big_rows + exact.shape[0]]
    scale_full = _exact_f32(exact_all[:, :scale_rows], n_scale).reshape(N_DEV, n_pool, -1)
    scale_full = scale_full.transpose(1, 0, 2).reshape(n_pool, d)
    convw_full = _exact_f32(exact_all[:, scale_rows:], n_convw).reshape(N_DEV, SSD_CONV, -1)
    convw_full = convw_full.transpose(1, 0, 2).reshape(SSD_CONV, -1)

    d_inner = w_ssd_out.shape[0]
    n_zx = w_ssd_in_t.shape[0] - ssd_dt_bias.shape[1]
    w_zx_t = w_ssd_in_t[:n_zx]
    n_ssd_heads = ssd_dt_bias.shape[1]
    n_ssd_groups = n_ssd_heads // SSD_HEADS_PER_GROUP
    w_dt_t = _ssd_group_pad(w_ssd_in_t[n_zx:].T, n_ssd_groups).T
    par = _pad_rows(_ssd_group_pad(jnp.concatenate([ssd_dt_bias, ssd_a_log, ssd_d], axis=0), n_ssd_groups), SUBLANES)
    d_rep = jnp.repeat(ssd_d[0], SSD_HEAD_DIM)[None]
    qk_gains = jnp.zeros((SUBLANES, LANES), F32).at[0].set(jnp.tile(sb_q_norm[0], 2)).at[1].set(jnp.tile(sb_k_norm[0], 2))

    act = x[0]
    saved = []
    for i in range(depth):
        kind, j = i % 3, i // 3
        gain = mix_norm[i:i + 1]
        if kind == 0:
            act, s = _pool_mixer_fwd(act, gain, w_pool_in[j], w_pool_group[j], scale_full[j:j + 1], f"l{i}")
        elif kind == 1:
            act, s = _ssd_mixer_fwd(act, gain, w_zx_t, w_dt_t, convw_full, ssd_conv_b, par, d_rep, ssd_out_norm,
                                    w_ssd_out, f"l{i}")
        else:
            act, s = _sb_mixer_fwd(act, gain, w_sb_qkv_t, qk_gains, w_sb_out, f"l{i}")
        act, f = _ffn_fwd(act, ffn_norm[i:i + 1], w_gate_t[i], w_up_t[i], w_down[i], f"l{i}")
        saved.append((s, f))
    dact, loss_cols = _loss_head(act, loss_target[0], name="loss_head")

    def layer_row(name, layer):
        return seg[name][0] + layer * (seg[name][1] // weights[name].shape[0])

    grads = jax.ShapeDtypeStruct((N_DEV, packed_rows, d), F32)
    g_mix_norm, g_ffn_norm = [None] * depth, [None] * depth
    g_pool_group, g_pool_scale = [None] * n_pool, [None] * n_pool
    for i in reversed(range(depth)):
        kind, j = i % 3, i // 3
        gain = mix_norm[i:i + 1]
        s, f = saved[i]
        dact, grads, g_ffn_norm[i] = _ffn_bwd(
            dact, f, ffn_norm[i:i + 1], w_gate_t[i], w_up_t[i], w_down[i], grads,
            [layer_row(n, i) for n in ("ffn_gate", "ffn_up", "ffn_down")], f"l{i}")
        if kind == 0:
            dact, grads, g_pool_group[j], g_pool_scale[j], g_mix_norm[i] = _pool_mixer_bwd(
                dact, s, gain, w_pool_in[j], w_pool_group[j], scale_full[j:j + 1], grads, layer_row("pool_in", j),
                f"l{i}")
        elif kind == 1:
            (dact, grads, g_zx_t, g_dt_t, g_conv_w, g_conv_b, g_par, g_out_norm,
             g_mix_norm[i]) = _ssd_mixer_bwd(dact, s, gain, w_zx_t, w_dt_t, convw_full, ssd_conv_b, par, d_rep,
                                             ssd_out_norm, w_ssd_out, grads, layer_row("ssd_out", j), f"l{i}")
        else:
            dact, grads, g_qk_gains, g_mix_norm[i] = _sb_mixer_bwd(
                dact, s, gain, w_sb_qkv_t, qk_gains, w_sb_out, grads, layer_row("sb_qkv", j),
                layer_row("sb_out", j), f"l{i}")
    grad_x = dact[None]

    g_ssd_in = jnp.concatenate([g_zx_t, _ssd_group_unpad(g_dt_t.T, n_ssd_groups).T], axis=0)
    g_group = jnp.concatenate([_device_blocks(gg[k], d) for gg in g_pool_group for k in range(gg.shape[0])], axis=1)
    for name, blocks in (("ssd_in", _device_blocks(g_ssd_in, d)), ("pool_group", g_group)):
        blocks = blocks.reshape(N_DEV // 2, 2, -1, d).swapaxes(0, 1).reshape(N_DEV, -1, d)
        blocks = jnp.pad(blocks, ((0, 0), (0, -blocks.shape[1] % SUBLANES), (0, 0)))
        grads = lax.dynamic_update_slice(grads, blocks, (0, seg[name][0], 0))
    grads = lax.dynamic_update_slice(grads, jnp.zeros((N_DEV, packed_rows - big_rows, d), F32), (0, big_rows, 0))
    parts = grads.reshape(2, N_DEV // 2, packed_rows, d)
    from_sibling = _exchange_sibling(parts, name="reduce_sibling")
    own_sum, chip_sums_wire = _add_pairs(parts, from_sibling, jnp.stack([cs, chip]).astype(jnp.int32),
                                         name="reduce_sibling_add")
    from_chips = _exchange_chips(chip_sums_wire, name="reduce_chips")

    def pack_f32(tree):
        return _pad_rows(pack(tree, F32), packed_rows)

    big_out = _adamw_sharded(own_sum, from_chips, pack_f32(weights), pack_f32(mom1), pack_f32(mom2),
                             name="adamw_sharded")

    def small_pack(mix, ffn, conv_b, out_norm, scale, conv_w, vec, qk, loss=None):
        buf = jnp.zeros((SMALL_ROWS, d), F32)
        buf = buf.at[ROW_MIX_NORM:ROW_MIX_NORM + depth].set(mix).at[ROW_FFN_NORM:ROW_FFN_NORM + depth].set(ffn)
        buf = buf.at[ROW_CONV_B:ROW_CONV_B + conv_b.size // d].set(conv_b.reshape(-1, d))
        buf = buf.at[ROW_OUT_NORM:ROW_OUT_NORM + out_norm.size // d].set(out_norm.reshape(-1, d))
        buf = buf.at[ROW_POOL_SCALE:ROW_POOL_SCALE + n_pool].set(scale)
        buf = buf.at[ROW_CONV_W:ROW_CONV_W + conv_w.size // d].set(conv_w.reshape(-1, d))
        buf = buf.at[ROW_SSD_VEC].set(vec.reshape(-1)).at[ROW_QK_GAIN].set(qk.reshape(-1))
        if loss is not None:
            buf = buf.at[ROW_LOSS].set(loss.reshape(-1))
        return buf

    def small_params(tree):
        scale = lax.dynamic_update_slice(jnp.zeros((n_pool, d), F32), tree["pool_scale"],
                                         (0, dev * tree["pool_scale"].shape[1]))
        conv_w = lax.dynamic_update_slice(jnp.zeros(convw_full.shape, F32), tree["ssd_conv_w"][0],
                                          (0, dev * tree["ssd_conv_w"].shape[2]))
        vec = jnp.zeros((SUBLANES, LANES), F32)
        vec = vec.at[0, :n_ssd_heads].set(tree["ssd_dt_bias"][0]).at[1, :n_ssd_heads].set(tree["ssd_a_log"][0])
        vec = vec.at[2, :n_ssd_heads].set(tree["ssd_d"][0])
        qk = jnp.zeros((SUBLANES, LANES), F32)
        qk = qk.at[0, SB_HEAD_DIM:].set(tree["sb_q_norm"][0]).at[1, SB_HEAD_DIM:].set(tree["sb_k_norm"][0])
        return small_pack(tree["mix_norm"], tree["ffn_norm"], tree["ssd_conv_b"], tree["ssd_out_norm"], scale,
                          conv_w, vec, qk)

    small_partial = small_pack(jnp.concatenate(g_mix_norm, axis=0), jnp.concatenate(g_ffn_norm, axis=0), g_conv_b,
                               g_out_norm, jnp.concatenate(g_pool_scale, axis=0), g_conv_w,
                               jnp.zeros((SUBLANES, LANES), F32).at[:3, :n_ssd_heads].set(
                                   _ssd_group_unpad(g_par[:3], n_ssd_groups)), g_qk_gains,
                               loss_cols)
    small_all = _all_gather(small_partial, name="gather_small")
    small_out = _adamw_small(small_all, small_params(weights), small_params(mom1), small_params(mom2),
                             name="adamw_small")
    loss = small_out[0][ROW_LOSS, 0]

    def unpack(big, small):
        out = {}
        for name in BIG_WEIGHTS:
            a, n = seg[name]
            out[name] = _from_rows(name, big[a:a + n], weights[name].shape)
        out["mix_norm"] = small[ROW_MIX_NORM:ROW_MIX_NORM + depth]
        out["ffn_norm"] = small[ROW_FFN_NORM:ROW_FFN_NORM + depth]
        out["ssd_conv_b"] = small[ROW_CONV_B:ROW_CONV_B + ssd_conv_b.size // d].reshape(ssd_conv_b.shape)
        out["ssd_out_norm"] = small[ROW_OUT_NORM:ROW_OUT_NORM + ssd_out_norm.size // d].reshape(ssd_out_norm.shape)
        out["pool_scale"] = lax.dynamic_slice(small[ROW_POOL_SCALE:ROW_POOL_SCALE + n_pool],
                                              (0, dev * pool_scale.shape[1]), pool_scale.shape)
        conv_w = small[ROW_CONV_W:ROW_CONV_W + convw_full.size // d].reshape(convw_full.shape)
        out["ssd_conv_w"] = lax.dynamic_slice(conv_w, (0, dev * ssd_conv_w.shape[2]), ssd_conv_w.shape[1:])[None]
        vec = small[ROW_SSD_VEC].reshape(SUBLANES, LANES)
        out["ssd_dt_bias"], out["ssd_a_log"], out["ssd_d"] = (vec[r:r + 1, :n_ssd_heads] for r in range(3))
        qk = small[ROW_QK_GAIN].reshape(SUBLANES, LANES)
        out["sb_q_norm"], out["sb_k_norm"] = qk[0:1, SB_HEAD_DIM:], qk[1:2, SB_HEAD_DIM:]
        return [out[n] for n in names]

    results = [unpack(b, s) for b, s in zip(big_out, small_out)]
    return (loss, grad_x, *results[0], *results[1], *results[2], *results[3])
```

```python
import math

import jax
import jax.numpy as jnp
from jax import lax
from jax.experimental import pallas as pl
from jax.experimental.pallas import tpu as pltpu

F32 = jnp.float32
BF16 = jnp.bfloat16

N_DEV = 8
NORM_EPS = 1e-6
V7X_VMEM_LIMIT_BYTES = 48 * 1024 * 1024
LANES = 128
SUBLANES = 8

POOL_WINDOWS = (2, 4, 8, 16)
SSD_CHUNK = 256
SSD_HEAD_DIM = 64
SSD_STATE = 128
SSD_HEADS_PER_GROUP = 4
SSD_CONV = 4
SB_HEAD_DIM = 64
SB_BLOCK = 256
SB_QUERY_BLOCK = 256

ADAM_LR = 0.001
ADAM_B1 = 0.9
ADAM_B2 = 0.999
ADAM_EPS = 1e-08
ADAM_WD = 0.01
ADAM_STEP = 10


def _params(*sem):
    return pltpu.CompilerParams(dimension_semantics=sem, vmem_limit_bytes=V7X_VMEM_LIMIT_BYTES)


def _tile(n, cap, mult):
    best = None
    for t in range(mult, min(n, cap) + 1, mult):
        if n % t == 0:
            best = t
    return best or n


def _load_slabs(ref, slabs):
    if not slabs:
        return ref[...]
    return jnp.concatenate([ref[p] for p in range(ref.shape[0])], axis=1)


def _matmul(a, b, mode, *, name, out_dtype=F32, resid=None, a_slabs=False, out_slabs=False,
            tm_cap=1024, tn_cap=1024, tk_cap=2048):
    pairs = list(zip(a, b)) if isinstance(a, (list, tuple)) else [(a, b)]
    a, b = pairs[0]
    if a_slabs:
        m, k = a.shape[1], a.shape[0] * LANES
    else:
        m, k = a.shape
    n = b.shape[1] if mode == "nn" else b.shape[0]
    assert (b.shape[0] if mode == "nn" else b.shape[1]) == k
    assert all(pa.shape == a.shape and pb.shape == b.shape for pa, pb in pairs)
    tm, tn, tk = _tile(m, tm_cap, SUBLANES), _tile(n, tn_cap, LANES), _tile(k, tk_cap, LANES)
    nk = k // tk
    dn = (((1,), (0,)), ((), ())) if mode == "nn" else (((1,), (1,)), ((), ()))
    has_resid = resid is not None
    n_pairs = len(pairs)

    def body(*refs):
        ab_refs, rest = refs[:2 * n_pairs], refs[2 * n_pairs:]
        r_ref = rest[0] if has_resid else None
        o_ref = rest[1] if has_resid else rest[0]
        kk = pl.program_id(2)

        def partial():
            total = None
            for p in range(n_pairs):
                d = lax.dot_general(_load_slabs(ab_refs[2 * p], a_slabs).astype(BF16),
                                    ab_refs[2 * p + 1][...].astype(BF16), dn, preferred_element_type=F32)
                total = d if total is None else total + d
            return total

        def finish(r):
            if has_resid:
                r = r + r_ref[...]
            if out_slabs:
                for p in range(tn // LANES):
                    o_ref[p] = r[:, p * LANES:(p + 1) * LANES].astype(out_dtype)
            else:
                o_ref[...] = r.astype(out_dtype)

        if nk == 1:
            finish(partial())
        else:
            acc = rest[-1]

            @pl.when(kk == 0)
            def _():
                acc[...] = jnp.zeros_like(acc)

            acc[...] += partial()

            @pl.when(kk == nk - 1)
            def _():
                finish(acc[...])

    b_spec = (pl.BlockSpec((tk, tn), lambda i, j, kk: (kk, j)) if mode == "nn"
              else pl.BlockSpec((tn, tk), lambda i, j, kk: (j, kk)))
    a_spec = (pl.BlockSpec((tk // LANES, tm, LANES), lambda i, j, kk: (kk, i, 0)) if a_slabs
              else pl.BlockSpec((tm, tk), lambda i, j, kk: (i, kk)))
    in_specs = [a_spec, b_spec] * n_pairs
    args = [t for pair in pairs for t in pair]
    if has_resid:
        in_specs.append(pl.BlockSpec((tm, tn), lambda i, j, kk: (i, j)))
        args.append(resid)
    if out_slabs:
        out_spec = pl.BlockSpec((tn // LANES, tm, LANES), lambda i, j, kk: (j, i, 0))
        out_shape = jax.ShapeDtypeStruct((n // LANES, m, LANES), out_dtype)
    else:
        out_spec = pl.BlockSpec((tm, tn), lambda i, j, kk: (i, j))
        out_shape = jax.ShapeDtypeStruct((m, n), out_dtype)
    return pl.pallas_call(
        body, name=name, grid=(m // tm, n // tn, nk),
        in_specs=in_specs, out_specs=out_spec, out_shape=out_shape,
        scratch_shapes=[pltpu.VMEM((tm, tn), F32)] if nk > 1 else [],
        compiler_params=_params("parallel", "parallel", "arbitrary"),
    )(*args)


def _matmul_tn(a, b, *, name, a_slabs=False, ta_cap=1024, tb_cap=1024, tr_cap=1024):
    if a_slabs:
        r, ka = a.shape[1], a.shape[0] * LANES
    else:
        r, ka = a.shape
    nb = b.shape[1]
    assert b.shape[0] == r
    ta, tb, tr = _tile(ka, ta_cap, LANES), _tile(nb, tb_cap, LANES), _tile(r, tr_cap, SUBLANES)

    def body(a_ref, b_ref, o_ref):
        @pl.when(pl.program_id(2) == 0)
        def _():
            o_ref[...] = jnp.zeros_like(o_ref)

        o_ref[...] += lax.dot_general(_load_slabs(a_ref, a_slabs).astype(BF16), b_ref[...].astype(BF16),
                                      (((0,), (0,)), ((), ())), preferred_element_type=F32)

    a_spec = (pl.BlockSpec((ta // LANES, tr, LANES), lambda i, j, kk: (i, kk, 0)) if a_slabs
              else pl.BlockSpec((tr, ta), lambda i, j, kk: (kk, i)))
    return pl.pallas_call(
        body, name=name, grid=(ka // ta, nb // tb, r // tr),
        in_specs=[a_spec, pl.BlockSpec((tr, tb), lambda i, j, kk: (kk, j))],
        out_specs=pl.BlockSpec((ta, tb), lambda i, j, kk: (i, j)),
        out_shape=jax.ShapeDtypeStruct((ka, nb), F32),
        compiler_params=_params("parallel", "parallel", "arbitrary"),
    )(a, b)


def _core_major(k):
    return (k % 2) * (N_DEV // 2) + k // 2


def _matmul_tn_into(buf, a, b, row_off, *, name, a_slabs=False, tr_cap=1024):
    if a_slabs:
        r, ka = a.shape[1], a.shape[0] * LANES
    else:
        r, ka = a.shape
    n_dev, _, n = buf.shape
    per = ka // n_dev
    assert b.shape == (r, n) and ka % n_dev == 0 and per % SUBLANES == 0 and row_off % per == 0
    tr = _tile(r, tr_cap, SUBLANES)
    fresh = isinstance(buf, jax.ShapeDtypeStruct)

    def body(*refs):
        a_ref, b_ref, o_ref = refs[-3:]
        prod = lax.dot_general(_load_slabs(a_ref, a_slabs).astype(BF16), b_ref[...].astype(BF16),
                               (((0,), (0,)), ((), ())), preferred_element_type=F32)
        @pl.when(pl.program_id(0) == 0)
        def _():
            for k in range(n_dev):
                o_ref[_core_major(k)] = prod[k * per:(k + 1) * per]

        @pl.when(pl.program_id(0) > 0)
        def _():
            for k in range(n_dev):
                o_ref[_core_major(k)] += prod[k * per:(k + 1) * per]

    a_spec = (pl.BlockSpec((ka // LANES, tr, LANES), lambda i: (0, i, 0)) if a_slabs
              else pl.BlockSpec((tr, ka), lambda i: (i, 0)))
    return pl.pallas_call(
        body, name=name, grid=(r // tr,),
        in_specs=([] if fresh else [pl.BlockSpec(memory_space=pl.ANY)]) + [a_spec, pl.BlockSpec((tr, n), lambda i: (i, 0))],
        out_specs=pl.BlockSpec((n_dev, per, n), lambda i: (0, row_off // per, 0)),
        out_shape=jax.ShapeDtypeStruct(buf.shape, F32),
        input_output_aliases={} if fresh else {0: 0},
        compiler_params=_params("arbitrary"),
    )(*(() if fresh else (buf,)), a, b)


def _rms_fwd(x, gain, *, name):
    t, d = x.shape
    tm = _tile(t, 512, SUBLANES)

    def body(x_ref, g_ref, o_ref):
        xv = x_ref[...]
        r = lax.rsqrt(jnp.mean(xv * xv, axis=-1, keepdims=True) + NORM_EPS)
        o_ref[...] = (xv * r * g_ref[...]).astype(BF16)

    return pl.pallas_call(
        body, name=name, grid=(t // tm,),
        in_specs=[pl.BlockSpec((tm, d), lambda i: (i, 0)), pl.BlockSpec((1, d), lambda i: (0, 0))],
        out_specs=pl.BlockSpec((tm, d), lambda i: (i, 0)),
        out_shape=jax.ShapeDtypeStruct((t, d), BF16),
        compiler_params=_params("parallel"),
    )(x, gain)


def _rms_bwd(x, gain, dh, dres, *, name):
    t, d = x.shape
    tm = _tile(t, 512, SUBLANES)

    def body(x_ref, g_ref, dh_ref, dres_ref, dx_ref, dxb_ref, dg_ref):
        @pl.when(pl.program_id(0) == 0)
        def _():
            dg_ref[...] = jnp.zeros_like(dg_ref)

        xv = x_ref[...]
        r = lax.rsqrt(jnp.mean(xv * xv, axis=-1, keepdims=True) + NORM_EPS)
        xhat = xv * r
        dhv = dh_ref[...]
        u = dhv * g_ref[...]
        dx = dres_ref[...] + r * (u - xhat * jnp.mean(u * xhat, axis=-1, keepdims=True))
        dx_ref[...] = dx
        dxb_ref[...] = dx.astype(BF16)
        dg_ref[...] += jnp.sum(dhv * xhat, axis=0, keepdims=True)

    blk = pl.BlockSpec((tm, d), lambda i: (i, 0))
    dx, dx_b, dg = pl.pallas_call(
        body, name=name, grid=(t // tm,),
        in_specs=[blk, pl.BlockSpec((1, d), lambda i: (0, 0)), blk, blk],
        out_specs=[blk, blk, pl.BlockSpec((1, d), lambda i: (0, 0))],
        out_shape=[jax.ShapeDtypeStruct((t, d), F32), jax.ShapeDtypeStruct((t, d), BF16),
                   jax.ShapeDtypeStruct((1, d), F32)],
        compiler_params=_params("arbitrary"),
    )(x, gain, dh, dres)
    return (dx, dx_b), dg


def _loss_head(y, target, *, name):
    t, d = y.shape
    tm = _tile(t, 512, SUBLANES)

    def body(y_ref, t_ref, dy_ref, dyb_ref, l_ref):
        @pl.when(pl.program_id(0) == 0)
        def _():
            l_ref[...] = jnp.zeros_like(l_ref)

        e = y_ref[...] - t_ref[...]
        dy = e * (1.0 / d)
        dy_ref[...] = dy
        dyb_ref[...] = dy.astype(BF16)
        l_ref[...] += jnp.sum(e * e, axis=0, keepdims=True) * (0.5 / d)

    blk = pl.BlockSpec((tm, d), lambda i: (i, 0))
    dy, dy_b, loss = pl.pallas_call(
        body, name=name, grid=(t // tm,),
        in_specs=[blk, blk], out_specs=[blk, blk, pl.BlockSpec((1, d), lambda i: (0, 0))],
        out_shape=[jax.ShapeDtypeStruct((t, d), F32), jax.ShapeDtypeStruct((t, d), BF16),
                   jax.ShapeDtypeStruct((1, d), F32)],
        compiler_params=_params("arbitrary"),
    )(y, target)
    return (dy, dy_b), loss


def _sigmoid(v):
    return 0.5 * jnp.tanh(0.5 * v) + 0.5


FFN_TOKEN_TILE = 512
FFN_HIDDEN_TILE = 1408
NT_DIMS = (((1,), (1,)), ((), ()))


def _ffn_up(h, w_gate_t, w_up_t, *, name):
    t, d = h.shape
    f = w_gate_t.shape[0]
    tm, tn = _tile(t, FFN_TOKEN_TILE, SUBLANES), _tile(f, FFN_HIDDEN_TILE, LANES)

    def body(h_ref, g_ref, u_ref, s_ref, a_ref, b_ref):
        hv = h_ref[...].astype(BF16)
        av = lax.dot_general(hv, g_ref[...].astype(BF16), NT_DIMS, preferred_element_type=F32)
        bv = lax.dot_general(hv, u_ref[...].astype(BF16), NT_DIMS, preferred_element_type=F32)
        s_ref[...] = (av * _sigmoid(av) * bv).astype(BF16)
        a_ref[...] = av.astype(BF16)
        b_ref[...] = bv.astype(BF16)

    w_spec = pl.BlockSpec((tn, d), lambda j, i: (j, 0))
    out_spec = pl.BlockSpec((tm, tn), lambda j, i: (i, j))
    out = jax.ShapeDtypeStruct((t, f), BF16)
    return pl.pallas_call(
        body, name=name, grid=(f // tn, t // tm),
        in_specs=[pl.BlockSpec((tm, d), lambda j, i: (i, 0)), w_spec, w_spec],
        out_specs=[out_spec, out_spec, out_spec], out_shape=[out, out, out],
        compiler_params=_params("parallel", "parallel"),
    )(h, w_gate_t, w_up_t)


def _ffn_dact(dx, w_down, a, b, *, name):
    t, d = dx.shape
    f = w_down.shape[0]
    tm, tn = _tile(t, FFN_TOKEN_TILE, SUBLANES), _tile(f, FFN_HIDDEN_TILE, LANES)

    def body(dx_ref, w_ref, a_ref, b_ref, da_ref, db_ref):
        ds = lax.dot_general(dx_ref[...].astype(BF16), w_ref[...].astype(BF16), NT_DIMS, preferred_element_type=F32)
        av = a_ref[...].astype(F32)
        sg = _sigmoid(av)
        da_ref[...] = (ds * b_ref[...].astype(F32) * (sg * (1.0 + av * (1.0 - sg)))).astype(BF16)
        db_ref[...] = (ds * av * sg).astype(BF16)

    blk = pl.BlockSpec((tm, tn), lambda j, i: (i, j))
    out = jax.ShapeDtypeStruct((t, f), BF16)
    return pl.pallas_call(
        body, name=name, grid=(f // tn, t // tm),
        in_specs=[pl.BlockSpec((tm, d), lambda j, i: (i, 0)), pl.BlockSpec((tn, d), lambda j, i: (j, 0)), blk, blk],
        out_specs=[blk, blk], out_shape=[out, out],
        compiler_params=_params("parallel", "parallel"),
    )(dx, w_down, a, b)


def _ffn_fwd(x, gain, w_gate_t, w_up_t, w_down, tag):
    h = _rms_fwd(x, gain, name=f"ffn_norm_{tag}")
    s, a, b = _ffn_up(h, w_gate_t, w_up_t, name=f"ffn_up_{tag}")
    x_new = _matmul(s, w_down, "nn", resid=x, tn_cap=1024, tk_cap=2816, name=f"ffn_down_{tag}")
    return x_new, (x, h, a, b, s)


def _ffn_bwd(dx, saved, gain, w_gate_t, w_up_t, w_down, grads, rows, tag):
    x, h, a, b, s = saved
    dx, dx_b = dx
    da, db = _ffn_dact(dx_b, w_down, a, b, name=f"ffn_dact_{tag}")
    grads = _matmul_tn_into(grads, da, h, rows[0], name=f"ffn_dwgate_{tag}")
    grads = _matmul_tn_into(grads, db, h, rows[1], name=f"ffn_dwup_{tag}")
    grads = _matmul_tn_into(grads, s, dx_b, rows[2], name=f"ffn_dwdown_{tag}")
    dh = _matmul([da, db], [w_gate_t, w_up_t], "nn", tm_cap=512, tn_cap=1024, tk_cap=2816, name=f"ffn_dh_{tag}")
    dx_in, dgain = _rms_bwd(x, gain, dh, dx, name=f"ffn_dnorm_{tag}")
    return dx_in, grads, dgain


POOL_HALO = 16


def _shift_rows(v, k):
    n = v.shape[0]
    return pltpu.roll(v, k % n, 0)


def _window_sum(v, w, direction):
    k = 1
    while k < w:
        v = v + _shift_rows(v, direction * k)
        k *= 2
    return v


def _pool_fwd(u, x, w_group, scale, *, name):
    t, d = u.shape
    ng, dg = w_group.shape[0], w_group.shape[1]
    tm = _tile(t, 512, POOL_HALO)
    hb = tm // POOL_HALO

    def body(u_ref, halo_ref, x_ref, w_ref, s_ref, xo_ref, p_ref, y_ref):
        i, g = pl.program_id(0), pl.program_id(1)
        halo = jnp.where(i > 0, halo_ref[...], 0.0)
        ext = jnp.concatenate([halo, u_ref[...]], axis=0)
        pos = i * tm + lax.broadcasted_iota(jnp.int32, (tm, 1), 0)
        for gi, win in enumerate(POOL_WINDOWS):
            @pl.when(g == gi)
            def _(win=win):
                tot = _window_sum(ext, win, 1)[POOL_HALO:]
                cnt = jnp.minimum(pos + 1, win).astype(F32)
                p = (tot / cnt - u_ref[...]).astype(BF16)
                p_ref[...] = p
                y = jnp.dot(p, w_ref[...].astype(BF16), preferred_element_type=F32)
                y_ref[...] = y
                xo_ref[...] = x_ref[...] + y * s_ref[...]

    blk = pl.BlockSpec((tm, dg), lambda i, g: (i, g))
    return pl.pallas_call(
        body, name=name, grid=(t // tm, ng),
        in_specs=[blk, pl.BlockSpec((POOL_HALO, dg), lambda i, g: (jnp.maximum(i * hb - 1, 0), g)), blk,
                  pl.BlockSpec((None, dg, dg), lambda i, g: (g, 0, 0)), pl.BlockSpec((1, dg), lambda i, g: (0, g))],
        out_specs=[blk, blk, blk],
        out_shape=[jax.ShapeDtypeStruct((t, d), F32), jax.ShapeDtypeStruct((t, d), BF16),
                   jax.ShapeDtypeStruct((t, d), F32)],
        compiler_params=_params("parallel", "parallel"),
    )(u, u, x, w_group, scale)


def _pool_bwd(dx, p, y_pre, w_group, scale, *, name):
    t, d = dx.shape
    ng, dg = w_group.shape[0], w_group.shape[1]
    tm = _tile(t, 512, POOL_HALO)
    hb = tm // POOL_HALO
    nt = t // tm

    def body(dx_ref, nxt_ref, p_ref, y_ref, w_ref, s_ref, du_ref, dw_ref, ds_ref):
        g, i = pl.program_id(0), pl.program_id(1)

        @pl.when(i == 0)
        def _():
            dw_ref[...] = jnp.zeros_like(dw_ref)
            ds_ref[...] = jnp.zeros_like(ds_ref)

        dxv = dx_ref[...]
        ds_ref[...] += jnp.sum(dxv * y_ref[...], axis=0, keepdims=True)
        nxt = jnp.where(i < nt - 1, nxt_ref[...], 0.0)
        dyp = (jnp.concatenate([dxv, nxt], axis=0) * s_ref[...]).astype(BF16)
        dw_ref[...] += lax.dot_general(p_ref[...], dyp[:tm], (((0,), (0,)), ((), ())), preferred_element_type=F32)
        dp = lax.dot_general(dyp, w_ref[...].astype(BF16), (((1,), (1,)), ((), ())), preferred_element_type=F32)
        pos = i * tm + lax.broadcasted_iota(jnp.int32, (tm + POOL_HALO, 1), 0)
        for gi, win in enumerate(POOL_WINDOWS):
            @pl.when(g == gi)
            def _(win=win):
                q = dp / jnp.minimum(pos + 1, win).astype(F32)
                du_ref[...] = (_window_sum(q, win, -1)[:tm] - dp[:tm]).astype(BF16)

    blk = pl.BlockSpec((tm, dg), lambda g, i: (i, g))
    return pl.pallas_call(
        body, name=name, grid=(ng, nt),
        in_specs=[blk, pl.BlockSpec((POOL_HALO, dg), lambda g, i: (jnp.minimum((i + 1) * hb, t // POOL_HALO - 1), g)),
                  blk, blk, pl.BlockSpec((None, dg, dg), lambda g, i: (g, 0, 0)),
                  pl.BlockSpec((1, dg), lambda g, i: (0, g))],
        out_specs=[blk, pl.BlockSpec((None, dg, dg), lambda g, i: (g, 0, 0)), pl.BlockSpec((1, dg), lambda g, i: (0, g))],
        out_shape=[jax.ShapeDtypeStruct((t, d), BF16), jax.ShapeDtypeStruct((ng, dg, dg), F32),
                   jax.ShapeDtypeStruct((1, d), F32)],
        compiler_params=_params("parallel", "arbitrary"),
    )(dx, dx, p, y_pre, w_group, scale)


def _pool_mixer_fwd(x, gain, w_in, w_group, scale, tag):
    h = _rms_fwd(x, gain, name=f"pool_norm_{tag}")
    u = _matmul(h, w_in, "nn", name=f"pool_in_{tag}")
    x_new, p, y_pre = _pool_fwd(u, x, w_group, scale, name=f"pool_mix_{tag}")
    return x_new, (x, h, p, y_pre)


def _pool_mixer_bwd(dx, saved, gain, w_in, w_group, scale, grads, row_in, tag):
    x, h, p, y_pre = saved
    dx, _ = dx
    du, dw_group, dscale = _pool_bwd(dx, p, y_pre, w_group, scale, name=f"pool_dmix_{tag}")
    grads = _matmul_tn_into(grads, h, du, row_in, name=f"pool_dwin_{tag}")
    dh = _matmul(du, w_in, "nt", name=f"pool_dh_{tag}")
    dx_in, dgain = _rms_bwd(x, gain, dh, dx, name=f"pool_dnorm_{tag}")
    return dx_in, grads, dw_group, dscale, dgain


CONV_HALO = 8
NEG_BIG = -1e30


def _softplus(v):
    return jnp.maximum(v, 0.0) + jnp.log(1.0 + jnp.exp(-jnp.abs(v)))


def _conv_taps(ext, w_ref, off, rows):
    acc = None
    for k in range(SSD_CONV):
        shift = SSD_CONV - 1 - k
        v = (_shift_rows(ext, shift) if shift else ext)[off:off + rows] * w_ref[k:k + 1, :]
        acc = v if acc is None else acc + v
    return acc


def _ssd_conv_fwd(zx, conv_w, conv_b, col0, *, name):
    t = zx.shape[0]
    c = conv_w.shape[1]
    tm, tc = _tile(t, 512, CONV_HALO), _tile(c, 512, LANES)
    hb, cb0 = tm // CONV_HALO, col0 // tc
    assert col0 % tc == 0

    def body(x_ref, halo_ref, w_ref, b_ref, o_ref):
        halo = jnp.where(pl.program_id(0) > 0, halo_ref[...], 0.0)
        ext = jnp.concatenate([halo, x_ref[...]], axis=0)
        pre = _conv_taps(ext, w_ref, CONV_HALO, tm) + b_ref[...]
        o_ref[...] = pre * _sigmoid(pre)

    return pl.pallas_call(
        body, name=name, grid=(t // tm, c // tc),
        in_specs=[pl.BlockSpec((tm, tc), lambda i, j: (i, j + cb0)),
                  pl.BlockSpec((CONV_HALO, tc), lambda i, j: (jnp.maximum(i * hb - 1, 0), j + cb0)),
                  pl.BlockSpec((SSD_CONV, tc), lambda i, j: (0, j)), pl.BlockSpec((1, tc), lambda i, j: (0, j))],
        out_specs=pl.BlockSpec((tm, tc), lambda i, j: (i, j)),
        out_shape=jax.ShapeDtypeStruct((t, c), F32),
        compiler_params=_params("parallel", "parallel"),
    )(zx, zx, conv_w, conv_b)


def _ssd_conv_bwd(d_parts, zx, conv_w, conv_b, col0, *, name):
    t = zx.shape[0]
    c = conv_w.shape[1]
    tm, tc = _tile(t, 512, CONV_HALO), _tile(c, 512, LANES)
    hb, cb0, nt = tm // CONV_HALO, col0 // tc, t // tm
    last_halo = t // CONV_HALO - 1
    starts = [0]
    for part in d_parts:
        assert part.shape[1] % tc == 0
        starts.append(starts[-1] + part.shape[1] // tc)
    assert starts[-1] == c // tc
    n_parts = len(d_parts)

    def pick(refs, j):
        value = refs[-1][...]
        for p in reversed(range(n_parts - 1)):
            value = jnp.where(j < starts[p + 1], refs[p][...], value)
        return value

    def body(x_ref, prev_ref, nxt_ref, *rest):
        d_refs, dnxt_refs = rest[:n_parts], rest[n_parts:2 * n_parts]
        w_ref, b_ref, dx_ref, dw_ref, db_ref = rest[2 * n_parts:]
        j, i = pl.program_id(0), pl.program_id(1)

        @pl.when(i == 0)
        def _():
            dw_ref[...] = jnp.zeros_like(dw_ref)
            db_ref[...] = jnp.zeros_like(db_ref)

        prev = jnp.where(i > 0, prev_ref[...], 0.0)
        has_next = i < nt - 1
        ext = jnp.concatenate([prev, x_ref[...], jnp.where(has_next, nxt_ref[...], 0.0)], axis=0)
        pre = _conv_taps(ext, w_ref, CONV_HALO, tm + CONV_HALO) + b_ref[...]
        sg = _sigmoid(pre)
        dact = jnp.concatenate([pick(d_refs, j), jnp.where(has_next, pick(dnxt_refs, j), 0.0)], axis=0)
        dpre = dact * (sg * (1.0 + pre * (1.0 - sg)))
        db_ref[...] += jnp.sum(dpre[:tm], axis=0, keepdims=True)
        acc = None
        for k in range(SSD_CONV):
            shift = SSD_CONV - 1 - k
            src = (_shift_rows(ext, shift) if shift else ext)[CONV_HALO:CONV_HALO + tm]
            dw_ref[k:k + 1, :] += jnp.sum(dpre[:tm] * src, axis=0, keepdims=True)
            v = (_shift_rows(dpre, -shift) if shift else dpre)[:tm] * w_ref[k:k + 1, :]
            acc = v if acc is None else acc + v
        dx_ref[...] = acc.astype(BF16)

    def part_specs(rows, row_index):
        def spec(p):
            def index(j, i):
                mine = (j >= starts[p]) & (j < starts[p + 1])
                return jnp.where(mine, row_index(i), 0), jnp.where(mine, j - starts[p], 0)
            return pl.BlockSpec((rows, tc), index)
        return [spec(p) for p in range(n_parts)]

    main = lambda j, i: (i, j + cb0)
    next_halo = lambda i: jnp.minimum((i + 1) * hb, last_halo)
    return pl.pallas_call(
        body, name=name, grid=(c // tc, nt),
        in_specs=[pl.BlockSpec((tm, tc), main),
                  pl.BlockSpec((CONV_HALO, tc), lambda j, i: (jnp.maximum(i * hb - 1, 0), j + cb0)),
                  pl.BlockSpec((CONV_HALO, tc), lambda j, i: (next_halo(i), j + cb0)),
                  *part_specs(tm, lambda i: i), *part_specs(CONV_HALO, next_halo),
                  pl.BlockSpec((SSD_CONV, tc), lambda j, i: (0, j)), pl.BlockSpec((1, tc), lambda j, i: (0, j))],
        out_specs=[pl.BlockSpec((tm, tc), lambda j, i: (i, j)), pl.BlockSpec((SSD_CONV, tc), lambda j, i: (0, j)),
                   pl.BlockSpec((1, tc), lambda j, i: (0, j))],
        out_shape=[jax.ShapeDtypeStruct((t, c), BF16), jax.ShapeDtypeStruct((SSD_CONV, c), F32),
                   jax.ShapeDtypeStruct((1, c), F32)],
        compiler_params=_params("parallel", "arbitrary"),
    )(zx, zx, zx, *d_parts, *d_parts, conv_w, conv_b)


SSD_CUMSUM_PIECES = 2
SSD_GROUPS_PER_STEP = 1


def _ssd_group_pad(v, n_groups):
    lead = v.shape[:-1]
    v = v.reshape(*lead, n_groups, SSD_HEADS_PER_GROUP)
    v = jnp.pad(v, [(0, 0)] * (len(lead) + 1) + [(0, LANES - SSD_HEADS_PER_GROUP)])
    return v.reshape(*lead, n_groups * LANES)


def _ssd_group_unpad(v, n_groups):
    lead = v.shape[:-1]
    return v.reshape(*lead, n_groups, LANES)[..., :SSD_HEADS_PER_GROUP].reshape(*lead, -1)


def _ssd_chunk_common(dtp_ref, par_ref):
    ell = SSD_CHUNK
    dt = _softplus(dtp_ref[...] + par_ref[0:1, :])
    a = -jnp.exp(par_ref[1:2, :])
    row = lax.broadcasted_iota(jnp.int32, (ell, ell), 0)
    col = lax.broadcasted_iota(jnp.int32, (ell, ell), 1)
    acum = _split_dot(dt * a, (row >= col).astype(BF16), SSD_CUMSUM_PIECES, left=True)
    return dt, a, acum, acum.T, row, col


def _ssd_scan_fwd(xa, dtp, par, n_groups, *, name):
    t = xa.shape[0]
    ell, hd, hpg, ns, gps = SSD_CHUNK, SSD_HEAD_DIM, SSD_HEADS_PER_GROUP, SSD_STATE, SSD_GROUPS_PER_STEP
    gw = hpg * hd
    nc = t // ell
    b_blk0, c_blk0 = n_groups * gw // (ns * gps), (n_groups * gw // ns + n_groups) // gps

    def body(xs_ref, b_ref, c_ref, dtp_ref, par_ref, y_ref, sin_ref, st):
        @pl.when(pl.program_id(1) == 0)
        def _():
            st[...] = jnp.zeros_like(st)

        dt, _, acum, acum_t, row, col = _ssd_chunk_common(dtp_ref, par_ref)
        for gi in range(gps):
            bb = b_ref[:, gi * ns:(gi + 1) * ns].astype(BF16)
            cc = c_ref[:, gi * ns:(gi + 1) * ns].astype(BF16)
            cb = lax.dot_general(cc, bb, NT_DIMS, preferred_element_type=F32)
            s_all = st[gi]
            sin_ref[gi] = s_all
            c_s = lax.dot_general(cc, s_all.astype(BF16), NT_DIMS, preferred_element_type=F32)
            weighted, keep = [], []
            for hh in range(hpg):
                lanes = slice(gi * gw + hh * hd, gi * gw + (hh + 1) * hd)
                hl = gi * LANES + hh
                col_a, row_a = acum[:, hl:hl + 1], acum_t[hl:hl + 1, :]
                decay = jnp.exp(jnp.where(row >= col, col_a - row_a, NEG_BIG))
                xdt = xs_ref[:, lanes] * dt[:, hl:hl + 1]
                y = jnp.dot((cb * decay).astype(BF16), xdt.astype(BF16), preferred_element_type=F32)
                y_ref[:, lanes] = y + jnp.exp(col_a) * c_s[:, hh * hd:(hh + 1) * hd]
                a_last = acum[ell - 1:ell, hl:hl + 1]
                weighted.append((xdt * jnp.exp(a_last - col_a)).astype(BF16))
                keep.append(jnp.broadcast_to(jnp.exp(a_last), (hd, 1)))
            st[gi] = jnp.concatenate(keep, axis=0) * s_all + lax.dot_general(
                jnp.concatenate(weighted, axis=1), bb, (((0,), (0,)), ((), ())), preferred_element_type=F32)

    return pl.pallas_call(
        body, name=name, grid=(n_groups // gps, nc),
        in_specs=[pl.BlockSpec((ell, gps * gw), lambda g, c: (c, g)),
                  pl.BlockSpec((ell, gps * ns), lambda g, c: (c, b_blk0 + g)),
                  pl.BlockSpec((ell, gps * ns), lambda g, c: (c, c_blk0 + g)),
                  pl.BlockSpec((ell, gps * LANES), lambda g, c: (c, g)),
                  pl.BlockSpec((SUBLANES, gps * LANES), lambda g, c: (0, g))],
        out_specs=[pl.BlockSpec((ell, gps * gw), lambda g, c: (c, g)),
                   pl.BlockSpec((None, gps, gw, ns), lambda g, c: (c, g, 0, 0))],
        out_shape=[jax.ShapeDtypeStruct((t, n_groups * gw), F32),
                   jax.ShapeDtypeStruct((nc, n_groups, gw, ns), F32)],
        scratch_shapes=[pltpu.VMEM((gps, gw, ns), F32)],
        compiler_params=_params("parallel", "arbitrary"),
    )(xa, xa, xa, dtp, par)


def _ssd_scan_bwd(dy, xa, dtp, par, s_in, n_groups, *, name):
    t = xa.shape[0]
    ell, hd, hpg, ns, gps = SSD_CHUNK, SSD_HEAD_DIM, SSD_HEADS_PER_GROUP, SSD_STATE, SSD_GROUPS_PER_STEP
    gw = hpg * hd
    nc = t // ell
    b_blk0, c_blk0 = n_groups * gw // (ns * gps), (n_groups * gw // ns + n_groups) // gps
    nt_dims = (((1,), (1,)), ((), ()))
    tn_dims = (((0,), (0,)), ((), ()))

    def body(dy_ref, xs_ref, b_ref, c_ref, dtp_ref, par_ref, sin_ref,
             dxs_ref, db_ref, dc_ref, ddtp_ref, dpar_ref, dst):
        @pl.when(pl.program_id(1) == 0)
        def _():
            dst[...] = jnp.zeros_like(dst)
            dpar_ref[...] = jnp.zeros_like(dpar_ref)

        dtg, a_g, acum, acum_t, row, col = _ssd_chunk_common(dtp_ref, par_ref)
        lane = lax.broadcasted_iota(jnp.int32, (1, gps * LANES), 1)
        dacum = jnp.zeros((ell, gps * LANES), F32)
        xsum = jnp.zeros((ell, gps * LANES), F32)
        dsum = jnp.zeros((1, gps * LANES), F32)
        for gi, hh in [(gi, hh) for gi in range(gps) for hh in range(hpg)]:
            if hh == 0:
                bb = b_ref[:, gi * ns:(gi + 1) * ns].astype(BF16)
                cc = c_ref[:, gi * ns:(gi + 1) * ns].astype(BF16)
                cb = lax.dot_general(cc, bb, nt_dims, preferred_element_type=F32)
                cb_t = lax.dot_general(bb, cc, nt_dims, preferred_element_type=F32)
                dcb = jnp.zeros((ell, ell), F32)
                dcb_t = jnp.zeros((ell, ell), F32)
                s_all, ds_all = sin_ref[gi], dst[gi]
                c_s_all = lax.dot_general(cc, s_all.astype(BF16), nt_dims, preferred_element_type=F32)
                b_ds_all = lax.dot_general(bb, ds_all.astype(BF16), nt_dims, preferred_element_type=F32)
                s_ds = jnp.sum(s_all * ds_all, axis=1, keepdims=True)
                dy_decayed, x_weighted, keep = [], [], []
            lanes = slice(gi * gw + hh * hd, gi * gw + (hh + 1) * hd)
            head = slice(hh * hd, (hh + 1) * hd)
            hl = gi * LANES + hh
            onehot = (lane == hl).astype(F32)
            col_a, row_a = acum[:, hl:hl + 1], acum_t[hl:hl + 1, :]
            decay = jnp.exp(jnp.where(row >= col, col_a - row_a, NEG_BIG))
            decay_t = jnp.exp(jnp.where(col >= row, row_a - col_a, NEG_BIG))
            e_col = jnp.exp(col_a)
            a_last = acum[ell - 1:ell, hl:hl + 1]
            w = jnp.exp(a_last - col_a)
            e_last = jnp.exp(a_last)
            xs_h, dy_h = xs_ref[:, lanes], dy_ref[:, lanes]
            dt_h = dtg[:, hl:hl + 1]
            xdt = xs_h * dt_h
            xdt_b, dy_b = xdt.astype(BF16), dy_h.astype(BF16)
            dm_decay = lax.dot_general(dy_b, xdt_b, nt_dims, preferred_element_type=F32) * decay
            dm_decay_t = lax.dot_general(xdt_b, dy_b, nt_dims, preferred_element_type=F32) * decay_t
            dcb += dm_decay
            dcb_t += dm_decay_t
            m_t = cb_t * decay_t
            dac = jnp.sum(dm_decay * cb, axis=1, keepdims=True) - jnp.sum(dm_decay_t * cb_t, axis=1, keepdims=True)
            b_ds = b_ds_all[:, head]
            dxdt = jnp.dot(m_t.astype(BF16), dy_b, preferred_element_type=F32) + w * b_ds
            dac += jnp.sum(dy_h * c_s_all[:, head], axis=1, keepdims=True) * e_col
            q = jnp.sum(xdt * b_ds, axis=1, keepdims=True) * w
            dac -= q
            d_last = jnp.sum(q, axis=0, keepdims=True) + e_last * jnp.sum(s_ds[head], axis=0, keepdims=True)
            is_last = lax.broadcasted_iota(jnp.int32, (ell, 1), 0) == ell - 1
            dac += jnp.where(is_last, d_last, 0.0)
            dacum += dac * onehot
            dy_decayed.append((dy_h * e_col).astype(BF16))
            x_weighted.append((xdt * w).astype(BF16))
            keep.append(jnp.broadcast_to(e_last, (hd, 1)))
            dxs_ref[:, lanes] = dxdt * dt_h + dy_h * par_ref[2:3, hl:hl + 1]
            xsum += jnp.sum(dxdt * xs_h, axis=1, keepdims=True) * onehot
            dsum += jnp.sum(jnp.sum(dy_h * xs_h, axis=1, keepdims=True), axis=0, keepdims=True) * onehot
            if hh == hpg - 1:
                group = slice(gi * ns, (gi + 1) * ns)
                dy_all, x_all = jnp.concatenate(dy_decayed, axis=1), jnp.concatenate(x_weighted, axis=1)
                dc_ref[:, group] = (jnp.dot(dy_all, s_all.astype(BF16), preferred_element_type=F32)
                                    + jnp.dot(dcb.astype(BF16), bb, preferred_element_type=F32))
                db_ref[:, group] = (jnp.dot(x_all, ds_all.astype(BF16), preferred_element_type=F32)
                                    + jnp.dot(dcb_t.astype(BF16), cc, preferred_element_type=F32))
                dst[gi] = jnp.concatenate(keep, axis=0) * ds_all + lax.dot_general(
                    dy_all, cc, tn_dims, preferred_element_type=F32)
        dda = _split_dot(dacum, (col >= row).astype(BF16), SSD_CUMSUM_PIECES, left=True)
        ddtp = (xsum + dda * a_g) * _sigmoid(dtp_ref[...] + par_ref[0:1, :])
        ddtp_ref[...] = ddtp
        dpar_ref[0:1, :] += jnp.sum(ddtp, axis=0, keepdims=True)
        dpar_ref[1:2, :] += jnp.sum(dda * dtg, axis=0, keepdims=True) * a_g
        dpar_ref[2:3, :] += dsum

    rev = lambda i: nc - 1 - i
    return pl.pallas_call(
        body, name=name, grid=(n_groups // gps, nc),
        in_specs=[pl.BlockSpec((ell, gps * gw), lambda g, i: (rev(i), g)),
                  pl.BlockSpec((ell, gps * gw), lambda g, i: (rev(i), g)),
                  pl.BlockSpec((ell, gps * ns), lambda g, i: (rev(i), b_blk0 + g)),
                  pl.BlockSpec((ell, gps * ns), lambda g, i: (rev(i), c_blk0 + g)),
                  pl.BlockSpec((ell, gps * LANES), lambda g, i: (rev(i), g)),
                  pl.BlockSpec((SUBLANES, gps * LANES), lambda g, i: (0, g)),
                  pl.BlockSpec((None, gps, gw, ns), lambda g, i: (rev(i), g, 0, 0))],
        out_specs=[pl.BlockSpec((ell, gps * gw), lambda g, i: (rev(i), g)),
                   pl.BlockSpec((ell, gps * ns), lambda g, i: (rev(i), g)),
                   pl.BlockSpec((ell, gps * ns), lambda g, i: (rev(i), g)),
                   pl.BlockSpec((ell, gps * LANES), lambda g, i: (rev(i), g)),
                   pl.BlockSpec((SUBLANES, gps * LANES), lambda g, i: (0, g))],
        out_shape=[jax.ShapeDtypeStruct((t, n_groups * gw), F32), jax.ShapeDtypeStruct((t, n_groups * ns), F32),
                   jax.ShapeDtypeStruct((t, n_groups * ns), F32), jax.ShapeDtypeStruct((t, n_groups * LANES), F32),
                   jax.ShapeDtypeStruct((SUBLANES, n_groups * LANES), F32)],
        scratch_shapes=[pltpu.VMEM((gps, gw, ns), F32)],
        compiler_params=_params("parallel", "arbitrary"),
    )(dy, xa, xa, xa, dtp, par, s_in)


def _ssd_gate_fwd(y, xa, zx, d_rep, out_norm, *, name):
    t, di = y.shape
    gw = SSD_HEADS_PER_GROUP * SSD_HEAD_DIM
    tm = _tile(t, 512, SUBLANES)

    def body(y_ref, xs_ref, z_ref, d_ref, n_ref, o_ref):
        zv = z_ref[...]
        gt = (y_ref[...] + d_ref[...] * xs_ref[...]) * (zv * _sigmoid(zv))
        r = lax.rsqrt(jnp.mean(gt * gt, axis=-1, keepdims=True) + NORM_EPS)
        o_ref[...] = (gt * r * n_ref[...]).astype(BF16)

    blk = pl.BlockSpec((tm, gw), lambda i, g: (i, g))
    vec = pl.BlockSpec((1, gw), lambda i, g: (0, g))
    return pl.pallas_call(
        body, name=name, grid=(t // tm, di // gw),
        in_specs=[blk, blk, blk, vec, vec], out_specs=blk,
        out_shape=jax.ShapeDtypeStruct((t, di), BF16),
        compiler_params=_params("parallel", "parallel"),
    )(y, xa, zx, d_rep, out_norm)


def _ssd_gate_bwd(dgn, y, xa, zx, d_rep, out_norm, *, name):
    t, di = y.shape
    gw = SSD_HEADS_PER_GROUP * SSD_HEAD_DIM
    tm = _tile(t, 512, SUBLANES)

    def body(dg_ref, y_ref, xs_ref, z_ref, d_ref, n_ref, dy_ref, dz_ref, dn_ref):
        @pl.when(pl.program_id(1) == 0)
        def _():
            dn_ref[...] = jnp.zeros_like(dn_ref)

        zv = z_ref[...]
        sg = _sigmoid(zv)
        sz = zv * sg
        y2 = y_ref[...] + d_ref[...] * xs_ref[...]
        gt = y2 * sz
        r = lax.rsqrt(jnp.mean(gt * gt, axis=-1, keepdims=True) + NORM_EPS)
        ghat = gt * r
        dgv = dg_ref[...]
        dn_ref[...] += jnp.sum(dgv * ghat, axis=0, keepdims=True)
        u = dgv * n_ref[...]
        dgt = r * (u - ghat * jnp.mean(u * ghat, axis=-1, keepdims=True))
        dy_ref[...] = dgt * sz
        dz_ref[...] = (dgt * y2 * (sg * (1.0 + zv * (1.0 - sg)))).astype(BF16)

    blk = pl.BlockSpec((tm, gw), lambda g, i: (i, g))
    vec = pl.BlockSpec((1, gw), lambda g, i: (0, g))
    return pl.pallas_call(
        body, name=name, grid=(di // gw, t // tm),
        in_specs=[blk, blk, blk, blk, vec, vec], out_specs=[blk, blk, vec],
        out_shape=[jax.ShapeDtypeStruct((t, di), F32), jax.ShapeDtypeStruct((t, di), BF16),
                   jax.ShapeDtypeStruct((1, di), F32)],
        compiler_params=_params("parallel", "arbitrary"),
    )(dgn, y, xa, zx, d_rep, out_norm)


def _ssd_mixer_fwd(x, gain, w_zx_t, w_dt_t, conv_w, conv_b, par, d_rep, out_norm, w_out, tag):
    di = w_out.shape[0]
    n_groups = di // (SSD_HEADS_PER_GROUP * SSD_HEAD_DIM)
    h = _rms_fwd(x, gain, name=f"ssd_norm_{tag}")
    zx = _matmul(h, w_zx_t, "nt", name=f"ssd_in_{tag}")
    dtp = _matmul(h, w_dt_t, "nt", name=f"ssd_dt_{tag}")
    xa = _ssd_conv_fwd(zx, conv_w, conv_b, di, name=f"ssd_conv_{tag}")
    y, s_in = _ssd_scan_fwd(xa, dtp, par, n_groups, name=f"ssd_scan_{tag}")
    gn = _ssd_gate_fwd(y, xa, zx, d_rep, out_norm, name=f"ssd_gate_{tag}")
    x_new = _matmul(gn, w_out, "nn", resid=x, name=f"ssd_out_{tag}")
    return x_new, (x, h, zx, dtp, xa, y, s_in, gn)


def _ssd_mixer_bwd(dx, saved, gain, w_zx_t, w_dt_t, conv_w, conv_b, par, d_rep, out_norm, w_out, grads, row_out,
                   tag):
    x, h, zx, dtp, xa, y, s_in, gn = saved
    di = w_out.shape[0]
    n_groups = di // (SSD_HEADS_PER_GROUP * SSD_HEAD_DIM)
    dx, dx_b = dx
    dgn = _matmul(dx_b, w_out, "nt", name=f"ssd_dgn_{tag}")
    grads = _matmul_tn_into(grads, gn, dx_b, row_out, name=f"ssd_dwout_{tag}")
    dy2, dz, dnorm = _ssd_gate_bwd(dgn, y, xa, zx, d_rep, out_norm, name=f"ssd_dgate_{tag}")
    dxs, db, dc, ddtp, dpar = _ssd_scan_bwd(dy2, xa, dtp, par, s_in, n_groups, name=f"ssd_dscan_{tag}")
    dxbc, dconv_w, dconv_b = _ssd_conv_bwd([dxs, db, dc], zx, conv_w, conv_b, di, name=f"ssd_dconv_{tag}")
    dzx = jnp.concatenate([dz, dxbc], axis=1)
    dw_zx_t = _matmul_tn(dzx, h, name=f"ssd_dwin_{tag}")
    dw_dt_t = _matmul_tn(ddtp, h, name=f"ssd_dwdt_{tag}")
    dh = _matmul(dzx, w_zx_t, "nn", name=f"ssd_dh_{tag}")
    dh = _matmul(ddtp, w_dt_t, "nn", resid=dh, name=f"ssd_dhdt_{tag}")
    dx_in, dgain = _rms_bwd(x, gain, dh, dx, name=f"ssd_dnorm_{tag}")
    return dx_in, grads, dw_zx_t, dw_dt_t, dconv_w, dconv_b, dpar, dnorm, dgain


HEAD_SUM_PIECES = 2


def _head_sums(v):
    row = lax.broadcasted_iota(jnp.int32, (LANES, LANES), 0)
    col = lax.broadcasted_iota(jnp.int32, (LANES, LANES), 1)
    same_head = (row // SB_HEAD_DIM == col // SB_HEAD_DIM).astype(BF16)
    return _split_dot(v, same_head, HEAD_SUM_PIECES)


def _sb_qk_norm_fwd(qkv, gains, *, name):
    ns, t, _ = qkv.shape
    per = ns // 3
    tm = _tile(t, 1024, SUBLANES)
    inv_sqrt_d = 1.0 / math.sqrt(SB_HEAD_DIM)

    def body(x_ref, g_ref, o_ref):
        kind = pl.program_id(0) // per
        xv = x_ref[...]

        @pl.when(kind == 2)
        def _():
            o_ref[...] = xv.astype(BF16)

        @pl.when(kind < 2)
        def _():
            ms = _head_sums(xv * xv) * (1.0 / SB_HEAD_DIM)
            y = xv * lax.rsqrt(ms + NORM_EPS) * g_ref[pl.ds(kind, 1), :]
            o_ref[...] = (y * jnp.where(kind == 0, inv_sqrt_d, 1.0)).astype(BF16)

    blk = pl.BlockSpec((None, tm, LANES), lambda s, i: (s, i, 0))
    return pl.pallas_call(
        body, name=name, grid=(ns, t // tm),
        in_specs=[blk, pl.BlockSpec((SUBLANES, LANES), lambda s, i: (0, 0))], out_specs=blk,
        out_shape=jax.ShapeDtypeStruct((ns, t, LANES), BF16),
        compiler_params=_params("parallel", "parallel"),
    )(qkv, gains)


def _sb_qk_norm_bwd(dq, dk, dv, qkv, gains, *, name):
    ns, t, _ = qkv.shape
    per = ns // 3
    tm = _tile(t, 1024, SUBLANES)
    inv_sqrt_d = 1.0 / math.sqrt(SB_HEAD_DIM)

    def body(dq_ref, dk_ref, dv_ref, x_ref, g_ref, o_ref, dg_ref):
        s = pl.program_id(0)
        kind = s // per

        @pl.when((s == 0) & (pl.program_id(1) == 0))
        def _():
            dg_ref[...] = jnp.zeros_like(dg_ref)

        @pl.when(kind == 2)
        def _():
            o_ref[...] = dv_ref[...].astype(BF16)

        @pl.when(kind < 2)
        def _():
            xv = x_ref[...]
            dy = jnp.where(kind == 0, dq_ref[...] * inv_sqrt_d, dk_ref[...])
            r = lax.rsqrt(_head_sums(xv * xv) * (1.0 / SB_HEAD_DIM) + NORM_EPS)
            xhat = xv * r
            u = dy * g_ref[pl.ds(kind, 1), :]
            o_ref[...] = (r * (u - xhat * _head_sums(u * xhat) * (1.0 / SB_HEAD_DIM))).astype(BF16)
            dg_ref[pl.ds(kind, 1), :] += jnp.sum(dy * xhat, axis=0, keepdims=True)

    def grad_blk(kind):
        def index(s, i):
            mine = (s >= kind * per) & (s < (kind + 1) * per)
            return jnp.where(mine, s - kind * per, 0), jnp.where(mine, i, 0), 0
        return pl.BlockSpec((None, tm, LANES), index)

    blk = pl.BlockSpec((None, tm, LANES), lambda s, i: (s, i, 0))
    vec = pl.BlockSpec((SUBLANES, LANES), lambda s, i: (0, 0))
    return pl.pallas_call(
        body, name=name, grid=(ns, t // tm),
        in_specs=[grad_blk(0), grad_blk(1), grad_blk(2), blk, vec], out_specs=[blk, vec],
        out_shape=[jax.ShapeDtypeStruct((ns, t, LANES), BF16), jax.ShapeDtypeStruct((SUBLANES, LANES), F32)],
        compiler_params=_params("arbitrary", "arbitrary"),
    )(dq, dk, dv, qkv, gains)


def _split_dot(v, ones_mat, pieces, left=False):
    total, rest = None, v
    for p in range(pieces):
        part = rest.astype(BF16)
        if p + 1 < pieces:
            rest = rest - part.astype(F32)
        d = (jnp.dot(ones_mat, part, preferred_element_type=F32) if left
             else jnp.dot(part, ones_mat, preferred_element_type=F32))
        total = d if total is None else total + d
    return total


LOGIT_SUM_PIECES = 2
GRAD_SUM_PIECES = 1
LOG_WEIGHT_UNDERFLOW = -105.0


def _sb_attn_fwd(qkv_n, n_heads, *, name):
    ns, t, _ = qkv_n.shape
    per = ns // 3
    bq, blk, hd = SB_QUERY_BLOCK, SB_BLOCK, SB_HEAD_DIM
    nq, n_diag = t // bq, bq // blk

    def body(q_ref, k_ref, v_ref, o_ref, walk_ref):
        i = pl.program_id(1)
        row = lax.broadcasted_iota(jnp.int32, (blk, blk), 0)
        col = lax.broadcasted_iota(jnp.int32, (blk, blk), 1)
        later_keys = (row > col).astype(BF16)
        qry = lax.broadcasted_iota(jnp.int32, (bq, blk), 0)
        key = lax.broadcasted_iota(jnp.int32, (bq, blk), 1)

        def tile(kb, carry, key_offset):
            out = []
            start = pl.multiple_of(kb * blk, blk)
            for hf in range(2):
                lanes = slice(hf * hd, (hf + 1) * hd)
                run, acc = carry[hf]
                z = lax.dot_general(q_ref[:, lanes], k_ref[pl.ds(start, blk), lanes], NT_DIMS,
                                    preferred_element_type=F32)
                sp = _softplus(z)
                lm = -sp if key_offset is None else jnp.where(key + key_offset < qry, -sp, 0.0)
                after = _split_dot(lm, later_keys, LOGIT_SUM_PIECES) + run
                a = jnp.exp(z - sp + after)
                if key_offset is not None:
                    a = jnp.where(key + key_offset < qry, a, 0.0)
                acc = acc + jnp.dot(a.astype(BF16), v_ref[pl.ds(start, blk), lanes], preferred_element_type=F32)
                out.append((run + jnp.sum(lm, axis=1, keepdims=True), acc))
            return tuple(out)

        def live(carry):
            return jnp.max(jnp.maximum(carry[0][0], carry[1][0])) > LOG_WEIGHT_UNDERFLOW

        def step(state):
            s, _, carry = state
            carry = tile(n_diag * i - 1 - s, carry, None)
            return s + 1, live(carry), carry

        carry = tuple((jnp.zeros((bq, 1), F32), jnp.zeros((bq, hd), F32)) for _ in range(2))
        for j in reversed(range(n_diag)):
            carry = tile(n_diag * i + j, carry, j * blk)
        walked, _, carry = lax.while_loop(lambda st: (st[0] < n_diag * i) & st[1], step,
                                          (jnp.int32(0), live(carry), carry))
        o_ref[...] = jnp.concatenate([carry[0][1], carry[1][1]], axis=1)
        lane = lax.broadcasted_iota(jnp.int32, (1, LANES), 1)
        walk_ref[...] = jnp.where(lane < WALK_LANES, carry[0][0],
                                  jnp.where(lane < 2 * WALK_LANES, carry[1][0], walked.astype(F32)))

    q_blk = pl.BlockSpec((None, bq, LANES), lambda p, i: (p, i, 0))
    return pl.pallas_call(
        body, name=name, grid=(per, nq),
        in_specs=[q_blk, pl.BlockSpec((None, t, LANES), lambda p, i: (per + p, 0, 0)),
                  pl.BlockSpec((None, t, LANES), lambda p, i: (2 * per + p, 0, 0))],
        out_specs=[pl.BlockSpec((bq, LANES), lambda p, i: (i, p)), q_blk],
        out_shape=[jax.ShapeDtypeStruct((t, n_heads * hd), F32), jax.ShapeDtypeStruct((per, t, LANES), F32)],
        compiler_params=_params("parallel", "arbitrary"),
    )(qkv_n, qkv_n, qkv_n)


WALK_LANES = 43


def _sb_attn_bwd(do, walk, qkv_n, *, name):
    ns, t, _ = qkv_n.shape
    per = ns // 3
    bq, blk, hd = SB_QUERY_BLOCK, SB_BLOCK, SB_HEAD_DIM
    nq, n_diag = t // bq, bq // blk
    nt_dims = (((1,), (1,)), ((), ()))
    tn_dims = (((0,), (0,)), ((), ()))

    def body(q_ref, k_ref, v_ref, do_ref, walk_ref, dq_ref, dk_ref, dv_ref):
        i = pl.program_id(1)

        @pl.when(i == 0)
        def _():
            dk_ref[...] = jnp.zeros_like(dk_ref)
            dv_ref[...] = jnp.zeros_like(dv_ref)

        walk_t = walk_ref[...].T
        tots = [walk_t[hf * WALK_LANES:hf * WALK_LANES + 1, :] for hf in range(2)]
        reached = jnp.clip(jnp.max(walk_t[2 * WALK_LANES:2 * WALK_LANES + 1, :]).astype(jnp.int32), 0, n_diag * i)

        row = lax.broadcasted_iota(jnp.int32, (blk, blk), 0)
        col = lax.broadcasted_iota(jnp.int32, (blk, blk), 1)
        later_keys = (col > row).astype(BF16)
        earlier_keys = (col < row).astype(BF16)
        key = lax.broadcasted_iota(jnp.int32, (blk, bq), 0)
        qry = lax.broadcasted_iota(jnp.int32, (blk, bq), 1)
        halves = [slice(hf * hd, (hf + 1) * hd) for hf in range(2)]
        q_hs = [q_ref[:, lanes] for lanes in halves]
        do_bs = [do_ref[:, lanes].astype(BF16) for lanes in halves]

        def scores(kb, hf, key_offset):
            k_blk = k_ref[pl.ds(pl.multiple_of(kb * blk, blk), blk), halves[hf]]
            z = lax.dot_general(k_blk, q_hs[hf], nt_dims, preferred_element_type=F32)
            sp = _softplus(z)
            return k_blk, z, sp, (-sp if key_offset is None else jnp.where(key + key_offset < qry, -sp, 0.0))

        def tile(kb, carry, key_offset):
            out = []
            start = pl.multiple_of(kb * blk, blk)
            for hf, lanes in enumerate(halves):
                seen, gsum, dq = carry[hf]
                q_h, do_b = q_hs[hf], do_bs[hf]
                k_blk, z, sp, lm = scores(kb, hf, key_offset)
                blk_tot = jnp.sum(lm, axis=0, keepdims=True)
                after = _split_dot(lm, later_keys, LOGIT_SUM_PIECES, left=True) + (tots[hf] - seen - blk_tot)
                a = jnp.exp(z - sp + after)
                if key_offset is not None:
                    a = jnp.where(key + key_offset < qry, a, 0.0)
                da = lax.dot_general(v_ref[pl.ds(start, blk), lanes], do_b, nt_dims, preferred_element_type=F32)
                g = da * a
                before = _split_dot(g, earlier_keys, GRAD_SUM_PIECES, left=True) + gsum
                omb = jnp.exp(-sp)
                dz = g * omb - (1.0 - omb) * before
                if key_offset is not None:
                    dz = jnp.where(key + key_offset < qry, dz, 0.0)
                dz_b = dz.astype(BF16)
                dk_ref[pl.ds(start, blk), lanes] += jnp.dot(dz_b, q_h, preferred_element_type=F32)
                dv_ref[pl.ds(start, blk), lanes] += jnp.dot(a.astype(BF16), do_b, preferred_element_type=F32)
                dq = dq + lax.dot_general(dz_b, k_blk, tn_dims, preferred_element_type=F32)
                out.append((seen + blk_tot, gsum + jnp.sum(g, axis=0, keepdims=True), dq))
            return tuple(out)

        init = tuple((jnp.zeros((1, bq), F32), jnp.zeros((1, bq), F32), jnp.zeros((bq, hd), F32))
                     for _ in range(2))
        carry = lax.fori_loop(n_diag * i - reached, n_diag * i, lambda kb, c: tile(kb, c, None), init)
        for j in range(n_diag):
            carry = tile(n_diag * i + j, carry, j * blk)
        dq_ref[...] = jnp.concatenate([carry[0][2], carry[1][2]], axis=1)

    full = lambda off: pl.BlockSpec((None, t, LANES), lambda p, i: (off + p, 0, 0))
    q_blk = pl.BlockSpec((None, bq, LANES), lambda p, i: (p, i, 0))
    slab = jax.ShapeDtypeStruct((per, t, LANES), F32)
    return pl.pallas_call(
        body, name=name, grid=(per, nq),
        in_specs=[q_blk, full(per), full(2 * per), pl.BlockSpec((bq, LANES), lambda p, i: (i, p)), q_blk],
        out_specs=[q_blk, full(0), full(0)],
        out_shape=[slab, slab, slab],
        compiler_params=_params("parallel", "arbitrary"),
    )(qkv_n, qkv_n, qkv_n, do, walk)


def _sb_mixer_fwd(x, gain, w_qkv_t, qk_gains, w_out, tag):
    n_heads = w_out.shape[0] // SB_HEAD_DIM
    h = _rms_fwd(x, gain, name=f"sb_norm_{tag}")
    qkv = _matmul(h, w_qkv_t, "nt", out_slabs=True, tn_cap=512, name=f"sb_qkv_{tag}")
    qkv_n = _sb_qk_norm_fwd(qkv, qk_gains, name=f"sb_qknorm_{tag}")
    o, walk = _sb_attn_fwd(qkv_n, n_heads, name=f"sb_attn_{tag}")
    x_new = _matmul(o, w_out, "nn", resid=x, name=f"sb_out_{tag}")
    return x_new, (x, h, qkv, qkv_n, o, walk)


def _sb_mixer_bwd(dx, saved, gain, w_qkv_t, qk_gains, w_out, grads, row_qkv, row_out, tag):
    x, h, qkv, qkv_n, o, walk = saved
    dx, dx_b = dx
    do = _matmul(dx_b, w_out, "nt", name=f"sb_do_{tag}")
    grads = _matmul_tn_into(grads, o, dx_b, row_out, name=f"sb_dwout_{tag}")
    dq, dk, dv = _sb_attn_bwd(do, walk, qkv_n, name=f"sb_dattn_{tag}")
    dqkv, dqk_gains = _sb_qk_norm_bwd(dq, dk, dv, qkv, qk_gains, name=f"sb_dqknorm_{tag}")
    grads = _matmul_tn_into(grads, dqkv, h, row_qkv, a_slabs=True, name=f"sb_dwqkv_{tag}")
    dh = _matmul(dqkv, w_qkv_t, "nn", a_slabs=True, name=f"sb_dh_{tag}")
    dx_in, dgain = _rms_bwd(x, gain, dh, dx, name=f"sb_dnorm_{tag}")
    return dx_in, grads, dqk_gains, dgain


MESH = pl.DeviceIdType.MESH


def _position():
    return lax.axis_index("x"), lax.axis_index("y"), lax.axis_index("c")


def _all_gather(shard, *, name):
    rows, n = shard.shape
    space = pltpu.VMEM

    def body(x_ref, out_ref, send_sems, recv_sems, local_sem):
        x, y, c = _position()
        me, sibling = (x, y, c), (x, y, 1 - c)
        chips = [(1 - x, y), (x, 1 - y), (1 - x, 1 - y)]

        def block(px, py, pc):
            return out_ref.at[4 * px + 2 * py + pc]

        def copy(k, blk, to, src=None):
            return pltpu.make_async_remote_copy(
                src_ref=block(*blk) if src is None else src, dst_ref=block(*blk),
                send_sem=send_sems.at[k], recv_sem=recv_sems.at[k], device_id=to, device_id_type=MESH)

        mine = pltpu.make_async_copy(x_ref, block(*me), local_sem)
        mine.start()
        first = [copy(0, me, sibling, src=x_ref)]
        first += [copy(1 + j, me, (*chip, c), src=x_ref) for j, chip in enumerate(chips)]
        for cp in first:
            cp.start()
        passed = [copy(4 + j, (*chip, c), sibling) for j, chip in enumerate(chips)]
        for j, chip in enumerate(chips):
            copy(1 + j, (*chip, c), me).wait_recv()
            passed[j].start()
        copy(0, sibling, me).wait_recv()
        for j, chip in enumerate(chips):
            copy(4 + j, (*chip, 1 - c), me).wait_recv()
        for cp in first + passed:
            cp.wait_send()
        mine.wait()

    return pl.pallas_call(
        body, name=name,
        out_shape=jax.ShapeDtypeStruct((N_DEV, rows, n), shard.dtype),
        in_specs=[pl.BlockSpec(memory_space=space)], out_specs=pl.BlockSpec(memory_space=space),
        scratch_shapes=[pltpu.SemaphoreType.DMA((7,)), pltpu.SemaphoreType.DMA((7,)), pltpu.SemaphoreType.DMA],
        compiler_params=pltpu.CompilerParams(vmem_limit_bytes=V7X_VMEM_LIMIT_BYTES),
    )(shard)


def _all_gather_forwarding(shard, *, name):
    rows, n = shard.shape
    half = rows // 2
    assert rows % (4 * SUBLANES) == 0

    def body(x_ref, out_ref, send_sems, recv_sems, local_sem):
        x, y, c = _position()
        me, sibling = (x, y, c), (x, y, 1 - c)
        x_nbr, y_nbr, diag = (1 - x, y), (x, 1 - y), (1 - x, 1 - y)
        lower, upper = pl.ds(0, half), pl.ds(half, half)

        def block(px, py, pc, part=None):
            ref = out_ref.at[4 * px + 2 * py + pc]
            return ref if part is None else ref.at[part]

        def copy(k, blk, to, src=None, part=None):
            return pltpu.make_async_remote_copy(
                src_ref=block(*blk, part) if src is None else src, dst_ref=block(*blk, part),
                send_sem=send_sems.at[k], recv_sem=recv_sems.at[k], device_id=to, device_id_type=MESH)

        mine = pltpu.make_async_copy(x_ref, block(*me), local_sem)
        mine.start()
        sent = [copy(0, me, sibling, src=x_ref), copy(1, me, (*x_nbr, c), src=x_ref),
                copy(2, me, (*y_nbr, c), src=x_ref)]
        for cp in sent:
            cp.start()
        copy(1, (*x_nbr, c), me).wait_recv()
        onward = [copy(3, (*x_nbr, c), (*y_nbr, c), part=lower), copy(5, (*x_nbr, c), sibling)]
        for cp in onward:
            cp.start()
        copy(2, (*y_nbr, c), me).wait_recv()
        onward += [copy(4, (*y_nbr, c), (*x_nbr, c), part=upper), copy(6, (*y_nbr, c), sibling)]
        for cp in onward[2:]:
            cp.start()
        copy(3, (*diag, c), me, part=lower).wait_recv()
        copy(4, (*diag, c), me, part=upper).wait_recv()
        onward.append(copy(7, (*diag, c), sibling))
        onward[-1].start()
        sent += onward
        copy(0, sibling, me).wait_recv()
        for k, chip in ((5, x_nbr), (6, y_nbr), (7, diag)):
            copy(k, (*chip, 1 - c), me).wait_recv()
        for cp in sent:
            cp.wait_send()
        mine.wait()

    hbm = pl.BlockSpec(memory_space=pltpu.HBM)
    return pl.pallas_call(
        body, name=name,
        out_shape=jax.ShapeDtypeStruct((N_DEV, rows, n), shard.dtype), in_specs=[hbm], out_specs=hbm,
        scratch_shapes=[pltpu.SemaphoreType.DMA((8,)), pltpu.SemaphoreType.DMA((8,)), pltpu.SemaphoreType.DMA],
    )(shard)


def _exchange_sibling(parts, *, name):
    _, nchip, rows, n = parts.shape

    def body(p_ref, recv_ref, send_sem, recv_sem):
        x, y, c = _position()
        cp = pltpu.make_async_remote_copy(src_ref=p_ref.at[1 - c], dst_ref=recv_ref, send_sem=send_sem,
                                          recv_sem=recv_sem, device_id=(x, y, 1 - c), device_id_type=MESH)
        cp.start()
        cp.wait()

    return pl.pallas_call(
        body, name=name,
        out_shape=jax.ShapeDtypeStruct((nchip, rows, n), parts.dtype),
        in_specs=[pl.BlockSpec(memory_space=pltpu.HBM)], out_specs=pl.BlockSpec(memory_space=pltpu.HBM),
        scratch_shapes=[pltpu.SemaphoreType.DMA, pltpu.SemaphoreType.DMA],
    )(parts)


def _exchange_chips(chip_sums, *, name):
    _, rows, n = chip_sums.shape

    def body(s_ref, recv_ref, send_sems, recv_sems):
        x, y, c = _position()
        chips = [(1 - x, y), (x, 1 - y), (1 - x, 1 - y)]
        copies = [pltpu.make_async_remote_copy(
            src_ref=s_ref.at[2 * cx + cy], dst_ref=recv_ref.at[j], send_sem=send_sems.at[j],
            recv_sem=recv_sems.at[j], device_id=(cx, cy, c), device_id_type=MESH)
            for j, (cx, cy) in enumerate(chips)]
        for cp in copies:
            cp.start()
        for cp in copies:
            cp.wait()

    return pl.pallas_call(
        body, name=name,
        out_shape=jax.ShapeDtypeStruct((3, rows, n), chip_sums.dtype),
        in_specs=[pl.BlockSpec(memory_space=pltpu.HBM)], out_specs=pl.BlockSpec(memory_space=pltpu.HBM),
        scratch_shapes=[pltpu.SemaphoreType.DMA((3,)), pltpu.SemaphoreType.DMA((3,))],
    )(chip_sums)


def _add_pairs(parts, recv, c_and_chip, *, name):
    _, nchip, rows, n = parts.shape
    tr = _tile(rows, 512, SUBLANES)

    def body(pos_ref, a_ref, b_ref, own_ref, wire_ref):
        s = a_ref[...] + b_ref[...]
        wire_ref[...] = s.astype(WIRE_DTYPE)

        @pl.when(pl.program_id(1) == pos_ref[1])
        def _():
            own_ref[...] = s

    return pl.pallas_call(
        body, name=name,
        grid_spec=pltpu.PrefetchScalarGridSpec(
            num_scalar_prefetch=1, grid=(rows // tr, nchip),
            in_specs=[pl.BlockSpec((None, None, tr, n), lambda i, k, pos: (pos[0], k, i, 0)),
                      pl.BlockSpec((None, tr, n), lambda i, k, pos: (k, i, 0))],
            out_specs=[pl.BlockSpec((tr, n), lambda i, k, pos: (i, 0)),
                       pl.BlockSpec((None, tr, n), lambda i, k, pos: (k, i, 0))]),
        out_shape=[jax.ShapeDtypeStruct((rows, n), parts.dtype),
                   jax.ShapeDtypeStruct((nchip, rows, n), WIRE_DTYPE)],
        compiler_params=_params("parallel", "arbitrary"),
    )(c_and_chip, parts, recv)


def _adamw_math(w, g, m, v):
    m = ADAM_B1 * m + (1.0 - ADAM_B1) * g
    v = ADAM_B2 * v + (1.0 - ADAM_B2) * (g * g)
    m_hat = m / (1.0 - ADAM_B1 ** ADAM_STEP)
    v_hat = v / (1.0 - ADAM_B2 ** ADAM_STEP)
    delta = -ADAM_LR * (m_hat / (jnp.sqrt(v_hat) + ADAM_EPS) + ADAM_WD * w)
    return delta, m, v


def _adamw_sharded(own_sum, recv, w, m, v, *, name):
    rows, n = w.shape
    tr = _tile(rows, 256, SUBLANES)

    def body(s_ref, r_ref, w_ref, m_ref, v_ref, g_out, d_out, m_out, v_out):
        g = ((s_ref[...] + r_ref[0].astype(F32)) + r_ref[1].astype(F32)) + r_ref[2].astype(F32)
        delta, m_new, v_new = _adamw_math(w_ref[...], g, m_ref[...], v_ref[...])
        g_out[...] = g
        d_out[...] = delta
        m_out[...] = m_new
        v_out[...] = v_new

    blk = pl.BlockSpec((tr, n), lambda i: (i, 0))
    out = jax.ShapeDtypeStruct((rows, n), F32)
    return pl.pallas_call(
        body, name=name, grid=(rows // tr,),
        in_specs=[blk, pl.BlockSpec((3, tr, n), lambda i: (0, i, 0)), blk, blk, blk],
        out_specs=[blk, blk, blk, blk], out_shape=[out, out, out, out],
        compiler_params=_params("parallel"),
    )(own_sum, recv, w, m, v)


SMALL_ROWS = 40
ROW_MIX_NORM, ROW_FFN_NORM, ROW_CONV_B, ROW_OUT_NORM, ROW_POOL_SCALE, ROW_CONV_W = 0, 4, 8, 12, 14, 16
ROW_SSD_VEC, ROW_QK_GAIN, ROW_LOSS = 32, 33, 34


def _adamw_small(gathered, w, m, v, *, name):
    _, rows, n = gathered.shape

    def body(a_ref, w_ref, m_ref, v_ref, g_out, d_out, m_out, v_out):
        g = a_ref[0]
        for d in range(1, N_DEV):
            g = g + a_ref[d]
        row = lax.broadcasted_iota(jnp.int32, (rows, 1), 0)
        g = jnp.where(row == ROW_QK_GAIN, g + pltpu.roll(g, SB_HEAD_DIM, 1), g)
        g = jnp.where(row == ROW_LOSS, jnp.sum(g, axis=1, keepdims=True), g)
        g_out[...] = g
        delta, m_new, v_new = _adamw_math(w_ref[...], g, m_ref[...], v_ref[...])
        d_out[...] = delta
        m_out[...] = m_new
        v_out[...] = v_new

    out = jax.ShapeDtypeStruct((rows, n), F32)
    return pl.pallas_call(body, name=name, out_shape=[out, out, out, out])(gathered, w, m, v)


BIG_WEIGHTS = ("ffn_gate", "ffn_up", "ffn_down", "sb_qkv", "ssd_out", "pool_in", "sb_out", "pool_group", "ssd_in")
COLUMN_SHARDED = ("ssd_in", "sb_qkv", "ffn_gate", "ffn_up")
ROW_PAD = 512
WIRE_DTYPE = jnp.bfloat16


def _to_rows(name, shard, d):
    if name in COLUMN_SHARDED:
        shard = jnp.swapaxes(shard, -1, -2)
    return shard.reshape(-1, d)


def _from_rows(name, rows, shard_shape):
    if name in COLUMN_SHARDED:
        lead, k, n = shard_shape
        return jnp.swapaxes(rows.reshape(lead, n, k), -1, -2)
    return rows.reshape(shard_shape)


def _pad_rows(a, total):
    return jnp.pad(a, ((0, total - a.shape[0]),) + ((0, 0),) * (a.ndim - 1))


def _exact_bf16_rows(v, d):
    words = lax.bitcast_convert_type(v.reshape(-1), WIRE_DTYPE).reshape(-1)
    return _pad_rows(words, -(-words.shape[0] // d) * d).reshape(-1, d)


def _exact_f32(rows, count):
    words = rows.reshape(rows.shape[0], -1)[:, :2 * count].reshape(rows.shape[0], count, 2)
    return lax.bitcast_convert_type(words, F32)


def _device_blocks(full, d):
    return full.reshape(N_DEV, -1, d)


def kernel(x, mix_norm, pool_in, pool_group, pool_scale, ssd_in, ssd_conv_w, ssd_conv_b, ssd_dt_bias, ssd_a_log, ssd_d, ssd_out_norm, ssd_out, sb_qkv, sb_q_norm, sb_k_norm, sb_out, ffn_norm, ffn_gate, ffn_up, ffn_down, loss_target, m_mix_norm, m_pool_in, m_pool_group, m_pool_scale, m_ssd_in, m_ssd_conv_w, m_ssd_conv_b, m_ssd_dt_bias, m_ssd_a_log, m_ssd_d, m_ssd_out_norm, m_ssd_out, m_sb_qkv, m_sb_q_norm, m_sb_k_norm, m_sb_out, m_ffn_norm, m_ffn_gate, m_ffn_up, m_ffn_down, v_mix_norm, v_pool_in, v_pool_group, v_pool_scale, v_ssd_in, v_ssd_conv_w, v_ssd_conv_b, v_ssd_dt_bias, v_ssd_a_log, v_ssd_d, v_ssd_out_norm, v_ssd_out, v_sb_qkv, v_sb_q_norm, v_sb_k_norm, v_sb_out, v_ffn_norm, v_ffn_gate, v_ffn_up, v_ffn_down):
    weights = dict(mix_norm=mix_norm, pool_in=pool_in, pool_group=pool_group, pool_scale=pool_scale, ssd_in=ssd_in,
                   ssd_conv_w=ssd_conv_w, ssd_conv_b=ssd_conv_b, ssd_dt_bias=ssd_dt_bias, ssd_a_log=ssd_a_log,
                   ssd_d=ssd_d, ssd_out_norm=ssd_out_norm, ssd_out=ssd_out, sb_qkv=sb_qkv, sb_q_norm=sb_q_norm,
                   sb_k_norm=sb_k_norm, sb_out=sb_out, ffn_norm=ffn_norm, ffn_gate=ffn_gate, ffn_up=ffn_up,
                   ffn_down=ffn_down)
    mom1 = dict(mix_norm=m_mix_norm, pool_in=m_pool_in, pool_group=m_pool_group, pool_scale=m_pool_scale,
                ssd_in=m_ssd_in, ssd_conv_w=m_ssd_conv_w, ssd_conv_b=m_ssd_conv_b, ssd_dt_bias=m_ssd_dt_bias,
                ssd_a_log=m_ssd_a_log, ssd_d=m_ssd_d, ssd_out_norm=m_ssd_out_norm, ssd_out=m_ssd_out,
                sb_qkv=m_sb_qkv, sb_q_norm=m_sb_q_norm, sb_k_norm=m_sb_k_norm, sb_out=m_sb_out,
                ffn_norm=m_ffn_norm, ffn_gate=m_ffn_gate, ffn_up=m_ffn_up, ffn_down=m_ffn_down)
    mom2 = dict(mix_norm=v_mix_norm, pool_in=v_pool_in, pool_group=v_pool_group, pool_scale=v_pool_scale,
                ssd_in=v_ssd_in, ssd_conv_w=v_ssd_conv_w, ssd_conv_b=v_ssd_conv_b, ssd_dt_bias=v_ssd_dt_bias,
                ssd_a_log=v_ssd_a_log, ssd_d=v_ssd_d, ssd_out_norm=v_ssd_out_norm, ssd_out=v_ssd_out,
                sb_qkv=v_sb_qkv, sb_q_norm=v_sb_q_norm, sb_k_norm=v_sb_k_norm, sb_out=v_sb_out,
                ffn_norm=v_ffn_norm, ffn_gate=v_ffn_gate, ffn_up=v_ffn_up, ffn_down=v_ffn_down)
    names = list(weights)
    depth, d = mix_norm.shape
    xs, ys, cs = _position()
    dev = 4 * xs + 2 * ys + cs
    chip = 2 * xs + ys

    seg = {}
    row = 0
    for name in BIG_WEIGHTS:
        n_rows = weights[name].size // d
        seg[name] = (row, n_rows)
        row += -(-n_rows // SUBLANES) * SUBLANES
    big_rows = row
    n_scale, n_convw = pool_scale.size, ssd_conv_w.size
    exact = jnp.concatenate([_exact_bf16_rows(pool_scale, d), _exact_bf16_rows(ssd_conv_w, d)], axis=0)
    scale_rows = _exact_bf16_rows(pool_scale, d).shape[0]
    packed_rows = -(-(big_rows + exact.shape[0]) // ROW_PAD) * ROW_PAD

    def pack(tree, dtype):
        ends = [seg[n][0] for n in BIG_WEIGHTS[1:]] + [big_rows]
        return jnp.concatenate([_pad_rows(_to_rows(n, tree[n], d).astype(dtype), end - seg[n][0])
                                for n, end in zip(BIG_WEIGHTS, ends)], axis=0)

    w_wire = _pad_rows(jnp.concatenate([pack(weights, WIRE_DTYPE), exact], axis=0), packed_rows)
    gathered = _all_gather_forwarding(w_wire, name="gather_weights")

    def seg_of(name):
        a, n = seg[name]
        return gathered[:, a:a + n]

    n_pool, n_ssd, n_sb = pool_in.shape[0], ssd_in.shape[0], sb_qkv.shape[0]
    assert n_ssd == 1 and n_sb == 1
    w_pool_in = seg_of("pool_in").reshape(N_DEV, n_pool, -1, d).transpose(1, 0, 2, 3).reshape(n_pool, d, d)
    grp = pool_group.shape
    w_pool_group = seg_of("pool_group").reshape(N_DEV, grp[0], grp[1], grp[2], grp[3]).transpose(1, 2, 0, 3, 4)
    w_pool_group = w_pool_group.reshape(grp[0], grp[1], grp[3], grp[3])
    w_ssd_in_t = seg_of("ssd_in").reshape(-1, d)
    w_ssd_out = seg_of("ssd_out").reshape(-1, d)
    w_sb_qkv_t = seg_of("sb_qkv").reshape(-1, d)
    w_sb_out = seg_of("sb_out").reshape(-1, d)
    hidden = ffn_down.shape[1] * N_DEV
    w_gate_t = seg_of("ffn_gate").reshape(N_DEV, depth, -1, d).transpose(1, 0, 2, 3).reshape(depth, hidden, d)
    w_up_t = seg_of("ffn_up").reshape(N_DEV, depth, -1, d).transpose(1, 0, 2, 3).reshape(depth, hidden, d)
    w_down = seg_of("ffn_down").reshape(N_DEV, depth, -1, d).transpose(1, 0, 2, 3).reshape(depth, hidden, d)
    exact_all = gathered[:, big_rows:big_rows + exact.shape[0]]
    scale_full = _exact_f32(exact_all[:, :scale_rows], n_scale).reshape(N_DEV, n_pool, -1)
    scale_full = scale_full.transpose(1, 0, 2).reshape(n_pool, d)
    convw_full = _exact_f32(exact_all[:, scale_rows:], n_convw).reshape(N_DEV, SSD_CONV, -1)
    convw_full = convw_full.transpose(1, 0, 2).reshape(SSD_CONV, -1)

    d_inner = w_ssd_out.shape[0]
    n_zx = w_ssd_in_t.shape[0] - ssd_dt_bias.shape[1]
    w_zx_t = w_ssd_in_t[:n_zx]
    n_ssd_heads = ssd_dt_bias.shape[1]
    n_ssd_groups = n_ssd_heads // SSD_HEADS_PER_GROUP
    w_dt_t = _ssd_group_pad(w_ssd_in_t[n_zx:].T, n_ssd_groups).T
    par = _pad_rows(_ssd_group_pad(jnp.concatenate([ssd_dt_bias, ssd_a_log, ssd_d], axis=0), n_ssd_groups), SUBLANES)
    d_rep = jnp.repeat(ssd_d[0], SSD_HEAD_DIM)[None]
    qk_gains = jnp.zeros((SUBLANES, LANES), F32).at[0].set(jnp.tile(sb_q_norm[0], 2)).at[1].set(jnp.tile(sb_k_norm[0], 2))

    act = x[0]
    saved = []
    for i in range(depth):
        kind, j = i % 3, i // 3
        gain = mix_norm[i:i + 1]
        if kind == 0:
            act, s = _pool_mixer_fwd(act, gain, w_pool_in[j], w_pool_group[j], scale_full[j:j + 1], f"l{i}")
        elif kind == 1:
            act, s = _ssd_mixer_fwd(act, gain, w_zx_t, w_dt_t, convw_full, ssd_conv_b, par, d_rep, ssd_out_norm,
                                    w_ssd_out, f"l{i}")
        else:
            act, s = _sb_mixer_fwd(act, gain, w_sb_qkv_t, qk_gains, w_sb_out, f"l{i}")
        act, f = _ffn_fwd(act, ffn_norm[i:i + 1], w_gate_t[i], w_up_t[i], w_down[i], f"l{i}")
        saved.append((s, f))
    dact, loss_cols = _loss_head(act, loss_target[0], name="loss_head")

    def layer_row(name, layer):
        return seg[name][0] + layer * (seg[name][1] // weights[name].shape[0])

    grads = jax.ShapeDtypeStruct((N_DEV, packed_rows, d), F32)
    g_mix_norm, g_ffn_norm = [None] * depth, [None] * depth
    g_pool_group, g_pool_scale = [None] * n_pool, [None] * n_pool
    for i in reversed(range(depth)):
        kind, j = i % 3, i // 3
        gain = mix_norm[i:i + 1]
        s, f = saved[i]
        dact, grads, g_ffn_norm[i] = _ffn_bwd(
            dact, f, ffn_norm[i:i + 1], w_gate_t[i], w_up_t[i], w_down[i], grads,
            [layer_row(n, i) for n in ("ffn_gate", "ffn_up", "ffn_down")], f"l{i}")
        if kind == 0:
            dact, grads, g_pool_group[j], g_pool_scale[j], g_mix_norm[i] = _pool_mixer_bwd(
                dact, s, gain, w_pool_in[j], w_pool_group[j], scale_full[j:j + 1], grads, layer_row("pool_in", j),
                f"l{i}")
        elif kind == 1:
            (dact, grads, g_zx_t, g_dt_t, g_conv_w, g_conv_b, g_par, g_out_norm,
             g_mix_norm[i]) = _ssd_mixer_bwd(dact, s, gain, w_zx_t, w_dt_t, convw_full, ssd_conv_b, par, d_rep,
                                             ssd_out_norm, w_ssd_out, grads, layer_row("ssd_out", j), f"l{i}")
        else:
            dact, grads, g_qk_gains, g_mix_norm[i] = _sb_mixer_bwd(
                dact, s, gain, w_sb_qkv_t, qk_gains, w_sb_out, grads, layer_row("sb_qkv", j),
                layer_row("sb_out", j), f"l{i}")
    grad_x = dact[0][None]

    g_ssd_in = jnp.concatenate([g_zx_t, _ssd_group_unpad(g_dt_t.T, n_ssd_groups).T], axis=0)
    g_group = jnp.concatenate([_device_blocks(gg[k], d) for gg in g_pool_group for k in range(gg.shape[0])], axis=1)
    for name, blocks in (("ssd_in", _device_blocks(g_ssd_in, d)), ("pool_group", g_group)):
        blocks = blocks.reshape(N_DEV // 2, 2, -1, d).swapaxes(0, 1).reshape(N_DEV, -1, d)
        blocks = jnp.pad(blocks, ((0, 0), (0, -blocks.shape[1] % SUBLANES), (0, 0)))
        grads = lax.dynamic_update_slice(grads, blocks, (0, seg[name][0], 0))
    grads = lax.dynamic_update_slice(grads, jnp.zeros((N_DEV, packed_rows - big_rows, d), F32), (0, big_rows, 0))
    parts = grads.reshape(2, N_DEV // 2, packed_rows, d)
    from_sibling = _exchange_sibling(parts, name="reduce_sibling")
    own_sum, chip_sums_wire = _add_pairs(parts, from_sibling, jnp.stack([cs, chip]).astype(jnp.int32),
                                         name="reduce_sibling_add")
    from_chips = _exchange_chips(chip_sums_wire, name="reduce_chips")

    def pack_f32(tree):
        return _pad_rows(pack(tree, F32), packed_rows)

    big_out = _adamw_sharded(own_sum, from_chips, pack_f32(weights), pack_f32(mom1), pack_f32(mom2),
                             name="adamw_sharded")

    def small_pack(mix, ffn, conv_b, out_norm, scale, conv_w, vec, qk, loss=None):
        buf = jnp.zeros((SMALL_ROWS, d), F32)
        buf = buf.at[ROW_MIX_NORM:ROW_MIX_NORM + depth].set(mix).at[ROW_FFN_NORM:ROW_FFN_NORM + depth].set(ffn)
        buf = buf.at[ROW_CONV_B:ROW_CONV_B + conv_b.size // d].set(conv_b.reshape(-1, d))
        buf = buf.at[ROW_OUT_NORM:ROW_OUT_NORM + out_norm.size // d].set(out_norm.reshape(-1, d))
        buf = buf.at[ROW_POOL_SCALE:ROW_POOL_SCALE + n_pool].set(scale)
        buf = buf.at[ROW_CONV_W:ROW_CONV_W + conv_w.size // d].set(conv_w.reshape(-1, d))
        buf = buf.at[ROW_SSD_VEC].set(vec.reshape(-1)).at[ROW_QK_GAIN].set(qk.reshape(-1))
        if loss is not None:
            buf = buf.at[ROW_LOSS].set(loss.reshape(-1))
        return buf

    def small_params(tree):
        scale = lax.dynamic_update_slice(jnp.zeros((n_pool, d), F32), tree["pool_scale"],
                                         (0, dev * tree["pool_scale"].shape[1]))
        conv_w = lax.dynamic_update_slice(jnp.zeros(convw_full.shape, F32), tree["ssd_conv_w"][0],
                                          (0, dev * tree["ssd_conv_w"].shape[2]))
        vec = jnp.zeros((SUBLANES, LANES), F32)
        vec = vec.at[0, :n_ssd_heads].set(tree["ssd_dt_bias"][0]).at[1, :n_ssd_heads].set(tree["ssd_a_log"][0])
        vec = vec.at[2, :n_ssd_heads].set(tree["ssd_d"][0])
        qk = jnp.zeros((SUBLANES, LANES), F32)
        qk = qk.at[0, SB_HEAD_DIM:].set(tree["sb_q_norm"][0]).at[1, SB_HEAD_DIM:].set(tree["sb_k_norm"][0])
        return small_pack(tree["mix_norm"], tree["ffn_norm"], tree["ssd_conv_b"], tree["ssd_out_norm"], scale,
                          conv_w, vec, qk)

    small_partial = small_pack(jnp.concatenate(g_mix_norm, axis=0), jnp.concatenate(g_ffn_norm, axis=0), g_conv_b,
                               g_out_norm, jnp.concatenate(g_pool_scale, axis=0), g_conv_w,
                               jnp.zeros((SUBLANES, LANES), F32).at[:3, :n_ssd_heads].set(
                                   _ssd_group_unpad(g_par[:3], n_ssd_groups)), g_qk_gains,
                               loss_cols)
    small_all = _all_gather(small_partial, name="gather_small")
    small_out = _adamw_small(small_all, small_params(weights), small_params(mom1), small_params(mom2),
                             name="adamw_small")
    loss = small_out[0][ROW_LOSS, 0]

    def unpack(big, small):
        out = {}
        for name in BIG_WEIGHTS:
            a, n = seg[name]
            out[name] = _from_rows(name, big[a:a + n], weights[name].shape)
        out["mix_norm"] = small[ROW_MIX_NORM:ROW_MIX_NORM + depth]
        out["ffn_norm"] = small[ROW_FFN_NORM:ROW_FFN_NORM + depth]
        out["ssd_conv_b"] = small[ROW_CONV_B:ROW_CONV_B + ssd_conv_b.size // d].reshape(ssd_conv_b.shape)
        out["ssd_out_norm"] = small[ROW_OUT_NORM:ROW_OUT_NORM + ssd_out_norm.size // d].reshape(ssd_out_norm.shape)
        out["pool_scale"] = lax.dynamic_slice(small[ROW_POOL_SCALE:ROW_POOL_SCALE + n_pool],
                                              (0, dev * pool_scale.shape[1]), pool_scale.shape)
        conv_w = small[ROW_CONV_W:ROW_CONV_W + convw_full.size // d].reshape(convw_full.shape)
        out["ssd_conv_w"] = lax.dynamic_slice(conv_w, (0, dev * ssd_conv_w.shape[2]), ssd_conv_w.shape[1:])[None]
        vec = small[ROW_SSD_VEC].reshape(SUBLANES, LANES)
        out["ssd_dt_bias"], out["ssd_a_log"], out["ssd_d"] = (vec[r:r + 1, :n_ssd_heads] for r in range(3))
        qk = small[ROW_QK_GAIN].reshape(SUBLANES, LANES)
        out["sb_q_norm"], out["sb_k_norm"] = qk[0:1, SB_HEAD_DIM:], qk[1:2, SB_HEAD_DIM:]
        return [out[n] for n in names]

    results = [unpack(b, s) for b, s in zip(big_out, small_out)]
    return (loss, grad_x, *results[0], *results[1], *results[2], *results[3])
```
